```python
import jax, jax.numpy as jnp
from jax import lax
import numpy as np

D_MODEL = 1024
BATCH = 16
SEQ = 256
DEPTH = 1
DEC_BATCH = 2
DEC_SEQ = 1024
PAST_LEN = 256

GRID_W = 64
POOL_WIDTH = 512
POOL_GROUPS = 4
POOL_GROUP_DIM = POOL_WIDTH // POOL_GROUPS
POOL_WINDOWS = (2, 4, 8, 16)
MLSTM_HEADS = 4
MLSTM_HEAD_DIM = 128
MLSTM_WIDTH = MLSTM_HEADS * MLSTM_HEAD_DIM
MIX_WIDTH = POOL_WIDTH + MLSTM_WIDTH
N_DIR = 2
GATE_COLS = N_DIR * 2 * MLSTM_HEADS
IN_COLS = POOL_WIDTH + 4 * MLSTM_WIDTH + GATE_COLS
CHUNK = 64
N_EXPERTS = 64
TOP_K = 6
N_EXPERT_GROUPS = 8
TOPK_GROUPS = 4
EXPERT_DIM = 256
SHARED_DIM = 256
ROUTED_SCALE = 2.5
N_MOD = 6
EPS = 1e-6

kernel_name = "hybrid_pool_mlstm_moe_flow_step"


def rmsnorm(x, g):
    xf = x.astype(jnp.float32)
    y = xf * lax.rsqrt(jnp.mean(xf * xf, axis=-1, keepdims=True) + EPS)
    return y * g.astype(jnp.float32)


def box_mean(x, w, axis):
    n = x.shape[axis]
    xf = x.astype(jnp.float32)
    pad = [(0, 0)] * x.ndim
    pad[axis] = (1, 0)
    cs = jnp.pad(jnp.cumsum(xf, axis=axis), pad)
    idx = jnp.arange(n)
    lo = jnp.clip(idx - w // 2, 0, n)
    hi = jnp.clip(idx + w - w // 2, 0, n)
    s = jnp.take(cs, hi, axis=axis) - jnp.take(cs, lo, axis=axis)
    shape = [1] * x.ndim
    shape[axis] = n
    cnt = (hi - lo).astype(jnp.float32).reshape(shape)
    return s / cnt


def pool_mixer(u, w_pool, pool_scale, grid):
    B, T, _ = u.shape
    ug = u.astype(jnp.float32).reshape(B, T, POOL_GROUPS, POOL_GROUP_DIM)
    diffs = []
    for g, w in enumerate(POOL_WINDOWS):
        xg = ug[:, :, g, :]
        if grid:
            rows = T // GRID_W
            x2 = xg.reshape(B, rows, GRID_W, POOL_GROUP_DIM)
            mean = box_mean(box_mean(x2, w, 1), w, 2).reshape(B, T, POOL_GROUP_DIM)
        else:
            mean = box_mean(xg, w, 1)
        diffs.append(mean - xg)
    d = jnp.stack(diffs, axis=2)
    y = jnp.einsum("btgc,gce->btge", d, w_pool).reshape(B, T, POOL_WIDTH)
    return y * pool_scale


def mlstm_scan(q, k, v, log_i, log_f, C0, n0, m0):
    B, H, T, Dh = q.shape
    nc = T // CHUNK

    def chunks(a):
        return jnp.moveaxis(a.reshape(a.shape[:2] + (nc, CHUNK) + a.shape[3:]), 2, 0)

    causal = jnp.tril(jnp.ones((CHUNK, CHUNK), dtype=bool))

    def step(carry, xs):
        C, n, m = carry
        qc, kc, vc, ai, lf = xs
        b = jnp.cumsum(lf, axis=-1)
        d_intra = jnp.where(causal, b[..., :, None] - b[..., None, :] + ai[..., None, :], -jnp.inf)
        m_inter = b + m[..., None]
        m_t = jnp.maximum(m_inter, jnp.max(d_intra, axis=-1))
        w_inter = jnp.exp(m_inter - m_t)
        p = jnp.exp(d_intra - m_t[..., None]) * jnp.einsum("bhtd,bhsd->bhts", qc, kc)
        num = w_inter[..., None] * jnp.einsum("bhtd,bhde->bhte", qc, C) + jnp.einsum("bhts,bhse->bhte", p, vc)
        den = w_inter * jnp.einsum("bhtd,bhd->bht", qc, n) + jnp.sum(p, axis=-1)
        h = num / jnp.maximum(jnp.abs(den), jnp.exp(-m_t))[..., None]
        b_last = b[..., -1]
        g = b_last[..., None] - b + ai
        m_new = jnp.maximum(b_last + m, jnp.max(g, axis=-1))
        w_c = jnp.exp(b_last + m - m_new)
        kw = jnp.exp(g - m_new[..., None])[..., None] * kc
        C_new = w_c[..., None, None] * C + jnp.einsum("bhsd,bhse->bhde", kw, vc)
        n_new = w_c[..., None] * n + jnp.sum(kw, axis=2)
        return (C_new, n_new, m_new), h

    carry0 = (C0.astype(jnp.float32), n0.astype(jnp.float32), m0.astype(jnp.float32))
    (C, n, m), h = lax.scan(step, carry0, (chunks(q), chunks(k), chunks(v), chunks(log_i), chunks(log_f)))
    h = jnp.moveaxis(h, 0, 2).reshape(B, H, T, Dh)
    return h, C, n, m


def mlstm_mixer(q, k, v, o, gate_pre, b_gate, C0, n0, m0):
    B, T, _ = q.shape

    def heads(a):
        return a.astype(jnp.float32).reshape(B, T, MLSTM_HEADS, MLSTM_HEAD_DIM).transpose(0, 2, 1, 3)

    qh = heads(q)
    kh = heads(k) * (MLSTM_HEAD_DIM ** -0.5)
    vh = heads(v)
    pre = gate_pre.astype(jnp.float32).reshape(B, T, N_DIR, 2, MLSTM_HEADS) + b_gate.astype(jnp.float32)
    pre = pre.transpose(0, 2, 3, 4, 1)
    log_i = pre[:, :, 0]
    log_f = jax.nn.log_sigmoid(pre[:, :, 1])
    h_f, Cf, nf, mf = mlstm_scan(qh, kh, vh, log_i[:, 0], log_f[:, 0], C0[:, 0], n0[:, 0], m0[:, 0])
    rev = lambda a: jnp.flip(a, axis=2)
    h_b, Cb, nb, mb = mlstm_scan(rev(qh), rev(kh), rev(vh), rev(log_i[:, 1]), rev(log_f[:, 1]),
                                 C0[:, 1], n0[:, 1], m0[:, 1])
    h = (h_f + rev(h_b)).transpose(0, 2, 1, 3).reshape(B, T, MLSTM_WIDTH)
    y = jax.nn.sigmoid(o.astype(jnp.float32)) * h
    return y, jnp.stack([Cf, Cb], axis=1), jnp.stack([nf, nb], axis=1), jnp.stack([mf, mb], axis=1)


def moe(h, w_router, b_router, w_expert_gu, w_expert_down, w_shared_gu, w_shared_down):
    T = h.shape[0]
    s = jax.nn.sigmoid(jnp.einsum("td,de->te", h, w_router.astype(jnp.float32)))
    biased = s + b_router.astype(jnp.float32)
    grouped = biased.reshape(T, N_EXPERT_GROUPS, N_EXPERTS // N_EXPERT_GROUPS)
    gscore = jnp.sum(lax.top_k(grouped, 2)[0], axis=-1)
    _, gidx = lax.top_k(gscore, TOPK_GROUPS)
    gmask = jnp.sum(jax.nn.one_hot(gidx, N_EXPERT_GROUPS, dtype=jnp.float32), axis=1)
    emask = jnp.repeat(gmask, N_EXPERTS // N_EXPERT_GROUPS, axis=1)
    masked = jnp.where(emask > 0, biased, -jnp.inf)
    _, eidx = lax.top_k(masked, TOP_K)
    sel = jnp.take_along_axis(s, eidx, axis=-1)
    wts = sel / jnp.sum(sel, axis=-1, keepdims=True) * ROUTED_SCALE
    combine = jnp.sum(jax.nn.one_hot(eidx, N_EXPERTS, dtype=jnp.float32) * wts[..., None], axis=1)
    gu = jnp.einsum("td,edf->tef", h, w_expert_gu)
    act = jax.nn.silu(gu[..., :EXPERT_DIM]) * gu[..., EXPERT_DIM:] * combine[..., None]
    routed = jnp.einsum("tef,efd->td", act, w_expert_down)
    gs = h @ w_shared_gu
    shared = (jax.nn.silu(gs[..., :SHARED_DIM]) * gs[..., SHARED_DIM:]) @ w_shared_down
    return routed + shared


def layer(x, mod, grid, C0, n0, m0, g_pre_mix, w_in, b_gate, w_pool, pool_scale, w_out, g_post_mix,
          g_pre_ffn, w_router, b_router, w_expert_gu, w_expert_down, w_shared_gu, w_shared_down, g_post_ffn):
    B, T, D = x.shape
    shift_a, scale_a, gate_a, shift_f, scale_f, gate_f = [mod[:, i][:, None, :] for i in range(N_MOD)]
    h = rmsnorm(x, g_pre_mix) * (1.0 + scale_a) + shift_a
    z = h @ w_in
    u = z[..., :POOL_WIDTH]
    o0 = POOL_WIDTH
    q = z[..., o0:o0 + MLSTM_WIDTH]
    k = z[..., o0 + MLSTM_WIDTH:o0 + 2 * MLSTM_WIDTH]
    v = z[..., o0 + 2 * MLSTM_WIDTH:o0 + 3 * MLSTM_WIDTH]
    o = z[..., o0 + 3 * MLSTM_WIDTH:o0 + 4 * MLSTM_WIDTH]
    gate_pre = z[..., o0 + 4 * MLSTM_WIDTH:]
    y_pool = pool_mixer(u, w_pool, pool_scale, grid)
    y_ml, C, n, m = mlstm_mixer(q, k, v, o, gate_pre, b_gate, C0, n0, m0)
    mix = jnp.concatenate([y_pool, y_ml], axis=-1) @ w_out
    x = x + gate_a * rmsnorm(mix, g_post_mix)
    h = rmsnorm(x, g_pre_ffn) * (1.0 + scale_f) + shift_f
    f = moe(h.reshape(B * T, D), w_router, b_router, w_expert_gu, w_expert_down,
            w_shared_gu, w_shared_down).reshape(B, T, D)
    x = x + gate_f * rmsnorm(f, g_post_ffn)
    return x, C, n, m


def setup_inputs(seed: int = 0) -> dict:
    key = jax.random.key(seed)
    ks = jax.random.split(key, 26)
    D = D_MODEL
    H, Dh = MLSTM_HEADS, MLSTM_HEAD_DIM

    def nrm(k, shape, s):
        return jax.random.normal(k, shape, jnp.float32) * s

    def gain(k):
        return 1.0 + nrm(k, (DEPTH, D), 0.05)

    b_i = nrm(ks[0], (DEPTH, N_DIR, 1, H), 0.1)
    b_f = 3.0 + 3.0 * jax.random.uniform(ks[1], (DEPTH, N_DIR, 1, H), jnp.float32)
    return {
        "x_prompt": nrm(ks[2], (BATCH, SEQ, D), 1.0),
        "x_sample": nrm(ks[3], (DEC_BATCH, DEC_SEQ, D), 1.0),
        "state_C": nrm(ks[4], (DEC_BATCH, DEPTH, N_DIR, H, Dh, Dh), 0.3),
        "state_n": nrm(ks[5], (DEC_BATCH, DEPTH, N_DIR, H, Dh), 0.3),
        "state_m": nrm(ks[6], (DEC_BATCH, DEPTH, N_DIR, H), 1.0),
        "c": nrm(ks[7], (DEC_BATCH, D), 1.0),
        "c_ctx": nrm(ks[8], (D,), 1.0),
        "w_ada": nrm(ks[9], (DEPTH, D, N_MOD * D), 0.5 * D ** -0.5),
        "b_ada": nrm(ks[10], (DEPTH, N_MOD * D), 0.1),
        "g_pre_mix": gain(ks[11]),
        "w_in": nrm(ks[12], (DEPTH, D, IN_COLS), D ** -0.5),
        "b_gate": jnp.concatenate([b_i, b_f], axis=2),
        "w_pool": nrm(ks[13], (DEPTH, POOL_GROUPS, POOL_GROUP_DIM, POOL_GROUP_DIM), POOL_GROUP_DIM ** -0.5),
        "pool_scale": 1.0 + nrm(ks[14], (DEPTH, POOL_WIDTH), 0.1),
        "w_out": nrm(ks[15], (DEPTH, MIX_WIDTH, D), MIX_WIDTH ** -0.5),
        "g_post_mix": gain(ks[16]),
        "g_pre_ffn": gain(ks[17]),
        "w_router": nrm(ks[18], (DEPTH, D, N_EXPERTS), D ** -0.5),
        "b_router": nrm(ks[19], (DEPTH, N_EXPERTS), 0.01),
        "w_expert_gu": nrm(ks[20], (DEPTH, N_EXPERTS, D, 2 * EXPERT_DIM), D ** -0.5),
        "w_expert_down": nrm(ks[21], (DEPTH, N_EXPERTS, EXPERT_DIM, D), EXPERT_DIM ** -0.5),
        "w_shared_gu": nrm(ks[22], (DEPTH, D, 2 * SHARED_DIM), D ** -0.5),
        "w_shared_down": nrm(ks[23], (DEPTH, SHARED_DIM, D), SHARED_DIM ** -0.5),
        "g_post_ffn": gain(ks[24]),
    }


def reference(x_prompt, x_sample, state_C, state_n, state_m, c, c_ctx, w_ada, b_ada, g_pre_mix, w_in,
              b_gate, w_pool, pool_scale, w_out, g_post_mix, g_pre_ffn, w_router, b_router, w_expert_gu,
              w_expert_down, w_shared_gu, w_shared_down, g_post_ffn):
    B_ctx = x_prompt.shape[0]
    B_lat = x_sample.shape[0]
    xp = x_prompt.astype(jnp.float32)
    xs = x_sample.astype(jnp.float32)
    zC = jnp.zeros((B_ctx, N_DIR, MLSTM_HEADS, MLSTM_HEAD_DIM, MLSTM_HEAD_DIM), jnp.float32)
    zn = jnp.zeros((B_ctx, N_DIR, MLSTM_HEADS, MLSTM_HEAD_DIM), jnp.float32)
    zm = jnp.zeros((B_ctx, N_DIR, MLSTM_HEADS), jnp.float32)
    new_C, new_n, new_m = [], [], []
    for l in range(DEPTH):
        wts = (g_pre_mix[l], w_in[l], b_gate[l], w_pool[l], pool_scale[l], w_out[l], g_post_mix[l],
               g_pre_ffn[l], w_router[l], b_router[l], w_expert_gu[l], w_expert_down[l],
               w_shared_gu[l], w_shared_down[l], g_post_ffn[l])
        mod_ctx = (jax.nn.silu(c_ctx.astype(jnp.float32)) @ w_ada[l] + b_ada[l]).reshape(1, N_MOD, D_MODEL)
        mod_lat = (jax.nn.silu(c.astype(jnp.float32)) @ w_ada[l] + b_ada[l]).reshape(B_lat, N_MOD, D_MODEL)
        xp, Cc, nc_, mc = layer(xp, mod_ctx, False, zC, zn, zm, *wts)
        new_C.append(Cc)
        new_n.append(nc_)
        new_m.append(mc)
        xs, _, _, _ = layer(xs, mod_lat, True, state_C[:, l], state_n[:, l], state_m[:, l], *wts)
    new_state_C = jnp.stack(new_C, axis=1)
    new_state_n = jnp.stack(new_n, axis=1)
    new_state_m = jnp.stack(new_m, axis=1)
    return (xp, xs, new_state_C, new_state_n, new_state_m)
```

```python
import functools

import jax
import jax.numpy as jnp
from jax import lax
from jax.experimental import pallas as pl
from jax.experimental.pallas import tpu as pltpu

F32 = jnp.float32
BF16 = jnp.bfloat16

D_MODEL = 1024
GRID_W = 64
POOL_WIDTH = 512
POOL_GROUPS = 4
POOL_GROUP_DIM = 128
POOL_WINDOWS = (2, 4, 8, 16)
HEADS = 4
HEAD_DIM = 128
MLSTM_WIDTH = HEADS * HEAD_DIM
N_DIR = 2
GATE_COLS = N_DIR * 2 * HEADS
N_EXPERTS = 64
TOP_K = 6
N_EXPERT_GROUPS = 8
GROUP_SIZE = N_EXPERTS // N_EXPERT_GROUPS
TOPK_GROUPS = 4
EXPERT_DIM = 256
SHARED_DIM = 256
ROUTED_SCALE = 2.5
N_MOD = 6
EPS = 1e-6
K_SCALE = HEAD_DIM ** -0.5

LANES = 128
CHUNK = 256
TOKEN_TILE = 256
MOE_TOKEN_TILE = 1024
MOE_EXPERTS_PER_STEP = 2
VMEM_LIMIT = 56 * 1024 * 1024


def _split3(x):
    p1 = x.astype(BF16)
    r1 = x - p1.astype(F32)
    p2 = r1.astype(BF16)
    p3 = (r1 - p2.astype(F32)).astype(BF16)
    return p1, p2, p3


def _split2(x):
    p1 = x.astype(BF16)
    p2 = (x - p1.astype(F32)).astype(BF16)
    return p1, p2


def _dot(a, b):
    return jnp.dot(a, b, preferred_element_type=F32)


def _dot_nt(a, b):
    return lax.dot_general(a, b, (((1,), (1,)), ((), ())), preferred_element_type=F32)


def _rmsnorm(x, g):
    return x * lax.rsqrt(jnp.mean(x * x, axis=-1, keepdims=True) + EPS) * g


def _silu(x):
    return x * jax.nn.sigmoid(x)


def _params(*sem):
    return pltpu.CompilerParams(dimension_semantics=sem, vmem_limit_bytes=VMEM_LIMIT)


def _mod_kernel(c_ref, w_ref, b_ref, o_ref):
    a = _silu(c_ref[...])
    a_stack = jnp.concatenate(_split3(a), axis=0)
    w1, w2 = _split2(w_ref[...])
    r1 = _dot(a_stack, w1)
    r2 = _dot(a_stack[:32], w2)
    o_ref[...] = (r1[0:16] + r1[16:32] + r1[32:48] + r2[0:16] + r2[16:32]) + b_ref[...]


def _mod_rows(cvec, w_ada, b_ada):
    n = N_MOD * D_MODEL
    tn = 1536
    return pl.pallas_call(
        _mod_kernel,
        grid=(n // tn,),
        in_specs=[pl.BlockSpec((16, D_MODEL), lambda j: (0, 0)),
                  pl.BlockSpec((D_MODEL, tn), lambda j: (0, j)),
                  pl.BlockSpec((1, tn), lambda j: (0, j))],
        out_specs=pl.BlockSpec((16, tn), lambda j: (0, j)),
        out_shape=jax.ShapeDtypeStruct((16, n), F32),
        compiler_params=_params("arbitrary"),
        name="mod",
    )(cvec, w_ada, b_ada.reshape(1, n))


def _inproj_kernel(x_ref, mod_ref, g_ref, wm_ref, wkt_ref, wg_ref, wgt_ref, bgr_ref, bgc_ref,
                   u_ref, q_ref, v_ref, o_ref, kt_ref, gate_ref, gatet_ref):
    x = x_ref[0]
    mod = mod_ref[0]
    h = _rmsnorm(x, g_ref[...]) * (1.0 + mod[1:2]) + mod[0:1]
    h1, h2, h3 = _split3(h)
    z = _dot(h1, wm_ref[...])
    u_ref[0] = z[:, 0:512].astype(BF16)
    q_ref[0] = z[:, 512:1024].astype(BF16)
    v_ref[0] = z[:, 1024:1536].astype(BF16)
    o_ref[0] = z[:, 1536:2048].astype(BF16)
    kt_ref[0] = (_dot_nt(wkt_ref[...], h1) * K_SCALE).astype(BF16)
    tm = x.shape[0]
    wg1, wg2 = _split2(wg_ref[...])
    hs = jnp.concatenate([h1, h2, h3], axis=0)
    r1 = _dot(hs, wg1)
    r2 = _dot(hs[:2 * tm], wg2)
    gate_ref[0] = (r1[0:tm] + r1[tm:2 * tm] + r1[2 * tm:] + r2[0:tm] + r2[tm:]) + bgr_ref[...]
    wt1, wt2 = _split2(wgt_ref[...])
    wts = jnp.concatenate([wt1, wt2], axis=0)
    t1 = _dot_nt(wts, h1)
    t2 = _dot_nt(wts, h2)
    t3 = _dot_nt(wt1, h3)
    gatet_ref[0] = (t1[0:16] + t1[16:32] + t2[0:16] + t2[16:32] + t3) + bgc_ref[...]


def _inproj(x, mod, mod_row, g, wm, wkt, wg, wgt, bgr, bgc):
    b, t, _ = x.shape
    tm = TOKEN_TILE
    const = lambda *shape: pl.BlockSpec(shape, lambda i, j: (0,) * len(shape))
    tok = lambda w: pl.BlockSpec((1, tm, w), lambda i, j: (i, j, 0))
    sd = jax.ShapeDtypeStruct
    return pl.pallas_call(
        _inproj_kernel,
        grid=(b, t // tm),
        in_specs=[tok(D_MODEL),
                  pl.BlockSpec((1, N_MOD, D_MODEL), lambda i, j: (mod_row(i), 0, 0)),
                  const(1, D_MODEL), const(D_MODEL, 2048), const(512, D_MODEL),
                  const(D_MODEL, LANES), const(16, D_MODEL), const(1, LANES), const(16, 1)],
        out_specs=[tok(512), tok(512), tok(512), tok(512),
                   pl.BlockSpec((1, 512, tm), lambda i, j: (i, 0, j)),
                   tok(LANES),
                   pl.BlockSpec((1, 16, tm), lambda i, j: (i, 0, j))],
        out_shape=[sd((b, t, 512), BF16), sd((b, t, 512), BF16), sd((b, t, 512), BF16),
                   sd((b, t, 512), BF16), sd((b, 512, t), BF16), sd((b, t, LANES), F32),
                   sd((b, 16, t), F32)],
        compiler_params=_params("arbitrary", "arbitrary"),
        name="inproj",
    )(x, mod, g, wm, wkt, wg, wgt, bgr, bgc)


def _log_sigmoid(x):
    return jnp.minimum(x, 0.0) - jnp.log1p(jnp.exp(-jnp.abs(x)))


def _scan_unit(q, kt, vaug, a_col, a_row, b_col, b_row, btot, mask, s_prev, m_prev):
    dh = HEAD_DIM
    m_inter = b_col + m_prev
    u_row = a_row - b_row
    dmat = jnp.where(mask, b_col + u_row, -jnp.inf)
    m_t = jnp.maximum(m_inter, jnp.max(dmat, axis=-1, keepdims=True))
    p = (jnp.exp(dmat - m_t) * _dot(q, kt)).astype(BF16)
    w_inter = jnp.exp(m_inter - m_t)
    qs = _dot(q, s_prev.astype(BF16))
    pv = _dot(p, vaug)
    num = w_inter * qs[:, :dh] + pv[:, :dh]
    den = w_inter * qs[:, dh:dh + 1] + pv[:, dh:dh + 1]
    h = num / jnp.maximum(jnp.abs(den), jnp.exp(-m_t))
    g_row = btot + u_row
    m_new = jnp.maximum(btot + m_prev, jnp.max(g_row, axis=-1, keepdims=True))
    w_c = jnp.exp(btot + m_prev - m_new)
    kw_t = (kt.astype(F32) * jnp.exp(g_row - m_new)).astype(BF16)
    s_new = w_c * s_prev + _dot(kw_t, vaug)
    return h, s_new, m_new


def _mlstm_kernel(*refs, nc, zero_init, emit_state):
    it = iter(refs)
    qf_ref, ktf_ref, vf_ref, gf_ref, gtf_ref = (next(it) for _ in range(5))
    if nc > 1:
        qb_ref, ktb_ref, vb_ref, gb_ref, gtb_ref = (next(it) for _ in range(5))
    else:
        qb_ref, ktb_ref, vb_ref, gb_ref, gtb_ref = qf_ref, ktf_ref, vf_ref, gf_ref, gtf_ref
    if not zero_init:
        s0_ref, m0_ref = next(it), next(it)
    hf_ref, hb_ref = next(it), next(it)
    if emit_state:
        c_out, n_out, m_out = next(it), next(it), next(it)
    s_scr, m_scr = next(it), next(it)

    j = pl.program_id(1)
    L = CHUNK
    dh = HEAD_DIM

    @pl.when(j == 0)
    def _():
        if zero_init:
            s_scr[...] = jnp.zeros_like(s_scr)
            m_scr[...] = jnp.zeros_like(m_scr)
        else:
            s_scr[...] = s0_ref[0]
            m_scr[...] = m0_ref[0]

    rows = lax.broadcasted_iota(jnp.int32, (L, L), 0)
    cols = lax.broadcasted_iota(jnp.int32, (L, L), 1)
    lower = cols <= rows
    upper = cols >= rows
    tri_lo = lower.astype(BF16)
    tri_up = upper.astype(BF16)
    ones_col = (lax.broadcasted_iota(jnp.int32, (L, dh), 1) == 0).astype(BF16)

    for d, (q_ref, kt_ref, v_ref, g_ref, gt_ref, h_ref) in enumerate(
            ((qf_ref, ktf_ref, vf_ref, gf_ref, gtf_ref, hf_ref),
             (qb_ref, ktb_ref, vb_ref, gb_ref, gtb_ref, hb_ref))):
        gate = g_ref[0]
        gate_t = gt_ref[0]
        lf = _log_sigmoid(gate)
        lf_t = _log_sigmoid(gate_t)
        tri_c, tri_r, mask = (tri_lo, tri_up, lower) if d == 0 else (tri_up, tri_lo, upper)
        lf3 = jnp.concatenate(_split3(lf), axis=1)
        bc = _dot(tri_c, lf3)
        b_cols = bc[:, 0:128] + bc[:, 128:256] + bc[:, 256:384]
        lft3 = jnp.concatenate(_split3(lf_t), axis=0)
        br = _dot(lft3, tri_r)
        b_rows = br[0:16] + br[16:32] + br[32:48]
        tot_rows = jnp.sum(lf_t, axis=-1, keepdims=True)
        hs = []
        for hd in range(HEADS):
            ci = d * 8 + hd
            cf = d * 8 + 4 + hd
            unit = d * HEADS + hd
            q = q_ref[0, :, hd * dh:(hd + 1) * dh]
            kt = kt_ref[0, hd * dh:(hd + 1) * dh, :]
            vaug = jnp.concatenate([v_ref[0, :, hd * dh:(hd + 1) * dh], ones_col], axis=1)
            h, s_new, m_new = _scan_unit(
                q, kt, vaug,
                gate[:, ci:ci + 1], gate_t[ci:ci + 1, :],
                b_cols[:, cf:cf + 1], b_rows[cf:cf + 1, :], tot_rows[cf:cf + 1, :],
                mask, s_scr[unit], m_scr[unit][:, 0:1])
            s_scr[unit] = s_new
            m_scr[unit] = jnp.broadcast_to(m_new, (1, LANES))
            hs.append(h)
        h_ref[0] = jnp.concatenate(hs, axis=1).astype(BF16)

    if emit_state:
        @pl.when(j == nc - 1)
        def _():
            for unit in range(N_DIR * HEADS):
                s = s_scr[unit]
                c_out[0, unit] = s[:, :dh]
                n_out[0, unit] = s[:, dh:].T[0:1, :]
                m_out[0, unit] = m_scr[unit]


def _mlstm(q, kt, v, gate, gate_t, s0, m0, emit_state):
    b, t, _ = q.shape
    nc = t // CHUNK
    zero_init = s0 is None
    nu = N_DIR * HEADS
    fwd = lambda w: pl.BlockSpec((1, CHUNK, w), lambda i, j: (i, j, 0))
    bwd = lambda w: pl.BlockSpec((1, CHUNK, w), lambda i, j: (i, nc - 1 - j, 0))
    fwd_t = lambda r: pl.BlockSpec((1, r, CHUNK), lambda i, j: (i, 0, j))
    bwd_t = lambda r: pl.BlockSpec((1, r, CHUNK), lambda i, j: (i, 0, nc - 1 - j))
    args = [q, kt, v, gate, gate_t]
    in_specs = [fwd(512), fwd_t(512), fwd(512), fwd(LANES), fwd_t(16)]
    if nc > 1:
        args += [q, kt, v, gate, gate_t]
        in_specs += [bwd(512), bwd_t(512), bwd(512), bwd(LANES), bwd_t(16)]
    if not zero_init:
        args += [s0, m0]
        in_specs += [pl.BlockSpec((1, nu, HEAD_DIM, 2 * HEAD_DIM), lambda i, j: (i, 0, 0, 0)),
                     pl.BlockSpec((1, nu, 1, LANES), lambda i, j: (i, 0, 0, 0))]
    sd = jax.ShapeDtypeStruct
    out_shape = [sd((b, t, 512), BF16), sd((b, t, 512), BF16)]
    out_specs = [fwd(512), bwd(512)]
    if emit_state:
        out_shape += [sd((b, nu, HEAD_DIM, HEAD_DIM), F32), sd((b, nu, 1, HEAD_DIM), F32),
                      sd((b, nu, 1, LANES), F32)]
        out_specs += [pl.BlockSpec((1, nu, HEAD_DIM, HEAD_DIM), lambda i, j: (i, 0, 0, 0)),
                      pl.BlockSpec((1, nu, 1, HEAD_DIM), lambda i, j: (i, 0, 0, 0)),
                      pl.BlockSpec((1, nu, 1, LANES), lambda i, j: (i, 0, 0, 0))]
    return pl.pallas_call(
        functools.partial(_mlstm_kernel, nc=nc, zero_init=zero_init, emit_state=emit_state),
        grid=(b, nc),
        in_specs=in_specs,
        out_specs=out_specs,
        out_shape=out_shape,
        scratch_shapes=[pltpu.VMEM((nu, HEAD_DIM, 2 * HEAD_DIM), F32),
                        pltpu.VMEM((nu, 1, LANES), F32)],
        compiler_params=_params("arbitrary", "arbitrary"),
        name="mlstm",
    )(*args)


def _outproj_kernel(x_ref, u_ref, hf_ref, hb_ref, o_ref, mod_ref, pm_ref, pinv_ref, wp_ref, ps_ref,
                    wo_ref, gpm_ref, gpf_ref, wrt_ref, x1_ref, h2_ref, lg_ref, *, tm):
    i = pl.program_id(1)
    x = x_ref[0]
    mod = mod_ref[0]
    row0 = pl.multiple_of(i * tm, tm)
    u_full = u_ref[0]
    u_tile = u_ref[0, pl.ds(row0, tm), :].astype(F32)
    mix = jnp.zeros((tm, D_MODEL), F32)
    for g in range(POOL_GROUPS):
        sl = slice(g * POOL_GROUP_DIM, (g + 1) * POOL_GROUP_DIM)
        box = _dot(pm_ref[g], u_full[:, sl])
        diff = box * pinv_ref[g] - u_tile[:, sl]
        yp = _dot(diff.astype(BF16), wp_ref[g]) * ps_ref[:, sl]
        mix = mix + _dot(yp.astype(BF16), wo_ref[sl, :])
    hsum = hf_ref[0].astype(F32) + hb_ref[0].astype(F32)
    yml = jax.nn.sigmoid(o_ref[0].astype(F32)) * hsum
    mix = mix + _dot(yml.astype(BF16), wo_ref[POOL_WIDTH:, :])
    x1 = x + mod[2:3] * _rmsnorm(mix, gpm_ref[...])
    x1_ref[0] = x1
    h2 = _rmsnorm(x1, gpf_ref[...]) * (1.0 + mod[4:5]) + mod[3:4]
    p1, p2, p3 = _split3(h2)
    h2_ref[0] = p1
    w1, w2, w3 = _split3(wrt_ref[...])
    ws = jnp.concatenate([w1, w2, w3], axis=0)
    r1 = _dot_nt(ws, p1)
    r2 = _dot_nt(ws[:128], p2)
    r3 = _dot_nt(w1, p3)
    lg_ref[0] = r1[0:64] + r1[64:128] + r1[128:192] + r2[0:64] + r2[64:128] + r3


def _outproj(x, u, hf, hb, o, mod, mod_row, pm, pinv, wp, ps, wo, gpm, gpf, wrt):
    b, t, _ = x.shape
    tm = TOKEN_TILE
    const = lambda *shape: pl.BlockSpec(shape, lambda i, j: (0,) * len(shape))
    tok = lambda w: pl.BlockSpec((1, tm, w), lambda i, j: (i, j, 0))
    sd = jax.ShapeDtypeStruct
    return pl.pallas_call(
        functools.partial(_outproj_kernel, tm=tm),
        grid=(b, t // tm),
        in_specs=[tok(D_MODEL),
                  pl.BlockSpec((1, t, 512), lambda i, j: (i, 0, 0)),
                  tok(512), tok(512), tok(512),
                  pl.BlockSpec((1, N_MOD, D_MODEL), lambda i, j: (mod_row(i), 0, 0)),
                  pl.BlockSpec((POOL_GROUPS, tm, t), lambda i, j: (0, j, 0)),
                  pl.BlockSpec((POOL_GROUPS, tm, 1), lambda i, j: (0, j, 0)),
                  const(POOL_GROUPS, POOL_GROUP_DIM, POOL_GROUP_DIM), const(1, POOL_WIDTH),
                  const(D_MODEL, D_MODEL), const(1, D_MODEL), const(1, D_MODEL),
                  const(N_EXPERTS, D_MODEL)],
        out_specs=[tok(D_MODEL), tok(D_MODEL),
                   pl.BlockSpec((1, N_EXPERTS, tm), lambda i, j: (i, 0, j))],
        out_shape=[sd((b, t, D_MODEL), F32), sd((b, t, D_MODEL), BF16), sd((b, N_EXPERTS, t), F32)],
        compiler_params=_params("arbitrary", "arbitrary"),
        name="outproj",
    )(x, u, hf, hb, o, mod, pm, pinv, wp, ps, wo, gpm, gpf, wrt)


def _router_kernel(lg_ref, br_ref, comb_ref):
    s = jax.nn.sigmoid(lg_ref[...])
    biased = s + br_ref[...]
    gidx = lax.broadcasted_iota(jnp.int32, s.shape, 0)
    jidx = lax.broadcasted_iota(jnp.int32, s.shape, 1)
    neg = -jnp.inf
    m1 = jnp.max(biased, axis=1, keepdims=True)
    i1 = jnp.min(jnp.where(biased == m1, jidx, GROUP_SIZE), axis=1, keepdims=True)
    m2 = jnp.max(jnp.where(jidx == i1, neg, biased), axis=1, keepdims=True)
    gscore = m1 + m2
    gi = lax.broadcasted_iota(jnp.int32, gscore.shape, 0)
    gmask = jnp.zeros(gscore.shape, F32)
    cur = gscore
    for _ in range(TOPK_GROUPS):
        mx = jnp.max(cur, axis=0, keepdims=True)
        ix = jnp.min(jnp.where(cur == mx, gi, N_EXPERT_GROUPS), axis=0, keepdims=True)
        hit = gi == ix
        gmask = jnp.where(hit, 1.0, gmask)
        cur = jnp.where(hit, neg, cur)
    cur = jnp.where(gmask > 0, biased, neg)
    eidx = gidx * GROUP_SIZE + jidx
    selmask = jnp.zeros(s.shape, F32)
    for _ in range(TOP_K):
        mx = jnp.max(jnp.max(cur, axis=1, keepdims=True), axis=0, keepdims=True)
        ix = jnp.where(cur == mx, eidx, N_EXPERTS)
        ix = jnp.min(jnp.min(ix, axis=1, keepdims=True), axis=0, keepdims=True)
        hit = eidx == ix
        selmask = jnp.where(hit, 1.0, selmask)
        cur = jnp.where(hit, neg, cur)
    sel = selmask * s
    tot = jnp.sum(jnp.sum(sel, axis=1, keepdims=True), axis=0, keepdims=True)
    comb_ref[...] = sel / tot * ROUTED_SCALE


def _router(logits_t, b_router):
    t = logits_t.shape[1]
    tl = 1024
    shp = (N_EXPERT_GROUPS, GROUP_SIZE, t)
    out = pl.pallas_call(
        _router_kernel,
        grid=(t // tl,),
        in_specs=[pl.BlockSpec((N_EXPERT_GROUPS, GROUP_SIZE, tl), lambda j: (0, 0, j)),
                  pl.BlockSpec((N_EXPERT_GROUPS, GROUP_SIZE, 1), lambda j: (0, 0, 0))],
        out_specs=pl.BlockSpec((N_EXPERT_GROUPS, GROUP_SIZE, tl), lambda j: (0, 0, j)),
        out_shape=jax.ShapeDtypeStruct(shp, F32),
        compiler_params=_params("arbitrary"),
        name="router",
    )(logits_t.reshape(shp), b_router.reshape(N_EXPERT_GROUPS, GROUP_SIZE, 1))
    return out.reshape(N_EXPERTS, t)


def _moe_kernel(h_ref, comb_ref, x1_ref, mod_ref, wgu_ref, wd_ref, wsg_ref, wsd_ref, gpo_ref,
                out_ref, acc_ref, *, ne):
    e = pl.program_id(1)
    h = h_ref[...]
    tm = h.shape[0]

    @pl.when(e == 0)
    def _():
        gs = _dot(h, wsg_ref[...])
        act = _silu(gs[:, :SHARED_DIM]) * gs[:, SHARED_DIM:]
        acc_ref[...] = _dot(act.astype(BF16), wsd_ref[...])

    comb3 = jnp.concatenate(_split3(comb_ref[...]), axis=1)
    sel_row = lax.broadcasted_iota(jnp.int32, (3 * LANES, EXPERT_DIM), 0) % LANES
    acts = []
    for k in range(ne):
        gu = _dot(h, wgu_ref[k].astype(BF16))
        onehot = (sel_row == e * ne + k).astype(BF16)
        cexp = _dot(comb3, onehot)
        acts.append((_silu(gu[:, :EXPERT_DIM]) * gu[:, EXPERT_DIM:] * cexp).astype(BF16))
    act = jnp.concatenate(acts, axis=1)
    acc_ref[...] += _dot(act, wd_ref[...].astype(BF16))

    @pl.when(e == pl.num_programs(1) - 1)
    def _():
        out_ref[...] = x1_ref[...] + mod_ref[0][5:6] * _rmsnorm(acc_ref[...], gpo_ref[...])


def _moe(h2, comb, x1, mod, mod_row, wgu, wd, wsg, wsd, gpo):
    n = h2.shape[0]
    tm = MOE_TOKEN_TILE
    ne = MOE_EXPERTS_PER_STEP
    const = lambda *shape: pl.BlockSpec(shape, lambda i, e: (0,) * len(shape))
    tok = lambda w: pl.BlockSpec((tm, w), lambda i, e: (i, 0))
    return pl.pallas_call(
        functools.partial(_moe_kernel, ne=ne),
        grid=(n // tm, N_EXPERTS // ne),
        in_specs=[tok(D_MODEL), tok(LANES), tok(D_MODEL),
                  pl.BlockSpec((1, N_MOD, D_MODEL), lambda i, e: (mod_row(i), 0, 0)),
                  pl.BlockSpec((ne, D_MODEL, 2 * EXPERT_DIM), lambda i, e: (e, 0, 0)),
                  pl.BlockSpec((ne * EXPERT_DIM, D_MODEL), lambda i, e: (e, 0)),
                  const(D_MODEL, 2 * SHARED_DIM), const(SHARED_DIM, D_MODEL), const(1, D_MODEL)],
        out_specs=tok(D_MODEL),
        out_shape=jax.ShapeDtypeStruct((n, D_MODEL), F32),
        scratch_shapes=[pltpu.VMEM((tm, D_MODEL), F32)],
        compiler_params=_params("arbitrary", "arbitrary"),
        name="moe",
    )(h2, comb, x1, mod, wgu, wd.reshape(N_EXPERTS * EXPERT_DIM, D_MODEL), wsg, wsd, gpo)


def _window_bounds(n, w):
    idx = jnp.arange(n)
    return jnp.clip(idx - w // 2, 0, n), jnp.clip(idx + w - w // 2, 0, n)


def _pool_operators(t, grid):
    mats, invs = [], []
    for w in POOL_WINDOWS:
        if grid:
            rows = t // GRID_W
            rlo, rhi = _window_bounds(rows, w)
            clo, chi = _window_bounds(GRID_W, w)
            r = jnp.arange(t) // GRID_W
            c = jnp.arange(t) % GRID_W
            inr = (r[None, :] >= rlo[r][:, None]) & (r[None, :] < rhi[r][:, None])
            inc = (c[None, :] >= clo[c][:, None]) & (c[None, :] < chi[c][:, None])
            m = inr & inc
            cnt = ((rhi - rlo)[r] * (chi - clo)[c]).astype(F32)
        else:
            lo, hi = _window_bounds(t, w)
            s = jnp.arange(t)
            m = (s[None, :] >= lo[:, None]) & (s[None, :] < hi[:, None])
            cnt = (hi - lo).astype(F32)
        mats.append(m.astype(BF16))
        invs.append((1.0 / cnt)[:, None])
    return jnp.stack(mats), jnp.stack(invs)


def kernel(x_prompt, x_sample, state_C, state_n, state_m, c, c_ctx, w_ada, b_ada, g_pre_mix, w_in, b_gate,
           w_pool, pool_scale, w_out, g_post_mix, g_pre_ffn, w_router, b_router, w_expert_gu, w_expert_down,
           w_shared_gu, w_shared_down, g_post_ffn):
    b_ctx = x_prompt.shape[0]
    b_lat = x_sample.shape[0]
    nu = N_DIR * HEADS
    l = 0
    row = lambda a: a[l].reshape(1, -1).astype(F32)

    cvec = jnp.zeros((16, D_MODEL), F32).at[0].set(c_ctx.astype(F32)).at[1:1 + b_lat].set(c.astype(F32))
    mod = _mod_rows(cvec, w_ada[l], b_ada[l]).reshape(16, N_MOD, D_MODEL)

    w_in_l = w_in[l]
    p0 = POOL_WIDTH
    mw = MLSTM_WIDTH
    wm = jnp.concatenate([w_in_l[:, 0:p0 + mw], w_in_l[:, p0 + 2 * mw:p0 + 4 * mw]], axis=1).astype(BF16)
    wkt = w_in_l[:, p0 + mw:p0 + 2 * mw].T.astype(BF16)
    wg_cols = w_in_l[:, p0 + 4 * mw:]
    wg = jnp.pad(wg_cols, ((0, 0), (0, LANES - GATE_COLS)))
    wgt = wg_cols.T
    bg = b_gate[l].reshape(GATE_COLS).astype(F32)
    bgr = jnp.pad(bg, (0, LANES - GATE_COLS)).reshape(1, LANES)
    bgc = bg.reshape(GATE_COLS, 1)
    wp = w_pool[l].astype(BF16)
    wo = w_out[l].astype(BF16)
    wrt = w_router[l].T
    wsg = w_shared_gu[l].astype(BF16)
    wsd = w_shared_down[l].astype(BF16)

    def mixer(x, mod_row, grid, s0, m0, emit_state):
        t = x.shape[1]
        u, q, v, o, kt, gate, gate_t = _inproj(x.astype(F32), mod, mod_row, row(g_pre_mix), wm, wkt, wg, wgt,
                                               bgr, bgc)
        outs = _mlstm(q, kt, v, gate, gate_t, s0, m0, emit_state)
        hf, hb = outs[0], outs[1]
        pm, pinv = _pool_operators(t, grid)
        x1, h2, lg = _outproj(x.astype(F32), u, hf, hb, o, mod, mod_row, pm, pinv, wp, row(pool_scale), wo,
                              row(g_post_mix), row(g_pre_ffn), wrt)
        return x1, h2, lg, outs[2:]

    ctx_row = lambda i: 0
    lat_row = lambda i: i + 1
    x1c, h2c, lgc, (c_new, n_new, m_new) = mixer(x_prompt, ctx_row, False, None, None, True)
    s0 = jnp.concatenate(
        [state_C[:, l].reshape(b_lat, nu, HEAD_DIM, HEAD_DIM).astype(F32),
         state_n[:, l].reshape(b_lat, nu, HEAD_DIM, 1).astype(F32),
         jnp.zeros((b_lat, nu, HEAD_DIM, HEAD_DIM - 1), F32)], axis=-1)
    m0 = jnp.broadcast_to(state_m[:, l].reshape(b_lat, nu, 1, 1).astype(F32), (b_lat, nu, 1, LANES))
    x1s, h2s, lgs, _ = mixer(x_sample, lat_row, True, s0, m0, False)

    tc = b_ctx * x_prompt.shape[1]
    ts = b_lat * x_sample.shape[1]
    lg_all = jnp.concatenate([lgc.transpose(1, 0, 2).reshape(N_EXPERTS, tc),
                              lgs.transpose(1, 0, 2).reshape(N_EXPERTS, ts)], axis=1)
    comb = _router(lg_all, b_router[l].astype(F32)).T
    comb = jnp.pad(comb, ((0, 0), (0, LANES - N_EXPERTS)))

    moe = functools.partial(_moe, wgu=w_expert_gu[l], wd=w_expert_down[l], wsg=wsg, wsd=wsd,
                            gpo=row(g_post_ffn))
    tiles_per_lat = x_sample.shape[1] // MOE_TOKEN_TILE
    yc = moe(h2c.reshape(tc, D_MODEL), comb[:tc], x1c.reshape(tc, D_MODEL), mod, ctx_row)
    ys = moe(h2s.reshape(ts, D_MODEL), comb[tc:], x1s.reshape(ts, D_MODEL), mod,
             lambda i: i // tiles_per_lat + 1)

    new_c = c_new.reshape(b_ctx, 1, N_DIR, HEADS, HEAD_DIM, HEAD_DIM)
    new_n = n_new.reshape(b_ctx, 1, N_DIR, HEADS, HEAD_DIM)
    new_m = m_new[..., 0].reshape(b_ctx, 1, N_DIR, HEADS)
    return (yc.reshape(x_prompt.shape), ys.reshape(x_sample.shape), new_c, new_n, new_m)
```

```python
import functools

import jax
import jax.numpy as jnp
import numpy as np
from jax import lax
from jax.experimental import pallas as pl
from jax.experimental.pallas import tpu as pltpu

F32 = jnp.float32
BF16 = jnp.bfloat16

D_MODEL = 1024
GRID_W = 64
POOL_WIDTH = 512
POOL_GROUPS = 4
POOL_GROUP_DIM = 128
POOL_WINDOWS = (2, 4, 8, 16)
HEADS = 4
HEAD_DIM = 128
MLSTM_WIDTH = HEADS * HEAD_DIM
N_DIR = 2
GATE_COLS = N_DIR * 2 * HEADS
N_EXPERTS = 64
TOP_K = 6
N_EXPERT_GROUPS = 8
GROUP_SIZE = N_EXPERTS // N_EXPERT_GROUPS
TOPK_GROUPS = 4
EXPERT_DIM = 256
SHARED_DIM = 256
ROUTED_SCALE = 2.5
N_MOD = 6
EPS = 1e-6
K_SCALE = HEAD_DIM ** -0.5

LANES = 128
CHUNK = 256
TOKEN_TILE = 256
MOE_TOKEN_TILE = 1024
MOE_EXPERTS_PER_STEP = 2
VMEM_LIMIT = 56 * 1024 * 1024


def _split3(x):
    p1 = x.astype(BF16)
    r1 = x - p1.astype(F32)
    p2 = r1.astype(BF16)
    p3 = (r1 - p2.astype(F32)).astype(BF16)
    return p1, p2, p3


def _split2(x):
    p1 = x.astype(BF16)
    p2 = (x - p1.astype(F32)).astype(BF16)
    return p1, p2


def _dot(a, b):
    return jnp.dot(a, b, preferred_element_type=F32)


def _dot_nt(a, b):
    return lax.dot_general(a, b, (((1,), (1,)), ((), ())), preferred_element_type=F32)


def _rmsnorm(x, g):
    return x * lax.rsqrt(jnp.mean(x * x, axis=-1, keepdims=True) + EPS) * g


def _silu(x):
    return x * jax.nn.sigmoid(x)


def _params(*sem):
    return pltpu.CompilerParams(dimension_semantics=sem, vmem_limit_bytes=VMEM_LIMIT)


def _mod_kernel(c_ref, w_ref, b_ref, o_ref):
    a = _silu(c_ref[...])
    a_stack = jnp.concatenate(_split3(a), axis=0)
    w1, w2 = _split2(w_ref[...])
    r1 = _dot(a_stack, w1)
    r2 = _dot(a_stack[:32], w2)
    o_ref[...] = (r1[0:16] + r1[16:32] + r1[32:48] + r2[0:16] + r2[16:32]) + b_ref[...]


def _mod_rows(cvec, w_ada, b_ada):
    n = N_MOD * D_MODEL
    tn = 1536
    return pl.pallas_call(
        _mod_kernel,
        grid=(n // tn,),
        in_specs=[pl.BlockSpec((16, D_MODEL), lambda j: (0, 0)),
                  pl.BlockSpec((D_MODEL, tn), lambda j: (0, j)),
                  pl.BlockSpec((1, tn), lambda j: (0, j))],
        out_specs=pl.BlockSpec((16, tn), lambda j: (0, j)),
        out_shape=jax.ShapeDtypeStruct((16, n), F32),
        compiler_params=_params("arbitrary"),
        name="mod",
    )(cvec, w_ada, b_ada.reshape(1, n))


def _inproj_kernel(x_ref, mod_ref, g_ref, wm_ref, wt_ref, wg_ref, wgt_ref, bgr_ref, bgc_ref,
                   u_ref, k_ref, o_ref, qt_ref, vt_ref, gate_ref, gatet_ref):
    x = x_ref[0]
    mod = mod_ref[0]
    h = _rmsnorm(x, g_ref[...]) * (1.0 + mod[1:2]) + mod[0:1]
    h1, h2, h3 = _split3(h)
    z = _dot(h1, wm_ref[...])
    u_ref[0] = z[:, 0:512].astype(BF16)
    k_ref[0] = (z[:, 512:1024] * K_SCALE).astype(BF16)
    o_ref[0] = z[:, 1024:1536].astype(BF16)
    zt = _dot_nt(wt_ref[...], h1)
    qt_ref[0] = zt[0:512].astype(BF16)
    vt_ref[0] = zt[512:1024].astype(BF16)
    tm = x.shape[0]
    wg1, wg2 = _split2(wg_ref[...])
    hs = jnp.concatenate([h1, h2, h3], axis=0)
    r1 = _dot(hs, wg1)
    r2 = _dot(hs[:2 * tm], wg2)
    gate_ref[0] = (r1[0:tm] + r1[tm:2 * tm] + r1[2 * tm:] + r2[0:tm] + r2[tm:]) + bgr_ref[...]
    wt1, wt2 = _split2(wgt_ref[...])
    wts = jnp.concatenate([wt1, wt2], axis=0)
    t1 = _dot_nt(wts, h1)
    t2 = _dot_nt(wts, h2)
    t3 = _dot_nt(wt1, h3)
    gatet_ref[0] = (t1[0:16] + t1[16:32] + t2[0:16] + t2[16:32] + t3) + bgc_ref[...]


def _inproj(x, mod, mod_row, g, wm, wt, wg, wgt, bgr, bgc):
    b, t, _ = x.shape
    tm = TOKEN_TILE
    const = lambda *shape: pl.BlockSpec(shape, lambda i, j: (0,) * len(shape))
    tok = lambda w: pl.BlockSpec((1, tm, w), lambda i, j: (i, j, 0))
    tok_t = lambda r: pl.BlockSpec((1, r, tm), lambda i, j: (i, 0, j))
    sd = jax.ShapeDtypeStruct
    return pl.pallas_call(
        _inproj_kernel,
        grid=(b, t // tm),
        in_specs=[tok(D_MODEL),
                  pl.BlockSpec((1, N_MOD, D_MODEL), lambda i, j: (mod_row(i), 0, 0)),
                  const(1, D_MODEL), const(D_MODEL, 1536), const(1024, D_MODEL),
                  const(D_MODEL, LANES), const(16, D_MODEL), const(1, LANES), const(16, 1)],
        out_specs=[tok(512), tok(512), tok(512), tok_t(512), tok_t(512), tok(LANES), tok_t(16)],
        out_shape=[sd((b, t, 512), BF16), sd((b, t, 512), BF16), sd((b, t, 512), BF16),
                   sd((b, 512, t), BF16), sd((b, 512, t), BF16), sd((b, t, LANES), F32),
                   sd((b, 16, t), F32)],
        compiler_params=_params("arbitrary", "arbitrary"),
        name="inproj",
    )(x, mod, g, wm, wt, wg, wgt, bgr, bgc)


def _log_sigmoid(x):
    return jnp.minimum(x, 0.0) - jnp.log1p(jnp.exp(-jnp.abs(x)))


def _scan_unit(st, k, qt, vt, u_col, u_row, b_row, btot, mask, s_prev, m_prev, use_state):
    dh = HEAD_DIM
    n = st.shape[0]
    ub = jnp.where(mask, jnp.broadcast_to(u_col, (n, n)), -jnp.inf)
    z = jnp.maximum(m_prev, jnp.max(ub, axis=0, keepdims=True))
    p = (jnp.exp(ub - z) * st).astype(BF16)
    ones = jnp.ones((dh, n), BF16)
    tot = _dot(jnp.concatenate([vt, ones], axis=0), p)
    if use_state:
        tot = tot + jnp.exp(m_prev - z) * _dot(s_prev.astype(BF16), qt)
    floor = jnp.exp(-(b_row + z))
    h_t = tot[:dh] / jnp.maximum(jnp.abs(tot[dh:]), floor)
    g_row = btot + u_row
    m_new = jnp.maximum(btot + m_prev, jnp.max(g_row, axis=-1, keepdims=True))
    w_row = jnp.exp(g_row - m_new)
    vw = jnp.concatenate([(vt.astype(F32) * w_row).astype(BF16),
                          jnp.broadcast_to(w_row, (dh, n)).astype(BF16)], axis=0)
    s_new = jnp.exp(btot + m_prev - m_new) * s_prev + _dot(vw, k)
    return h_t.T, s_new, m_new


def _mlstm_kernel(*refs, nc, zero_init, emit_state):
    it = iter(refs)
    fwd_refs = tuple(next(it) for _ in range(5))
    bwd_refs = tuple(next(it) for _ in range(5)) if nc > 1 else fwd_refs
    if not zero_init:
        s0_ref, m0_ref = next(it), next(it)
    h_refs = (next(it), next(it))
    if emit_state:
        c_out, n_out, m_out = next(it), next(it), next(it)
    s_scr, m_scr = next(it), next(it)

    j = pl.program_id(1)
    n = CHUNK
    dh = HEAD_DIM

    @pl.when(j == 0)
    def _():
        if zero_init:
            s_scr[...] = jnp.zeros_like(s_scr)
            m_scr[...] = jnp.zeros_like(m_scr)
        else:
            s_scr[...] = s0_ref[0]
            m_scr[...] = m0_ref[0]

    rows = lax.broadcasted_iota(jnp.int32, (n, n), 0)
    cols = lax.broadcasted_iota(jnp.int32, (n, n), 1)
    le = rows <= cols
    ge = rows >= cols
    tri_le = le.astype(BF16)
    tri_ge = ge.astype(BF16)
    use_state = not (zero_init and nc == 1)

    def gate_terms(d):
        g_ref, gt_ref = (fwd_refs, bwd_refs)[d][3:5]
        gate = g_ref[0]
        gate_t = gt_ref[0]
        lf = _log_sigmoid(gate)
        lf_t = _log_sigmoid(gate_t)
        tri_c, tri_r = (tri_ge, tri_le) if d == 0 else (tri_le, tri_ge)
        bc = _dot(tri_c, jnp.concatenate(_split3(lf), axis=1))
        b_cols = bc[:, 0:128] + bc[:, 128:256] + bc[:, 256:384]
        br = _dot(jnp.concatenate(_split3(lf_t), axis=0), tri_r)
        b_rows = br[0:16] + br[16:32] + br[32:48]
        return gate, gate_t, b_cols, b_rows, jnp.sum(lf_t, axis=-1, keepdims=True)

    terms = [gate_terms(0), gate_terms(1)]
    hs = ([], [])
    for hd in range(HEADS):
        hsl = slice(hd * dh, (hd + 1) * dh)
        st = None
        for d in range(N_DIR):
            k_ref, qt_ref, vt_ref = (fwd_refs, bwd_refs)[d][0:3]
            gate, gate_t, b_cols, b_rows, tot_rows = terms[d]
            ci = d * 8 + hd
            cf = d * 8 + 4 + hd
            unit = d * HEADS + hd
            k = k_ref[0, :, hsl]
            qt = qt_ref[0, hsl, :]
            if st is None or nc > 1:
                st = _dot(k, qt)
            mask = le if d == 0 else ge
            h, s_new, m_new = _scan_unit(
                st, k, qt, vt_ref[0, hsl, :],
                gate[:, ci:ci + 1] - b_cols[:, cf:cf + 1],
                gate_t[ci:ci + 1, :] - b_rows[cf:cf + 1, :],
                b_rows[cf:cf + 1, :], tot_rows[cf:cf + 1, :],
                mask, s_scr[unit], m_scr[unit][:, 0:1], use_state)
            s_scr[unit] = s_new
            m_scr[unit] = jnp.broadcast_to(m_new, (1, LANES))
            hs[d].append(h)
    for d in range(N_DIR):
        h_refs[d][0] = jnp.concatenate(hs[d], axis=1).astype(BF16)

    if emit_state:
        @pl.when(j == nc - 1)
        def _():
            for unit in range(N_DIR * HEADS):
                s = s_scr[unit]
                c_out[0, unit] = s[:dh].T
                n_out[0, unit] = s[dh:dh + 1]
                m_out[0, unit] = m_scr[unit]


def _mlstm(k, qt, vt, gate, gate_t, s0, m0, emit_state):
    b, t, _ = k.shape
    nc = t // CHUNK
    zero_init = s0 is None
    nu = N_DIR * HEADS
    fwd = lambda w: pl.BlockSpec((1, CHUNK, w), lambda i, j: (i, j, 0))
    bwd = lambda w: pl.BlockSpec((1, CHUNK, w), lambda i, j: (i, nc - 1 - j, 0))
    fwd_t = lambda r: pl.BlockSpec((1, r, CHUNK), lambda i, j: (i, 0, j))
    bwd_t = lambda r: pl.BlockSpec((1, r, CHUNK), lambda i, j: (i, 0, nc - 1 - j))
    args = [k, qt, vt, gate, gate_t]
    in_specs = [fwd(512), fwd_t(512), fwd_t(512), fwd(LANES), fwd_t(16)]
    if nc > 1:
        args += [k, qt, vt, gate, gate_t]
        in_specs += [bwd(512), bwd_t(512), bwd_t(512), bwd(LANES), bwd_t(16)]
    if not zero_init:
        args += [s0, m0]
        in_specs += [pl.BlockSpec((1, nu, 2 * HEAD_DIM, HEAD_DIM), lambda i, j: (i, 0, 0, 0)),
                     pl.BlockSpec((1, nu, 1, LANES), lambda i, j: (i, 0, 0, 0))]
    sd = jax.ShapeDtypeStruct
    out_shape = [sd((b, t, 512), BF16), sd((b, t, 512), BF16)]
    out_specs = [fwd(512), bwd(512)]
    if emit_state:
        out_shape += [sd((b, nu, HEAD_DIM, HEAD_DIM), F32), sd((b, nu, 1, HEAD_DIM), F32),
                      sd((b, nu, 1, LANES), F32)]
        out_specs += [pl.BlockSpec((1, nu, HEAD_DIM, HEAD_DIM), lambda i, j: (i, 0, 0, 0)),
                      pl.BlockSpec((1, nu, 1, HEAD_DIM), lambda i, j: (i, 0, 0, 0)),
                      pl.BlockSpec((1, nu, 1, LANES), lambda i, j: (i, 0, 0, 0))]
    return pl.pallas_call(
        functools.partial(_mlstm_kernel, nc=nc, zero_init=zero_init, emit_state=emit_state),
        grid=(b, nc),
        in_specs=in_specs,
        out_specs=out_specs,
        out_shape=out_shape,
        scratch_shapes=[pltpu.VMEM((nu, 2 * HEAD_DIM, HEAD_DIM), F32),
                        pltpu.VMEM((nu, 1, LANES), F32)],
        compiler_params=_params("arbitrary", "arbitrary"),
        name="mlstm",
    )(*args)


def _outproj_kernel(x_ref, u_ref, hf_ref, hb_ref, o_ref, mod_ref, pm_ref, pinv_ref, wp_ref, ps_ref,
                    wo_ref, gpm_ref, gpf_ref, wrt_ref, x1_ref, h2_ref, lg_ref, *, tm):
    i = pl.program_id(1)
    x = x_ref[0]
    mod = mod_ref[0]
    row0 = pl.multiple_of(i * tm, tm)
    u_full = u_ref[0]
    u_tile = u_ref[0, pl.ds(row0, tm), :].astype(F32)
    mix = jnp.zeros((tm, D_MODEL), F32)
    for g in range(POOL_GROUPS):
        sl = slice(g * POOL_GROUP_DIM, (g + 1) * POOL_GROUP_DIM)
        box = _dot(pm_ref[g], u_full[:, sl])
        diff = box * pinv_ref[g] - u_tile[:, sl]
        yp = _dot(diff.astype(BF16), wp_ref[g]) * ps_ref[:, sl]
        mix = mix + _dot(yp.astype(BF16), wo_ref[sl, :])
    hsum = hf_ref[0].astype(F32) + hb_ref[0].astype(F32)
    yml = jax.nn.sigmoid(o_ref[0].astype(F32)) * hsum
    mix = mix + _dot(yml.astype(BF16), wo_ref[POOL_WIDTH:, :])
    x1 = x + mod[2:3] * _rmsnorm(mix, gpm_ref[...])
    x1_ref[0] = x1
    h2 = _rmsnorm(x1, gpf_ref[...]) * (1.0 + mod[4:5]) + mod[3:4]
    p1, p2, p3 = _split3(h2)
    h2_ref[0] = p1
    w1, w2, w3 = _split3(wrt_ref[...])
    ws = jnp.concatenate([w1, w2, w3], axis=0)
    r1 = _dot_nt(ws, p1)
    r2 = _dot_nt(ws[:128], p2)
    r3 = _dot_nt(w1, p3)
    lg_ref[0] = r1[0:64] + r1[64:128] + r1[128:192] + r2[0:64] + r2[64:128] + r3


def _outproj(x, u, hf, hb, o, mod, mod_row, pm, pinv, wp, ps, wo, gpm, gpf, wrt):
    b, t, _ = x.shape
    tm = TOKEN_TILE
    const = lambda *shape: pl.BlockSpec(shape, lambda i, j: (0,) * len(shape))
    tok = lambda w: pl.BlockSpec((1, tm, w), lambda i, j: (i, j, 0))
    sd = jax.ShapeDtypeStruct
    return pl.pallas_call(
        functools.partial(_outproj_kernel, tm=tm),
        grid=(b, t // tm),
        in_specs=[tok(D_MODEL),
                  pl.BlockSpec((1, t, 512), lambda i, j: (i, 0, 0)),
                  tok(512), tok(512), tok(512),
                  pl.BlockSpec((1, N_MOD, D_MODEL), lambda i, j: (mod_row(i), 0, 0)),
                  pl.BlockSpec((POOL_GROUPS, tm, t), lambda i, j: (0, j, 0)),
                  pl.BlockSpec((POOL_GROUPS, tm, 1), lambda i, j: (0, j, 0)),
                  const(POOL_GROUPS, POOL_GROUP_DIM, POOL_GROUP_DIM), const(1, POOL_WIDTH),
                  const(D_MODEL, D_MODEL), const(1, D_MODEL), const(1, D_MODEL),
                  const(N_EXPERTS, D_MODEL)],
        out_specs=[tok(D_MODEL), tok(D_MODEL),
                   pl.BlockSpec((1, N_EXPERTS, tm), lambda i, j: (i, 0, j))],
        out_shape=[sd((b, t, D_MODEL), F32), sd((b, t, D_MODEL), BF16), sd((b, N_EXPERTS, t), F32)],
        compiler_params=_params("arbitrary", "arbitrary"),
        name="outproj",
    )(x, u, hf, hb, o, mod, pm, pinv, wp, ps, wo, gpm, gpf, wrt)


def _router_kernel(lg_ref, br_ref, comb_ref):
    s = jax.nn.sigmoid(lg_ref[...])
    biased = s + br_ref[...]
    gidx = lax.broadcasted_iota(jnp.int32, s.shape, 0)
    jidx = lax.broadcasted_iota(jnp.int32, s.shape, 1)
    neg = -jnp.inf
    m1 = jnp.max(biased, axis=1, keepdims=True)
    i1 = jnp.min(jnp.where(biased == m1, jidx, GROUP_SIZE), axis=1, keepdims=True)
    m2 = jnp.max(jnp.where(jidx == i1, neg, biased), axis=1, keepdims=True)
    gscore = m1 + m2
    gi = lax.broadcasted_iota(jnp.int32, gscore.shape, 0)
    gmask = jnp.zeros(gscore.shape, F32)
    cur = gscore
    for _ in range(TOPK_GROUPS):
        mx = jnp.max(cur, axis=0, keepdims=True)
        ix = jnp.min(jnp.where(cur == mx, gi, N_EXPERT_GROUPS), axis=0, keepdims=True)
        hit = gi == ix
        gmask = jnp.where(hit, 1.0, gmask)
        cur = jnp.where(hit, neg, cur)
    cur = jnp.where(gmask > 0, biased, neg)
    eidx = gidx * GROUP_SIZE + jidx
    selmask = jnp.zeros(s.shape, F32)
    for _ in range(TOP_K):
        mx = jnp.max(jnp.max(cur, axis=1, keepdims=True), axis=0, keepdims=True)
        ix = jnp.where(cur == mx, eidx, N_EXPERTS)
        ix = jnp.min(jnp.min(ix, axis=1, keepdims=True), axis=0, keepdims=True)
        hit = eidx == ix
        selmask = jnp.where(hit, 1.0, selmask)
        cur = jnp.where(hit, neg, cur)
    sel = selmask * s
    tot = jnp.sum(jnp.sum(sel, axis=1, keepdims=True), axis=0, keepdims=True)
    comb_ref[...] = sel / tot * ROUTED_SCALE


def _router(logits_t, b_router):
    t = logits_t.shape[1]
    tl = 1024
    shp = (N_EXPERT_GROUPS, GROUP_SIZE, t)
    out = pl.pallas_call(
        _router_kernel,
        grid=(t // tl,),
        in_specs=[pl.BlockSpec((N_EXPERT_GROUPS, GROUP_SIZE, tl), lambda j: (0, 0, j)),
                  pl.BlockSpec((N_EXPERT_GROUPS, GROUP_SIZE, 1), lambda j: (0, 0, 0))],
        out_specs=pl.BlockSpec((N_EXPERT_GROUPS, GROUP_SIZE, tl), lambda j: (0, 0, j)),
        out_shape=jax.ShapeDtypeStruct(shp, F32),
        compiler_params=_params("arbitrary"),
        name="router",
    )(logits_t.reshape(shp), b_router.reshape(N_EXPERT_GROUPS, GROUP_SIZE, 1))
    return out.reshape(N_EXPERTS, t)


def _moe_kernel(h_ref, comb_ref, x1_ref, mod_ref, wgu_ref, wd_ref, wsg_ref, wsd_ref, gpo_ref,
                out_ref, acc_ref, *, ne):
    e = pl.program_id(1)
    h = h_ref[...]

    @pl.when(e == 0)
    def _():
        gs = _dot(h, wsg_ref[...])
        act = _silu(gs[:, :SHARED_DIM]) * gs[:, SHARED_DIM:]
        acc_ref[...] = _dot(act.astype(BF16), wsd_ref[...])

    comb3 = jnp.concatenate(_split3(comb_ref[...]), axis=1)
    sel_row = lax.broadcasted_iota(jnp.int32, (3 * LANES, EXPERT_DIM), 0) % LANES
    acts = []
    for k in range(ne):
        gu = _dot(h, wgu_ref[k].astype(BF16))
        onehot = (sel_row == e * ne + k).astype(BF16)
        cexp = _dot(comb3, onehot)
        acts.append((_silu(gu[:, :EXPERT_DIM]) * gu[:, EXPERT_DIM:] * cexp).astype(BF16))
    act = jnp.concatenate(acts, axis=1)
    acc_ref[...] += _dot(act, wd_ref[...].astype(BF16))

    @pl.when(e == pl.num_programs(1) - 1)
    def _():
        out_ref[...] = x1_ref[...] + mod_ref[0][5:6] * _rmsnorm(acc_ref[...], gpo_ref[...])


def _moe(h2, comb, x1, mod, mod_row, wgu, wd, wsg, wsd, gpo):
    n = h2.shape[0]
    tm = MOE_TOKEN_TILE
    ne = MOE_EXPERTS_PER_STEP
    const = lambda *shape: pl.BlockSpec(shape, lambda i, e: (0,) * len(shape))
    tok = lambda w: pl.BlockSpec((tm, w), lambda i, e: (i, 0))
    return pl.pallas_call(
        functools.partial(_moe_kernel, ne=ne),
        grid=(n // tm, N_EXPERTS // ne),
        in_specs=[tok(D_MODEL), tok(LANES), tok(D_MODEL),
                  pl.BlockSpec((1, N_MOD, D_MODEL), lambda i, e: (mod_row(i), 0, 0)),
                  pl.BlockSpec((ne, D_MODEL, 2 * EXPERT_DIM), lambda i, e: (e, 0, 0)),
                  pl.BlockSpec((ne * EXPERT_DIM, D_MODEL), lambda i, e: (e, 0)),
                  const(D_MODEL, 2 * SHARED_DIM), const(SHARED_DIM, D_MODEL), const(1, D_MODEL)],
        out_specs=tok(D_MODEL),
        out_shape=jax.ShapeDtypeStruct((n, D_MODEL), F32),
        scratch_shapes=[pltpu.VMEM((tm, D_MODEL), F32)],
        compiler_params=_params("arbitrary", "arbitrary"),
        name="moe",
    )(h2, comb, x1, mod, wgu, wd.reshape(N_EXPERTS * EXPERT_DIM, D_MODEL), wsg, wsd, gpo)


def _window_bounds(n, w):
    idx = np.arange(n)
    return np.clip(idx - w // 2, 0, n), np.clip(idx + w - w // 2, 0, n)


def _pool_operators(t, grid):
    mats, invs = [], []
    for w in POOL_WINDOWS:
        if grid:
            rlo, rhi = _window_bounds(t // GRID_W, w)
            clo, chi = _window_bounds(GRID_W, w)
            r = np.arange(t) // GRID_W
            c = np.arange(t) % GRID_W
            m = ((r[None, :] >= rlo[r][:, None]) & (r[None, :] < rhi[r][:, None])
                 & (c[None, :] >= clo[c][:, None]) & (c[None, :] < chi[c][:, None]))
            cnt = (rhi - rlo)[r] * (chi - clo)[c]
        else:
            lo, hi = _window_bounds(t, w)
            sidx = np.arange(t)
            m = (sidx[None, :] >= lo[:, None]) & (sidx[None, :] < hi[:, None])
            cnt = hi - lo
        mats.append(m.astype(np.float32))
        invs.append((1.0 / cnt.astype(np.float64)).astype(np.float32)[:, None])
    return jnp.asarray(np.stack(mats), BF16), jnp.asarray(np.stack(invs), F32)


def kernel(x_prompt, x_sample, state_C, state_n, state_m, c, c_ctx, w_ada, b_ada, g_pre_mix, w_in, b_gate,
           w_pool, pool_scale, w_out, g_post_mix, g_pre_ffn, w_router, b_router, w_expert_gu, w_expert_down,
           w_shared_gu, w_shared_down, g_post_ffn):
    b_ctx = x_prompt.shape[0]
    b_lat = x_sample.shape[0]
    nu = N_DIR * HEADS
    l = 0
    row = lambda a: a[l].reshape(1, -1).astype(F32)

    cvec = jnp.zeros((16, D_MODEL), F32).at[0].set(c_ctx.astype(F32)).at[1:1 + b_lat].set(c.astype(F32))
    mod = _mod_rows(cvec, w_ada[l], b_ada[l]).reshape(16, N_MOD, D_MODEL)

    w_in_l = w_in[l]
    p0 = POOL_WIDTH
    mw = MLSTM_WIDTH
    w_u, w_q, w_k, w_v, w_o = (w_in_l[:, lo:lo + 512] for lo in (0, p0, p0 + mw, p0 + 2 * mw, p0 + 3 * mw))
    wm = jnp.concatenate([w_u, w_k, w_o], axis=1).astype(BF16)
    wt = jnp.concatenate([w_q.T, w_v.T], axis=0).astype(BF16)
    wg_cols = w_in_l[:, p0 + 4 * mw:]
    wg = jnp.pad(wg_cols, ((0, 0), (0, LANES - GATE_COLS)))
    wgt = wg_cols.T
    bg = b_gate[l].reshape(GATE_COLS).astype(F32)
    bgr = jnp.pad(bg, (0, LANES - GATE_COLS)).reshape(1, LANES)
    bgc = bg.reshape(GATE_COLS, 1)
    wp = w_pool[l].astype(BF16)
    wo = w_out[l].astype(BF16)
    wrt = w_router[l].T
    wsg = w_shared_gu[l].astype(BF16)
    wsd = w_shared_down[l].astype(BF16)

    def mixer(x, mod_row, grid, s0, m0, emit_state):
        t = x.shape[1]
        u, k, o, qt, vt, gate, gate_t = _inproj(x.astype(F32), mod, mod_row, row(g_pre_mix), wm, wt, wg, wgt,
                                                bgr, bgc)
        outs = _mlstm(k, qt, vt, gate, gate_t, s0, m0, emit_state)
        hf, hb = outs[0], outs[1]
        pm, pinv = _pool_operators(t, grid)
        x1, h2, lg = _outproj(x.astype(F32), u, hf, hb, o, mod, mod_row, pm, pinv, wp, row(pool_scale), wo,
                              row(g_post_mix), row(g_pre_ffn), wrt)
        return x1, h2, lg, outs[2:]

    ctx_row = lambda i: 0
    lat_row = lambda i: i + 1
    x1c, h2c, lgc, (c_new, n_new, m_new) = mixer(x_prompt, ctx_row, False, None, None, True)
    s0 = jnp.concatenate(
        [jnp.swapaxes(state_C[:, l].reshape(b_lat, nu, HEAD_DIM, HEAD_DIM).astype(F32), -1, -2),
         jnp.broadcast_to(state_n[:, l].reshape(b_lat, nu, 1, HEAD_DIM).astype(F32),
                          (b_lat, nu, HEAD_DIM, HEAD_DIM))], axis=-2)
    m0 = jnp.broadcast_to(state_m[:, l].reshape(b_lat, nu, 1, 1).astype(F32), (b_lat, nu, 1, LANES))
    x1s, h2s, lgs, _ = mixer(x_sample, lat_row, True, s0, m0, False)

    tc = b_ctx * x_prompt.shape[1]
    ts = b_lat * x_sample.shape[1]
    lg_all = jnp.concatenate([lgc.transpose(1, 0, 2).reshape(N_EXPERTS, tc),
                              lgs.transpose(1, 0, 2).reshape(N_EXPERTS, ts)], axis=1)
    comb = _router(lg_all, b_router[l].astype(F32)).T
    comb = jnp.pad(comb, ((0, 0), (0, LANES - N_EXPERTS)))

    moe = functools.partial(_moe, wgu=w_expert_gu[l], wd=w_expert_down[l], wsg=wsg, wsd=wsd,
                            gpo=row(g_post_ffn))
    tiles_per_lat = x_sample.shape[1] // MOE_TOKEN_TILE
    yc = moe(h2c.reshape(tc, D_MODEL), comb[:tc], x1c.reshape(tc, D_MODEL), mod, ctx_row)
    ys = moe(h2s.reshape(ts, D_MODEL), comb[tc:], x1s.reshape(ts, D_MODEL), mod,
             lambda i: i // tiles_per_lat + 1)

    new_c = c_new.reshape(b_ctx, 1, N_DIR, HEADS, HEAD_DIM, HEAD_DIM)
    new_n = n_new.reshape(b_ctx, 1, N_DIR, HEADS, HEAD_DIM)
    new_m = m_new[..., 0].reshape(b_ctx, 1, N_DIR, HEADS)
    return (yc.reshape(x_prompt.shape), ys.reshape(x_sample.shape), new_c, new_n, new_m)
```

```python
import functools

import jax
import jax.numpy as jnp
import numpy as np
from jax import lax
from jax.experimental import pallas as pl
from jax.experimental.pallas import tpu as pltpu

F32 = jnp.float32
BF16 = jnp.bfloat16

D_MODEL = 1024
GRID_W = 64
POOL_WIDTH = 512
POOL_GROUPS = 4
POOL_GROUP_DIM = 128
POOL_WINDOWS = (2, 4, 8, 16)
HEADS = 4
HEAD_DIM = 128
MLSTM_WIDTH = HEADS * HEAD_DIM
N_DIR = 2
GATE_COLS = N_DIR * 2 * HEADS
N_EXPERTS = 64
TOP_K = 6
N_EXPERT_GROUPS = 8
GROUP_SIZE = N_EXPERTS // N_EXPERT_GROUPS
TOPK_GROUPS = 4
EXPERT_DIM = 256
SHARED_DIM = 256
ROUTED_SCALE = 2.5
N_MOD = 6
EPS = 1e-6
K_SCALE = HEAD_DIM ** -0.5

LANES = 128
CHUNK = 256
TOKEN_TILE = 256
MOE_TILE = 256
PACK_ROWS = 4
Y_PITCH = MOE_TILE + 8
VMEM_LIMIT = 56 * 1024 * 1024
EXPERTS_VMEM_LIMIT = 58 * 1024 * 1024


def _split3(x):
    p1 = x.astype(BF16)
    r1 = x - p1.astype(F32)
    p2 = r1.astype(BF16)
    p3 = (r1 - p2.astype(F32)).astype(BF16)
    return p1, p2, p3


def _split2(x):
    p1 = x.astype(BF16)
    p2 = (x - p1.astype(F32)).astype(BF16)
    return p1, p2


def _dot(a, b):
    return jnp.dot(a, b, preferred_element_type=F32)


def _dot_nt(a, b):
    return lax.dot_general(a, b, (((1,), (1,)), ((), ())), preferred_element_type=F32)


def _rmsnorm(x, g):
    return x * lax.rsqrt(jnp.mean(x * x, axis=-1, keepdims=True) + EPS) * g


def _silu(x):
    return x * jax.nn.sigmoid(x)


def _params(*sem):
    return pltpu.CompilerParams(dimension_semantics=sem, vmem_limit_bytes=VMEM_LIMIT)


def _mod_kernel(c_ref, w_ref, b_ref, o_ref):
    a = _silu(c_ref[...])
    a_stack = jnp.concatenate(_split3(a), axis=0)
    w1, w2 = _split2(w_ref[...])
    r1 = _dot(a_stack, w1)
    r2 = _dot(a_stack[:32], w2)
    o_ref[...] = (r1[0:16] + r1[16:32] + r1[32:48] + r2[0:16] + r2[16:32]) + b_ref[...]


def _mod_rows(cvec, w_ada, b_ada):
    n = N_MOD * D_MODEL
    tn = 1536
    return pl.pallas_call(
        _mod_kernel,
        grid=(n // tn,),
        in_specs=[pl.BlockSpec((16, D_MODEL), lambda j: (0, 0)),
                  pl.BlockSpec((D_MODEL, tn), lambda j: (0, j)),
                  pl.BlockSpec((1, tn), lambda j: (0, j))],
        out_specs=pl.BlockSpec((16, tn), lambda j: (0, j)),
        out_shape=jax.ShapeDtypeStruct((16, n), F32),
        compiler_params=_params("arbitrary"),
        name="mod",
    )(cvec, w_ada, b_ada.reshape(1, n))


def _inproj_kernel(x_ref, mod_ref, g_ref, wm_ref, wt_ref, wg_ref, wgt_ref, bgr_ref, bgc_ref,
                   u_ref, k_ref, o_ref, qt_ref, vt_ref, gate_ref, gatet_ref):
    x = x_ref[0]
    mod = mod_ref[0]
    h = _rmsnorm(x, g_ref[...]) * (1.0 + mod[1:2]) + mod[0:1]
    h1, h2, h3 = _split3(h)
    z = _dot(h1, wm_ref[...])
    u_ref[0] = z[:, 0:512].astype(BF16)
    k_ref[0] = (z[:, 512:1024] * K_SCALE).astype(BF16)
    o_ref[0] = z[:, 1024:1536].astype(BF16)
    zt = _dot_nt(wt_ref[...], h1)
    qt_ref[0] = zt[0:512].astype(BF16)
    vt_ref[0] = zt[512:1024].astype(BF16)
    tm = x.shape[0]
    wg1, wg2 = _split2(wg_ref[...])
    hs = jnp.concatenate([h1, h2, h3], axis=0)
    r1 = _dot(hs, wg1)
    r2 = _dot(hs[:2 * tm], wg2)
    gate_ref[0] = (r1[0:tm] + r1[tm:2 * tm] + r1[2 * tm:] + r2[0:tm] + r2[tm:]) + bgr_ref[...]
    wt1, wt2 = _split2(wgt_ref[...])
    wts = jnp.concatenate([wt1, wt2], axis=0)
    t1 = _dot_nt(wts, h1)
    t2 = _dot_nt(wts, h2)
    t3 = _dot_nt(wt1, h3)
    gatet_ref[0] = (t1[0:16] + t1[16:32] + t2[0:16] + t2[16:32] + t3) + bgc_ref[...]


def _inproj(x, mod, mod_row, g, wm, wt, wg, wgt, bgr, bgc):
    b, t, _ = x.shape
    tm = TOKEN_TILE
    const = lambda *shape: pl.BlockSpec(shape, lambda i, j: (0,) * len(shape))
    tok = lambda w: pl.BlockSpec((1, tm, w), lambda i, j: (i, j, 0))
    tok_t = lambda r: pl.BlockSpec((1, r, tm), lambda i, j: (i, 0, j))
    sd = jax.ShapeDtypeStruct
    return pl.pallas_call(
        _inproj_kernel,
        grid=(b, t // tm),
        in_specs=[tok(D_MODEL),
                  pl.BlockSpec((1, N_MOD, D_MODEL), lambda i, j: (mod_row(i), 0, 0)),
                  const(1, D_MODEL), const(D_MODEL, 1536), const(1024, D_MODEL),
                  const(D_MODEL, LANES), const(16, D_MODEL), const(1, LANES), const(16, 1)],
        out_specs=[tok(512), tok(512), tok(512), tok_t(512), tok_t(512), tok(LANES), tok_t(16)],
        out_shape=[sd((b, t, 512), BF16), sd((b, t, 512), BF16), sd((b, t, 512), BF16),
                   sd((b, 512, t), BF16), sd((b, 512, t), BF16), sd((b, t, LANES), F32),
                   sd((b, 16, t), F32)],
        compiler_params=_params("arbitrary", "arbitrary"),
        name="inproj",
    )(x, mod, g, wm, wt, wg, wgt, bgr, bgc)


def _log_sigmoid(x):
    return jnp.minimum(x, 0.0) - jnp.log1p(jnp.exp(-jnp.abs(x)))


def _scan_unit(st, k, qt, vt, u_col, u_row, b_row, btot, mask, s_prev, m_prev, use_state):
    dh = HEAD_DIM
    n = st.shape[0]
    ub = jnp.where(mask, jnp.broadcast_to(u_col, (n, n)), -jnp.inf)
    z = jnp.maximum(m_prev, jnp.max(ub, axis=0, keepdims=True))
    p = (jnp.exp(ub - z) * st).astype(BF16)
    ones = jnp.ones((dh, n), BF16)
    tot = _dot(jnp.concatenate([vt, ones], axis=0), p)
    if use_state:
        tot = tot + jnp.exp(m_prev - z) * _dot(s_prev.astype(BF16), qt)
    floor = jnp.exp(-(b_row + z))
    h_t = tot[:dh] / jnp.maximum(jnp.abs(tot[dh:]), floor)
    g_row = btot + u_row
    m_new = jnp.maximum(btot + m_prev, jnp.max(g_row, axis=-1, keepdims=True))
    w_row = jnp.exp(g_row - m_new)
    vw = jnp.concatenate([(vt.astype(F32) * w_row).astype(BF16),
                          jnp.broadcast_to(w_row, (dh, n)).astype(BF16)], axis=0)
    s_new = jnp.exp(btot + m_prev - m_new) * s_prev + _dot(vw, k)
    return h_t.T, s_new, m_new


def _mlstm_kernel(*refs, nc, zero_init, emit_state):
    it = iter(refs)
    fwd_refs = tuple(next(it) for _ in range(5))
    bwd_refs = tuple(next(it) for _ in range(5)) if nc > 1 else fwd_refs
    if not zero_init:
        s0_ref, m0_ref = next(it), next(it)
    h_refs = (next(it), next(it))
    if emit_state:
        c_out, n_out, m_out = next(it), next(it), next(it)
    s_scr, m_scr = next(it), next(it)

    j = pl.program_id(1)
    n = CHUNK
    dh = HEAD_DIM

    @pl.when(j == 0)
    def _():
        if zero_init:
            s_scr[...] = jnp.zeros_like(s_scr)
            m_scr[...] = jnp.zeros_like(m_scr)
        else:
            s_scr[...] = s0_ref[0]
            m_scr[...] = m0_ref[0]

    rows = lax.broadcasted_iota(jnp.int32, (n, n), 0)
    cols = lax.broadcasted_iota(jnp.int32, (n, n), 1)
    le = rows <= cols
    ge = rows >= cols
    tri_le = le.astype(BF16)
    tri_ge = ge.astype(BF16)
    use_state = not (zero_init and nc == 1)

    def gate_terms(d):
        g_ref, gt_ref = (fwd_refs, bwd_refs)[d][3:5]
        gate = g_ref[0]
        gate_t = gt_ref[0]
        lf = _log_sigmoid(gate)
        lf_t = _log_sigmoid(gate_t)
        tri_c, tri_r = (tri_ge, tri_le) if d == 0 else (tri_le, tri_ge)
        bc = _dot(tri_c, jnp.concatenate(_split3(lf), axis=1))
        b_cols = bc[:, 0:128] + bc[:, 128:256] + bc[:, 256:384]
        br = _dot(jnp.concatenate(_split3(lf_t), axis=0), tri_r)
        b_rows = br[0:16] + br[16:32] + br[32:48]
        return gate, gate_t, b_cols, b_rows, jnp.sum(lf_t, axis=-1, keepdims=True)

    terms = [gate_terms(0), gate_terms(1)]
    hs = ([], [])
    for hd in range(HEADS):
        hsl = slice(hd * dh, (hd + 1) * dh)
        st = None
        for d in range(N_DIR):
            k_ref, qt_ref, vt_ref = (fwd_refs, bwd_refs)[d][0:3]
            gate, gate_t, b_cols, b_rows, tot_rows = terms[d]
            ci = d * 8 + hd
            cf = d * 8 + 4 + hd
            unit = d * HEADS + hd
            k = k_ref[0, :, hsl]
            qt = qt_ref[0, hsl, :]
            if st is None or nc > 1:
                st = _dot(k, qt)
            mask = le if d == 0 else ge
            h, s_new, m_new = _scan_unit(
                st, k, qt, vt_ref[0, hsl, :],
                gate[:, ci:ci + 1] - b_cols[:, cf:cf + 1],
                gate_t[ci:ci + 1, :] - b_rows[cf:cf + 1, :],
                b_rows[cf:cf + 1, :], tot_rows[cf:cf + 1, :],
                mask, s_scr[unit], m_scr[unit][:, 0:1], use_state)
            s_scr[unit] = s_new
            m_scr[unit] = jnp.broadcast_to(m_new, (1, LANES))
            hs[d].append(h)
    for d in range(N_DIR):
        h_refs[d][0] = jnp.concatenate(hs[d], axis=1).astype(BF16)

    if emit_state:
        @pl.when(j == nc - 1)
        def _():
            for unit in range(N_DIR * HEADS):
                s = s_scr[unit]
                c_out[0, unit] = s[:dh].T
                n_out[0, unit] = s[dh:dh + 1]
                m_out[0, unit] = m_scr[unit]


def _mlstm(k, qt, vt, gate, gate_t, s0, m0, emit_state):
    b, t, _ = k.shape
    nc = t // CHUNK
    zero_init = s0 is None
    nu = N_DIR * HEADS
    fwd = lambda w: pl.BlockSpec((1, CHUNK, w), lambda i, j: (i, j, 0))
    bwd = lambda w: pl.BlockSpec((1, CHUNK, w), lambda i, j: (i, nc - 1 - j, 0))
    fwd_t = lambda r: pl.BlockSpec((1, r, CHUNK), lambda i, j: (i, 0, j))
    bwd_t = lambda r: pl.BlockSpec((1, r, CHUNK), lambda i, j: (i, 0, nc - 1 - j))
    args = [k, qt, vt, gate, gate_t]
    in_specs = [fwd(512), fwd_t(512), fwd_t(512), fwd(LANES), fwd_t(16)]
    if nc > 1:
        args += [k, qt, vt, gate, gate_t]
        in_specs += [bwd(512), bwd_t(512), bwd_t(512), bwd(LANES), bwd_t(16)]
    if not zero_init:
        args += [s0, m0]
        in_specs += [pl.BlockSpec((1, nu, 2 * HEAD_DIM, HEAD_DIM), lambda i, j: (i, 0, 0, 0)),
                     pl.BlockSpec((1, nu, 1, LANES), lambda i, j: (i, 0, 0, 0))]
    sd = jax.ShapeDtypeStruct
    out_shape = [sd((b, t, 512), BF16), sd((b, t, 512), BF16)]
    out_specs = [fwd(512), bwd(512)]
    if emit_state:
        out_shape += [sd((b, nu, HEAD_DIM, HEAD_DIM), F32), sd((b, nu, 1, HEAD_DIM), F32),
                      sd((b, nu, 1, LANES), F32)]
        out_specs += [pl.BlockSpec((1, nu, HEAD_DIM, HEAD_DIM), lambda i, j: (i, 0, 0, 0)),
                      pl.BlockSpec((1, nu, 1, HEAD_DIM), lambda i, j: (i, 0, 0, 0)),
                      pl.BlockSpec((1, nu, 1, LANES), lambda i, j: (i, 0, 0, 0))]
    return pl.pallas_call(
        functools.partial(_mlstm_kernel, nc=nc, zero_init=zero_init, emit_state=emit_state),
        grid=(b, nc),
        in_specs=in_specs,
        out_specs=out_specs,
        out_shape=out_shape,
        scratch_shapes=[pltpu.VMEM((nu, 2 * HEAD_DIM, HEAD_DIM), F32),
                        pltpu.VMEM((nu, 1, LANES), F32)],
        compiler_params=_params("arbitrary", "arbitrary"),
        name="mlstm",
    )(*args)


def _outproj_kernel(x_ref, u_ref, hf_ref, hb_ref, o_ref, mod_ref, pm_ref, pinv_ref, wp_ref, ps_ref,
                    wo_ref, gpm_ref, gpf_ref, wrt_ref, x1_ref, h2_ref, xp_ref, lg_ref, *, tm):
    i = pl.program_id(1)
    x = x_ref[0]
    mod = mod_ref[0]
    row0 = pl.multiple_of(i * tm, tm)
    u_full = u_ref[0]
    u_tile = u_ref[0, pl.ds(row0, tm), :].astype(F32)
    mix = jnp.zeros((tm, D_MODEL), F32)
    for g in range(POOL_GROUPS):
        sl = slice(g * POOL_GROUP_DIM, (g + 1) * POOL_GROUP_DIM)
        box = _dot(pm_ref[g], u_full[:, sl])
        diff = box * pinv_ref[g] - u_tile[:, sl]
        yp = _dot(diff.astype(BF16), wp_ref[g]) * ps_ref[:, sl]
        mix = mix + _dot(yp.astype(BF16), wo_ref[sl, :])
    hsum = hf_ref[0].astype(F32) + hb_ref[0].astype(F32)
    yml = jax.nn.sigmoid(o_ref[0].astype(F32)) * hsum
    mix = mix + _dot(yml.astype(BF16), wo_ref[POOL_WIDTH:, :])
    x1 = x + mod[2:3] * _rmsnorm(mix, gpm_ref[...])
    x1_ref[0] = x1
    h2 = _rmsnorm(x1, gpf_ref[...]) * (1.0 + mod[4:5]) + mod[3:4]
    p1, p2, p3 = _split3(h2)
    h2_ref[0] = p1
    pf = p1.astype(F32)
    half = D_MODEL // 2
    word = ((pltpu.bitcast(pf[:, half:], jnp.uint32) & jnp.uint32(0xFFFF0000))
            | (pltpu.bitcast(pf[:, :half], jnp.uint32) >> 16))
    for cc in range(PACK_ROWS):
        xp_ref[0, :, cc, :] = word[:, cc * LANES:(cc + 1) * LANES]
    w1, w2, w3 = _split3(wrt_ref[...])
    ws = jnp.concatenate([w1, w2, w3], axis=0)
    r1 = _dot_nt(ws, p1)
    r2 = _dot_nt(ws[:128], p2)
    r3 = _dot_nt(w1, p3)
    lg_ref[0] = r1[0:64] + r1[64:128] + r1[128:192] + r2[0:64] + r2[64:128] + r3


def _outproj(x, u, hf, hb, o, mod, mod_row, pm, pinv, wp, ps, wo, gpm, gpf, wrt):
    b, t, _ = x.shape
    tm = TOKEN_TILE
    const = lambda *shape: pl.BlockSpec(shape, lambda i, j: (0,) * len(shape))
    tok = lambda w: pl.BlockSpec((1, tm, w), lambda i, j: (i, j, 0))
    sd = jax.ShapeDtypeStruct
    return pl.pallas_call(
        functools.partial(_outproj_kernel, tm=tm),
        grid=(b, t // tm),
        in_specs=[tok(D_MODEL),
                  pl.BlockSpec((1, t, 512), lambda i, j: (i, 0, 0)),
                  tok(512), tok(512), tok(512),
                  pl.BlockSpec((1, N_MOD, D_MODEL), lambda i, j: (mod_row(i), 0, 0)),
                  pl.BlockSpec((POOL_GROUPS, tm, t), lambda i, j: (0, j, 0)),
                  pl.BlockSpec((POOL_GROUPS, tm, 1), lambda i, j: (0, j, 0)),
                  const(POOL_GROUPS, POOL_GROUP_DIM, POOL_GROUP_DIM), const(1, POOL_WIDTH),
                  const(D_MODEL, D_MODEL), const(1, D_MODEL), const(1, D_MODEL),
                  const(N_EXPERTS, D_MODEL)],
        out_specs=[tok(D_MODEL), tok(D_MODEL),
                   pl.BlockSpec((1, tm, PACK_ROWS, LANES), lambda i, j: (i, j, 0, 0)),
                   pl.BlockSpec((1, N_EXPERTS, tm), lambda i, j: (i, 0, j))],
        out_shape=[sd((b, t, D_MODEL), F32), sd((b, t, D_MODEL), BF16),
                   sd((b, t, PACK_ROWS, LANES), jnp.uint32), sd((b, N_EXPERTS, t), F32)],
        compiler_params=_params("arbitrary", "arbitrary"),
        name="outproj",
    )(x, u, hf, hb, o, mod, pm, pinv, wp, ps, wo, gpm, gpf, wrt)


def _router_kernel(lg_ref, br_ref, comb_ref, sel_ref):
    s = jax.nn.sigmoid(lg_ref[...])
    biased = s + br_ref[...]
    gidx = lax.broadcasted_iota(jnp.int32, s.shape, 0)
    jidx = lax.broadcasted_iota(jnp.int32, s.shape, 1)
    neg = -jnp.inf
    m1 = jnp.max(biased, axis=1, keepdims=True)
    i1 = jnp.min(jnp.where(biased == m1, jidx, GROUP_SIZE), axis=1, keepdims=True)
    m2 = jnp.max(jnp.where(jidx == i1, neg, biased), axis=1, keepdims=True)
    gscore = m1 + m2
    gi = lax.broadcasted_iota(jnp.int32, gscore.shape, 0)
    gmask = jnp.zeros(gscore.shape, F32)
    cur = gscore
    for _ in range(TOPK_GROUPS):
        mx = jnp.max(cur, axis=0, keepdims=True)
        ix = jnp.min(jnp.where(cur == mx, gi, N_EXPERT_GROUPS), axis=0, keepdims=True)
        hit = gi == ix
        gmask = jnp.where(hit, 1.0, gmask)
        cur = jnp.where(hit, neg, cur)
    cur = jnp.where(gmask > 0, biased, neg)
    eidx = gidx * GROUP_SIZE + jidx
    selmask = jnp.zeros(s.shape, F32)
    for _ in range(TOP_K):
        mx = jnp.max(jnp.max(cur, axis=1, keepdims=True), axis=0, keepdims=True)
        ix = jnp.where(cur == mx, eidx, N_EXPERTS)
        ix = jnp.min(jnp.min(ix, axis=1, keepdims=True), axis=0, keepdims=True)
        hit = eidx == ix
        selmask = jnp.where(hit, 1.0, selmask)
        cur = jnp.where(hit, neg, cur)
    sel = selmask * s
    tot = jnp.sum(jnp.sum(sel, axis=1, keepdims=True), axis=0, keepdims=True)
    comb_ref[...] = sel / tot * ROUTED_SCALE
    sel_ref[...] = selmask


def _router(logits_t, b_router):
    t = logits_t.shape[1]
    tl = 1024
    shp = (N_EXPERT_GROUPS, GROUP_SIZE, t)
    blk = pl.BlockSpec((N_EXPERT_GROUPS, GROUP_SIZE, tl), lambda j: (0, 0, j))
    comb, sel = pl.pallas_call(
        _router_kernel,
        grid=(t // tl,),
        in_specs=[blk, pl.BlockSpec((N_EXPERT_GROUPS, GROUP_SIZE, 1), lambda j: (0, 0, 0))],
        out_specs=[blk, blk],
        out_shape=[jax.ShapeDtypeStruct(shp, F32), jax.ShapeDtypeStruct(shp, F32)],
        compiler_params=_params("arbitrary"),
        name="router",
    )(logits_t.reshape(shp), b_router.reshape(N_EXPERT_GROUPS, GROUP_SIZE, 1))
    return comb.reshape(N_EXPERTS, t), sel.reshape(N_EXPERTS, t)


def _plan_kernel(sel_ref, pos_ref, meta_ref, emeta_ref, *, n_meta):
    t = sel_ref.shape[1]
    tm = float(MOE_TILE)
    sel = sel_ref[...]
    selb = sel.astype(BF16)
    blk = 256
    rr = lax.broadcasted_iota(jnp.int32, (blk, blk), 0)
    cc = lax.broadcasted_iota(jnp.int32, (blk, blk), 1)
    before = (rr < cc).astype(BF16)
    carry = jnp.zeros((N_EXPERTS, 1), F32)
    ranks = []
    for b in range(t // blk):
        sb = selb[:, b * blk:(b + 1) * blk]
        ranks.append(_dot(sb, before) + carry)
        carry = carry + jnp.sum(sel[:, b * blk:(b + 1) * blk], axis=1, keepdims=True)
    rank = jnp.concatenate(ranks, axis=1)
    cnt = carry
    ntile = jnp.floor((cnt + (tm - 1.0)) * (1.0 / tm))
    er = lax.broadcasted_iota(jnp.int32, (N_EXPERTS, N_EXPERTS), 0)
    ec = lax.broadcasted_iota(jnp.int32, (N_EXPERTS, N_EXPERTS), 1)
    below = (ec < er).astype(BF16)
    tstart = _dot(below, jnp.broadcast_to(ntile, (N_EXPERTS, LANES)).astype(BF16))[:, 0:1]
    pos = tstart * tm + rank
    erank = _dot(below, selb)
    rows = []
    for k in range(TOP_K):
        hit = (sel > 0.0) & (erank == float(k))
        rows.append(jnp.sum(jnp.where(hit, pos, 0.0), axis=0, keepdims=True))
    rows += [jnp.zeros((1, t), F32)] * (8 - TOP_K)
    pos_ref[...] = jnp.concatenate(rows, axis=0).astype(jnp.int32)

    tau = lax.broadcasted_iota(jnp.int32, (N_EXPERTS, n_meta), 1).astype(F32)
    eidx = lax.broadcasted_iota(jnp.int32, (N_EXPERTS, n_meta), 0).astype(F32)
    te = jnp.sum(((tstart + ntile) <= tau).astype(F32), axis=0, keepdims=True)
    te = jnp.minimum(te, float(N_EXPERTS - 1))
    onehot = eidx == te
    cnt_t = jnp.sum(jnp.where(onehot, cnt, 0.0), axis=0, keepdims=True)
    ts_t = jnp.sum(jnp.where(onehot, tstart, 0.0), axis=0, keepdims=True)
    tr = jnp.clip(cnt_t - (tau[0:1] - ts_t) * tm, 0.0, tm)
    tf = jnp.where((tau[0:1] == ts_t) & (tr > 0.0), 1.0, 0.0)
    meta_ref[...] = jnp.concatenate([te, tr, tf] + [jnp.zeros((1, n_meta), F32)] * 5, axis=0).astype(jnp.int32)

    eye = (lax.broadcasted_iota(jnp.int32, (N_EXPERTS, LANES), 0)
           == lax.broadcasted_iota(jnp.int32, (N_EXPERTS, LANES), 1))
    as_row = lambda col: jnp.sum(jnp.where(eye, col, 0.0), axis=0, keepdims=True)
    emeta_ref[...] = jnp.concatenate([as_row(cnt), as_row(tstart), as_row(ntile)]
                                     + [jnp.zeros((1, LANES), F32)] * 5, axis=0).astype(jnp.int32)


def _plan(sel, n_meta):
    t = sel.shape[1]
    sd = jax.ShapeDtypeStruct
    return pl.pallas_call(
        functools.partial(_plan_kernel, n_meta=n_meta),
        out_shape=[sd((8, t), jnp.int32), sd((8, n_meta), jnp.int32), sd((8, LANES), jnp.int32)],
        compiler_params=pltpu.CompilerParams(vmem_limit_bytes=VMEM_LIMIT),
        name="plan",
    )(sel)


def _invert_kernel(pos_ref, emeta_ref, tbl_ref, *, n_tokens):
    def pad_expert(e, carry):
        first = emeta_ref[1, e] * MOE_TILE
        def put(i, c):
            tbl_ref[i] = n_tokens
            return c
        lax.fori_loop(first + emeta_ref[0, e], first + emeta_ref[2, e] * MOE_TILE, put, 0)
        return carry
    lax.fori_loop(0, N_EXPERTS, pad_expert, 0)
    last = N_EXPERTS - 1
    def put_tail(i, c):
        tbl_ref[i] = n_tokens
        return c
    lax.fori_loop((emeta_ref[1, last] + emeta_ref[2, last]) * MOE_TILE, tbl_ref.shape[0], put_tail, 0)

    unroll = 4
    def scatter(i, carry):
        for u in range(unroll):
            tok = i * unroll + u
            for k in range(TOP_K):
                tbl_ref[pos_ref[k, tok]] = tok
        return carry
    lax.fori_loop(0, n_tokens // unroll, scatter, 0)


def _invert(pos, emeta, n_slots):
    n_tokens = pos.shape[1]
    smem = pl.BlockSpec(memory_space=pltpu.SMEM)
    return pl.pallas_call(
        functools.partial(_invert_kernel, n_tokens=n_tokens),
        in_specs=[smem, smem],
        out_specs=smem,
        out_shape=jax.ShapeDtypeStruct((n_slots,), jnp.int32),
        name="invert",
    )(pos, emeta)


def _experts_kernel(te_ref, tr_ref, tf_ref, tbl_ref, xp_ref, comb_ref, wgu_ref, wd_ref, acc_out,
                    acc, stage, cstage, ybuf, wgu_b, wd_b, sem):
    tau = pl.program_id(0)
    e = te_ref[tau]
    rows = tr_ref[tau]
    tm = MOE_TILE

    @pl.when(tau == 0)
    def _():
        acc[...] = jnp.zeros_like(acc)
        stage[...] = jnp.zeros_like(stage)
        cstage[...] = jnp.zeros_like(cstage)

    @pl.when(tf_ref[tau] == 1)
    def _():
        wgu_b[...] = wgu_ref[0].astype(BF16)
        wd_b[...] = wd_ref[0].astype(BF16)

    @pl.when(rows > 0)
    def _():
        base = tau * tm
        gu_n = 8
        def gather(i, c):
            for u in range(gu_n):
                j = i * gu_n + u
                tok = tbl_ref[base + j]
                stage[pl.ds(pl.multiple_of(j * PACK_ROWS, PACK_ROWS), PACK_ROWS), :] = xp_ref[tok]
                cstage[pl.ds(j, 1), :] = comb_ref[pl.ds(tok, 1), :]
            return c
        lax.fori_loop(0, lax.shift_right_logical(rows + (gu_n - 1), 3), gather, 0)

        lo, hi = [], []
        for cc in range(PACK_ROWS):
            w = stage[pl.ds(cc, tm, stride=PACK_ROWS), :]
            lo.append(pltpu.bitcast(w << 16, F32))
            hi.append(pltpu.bitcast(w & jnp.uint32(0xFFFF0000), F32))
        xb = jnp.concatenate(lo + hi, axis=1).astype(BF16)
        gu = _dot(xb, wgu_b[...])
        comb3 = jnp.concatenate(_split3(cstage[...]), axis=1)
        sel_row = lax.broadcasted_iota(jnp.int32, (3 * LANES, EXPERT_DIM), 0) % LANES
        cexp = _dot(comb3, (sel_row == e).astype(BF16))
        act = (_silu(gu[:, :EXPERT_DIM]) * gu[:, EXPERT_DIM:] * cexp).astype(BF16)
        y = _dot(act, wd_b[...])
        for cc in range(D_MODEL // LANES):
            ybuf[cc * Y_PITCH:cc * Y_PITCH + tm, :] = y[:, cc * LANES:(cc + 1) * LANES]

        sc_n = 16
        def scatter(i, c):
            toks = [tbl_ref[base + i * sc_n + u] for u in range(sc_n)]
            olds = [acc[toks[u]] for u in range(sc_n)]
            news = [olds[u] + ybuf[pl.ds(i * sc_n + u, D_MODEL // LANES, stride=Y_PITCH), :] for u in range(sc_n)]
            for u in range(sc_n):
                acc[toks[u]] = news[u]
            return c
        lax.fori_loop(0, lax.shift_right_logical(rows + (sc_n - 1), 4), scatter, 0)

    @pl.when(tau == pl.num_programs(0) - 1)
    def _():
        cp = pltpu.make_async_copy(acc, acc_out, sem)
        cp.start()
        cp.wait()


def _experts(te, tr, tf, tbl, xp, comb, wgu, wd):
    n_tok = xp.shape[0]
    n_tiles = te.shape[0]
    tm = MOE_TILE
    grid_spec = pltpu.PrefetchScalarGridSpec(
        num_scalar_prefetch=4,
        grid=(n_tiles,),
        in_specs=[pl.BlockSpec((n_tok, PACK_ROWS, LANES), lambda t, *_: (0, 0, 0), pipeline_mode=pl.Buffered(1)),
                  pl.BlockSpec((n_tok, LANES), lambda t, *_: (0, 0), pipeline_mode=pl.Buffered(1)),
                  pl.BlockSpec((1, D_MODEL, 2 * EXPERT_DIM), lambda t, te, tr, tf, tbl: (te[t], 0, 0)),
                  pl.BlockSpec((1, EXPERT_DIM, D_MODEL), lambda t, te, tr, tf, tbl: (te[t], 0, 0))],
        out_specs=pl.BlockSpec(memory_space=pl.ANY),
        scratch_shapes=[pltpu.VMEM((n_tok, D_MODEL // LANES, LANES), F32),
                        pltpu.VMEM((tm * PACK_ROWS, LANES), jnp.uint32),
                        pltpu.VMEM((tm, LANES), F32),
                        pltpu.VMEM((D_MODEL // LANES * Y_PITCH, LANES), F32),
                        pltpu.VMEM((D_MODEL, 2 * EXPERT_DIM), BF16),
                        pltpu.VMEM((EXPERT_DIM, D_MODEL), BF16),
                        pltpu.SemaphoreType.DMA],
    )
    return pl.pallas_call(
        _experts_kernel,
        grid_spec=grid_spec,
        out_shape=jax.ShapeDtypeStruct((n_tok, D_MODEL // LANES, LANES), F32),
        compiler_params=pltpu.CompilerParams(dimension_semantics=("arbitrary",),
                                             vmem_limit_bytes=EXPERTS_VMEM_LIMIT),
        name="experts",
    )(te, tr, tf, tbl, xp, comb, wgu, wd)


def _final_kernel(acc_ref, h_ref, x1_ref, mod_ref, wsg_ref, wsd_ref, gpo_ref, out_ref):
    routed = jnp.concatenate([acc_ref[:, cc, :] for cc in range(D_MODEL // LANES)], axis=1)
    gs = _dot(h_ref[...], wsg_ref[...])
    act = _silu(gs[:, :SHARED_DIM]) * gs[:, SHARED_DIM:]
    f = routed + _dot(act.astype(BF16), wsd_ref[...])
    out_ref[...] = x1_ref[...] + mod_ref[0][5:6] * _rmsnorm(f, gpo_ref[...])


def _final(acc, tile0, h2, x1, mod, mod_row, wsg, wsd, gpo):
    n = h2.shape[0]
    tm = TOKEN_TILE
    const = lambda *shape: pl.BlockSpec(shape, lambda i: (0,) * len(shape))
    tok = lambda w: pl.BlockSpec((tm, w), lambda i: (i, 0))
    return pl.pallas_call(
        _final_kernel,
        grid=(n // tm,),
        in_specs=[pl.BlockSpec((tm, D_MODEL // LANES, LANES), lambda i: (i + tile0, 0, 0)),
                  tok(D_MODEL), tok(D_MODEL),
                  pl.BlockSpec((1, N_MOD, D_MODEL), lambda i: (mod_row(i), 0, 0)),
                  const(D_MODEL, 2 * SHARED_DIM), const(SHARED_DIM, D_MODEL), const(1, D_MODEL)],
        out_specs=tok(D_MODEL),
        out_shape=jax.ShapeDtypeStruct((n, D_MODEL), F32),
        compiler_params=_params("arbitrary"),
        name="final",
    )(acc, h2, x1, mod, wsg, wsd, gpo)


def _window_bounds(n, w):
    idx = np.arange(n)
    return np.clip(idx - w // 2, 0, n), np.clip(idx + w - w // 2, 0, n)


def _pool_operators(t, grid):
    mats, invs = [], []
    for w in POOL_WINDOWS:
        if grid:
            rlo, rhi = _window_bounds(t // GRID_W, w)
            clo, chi = _window_bounds(GRID_W, w)
            r = np.arange(t) // GRID_W
            c = np.arange(t) % GRID_W
            m = ((r[None, :] >= rlo[r][:, None]) & (r[None, :] < rhi[r][:, None])
                 & (c[None, :] >= clo[c][:, None]) & (c[None, :] < chi[c][:, None]))
            cnt = (rhi - rlo)[r] * (chi - clo)[c]
        else:
            lo, hi = _window_bounds(t, w)
            sidx = np.arange(t)
            m = (sidx[None, :] >= lo[:, None]) & (sidx[None, :] < hi[:, None])
            cnt = hi - lo
        mats.append(m.astype(np.float32))
        invs.append((1.0 / cnt.astype(np.float64)).astype(np.float32)[:, None])
    return jnp.asarray(np.stack(mats), BF16), jnp.asarray(np.stack(invs), F32)


def kernel(x_prompt, x_sample, state_C, state_n, state_m, c, c_ctx, w_ada, b_ada, g_pre_mix, w_in, b_gate,
           w_pool, pool_scale, w_out, g_post_mix, g_pre_ffn, w_router, b_router, w_expert_gu, w_expert_down,
           w_shared_gu, w_shared_down, g_post_ffn):
    b_ctx = x_prompt.shape[0]
    b_lat = x_sample.shape[0]
    nu = N_DIR * HEADS
    l = 0
    row = lambda a: a[l].reshape(1, -1).astype(F32)

    cvec = jnp.zeros((16, D_MODEL), F32).at[0].set(c_ctx.astype(F32)).at[1:1 + b_lat].set(c.astype(F32))
    mod = _mod_rows(cvec, w_ada[l], b_ada[l]).reshape(16, N_MOD, D_MODEL)

    w_in_l = w_in[l]
    p0 = POOL_WIDTH
    mw = MLSTM_WIDTH
    w_u, w_q, w_k, w_v, w_o = (w_in_l[:, lo:lo + 512] for lo in (0, p0, p0 + mw, p0 + 2 * mw, p0 + 3 * mw))
    wm = jnp.concatenate([w_u, w_k, w_o], axis=1).astype(BF16)
    wt = jnp.concatenate([w_q.T, w_v.T], axis=0).astype(BF16)
    wg_cols = w_in_l[:, p0 + 4 * mw:]
    wg = jnp.pad(wg_cols, ((0, 0), (0, LANES - GATE_COLS)))
    wgt = wg_cols.T
    bg = b_gate[l].reshape(GATE_COLS).astype(F32)
    bgr = jnp.pad(bg, (0, LANES - GATE_COLS)).reshape(1, LANES)
    bgc = bg.reshape(GATE_COLS, 1)
    wp = w_pool[l].astype(BF16)
    wo = w_out[l].astype(BF16)
    wrt = w_router[l].T
    wsg = w_shared_gu[l].astype(BF16)
    wsd = w_shared_down[l].astype(BF16)

    def mixer(x, mod_row, grid, s0, m0, emit_state):
        t = x.shape[1]
        u, k, o, qt, vt, gate, gate_t = _inproj(x.astype(F32), mod, mod_row, row(g_pre_mix), wm, wt, wg, wgt,
                                                bgr, bgc)
        outs = _mlstm(k, qt, vt, gate, gate_t, s0, m0, emit_state)
        hf, hb = outs[0], outs[1]
        pm, pinv = _pool_operators(t, grid)
        x1, h2, xp, lg = _outproj(x.astype(F32), u, hf, hb, o, mod, mod_row, pm, pinv, wp, row(pool_scale), wo,
                                  row(g_post_mix), row(g_pre_ffn), wrt)
        return x1, h2, xp, lg, outs[2:]

    ctx_row = lambda i: 0
    lat_row = lambda i: i + 1
    x1c, h2c, xpc, lgc, (c_new, n_new, m_new) = mixer(x_prompt, ctx_row, False, None, None, True)
    s0 = jnp.concatenate(
        [jnp.swapaxes(state_C[:, l].reshape(b_lat, nu, HEAD_DIM, HEAD_DIM).astype(F32), -1, -2),
         jnp.broadcast_to(state_n[:, l].reshape(b_lat, nu, 1, HEAD_DIM).astype(F32),
                          (b_lat, nu, HEAD_DIM, HEAD_DIM))], axis=-2)
    m0 = jnp.broadcast_to(state_m[:, l].reshape(b_lat, nu, 1, 1).astype(F32), (b_lat, nu, 1, LANES))
    x1s, h2s, xps, lgs, _ = mixer(x_sample, lat_row, True, s0, m0, False)

    tc = b_ctx * x_prompt.shape[1]
    ts = b_lat * x_sample.shape[1]
    n_tok = tc + ts
    lg_all = jnp.concatenate([lgc.transpose(1, 0, 2).reshape(N_EXPERTS, tc),
                              lgs.transpose(1, 0, 2).reshape(N_EXPERTS, ts)], axis=1)
    comb, sel = _router(lg_all, b_router[l].astype(F32))
    n_tiles = n_tok * TOP_K // MOE_TILE + N_EXPERTS
    n_meta = -(-n_tiles // LANES) * LANES
    pos, meta, emeta = _plan(sel, n_meta)
    tbl = _invert(pos, emeta, n_tiles * MOE_TILE)

    pad_rows = 8
    xp_all = jnp.concatenate([xpc.reshape(tc, PACK_ROWS, LANES), xps.reshape(ts, PACK_ROWS, LANES),
                              jnp.zeros((pad_rows, PACK_ROWS, LANES), jnp.uint32)], axis=0)
    comb_tok = jnp.pad(comb.T, ((0, pad_rows), (0, LANES - N_EXPERTS)))
    acc = _experts(meta[0, :n_tiles], meta[1, :n_tiles], meta[2, :n_tiles], tbl, xp_all, comb_tok,
                   w_expert_gu[l], w_expert_down[l])

    fin = functools.partial(_final, wsg=wsg, wsd=wsd, gpo=row(g_post_ffn))
    tiles_per_lat = x_sample.shape[1] // TOKEN_TILE
    yc = fin(acc, 0, h2c.reshape(tc, D_MODEL), x1c.reshape(tc, D_MODEL), mod, ctx_row)
    ys = fin(acc, tc // TOKEN_TILE, h2s.reshape(ts, D_MODEL), x1s.reshape(ts, D_MODEL), mod,
             lambda i: i // tiles_per_lat + 1)

    new_c = c_new.reshape(b_ctx, 1, N_DIR, HEADS, HEAD_DIM, HEAD_DIM)
    new_n = n_new.reshape(b_ctx, 1, N_DIR, HEADS, HEAD_DIM)
    new_m = m_new[..., 0].reshape(b_ctx, 1, N_DIR, HEADS)
    return (yc.reshape(x_prompt.shape), ys.reshape(x_sample.shape), new_c, new_n, new_m)
```

```python
import functools

import jax
import jax.numpy as jnp
import numpy as np
from jax import lax
from jax.experimental import pallas as pl
from jax.experimental.pallas import tpu as pltpu

F32 = jnp.float32
BF16 = jnp.bfloat16

D_MODEL = 1024
GRID_W = 64
POOL_WIDTH = 512
POOL_GROUPS = 4
POOL_GROUP_DIM = 128
POOL_WINDOWS = (2, 4, 8, 16)
HEADS = 4
HEAD_DIM = 128
MLSTM_WIDTH = HEADS * HEAD_DIM
N_DIR = 2
GATE_COLS = N_DIR * 2 * HEADS
N_EXPERTS = 64
TOP_K = 6
N_EXPERT_GROUPS = 8
GROUP_SIZE = N_EXPERTS // N_EXPERT_GROUPS
TOPK_GROUPS = 4
EXPERT_DIM = 256
SHARED_DIM = 256
ROUTED_SCALE = 2.5
N_MOD = 6
EPS = 1e-6
K_SCALE = HEAD_DIM ** -0.5

LANES = 128
CHUNK = 256
TOKEN_TILE = 256
MOE_TILE = 256
Y_PITCH = MOE_TILE + 8
VMEM_LIMIT = 56 * 1024 * 1024
EXPERTS_VMEM_LIMIT = 58 * 1024 * 1024


def _split3(x):
    p1 = x.astype(BF16)
    r1 = x - p1.astype(F32)
    p2 = r1.astype(BF16)
    p3 = (r1 - p2.astype(F32)).astype(BF16)
    return p1, p2, p3


def _split2(x):
    p1 = x.astype(BF16)
    p2 = (x - p1.astype(F32)).astype(BF16)
    return p1, p2


def _dot(a, b):
    return jnp.dot(a, b, preferred_element_type=F32)


def _dot_nt(a, b):
    return lax.dot_general(a, b, (((1,), (1,)), ((), ())), preferred_element_type=F32)


def _rmsnorm(x, g):
    return x * lax.rsqrt(jnp.mean(x * x, axis=-1, keepdims=True) + EPS) * g


def _silu(x):
    return x * jax.nn.sigmoid(x)


def _params(*sem):
    return pltpu.CompilerParams(dimension_semantics=sem, vmem_limit_bytes=VMEM_LIMIT)


def _mod_kernel(c_ref, w_ref, b_ref, o_ref):
    a = _silu(c_ref[...])
    a_stack = jnp.concatenate(_split3(a), axis=0)
    w1, w2 = _split2(w_ref[...])
    r1 = _dot(a_stack, w1)
    r2 = _dot(a_stack[:32], w2)
    o_ref[...] = (r1[0:16] + r1[16:32] + r1[32:48] + r2[0:16] + r2[16:32]) + b_ref[...]


def _mod_rows(cvec, w_ada, b_ada):
    n = N_MOD * D_MODEL
    tn = 1536
    return pl.pallas_call(
        _mod_kernel,
        grid=(n // tn,),
        in_specs=[pl.BlockSpec((16, D_MODEL), lambda j: (0, 0)),
                  pl.BlockSpec((D_MODEL, tn), lambda j: (0, j)),
                  pl.BlockSpec((1, tn), lambda j: (0, j))],
        out_specs=pl.BlockSpec((16, tn), lambda j: (0, j)),
        out_shape=jax.ShapeDtypeStruct((16, n), F32),
        compiler_params=_params("arbitrary"),
        name="mod",
    )(cvec, w_ada, b_ada.reshape(1, n))


def _inproj_kernel(x_ref, mod_ref, g_ref, wm_ref, wt_ref, wg_ref, wgt_ref, bgr_ref, bgc_ref,
                   u_ref, k_ref, o_ref, qt_ref, vt_ref, gate_ref, gatet_ref):
    x = x_ref[0]
    mod = mod_ref[0]
    h = _rmsnorm(x, g_ref[...]) * (1.0 + mod[1:2]) + mod[0:1]
    h1, h2, h3 = _split3(h)
    z = _dot(h1, wm_ref[...])
    u_ref[0] = z[:, 0:512].astype(BF16)
    k_ref[0] = (z[:, 512:1024] * K_SCALE).astype(BF16)
    o_ref[0] = z[:, 1024:1536].astype(BF16)
    zt = _dot_nt(wt_ref[...], h1)
    qt_ref[0] = zt[0:512].astype(BF16)
    vt_ref[0] = zt[512:1024].astype(BF16)
    tm = x.shape[0]
    wg1, wg2 = _split2(wg_ref[...])
    hs = jnp.concatenate([h1, h2, h3], axis=0)
    r1 = _dot(hs, wg1)
    r2 = _dot(hs[:2 * tm], wg2)
    gate_ref[0] = (r1[0:tm] + r1[tm:2 * tm] + r1[2 * tm:] + r2[0:tm] + r2[tm:]) + bgr_ref[...]
    wt1, wt2 = _split2(wgt_ref[...])
    wts = jnp.concatenate([wt1, wt2], axis=0)
    t1 = _dot_nt(wts, h1)
    t2 = _dot_nt(wts, h2)
    t3 = _dot_nt(wt1, h3)
    gatet_ref[0] = (t1[0:16] + t1[16:32] + t2[0:16] + t2[16:32] + t3) + bgc_ref[...]


def _inproj(x, mod, mod_row, g, wm, wt, wg, wgt, bgr, bgc):
    b, t, _ = x.shape
    tm = TOKEN_TILE
    const = lambda *shape: pl.BlockSpec(shape, lambda i, j: (0,) * len(shape))
    tok = lambda w: pl.BlockSpec((1, tm, w), lambda i, j: (i, j, 0))
    tok_t = lambda r: pl.BlockSpec((1, r, tm), lambda i, j: (i, 0, j))
    sd = jax.ShapeDtypeStruct
    return pl.pallas_call(
        _inproj_kernel,
        grid=(b, t // tm),
        in_specs=[tok(D_MODEL),
                  pl.BlockSpec((1, N_MOD, D_MODEL), lambda i, j: (mod_row(i), 0, 0)),
                  const(1, D_MODEL), const(D_MODEL, 1536), const(1024, D_MODEL),
                  const(D_MODEL, LANES), const(16, D_MODEL), const(1, LANES), const(16, 1)],
        out_specs=[tok(512), tok(512), tok(512), tok_t(512), tok_t(512), tok(LANES), tok_t(16)],
        out_shape=[sd((b, t, 512), BF16), sd((b, t, 512), BF16), sd((b, t, 512), BF16),
                   sd((b, 512, t), BF16), sd((b, 512, t), BF16), sd((b, t, LANES), F32),
                   sd((b, 16, t), F32)],
        compiler_params=_params("arbitrary", "arbitrary"),
        name="inproj",
    )(x, mod, g, wm, wt, wg, wgt, bgr, bgc)


def _log_sigmoid(x):
    return jnp.minimum(x, 0.0) - jnp.log1p(jnp.exp(-jnp.abs(x)))


def _scan_unit(st, k, qt, vt, u_col, u_row, b_row, btot, mask, s_prev, m_prev, use_state):
    dh = HEAD_DIM
    n = st.shape[0]
    ub = jnp.where(mask, jnp.broadcast_to(u_col, (n, n)), -jnp.inf)
    z = jnp.maximum(m_prev, jnp.max(ub, axis=0, keepdims=True))
    p = (jnp.exp(ub - z) * st).astype(BF16)
    ones = jnp.ones((dh, n), BF16)
    tot = _dot(jnp.concatenate([vt, ones], axis=0), p)
    if use_state:
        tot = tot + jnp.exp(m_prev - z) * _dot(s_prev.astype(BF16), qt)
    floor = jnp.exp(-(b_row + z))
    h_t = tot[:dh] / jnp.maximum(jnp.abs(tot[dh:]), floor)
    g_row = btot + u_row
    m_new = jnp.maximum(btot + m_prev, jnp.max(g_row, axis=-1, keepdims=True))
    w_row = jnp.exp(g_row - m_new)
    vw = jnp.concatenate([(vt.astype(F32) * w_row).astype(BF16),
                          jnp.broadcast_to(w_row, (dh, n)).astype(BF16)], axis=0)
    s_new = jnp.exp(btot + m_prev - m_new) * s_prev + _dot(vw, k)
    return h_t.T, s_new, m_new


def _mlstm_kernel(*refs, nc, zero_init, emit_state):
    it = iter(refs)
    fwd_refs = tuple(next(it) for _ in range(5))
    bwd_refs = tuple(next(it) for _ in range(5)) if nc > 1 else fwd_refs
    if not zero_init:
        s0_ref, m0_ref = next(it), next(it)
    h_refs = (next(it), next(it))
    if emit_state:
        c_out, n_out, m_out = next(it), next(it), next(it)
    s_scr, m_scr = next(it), next(it)

    j = pl.program_id(1)
    n = CHUNK
    dh = HEAD_DIM

    @pl.when(j == 0)
    def _():
        if zero_init:
            s_scr[...] = jnp.zeros_like(s_scr)
            m_scr[...] = jnp.zeros_like(m_scr)
        else:
            s_scr[...] = s0_ref[0]
            m_scr[...] = m0_ref[0]

    rows = lax.broadcasted_iota(jnp.int32, (n, n), 0)
    cols = lax.broadcasted_iota(jnp.int32, (n, n), 1)
    le = rows <= cols
    ge = rows >= cols
    tri_le = le.astype(BF16)
    tri_ge = ge.astype(BF16)
    use_state = not (zero_init and nc == 1)

    def gate_terms(d):
        g_ref, gt_ref = (fwd_refs, bwd_refs)[d][3:5]
        gate = g_ref[0]
        gate_t = gt_ref[0]
        lf = _log_sigmoid(gate)
        lf_t = _log_sigmoid(gate_t)
        tri_c, tri_r = (tri_ge, tri_le) if d == 0 else (tri_le, tri_ge)
        bc = _dot(tri_c, jnp.concatenate(_split3(lf), axis=1))
        b_cols = bc[:, 0:128] + bc[:, 128:256] + bc[:, 256:384]
        br = _dot(jnp.concatenate(_split3(lf_t), axis=0), tri_r)
        b_rows = br[0:16] + br[16:32] + br[32:48]
        return gate, gate_t, b_cols, b_rows, jnp.sum(lf_t, axis=-1, keepdims=True)

    terms = [gate_terms(0), gate_terms(1)]
    hs = ([], [])
    for hd in range(HEADS):
        hsl = slice(hd * dh, (hd + 1) * dh)
        st = None
        for d in range(N_DIR):
            k_ref, qt_ref, vt_ref = (fwd_refs, bwd_refs)[d][0:3]
            gate, gate_t, b_cols, b_rows, tot_rows = terms[d]
            ci = d * 8 + hd
            cf = d * 8 + 4 + hd
            unit = d * HEADS + hd
            k = k_ref[0, :, hsl]
            qt = qt_ref[0, hsl, :]
            if st is None or nc > 1:
                st = _dot(k, qt)
            mask = le if d == 0 else ge
            h, s_new, m_new = _scan_unit(
                st, k, qt, vt_ref[0, hsl, :],
                gate[:, ci:ci + 1] - b_cols[:, cf:cf + 1],
                gate_t[ci:ci + 1, :] - b_rows[cf:cf + 1, :],
                b_rows[cf:cf + 1, :], tot_rows[cf:cf + 1, :],
                mask, s_scr[unit], m_scr[unit][:, 0:1], use_state)
            s_scr[unit] = s_new
            m_scr[unit] = jnp.broadcast_to(m_new, (1, LANES))
            hs[d].append(h)
    for d in range(N_DIR):
        h_refs[d][0] = jnp.concatenate(hs[d], axis=1).astype(BF16)

    if emit_state:
        @pl.when(j == nc - 1)
        def _():
            for unit in range(N_DIR * HEADS):
                s = s_scr[unit]
                c_out[0, unit] = s[:dh].T
                n_out[0, unit] = s[dh:dh + 1]
                m_out[0, unit] = m_scr[unit]


def _mlstm(k, qt, vt, gate, gate_t, s0, m0, emit_state):
    b, t, _ = k.shape
    nc = t // CHUNK
    zero_init = s0 is None
    nu = N_DIR * HEADS
    fwd = lambda w: pl.BlockSpec((1, CHUNK, w), lambda i, j: (i, j, 0))
    bwd = lambda w: pl.BlockSpec((1, CHUNK, w), lambda i, j: (i, nc - 1 - j, 0))
    fwd_t = lambda r: pl.BlockSpec((1, r, CHUNK), lambda i, j: (i, 0, j))
    bwd_t = lambda r: pl.BlockSpec((1, r, CHUNK), lambda i, j: (i, 0, nc - 1 - j))
    args = [k, qt, vt, gate, gate_t]
    in_specs = [fwd(512), fwd_t(512), fwd_t(512), fwd(LANES), fwd_t(16)]
    if nc > 1:
        args += [k, qt, vt, gate, gate_t]
        in_specs += [bwd(512), bwd_t(512), bwd_t(512), bwd(LANES), bwd_t(16)]
    if not zero_init:
        args += [s0, m0]
        in_specs += [pl.BlockSpec((1, nu, 2 * HEAD_DIM, HEAD_DIM), lambda i, j: (i, 0, 0, 0)),
                     pl.BlockSpec((1, nu, 1, LANES), lambda i, j: (i, 0, 0, 0))]
    sd = jax.ShapeDtypeStruct
    out_shape = [sd((b, t, 512), BF16), sd((b, t, 512), BF16)]
    out_specs = [fwd(512), bwd(512)]
    if emit_state:
        out_shape += [sd((b, nu, HEAD_DIM, HEAD_DIM), F32), sd((b, nu, 1, HEAD_DIM), F32),
                      sd((b, nu, 1, LANES), F32)]
        out_specs += [pl.BlockSpec((1, nu, HEAD_DIM, HEAD_DIM), lambda i, j: (i, 0, 0, 0)),
                      pl.BlockSpec((1, nu, 1, HEAD_DIM), lambda i, j: (i, 0, 0, 0)),
                      pl.BlockSpec((1, nu, 1, LANES), lambda i, j: (i, 0, 0, 0))]
    return pl.pallas_call(
        functools.partial(_mlstm_kernel, nc=nc, zero_init=zero_init, emit_state=emit_state),
        grid=(b, nc),
        in_specs=in_specs,
        out_specs=out_specs,
        out_shape=out_shape,
        scratch_shapes=[pltpu.VMEM((nu, 2 * HEAD_DIM, HEAD_DIM), F32),
                        pltpu.VMEM((nu, 1, LANES), F32)],
        compiler_params=_params("arbitrary", "arbitrary"),
        name="mlstm",
    )(*args)


def _outproj_kernel(x_ref, u_ref, hf_ref, hb_ref, o_ref, mod_ref, pm_ref, pinv_ref, wp_ref, ps_ref,
                    wo_ref, gpm_ref, gpf_ref, wrt_ref, x1_ref, h2_ref, xp_ref, lg_ref, slab_ref, *, tm):
    i = pl.program_id(1)
    x = x_ref[0]
    mod = mod_ref[0]
    row0 = pl.multiple_of(i * tm, tm)
    u_full = u_ref[0]
    u_tile = u_ref[0, pl.ds(row0, tm), :].astype(F32)
    mix = jnp.zeros((tm, D_MODEL), F32)
    for g in range(POOL_GROUPS):
        sl = slice(g * POOL_GROUP_DIM, (g + 1) * POOL_GROUP_DIM)
        box = _dot(pm_ref[g], u_full[:, sl])
        diff = box * pinv_ref[g] - u_tile[:, sl]
        yp = _dot(diff.astype(BF16), wp_ref[g]) * ps_ref[:, sl]
        mix = mix + _dot(yp.astype(BF16), wo_ref[sl, :])
    hsum = hf_ref[0].astype(F32) + hb_ref[0].astype(F32)
    yml = jax.nn.sigmoid(o_ref[0].astype(F32)) * hsum
    mix = mix + _dot(yml.astype(BF16), wo_ref[POOL_WIDTH:, :])
    x1 = x + mod[2:3] * _rmsnorm(mix, gpm_ref[...])
    x1_ref[0] = x1
    h2 = _rmsnorm(x1, gpf_ref[...]) * (1.0 + mod[4:5]) + mod[3:4]
    p1, p2, p3 = _split3(h2)
    h2_ref[0] = p1
    for cc in range(D_MODEL // LANES):
        slab_ref[:, cc, :] = h2[:, cc * LANES:(cc + 1) * LANES]
    xp_ref[0] = slab_ref[...].astype(BF16)
    w1, w2, w3 = _split3(wrt_ref[...])
    ws = jnp.concatenate([w1, w2, w3], axis=0)
    r1 = _dot_nt(ws, p1)
    r2 = _dot_nt(ws[:128], p2)
    r3 = _dot_nt(w1, p3)
    lg_ref[0] = r1[0:64] + r1[64:128] + r1[128:192] + r2[0:64] + r2[64:128] + r3


def _outproj(x, u, hf, hb, o, mod, mod_row, pm, pinv, wp, ps, wo, gpm, gpf, wrt):
    b, t, _ = x.shape
    tm = TOKEN_TILE
    const = lambda *shape: pl.BlockSpec(shape, lambda i, j: (0,) * len(shape))
    tok = lambda w: pl.BlockSpec((1, tm, w), lambda i, j: (i, j, 0))
    sd = jax.ShapeDtypeStruct
    return pl.pallas_call(
        functools.partial(_outproj_kernel, tm=tm),
        grid=(b, t // tm),
        in_specs=[tok(D_MODEL),
                  pl.BlockSpec((1, t, 512), lambda i, j: (i, 0, 0)),
                  tok(512), tok(512), tok(512),
                  pl.BlockSpec((1, N_MOD, D_MODEL), lambda i, j: (mod_row(i), 0, 0)),
                  pl.BlockSpec((POOL_GROUPS, tm, t), lambda i, j: (0, j, 0)),
                  pl.BlockSpec((POOL_GROUPS, tm, 1), lambda i, j: (0, j, 0)),
                  const(POOL_GROUPS, POOL_GROUP_DIM, POOL_GROUP_DIM), const(1, POOL_WIDTH),
                  const(D_MODEL, D_MODEL), const(1, D_MODEL), const(1, D_MODEL),
                  const(N_EXPERTS, D_MODEL)],
        out_specs=[tok(D_MODEL), tok(D_MODEL),
                   pl.BlockSpec((1, tm, D_MODEL // LANES, LANES), lambda i, j: (i, j, 0, 0)),
                   pl.BlockSpec((1, N_EXPERTS, tm), lambda i, j: (i, 0, j))],
        out_shape=[sd((b, t, D_MODEL), F32), sd((b, t, D_MODEL), BF16),
                   sd((b, t, D_MODEL // LANES, LANES), BF16), sd((b, N_EXPERTS, t), F32)],
        scratch_shapes=[pltpu.VMEM((tm, D_MODEL // LANES, LANES), F32)],
        compiler_params=_params("arbitrary", "arbitrary"),
        name="outproj",
    )(x, u, hf, hb, o, mod, pm, pinv, wp, ps, wo, gpm, gpf, wrt)


def _router_kernel(lg_ref, br_ref, comb_ref, sel_ref):
    s = jax.nn.sigmoid(lg_ref[...])
    biased = s + br_ref[...]
    gidx = lax.broadcasted_iota(jnp.int32, s.shape, 0)
    jidx = lax.broadcasted_iota(jnp.int32, s.shape, 1)
    neg = -jnp.inf
    m1 = jnp.max(biased, axis=1, keepdims=True)
    i1 = jnp.min(jnp.where(biased == m1, jidx, GROUP_SIZE), axis=1, keepdims=True)
    m2 = jnp.max(jnp.where(jidx == i1, neg, biased), axis=1, keepdims=True)
    gscore = m1 + m2
    gi = lax.broadcasted_iota(jnp.int32, gscore.shape, 0)
    gmask = jnp.zeros(gscore.shape, F32)
    cur = gscore
    for _ in range(TOPK_GROUPS):
        mx = jnp.max(cur, axis=0, keepdims=True)
        ix = jnp.min(jnp.where(cur == mx, gi, N_EXPERT_GROUPS), axis=0, keepdims=True)
        hit = gi == ix
        gmask = jnp.where(hit, 1.0, gmask)
        cur = jnp.where(hit, neg, cur)
    cur = jnp.where(gmask > 0, biased, neg)
    eidx = gidx * GROUP_SIZE + jidx
    selmask = jnp.zeros(s.shape, F32)
    for _ in range(TOP_K):
        mx = jnp.max(jnp.max(cur, axis=1, keepdims=True), axis=0, keepdims=True)
        ix = jnp.where(cur == mx, eidx, N_EXPERTS)
        ix = jnp.min(jnp.min(ix, axis=1, keepdims=True), axis=0, keepdims=True)
        hit = eidx == ix
        selmask = jnp.where(hit, 1.0, selmask)
        cur = jnp.where(hit, neg, cur)
    sel = selmask * s
    tot = jnp.sum(jnp.sum(sel, axis=1, keepdims=True), axis=0, keepdims=True)
    comb_ref[...] = sel / tot * ROUTED_SCALE
    sel_ref[...] = selmask


def _router(logits_t, b_router):
    t = logits_t.shape[1]
    tl = 1024
    shp = (N_EXPERT_GROUPS, GROUP_SIZE, t)
    blk = pl.BlockSpec((N_EXPERT_GROUPS, GROUP_SIZE, tl), lambda j: (0, 0, j))
    comb, sel = pl.pallas_call(
        _router_kernel,
        grid=(t // tl,),
        in_specs=[blk, pl.BlockSpec((N_EXPERT_GROUPS, GROUP_SIZE, 1), lambda j: (0, 0, 0))],
        out_specs=[blk, blk],
        out_shape=[jax.ShapeDtypeStruct(shp, F32), jax.ShapeDtypeStruct(shp, F32)],
        compiler_params=_params("arbitrary"),
        name="router",
    )(logits_t.reshape(shp), b_router.reshape(N_EXPERT_GROUPS, GROUP_SIZE, 1))
    return comb.reshape(N_EXPERTS, t), sel.reshape(N_EXPERTS, t)


def _plan_kernel(sel_ref, pos_ref, meta_ref, emeta_ref, *, n_meta):
    t = sel_ref.shape[1]
    tm = float(MOE_TILE)
    sel = sel_ref[...]
    selb = sel.astype(BF16)
    blk = 256
    rr = lax.broadcasted_iota(jnp.int32, (blk, blk), 0)
    cc = lax.broadcasted_iota(jnp.int32, (blk, blk), 1)
    before = (rr < cc).astype(BF16)
    carry = jnp.zeros((N_EXPERTS, 1), F32)
    ranks = []
    for b in range(t // blk):
        sb = selb[:, b * blk:(b + 1) * blk]
        ranks.append(_dot(sb, before) + carry)
        carry = carry + jnp.sum(sel[:, b * blk:(b + 1) * blk], axis=1, keepdims=True)
    rank = jnp.concatenate(ranks, axis=1)
    cnt = carry
    ntile = jnp.floor((cnt + (tm - 1.0)) * (1.0 / tm))
    er = lax.broadcasted_iota(jnp.int32, (N_EXPERTS, N_EXPERTS), 0)
    ec = lax.broadcasted_iota(jnp.int32, (N_EXPERTS, N_EXPERTS), 1)
    below = (ec < er).astype(BF16)
    tstart = _dot(below, jnp.broadcast_to(ntile, (N_EXPERTS, LANES)).astype(BF16))[:, 0:1]
    pos = tstart * tm + rank
    erank = _dot(below, selb)
    rows = []
    for k in range(TOP_K):
        hit = (sel > 0.0) & (erank == float(k))
        rows.append(jnp.sum(jnp.where(hit, pos, 0.0), axis=0, keepdims=True))
    rows += [jnp.zeros((1, t), F32)] * (8 - TOP_K)
    pos_ref[...] = jnp.concatenate(rows, axis=0).astype(jnp.int32)

    tau = lax.broadcasted_iota(jnp.int32, (N_EXPERTS, n_meta), 1).astype(F32)
    eidx = lax.broadcasted_iota(jnp.int32, (N_EXPERTS, n_meta), 0).astype(F32)
    te = jnp.sum(((tstart + ntile) <= tau).astype(F32), axis=0, keepdims=True)
    te = jnp.minimum(te, float(N_EXPERTS - 1))
    onehot = eidx == te
    cnt_t = jnp.sum(jnp.where(onehot, cnt, 0.0), axis=0, keepdims=True)
    ts_t = jnp.sum(jnp.where(onehot, tstart, 0.0), axis=0, keepdims=True)
    tr = jnp.clip(cnt_t - (tau[0:1] - ts_t) * tm, 0.0, tm)
    tf = jnp.where((tau[0:1] == ts_t) & (tr > 0.0), 1.0, 0.0)
    meta_ref[...] = jnp.concatenate([te, tr, tf] + [jnp.zeros((1, n_meta), F32)] * 5, axis=0).astype(jnp.int32)

    eye = (lax.broadcasted_iota(jnp.int32, (N_EXPERTS, LANES), 0)
           == lax.broadcasted_iota(jnp.int32, (N_EXPERTS, LANES), 1))
    as_row = lambda col: jnp.sum(jnp.where(eye, col, 0.0), axis=0, keepdims=True)
    emeta_ref[...] = jnp.concatenate([as_row(cnt), as_row(tstart), as_row(ntile)]
                                     + [jnp.zeros((1, LANES), F32)] * 5, axis=0).astype(jnp.int32)


def _plan(sel, n_meta):
    t = sel.shape[1]
    sd = jax.ShapeDtypeStruct
    return pl.pallas_call(
        functools.partial(_plan_kernel, n_meta=n_meta),
        out_shape=[sd((8, t), jnp.int32), sd((8, n_meta), jnp.int32), sd((8, LANES), jnp.int32)],
        compiler_params=pltpu.CompilerParams(vmem_limit_bytes=VMEM_LIMIT),
        name="plan",
    )(sel)


def _invert_kernel(pos_ref, emeta_ref, tbl_ref, *, n_tokens):
    def pad_expert(e, carry):
        first = emeta_ref[1, e] * MOE_TILE
        def put(i, c):
            tbl_ref[i] = n_tokens
            return c
        lax.fori_loop(first + emeta_ref[0, e], first + emeta_ref[2, e] * MOE_TILE, put, 0)
        return carry
    lax.fori_loop(0, N_EXPERTS, pad_expert, 0)
    last = N_EXPERTS - 1
    def put_tail(i, c):
        tbl_ref[i] = n_tokens
        return c
    lax.fori_loop((emeta_ref[1, last] + emeta_ref[2, last]) * MOE_TILE, tbl_ref.shape[0], put_tail, 0)

    unroll = 4
    def scatter(i, carry):
        for u in range(unroll):
            tok = i * unroll + u
            for k in range(TOP_K):
                tbl_ref[pos_ref[tok * 8 + k]] = tok
        return carry
    lax.fori_loop(0, n_tokens // unroll, scatter, 0)


def _invert(pos, emeta, n_slots):
    n_tokens = pos.shape[0] // 8
    smem = pl.BlockSpec(memory_space=pltpu.SMEM)
    return pl.pallas_call(
        functools.partial(_invert_kernel, n_tokens=n_tokens),
        in_specs=[smem, smem],
        out_specs=smem,
        out_shape=jax.ShapeDtypeStruct((n_slots,), jnp.int32),
        name="invert",
    )(pos, emeta)


def _experts_kernel(te_ref, tr_ref, tf_ref, tbl_ref, xc_ref, xs_ref, comb_ref, wgu_ref, wd_ref, acc_out,
                    xbuf, acc, stage, cstage, ybuf, wgu_b, wd_b, sems):
    tau = pl.program_id(0)
    e = te_ref[tau]
    rows = tr_ref[tau]
    tm = MOE_TILE
    nch = D_MODEL // LANES

    @pl.when(tau == 0)
    def _():
        tc, ts = xc_ref.shape[0], xs_ref.shape[0]
        copies = (pltpu.make_async_copy(xc_ref, xbuf.at[pl.ds(0, tc)], sems.at[0]),
                  pltpu.make_async_copy(xs_ref, xbuf.at[pl.ds(tc, ts)], sems.at[1]))
        for cp in copies:
            cp.start()
        n_pad = xbuf.shape[0] - tc - ts
        xbuf[pl.ds(tc + ts, n_pad)] = jnp.zeros((n_pad,) + xbuf.shape[1:], BF16)
        acc[...] = jnp.zeros_like(acc)
        stage[...] = jnp.zeros_like(stage)
        cstage[...] = jnp.zeros_like(cstage)
        for cp in copies:
            cp.wait()

    @pl.when(tf_ref[tau] == 1)
    def _():
        wgu_b[...] = wgu_ref[0].astype(BF16)
        wd_b[...] = wd_ref[0].astype(BF16)

    @pl.when(rows > 0)
    def _():
        base = tau * tm
        gu_n = 8
        def gather(i, c):
            for u in range(gu_n):
                j = i * gu_n + u
                tok = tbl_ref[base + j]
                stage[pl.ds(pl.multiple_of(j * nch, nch), nch), :] = xbuf[tok].astype(F32)
                cstage[pl.ds(j, 1), :] = comb_ref[pl.ds(tok, 1), :]
            return c
        lax.fori_loop(0, lax.shift_right_logical(rows + (gu_n - 1), 3), gather, 0)

        xb = jnp.concatenate([stage[pl.ds(cc, tm, stride=nch), :] for cc in range(nch)], axis=1).astype(BF16)
        gu = _dot(xb, wgu_b[...])
        comb3 = jnp.concatenate(_split3(cstage[...]), axis=1)
        sel_row = lax.broadcasted_iota(jnp.int32, (3 * LANES, EXPERT_DIM), 0) % LANES
        cexp = _dot(comb3, (sel_row == e).astype(BF16))
        act = (_silu(gu[:, :EXPERT_DIM]) * gu[:, EXPERT_DIM:] * cexp).astype(BF16)
        y = _dot(act, wd_b[...])
        for cc in range(D_MODEL // LANES):
            ybuf[cc * Y_PITCH:cc * Y_PITCH + tm, :] = y[:, cc * LANES:(cc + 1) * LANES]

        sc_n = 16
        def scatter(i, c):
            toks = [tbl_ref[base + i * sc_n + u] for u in range(sc_n)]
            olds = [acc[toks[u]] for u in range(sc_n)]
            news = [olds[u] + ybuf[pl.ds(i * sc_n + u, D_MODEL // LANES, stride=Y_PITCH), :] for u in range(sc_n)]
            for u in range(sc_n):
                acc[toks[u]] = news[u]
            return c
        lax.fori_loop(0, lax.shift_right_logical(rows + (sc_n - 1), 4), scatter, 0)

    @pl.when(tau == pl.num_programs(0) - 1)
    def _():
        cp = pltpu.make_async_copy(acc, acc_out, sems.at[2])
        cp.start()
        cp.wait()


def _experts(te, tr, tf, tbl, xc, xs, comb, wgu, wd):
    n_tok = comb.shape[0]
    n_tiles = te.shape[0]
    tm = MOE_TILE
    grid_spec = pltpu.PrefetchScalarGridSpec(
        num_scalar_prefetch=4,
        grid=(n_tiles,),
        in_specs=[pl.BlockSpec(memory_space=pl.ANY), pl.BlockSpec(memory_space=pl.ANY),
                  pl.BlockSpec((n_tok, LANES), lambda t, *_: (0, 0), pipeline_mode=pl.Buffered(1)),
                  pl.BlockSpec((1, D_MODEL, 2 * EXPERT_DIM), lambda t, te, tr, tf, tbl: (te[t], 0, 0)),
                  pl.BlockSpec((1, EXPERT_DIM, D_MODEL), lambda t, te, tr, tf, tbl: (te[t], 0, 0))],
        out_specs=pl.BlockSpec(memory_space=pl.ANY),
        scratch_shapes=[pltpu.VMEM((n_tok, D_MODEL // LANES, LANES), BF16),
                        pltpu.VMEM((n_tok, D_MODEL // LANES, LANES), F32),
                        pltpu.VMEM((tm * (D_MODEL // LANES), LANES), F32),
                        pltpu.VMEM((tm, LANES), F32),
                        pltpu.VMEM((D_MODEL // LANES * Y_PITCH, LANES), F32),
                        pltpu.VMEM((D_MODEL, 2 * EXPERT_DIM), BF16),
                        pltpu.VMEM((EXPERT_DIM, D_MODEL), BF16),
                        pltpu.SemaphoreType.DMA((3,))],
    )
    return pl.pallas_call(
        _experts_kernel,
        grid_spec=grid_spec,
        out_shape=jax.ShapeDtypeStruct((n_tok, D_MODEL // LANES, LANES), F32),
        compiler_params=pltpu.CompilerParams(dimension_semantics=("arbitrary",),
                                             vmem_limit_bytes=EXPERTS_VMEM_LIMIT),
        name="experts",
    )(te, tr, tf, tbl, xc, xs, comb, wgu, wd)


def _final_kernel(acc_ref, h_ref, x1_ref, mod_ref, wsg_ref, wsd_ref, gpo_ref, out_ref):
    routed = jnp.concatenate([acc_ref[:, cc, :] for cc in range(D_MODEL // LANES)], axis=1)
    gs = _dot(h_ref[...], wsg_ref[...])
    act = _silu(gs[:, :SHARED_DIM]) * gs[:, SHARED_DIM:]
    f = routed + _dot(act.astype(BF16), wsd_ref[...])
    out_ref[...] = x1_ref[...] + mod_ref[0][5:6] * _rmsnorm(f, gpo_ref[...])


def _final(acc, tile0, h2, x1, mod, mod_row, wsg, wsd, gpo):
    n = h2.shape[0]
    tm = TOKEN_TILE
    const = lambda *shape: pl.BlockSpec(shape, lambda i: (0,) * len(shape))
    tok = lambda w: pl.BlockSpec((tm, w), lambda i: (i, 0))
    return pl.pallas_call(
        _final_kernel,
        grid=(n // tm,),
        in_specs=[pl.BlockSpec((tm, D_MODEL // LANES, LANES), lambda i: (i + tile0, 0, 0)),
                  tok(D_MODEL), tok(D_MODEL),
                  pl.BlockSpec((1, N_MOD, D_MODEL), lambda i: (mod_row(i), 0, 0)),
                  const(D_MODEL, 2 * SHARED_DIM), const(SHARED_DIM, D_MODEL), const(1, D_MODEL)],
        out_specs=tok(D_MODEL),
        out_shape=jax.ShapeDtypeStruct((n, D_MODEL), F32),
        compiler_params=_params("arbitrary"),
        name="final",
    )(acc, h2, x1, mod, wsg, wsd, gpo)


def _window_bounds(n, w):
    idx = np.arange(n)
    return np.clip(idx - w // 2, 0, n), np.clip(idx + w - w // 2, 0, n)


def _pool_operators(t, grid):
    mats, invs = [], []
    for w in POOL_WINDOWS:
        if grid:
            rlo, rhi = _window_bounds(t // GRID_W, w)
            clo, chi = _window_bounds(GRID_W, w)
            r = np.arange(t) // GRID_W
            c = np.arange(t) % GRID_W
            m = ((r[None, :] >= rlo[r][:, None]) & (r[None, :] < rhi[r][:, None])
                 & (c[None, :] >= clo[c][:, None]) & (c[None, :] < chi[c][:, None]))
            cnt = (rhi - rlo)[r] * (chi - clo)[c]
        else:
            lo, hi = _window_bounds(t, w)
            sidx = np.arange(t)
            m = (sidx[None, :] >= lo[:, None]) & (sidx[None, :] < hi[:, None])
            cnt = hi - lo
        mats.append(m.astype(np.float32))
        invs.append((1.0 / cnt.astype(np.float64)).astype(np.float32)[:, None])
    return jnp.asarray(np.stack(mats), BF16), jnp.asarray(np.stack(invs), F32)


def kernel(x_prompt, x_sample, state_C, state_n, state_m, c, c_ctx, w_ada, b_ada, g_pre_mix, w_in, b_gate,
           w_pool, pool_scale, w_out, g_post_mix, g_pre_ffn, w_router, b_router, w_expert_gu, w_expert_down,
           w_shared_gu, w_shared_down, g_post_ffn):
    b_ctx = x_prompt.shape[0]
    b_lat = x_sample.shape[0]
    nu = N_DIR * HEADS
    l = 0
    row = lambda a: a[l].reshape(1, -1).astype(F32)

    cvec = jnp.zeros((16, D_MODEL), F32).at[0].set(c_ctx.astype(F32)).at[1:1 + b_lat].set(c.astype(F32))
    mod = _mod_rows(cvec, w_ada[l], b_ada[l]).reshape(16, N_MOD, D_MODEL)

    w_in_l = w_in[l]
    p0 = POOL_WIDTH
    mw = MLSTM_WIDTH
    w_u, w_q, w_k, w_v, w_o = (w_in_l[:, lo:lo + 512] for lo in (0, p0, p0 + mw, p0 + 2 * mw, p0 + 3 * mw))
    wm = jnp.concatenate([w_u, w_k, w_o], axis=1).astype(BF16)
    wt = jnp.concatenate([w_q.T, w_v.T], axis=0).astype(BF16)
    wg_cols = w_in_l[:, p0 + 4 * mw:]
    wg = jnp.pad(wg_cols, ((0, 0), (0, LANES - GATE_COLS)))
    wgt = wg_cols.T
    bg = b_gate[l].reshape(GATE_COLS).astype(F32)
    bgr = jnp.pad(bg, (0, LANES - GATE_COLS)).reshape(1, LANES)
    bgc = bg.reshape(GATE_COLS, 1)
    wp = w_pool[l].astype(BF16)
    wo = w_out[l].astype(BF16)
    wrt = w_router[l].T
    wsg = w_shared_gu[l].astype(BF16)
    wsd = w_shared_down[l].astype(BF16)

    def mixer(x, mod_row, grid, s0, m0, emit_state):
        t = x.shape[1]
        u, k, o, qt, vt, gate, gate_t = _inproj(x.astype(F32), mod, mod_row, row(g_pre_mix), wm, wt, wg, wgt,
                                                bgr, bgc)
        outs = _mlstm(k, qt, vt, gate, gate_t, s0, m0, emit_state)
        hf, hb = outs[0], outs[1]
        pm, pinv = _pool_operators(t, grid)
        x1, h2, xp, lg = _outproj(x.astype(F32), u, hf, hb, o, mod, mod_row, pm, pinv, wp, row(pool_scale), wo,
                                  row(g_post_mix), row(g_pre_ffn), wrt)
        return x1, h2, xp, lg, outs[2:]

    ctx_row = lambda i: 0
    lat_row = lambda i: i + 1
    x1c, h2c, xpc, lgc, (c_new, n_new, m_new) = mixer(x_prompt, ctx_row, False, None, None, True)
    s0 = jnp.concatenate(
        [jnp.swapaxes(state_C[:, l].reshape(b_lat, nu, HEAD_DIM, HEAD_DIM).astype(F32), -1, -2),
         jnp.broadcast_to(state_n[:, l].reshape(b_lat, nu, 1, HEAD_DIM).astype(F32),
                          (b_lat, nu, HEAD_DIM, HEAD_DIM))], axis=-2)
    m0 = jnp.broadcast_to(state_m[:, l].reshape(b_lat, nu, 1, 1).astype(F32), (b_lat, nu, 1, LANES))
    x1s, h2s, xps, lgs, _ = mixer(x_sample, lat_row, True, s0, m0, False)

    tc = b_ctx * x_prompt.shape[1]
    ts = b_lat * x_sample.shape[1]
    n_tok = tc + ts
    lg_all = jnp.concatenate([lgc.transpose(1, 0, 2).reshape(N_EXPERTS, tc),
                              lgs.transpose(1, 0, 2).reshape(N_EXPERTS, ts)], axis=1)
    comb, sel = _router(lg_all, b_router[l].astype(F32))
    n_tiles = n_tok * TOP_K // MOE_TILE + N_EXPERTS
    n_meta = -(-n_tiles // LANES) * LANES
    pos, meta, emeta = _plan(sel, n_meta)
    tbl = _invert(pos.T.reshape(-1), emeta, n_tiles * MOE_TILE)

    comb_tok = jnp.pad(comb.T, ((0, 8), (0, LANES - N_EXPERTS)))
    slab = (D_MODEL // LANES, LANES)
    acc = _experts(meta[0, :n_tiles], meta[1, :n_tiles], meta[2, :n_tiles], tbl,
                   xpc.reshape((tc,) + slab), xps.reshape((ts,) + slab), comb_tok,
                   w_expert_gu[l], w_expert_down[l])

    fin = functools.partial(_final, wsg=wsg, wsd=wsd, gpo=row(g_post_ffn))
    tiles_per_lat = x_sample.shape[1] // TOKEN_TILE
    yc = fin(acc, 0, h2c.reshape(tc, D_MODEL), x1c.reshape(tc, D_MODEL), mod, ctx_row)
    ys = fin(acc, tc // TOKEN_TILE, h2s.reshape(ts, D_MODEL), x1s.reshape(ts, D_MODEL), mod,
             lambda i: i // tiles_per_lat + 1)

    new_c = c_new.reshape(b_ctx, 1, N_DIR, HEADS, HEAD_DIM, HEAD_DIM)
    new_n = n_new.reshape(b_ctx, 1, N_DIR, HEADS, HEAD_DIM)
    new_m = m_new[..., 0].reshape(b_ctx, 1, N_DIR, HEADS)
    return (yc.reshape(x_prompt.shape), ys.reshape(x_sample.shape), new_c, new_n, new_m)
```

```python
import functools

import jax
import jax.numpy as jnp
import numpy as np
from jax import lax
from jax.experimental import pallas as pl
from jax.experimental.pallas import tpu as pltpu

F32 = jnp.float32
BF16 = jnp.bfloat16

D_MODEL = 1024
GRID_W = 64
POOL_WIDTH = 512
POOL_GROUPS = 4
POOL_GROUP_DIM = 128
POOL_WINDOWS = (2, 4, 8, 16)
HEADS = 4
HEAD_DIM = 128
MLSTM_WIDTH = HEADS * HEAD_DIM
N_DIR = 2
GATE_COLS = N_DIR * 2 * HEADS
N_EXPERTS = 64
TOP_K = 6
N_EXPERT_GROUPS = 8
GROUP_SIZE = N_EXPERTS // N_EXPERT_GROUPS
TOPK_GROUPS = 4
EXPERT_DIM = 256
SHARED_DIM = 256
ROUTED_SCALE = 2.5
N_MOD = 6
EPS = 1e-6
K_SCALE = HEAD_DIM ** -0.5

LANES = 128
CHUNK = 256
TOKEN_TILE = 256
MOE_TILE = 256
Y_PITCH = MOE_TILE + 8
VMEM_LIMIT = 56 * 1024 * 1024
EXPERTS_VMEM_LIMIT = 58 * 1024 * 1024


def _split3(x):
    p1 = x.astype(BF16)
    r1 = x - p1.astype(F32)
    p2 = r1.astype(BF16)
    p3 = (r1 - p2.astype(F32)).astype(BF16)
    return p1, p2, p3


def _split2(x):
    p1 = x.astype(BF16)
    p2 = (x - p1.astype(F32)).astype(BF16)
    return p1, p2


def _dot(a, b):
    return jnp.dot(a, b, preferred_element_type=F32)


def _dot_nt(a, b):
    return lax.dot_general(a, b, (((1,), (1,)), ((), ())), preferred_element_type=F32)


def _rmsnorm(x, g):
    return x * lax.rsqrt(jnp.mean(x * x, axis=-1, keepdims=True) + EPS) * g


def _silu(x):
    return x * jax.nn.sigmoid(x)


def _params(*sem):
    return pltpu.CompilerParams(dimension_semantics=sem, vmem_limit_bytes=VMEM_LIMIT)


def _mod_kernel(c_ref, w_ref, b_ref, o_ref):
    a = _silu(c_ref[...])
    a_stack = jnp.concatenate(_split3(a), axis=0)
    w1, w2 = _split2(w_ref[...])
    r1 = _dot(a_stack, w1)
    r2 = _dot(a_stack[:32], w2)
    o_ref[...] = (r1[0:16] + r1[16:32] + r1[32:48] + r2[0:16] + r2[16:32]) + b_ref[...]


def _mod_rows(cvec, w_ada, b_ada):
    n = N_MOD * D_MODEL
    tn = 1536
    return pl.pallas_call(
        _mod_kernel,
        grid=(n // tn,),
        in_specs=[pl.BlockSpec((16, D_MODEL), lambda j: (0, 0)),
                  pl.BlockSpec((D_MODEL, tn), lambda j: (0, j)),
                  pl.BlockSpec((1, tn), lambda j: (0, j))],
        out_specs=pl.BlockSpec((16, tn), lambda j: (0, j)),
        out_shape=jax.ShapeDtypeStruct((16, n), F32),
        compiler_params=_params("arbitrary"),
        name="mod",
    )(cvec, w_ada, b_ada.reshape(1, n))


def _inproj_kernel(x_ref, mod_ref, g_ref, wm_ref, wt_ref, wg_ref, wgt_ref, bgr_ref, bgc_ref,
                   u_ref, k_ref, o_ref, qt_ref, vt_ref, gate_ref, gatet_ref):
    x = x_ref[0]
    mod = mod_ref[0]
    h = _rmsnorm(x, g_ref[...]) * (1.0 + mod[1:2]) + mod[0:1]
    h1, h2, h3 = _split3(h)
    z = _dot(h1, wm_ref[...])
    u_ref[0] = z[:, 0:512].astype(BF16)
    k_ref[0] = (z[:, 512:1024] * K_SCALE).astype(BF16)
    o_ref[0] = z[:, 1024:1536].astype(BF16)
    zt = _dot_nt(wt_ref[...], h1)
    qt_ref[0] = zt[0:512].astype(BF16)
    vt_ref[0] = zt[512:1024].astype(BF16)
    tm = x.shape[0]
    wg1, wg2 = _split2(wg_ref[...])
    hs = jnp.concatenate([h1, h2, h3], axis=0)
    r1 = _dot(hs, wg1)
    r2 = _dot(hs[:2 * tm], wg2)
    gate_ref[0] = (r1[0:tm] + r1[tm:2 * tm] + r1[2 * tm:] + r2[0:tm] + r2[tm:]) + bgr_ref[...]
    wt1, wt2 = _split2(wgt_ref[...])
    wts = jnp.concatenate([wt1, wt2], axis=0)
    t1 = _dot_nt(wts, h1)
    t2 = _dot_nt(wts, h2)
    t3 = _dot_nt(wt1, h3)
    gatet_ref[0] = (t1[0:16] + t1[16:32] + t2[0:16] + t2[16:32] + t3) + bgc_ref[...]


def _inproj(x, mod, mod_row, g, wm, wt, wg, wgt, bgr, bgc):
    b, t, _ = x.shape
    tm = TOKEN_TILE
    const = lambda *shape: pl.BlockSpec(shape, lambda i, j: (0,) * len(shape))
    tok = lambda w: pl.BlockSpec((1, tm, w), lambda i, j: (i, j, 0))
    tok_t = lambda r: pl.BlockSpec((1, r, tm), lambda i, j: (i, 0, j))
    sd = jax.ShapeDtypeStruct
    return pl.pallas_call(
        _inproj_kernel,
        grid=(b, t // tm),
        in_specs=[tok(D_MODEL),
                  pl.BlockSpec((1, N_MOD, D_MODEL), lambda i, j: (mod_row(i), 0, 0)),
                  const(1, D_MODEL), const(D_MODEL, 1536), const(1024, D_MODEL),
                  const(D_MODEL, LANES), const(16, D_MODEL), const(1, LANES), const(16, 1)],
        out_specs=[tok(512), tok(512), tok(512), tok_t(512), tok_t(512), tok(LANES), tok_t(16)],
        out_shape=[sd((b, t, 512), BF16), sd((b, t, 512), BF16), sd((b, t, 512), BF16),
                   sd((b, 512, t), BF16), sd((b, 512, t), BF16), sd((b, t, LANES), F32),
                   sd((b, 16, t), F32)],
        compiler_params=_params("arbitrary", "arbitrary"),
        name="inproj",
    )(x, mod, g, wm, wt, wg, wgt, bgr, bgc)


def _log_sigmoid(x):
    return jnp.minimum(x, 0.0) - jnp.log1p(jnp.exp(-jnp.abs(x)))


def _scan_unit(st, k, qt, vt, u_col, u_row, b_row, btot, mask, s_prev, m_prev, use_state):
    dh = HEAD_DIM
    n = st.shape[0]
    ub = jnp.where(mask, jnp.broadcast_to(u_col, (n, n)), -jnp.inf)
    z = jnp.maximum(m_prev, jnp.max(ub, axis=0, keepdims=True))
    p = (jnp.exp(ub - z) * st).astype(BF16)
    ones = jnp.ones((dh, n), BF16)
    tot = _dot(jnp.concatenate([vt, ones], axis=0), p)
    if use_state:
        tot = tot + jnp.exp(m_prev - z) * _dot(s_prev.astype(BF16), qt)
    floor = jnp.exp(-(b_row + z))
    h_t = tot[:dh] / jnp.maximum(jnp.abs(tot[dh:]), floor)
    g_row = btot + u_row
    m_new = jnp.maximum(btot + m_prev, jnp.max(g_row, axis=-1, keepdims=True))
    w_row = jnp.exp(g_row - m_new)
    vw = jnp.concatenate([(vt.astype(F32) * w_row).astype(BF16),
                          jnp.broadcast_to(w_row, (dh, n)).astype(BF16)], axis=0)
    s_new = jnp.exp(btot + m_prev - m_new) * s_prev + _dot(vw, k)
    return h_t.T, s_new, m_new


def _mlstm_kernel(*refs, nc, zero_init, emit_state):
    it = iter(refs)
    fwd_refs = tuple(next(it) for _ in range(5))
    bwd_refs = tuple(next(it) for _ in range(5)) if nc > 1 else fwd_refs
    if not zero_init:
        s0_ref, m0_ref = next(it), next(it)
    h_refs = (next(it), next(it))
    if emit_state:
        c_out, n_out, m_out = next(it), next(it), next(it)
    s_scr, m_scr = next(it), next(it)

    j = pl.program_id(1)
    n = CHUNK
    dh = HEAD_DIM

    @pl.when(j == 0)
    def _():
        if zero_init:
            s_scr[...] = jnp.zeros_like(s_scr)
            m_scr[...] = jnp.zeros_like(m_scr)
        else:
            s_scr[...] = s0_ref[0]
            m_scr[...] = m0_ref[0]

    rows = lax.broadcasted_iota(jnp.int32, (n, n), 0)
    cols = lax.broadcasted_iota(jnp.int32, (n, n), 1)
    le = rows <= cols
    ge = rows >= cols
    tri_le = le.astype(BF16)
    tri_ge = ge.astype(BF16)
    use_state = not (zero_init and nc == 1)

    def gate_terms(d):
        g_ref, gt_ref = (fwd_refs, bwd_refs)[d][3:5]
        gate = g_ref[0]
        gate_t = gt_ref[0]
        lf = _log_sigmoid(gate)
        lf_t = _log_sigmoid(gate_t)
        tri_c, tri_r = (tri_ge, tri_le) if d == 0 else (tri_le, tri_ge)
        bc = _dot(tri_c, jnp.concatenate(_split3(lf), axis=1))
        b_cols = bc[:, 0:128] + bc[:, 128:256] + bc[:, 256:384]
        br = _dot(jnp.concatenate(_split3(lf_t), axis=0), tri_r)
        b_rows = br[0:16] + br[16:32] + br[32:48]
        return gate, gate_t, b_cols, b_rows, jnp.sum(lf_t, axis=-1, keepdims=True)

    terms = [gate_terms(0), gate_terms(1)]
    hs = ([], [])
    for hd in range(HEADS):
        hsl = slice(hd * dh, (hd + 1) * dh)
        st = None
        for d in range(N_DIR):
            k_ref, qt_ref, vt_ref = (fwd_refs, bwd_refs)[d][0:3]
            gate, gate_t, b_cols, b_rows, tot_rows = terms[d]
            ci = d * 8 + hd
            cf = d * 8 + 4 + hd
            unit = d * HEADS + hd
            k = k_ref[0, :, hsl]
            qt = qt_ref[0, hsl, :]
            if st is None or nc > 1:
                st = _dot(k, qt)
            mask = le if d == 0 else ge
            h, s_new, m_new = _scan_unit(
                st, k, qt, vt_ref[0, hsl, :],
                gate[:, ci:ci + 1] - b_cols[:, cf:cf + 1],
                gate_t[ci:ci + 1, :] - b_rows[cf:cf + 1, :],
                b_rows[cf:cf + 1, :], tot_rows[cf:cf + 1, :],
                mask, s_scr[unit], m_scr[unit][:, 0:1], use_state)
            s_scr[unit] = s_new
            m_scr[unit] = jnp.broadcast_to(m_new, (1, LANES))
            hs[d].append(h)
    for d in range(N_DIR):
        h_refs[d][0] = jnp.concatenate(hs[d], axis=1).astype(BF16)

    if emit_state:
        @pl.when(j == nc - 1)
        def _():
            for unit in range(N_DIR * HEADS):
                s = s_scr[unit]
                c_out[0, unit] = s[:dh].T
                n_out[0, unit] = s[dh:dh + 1]
                m_out[0, unit] = m_scr[unit]


def _mlstm(k, qt, vt, gate, gate_t, s0, m0, emit_state):
    b, t, _ = k.shape
    nc = t // CHUNK
    zero_init = s0 is None
    nu = N_DIR * HEADS
    fwd = lambda w: pl.BlockSpec((1, CHUNK, w), lambda i, j: (i, j, 0))
    bwd = lambda w: pl.BlockSpec((1, CHUNK, w), lambda i, j: (i, nc - 1 - j, 0))
    fwd_t = lambda r: pl.BlockSpec((1, r, CHUNK), lambda i, j: (i, 0, j))
    bwd_t = lambda r: pl.BlockSpec((1, r, CHUNK), lambda i, j: (i, 0, nc - 1 - j))
    args = [k, qt, vt, gate, gate_t]
    in_specs = [fwd(512), fwd_t(512), fwd_t(512), fwd(LANES), fwd_t(16)]
    if nc > 1:
        args += [k, qt, vt, gate, gate_t]
        in_specs += [bwd(512), bwd_t(512), bwd_t(512), bwd(LANES), bwd_t(16)]
    if not zero_init:
        args += [s0, m0]
        in_specs += [pl.BlockSpec((1, nu, 2 * HEAD_DIM, HEAD_DIM), lambda i, j: (i, 0, 0, 0)),
                     pl.BlockSpec((1, nu, 1, LANES), lambda i, j: (i, 0, 0, 0))]
    sd = jax.ShapeDtypeStruct
    out_shape = [sd((b, t, 512), BF16), sd((b, t, 512), BF16)]
    out_specs = [fwd(512), bwd(512)]
    if emit_state:
        out_shape += [sd((b, nu, HEAD_DIM, HEAD_DIM), F32), sd((b, nu, 1, HEAD_DIM), F32),
                      sd((b, nu, 1, LANES), F32)]
        out_specs += [pl.BlockSpec((1, nu, HEAD_DIM, HEAD_DIM), lambda i, j: (i, 0, 0, 0)),
                      pl.BlockSpec((1, nu, 1, HEAD_DIM), lambda i, j: (i, 0, 0, 0)),
                      pl.BlockSpec((1, nu, 1, LANES), lambda i, j: (i, 0, 0, 0))]
    return pl.pallas_call(
        functools.partial(_mlstm_kernel, nc=nc, zero_init=zero_init, emit_state=emit_state),
        grid=(b, nc),
        in_specs=in_specs,
        out_specs=out_specs,
        out_shape=out_shape,
        scratch_shapes=[pltpu.VMEM((nu, 2 * HEAD_DIM, HEAD_DIM), F32),
                        pltpu.VMEM((nu, 1, LANES), F32)],
        compiler_params=_params("arbitrary", "arbitrary"),
        name="mlstm",
    )(*args)


def _outproj_kernel(x_ref, u_ref, hf_ref, hb_ref, o_ref, mod_ref, pm_ref, pinv_ref, wp_ref, ps_ref,
                    wo_ref, gpm_ref, gpf_ref, wrt_ref, x1_ref, h2_ref, xp_ref, lg_ref, slab_ref, *, tm):
    i = pl.program_id(1)
    x = x_ref[0]
    mod = mod_ref[0]
    row0 = pl.multiple_of(i * tm, tm)
    u_full = u_ref[0]
    u_tile = u_ref[0, pl.ds(row0, tm), :].astype(F32)
    mix = jnp.zeros((tm, D_MODEL), F32)
    for g in range(POOL_GROUPS):
        sl = slice(g * POOL_GROUP_DIM, (g + 1) * POOL_GROUP_DIM)
        box = _dot(pm_ref[g], u_full[:, sl])
        diff = box * pinv_ref[g] - u_tile[:, sl]
        yp = _dot(diff.astype(BF16), wp_ref[g]) * ps_ref[:, sl]
        mix = mix + _dot(yp.astype(BF16), wo_ref[sl, :])
    hsum = hf_ref[0].astype(F32) + hb_ref[0].astype(F32)
    yml = jax.nn.sigmoid(o_ref[0].astype(F32)) * hsum
    mix = mix + _dot(yml.astype(BF16), wo_ref[POOL_WIDTH:, :])
    x1 = x + mod[2:3] * _rmsnorm(mix, gpm_ref[...])
    x1_ref[0] = x1
    h2 = _rmsnorm(x1, gpf_ref[...]) * (1.0 + mod[4:5]) + mod[3:4]
    p1, p2, p3 = _split3(h2)
    h2_ref[0] = p1
    for cc in range(D_MODEL // LANES):
        slab_ref[:, cc, :] = h2[:, cc * LANES:(cc + 1) * LANES]
    xp_ref[0] = slab_ref[...].astype(BF16)
    w1, w2, w3 = _split3(wrt_ref[...])
    ws = jnp.concatenate([w1, w2, w3], axis=0)
    r1 = _dot_nt(ws, p1)
    r2 = _dot_nt(ws[:128], p2)
    r3 = _dot_nt(w1, p3)
    lg_ref[0] = r1[0:64] + r1[64:128] + r1[128:192] + r2[0:64] + r2[64:128] + r3


def _outproj(x, u, hf, hb, o, mod, mod_row, pm, pinv, wp, ps, wo, gpm, gpf, wrt):
    b, t, _ = x.shape
    tm = TOKEN_TILE
    const = lambda *shape: pl.BlockSpec(shape, lambda i, j: (0,) * len(shape))
    tok = lambda w: pl.BlockSpec((1, tm, w), lambda i, j: (i, j, 0))
    sd = jax.ShapeDtypeStruct
    return pl.pallas_call(
        functools.partial(_outproj_kernel, tm=tm),
        grid=(b, t // tm),
        in_specs=[tok(D_MODEL),
                  pl.BlockSpec((1, t, 512), lambda i, j: (i, 0, 0)),
                  tok(512), tok(512), tok(512),
                  pl.BlockSpec((1, N_MOD, D_MODEL), lambda i, j: (mod_row(i), 0, 0)),
                  pl.BlockSpec((POOL_GROUPS, tm, t), lambda i, j: (0, j, 0)),
                  pl.BlockSpec((POOL_GROUPS, tm, 1), lambda i, j: (0, j, 0)),
                  const(POOL_GROUPS, POOL_GROUP_DIM, POOL_GROUP_DIM), const(1, POOL_WIDTH),
                  const(D_MODEL, D_MODEL), const(1, D_MODEL), const(1, D_MODEL),
                  const(N_EXPERTS, D_MODEL)],
        out_specs=[tok(D_MODEL), tok(D_MODEL),
                   pl.BlockSpec((1, tm, D_MODEL // LANES, LANES), lambda i, j: (i, j, 0, 0)),
                   pl.BlockSpec((1, N_EXPERTS, tm), lambda i, j: (i, 0, j))],
        out_shape=[sd((b, t, D_MODEL), F32), sd((b, t, D_MODEL), BF16),
                   sd((b, t, D_MODEL // LANES, LANES), BF16), sd((b, N_EXPERTS, t), F32)],
        scratch_shapes=[pltpu.VMEM((tm, D_MODEL // LANES, LANES), F32)],
        compiler_params=_params("arbitrary", "arbitrary"),
        name="outproj",
    )(x, u, hf, hb, o, mod, pm, pinv, wp, ps, wo, gpm, gpf, wrt)


def _router_kernel(lg_ref, br_ref, comb_ref, sel_ref):
    s = jax.nn.sigmoid(lg_ref[...])
    biased = s + br_ref[...]
    gidx = lax.broadcasted_iota(jnp.int32, s.shape, 0)
    jidx = lax.broadcasted_iota(jnp.int32, s.shape, 1)
    neg = -jnp.inf
    m1 = jnp.max(biased, axis=1, keepdims=True)
    i1 = jnp.min(jnp.where(biased == m1, jidx, GROUP_SIZE), axis=1, keepdims=True)
    m2 = jnp.max(jnp.where(jidx == i1, neg, biased), axis=1, keepdims=True)
    gscore = m1 + m2
    gi = lax.broadcasted_iota(jnp.int32, gscore.shape, 0)
    gmask = jnp.zeros(gscore.shape, F32)
    cur = gscore
    for _ in range(TOPK_GROUPS):
        mx = jnp.max(cur, axis=0, keepdims=True)
        ix = jnp.min(jnp.where(cur == mx, gi, N_EXPERT_GROUPS), axis=0, keepdims=True)
        hit = gi == ix
        gmask = jnp.where(hit, 1.0, gmask)
        cur = jnp.where(hit, neg, cur)
    cur = jnp.where(gmask > 0, biased, neg)
    eidx = gidx * GROUP_SIZE + jidx
    selmask = jnp.zeros(s.shape, F32)
    for _ in range(TOP_K):
        mx = jnp.max(jnp.max(cur, axis=1, keepdims=True), axis=0, keepdims=True)
        ix = jnp.where(cur == mx, eidx, N_EXPERTS)
        ix = jnp.min(jnp.min(ix, axis=1, keepdims=True), axis=0, keepdims=True)
        hit = eidx == ix
        selmask = jnp.where(hit, 1.0, selmask)
        cur = jnp.where(hit, neg, cur)
    sel = selmask * s
    tot = jnp.sum(jnp.sum(sel, axis=1, keepdims=True), axis=0, keepdims=True)
    comb_ref[...] = sel / tot * ROUTED_SCALE
    sel_ref[...] = selmask


def _router(logits_t, b_router):
    t = logits_t.shape[1]
    tl = 1024
    shp = (N_EXPERT_GROUPS, GROUP_SIZE, t)
    blk = pl.BlockSpec((N_EXPERT_GROUPS, GROUP_SIZE, tl), lambda j: (0, 0, j))
    comb, sel = pl.pallas_call(
        _router_kernel,
        grid=(t // tl,),
        in_specs=[blk, pl.BlockSpec((N_EXPERT_GROUPS, GROUP_SIZE, 1), lambda j: (0, 0, 0))],
        out_specs=[blk, blk],
        out_shape=[jax.ShapeDtypeStruct(shp, F32), jax.ShapeDtypeStruct(shp, F32)],
        compiler_params=_params("arbitrary"),
        name="router",
    )(logits_t.reshape(shp), b_router.reshape(N_EXPERT_GROUPS, GROUP_SIZE, 1))
    return comb.reshape(N_EXPERTS, t), sel.reshape(N_EXPERTS, t)


def _plan_kernel(sel_ref, pos_ref, meta_ref, emeta_ref, *, n_meta):
    t = sel_ref.shape[1]
    tm = float(MOE_TILE)
    sel = sel_ref[...]
    selb = sel.astype(BF16)
    blk = 256
    rr = lax.broadcasted_iota(jnp.int32, (blk, blk), 0)
    cc = lax.broadcasted_iota(jnp.int32, (blk, blk), 1)
    before = (rr < cc).astype(BF16)
    carry = jnp.zeros((N_EXPERTS, 1), F32)
    ranks = []
    for b in range(t // blk):
        sb = selb[:, b * blk:(b + 1) * blk]
        ranks.append(_dot(sb, before) + carry)
        carry = carry + jnp.sum(sel[:, b * blk:(b + 1) * blk], axis=1, keepdims=True)
    rank = jnp.concatenate(ranks, axis=1)
    cnt = carry
    ntile = jnp.floor((cnt + (tm - 1.0)) * (1.0 / tm))
    er = lax.broadcasted_iota(jnp.int32, (N_EXPERTS, N_EXPERTS), 0)
    ec = lax.broadcasted_iota(jnp.int32, (N_EXPERTS, N_EXPERTS), 1)
    below = (ec < er).astype(BF16)
    tstart = _dot(below, jnp.broadcast_to(ntile, (N_EXPERTS, LANES)).astype(BF16))[:, 0:1]
    pos = tstart * tm + rank
    erank = _dot(below, selb)
    rows = []
    for k in range(TOP_K):
        hit = (sel > 0.0) & (erank == float(k))
        rows.append(jnp.sum(jnp.where(hit, pos, 0.0), axis=0, keepdims=True))
    rows += [jnp.zeros((1, t), F32)] * (8 - TOP_K)
    pos_ref[...] = jnp.concatenate(rows, axis=0).astype(jnp.int32)

    tau = lax.broadcasted_iota(jnp.int32, (N_EXPERTS, n_meta), 1).astype(F32)
    eidx = lax.broadcasted_iota(jnp.int32, (N_EXPERTS, n_meta), 0).astype(F32)
    te = jnp.sum(((tstart + ntile) <= tau).astype(F32), axis=0, keepdims=True)
    te = jnp.minimum(te, float(N_EXPERTS - 1))
    onehot = eidx == te
    cnt_t = jnp.sum(jnp.where(onehot, cnt, 0.0), axis=0, keepdims=True)
    ts_t = jnp.sum(jnp.where(onehot, tstart, 0.0), axis=0, keepdims=True)
    tr = jnp.clip(cnt_t - (tau[0:1] - ts_t) * tm, 0.0, tm)
    tf = jnp.where((tau[0:1] == ts_t) & (tr > 0.0), 1.0, 0.0)
    meta_ref[...] = jnp.concatenate([te, tr, tf] + [jnp.zeros((1, n_meta), F32)] * 5, axis=0).astype(jnp.int32)

    eye = (lax.broadcasted_iota(jnp.int32, (N_EXPERTS, LANES), 0)
           == lax.broadcasted_iota(jnp.int32, (N_EXPERTS, LANES), 1))
    as_row = lambda col: jnp.sum(jnp.where(eye, col, 0.0), axis=0, keepdims=True)
    emeta_ref[...] = jnp.concatenate([as_row(cnt), as_row(tstart), as_row(ntile)]
                                     + [jnp.zeros((1, LANES), F32)] * 5, axis=0).astype(jnp.int32)


def _plan(sel, n_meta):
    t = sel.shape[1]
    sd = jax.ShapeDtypeStruct
    return pl.pallas_call(
        functools.partial(_plan_kernel, n_meta=n_meta),
        out_shape=[sd((8, t), jnp.int32), sd((8, n_meta), jnp.int32), sd((8, LANES), jnp.int32)],
        compiler_params=pltpu.CompilerParams(vmem_limit_bytes=VMEM_LIMIT),
        name="plan",
    )(sel)


def _fill_slot_table(pos_ref, emeta_ref, tbl_ref, n_tokens):
    def pad_expert(e, carry):
        first = emeta_ref[LANES + e] * MOE_TILE
        def put(i, c):
            tbl_ref[i] = n_tokens
            return c
        lax.fori_loop(first + emeta_ref[e], first + emeta_ref[2 * LANES + e] * MOE_TILE, put, 0)
        return carry
    lax.fori_loop(0, N_EXPERTS, pad_expert, 0)
    last = N_EXPERTS - 1
    def put_tail(i, c):
        tbl_ref[i] = n_tokens
        return c
    lax.fori_loop((emeta_ref[LANES + last] + emeta_ref[2 * LANES + last]) * MOE_TILE, tbl_ref.shape[0], put_tail, 0)

    unroll = 8
    def scatter(i, carry):
        first = i * (8 * unroll)
        for u in range(unroll):
            for k in range(TOP_K):
                tbl_ref[pos_ref[first + (8 * u + k)]] = i * unroll + u
        return carry
    lax.fori_loop(0, n_tokens // unroll, scatter, 0)


def _experts_kernel(te_ref, tr_ref, tf_ref, pos_ref, emeta_ref, xc_ref, xs_ref, comb_ref, wgu_ref, wd_ref, acc_out,
                    tbl_ref, xbuf, acc, stage, cstage, ybuf, wgu_b, wd_b, sems):
    tau = pl.program_id(0)
    e = te_ref[tau]
    rows = tr_ref[tau]
    tm = MOE_TILE
    nch = D_MODEL // LANES

    @pl.when(tau == 0)
    def _():
        tc, ts = xc_ref.shape[0], xs_ref.shape[0]
        copies = (pltpu.make_async_copy(xc_ref, xbuf.at[pl.ds(0, tc)], sems.at[0]),
                  pltpu.make_async_copy(xs_ref, xbuf.at[pl.ds(tc, ts)], sems.at[1]))
        for cp in copies:
            cp.start()
        n_pad = xbuf.shape[0] - tc - ts
        xbuf[pl.ds(tc + ts, n_pad)] = jnp.zeros((n_pad,) + xbuf.shape[1:], BF16)
        acc[...] = jnp.zeros_like(acc)
        stage[...] = jnp.zeros_like(stage)
        cstage[...] = jnp.zeros_like(cstage)
        _fill_slot_table(pos_ref, emeta_ref, tbl_ref, tc + ts)
        for cp in copies:
            cp.wait()

    @pl.when(tf_ref[tau] == 1)
    def _():
        wgu_b[...] = wgu_ref[0].astype(BF16)
        wd_b[...] = wd_ref[0].astype(BF16)

    @pl.when(rows > 0)
    def _():
        base = tau * tm
        gu_n = 8
        def gather(i, c):
            for u in range(gu_n):
                j = i * gu_n + u
                tok = tbl_ref[base + j]
                stage[pl.ds(pl.multiple_of(j * nch, nch), nch), :] = xbuf[tok].astype(F32)
                cstage[pl.ds(j, 1), :] = comb_ref[pl.ds(tok, 1), :]
            return c
        lax.fori_loop(0, lax.shift_right_logical(rows + (gu_n - 1), 3), gather, 0)

        xb = jnp.concatenate([stage[pl.ds(cc, tm, stride=nch), :] for cc in range(nch)], axis=1).astype(BF16)
        gu = _dot(xb, wgu_b[...])
        comb3 = jnp.concatenate(_split3(cstage[...]), axis=1)
        sel_row = lax.broadcasted_iota(jnp.int32, (3 * LANES, EXPERT_DIM), 0) % LANES
        cexp = _dot(comb3, (sel_row == e).astype(BF16))
        act = (_silu(gu[:, :EXPERT_DIM]) * gu[:, EXPERT_DIM:] * cexp).astype(BF16)
        y = _dot(act, wd_b[...])
        for cc in range(D_MODEL // LANES):
            ybuf[cc * Y_PITCH:cc * Y_PITCH + tm, :] = y[:, cc * LANES:(cc + 1) * LANES]

        sc_n = 16
        def scatter(i, c):
            toks = [tbl_ref[base + i * sc_n + u] for u in range(sc_n)]
            olds = [acc[toks[u]] for u in range(sc_n)]
            news = [olds[u] + ybuf[pl.ds(i * sc_n + u, D_MODEL // LANES, stride=Y_PITCH), :] for u in range(sc_n)]
            for u in range(sc_n):
                acc[toks[u]] = news[u]
            return c
        lax.fori_loop(0, lax.shift_right_logical(rows + (sc_n - 1), 4), scatter, 0)

    @pl.when(tau == pl.num_programs(0) - 1)
    def _():
        cp = pltpu.make_async_copy(acc, acc_out, sems.at[2])
        cp.start()
        cp.wait()


def _experts(te, tr, tf, pos, emeta, xc, xs, comb, wgu, wd):
    n_tok = comb.shape[0]
    n_tiles = te.shape[0]
    tm = MOE_TILE
    grid_spec = pltpu.PrefetchScalarGridSpec(
        num_scalar_prefetch=5,
        grid=(n_tiles,),
        in_specs=[pl.BlockSpec(memory_space=pl.ANY), pl.BlockSpec(memory_space=pl.ANY),
                  pl.BlockSpec((n_tok, LANES), lambda t, *_: (0, 0), pipeline_mode=pl.Buffered(1)),
                  pl.BlockSpec((1, D_MODEL, 2 * EXPERT_DIM), lambda t, te, *_: (te[t], 0, 0)),
                  pl.BlockSpec((1, EXPERT_DIM, D_MODEL), lambda t, te, *_: (te[t], 0, 0))],
        out_specs=pl.BlockSpec(memory_space=pl.ANY),
        scratch_shapes=[pltpu.SMEM((n_tiles * tm,), jnp.int32),
                        pltpu.VMEM((n_tok, D_MODEL // LANES, LANES), BF16),
                        pltpu.VMEM((n_tok, D_MODEL // LANES, LANES), F32),
                        pltpu.VMEM((tm * (D_MODEL // LANES), LANES), F32),
                        pltpu.VMEM((tm, LANES), F32),
                        pltpu.VMEM((D_MODEL // LANES * Y_PITCH, LANES), F32),
                        pltpu.VMEM((D_MODEL, 2 * EXPERT_DIM), BF16),
                        pltpu.VMEM((EXPERT_DIM, D_MODEL), BF16),
                        pltpu.SemaphoreType.DMA((3,))],
    )
    return pl.pallas_call(
        _experts_kernel,
        grid_spec=grid_spec,
        out_shape=jax.ShapeDtypeStruct((n_tok, D_MODEL // LANES, LANES), F32),
        compiler_params=pltpu.CompilerParams(dimension_semantics=("arbitrary",),
                                             vmem_limit_bytes=EXPERTS_VMEM_LIMIT),
        name="experts",
    )(te, tr, tf, pos, emeta, xc, xs, comb, wgu, wd)


def _final_kernel(acc_ref, h_ref, x1_ref, mod_ref, wsg_ref, wsd_ref, gpo_ref, out_ref):
    routed = jnp.concatenate([acc_ref[:, cc, :] for cc in range(D_MODEL // LANES)], axis=1)
    gs = _dot(h_ref[...], wsg_ref[...])
    act = _silu(gs[:, :SHARED_DIM]) * gs[:, SHARED_DIM:]
    f = routed + _dot(act.astype(BF16), wsd_ref[...])
    out_ref[...] = x1_ref[...] + mod_ref[0][5:6] * _rmsnorm(f, gpo_ref[...])


def _final(acc, tile0, h2, x1, mod, mod_row, wsg, wsd, gpo):
    n = h2.shape[0]
    tm = TOKEN_TILE
    const = lambda *shape: pl.BlockSpec(shape, lambda i: (0,) * len(shape))
    tok = lambda w: pl.BlockSpec((tm, w), lambda i: (i, 0))
    return pl.pallas_call(
        _final_kernel,
        grid=(n // tm,),
        in_specs=[pl.BlockSpec((tm, D_MODEL // LANES, LANES), lambda i: (i + tile0, 0, 0)),
                  tok(D_MODEL), tok(D_MODEL),
                  pl.BlockSpec((1, N_MOD, D_MODEL), lambda i: (mod_row(i), 0, 0)),
                  const(D_MODEL, 2 * SHARED_DIM), const(SHARED_DIM, D_MODEL), const(1, D_MODEL)],
        out_specs=tok(D_MODEL),
        out_shape=jax.ShapeDtypeStruct((n, D_MODEL), F32),
        compiler_params=_params("arbitrary"),
        name="final",
    )(acc, h2, x1, mod, wsg, wsd, gpo)


def _window_bounds(n, w):
    idx = np.arange(n)
    return np.clip(idx - w // 2, 0, n), np.clip(idx + w - w // 2, 0, n)


def _pool_operators(t, grid):
    mats, invs = [], []
    for w in POOL_WINDOWS:
        if grid:
            rlo, rhi = _window_bounds(t // GRID_W, w)
            clo, chi = _window_bounds(GRID_W, w)
            r = np.arange(t) // GRID_W
            c = np.arange(t) % GRID_W
            m = ((r[None, :] >= rlo[r][:, None]) & (r[None, :] < rhi[r][:, None])
                 & (c[None, :] >= clo[c][:, None]) & (c[None, :] < chi[c][:, None]))
            cnt = (rhi - rlo)[r] * (chi - clo)[c]
        else:
            lo, hi = _window_bounds(t, w)
            sidx = np.arange(t)
            m = (sidx[None, :] >= lo[:, None]) & (sidx[None, :] < hi[:, None])
            cnt = hi - lo
        mats.append(m.astype(np.float32))
        invs.append((1.0 / cnt.astype(np.float64)).astype(np.float32)[:, None])
    return jnp.asarray(np.stack(mats), BF16), jnp.asarray(np.stack(invs), F32)


def kernel(x_prompt, x_sample, state_C, state_n, state_m, c, c_ctx, w_ada, b_ada, g_pre_mix, w_in, b_gate,
           w_pool, pool_scale, w_out, g_post_mix, g_pre_ffn, w_router, b_router, w_expert_gu, w_expert_down,
           w_shared_gu, w_shared_down, g_post_ffn):
    b_ctx = x_prompt.shape[0]
    b_lat = x_sample.shape[0]
    nu = N_DIR * HEADS
    l = 0
    row = lambda a: a[l].reshape(1, -1).astype(F32)

    cvec = jnp.zeros((16, D_MODEL), F32).at[0].set(c_ctx.astype(F32)).at[1:1 + b_lat].set(c.astype(F32))
    mod = _mod_rows(cvec, w_ada[l], b_ada[l]).reshape(16, N_MOD, D_MODEL)

    w_in_l = w_in[l]
    p0 = POOL_WIDTH
    mw = MLSTM_WIDTH
    w_u, w_q, w_k, w_v, w_o = (w_in_l[:, lo:lo + 512] for lo in (0, p0, p0 + mw, p0 + 2 * mw, p0 + 3 * mw))
    wm = jnp.concatenate([w_u, w_k, w_o], axis=1).astype(BF16)
    wt = jnp.concatenate([w_q.T, w_v.T], axis=0).astype(BF16)
    wg_cols = w_in_l[:, p0 + 4 * mw:]
    wg = jnp.pad(wg_cols, ((0, 0), (0, LANES - GATE_COLS)))
    wgt = wg_cols.T
    bg = b_gate[l].reshape(GATE_COLS).astype(F32)
    bgr = jnp.pad(bg, (0, LANES - GATE_COLS)).reshape(1, LANES)
    bgc = bg.reshape(GATE_COLS, 1)
    wp = w_pool[l].astype(BF16)
    wo = w_out[l].astype(BF16)
    wrt = w_router[l].T
    wsg = w_shared_gu[l].astype(BF16)
    wsd = w_shared_down[l].astype(BF16)

    def mixer(x, mod_row, grid, s0, m0, emit_state):
        t = x.shape[1]
        u, k, o, qt, vt, gate, gate_t = _inproj(x.astype(F32), mod, mod_row, row(g_pre_mix), wm, wt, wg, wgt,
                                                bgr, bgc)
        outs = _mlstm(k, qt, vt, gate, gate_t, s0, m0, emit_state)
        hf, hb = outs[0], outs[1]
        pm, pinv = _pool_operators(t, grid)
        x1, h2, xp, lg = _outproj(x.astype(F32), u, hf, hb, o, mod, mod_row, pm, pinv, wp, row(pool_scale), wo,
                                  row(g_post_mix), row(g_pre_ffn), wrt)
        return x1, h2, xp, lg, outs[2:]

    ctx_row = lambda i: 0
    lat_row = lambda i: i + 1
    x1c, h2c, xpc, lgc, (c_new, n_new, m_new) = mixer(x_prompt, ctx_row, False, None, None, True)
    s0 = jnp.concatenate(
        [jnp.swapaxes(state_C[:, l].reshape(b_lat, nu, HEAD_DIM, HEAD_DIM).astype(F32), -1, -2),
         jnp.broadcast_to(state_n[:, l].reshape(b_lat, nu, 1, HEAD_DIM).astype(F32),
                          (b_lat, nu, HEAD_DIM, HEAD_DIM))], axis=-2)
    m0 = jnp.broadcast_to(state_m[:, l].reshape(b_lat, nu, 1, 1).astype(F32), (b_lat, nu, 1, LANES))
    x1s, h2s, xps, lgs, _ = mixer(x_sample, lat_row, True, s0, m0, False)

    tc = b_ctx * x_prompt.shape[1]
    ts = b_lat * x_sample.shape[1]
    n_tok = tc + ts
    lg_all = jnp.concatenate([lgc.transpose(1, 0, 2).reshape(N_EXPERTS, tc),
                              lgs.transpose(1, 0, 2).reshape(N_EXPERTS, ts)], axis=1)
    comb, sel = _router(lg_all, b_router[l].astype(F32))
    n_tiles = n_tok * TOP_K // MOE_TILE + N_EXPERTS
    n_meta = -(-n_tiles // LANES) * LANES
    pos, meta, emeta = _plan(sel, n_meta)

    comb_tok = jnp.pad(comb.T, ((0, 8), (0, LANES - N_EXPERTS)))
    slab = (D_MODEL // LANES, LANES)
    acc = _experts(meta[0, :n_tiles], meta[1, :n_tiles], meta[2, :n_tiles],
                   pos.T.reshape(-1), emeta.reshape(-1),
                   xpc.reshape((tc,) + slab), xps.reshape((ts,) + slab), comb_tok,
                   w_expert_gu[l], w_expert_down[l])

    fin = functools.partial(_final, wsg=wsg, wsd=wsd, gpo=row(g_post_ffn))
    tiles_per_lat = x_sample.shape[1] // TOKEN_TILE
    yc = fin(acc, 0, h2c.reshape(tc, D_MODEL), x1c.reshape(tc, D_MODEL), mod, ctx_row)
    ys = fin(acc, tc // TOKEN_TILE, h2s.reshape(ts, D_MODEL), x1s.reshape(ts, D_MODEL), mod,
             lambda i: i // tiles_per_lat + 1)

    new_c = c_new.reshape(b_ctx, 1, N_DIR, HEADS, HEAD_DIM, HEAD_DIM)
    new_n = n_new.reshape(b_ctx, 1, N_DIR, HEADS, HEAD_DIM)
    new_m = m_new[..., 0].reshape(b_ctx, 1, N_DIR, HEADS)
    return (yc.reshape(x_prompt.shape), ys.reshape(x_sample.shape), new_c, new_n, new_m)
```

```python
import functools

import jax
import jax.numpy as jnp
import numpy as np
from jax import lax
from jax.experimental import pallas as pl
from jax.experimental.pallas import tpu as pltpu

F32 = jnp.float32
BF16 = jnp.bfloat16

D_MODEL = 1024
GRID_W = 64
POOL_WIDTH = 512
POOL_GROUPS = 4
POOL_GROUP_DIM = 128
POOL_WINDOWS = (2, 4, 8, 16)
HEADS = 4
HEAD_DIM = 128
MLSTM_WIDTH = HEADS * HEAD_DIM
N_DIR = 2
GATE_COLS = N_DIR * 2 * HEADS
N_EXPERTS = 64
TOP_K = 6
N_EXPERT_GROUPS = 8
GROUP_SIZE = N_EXPERTS // N_EXPERT_GROUPS
TOPK_GROUPS = 4
EXPERT_DIM = 256
SHARED_DIM = 256
ROUTED_SCALE = 2.5
N_MOD = 6
EPS = 1e-6
K_SCALE = HEAD_DIM ** -0.5

LANES = 128
CHUNK = 256
TOKEN_TILE = 256
MOE_TILE = 256
Y_PITCH = MOE_TILE + 8
VMEM_LIMIT = 56 * 1024 * 1024
EXPERTS_VMEM_LIMIT = 58 * 1024 * 1024


def _split3(x):
    p1 = x.astype(BF16)
    r1 = x - p1.astype(F32)
    p2 = r1.astype(BF16)
    p3 = (r1 - p2.astype(F32)).astype(BF16)
    return p1, p2, p3


def _split2(x):
    p1 = x.astype(BF16)
    p2 = (x - p1.astype(F32)).astype(BF16)
    return p1, p2


def _dot(a, b):
    return jnp.dot(a, b, preferred_element_type=F32)


def _dot_nt(a, b):
    return lax.dot_general(a, b, (((1,), (1,)), ((), ())), preferred_element_type=F32)


def _rmsnorm(x, g):
    return x * lax.rsqrt(jnp.mean(x * x, axis=-1, keepdims=True) + EPS) * g


def _silu(x):
    return x * jax.nn.sigmoid(x)


def _params(*sem):
    return pltpu.CompilerParams(dimension_semantics=sem, vmem_limit_bytes=VMEM_LIMIT)


def _mod_kernel(c_ref, w_ref, b_ref, o_ref):
    a = _silu(c_ref[...])
    a_stack = jnp.concatenate(_split3(a), axis=0)
    w1, w2 = _split2(w_ref[...])
    r1 = _dot(a_stack, w1)
    r2 = _dot(a_stack[:32], w2)
    o_ref[...] = (r1[0:16] + r1[16:32] + r1[32:48] + r2[0:16] + r2[16:32]) + b_ref[...]


def _mod_rows(cvec, w_ada, b_ada):
    n = N_MOD * D_MODEL
    tn = 1536
    return pl.pallas_call(
        _mod_kernel,
        grid=(n // tn,),
        in_specs=[pl.BlockSpec((16, D_MODEL), lambda j: (0, 0)),
                  pl.BlockSpec((D_MODEL, tn), lambda j: (0, j)),
                  pl.BlockSpec((1, tn), lambda j: (0, j))],
        out_specs=pl.BlockSpec((16, tn), lambda j: (0, j)),
        out_shape=jax.ShapeDtypeStruct((16, n), F32),
        compiler_params=_params("arbitrary"),
        name="mod",
    )(cvec, w_ada, b_ada.reshape(1, n))


def _inproj_kernel(x_ref, mod_ref, g_ref, wm_ref, wt_ref, wg_ref, wgt_ref, bgr_ref, bgc_ref,
                   u_ref, k_ref, o_ref, qt_ref, vt_ref, gate_ref, gatet_ref):
    x = x_ref[0]
    mod = mod_ref[0]
    h = _rmsnorm(x, g_ref[...]) * (1.0 + mod[1:2]) + mod[0:1]
    h1, h2, h3 = _split3(h)
    z = _dot(h1, wm_ref[...])
    u_ref[0] = z[:, 0:512].astype(BF16)
    k_ref[0] = (z[:, 512:1024] * K_SCALE).astype(BF16)
    o_ref[0] = z[:, 1024:1536].astype(BF16)
    zt = _dot_nt(wt_ref[...], h1)
    qt_ref[0] = zt[0:512].astype(BF16)
    vt_ref[0] = zt[512:1024].astype(BF16)
    tm = x.shape[0]
    wg1, wg2 = _split2(wg_ref[...])
    hs = jnp.concatenate([h1, h2, h3], axis=0)
    r1 = _dot(hs, wg1)
    r2 = _dot(hs[:2 * tm], wg2)
    gate_ref[0] = (r1[0:tm] + r1[tm:2 * tm] + r1[2 * tm:] + r2[0:tm] + r2[tm:]) + bgr_ref[...]
    wt1, wt2 = _split2(wgt_ref[...])
    wts = jnp.concatenate([wt1, wt2], axis=0)
    t1 = _dot_nt(wts, h1)
    t2 = _dot_nt(wts, h2)
    t3 = _dot_nt(wt1, h3)
    gatet_ref[0] = (t1[0:16] + t1[16:32] + t2[0:16] + t2[16:32] + t3) + bgc_ref[...]


def _inproj(x, mod, mod_row, g, wm, wt, wg, wgt, bgr, bgc):
    b, t, _ = x.shape
    tm = TOKEN_TILE
    const = lambda *shape: pl.BlockSpec(shape, lambda i, j: (0,) * len(shape))
    tok = lambda w: pl.BlockSpec((1, tm, w), lambda i, j: (i, j, 0))
    tok_t = lambda r: pl.BlockSpec((1, r, tm), lambda i, j: (i, 0, j))
    sd = jax.ShapeDtypeStruct
    return pl.pallas_call(
        _inproj_kernel,
        grid=(b, t // tm),
        in_specs=[tok(D_MODEL),
                  pl.BlockSpec((1, N_MOD, D_MODEL), lambda i, j: (mod_row(i), 0, 0)),
                  const(1, D_MODEL), const(D_MODEL, 1536), const(1024, D_MODEL),
                  const(D_MODEL, LANES), const(16, D_MODEL), const(1, LANES), const(16, 1)],
        out_specs=[tok(512), tok(512), tok(512), tok_t(512), tok_t(512), tok(LANES), tok_t(16)],
        out_shape=[sd((b, t, 512), BF16), sd((b, t, 512), BF16), sd((b, t, 512), BF16),
                   sd((b, 512, t), BF16), sd((b, 512, t), BF16), sd((b, t, LANES), F32),
                   sd((b, 16, t), F32)],
        compiler_params=_params("arbitrary", "arbitrary"),
        name="inproj",
    )(x, mod, g, wm, wt, wg, wgt, bgr, bgc)


def _log_sigmoid(x):
    return jnp.minimum(x, 0.0) - jnp.log1p(jnp.exp(-jnp.abs(x)))


def _scan_unit(st, k, qt, vt, u_col, u_row, b_row, btot, mask, s_prev, m_prev, use_state):
    dh = HEAD_DIM
    n = st.shape[0]
    ub = jnp.where(mask, jnp.broadcast_to(u_col, (n, n)), -jnp.inf)
    z = jnp.maximum(m_prev, jnp.max(ub, axis=0, keepdims=True))
    p = (jnp.exp(ub - z) * st).astype(BF16)
    ones = jnp.ones((dh, n), BF16)
    tot = _dot(jnp.concatenate([vt, ones], axis=0), p)
    if use_state:
        tot = tot + jnp.exp(m_prev - z) * _dot(s_prev.astype(BF16), qt)
    floor = jnp.exp(-(b_row + z))
    h_t = tot[:dh] / jnp.maximum(jnp.abs(tot[dh:]), floor)
    g_row = btot + u_row
    m_new = jnp.maximum(btot + m_prev, jnp.max(g_row, axis=-1, keepdims=True))
    w_row = jnp.exp(g_row - m_new)
    vw = jnp.concatenate([(vt.astype(F32) * w_row).astype(BF16),
                          jnp.broadcast_to(w_row, (dh, n)).astype(BF16)], axis=0)
    s_new = jnp.exp(btot + m_prev - m_new) * s_prev + _dot(vw, k)
    return h_t.T, s_new, m_new


def _mlstm_kernel(*refs, nc, zero_init, emit_state):
    it = iter(refs)
    fwd_refs = tuple(next(it) for _ in range(5))
    bwd_refs = tuple(next(it) for _ in range(5)) if nc > 1 else fwd_refs
    if not zero_init:
        s0_ref, m0_ref = next(it), next(it)
    h_refs = (next(it), next(it))
    if emit_state:
        c_out, n_out, m_out = next(it), next(it), next(it)
    s_scr, m_scr = next(it), next(it)

    j = pl.program_id(1)
    n = CHUNK
    dh = HEAD_DIM

    @pl.when(j == 0)
    def _():
        if zero_init:
            s_scr[...] = jnp.zeros_like(s_scr)
            m_scr[...] = jnp.zeros_like(m_scr)
        else:
            s_scr[...] = s0_ref[0]
            m_scr[...] = m0_ref[0]

    rows = lax.broadcasted_iota(jnp.int32, (n, n), 0)
    cols = lax.broadcasted_iota(jnp.int32, (n, n), 1)
    le = rows <= cols
    ge = rows >= cols
    tri_le = le.astype(BF16)
    tri_ge = ge.astype(BF16)
    use_state = not (zero_init and nc == 1)

    def gate_terms(d):
        g_ref, gt_ref = (fwd_refs, bwd_refs)[d][3:5]
        gate = g_ref[0]
        gate_t = gt_ref[0]
        lf = _log_sigmoid(gate)
        lf_t = _log_sigmoid(gate_t)
        tri_c, tri_r = (tri_ge, tri_le) if d == 0 else (tri_le, tri_ge)
        bc = _dot(tri_c, jnp.concatenate(_split3(lf), axis=1))
        b_cols = bc[:, 0:128] + bc[:, 128:256] + bc[:, 256:384]
        br = _dot(jnp.concatenate(_split3(lf_t), axis=0), tri_r)
        b_rows = br[0:16] + br[16:32] + br[32:48]
        return gate, gate_t, b_cols, b_rows, jnp.sum(lf_t, axis=-1, keepdims=True)

    terms = [gate_terms(0), gate_terms(1)]
    hs = ([], [])
    for hd in range(HEADS):
        hsl = slice(hd * dh, (hd + 1) * dh)
        st = None
        for d in range(N_DIR):
            k_ref, qt_ref, vt_ref = (fwd_refs, bwd_refs)[d][0:3]
            gate, gate_t, b_cols, b_rows, tot_rows = terms[d]
            ci = d * 8 + hd
            cf = d * 8 + 4 + hd
            unit = d * HEADS + hd
            k = k_ref[0, :, hsl]
            qt = qt_ref[0, hsl, :]
            if st is None or nc > 1:
                st = _dot(k, qt)
            mask = le if d == 0 else ge
            h, s_new, m_new = _scan_unit(
                st, k, qt, vt_ref[0, hsl, :],
                gate[:, ci:ci + 1] - b_cols[:, cf:cf + 1],
                gate_t[ci:ci + 1, :] - b_rows[cf:cf + 1, :],
                b_rows[cf:cf + 1, :], tot_rows[cf:cf + 1, :],
                mask, s_scr[unit], m_scr[unit][:, 0:1], use_state)
            s_scr[unit] = s_new
            m_scr[unit] = jnp.broadcast_to(m_new, (1, LANES))
            hs[d].append(h)
    for d in range(N_DIR):
        h_refs[d][0] = jnp.concatenate(hs[d], axis=1).astype(BF16)

    if emit_state:
        @pl.when(j == nc - 1)
        def _():
            for unit in range(N_DIR * HEADS):
                s = s_scr[unit]
                c_out[0, unit] = s[:dh].T
                n_out[0, unit] = s[dh:dh + 1]
                m_out[0, unit] = m_scr[unit]


def _mlstm(k, qt, vt, gate, gate_t, s0, m0, emit_state):
    b, t, _ = k.shape
    nc = t // CHUNK
    zero_init = s0 is None
    nu = N_DIR * HEADS
    fwd = lambda w: pl.BlockSpec((1, CHUNK, w), lambda i, j: (i, j, 0))
    bwd = lambda w: pl.BlockSpec((1, CHUNK, w), lambda i, j: (i, nc - 1 - j, 0))
    fwd_t = lambda r: pl.BlockSpec((1, r, CHUNK), lambda i, j: (i, 0, j))
    bwd_t = lambda r: pl.BlockSpec((1, r, CHUNK), lambda i, j: (i, 0, nc - 1 - j))
    args = [k, qt, vt, gate, gate_t]
    in_specs = [fwd(512), fwd_t(512), fwd_t(512), fwd(LANES), fwd_t(16)]
    if nc > 1:
        args += [k, qt, vt, gate, gate_t]
        in_specs += [bwd(512), bwd_t(512), bwd_t(512), bwd(LANES), bwd_t(16)]
    if not zero_init:
        args += [s0, m0]
        in_specs += [pl.BlockSpec((1, nu, 2 * HEAD_DIM, HEAD_DIM), lambda i, j: (i, 0, 0, 0)),
                     pl.BlockSpec((1, nu, 1, LANES), lambda i, j: (i, 0, 0, 0))]
    sd = jax.ShapeDtypeStruct
    out_shape = [sd((b, t, 512), BF16), sd((b, t, 512), BF16)]
    out_specs = [fwd(512), bwd(512)]
    if emit_state:
        out_shape += [sd((b, nu, HEAD_DIM, HEAD_DIM), F32), sd((b, nu, 1, HEAD_DIM), F32),
                      sd((b, nu, 1, LANES), F32)]
        out_specs += [pl.BlockSpec((1, nu, HEAD_DIM, HEAD_DIM), lambda i, j: (i, 0, 0, 0)),
                      pl.BlockSpec((1, nu, 1, HEAD_DIM), lambda i, j: (i, 0, 0, 0)),
                      pl.BlockSpec((1, nu, 1, LANES), lambda i, j: (i, 0, 0, 0))]
    return pl.pallas_call(
        functools.partial(_mlstm_kernel, nc=nc, zero_init=zero_init, emit_state=emit_state),
        grid=(b, nc),
        in_specs=in_specs,
        out_specs=out_specs,
        out_shape=out_shape,
        scratch_shapes=[pltpu.VMEM((nu, 2 * HEAD_DIM, HEAD_DIM), F32),
                        pltpu.VMEM((nu, 1, LANES), F32)],
        compiler_params=_params("arbitrary", "arbitrary"),
        name="mlstm",
    )(*args)


def _outproj_kernel(x_ref, u_ref, hf_ref, hb_ref, o_ref, mod_ref, pm_ref, pinv_ref, wp_ref, ps_ref,
                    wo_ref, gpm_ref, gpf_ref, wrt_ref, x1_ref, h2_ref, xp_ref, lg_ref, slab_ref, *, tm):
    i = pl.program_id(1)
    x = x_ref[0]
    mod = mod_ref[0]
    row0 = pl.multiple_of(i * tm, tm)
    u_full = u_ref[0]
    u_tile = u_ref[0, pl.ds(row0, tm), :].astype(F32)
    mix = jnp.zeros((tm, D_MODEL), F32)
    for g in range(POOL_GROUPS):
        sl = slice(g * POOL_GROUP_DIM, (g + 1) * POOL_GROUP_DIM)
        box = _dot(pm_ref[g], u_full[:, sl])
        diff = box * pinv_ref[g] - u_tile[:, sl]
        yp = _dot(diff.astype(BF16), wp_ref[g]) * ps_ref[:, sl]
        mix = mix + _dot(yp.astype(BF16), wo_ref[sl, :])
    hsum = hf_ref[0].astype(F32) + hb_ref[0].astype(F32)
    yml = jax.nn.sigmoid(o_ref[0].astype(F32)) * hsum
    mix = mix + _dot(yml.astype(BF16), wo_ref[POOL_WIDTH:, :])
    x1 = x + mod[2:3] * _rmsnorm(mix, gpm_ref[...])
    x1_ref[0] = x1
    h2 = _rmsnorm(x1, gpf_ref[...]) * (1.0 + mod[4:5]) + mod[3:4]
    p1, p2, p3 = _split3(h2)
    h2_ref[0] = p1
    for cc in range(D_MODEL // LANES):
        slab_ref[:, cc, :] = h2[:, cc * LANES:(cc + 1) * LANES]
    xp_ref[0] = slab_ref[...].astype(BF16)
    w1, w2, w3 = _split3(wrt_ref[...])
    ws = jnp.concatenate([w1, w2, w3], axis=0)
    r1 = _dot_nt(ws, p1)
    r2 = _dot_nt(ws[:128], p2)
    r3 = _dot_nt(w1, p3)
    lg_ref[0] = r1[0:64] + r1[64:128] + r1[128:192] + r2[0:64] + r2[64:128] + r3


def _outproj(x, u, hf, hb, o, mod, mod_row, pm, pinv, wp, ps, wo, gpm, gpf, wrt):
    b, t, _ = x.shape
    tm = TOKEN_TILE
    const = lambda *shape: pl.BlockSpec(shape, lambda i, j: (0,) * len(shape))
    tok = lambda w: pl.BlockSpec((1, tm, w), lambda i, j: (i, j, 0))
    sd = jax.ShapeDtypeStruct
    return pl.pallas_call(
        functools.partial(_outproj_kernel, tm=tm),
        grid=(b, t // tm),
        in_specs=[tok(D_MODEL),
                  pl.BlockSpec((1, t, 512), lambda i, j: (i, 0, 0)),
                  tok(512), tok(512), tok(512),
                  pl.BlockSpec((1, N_MOD, D_MODEL), lambda i, j: (mod_row(i), 0, 0)),
                  pl.BlockSpec((POOL_GROUPS, tm, t), lambda i, j: (0, j, 0)),
                  pl.BlockSpec((POOL_GROUPS, tm, 1), lambda i, j: (0, j, 0)),
                  const(POOL_GROUPS, POOL_GROUP_DIM, POOL_GROUP_DIM), const(1, POOL_WIDTH),
                  const(D_MODEL, D_MODEL), const(1, D_MODEL), const(1, D_MODEL),
                  const(N_EXPERTS, D_MODEL)],
        out_specs=[tok(D_MODEL), tok(D_MODEL),
                   pl.BlockSpec((1, tm, D_MODEL // LANES, LANES), lambda i, j: (i, j, 0, 0)),
                   pl.BlockSpec((1, N_EXPERTS, tm), lambda i, j: (i, 0, j))],
        out_shape=[sd((b, t, D_MODEL), F32), sd((b, t, D_MODEL), BF16),
                   sd((b, t, D_MODEL // LANES, LANES), BF16), sd((b, N_EXPERTS, t), F32)],
        scratch_shapes=[pltpu.VMEM((tm, D_MODEL // LANES, LANES), F32)],
        compiler_params=_params("arbitrary", "arbitrary"),
        name="outproj",
    )(x, u, hf, hb, o, mod, pm, pinv, wp, ps, wo, gpm, gpf, wrt)


def _router_kernel(lg_ref, br_ref, comb_ref, sel_ref):
    s = jax.nn.sigmoid(lg_ref[...])
    biased = s + br_ref[...]
    gidx = lax.broadcasted_iota(jnp.int32, s.shape, 0)
    jidx = lax.broadcasted_iota(jnp.int32, s.shape, 1)
    neg = -jnp.inf
    m1 = jnp.max(biased, axis=1, keepdims=True)
    i1 = jnp.min(jnp.where(biased == m1, jidx, GROUP_SIZE), axis=1, keepdims=True)
    m2 = jnp.max(jnp.where(jidx == i1, neg, biased), axis=1, keepdims=True)
    gscore = m1 + m2
    gi = lax.broadcasted_iota(jnp.int32, gscore.shape, 0)
    gmask = jnp.zeros(gscore.shape, F32)
    cur = gscore
    for _ in range(TOPK_GROUPS):
        mx = jnp.max(cur, axis=0, keepdims=True)
        ix = jnp.min(jnp.where(cur == mx, gi, N_EXPERT_GROUPS), axis=0, keepdims=True)
        hit = gi == ix
        gmask = jnp.where(hit, 1.0, gmask)
        cur = jnp.where(hit, neg, cur)
    cur = jnp.where(gmask > 0, biased, neg)
    eidx = gidx * GROUP_SIZE + jidx
    selmask = jnp.zeros(s.shape, F32)
    for _ in range(TOP_K):
        mx = jnp.max(jnp.max(cur, axis=1, keepdims=True), axis=0, keepdims=True)
        ix = jnp.where(cur == mx, eidx, N_EXPERTS)
        ix = jnp.min(jnp.min(ix, axis=1, keepdims=True), axis=0, keepdims=True)
        hit = eidx == ix
        selmask = jnp.where(hit, 1.0, selmask)
        cur = jnp.where(hit, neg, cur)
    sel = selmask * s
    tot = jnp.sum(jnp.sum(sel, axis=1, keepdims=True), axis=0, keepdims=True)
    comb_ref[...] = sel / tot * ROUTED_SCALE
    sel_ref[...] = selmask


def _router(logits_t, b_router):
    t = logits_t.shape[1]
    tl = 1024
    shp = (N_EXPERT_GROUPS, GROUP_SIZE, t)
    blk = pl.BlockSpec((N_EXPERT_GROUPS, GROUP_SIZE, tl), lambda j: (0, 0, j))
    comb, sel = pl.pallas_call(
        _router_kernel,
        grid=(t // tl,),
        in_specs=[blk, pl.BlockSpec((N_EXPERT_GROUPS, GROUP_SIZE, 1), lambda j: (0, 0, 0))],
        out_specs=[blk, blk],
        out_shape=[jax.ShapeDtypeStruct(shp, F32), jax.ShapeDtypeStruct(shp, F32)],
        compiler_params=_params("arbitrary"),
        name="router",
    )(logits_t.reshape(shp), b_router.reshape(N_EXPERT_GROUPS, GROUP_SIZE, 1))
    return comb.reshape(N_EXPERTS, t), sel.reshape(N_EXPERTS, t)


def _plan_kernel(sel_ref, pos_ref, meta_ref, emeta_ref, *, n_meta):
    t = sel_ref.shape[1]
    tm = float(MOE_TILE)
    sel = sel_ref[...]
    selb = sel.astype(BF16)
    blk = 256
    rr = lax.broadcasted_iota(jnp.int32, (blk, blk), 0)
    cc = lax.broadcasted_iota(jnp.int32, (blk, blk), 1)
    before = (rr < cc).astype(BF16)
    carry = jnp.zeros((N_EXPERTS, 1), F32)
    ranks = []
    for b in range(t // blk):
        sb = selb[:, b * blk:(b + 1) * blk]
        ranks.append(_dot(sb, before) + carry)
        carry = carry + jnp.sum(sel[:, b * blk:(b + 1) * blk], axis=1, keepdims=True)
    rank = jnp.concatenate(ranks, axis=1)
    cnt = carry
    ntile = jnp.floor((cnt + (tm - 1.0)) * (1.0 / tm))
    er = lax.broadcasted_iota(jnp.int32, (N_EXPERTS, N_EXPERTS), 0)
    ec = lax.broadcasted_iota(jnp.int32, (N_EXPERTS, N_EXPERTS), 1)
    below = (ec < er).astype(BF16)
    tstart = _dot(below, jnp.broadcast_to(ntile, (N_EXPERTS, LANES)).astype(BF16))[:, 0:1]
    pos = tstart * tm + rank
    erank = _dot(below, selb)
    rows = []
    for k in range(TOP_K):
        hit = (sel > 0.0) & (erank == float(k))
        rows.append(jnp.sum(jnp.where(hit, pos, 0.0), axis=0, keepdims=True))
    rows += [jnp.zeros((1, t), F32)] * (8 - TOP_K)
    pos_ref[...] = jnp.concatenate(rows, axis=0).astype(jnp.int32)

    tau = lax.broadcasted_iota(jnp.int32, (N_EXPERTS, n_meta), 1).astype(F32)
    eidx = lax.broadcasted_iota(jnp.int32, (N_EXPERTS, n_meta), 0).astype(F32)
    te = jnp.sum(((tstart + ntile) <= tau).astype(F32), axis=0, keepdims=True)
    te = jnp.minimum(te, float(N_EXPERTS - 1))
    onehot = eidx == te
    cnt_t = jnp.sum(jnp.where(onehot, cnt, 0.0), axis=0, keepdims=True)
    ts_t = jnp.sum(jnp.where(onehot, tstart, 0.0), axis=0, keepdims=True)
    tr = jnp.clip(cnt_t - (tau[0:1] - ts_t) * tm, 0.0, tm)
    tf = jnp.where((tau[0:1] == ts_t) & (tr > 0.0), 1.0, 0.0)
    meta_ref[...] = jnp.concatenate([te, tr, tf] + [jnp.zeros((1, n_meta), F32)] * 5, axis=0).astype(jnp.int32)

    eye = (lax.broadcasted_iota(jnp.int32, (N_EXPERTS, LANES), 0)
           == lax.broadcasted_iota(jnp.int32, (N_EXPERTS, LANES), 1))
    as_row = lambda col: jnp.sum(jnp.where(eye, col, 0.0), axis=0, keepdims=True)
    emeta_ref[...] = jnp.concatenate([as_row(cnt), as_row(tstart), as_row(ntile)]
                                     + [jnp.zeros((1, LANES), F32)] * 5, axis=0).astype(jnp.int32)


def _plan(sel, n_meta):
    t = sel.shape[1]
    sd = jax.ShapeDtypeStruct
    return pl.pallas_call(
        functools.partial(_plan_kernel, n_meta=n_meta),
        out_shape=[sd((8, t), jnp.int32), sd((8, n_meta), jnp.int32), sd((8, LANES), jnp.int32)],
        compiler_params=pltpu.CompilerParams(vmem_limit_bytes=VMEM_LIMIT),
        name="plan",
    )(sel)


def _fill_slot_table(pos_ref, emeta_ref, tbl_ref, n_tokens):
    group = 32
    def pad_expert(e, carry):
        n_t = emeta_ref[2 * LANES + e]
        start = (emeta_ref[LANES + e] + n_t - 1) * MOE_TILE

        @pl.when(n_t > 0)
        def _():
            def put(i, c):
                for u in range(group):
                    tbl_ref[start + i * group + u] = n_tokens
                return c
            lax.fori_loop(0, MOE_TILE // group, put, 0)
        return carry
    lax.fori_loop(0, N_EXPERTS, pad_expert, 0)

    unroll = 8
    def scatter(i, carry):
        first = i * (8 * unroll)
        for u in range(unroll):
            for k in range(TOP_K):
                tbl_ref[pos_ref[first + (8 * u + k)]] = i * unroll + u
        return carry
    lax.fori_loop(0, n_tokens // unroll, scatter, 0)


def _experts_kernel(te_ref, tr_ref, tf_ref, pos_ref, emeta_ref, xc_ref, xs_ref, comb_ref, wgu_ref, wd_ref, acc_out,
                    tbl_ref, xbuf, acc, stage, cstage, ybuf, wgu_b, wd_b, sems):
    tau = pl.program_id(0)
    e = te_ref[tau]
    rows = tr_ref[tau]
    tm = MOE_TILE
    nch = D_MODEL // LANES

    @pl.when(tau == 0)
    def _():
        tc, ts = xc_ref.shape[0], xs_ref.shape[0]
        copies = (pltpu.make_async_copy(xc_ref, xbuf.at[pl.ds(0, tc)], sems.at[0]),
                  pltpu.make_async_copy(xs_ref, xbuf.at[pl.ds(tc, ts)], sems.at[1]))
        for cp in copies:
            cp.start()
        n_pad = xbuf.shape[0] - tc - ts
        xbuf[pl.ds(tc + ts, n_pad)] = jnp.zeros((n_pad,) + xbuf.shape[1:], BF16)
        acc[...] = jnp.zeros_like(acc)
        stage[...] = jnp.zeros_like(stage)
        cstage[...] = jnp.zeros_like(cstage)
        _fill_slot_table(pos_ref, emeta_ref, tbl_ref, tc + ts)
        for cp in copies:
            cp.wait()

    @pl.when(tf_ref[tau] == 1)
    def _():
        wgu_b[...] = wgu_ref[0].astype(BF16)
        wd_b[...] = wd_ref[0].astype(BF16)

    @pl.when(rows > 0)
    def _():
        base = tau * tm
        gu_n = 16
        def gather(i, c):
            for u in range(gu_n):
                j = i * gu_n + u
                tok = tbl_ref[base + j]
                stage[pl.ds(pl.multiple_of(j * nch, nch), nch), :] = xbuf[tok].astype(F32)
                cstage[pl.ds(j, 1), :] = comb_ref[pl.ds(tok, 1), :]
            return c
        lax.fori_loop(0, lax.shift_right_logical(rows + (gu_n - 1), 4), gather, 0)

        xb = jnp.concatenate([stage[pl.ds(cc, tm, stride=nch), :] for cc in range(nch)], axis=1).astype(BF16)
        gu = _dot(xb, wgu_b[...])
        comb3 = jnp.concatenate(_split3(cstage[...]), axis=1)
        sel_row = lax.broadcasted_iota(jnp.int32, (3 * LANES, EXPERT_DIM), 0) % LANES
        cexp = _dot(comb3, (sel_row == e).astype(BF16))
        act = (_silu(gu[:, :EXPERT_DIM]) * gu[:, EXPERT_DIM:] * cexp).astype(BF16)
        y = _dot(act, wd_b[...])
        for cc in range(D_MODEL // LANES):
            ybuf[cc * Y_PITCH:cc * Y_PITCH + tm, :] = y[:, cc * LANES:(cc + 1) * LANES]

        sc_n = 16
        def scatter(i, c):
            toks = [tbl_ref[base + i * sc_n + u] for u in range(sc_n)]
            olds = [acc[toks[u]] for u in range(sc_n)]
            news = [olds[u] + ybuf[pl.ds(i * sc_n + u, D_MODEL // LANES, stride=Y_PITCH), :] for u in range(sc_n)]
            for u in range(sc_n):
                acc[toks[u]] = news[u]
            return c
        lax.fori_loop(0, lax.shift_right_logical(rows + (sc_n - 1), 4), scatter, 0)

    @pl.when(tau == pl.num_programs(0) - 1)
    def _():
        cp = pltpu.make_async_copy(acc, acc_out, sems.at[2])
        cp.start()
        cp.wait()


def _experts(te, tr, tf, pos, emeta, xc, xs, comb, wgu, wd):
    n_tok = comb.shape[0]
    n_tiles = te.shape[0]
    tm = MOE_TILE
    grid_spec = pltpu.PrefetchScalarGridSpec(
        num_scalar_prefetch=5,
        grid=(n_tiles,),
        in_specs=[pl.BlockSpec(memory_space=pl.ANY), pl.BlockSpec(memory_space=pl.ANY),
                  pl.BlockSpec((n_tok, LANES), lambda t, *_: (0, 0), pipeline_mode=pl.Buffered(1)),
                  pl.BlockSpec((1, D_MODEL, 2 * EXPERT_DIM), lambda t, te, *_: (te[t], 0, 0)),
                  pl.BlockSpec((1, EXPERT_DIM, D_MODEL), lambda t, te, *_: (te[t], 0, 0))],
        out_specs=pl.BlockSpec(memory_space=pl.ANY),
        scratch_shapes=[pltpu.SMEM((n_tiles * tm,), jnp.int32),
                        pltpu.VMEM((n_tok, D_MODEL // LANES, LANES), BF16),
                        pltpu.VMEM((n_tok, D_MODEL // LANES, LANES), F32),
                        pltpu.VMEM((tm * (D_MODEL // LANES), LANES), F32),
                        pltpu.VMEM((tm, LANES), F32),
                        pltpu.VMEM((D_MODEL // LANES * Y_PITCH, LANES), F32),
                        pltpu.VMEM((D_MODEL, 2 * EXPERT_DIM), BF16),
                        pltpu.VMEM((EXPERT_DIM, D_MODEL), BF16),
                        pltpu.SemaphoreType.DMA((3,))],
    )
    return pl.pallas_call(
        _experts_kernel,
        grid_spec=grid_spec,
        out_shape=jax.ShapeDtypeStruct((n_tok, D_MODEL // LANES, LANES), F32),
        compiler_params=pltpu.CompilerParams(dimension_semantics=("arbitrary",),
                                             vmem_limit_bytes=EXPERTS_VMEM_LIMIT),
        name="experts",
    )(te, tr, tf, pos, emeta, xc, xs, comb, wgu, wd)


def _final_kernel(acc_ref, h_ref, x1_ref, mod_ref, wsg_ref, wsd_ref, gpo_ref, out_ref):
    routed = jnp.concatenate([acc_ref[:, cc, :] for cc in range(D_MODEL // LANES)], axis=1)
    gs = _dot(h_ref[...], wsg_ref[...])
    act = _silu(gs[:, :SHARED_DIM]) * gs[:, SHARED_DIM:]
    f = routed + _dot(act.astype(BF16), wsd_ref[...])
    out_ref[...] = x1_ref[...] + mod_ref[0][5:6] * _rmsnorm(f, gpo_ref[...])


def _final(acc, tile0, h2, x1, mod, mod_row, wsg, wsd, gpo):
    n = h2.shape[0]
    tm = TOKEN_TILE
    const = lambda *shape: pl.BlockSpec(shape, lambda i: (0,) * len(shape))
    tok = lambda w: pl.BlockSpec((tm, w), lambda i: (i, 0))
    return pl.pallas_call(
        _final_kernel,
        grid=(n // tm,),
        in_specs=[pl.BlockSpec((tm, D_MODEL // LANES, LANES), lambda i: (i + tile0, 0, 0)),
                  tok(D_MODEL), tok(D_MODEL),
                  pl.BlockSpec((1, N_MOD, D_MODEL), lambda i: (mod_row(i), 0, 0)),
                  const(D_MODEL, 2 * SHARED_DIM), const(SHARED_DIM, D_MODEL), const(1, D_MODEL)],
        out_specs=tok(D_MODEL),
        out_shape=jax.ShapeDtypeStruct((n, D_MODEL), F32),
        compiler_params=_params("arbitrary"),
        name="final",
    )(acc, h2, x1, mod, wsg, wsd, gpo)


def _window_bounds(n, w):
    idx = np.arange(n)
    return np.clip(idx - w // 2, 0, n), np.clip(idx + w - w // 2, 0, n)


def _pool_operators(t, grid):
    mats, invs = [], []
    for w in POOL_WINDOWS:
        if grid:
            rlo, rhi = _window_bounds(t // GRID_W, w)
            clo, chi = _window_bounds(GRID_W, w)
            r = np.arange(t) // GRID_W
            c = np.arange(t) % GRID_W
            m = ((r[None, :] >= rlo[r][:, None]) & (r[None, :] < rhi[r][:, None])
                 & (c[None, :] >= clo[c][:, None]) & (c[None, :] < chi[c][:, None]))
            cnt = (rhi - rlo)[r] * (chi - clo)[c]
        else:
            lo, hi = _window_bounds(t, w)
            sidx = np.arange(t)
            m = (sidx[None, :] >= lo[:, None]) & (sidx[None, :] < hi[:, None])
            cnt = hi - lo
        mats.append(m.astype(np.float32))
        invs.append((1.0 / cnt.astype(np.float64)).astype(np.float32)[:, None])
    return jnp.asarray(np.stack(mats), BF16), jnp.asarray(np.stack(invs), F32)


def kernel(x_prompt, x_sample, state_C, state_n, state_m, c, c_ctx, w_ada, b_ada, g_pre_mix, w_in, b_gate,
           w_pool, pool_scale, w_out, g_post_mix, g_pre_ffn, w_router, b_router, w_expert_gu, w_expert_down,
           w_shared_gu, w_shared_down, g_post_ffn):
    b_ctx = x_prompt.shape[0]
    b_lat = x_sample.shape[0]
    nu = N_DIR * HEADS
    l = 0
    row = lambda a: a[l].reshape(1, -1).astype(F32)

    cvec = jnp.zeros((16, D_MODEL), F32).at[0].set(c_ctx.astype(F32)).at[1:1 + b_lat].set(c.astype(F32))
    mod = _mod_rows(cvec, w_ada[l], b_ada[l]).reshape(16, N_MOD, D_MODEL)

    w_in_l = w_in[l]
    p0 = POOL_WIDTH
    mw = MLSTM_WIDTH
    w_u, w_q, w_k, w_v, w_o = (w_in_l[:, lo:lo + 512] for lo in (0, p0, p0 + mw, p0 + 2 * mw, p0 + 3 * mw))
    wm = jnp.concatenate([w_u, w_k, w_o], axis=1).astype(BF16)
    wt = jnp.concatenate([w_q.T, w_v.T], axis=0).astype(BF16)
    wg_cols = w_in_l[:, p0 + 4 * mw:]
    wg = jnp.pad(wg_cols, ((0, 0), (0, LANES - GATE_COLS)))
    wgt = wg_cols.T
    bg = b_gate[l].reshape(GATE_COLS).astype(F32)
    bgr = jnp.pad(bg, (0, LANES - GATE_COLS)).reshape(1, LANES)
    bgc = bg.reshape(GATE_COLS, 1)
    wp = w_pool[l].astype(BF16)
    wo = w_out[l].astype(BF16)
    wrt = w_router[l].T
    wsg = w_shared_gu[l].astype(BF16)
    wsd = w_shared_down[l].astype(BF16)

    def mixer(x, mod_row, grid, s0, m0, emit_state):
        t = x.shape[1]
        u, k, o, qt, vt, gate, gate_t = _inproj(x.astype(F32), mod, mod_row, row(g_pre_mix), wm, wt, wg, wgt,
                                                bgr, bgc)
        outs = _mlstm(k, qt, vt, gate, gate_t, s0, m0, emit_state)
        hf, hb = outs[0], outs[1]
        pm, pinv = _pool_operators(t, grid)
        x1, h2, xp, lg = _outproj(x.astype(F32), u, hf, hb, o, mod, mod_row, pm, pinv, wp, row(pool_scale), wo,
                                  row(g_post_mix), row(g_pre_ffn), wrt)
        return x1, h2, xp, lg, outs[2:]

    ctx_row = lambda i: 0
    lat_row = lambda i: i + 1
    x1c, h2c, xpc, lgc, (c_new, n_new, m_new) = mixer(x_prompt, ctx_row, False, None, None, True)
    s0 = jnp.concatenate(
        [jnp.swapaxes(state_C[:, l].reshape(b_lat, nu, HEAD_DIM, HEAD_DIM).astype(F32), -1, -2),
         jnp.broadcast_to(state_n[:, l].reshape(b_lat, nu, 1, HEAD_DIM).astype(F32),
                          (b_lat, nu, HEAD_DIM, HEAD_DIM))], axis=-2)
    m0 = jnp.broadcast_to(state_m[:, l].reshape(b_lat, nu, 1, 1).astype(F32), (b_lat, nu, 1, LANES))
    x1s, h2s, xps, lgs, _ = mixer(x_sample, lat_row, True, s0, m0, False)

    tc = b_ctx * x_prompt.shape[1]
    ts = b_lat * x_sample.shape[1]
    n_tok = tc + ts
    lg_all = jnp.concatenate([lgc.transpose(1, 0, 2).reshape(N_EXPERTS, tc),
                              lgs.transpose(1, 0, 2).reshape(N_EXPERTS, ts)], axis=1)
    comb, sel = _router(lg_all, b_router[l].astype(F32))
    n_tiles = n_tok * TOP_K // MOE_TILE + N_EXPERTS
    n_meta = -(-n_tiles // LANES) * LANES
    pos, meta, emeta = _plan(sel, n_meta)

    comb_tok = jnp.pad(comb.T, ((0, 8), (0, LANES - N_EXPERTS)))
    slab = (D_MODEL // LANES, LANES)
    acc = _experts(meta[0, :n_tiles], meta[1, :n_tiles], meta[2, :n_tiles],
                   pos.T.reshape(-1), emeta.reshape(-1),
                   xpc.reshape((tc,) + slab), xps.reshape((ts,) + slab), comb_tok,
                   w_expert_gu[l], w_expert_down[l])

    fin = functools.partial(_final, wsg=wsg, wsd=wsd, gpo=row(g_post_ffn))
    tiles_per_lat = x_sample.shape[1] // TOKEN_TILE
    yc = fin(acc, 0, h2c.reshape(tc, D_MODEL), x1c.reshape(tc, D_MODEL), mod, ctx_row)
    ys = fin(acc, tc // TOKEN_TILE, h2s.reshape(ts, D_MODEL), x1s.reshape(ts, D_MODEL), mod,
             lambda i: i // tiles_per_lat + 1)

    new_c = c_new.reshape(b_ctx, 1, N_DIR, HEADS, HEAD_DIM, HEAD_DIM)
    new_n = n_new.reshape(b_ctx, 1, N_DIR, HEADS, HEAD_DIM)
    new_m = m_new[..., 0].reshape(b_ctx, 1, N_DIR, HEADS)
    return (yc.reshape(x_prompt.shape), ys.reshape(x_sample.shape), new_c, new_n, new_m)
```

```python
import functools

import jax
import jax.numpy as jnp
import numpy as np
from jax import lax
from jax.experimental import pallas as pl
from jax.experimental.pallas import tpu as pltpu

F32 = jnp.float32
BF16 = jnp.bfloat16

D_MODEL = 1024
GRID_W = 64
POOL_WIDTH = 512
POOL_GROUPS = 4
POOL_GROUP_DIM = 128
POOL_WINDOWS = (2, 4, 8, 16)
HEADS = 4
HEAD_DIM = 128
MLSTM_WIDTH = HEADS * HEAD_DIM
N_DIR = 2
GATE_COLS = N_DIR * 2 * HEADS
N_EXPERTS = 64
TOP_K = 6
N_EXPERT_GROUPS = 8
GROUP_SIZE = N_EXPERTS // N_EXPERT_GROUPS
TOPK_GROUPS = 4
EXPERT_DIM = 256
SHARED_DIM = 256
ROUTED_SCALE = 2.5
N_MOD = 6
EPS = 1e-6
K_SCALE = HEAD_DIM ** -0.5

LANES = 128
CHUNK = 256
TOKEN_TILE = 256
MOE_TILE = 256
Y_PITCH = MOE_TILE + 8
VMEM_LIMIT = 56 * 1024 * 1024
EXPERTS_VMEM_LIMIT = 58 * 1024 * 1024


def _split3(x):
    p1 = x.astype(BF16)
    r1 = x - p1.astype(F32)
    p2 = r1.astype(BF16)
    p3 = (r1 - p2.astype(F32)).astype(BF16)
    return p1, p2, p3


def _split2(x):
    p1 = x.astype(BF16)
    p2 = (x - p1.astype(F32)).astype(BF16)
    return p1, p2


def _dot(a, b):
    return jnp.dot(a, b, preferred_element_type=F32)


def _dot_nt(a, b):
    return lax.dot_general(a, b, (((1,), (1,)), ((), ())), preferred_element_type=F32)


def _rmsnorm(x, g):
    return x * lax.rsqrt(jnp.mean(x * x, axis=-1, keepdims=True) + EPS) * g


def _silu(x):
    return x * jax.nn.sigmoid(x)


def _params(*sem):
    return pltpu.CompilerParams(dimension_semantics=sem, vmem_limit_bytes=VMEM_LIMIT)


def _mod_kernel(c_ref, w_ref, b_ref, o_ref):
    a = _silu(c_ref[...])
    a_stack = jnp.concatenate(_split3(a), axis=0)
    w1, w2 = _split2(w_ref[...])
    r1 = _dot(a_stack, w1)
    r2 = _dot(a_stack[:32], w2)
    o_ref[...] = (r1[0:16] + r1[16:32] + r1[32:48] + r2[0:16] + r2[16:32]) + b_ref[...]


def _mod_rows(cvec, w_ada, b_ada):
    n = N_MOD * D_MODEL
    tn = 1536
    return pl.pallas_call(
        _mod_kernel,
        grid=(n // tn,),
        in_specs=[pl.BlockSpec((16, D_MODEL), lambda j: (0, 0)),
                  pl.BlockSpec((D_MODEL, tn), lambda j: (0, j)),
                  pl.BlockSpec((1, tn), lambda j: (0, j))],
        out_specs=pl.BlockSpec((16, tn), lambda j: (0, j)),
        out_shape=jax.ShapeDtypeStruct((16, n), F32),
        compiler_params=_params("arbitrary"),
        name="mod",
    )(cvec, w_ada, b_ada.reshape(1, n))


def _inproj_kernel(x_ref, mod_ref, g_ref, wm_ref, wt_ref, wg_ref, wgt_ref, bgr_ref, bgc_ref,
                   u_ref, k_ref, o_ref, qt_ref, vt_ref, gate_ref, gatet_ref):
    x = x_ref[0]
    mod = mod_ref[0]
    h = _rmsnorm(x, g_ref[...]) * (1.0 + mod[1:2]) + mod[0:1]
    h1, h2, h3 = _split3(h)
    z = _dot(h1, wm_ref[...])
    u_ref[0] = z[:, 0:512].astype(BF16)
    k_ref[0] = (z[:, 512:1024] * K_SCALE).astype(BF16)
    o_ref[0] = z[:, 1024:1536].astype(BF16)
    zt = _dot_nt(wt_ref[...], h1)
    qt_ref[0] = zt[0:512].astype(BF16)
    vt_ref[0] = zt[512:1024].astype(BF16)
    tm = x.shape[0]
    wg1, wg2 = _split2(wg_ref[...])
    hs = jnp.concatenate([h1, h2, h3], axis=0)
    r1 = _dot(hs, wg1)
    r2 = _dot(hs[:2 * tm], wg2)
    gate_ref[0] = (r1[0:tm] + r1[tm:2 * tm] + r1[2 * tm:] + r2[0:tm] + r2[tm:]) + bgr_ref[...]
    wt1, wt2 = _split2(wgt_ref[...])
    wts = jnp.concatenate([wt1, wt2], axis=0)
    t1 = _dot_nt(wts, h1)
    t2 = _dot_nt(wts, h2)
    t3 = _dot_nt(wt1, h3)
    gatet_ref[0] = (t1[0:16] + t1[16:32] + t2[0:16] + t2[16:32] + t3) + bgc_ref[...]


def _inproj(x, mod, mod_row, g, wm, wt, wg, wgt, bgr, bgc):
    b, t, _ = x.shape
    tm = TOKEN_TILE
    const = lambda *shape: pl.BlockSpec(shape, lambda i, j: (0,) * len(shape))
    tok = lambda w: pl.BlockSpec((1, tm, w), lambda i, j: (i, j, 0))
    tok_t = lambda r: pl.BlockSpec((1, r, tm), lambda i, j: (i, 0, j))
    sd = jax.ShapeDtypeStruct
    return pl.pallas_call(
        _inproj_kernel,
        grid=(b, t // tm),
        in_specs=[tok(D_MODEL),
                  pl.BlockSpec((1, N_MOD, D_MODEL), lambda i, j: (mod_row(i), 0, 0)),
                  const(1, D_MODEL), const(D_MODEL, 1536), const(1024, D_MODEL),
                  const(D_MODEL, LANES), const(16, D_MODEL), const(1, LANES), const(16, 1)],
        out_specs=[tok(512), tok(512), tok(512), tok_t(512), tok_t(512), tok(LANES), tok_t(16)],
        out_shape=[sd((b, t, 512), BF16), sd((b, t, 512), BF16), sd((b, t, 512), BF16),
                   sd((b, 512, t), BF16), sd((b, 512, t), BF16), sd((b, t, LANES), F32),
                   sd((b, 16, t), F32)],
        compiler_params=_params("arbitrary", "arbitrary"),
        name="inproj",
    )(x, mod, g, wm, wt, wg, wgt, bgr, bgc)


def _log_sigmoid(x):
    return jnp.minimum(x, 0.0) - jnp.log1p(jnp.exp(-jnp.abs(x)))


def _scan_unit(st, k, qt, vt, u_col, u_row, b_row, btot, mask, s_prev, m_prev, use_state):
    dh = HEAD_DIM
    n = st.shape[0]
    ub = jnp.where(mask, jnp.broadcast_to(u_col, (n, n)), -jnp.inf)
    z = jnp.maximum(m_prev, jnp.max(ub, axis=0, keepdims=True))
    p = (jnp.exp(ub - z) * st).astype(BF16)
    ones = jnp.ones((dh, n), BF16)
    tot = _dot(jnp.concatenate([vt, ones], axis=0), p)
    if use_state:
        tot = tot + jnp.exp(m_prev - z) * _dot(s_prev.astype(BF16), qt)
    floor = jnp.exp(-(b_row + z))
    h_t = tot[:dh] / jnp.maximum(jnp.abs(tot[dh:]), floor)
    g_row = btot + u_row
    m_new = jnp.maximum(btot + m_prev, jnp.max(g_row, axis=-1, keepdims=True))
    w_row = jnp.exp(g_row - m_new)
    vw = jnp.concatenate([(vt.astype(F32) * w_row).astype(BF16),
                          jnp.broadcast_to(w_row, (dh, n)).astype(BF16)], axis=0)
    s_new = jnp.exp(btot + m_prev - m_new) * s_prev + _dot(vw, k)
    return h_t.T, s_new, m_new


def _mlstm_kernel(*refs, nc, zero_init, emit_state):
    it = iter(refs)
    fwd_refs = tuple(next(it) for _ in range(5))
    bwd_refs = tuple(next(it) for _ in range(5)) if nc > 1 else fwd_refs
    if not zero_init:
        s0_ref, m0_ref = next(it), next(it)
    h_refs = (next(it), next(it))
    if emit_state:
        c_out, n_out, m_out = next(it), next(it), next(it)
    s_scr, m_scr = next(it), next(it)

    j = pl.program_id(1)
    n = CHUNK
    dh = HEAD_DIM

    @pl.when(j == 0)
    def _():
        if zero_init:
            s_scr[...] = jnp.zeros_like(s_scr)
            m_scr[...] = jnp.zeros_like(m_scr)
        else:
            s_scr[...] = s0_ref[0]
            m_scr[...] = m0_ref[0]

    rows = lax.broadcasted_iota(jnp.int32, (n, n), 0)
    cols = lax.broadcasted_iota(jnp.int32, (n, n), 1)
    le = rows <= cols
    ge = rows >= cols
    tri_le = le.astype(BF16)
    tri_ge = ge.astype(BF16)
    use_state = not (zero_init and nc == 1)

    def gate_terms(d):
        g_ref, gt_ref = (fwd_refs, bwd_refs)[d][3:5]
        gate = g_ref[0]
        gate_t = gt_ref[0]
        lf = _log_sigmoid(gate)
        lf_t = _log_sigmoid(gate_t)
        tri_c, tri_r = (tri_ge, tri_le) if d == 0 else (tri_le, tri_ge)
        bc = _dot(tri_c, jnp.concatenate(_split3(lf), axis=1))
        b_cols = bc[:, 0:128] + bc[:, 128:256] + bc[:, 256:384]
        br = _dot(jnp.concatenate(_split3(lf_t), axis=0), tri_r)
        b_rows = br[0:16] + br[16:32] + br[32:48]
        return gate, gate_t, b_cols, b_rows, jnp.sum(lf_t, axis=-1, keepdims=True)

    terms = [gate_terms(0), gate_terms(1)]
    hs = ([], [])
    for hd in range(HEADS):
        hsl = slice(hd * dh, (hd + 1) * dh)
        st = None
        for d in range(N_DIR):
            k_ref, qt_ref, vt_ref = (fwd_refs, bwd_refs)[d][0:3]
            gate, gate_t, b_cols, b_rows, tot_rows = terms[d]
            ci = d * 8 + hd
            cf = d * 8 + 4 + hd
            unit = d * HEADS + hd
            k = k_ref[0, :, hsl]
            qt = qt_ref[0, hsl, :]
            if st is None or nc > 1:
                st = _dot(k, qt)
            mask = le if d == 0 else ge
            h, s_new, m_new = _scan_unit(
                st, k, qt, vt_ref[0, hsl, :],
                gate[:, ci:ci + 1] - b_cols[:, cf:cf + 1],
                gate_t[ci:ci + 1, :] - b_rows[cf:cf + 1, :],
                b_rows[cf:cf + 1, :], tot_rows[cf:cf + 1, :],
                mask, s_scr[unit], m_scr[unit][:, 0:1], use_state)
            s_scr[unit] = s_new
            m_scr[unit] = jnp.broadcast_to(m_new, (1, LANES))
            hs[d].append(h)
    for d in range(N_DIR):
        h_refs[d][0] = jnp.concatenate(hs[d], axis=1).astype(BF16)

    if emit_state:
        @pl.when(j == nc - 1)
        def _():
            for unit in range(N_DIR * HEADS):
                s = s_scr[unit]
                c_out[0, unit] = s[:dh].T
                n_out[0, unit] = s[dh:dh + 1]
                m_out[0, unit] = m_scr[unit]


def _mlstm(k, qt, vt, gate, gate_t, s0, m0, emit_state):
    b, t, _ = k.shape
    nc = t // CHUNK
    zero_init = s0 is None
    nu = N_DIR * HEADS
    fwd = lambda w: pl.BlockSpec((1, CHUNK, w), lambda i, j: (i, j, 0))
    bwd = lambda w: pl.BlockSpec((1, CHUNK, w), lambda i, j: (i, nc - 1 - j, 0))
    fwd_t = lambda r: pl.BlockSpec((1, r, CHUNK), lambda i, j: (i, 0, j))
    bwd_t = lambda r: pl.BlockSpec((1, r, CHUNK), lambda i, j: (i, 0, nc - 1 - j))
    args = [k, qt, vt, gate, gate_t]
    in_specs = [fwd(512), fwd_t(512), fwd_t(512), fwd(LANES), fwd_t(16)]
    if nc > 1:
        args += [k, qt, vt, gate, gate_t]
        in_specs += [bwd(512), bwd_t(512), bwd_t(512), bwd(LANES), bwd_t(16)]
    if not zero_init:
        args += [s0, m0]
        in_specs += [pl.BlockSpec((1, nu, 2 * HEAD_DIM, HEAD_DIM), lambda i, j: (i, 0, 0, 0)),
                     pl.BlockSpec((1, nu, 1, LANES), lambda i, j: (i, 0, 0, 0))]
    sd = jax.ShapeDtypeStruct
    out_shape = [sd((b, t, 512), BF16), sd((b, t, 512), BF16)]
    out_specs = [fwd(512), bwd(512)]
    if emit_state:
        out_shape += [sd((b, nu, HEAD_DIM, HEAD_DIM), F32), sd((b, nu, 1, HEAD_DIM), F32),
                      sd((b, nu, 1, LANES), F32)]
        out_specs += [pl.BlockSpec((1, nu, HEAD_DIM, HEAD_DIM), lambda i, j: (i, 0, 0, 0)),
                      pl.BlockSpec((1, nu, 1, HEAD_DIM), lambda i, j: (i, 0, 0, 0)),
                      pl.BlockSpec((1, nu, 1, LANES), lambda i, j: (i, 0, 0, 0))]
    return pl.pallas_call(
        functools.partial(_mlstm_kernel, nc=nc, zero_init=zero_init, emit_state=emit_state),
        grid=(b, nc),
        in_specs=in_specs,
        out_specs=out_specs,
        out_shape=out_shape,
        scratch_shapes=[pltpu.VMEM((nu, 2 * HEAD_DIM, HEAD_DIM), F32),
                        pltpu.VMEM((nu, 1, LANES), F32)],
        compiler_params=_params("arbitrary", "arbitrary"),
        name="mlstm",
    )(*args)


def _outproj_kernel(x_ref, u_ref, hf_ref, hb_ref, o_ref, mod_ref, pm_ref, pinv_ref, wp_ref, ps_ref,
                    wo_ref, gpm_ref, gpf_ref, wrt_ref, x1_ref, h2_ref, xp_ref, lg_ref, slab_ref, *, tm):
    i = pl.program_id(1)
    x = x_ref[0]
    mod = mod_ref[0]
    row0 = pl.multiple_of(i * tm, tm)
    u_full = u_ref[0]
    u_tile = u_ref[0, pl.ds(row0, tm), :].astype(F32)
    mix = jnp.zeros((tm, D_MODEL), F32)
    for g in range(POOL_GROUPS):
        sl = slice(g * POOL_GROUP_DIM, (g + 1) * POOL_GROUP_DIM)
        box = _dot(pm_ref[g], u_full[:, sl])
        diff = box * pinv_ref[g] - u_tile[:, sl]
        yp = _dot(diff.astype(BF16), wp_ref[g]) * ps_ref[:, sl]
        mix = mix + _dot(yp.astype(BF16), wo_ref[sl, :])
    hsum = hf_ref[0].astype(F32) + hb_ref[0].astype(F32)
    yml = jax.nn.sigmoid(o_ref[0].astype(F32)) * hsum
    mix = mix + _dot(yml.astype(BF16), wo_ref[POOL_WIDTH:, :])
    x1 = x + mod[2:3] * _rmsnorm(mix, gpm_ref[...])
    x1_ref[0] = x1
    h2 = _rmsnorm(x1, gpf_ref[...]) * (1.0 + mod[4:5]) + mod[3:4]
    p1, p2, p3 = _split3(h2)
    h2_ref[0] = p1
    for cc in range(D_MODEL // LANES):
        slab_ref[:, cc, :] = h2[:, cc * LANES:(cc + 1) * LANES]
    xp_ref[0] = slab_ref[...].astype(BF16)
    w1, w2, w3 = _split3(wrt_ref[...])
    ws = jnp.concatenate([w1, w2, w3], axis=0)
    r1 = _dot_nt(ws, p1)
    r2 = _dot_nt(ws[:128], p2)
    r3 = _dot_nt(w1, p3)
    lg_ref[0] = r1[0:64] + r1[64:128] + r1[128:192] + r2[0:64] + r2[64:128] + r3


def _outproj(x, u, hf, hb, o, mod, mod_row, pm, pinv, wp, ps, wo, gpm, gpf, wrt):
    b, t, _ = x.shape
    tm = TOKEN_TILE
    const = lambda *shape: pl.BlockSpec(shape, lambda i, j: (0,) * len(shape))
    tok = lambda w: pl.BlockSpec((1, tm, w), lambda i, j: (i, j, 0))
    sd = jax.ShapeDtypeStruct
    return pl.pallas_call(
        functools.partial(_outproj_kernel, tm=tm),
        grid=(b, t // tm),
        in_specs=[tok(D_MODEL),
                  pl.BlockSpec((1, t, 512), lambda i, j: (i, 0, 0)),
                  tok(512), tok(512), tok(512),
                  pl.BlockSpec((1, N_MOD, D_MODEL), lambda i, j: (mod_row(i), 0, 0)),
                  pl.BlockSpec((POOL_GROUPS, tm, t), lambda i, j: (0, j, 0)),
                  pl.BlockSpec((POOL_GROUPS, tm, 1), lambda i, j: (0, j, 0)),
                  const(POOL_GROUPS, POOL_GROUP_DIM, POOL_GROUP_DIM), const(1, POOL_WIDTH),
                  const(D_MODEL, D_MODEL), const(1, D_MODEL), const(1, D_MODEL),
                  const(N_EXPERTS, D_MODEL)],
        out_specs=[tok(D_MODEL), tok(D_MODEL),
                   pl.BlockSpec((1, tm, D_MODEL // LANES, LANES), lambda i, j: (i, j, 0, 0)),
                   pl.BlockSpec((1, N_EXPERTS, tm), lambda i, j: (i, 0, j))],
        out_shape=[sd((b, t, D_MODEL), F32), sd((b, t, D_MODEL), BF16),
                   sd((b, t, D_MODEL // LANES, LANES), BF16), sd((b, N_EXPERTS, t), F32)],
        scratch_shapes=[pltpu.VMEM((tm, D_MODEL // LANES, LANES), F32)],
        compiler_params=_params("arbitrary", "arbitrary"),
        name="outproj",
    )(x, u, hf, hb, o, mod, pm, pinv, wp, ps, wo, gpm, gpf, wrt)


def _router_kernel(lg_ref, br_ref, comb_ref, sel_ref):
    s = jax.nn.sigmoid(lg_ref[...])
    biased = s + br_ref[...]
    gidx = lax.broadcasted_iota(jnp.int32, s.shape, 0)
    jidx = lax.broadcasted_iota(jnp.int32, s.shape, 1)
    neg = -jnp.inf
    m1 = jnp.max(biased, axis=1, keepdims=True)
    i1 = jnp.min(jnp.where(biased == m1, jidx, GROUP_SIZE), axis=1, keepdims=True)
    m2 = jnp.max(jnp.where(jidx == i1, neg, biased), axis=1, keepdims=True)
    gscore = m1 + m2
    gi = lax.broadcasted_iota(jnp.int32, gscore.shape, 0)
    gmask = jnp.zeros(gscore.shape, F32)
    cur = gscore
    for _ in range(TOPK_GROUPS):
        mx = jnp.max(cur, axis=0, keepdims=True)
        ix = jnp.min(jnp.where(cur == mx, gi, N_EXPERT_GROUPS), axis=0, keepdims=True)
        hit = gi == ix
        gmask = jnp.where(hit, 1.0, gmask)
        cur = jnp.where(hit, neg, cur)
    cur = jnp.where(gmask > 0, biased, neg)
    eidx = gidx * GROUP_SIZE + jidx
    selmask = jnp.zeros(s.shape, F32)
    for _ in range(TOP_K):
        mx = jnp.max(jnp.max(cur, axis=1, keepdims=True), axis=0, keepdims=True)
        ix = jnp.where(cur == mx, eidx, N_EXPERTS)
        ix = jnp.min(jnp.min(ix, axis=1, keepdims=True), axis=0, keepdims=True)
        hit = eidx == ix
        selmask = jnp.where(hit, 1.0, selmask)
        cur = jnp.where(hit, neg, cur)
    sel = selmask * s
    tot = jnp.sum(jnp.sum(sel, axis=1, keepdims=True), axis=0, keepdims=True)
    comb_ref[...] = sel / tot * ROUTED_SCALE
    sel_ref[...] = selmask


def _router(logits_t, b_router):
    t = logits_t.shape[1]
    tl = 1024
    shp = (N_EXPERT_GROUPS, GROUP_SIZE, t)
    blk = pl.BlockSpec((N_EXPERT_GROUPS, GROUP_SIZE, tl), lambda j: (0, 0, j))
    comb, sel = pl.pallas_call(
        _router_kernel,
        grid=(t // tl,),
        in_specs=[blk, pl.BlockSpec((N_EXPERT_GROUPS, GROUP_SIZE, 1), lambda j: (0, 0, 0))],
        out_specs=[blk, blk],
        out_shape=[jax.ShapeDtypeStruct(shp, F32), jax.ShapeDtypeStruct(shp, F32)],
        compiler_params=_params("arbitrary"),
        name="router",
    )(logits_t.reshape(shp), b_router.reshape(N_EXPERT_GROUPS, GROUP_SIZE, 1))
    return comb.reshape(N_EXPERTS, t), sel.reshape(N_EXPERTS, t)


def _plan_kernel(sel_ref, pos_ref, meta_ref, emeta_ref, *, n_meta):
    t = sel_ref.shape[1]
    tm = float(MOE_TILE)
    sel = sel_ref[...]
    selb = sel.astype(BF16)
    blk = 256
    rr = lax.broadcasted_iota(jnp.int32, (blk, blk), 0)
    cc = lax.broadcasted_iota(jnp.int32, (blk, blk), 1)
    before = (rr < cc).astype(BF16)
    carry = jnp.zeros((N_EXPERTS, 1), F32)
    ranks = []
    for b in range(t // blk):
        sb = selb[:, b * blk:(b + 1) * blk]
        ranks.append(_dot(sb, before) + carry)
        carry = carry + jnp.sum(sel[:, b * blk:(b + 1) * blk], axis=1, keepdims=True)
    rank = jnp.concatenate(ranks, axis=1)
    cnt = carry
    ntile = jnp.floor((cnt + (tm - 1.0)) * (1.0 / tm))
    er = lax.broadcasted_iota(jnp.int32, (N_EXPERTS, N_EXPERTS), 0)
    ec = lax.broadcasted_iota(jnp.int32, (N_EXPERTS, N_EXPERTS), 1)
    below = (ec < er).astype(BF16)
    tstart = _dot(below, jnp.broadcast_to(ntile, (N_EXPERTS, LANES)).astype(BF16))[:, 0:1]
    pos = tstart * tm + rank
    erank = _dot(below, selb)
    rows = []
    for k in range(TOP_K):
        hit = (sel > 0.0) & (erank == float(k))
        rows.append(jnp.sum(jnp.where(hit, pos, 0.0), axis=0, keepdims=True))
    rows += [jnp.zeros((1, t), F32)] * (8 - TOP_K)
    pos_ref[...] = jnp.concatenate(rows, axis=0).astype(jnp.int32)

    tau = lax.broadcasted_iota(jnp.int32, (N_EXPERTS, n_meta), 1).astype(F32)
    eidx = lax.broadcasted_iota(jnp.int32, (N_EXPERTS, n_meta), 0).astype(F32)
    te = jnp.sum(((tstart + ntile) <= tau).astype(F32), axis=0, keepdims=True)
    te = jnp.minimum(te, float(N_EXPERTS - 1))
    onehot = eidx == te
    cnt_t = jnp.sum(jnp.where(onehot, cnt, 0.0), axis=0, keepdims=True)
    ts_t = jnp.sum(jnp.where(onehot, tstart, 0.0), axis=0, keepdims=True)
    tr = jnp.clip(cnt_t - (tau[0:1] - ts_t) * tm, 0.0, tm)
    tf = jnp.where((tau[0:1] == ts_t) & (tr > 0.0), 1.0, 0.0)
    meta_ref[...] = jnp.concatenate([te, tr, tf] + [jnp.zeros((1, n_meta), F32)] * 5, axis=0).astype(jnp.int32)

    eye = (lax.broadcasted_iota(jnp.int32, (N_EXPERTS, LANES), 0)
           == lax.broadcasted_iota(jnp.int32, (N_EXPERTS, LANES), 1))
    as_row = lambda col: jnp.sum(jnp.where(eye, col, 0.0), axis=0, keepdims=True)
    emeta_ref[...] = jnp.concatenate([as_row(cnt), as_row(tstart), as_row(ntile)]
                                     + [jnp.zeros((1, LANES), F32)] * 5, axis=0).astype(jnp.int32)


def _plan(sel, n_meta):
    t = sel.shape[1]
    sd = jax.ShapeDtypeStruct
    return pl.pallas_call(
        functools.partial(_plan_kernel, n_meta=n_meta),
        out_shape=[sd((8, t), jnp.int32), sd((8, n_meta), jnp.int32), sd((8, LANES), jnp.int32)],
        compiler_params=pltpu.CompilerParams(vmem_limit_bytes=VMEM_LIMIT),
        name="plan",
    )(sel)


def _fill_slot_table(pos_ref, emeta_ref, tbl_ref, n_tokens):
    group = 32
    def pad_expert(e, carry):
        n_t = emeta_ref[2 * LANES + e]
        start = (emeta_ref[LANES + e] + n_t - 1) * MOE_TILE

        @pl.when(n_t > 0)
        def _():
            def put(i, c):
                for u in range(group):
                    tbl_ref[start + i * group + u] = n_tokens
                return c
            lax.fori_loop(0, MOE_TILE // group, put, 0)
        return carry
    lax.fori_loop(0, N_EXPERTS, pad_expert, 0)
    last = N_EXPERTS - 1
    def pad_tile(tile, carry):
        def put(i, c):
            for u in range(group):
                tbl_ref[tile * MOE_TILE + i * group + u] = n_tokens
            return c
        lax.fori_loop(0, MOE_TILE // group, put, 0)
        return carry
    lax.fori_loop(emeta_ref[LANES + last] + emeta_ref[2 * LANES + last], tbl_ref.shape[0] // MOE_TILE, pad_tile, 0)

    unroll = 8
    def scatter(i, carry):
        first = i * (8 * unroll)
        for u in range(unroll):
            for k in range(TOP_K):
                tbl_ref[pos_ref[first + (8 * u + k)]] = i * unroll + u
        return carry
    lax.fori_loop(0, n_tokens // unroll, scatter, 0)


def _experts_kernel(te_ref, tr_ref, tf_ref, pos_ref, emeta_ref, xc_ref, xs_ref, comb_ref, wgu_ref, wd_ref, acc_out,
                    tbl_ref, xbuf, acc, stage0, stage1, cst0, cst1, ybuf0, ybuf1, wgu_b, wd_b, sems):
    s = pl.program_id(0)
    n_tiles = te_ref.shape[0]
    tm = MOE_TILE
    nch = D_MODEL // LANES
    t_g = jnp.minimum(s, n_tiles - 1)
    t_c = jnp.clip(s - 1, 0, n_tiles - 1)
    t_s = jnp.clip(s - 2, 0, n_tiles - 1)

    @pl.when(s == 0)
    def _():
        tc, ts = xc_ref.shape[0], xs_ref.shape[0]
        copies = (pltpu.make_async_copy(xc_ref, xbuf.at[pl.ds(0, tc)], sems.at[0]),
                  pltpu.make_async_copy(xs_ref, xbuf.at[pl.ds(tc, ts)], sems.at[1]))
        for cp in copies:
            cp.start()
        n_pad = xbuf.shape[0] - tc - ts
        xbuf[pl.ds(tc + ts, n_pad)] = jnp.zeros((n_pad,) + xbuf.shape[1:], BF16)
        for ref in (acc, stage0, stage1, cst0, cst1, ybuf0, ybuf1, wgu_b, wd_b):
            ref[...] = jnp.zeros_like(ref)
        _fill_slot_table(pos_ref, emeta_ref, tbl_ref, tc + ts)
        for cp in copies:
            cp.wait()

    @pl.when((s >= 1) & (tf_ref[t_c] == 1))
    def _():
        wgu_b[...] = wgu_ref[0].astype(BF16)
        wd_b[...] = wd_ref[0].astype(BF16)

    def gather(tile, stage, cst):
        base = tile * tm
        for j in range(tm):
            tok = tbl_ref[base + j]
            stage[pl.ds(j * nch, nch), :] = xbuf[tok].astype(F32)
            cst[pl.ds(j, 1), :] = comb_ref[pl.ds(tok, 1), :]

    def experts(tile, stage, cst, ybuf):
        e = te_ref[tile]
        xb = jnp.concatenate([stage[pl.ds(cc, tm, stride=nch), :] for cc in range(nch)], axis=1).astype(BF16)
        gu = _dot(xb, wgu_b[...])
        comb3 = jnp.concatenate(_split3(cst[...]), axis=1)
        sel_row = lax.broadcasted_iota(jnp.int32, (3 * LANES, EXPERT_DIM), 0) % LANES
        cexp = _dot(comb3, (sel_row == e).astype(BF16))
        act = (_silu(gu[:, :EXPERT_DIM]) * gu[:, EXPERT_DIM:] * cexp).astype(BF16)
        y = _dot(act, wd_b[...])
        for cc in range(nch):
            ybuf[cc * Y_PITCH:cc * Y_PITCH + tm, :] = y[:, cc * LANES:(cc + 1) * LANES]

    def scatter(tile, ybuf):
        base = tile * tm
        sc_n = 16
        for i in range(tm // sc_n):
            toks = [tbl_ref[base + i * sc_n + u] for u in range(sc_n)]
            olds = [acc[toks[u]] for u in range(sc_n)]
            news = [olds[u] + ybuf[pl.ds(i * sc_n + u, nch, stride=Y_PITCH), :] for u in range(sc_n)]
            for u in range(sc_n):
                acc[toks[u]] = news[u]

    busy = (tr_ref[t_g] + tr_ref[t_c] + tr_ref[t_s]) > 0
    for par, (st_g, cs_g, st_c, cs_c, yb_c, yb_s) in enumerate(
            ((stage0, cst0, stage1, cst1, ybuf1, ybuf0), (stage1, cst1, stage0, cst0, ybuf0, ybuf1))):
        @pl.when(busy & (s % 2 == par))
        def _():
            gather(t_g, st_g, cs_g)
            experts(t_c, st_c, cs_c, yb_c)
            scatter(t_s, yb_s)

    @pl.when(s == pl.num_programs(0) - 1)
    def _():
        cp = pltpu.make_async_copy(acc, acc_out, sems.at[2])
        cp.start()
        cp.wait()


def _experts(te, tr, tf, pos, emeta, xc, xs, comb, wgu, wd):
    n_tok = comb.shape[0]
    n_tiles = te.shape[0]
    tm = MOE_TILE
    nch = D_MODEL // LANES
    w_idx = lambda s, te, *_: (te[jnp.clip(s - 1, 0, n_tiles - 1)], 0, 0)
    grid_spec = pltpu.PrefetchScalarGridSpec(
        num_scalar_prefetch=5,
        grid=(n_tiles + 2,),
        in_specs=[pl.BlockSpec(memory_space=pl.ANY), pl.BlockSpec(memory_space=pl.ANY),
                  pl.BlockSpec((n_tok, LANES), lambda s, *_: (0, 0), pipeline_mode=pl.Buffered(1)),
                  pl.BlockSpec((1, D_MODEL, 2 * EXPERT_DIM), w_idx),
                  pl.BlockSpec((1, EXPERT_DIM, D_MODEL), w_idx)],
        out_specs=pl.BlockSpec(memory_space=pl.ANY),
        scratch_shapes=[pltpu.SMEM((n_tiles * tm,), jnp.int32),
                        pltpu.VMEM((n_tok, nch, LANES), BF16),
                        pltpu.VMEM((n_tok, nch, LANES), F32),
                        pltpu.VMEM((tm * nch, LANES), F32), pltpu.VMEM((tm * nch, LANES), F32),
                        pltpu.VMEM((tm, LANES), F32), pltpu.VMEM((tm, LANES), F32),
                        pltpu.VMEM((nch * Y_PITCH, LANES), F32), pltpu.VMEM((nch * Y_PITCH, LANES), F32),
                        pltpu.VMEM((D_MODEL, 2 * EXPERT_DIM), BF16),
                        pltpu.VMEM((EXPERT_DIM, D_MODEL), BF16),
                        pltpu.SemaphoreType.DMA((3,))],
    )
    return pl.pallas_call(
        _experts_kernel,
        grid_spec=grid_spec,
        out_shape=jax.ShapeDtypeStruct((n_tok, nch, LANES), F32),
        compiler_params=pltpu.CompilerParams(dimension_semantics=("arbitrary",),
                                             vmem_limit_bytes=EXPERTS_VMEM_LIMIT),
        name="experts",
    )(te, tr, tf, pos, emeta, xc, xs, comb, wgu, wd)


def _final_kernel(acc_ref, h_ref, x1_ref, mod_ref, wsg_ref, wsd_ref, gpo_ref, out_ref):
    routed = jnp.concatenate([acc_ref[:, cc, :] for cc in range(D_MODEL // LANES)], axis=1)
    gs = _dot(h_ref[...], wsg_ref[...])
    act = _silu(gs[:, :SHARED_DIM]) * gs[:, SHARED_DIM:]
    f = routed + _dot(act.astype(BF16), wsd_ref[...])
    out_ref[...] = x1_ref[...] + mod_ref[0][5:6] * _rmsnorm(f, gpo_ref[...])


def _final(acc, tile0, h2, x1, mod, mod_row, wsg, wsd, gpo):
    n = h2.shape[0]
    tm = TOKEN_TILE
    const = lambda *shape: pl.BlockSpec(shape, lambda i: (0,) * len(shape))
    tok = lambda w: pl.BlockSpec((tm, w), lambda i: (i, 0))
    return pl.pallas_call(
        _final_kernel,
        grid=(n // tm,),
        in_specs=[pl.BlockSpec((tm, D_MODEL // LANES, LANES), lambda i: (i + tile0, 0, 0)),
                  tok(D_MODEL), tok(D_MODEL),
                  pl.BlockSpec((1, N_MOD, D_MODEL), lambda i: (mod_row(i), 0, 0)),
                  const(D_MODEL, 2 * SHARED_DIM), const(SHARED_DIM, D_MODEL), const(1, D_MODEL)],
        out_specs=tok(D_MODEL),
        out_shape=jax.ShapeDtypeStruct((n, D_MODEL), F32),
        compiler_params=_params("arbitrary"),
        name="final",
    )(acc, h2, x1, mod, wsg, wsd, gpo)


def _window_bounds(n, w):
    idx = np.arange(n)
    return np.clip(idx - w // 2, 0, n), np.clip(idx + w - w // 2, 0, n)


def _pool_operators(t, grid):
    mats, invs = [], []
    for w in POOL_WINDOWS:
        if grid:
            rlo, rhi = _window_bounds(t // GRID_W, w)
            clo, chi = _window_bounds(GRID_W, w)
            r = np.arange(t) // GRID_W
            c = np.arange(t) % GRID_W
            m = ((r[None, :] >= rlo[r][:, None]) & (r[None, :] < rhi[r][:, None])
                 & (c[None, :] >= clo[c][:, None]) & (c[None, :] < chi[c][:, None]))
            cnt = (rhi - rlo)[r] * (chi - clo)[c]
        else:
            lo, hi = _window_bounds(t, w)
            sidx = np.arange(t)
            m = (sidx[None, :] >= lo[:, None]) & (sidx[None, :] < hi[:, None])
            cnt = hi - lo
        mats.append(m.astype(np.float32))
        invs.append((1.0 / cnt.astype(np.float64)).astype(np.float32)[:, None])
    return jnp.asarray(np.stack(mats), BF16), jnp.asarray(np.stack(invs), F32)


def kernel(x_prompt, x_sample, state_C, state_n, state_m, c, c_ctx, w_ada, b_ada, g_pre_mix, w_in, b_gate,
           w_pool, pool_scale, w_out, g_post_mix, g_pre_ffn, w_router, b_router, w_expert_gu, w_expert_down,
           w_shared_gu, w_shared_down, g_post_ffn):
    b_ctx = x_prompt.shape[0]
    b_lat = x_sample.shape[0]
    nu = N_DIR * HEADS
    l = 0
    row = lambda a: a[l].reshape(1, -1).astype(F32)

    cvec = jnp.zeros((16, D_MODEL), F32).at[0].set(c_ctx.astype(F32)).at[1:1 + b_lat].set(c.astype(F32))
    mod = _mod_rows(cvec, w_ada[l], b_ada[l]).reshape(16, N_MOD, D_MODEL)

    w_in_l = w_in[l]
    p0 = POOL_WIDTH
    mw = MLSTM_WIDTH
    w_u, w_q, w_k, w_v, w_o = (w_in_l[:, lo:lo + 512] for lo in (0, p0, p0 + mw, p0 + 2 * mw, p0 + 3 * mw))
    wm = jnp.concatenate([w_u, w_k, w_o], axis=1).astype(BF16)
    wt = jnp.concatenate([w_q.T, w_v.T], axis=0).astype(BF16)
    wg_cols = w_in_l[:, p0 + 4 * mw:]
    wg = jnp.pad(wg_cols, ((0, 0), (0, LANES - GATE_COLS)))
    wgt = wg_cols.T
    bg = b_gate[l].reshape(GATE_COLS).astype(F32)
    bgr = jnp.pad(bg, (0, LANES - GATE_COLS)).reshape(1, LANES)
    bgc = bg.reshape(GATE_COLS, 1)
    wp = w_pool[l].astype(BF16)
    wo = w_out[l].astype(BF16)
    wrt = w_router[l].T
    wsg = w_shared_gu[l].astype(BF16)
    wsd = w_shared_down[l].astype(BF16)

    def mixer(x, mod_row, grid, s0, m0, emit_state):
        t = x.shape[1]
        u, k, o, qt, vt, gate, gate_t = _inproj(x.astype(F32), mod, mod_row, row(g_pre_mix), wm, wt, wg, wgt,
                                                bgr, bgc)
        outs = _mlstm(k, qt, vt, gate, gate_t, s0, m0, emit_state)
        hf, hb = outs[0], outs[1]
        pm, pinv = _pool_operators(t, grid)
        x1, h2, xp, lg = _outproj(x.astype(F32), u, hf, hb, o, mod, mod_row, pm, pinv, wp, row(pool_scale), wo,
                                  row(g_post_mix), row(g_pre_ffn), wrt)
        return x1, h2, xp, lg, outs[2:]

    ctx_row = lambda i: 0
    lat_row = lambda i: i + 1
    x1c, h2c, xpc, lgc, (c_new, n_new, m_new) = mixer(x_prompt, ctx_row, False, None, None, True)
    s0 = jnp.concatenate(
        [jnp.swapaxes(state_C[:, l].reshape(b_lat, nu, HEAD_DIM, HEAD_DIM).astype(F32), -1, -2),
         jnp.broadcast_to(state_n[:, l].reshape(b_lat, nu, 1, HEAD_DIM).astype(F32),
                          (b_lat, nu, HEAD_DIM, HEAD_DIM))], axis=-2)
    m0 = jnp.broadcast_to(state_m[:, l].reshape(b_lat, nu, 1, 1).astype(F32), (b_lat, nu, 1, LANES))
    x1s, h2s, xps, lgs, _ = mixer(x_sample, lat_row, True, s0, m0, False)

    tc = b_ctx * x_prompt.shape[1]
    ts = b_lat * x_sample.shape[1]
    n_tok = tc + ts
    lg_all = jnp.concatenate([lgc.transpose(1, 0, 2).reshape(N_EXPERTS, tc),
                              lgs.transpose(1, 0, 2).reshape(N_EXPERTS, ts)], axis=1)
    comb, sel = _router(lg_all, b_router[l].astype(F32))
    n_tiles = n_tok * TOP_K // MOE_TILE + N_EXPERTS
    n_meta = -(-n_tiles // LANES) * LANES
    pos, meta, emeta = _plan(sel, n_meta)

    comb_tok = jnp.pad(comb.T, ((0, 8), (0, LANES - N_EXPERTS)))
    slab = (D_MODEL // LANES, LANES)
    acc = _experts(meta[0, :n_tiles], meta[1, :n_tiles], meta[2, :n_tiles],
                   pos.T.reshape(-1), emeta.reshape(-1),
                   xpc.reshape((tc,) + slab), xps.reshape((ts,) + slab), comb_tok,
                   w_expert_gu[l], w_expert_down[l])

    fin = functools.partial(_final, wsg=wsg, wsd=wsd, gpo=row(g_post_ffn))
    tiles_per_lat = x_sample.shape[1] // TOKEN_TILE
    yc = fin(acc, 0, h2c.reshape(tc, D_MODEL), x1c.reshape(tc, D_MODEL), mod, ctx_row)
    ys = fin(acc, tc // TOKEN_TILE, h2s.reshape(ts, D_MODEL), x1s.reshape(ts, D_MODEL), mod,
             lambda i: i // tiles_per_lat + 1)

    new_c = c_new.reshape(b_ctx, 1, N_DIR, HEADS, HEAD_DIM, HEAD_DIM)
    new_n = n_new.reshape(b_ctx, 1, N_DIR, HEADS, HEAD_DIM)
    new_m = m_new[..., 0].reshape(b_ctx, 1, N_DIR, HEADS)
    return (yc.reshape(x_prompt.shape), ys.reshape(x_sample.shape), new_c, new_n, new_m)
```

```python
import functools

import jax
import jax.numpy as jnp
import numpy as np
from jax import lax
from jax.experimental import pallas as pl
from jax.experimental.pallas import tpu as pltpu

F32 = jnp.float32
BF16 = jnp.bfloat16

D_MODEL = 1024
GRID_W = 64
POOL_WIDTH = 512
POOL_GROUPS = 4
POOL_GROUP_DIM = 128
POOL_WINDOWS = (2, 4, 8, 16)
HEADS = 4
HEAD_DIM = 128
MLSTM_WIDTH = HEADS * HEAD_DIM
N_DIR = 2
GATE_COLS = N_DIR * 2 * HEADS
N_EXPERTS = 64
TOP_K = 6
N_EXPERT_GROUPS = 8
GROUP_SIZE = N_EXPERTS // N_EXPERT_GROUPS
TOPK_GROUPS = 4
EXPERT_DIM = 256
SHARED_DIM = 256
ROUTED_SCALE = 2.5
N_MOD = 6
EPS = 1e-6
K_SCALE = HEAD_DIM ** -0.5

LANES = 128
CHUNK = 256
TOKEN_TILE = 256
MOE_TILE = 256
Y_PITCH = MOE_TILE + 8
VMEM_LIMIT = 56 * 1024 * 1024
EXPERTS_VMEM_LIMIT = 58 * 1024 * 1024


def _split3(x):
    p1 = x.astype(BF16)
    r1 = x - p1.astype(F32)
    p2 = r1.astype(BF16)
    p3 = (r1 - p2.astype(F32)).astype(BF16)
    return p1, p2, p3


def _split2(x):
    p1 = x.astype(BF16)
    p2 = (x - p1.astype(F32)).astype(BF16)
    return p1, p2


def _dot(a, b):
    return jnp.dot(a, b, preferred_element_type=F32)


def _dot_nt(a, b):
    return lax.dot_general(a, b, (((1,), (1,)), ((), ())), preferred_element_type=F32)


def _rmsnorm(x, g):
    return x * lax.rsqrt(jnp.mean(x * x, axis=-1, keepdims=True) + EPS) * g


def _silu(x):
    return x * jax.nn.sigmoid(x)


def _params(*sem):
    return pltpu.CompilerParams(dimension_semantics=sem, vmem_limit_bytes=VMEM_LIMIT)


def _mod_kernel(c_ref, w_ref, b_ref, o_ref):
    a = _silu(c_ref[...])
    a_stack = jnp.concatenate(_split3(a), axis=0)
    w1, w2 = _split2(w_ref[...])
    r1 = _dot(a_stack, w1)
    r2 = _dot(a_stack[:32], w2)
    o_ref[...] = (r1[0:16] + r1[16:32] + r1[32:48] + r2[0:16] + r2[16:32]) + b_ref[...]


def _mod_rows(cvec, w_ada, b_ada):
    n = N_MOD * D_MODEL
    tn = 1536
    return pl.pallas_call(
        _mod_kernel,
        grid=(n // tn,),
        in_specs=[pl.BlockSpec((16, D_MODEL), lambda j: (0, 0)),
                  pl.BlockSpec((D_MODEL, tn), lambda j: (0, j)),
                  pl.BlockSpec((1, tn), lambda j: (0, j))],
        out_specs=pl.BlockSpec((16, tn), lambda j: (0, j)),
        out_shape=jax.ShapeDtypeStruct((16, n), F32),
        compiler_params=_params("arbitrary"),
        name="mod",
    )(cvec, w_ada, b_ada.reshape(1, n))


def _inproj_kernel(x_ref, mod_ref, g_ref, wm_ref, wt_ref, wg_ref, wgt_ref, bgr_ref, bgc_ref,
                   u_ref, k_ref, o_ref, qt_ref, vt_ref, gate_ref, gatet_ref):
    x = x_ref[0]
    mod = mod_ref[0]
    h = _rmsnorm(x, g_ref[...]) * (1.0 + mod[1:2]) + mod[0:1]
    h1, h2, h3 = _split3(h)
    z = _dot(h1, wm_ref[...])
    u_ref[0] = z[:, 0:512].astype(BF16)
    k_ref[0] = (z[:, 512:1024] * K_SCALE).astype(BF16)
    o_ref[0] = z[:, 1024:1536].astype(BF16)
    zt = _dot_nt(wt_ref[...], h1)
    qt_ref[0] = zt[0:512].astype(BF16)
    vt_ref[0] = zt[512:1024].astype(BF16)
    tm = x.shape[0]
    wg1, wg2 = _split2(wg_ref[...])
    hs = jnp.concatenate([h1, h2, h3], axis=0)
    r1 = _dot(hs, wg1)
    r2 = _dot(hs[:2 * tm], wg2)
    gate_ref[0] = (r1[0:tm] + r1[tm:2 * tm] + r1[2 * tm:] + r2[0:tm] + r2[tm:]) + bgr_ref[...]
    wt1, wt2 = _split2(wgt_ref[...])
    wts = jnp.concatenate([wt1, wt2], axis=0)
    t1 = _dot_nt(wts, h1)
    t2 = _dot_nt(wts, h2)
    t3 = _dot_nt(wt1, h3)
    gatet_ref[0] = (t1[0:16] + t1[16:32] + t2[0:16] + t2[16:32] + t3) + bgc_ref[...]


def _inproj(x, mod, mod_row, g, wm, wt, wg, wgt, bgr, bgc):
    b, t, _ = x.shape
    tm = TOKEN_TILE
    const = lambda *shape: pl.BlockSpec(shape, lambda i, j: (0,) * len(shape))
    tok = lambda w: pl.BlockSpec((1, tm, w), lambda i, j: (i, j, 0))
    tok_t = lambda r: pl.BlockSpec((1, r, tm), lambda i, j: (i, 0, j))
    sd = jax.ShapeDtypeStruct
    return pl.pallas_call(
        _inproj_kernel,
        grid=(b, t // tm),
        in_specs=[tok(D_MODEL),
                  pl.BlockSpec((1, N_MOD, D_MODEL), lambda i, j: (mod_row(i), 0, 0)),
                  const(1, D_MODEL), const(D_MODEL, 1536), const(1024, D_MODEL),
                  const(D_MODEL, LANES), const(16, D_MODEL), const(1, LANES), const(16, 1)],
        out_specs=[tok(512), tok(512), tok(512), tok_t(512), tok_t(512), tok(LANES), tok_t(16)],
        out_shape=[sd((b, t, 512), BF16), sd((b, t, 512), BF16), sd((b, t, 512), BF16),
                   sd((b, 512, t), BF16), sd((b, 512, t), BF16), sd((b, t, LANES), F32),
                   sd((b, 16, t), F32)],
        compiler_params=_params("arbitrary", "arbitrary"),
        name="inproj",
    )(x, mod, g, wm, wt, wg, wgt, bgr, bgc)


def _log_sigmoid(x):
    return jnp.minimum(x, 0.0) - jnp.log1p(jnp.exp(-jnp.abs(x)))


def _scan_unit(st, k, qt, vt, u_col, u_row, b_row, btot, mask, s_prev, m_prev, use_state):
    dh = HEAD_DIM
    n = st.shape[0]
    ub = jnp.where(mask, jnp.broadcast_to(u_col, (n, n)), -jnp.inf)
    z = jnp.maximum(m_prev, jnp.max(ub, axis=0, keepdims=True))
    p = (jnp.exp(ub - z) * st).astype(BF16)
    ones = jnp.ones((dh, n), BF16)
    tot = _dot(jnp.concatenate([vt, ones], axis=0), p)
    if use_state:
        tot = tot + jnp.exp(m_prev - z) * _dot(s_prev.astype(BF16), qt)
    floor = jnp.exp(-(b_row + z))
    h_t = tot[:dh] / jnp.maximum(jnp.abs(tot[dh:]), floor)
    g_row = btot + u_row
    m_new = jnp.maximum(btot + m_prev, jnp.max(g_row, axis=-1, keepdims=True))
    w_row = jnp.exp(g_row - m_new)
    vw = jnp.concatenate([(vt.astype(F32) * w_row).astype(BF16),
                          jnp.broadcast_to(w_row, (dh, n)).astype(BF16)], axis=0)
    s_new = jnp.exp(btot + m_prev - m_new) * s_prev + _dot(vw, k)
    return h_t.T, s_new, m_new


def _mlstm_kernel(*refs, nc, zero_init, emit_state):
    it = iter(refs)
    fwd_refs = tuple(next(it) for _ in range(5))
    bwd_refs = tuple(next(it) for _ in range(5)) if nc > 1 else fwd_refs
    if not zero_init:
        s0_ref, m0_ref = next(it), next(it)
    h_refs = (next(it), next(it))
    if emit_state:
        c_out, n_out, m_out = next(it), next(it), next(it)
    s_scr, m_scr = next(it), next(it)

    j = pl.program_id(1)
    n = CHUNK
    dh = HEAD_DIM

    @pl.when(j == 0)
    def _():
        if zero_init:
            s_scr[...] = jnp.zeros_like(s_scr)
            m_scr[...] = jnp.zeros_like(m_scr)
        else:
            s_scr[...] = s0_ref[0]
            m_scr[...] = m0_ref[0]

    rows = lax.broadcasted_iota(jnp.int32, (n, n), 0)
    cols = lax.broadcasted_iota(jnp.int32, (n, n), 1)
    le = rows <= cols
    ge = rows >= cols
    tri_le = le.astype(BF16)
    tri_ge = ge.astype(BF16)
    use_state = not (zero_init and nc == 1)

    def gate_terms(d):
        g_ref, gt_ref = (fwd_refs, bwd_refs)[d][3:5]
        gate = g_ref[0]
        gate_t = gt_ref[0]
        lf = _log_sigmoid(gate)
        lf_t = _log_sigmoid(gate_t)
        tri_c, tri_r = (tri_ge, tri_le) if d == 0 else (tri_le, tri_ge)
        bc = _dot(tri_c, jnp.concatenate(_split3(lf), axis=1))
        b_cols = bc[:, 0:128] + bc[:, 128:256] + bc[:, 256:384]
        br = _dot(jnp.concatenate(_split3(lf_t), axis=0), tri_r)
        b_rows = br[0:16] + br[16:32] + br[32:48]
        return gate, gate_t, b_cols, b_rows, jnp.sum(lf_t, axis=-1, keepdims=True)

    terms = [gate_terms(0), gate_terms(1)]
    hs = ([], [])
    for hd in range(HEADS):
        hsl = slice(hd * dh, (hd + 1) * dh)
        st = None
        for d in range(N_DIR):
            k_ref, qt_ref, vt_ref = (fwd_refs, bwd_refs)[d][0:3]
            gate, gate_t, b_cols, b_rows, tot_rows = terms[d]
            ci = d * 8 + hd
            cf = d * 8 + 4 + hd
            unit = d * HEADS + hd
            k = k_ref[0, :, hsl]
            qt = qt_ref[0, hsl, :]
            if st is None or nc > 1:
                st = _dot(k, qt)
            mask = le if d == 0 else ge
            h, s_new, m_new = _scan_unit(
                st, k, qt, vt_ref[0, hsl, :],
                gate[:, ci:ci + 1] - b_cols[:, cf:cf + 1],
                gate_t[ci:ci + 1, :] - b_rows[cf:cf + 1, :],
                b_rows[cf:cf + 1, :], tot_rows[cf:cf + 1, :],
                mask, s_scr[unit], m_scr[unit][:, 0:1], use_state)
            s_scr[unit] = s_new
            m_scr[unit] = jnp.broadcast_to(m_new, (1, LANES))
            hs[d].append(h)
    for d in range(N_DIR):
        h_refs[d][0] = jnp.concatenate(hs[d], axis=1).astype(BF16)

    if emit_state:
        @pl.when(j == nc - 1)
        def _():
            for unit in range(N_DIR * HEADS):
                s = s_scr[unit]
                c_out[0, unit] = s[:dh].T
                n_out[0, unit] = s[dh:dh + 1]
                m_out[0, unit] = m_scr[unit]


def _mlstm(k, qt, vt, gate, gate_t, s0, m0, emit_state):
    b, t, _ = k.shape
    nc = t // CHUNK
    zero_init = s0 is None
    nu = N_DIR * HEADS
    fwd = lambda w: pl.BlockSpec((1, CHUNK, w), lambda i, j: (i, j, 0))
    bwd = lambda w: pl.BlockSpec((1, CHUNK, w), lambda i, j: (i, nc - 1 - j, 0))
    fwd_t = lambda r: pl.BlockSpec((1, r, CHUNK), lambda i, j: (i, 0, j))
    bwd_t = lambda r: pl.BlockSpec((1, r, CHUNK), lambda i, j: (i, 0, nc - 1 - j))
    args = [k, qt, vt, gate, gate_t]
    in_specs = [fwd(512), fwd_t(512), fwd_t(512), fwd(LANES), fwd_t(16)]
    if nc > 1:
        args += [k, qt, vt, gate, gate_t]
        in_specs += [bwd(512), bwd_t(512), bwd_t(512), bwd(LANES), bwd_t(16)]
    if not zero_init:
        args += [s0, m0]
        in_specs += [pl.BlockSpec((1, nu, 2 * HEAD_DIM, HEAD_DIM), lambda i, j: (i, 0, 0, 0)),
                     pl.BlockSpec((1, nu, 1, LANES), lambda i, j: (i, 0, 0, 0))]
    sd = jax.ShapeDtypeStruct
    out_shape = [sd((b, t, 512), BF16), sd((b, t, 512), BF16)]
    out_specs = [fwd(512), bwd(512)]
    if emit_state:
        out_shape += [sd((b, nu, HEAD_DIM, HEAD_DIM), F32), sd((b, nu, 1, HEAD_DIM), F32),
                      sd((b, nu, 1, LANES), F32)]
        out_specs += [pl.BlockSpec((1, nu, HEAD_DIM, HEAD_DIM), lambda i, j: (i, 0, 0, 0)),
                      pl.BlockSpec((1, nu, 1, HEAD_DIM), lambda i, j: (i, 0, 0, 0)),
                      pl.BlockSpec((1, nu, 1, LANES), lambda i, j: (i, 0, 0, 0))]
    return pl.pallas_call(
        functools.partial(_mlstm_kernel, nc=nc, zero_init=zero_init, emit_state=emit_state),
        grid=(b, nc),
        in_specs=in_specs,
        out_specs=out_specs,
        out_shape=out_shape,
        scratch_shapes=[pltpu.VMEM((nu, 2 * HEAD_DIM, HEAD_DIM), F32),
                        pltpu.VMEM((nu, 1, LANES), F32)],
        compiler_params=_params("arbitrary", "arbitrary"),
        name="mlstm",
    )(*args)


def _outproj_kernel(x_ref, u_ref, hf_ref, hb_ref, o_ref, mod_ref, pm_ref, pinv_ref, wp_ref, ps_ref,
                    wo_ref, gpm_ref, gpf_ref, wrt_ref, x1_ref, h2_ref, xp_ref, lg_ref, slab_ref, *, tm):
    i = pl.program_id(1)
    x = x_ref[0]
    mod = mod_ref[0]
    row0 = pl.multiple_of(i * tm, tm)
    u_full = u_ref[0]
    u_tile = u_ref[0, pl.ds(row0, tm), :].astype(F32)
    mix = jnp.zeros((tm, D_MODEL), F32)
    for g in range(POOL_GROUPS):
        sl = slice(g * POOL_GROUP_DIM, (g + 1) * POOL_GROUP_DIM)
        box = _dot(pm_ref[g], u_full[:, sl])
        diff = box * pinv_ref[g] - u_tile[:, sl]
        yp = _dot(diff.astype(BF16), wp_ref[g]) * ps_ref[:, sl]
        mix = mix + _dot(yp.astype(BF16), wo_ref[sl, :])
    hsum = hf_ref[0].astype(F32) + hb_ref[0].astype(F32)
    yml = jax.nn.sigmoid(o_ref[0].astype(F32)) * hsum
    mix = mix + _dot(yml.astype(BF16), wo_ref[POOL_WIDTH:, :])
    x1 = x + mod[2:3] * _rmsnorm(mix, gpm_ref[...])
    x1_ref[0] = x1
    h2 = _rmsnorm(x1, gpf_ref[...]) * (1.0 + mod[4:5]) + mod[3:4]
    p1, p2, p3 = _split3(h2)
    h2_ref[0] = p1
    for cc in range(D_MODEL // LANES):
        slab_ref[:, cc, :] = h2[:, cc * LANES:(cc + 1) * LANES]
    xp_ref[0] = slab_ref[...].astype(BF16)
    w1, w2, w3 = _split3(wrt_ref[...])
    ws = jnp.concatenate([w1, w2, w3], axis=0)
    r1 = _dot_nt(ws, p1)
    r2 = _dot_nt(ws[:128], p2)
    r3 = _dot_nt(w1, p3)
    lg_ref[0] = r1[0:64] + r1[64:128] + r1[128:192] + r2[0:64] + r2[64:128] + r3


def _outproj(x, u, hf, hb, o, mod, mod_row, pm, pinv, wp, ps, wo, gpm, gpf, wrt):
    b, t, _ = x.shape
    tm = TOKEN_TILE
    const = lambda *shape: pl.BlockSpec(shape, lambda i, j: (0,) * len(shape))
    tok = lambda w: pl.BlockSpec((1, tm, w), lambda i, j: (i, j, 0))
    sd = jax.ShapeDtypeStruct
    return pl.pallas_call(
        functools.partial(_outproj_kernel, tm=tm),
        grid=(b, t // tm),
        in_specs=[tok(D_MODEL),
                  pl.BlockSpec((1, t, 512), lambda i, j: (i, 0, 0)),
                  tok(512), tok(512), tok(512),
                  pl.BlockSpec((1, N_MOD, D_MODEL), lambda i, j: (mod_row(i), 0, 0)),
                  pl.BlockSpec((POOL_GROUPS, tm, t), lambda i, j: (0, j, 0)),
                  pl.BlockSpec((POOL_GROUPS, tm, 1), lambda i, j: (0, j, 0)),
                  const(POOL_GROUPS, POOL_GROUP_DIM, POOL_GROUP_DIM), const(1, POOL_WIDTH),
                  const(D_MODEL, D_MODEL), const(1, D_MODEL), const(1, D_MODEL),
                  const(N_EXPERTS, D_MODEL)],
        out_specs=[tok(D_MODEL), tok(D_MODEL),
                   pl.BlockSpec((1, tm, D_MODEL // LANES, LANES), lambda i, j: (i, j, 0, 0)),
                   pl.BlockSpec((1, N_EXPERTS, tm), lambda i, j: (i, 0, j))],
        out_shape=[sd((b, t, D_MODEL), F32), sd((b, t, D_MODEL), BF16),
                   sd((b, t, D_MODEL // LANES, LANES), BF16), sd((b, N_EXPERTS, t), F32)],
        scratch_shapes=[pltpu.VMEM((tm, D_MODEL // LANES, LANES), F32)],
        compiler_params=_params("arbitrary", "arbitrary"),
        name="outproj",
    )(x, u, hf, hb, o, mod, pm, pinv, wp, ps, wo, gpm, gpf, wrt)


def _router_kernel(lg_ref, br_ref, comb_ref, sel_ref):
    s = jax.nn.sigmoid(lg_ref[...])
    biased = s + br_ref[...]
    gidx = lax.broadcasted_iota(jnp.int32, s.shape, 0)
    jidx = lax.broadcasted_iota(jnp.int32, s.shape, 1)
    neg = -jnp.inf
    m1 = jnp.max(biased, axis=1, keepdims=True)
    i1 = jnp.min(jnp.where(biased == m1, jidx, GROUP_SIZE), axis=1, keepdims=True)
    m2 = jnp.max(jnp.where(jidx == i1, neg, biased), axis=1, keepdims=True)
    gscore = m1 + m2
    gi = lax.broadcasted_iota(jnp.int32, gscore.shape, 0)
    gmask = jnp.zeros(gscore.shape, F32)
    cur = gscore
    for _ in range(TOPK_GROUPS):
        mx = jnp.max(cur, axis=0, keepdims=True)
        ix = jnp.min(jnp.where(cur == mx, gi, N_EXPERT_GROUPS), axis=0, keepdims=True)
        hit = gi == ix
        gmask = jnp.where(hit, 1.0, gmask)
        cur = jnp.where(hit, neg, cur)
    cur = jnp.where(gmask > 0, biased, neg)
    eidx = gidx * GROUP_SIZE + jidx
    selmask = jnp.zeros(s.shape, F32)
    for _ in range(TOP_K):
        mx = jnp.max(jnp.max(cur, axis=1, keepdims=True), axis=0, keepdims=True)
        ix = jnp.where(cur == mx, eidx, N_EXPERTS)
        ix = jnp.min(jnp.min(ix, axis=1, keepdims=True), axis=0, keepdims=True)
        hit = eidx == ix
        selmask = jnp.where(hit, 1.0, selmask)
        cur = jnp.where(hit, neg, cur)
    sel = selmask * s
    tot = jnp.sum(jnp.sum(sel, axis=1, keepdims=True), axis=0, keepdims=True)
    comb_ref[...] = sel / tot * ROUTED_SCALE
    sel_ref[...] = selmask


def _router(logits_t, b_router):
    t = logits_t.shape[1]
    tl = 1024
    shp = (N_EXPERT_GROUPS, GROUP_SIZE, t)
    blk = pl.BlockSpec((N_EXPERT_GROUPS, GROUP_SIZE, tl), lambda j: (0, 0, j))
    comb, sel = pl.pallas_call(
        _router_kernel,
        grid=(t // tl,),
        in_specs=[blk, pl.BlockSpec((N_EXPERT_GROUPS, GROUP_SIZE, 1), lambda j: (0, 0, 0))],
        out_specs=[blk, blk],
        out_shape=[jax.ShapeDtypeStruct(shp, F32), jax.ShapeDtypeStruct(shp, F32)],
        compiler_params=_params("arbitrary"),
        name="router",
    )(logits_t.reshape(shp), b_router.reshape(N_EXPERT_GROUPS, GROUP_SIZE, 1))
    return comb.reshape(N_EXPERTS, t), sel.reshape(N_EXPERTS, t)


def _plan_kernel(sel_ref, pos_ref, meta_ref, emeta_ref, *, n_meta):
    t = sel_ref.shape[1]
    tm = float(MOE_TILE)
    sel = sel_ref[...]
    selb = sel.astype(BF16)
    blk = 256
    rr = lax.broadcasted_iota(jnp.int32, (blk, blk), 0)
    cc = lax.broadcasted_iota(jnp.int32, (blk, blk), 1)
    before = (rr < cc).astype(BF16)
    carry = jnp.zeros((N_EXPERTS, 1), F32)
    ranks = []
    for b in range(t // blk):
        sb = selb[:, b * blk:(b + 1) * blk]
        ranks.append(_dot(sb, before) + carry)
        carry = carry + jnp.sum(sel[:, b * blk:(b + 1) * blk], axis=1, keepdims=True)
    rank = jnp.concatenate(ranks, axis=1)
    cnt = carry
    ntile = jnp.floor((cnt + (tm - 1.0)) * (1.0 / tm))
    er = lax.broadcasted_iota(jnp.int32, (N_EXPERTS, N_EXPERTS), 0)
    ec = lax.broadcasted_iota(jnp.int32, (N_EXPERTS, N_EXPERTS), 1)
    below = (ec < er).astype(BF16)
    tstart = _dot(below, jnp.broadcast_to(ntile, (N_EXPERTS, LANES)).astype(BF16))[:, 0:1]
    pos = tstart * tm + rank
    erank = _dot(below, selb)
    rows = []
    for k in range(TOP_K):
        hit = (sel > 0.0) & (erank == float(k))
        rows.append(jnp.sum(jnp.where(hit, pos, 0.0), axis=0, keepdims=True))
    rows += [jnp.zeros((1, t), F32)] * (8 - TOP_K)
    pos_ref[...] = jnp.concatenate(rows, axis=0).astype(jnp.int32)

    tau = lax.broadcasted_iota(jnp.int32, (N_EXPERTS, n_meta), 1).astype(F32)
    eidx = lax.broadcasted_iota(jnp.int32, (N_EXPERTS, n_meta), 0).astype(F32)
    te = jnp.sum(((tstart + ntile) <= tau).astype(F32), axis=0, keepdims=True)
    te = jnp.minimum(te, float(N_EXPERTS - 1))
    onehot = eidx == te
    cnt_t = jnp.sum(jnp.where(onehot, cnt, 0.0), axis=0, keepdims=True)
    ts_t = jnp.sum(jnp.where(onehot, tstart, 0.0), axis=0, keepdims=True)
    tr = jnp.clip(cnt_t - (tau[0:1] - ts_t) * tm, 0.0, tm)
    tf = jnp.where((tau[0:1] == ts_t) & (tr > 0.0), 1.0, 0.0)
    meta_ref[...] = jnp.concatenate([te, tr, tf] + [jnp.zeros((1, n_meta), F32)] * 5, axis=0).astype(jnp.int32)

    eye = (lax.broadcasted_iota(jnp.int32, (N_EXPERTS, LANES), 0)
           == lax.broadcasted_iota(jnp.int32, (N_EXPERTS, LANES), 1))
    as_row = lambda col: jnp.sum(jnp.where(eye, col, 0.0), axis=0, keepdims=True)
    emeta_ref[...] = jnp.concatenate([as_row(cnt), as_row(tstart), as_row(ntile)]
                                     + [jnp.zeros((1, LANES), F32)] * 5, axis=0).astype(jnp.int32)


def _plan(sel, n_meta):
    t = sel.shape[1]
    sd = jax.ShapeDtypeStruct
    return pl.pallas_call(
        functools.partial(_plan_kernel, n_meta=n_meta),
        out_shape=[sd((8, t), jnp.int32), sd((8, n_meta), jnp.int32), sd((8, LANES), jnp.int32)],
        compiler_params=pltpu.CompilerParams(vmem_limit_bytes=VMEM_LIMIT),
        name="plan",
    )(sel)


def _fill_slot_table(pos_ref, emeta_ref, tbl_ref, n_tokens):
    group = 32
    def pad_expert(e, carry):
        n_t = emeta_ref[2 * LANES + e]
        start = (emeta_ref[LANES + e] + n_t - 1) * MOE_TILE

        @pl.when(n_t > 0)
        def _():
            def put(i, c):
                for u in range(group):
                    tbl_ref[start + i * group + u] = n_tokens
                return c
            lax.fori_loop(0, MOE_TILE // group, put, 0)
        return carry
    lax.fori_loop(0, N_EXPERTS, pad_expert, 0)
    last = N_EXPERTS - 1
    def pad_tile(tile, carry):
        def put(i, c):
            for u in range(group):
                tbl_ref[tile * MOE_TILE + i * group + u] = n_tokens
            return c
        lax.fori_loop(0, MOE_TILE // group, put, 0)
        return carry
    lax.fori_loop(emeta_ref[LANES + last] + emeta_ref[2 * LANES + last], tbl_ref.shape[0] // MOE_TILE, pad_tile, 0)

    unroll = 8
    def scatter(i, carry):
        first = i * (8 * unroll)
        for u in range(unroll):
            for k in range(TOP_K):
                tbl_ref[pos_ref[first + (8 * u + k)]] = i * unroll + u
        return carry
    lax.fori_loop(0, n_tokens // unroll, scatter, 0)


def _experts_kernel(te_ref, tr_ref, pos_ref, emeta_ref, xc_ref, xs_ref, comb_ref, wgu_ref, wd_ref, acc_out,
                    tbl_ref, xbuf, acc, stage0, stage1, cst0, cst1, act0, act1, ybuf0, ybuf1,
                    wgu_b0, wgu_b1, wd_b0, wd_b1, sems):
    s = pl.program_id(0)
    n_tiles = te_ref.shape[0]
    tm = MOE_TILE
    nch = D_MODEL // LANES
    tile_at = lambda lag: jnp.clip(s - lag, 0, n_tiles - 1)
    t_g, t_1, t_2, t_3 = tile_at(0), tile_at(1), tile_at(2), tile_at(3)

    @pl.when(s == 0)
    def _():
        tc, ts = xc_ref.shape[0], xs_ref.shape[0]
        copies = (pltpu.make_async_copy(xc_ref, xbuf.at[pl.ds(0, tc)], sems.at[0]),
                  pltpu.make_async_copy(xs_ref, xbuf.at[pl.ds(tc, ts)], sems.at[1]))
        for cp in copies:
            cp.start()
        n_pad = xbuf.shape[0] - tc - ts
        xbuf[pl.ds(tc + ts, n_pad)] = jnp.zeros((n_pad,) + xbuf.shape[1:], BF16)
        for ref in (acc, stage0, stage1, cst0, cst1, act0, act1, ybuf0, ybuf1, wgu_b0, wgu_b1, wd_b0, wd_b1):
            ref[...] = jnp.zeros_like(ref)
        _fill_slot_table(pos_ref, emeta_ref, tbl_ref, tc + ts)
        for cp in copies:
            cp.wait()

    def gather(tile, stage, cst):
        base = tile * tm
        for j in range(tm):
            tok = tbl_ref[base + j]
            stage[pl.ds(j * nch, nch), :] = xbuf[tok].astype(F32)
            cst[pl.ds(j, 1), :] = comb_ref[pl.ds(tok, 1), :]

    def gate_up(tile, stage, cst, wgu_b, act):
        xb = jnp.concatenate([stage[pl.ds(cc, tm, stride=nch), :] for cc in range(nch)], axis=1).astype(BF16)
        gu = _dot(xb, wgu_b[...])
        lane = lax.broadcasted_iota(jnp.int32, (1, LANES), 1)
        w_col = jnp.sum(jnp.where(lane == te_ref[tile], cst[...], 0.0), axis=1, keepdims=True)
        act[...] = (_silu(gu[:, :EXPERT_DIM]) * gu[:, EXPERT_DIM:] * w_col).astype(BF16)

    def down(act, wd_b, ybuf):
        y = _dot(act[...], wd_b[...])
        for cc in range(nch):
            ybuf[cc * Y_PITCH:cc * Y_PITCH + tm, :] = y[:, cc * LANES:(cc + 1) * LANES]

    def scatter(tile, ybuf):
        base = tile * tm
        sc_n = 16
        for i in range(tm // sc_n):
            toks = [tbl_ref[base + i * sc_n + u] for u in range(sc_n)]
            olds = [acc[toks[u]] for u in range(sc_n)]
            news = [olds[u] + ybuf[pl.ds(i * sc_n + u, nch, stride=Y_PITCH), :] for u in range(sc_n)]
            for u in range(sc_n):
                acc[toks[u]] = news[u]

    busy = (tr_ref[t_g] + tr_ref[t_1] + tr_ref[t_2] + tr_ref[t_3]) > 0
    bufs = ((stage0, cst0, act0, ybuf0, wgu_b0, wd_b0), (stage1, cst1, act1, ybuf1, wgu_b1, wd_b1))
    for par in range(2):
        stage_p, cst_p, act_p, ybuf_p, wgu_p, wd_p = bufs[par]
        stage_q, cst_q, act_q, ybuf_q, wgu_q, wd_q = bufs[1 - par]

        @pl.when(busy & (s % 2 == par))
        def _():
            gather(t_g, stage_p, cst_p)
            wgu_p[...] = wgu_ref[0].astype(BF16)
            gate_up(t_1, stage_q, cst_q, wgu_q, act_q)
            wd_q[...] = wd_ref[0].astype(BF16)
            down(act_p, wd_p, ybuf_p)
            scatter(t_3, ybuf_q)

    @pl.when(s == pl.num_programs(0) - 1)
    def _():
        cp = pltpu.make_async_copy(acc, acc_out, sems.at[2])
        cp.start()
        cp.wait()


def _experts(te, tr, pos, emeta, xc, xs, comb, wgu, wd):
    n_tok = comb.shape[0]
    n_tiles = te.shape[0]
    tm = MOE_TILE
    nch = D_MODEL // LANES
    vm = pltpu.VMEM
    grid_spec = pltpu.PrefetchScalarGridSpec(
        num_scalar_prefetch=4,
        grid=(n_tiles + 3,),
        in_specs=[pl.BlockSpec(memory_space=pl.ANY), pl.BlockSpec(memory_space=pl.ANY),
                  pl.BlockSpec((n_tok, LANES), lambda s, *_: (0, 0), pipeline_mode=pl.Buffered(1)),
                  pl.BlockSpec((1, D_MODEL, 2 * EXPERT_DIM), lambda s, te, *_: (te[jnp.minimum(s, n_tiles - 1)], 0, 0)),
                  pl.BlockSpec((1, EXPERT_DIM, D_MODEL), lambda s, te, *_: (te[jnp.clip(s - 1, 0, n_tiles - 1)], 0, 0))],
        out_specs=pl.BlockSpec(memory_space=pl.ANY),
        scratch_shapes=[pltpu.SMEM((n_tiles * tm,), jnp.int32),
                        vm((n_tok, nch, LANES), BF16), vm((n_tok, nch, LANES), F32),
                        vm((tm * nch, LANES), F32), vm((tm * nch, LANES), F32),
                        vm((tm, LANES), F32), vm((tm, LANES), F32),
                        vm((tm, EXPERT_DIM), BF16), vm((tm, EXPERT_DIM), BF16),
                        vm((nch * Y_PITCH, LANES), F32), vm((nch * Y_PITCH, LANES), F32),
                        vm((D_MODEL, 2 * EXPERT_DIM), BF16), vm((D_MODEL, 2 * EXPERT_DIM), BF16),
                        vm((EXPERT_DIM, D_MODEL), BF16), vm((EXPERT_DIM, D_MODEL), BF16),
                        pltpu.SemaphoreType.DMA((3,))],
    )
    return pl.pallas_call(
        _experts_kernel,
        grid_spec=grid_spec,
        out_shape=jax.ShapeDtypeStruct((n_tok, nch, LANES), F32),
        compiler_params=pltpu.CompilerParams(dimension_semantics=("arbitrary",),
                                             vmem_limit_bytes=EXPERTS_VMEM_LIMIT),
        name="experts",
    )(te, tr, pos, emeta, xc, xs, comb, wgu, wd)


def _final_kernel(acc_ref, h_ref, x1_ref, mod_ref, wsg_ref, wsd_ref, gpo_ref, out_ref):
    routed = jnp.concatenate([acc_ref[:, cc, :] for cc in range(D_MODEL // LANES)], axis=1)
    gs = _dot(h_ref[...], wsg_ref[...])
    act = _silu(gs[:, :SHARED_DIM]) * gs[:, SHARED_DIM:]
    f = routed + _dot(act.astype(BF16), wsd_ref[...])
    out_ref[...] = x1_ref[...] + mod_ref[0][5:6] * _rmsnorm(f, gpo_ref[...])


def _final(acc, tile0, h2, x1, mod, mod_row, wsg, wsd, gpo):
    n = h2.shape[0]
    tm = TOKEN_TILE
    const = lambda *shape: pl.BlockSpec(shape, lambda i: (0,) * len(shape))
    tok = lambda w: pl.BlockSpec((tm, w), lambda i: (i, 0))
    return pl.pallas_call(
        _final_kernel,
        grid=(n // tm,),
        in_specs=[pl.BlockSpec((tm, D_MODEL // LANES, LANES), lambda i: (i + tile0, 0, 0)),
                  tok(D_MODEL), tok(D_MODEL),
                  pl.BlockSpec((1, N_MOD, D_MODEL), lambda i: (mod_row(i), 0, 0)),
                  const(D_MODEL, 2 * SHARED_DIM), const(SHARED_DIM, D_MODEL), const(1, D_MODEL)],
        out_specs=tok(D_MODEL),
        out_shape=jax.ShapeDtypeStruct((n, D_MODEL), F32),
        compiler_params=_params("arbitrary"),
        name="final",
    )(acc, h2, x1, mod, wsg, wsd, gpo)


def _window_bounds(n, w):
    idx = np.arange(n)
    return np.clip(idx - w // 2, 0, n), np.clip(idx + w - w // 2, 0, n)


def _pool_operators(t, grid):
    mats, invs = [], []
    for w in POOL_WINDOWS:
        if grid:
            rlo, rhi = _window_bounds(t // GRID_W, w)
            clo, chi = _window_bounds(GRID_W, w)
            r = np.arange(t) // GRID_W
            c = np.arange(t) % GRID_W
            m = ((r[None, :] >= rlo[r][:, None]) & (r[None, :] < rhi[r][:, None])
                 & (c[None, :] >= clo[c][:, None]) & (c[None, :] < chi[c][:, None]))
            cnt = (rhi - rlo)[r] * (chi - clo)[c]
        else:
            lo, hi = _window_bounds(t, w)
            sidx = np.arange(t)
            m = (sidx[None, :] >= lo[:, None]) & (sidx[None, :] < hi[:, None])
            cnt = hi - lo
        mats.append(m.astype(np.float32))
        invs.append((1.0 / cnt.astype(np.float64)).astype(np.float32)[:, None])
    return jnp.asarray(np.stack(mats), BF16), jnp.asarray(np.stack(invs), F32)


def kernel(x_prompt, x_sample, state_C, state_n, state_m, c, c_ctx, w_ada, b_ada, g_pre_mix, w_in, b_gate,
           w_pool, pool_scale, w_out, g_post_mix, g_pre_ffn, w_router, b_router, w_expert_gu, w_expert_down,
           w_shared_gu, w_shared_down, g_post_ffn):
    b_ctx = x_prompt.shape[0]
    b_lat = x_sample.shape[0]
    nu = N_DIR * HEADS
    l = 0
    row = lambda a: a[l].reshape(1, -1).astype(F32)

    cvec = jnp.zeros((16, D_MODEL), F32).at[0].set(c_ctx.astype(F32)).at[1:1 + b_lat].set(c.astype(F32))
    mod = _mod_rows(cvec, w_ada[l], b_ada[l]).reshape(16, N_MOD, D_MODEL)

    w_in_l = w_in[l]
    p0 = POOL_WIDTH
    mw = MLSTM_WIDTH
    w_u, w_q, w_k, w_v, w_o = (w_in_l[:, lo:lo + 512] for lo in (0, p0, p0 + mw, p0 + 2 * mw, p0 + 3 * mw))
    wm = jnp.concatenate([w_u, w_k, w_o], axis=1).astype(BF16)
    wt = jnp.concatenate([w_q.T, w_v.T], axis=0).astype(BF16)
    wg_cols = w_in_l[:, p0 + 4 * mw:]
    wg = jnp.pad(wg_cols, ((0, 0), (0, LANES - GATE_COLS)))
    wgt = wg_cols.T
    bg = b_gate[l].reshape(GATE_COLS).astype(F32)
    bgr = jnp.pad(bg, (0, LANES - GATE_COLS)).reshape(1, LANES)
    bgc = bg.reshape(GATE_COLS, 1)
    wp = w_pool[l].astype(BF16)
    wo = w_out[l].astype(BF16)
    wrt = w_router[l].T
    wsg = w_shared_gu[l].astype(BF16)
    wsd = w_shared_down[l].astype(BF16)

    def mixer(x, mod_row, grid, s0, m0, emit_state):
        t = x.shape[1]
        u, k, o, qt, vt, gate, gate_t = _inproj(x.astype(F32), mod, mod_row, row(g_pre_mix), wm, wt, wg, wgt,
                                                bgr, bgc)
        outs = _mlstm(k, qt, vt, gate, gate_t, s0, m0, emit_state)
        hf, hb = outs[0], outs[1]
        pm, pinv = _pool_operators(t, grid)
        x1, h2, xp, lg = _outproj(x.astype(F32), u, hf, hb, o, mod, mod_row, pm, pinv, wp, row(pool_scale), wo,
                                  row(g_post_mix), row(g_pre_ffn), wrt)
        return x1, h2, xp, lg, outs[2:]

    ctx_row = lambda i: 0
    lat_row = lambda i: i + 1
    x1c, h2c, xpc, lgc, (c_new, n_new, m_new) = mixer(x_prompt, ctx_row, False, None, None, True)
    s0 = jnp.concatenate(
        [jnp.swapaxes(state_C[:, l].reshape(b_lat, nu, HEAD_DIM, HEAD_DIM).astype(F32), -1, -2),
         jnp.broadcast_to(state_n[:, l].reshape(b_lat, nu, 1, HEAD_DIM).astype(F32),
                          (b_lat, nu, HEAD_DIM, HEAD_DIM))], axis=-2)
    m0 = jnp.broadcast_to(state_m[:, l].reshape(b_lat, nu, 1, 1).astype(F32), (b_lat, nu, 1, LANES))
    x1s, h2s, xps, lgs, _ = mixer(x_sample, lat_row, True, s0, m0, False)

    tc = b_ctx * x_prompt.shape[1]
    ts = b_lat * x_sample.shape[1]
    n_tok = tc + ts
    lg_all = jnp.concatenate([lgc.transpose(1, 0, 2).reshape(N_EXPERTS, tc),
                              lgs.transpose(1, 0, 2).reshape(N_EXPERTS, ts)], axis=1)
    comb, sel = _router(lg_all, b_router[l].astype(F32))
    n_tiles = n_tok * TOP_K // MOE_TILE + N_EXPERTS
    n_meta = -(-n_tiles // LANES) * LANES
    pos, meta, emeta = _plan(sel, n_meta)

    comb_tok = jnp.pad(comb.T, ((0, 8), (0, LANES - N_EXPERTS)))
    slab = (D_MODEL // LANES, LANES)
    acc = _experts(meta[0, :n_tiles], meta[1, :n_tiles],
                   pos.T.reshape(-1), emeta.reshape(-1),
                   xpc.reshape((tc,) + slab), xps.reshape((ts,) + slab), comb_tok,
                   w_expert_gu[l], w_expert_down[l])

    fin = functools.partial(_final, wsg=wsg, wsd=wsd, gpo=row(g_post_ffn))
    tiles_per_lat = x_sample.shape[1] // TOKEN_TILE
    yc = fin(acc, 0, h2c.reshape(tc, D_MODEL), x1c.reshape(tc, D_MODEL), mod, ctx_row)
    ys = fin(acc, tc // TOKEN_TILE, h2s.reshape(ts, D_MODEL), x1s.reshape(ts, D_MODEL), mod,
             lambda i: i // tiles_per_lat + 1)

    new_c = c_new.reshape(b_ctx, 1, N_DIR, HEADS, HEAD_DIM, HEAD_DIM)
    new_n = n_new.reshape(b_ctx, 1, N_DIR, HEADS, HEAD_DIM)
    new_m = m_new[..., 0].reshape(b_ctx, 1, N_DIR, HEADS)
    return (yc.reshape(x_prompt.shape), ys.reshape(x_sample.shape), new_c, new_n, new_m)
```

```python
import functools

import jax
import jax.numpy as jnp
import numpy as np
from jax import lax
from jax.experimental import pallas as pl
from jax.experimental.pallas import tpu as pltpu

F32 = jnp.float32
BF16 = jnp.bfloat16

D_MODEL = 1024
GRID_W = 64
POOL_WIDTH = 512
POOL_GROUPS = 4
POOL_GROUP_DIM = 128
POOL_WINDOWS = (2, 4, 8, 16)
HEADS = 4
HEAD_DIM = 128
MLSTM_WIDTH = HEADS * HEAD_DIM
N_DIR = 2
GATE_COLS = N_DIR * 2 * HEADS
N_EXPERTS = 64
TOP_K = 6
N_EXPERT_GROUPS = 8
GROUP_SIZE = N_EXPERTS // N_EXPERT_GROUPS
TOPK_GROUPS = 4
EXPERT_DIM = 256
SHARED_DIM = 256
ROUTED_SCALE = 2.5
N_MOD = 6
EPS = 1e-6
K_SCALE = HEAD_DIM ** -0.5

LANES = 128
CHUNK = 256
TOKEN_TILE = 256
FINAL_TILE = 512
MOE_TILE = 256
Y_PITCH = MOE_TILE + 8
VMEM_LIMIT = 56 * 1024 * 1024
EXPERTS_VMEM_LIMIT = 58 * 1024 * 1024


def _split3(x):
    p1 = x.astype(BF16)
    r1 = x - p1.astype(F32)
    p2 = r1.astype(BF16)
    p3 = (r1 - p2.astype(F32)).astype(BF16)
    return p1, p2, p3


def _split2(x):
    p1 = x.astype(BF16)
    p2 = (x - p1.astype(F32)).astype(BF16)
    return p1, p2


def _dot(a, b):
    return jnp.dot(a, b, preferred_element_type=F32)


def _dot_nt(a, b):
    return lax.dot_general(a, b, (((1,), (1,)), ((), ())), preferred_element_type=F32)


def _rmsnorm(x, g):
    return x * lax.rsqrt(jnp.mean(x * x, axis=-1, keepdims=True) + EPS) * g


def _silu(x):
    return x * jax.nn.sigmoid(x)


def _params(*sem):
    return pltpu.CompilerParams(dimension_semantics=sem, vmem_limit_bytes=VMEM_LIMIT)


def _mod_kernel(c_ref, w_ref, b_ref, o_ref):
    a = _silu(c_ref[...])
    a_stack = jnp.concatenate(_split3(a), axis=0)
    w1, w2 = _split2(w_ref[...])
    r1 = _dot(a_stack, w1)
    r2 = _dot(a_stack[:32], w2)
    o_ref[...] = (r1[0:16] + r1[16:32] + r1[32:48] + r2[0:16] + r2[16:32]) + b_ref[...]


def _mod_rows(cvec, w_ada, b_ada):
    n = N_MOD * D_MODEL
    tn = 1536
    return pl.pallas_call(
        _mod_kernel,
        grid=(n // tn,),
        in_specs=[pl.BlockSpec((16, D_MODEL), lambda j: (0, 0)),
                  pl.BlockSpec((D_MODEL, tn), lambda j: (0, j)),
                  pl.BlockSpec((1, tn), lambda j: (0, j))],
        out_specs=pl.BlockSpec((16, tn), lambda j: (0, j)),
        out_shape=jax.ShapeDtypeStruct((16, n), F32),
        compiler_params=_params("arbitrary"),
        name="mod",
    )(cvec, w_ada, b_ada.reshape(1, n))


def _inproj_kernel(x_ref, mod_ref, g_ref, wm_ref, wt_ref, wg1_ref, wg2_ref, bgr_ref,
                   u_ref, k_ref, o_ref, qt_ref, vt_ref, gate_ref, gatet_ref):
    x = x_ref[0]
    mod = mod_ref[0]
    h = _rmsnorm(x, g_ref[...]) * (1.0 + mod[1:2]) + mod[0:1]
    h1, h2, h3 = _split3(h)
    z = _dot(h1, wm_ref[...])
    u_ref[0] = z[:, 0:512].astype(BF16)
    k_ref[0] = (z[:, 512:1024] * K_SCALE).astype(BF16)
    o_ref[0] = z[:, 1024:1536].astype(BF16)
    zt = _dot_nt(wt_ref[...], h1)
    qt_ref[0] = zt[0:512].astype(BF16)
    vt_ref[0] = zt[512:1024].astype(BF16)
    tm = x.shape[0]
    hs = jnp.concatenate([h1, h2, h3], axis=0)
    r1 = _dot(hs, wg1_ref[...])
    r2 = _dot(hs[:2 * tm], wg2_ref[...])
    gate = (r1[0:tm] + r1[tm:2 * tm] + r1[2 * tm:] + r2[0:tm] + r2[tm:]) + bgr_ref[...]
    gate_ref[0] = gate
    gatet_ref[0] = gate.T[0:16]


def _inproj(x, mod, mod_row, g, wm, wt, wg1, wg2, bgr):
    b, t, _ = x.shape
    tm = min(t, 2 * TOKEN_TILE)
    const = lambda *shape: pl.BlockSpec(shape, lambda i, j: (0,) * len(shape))
    tok = lambda w: pl.BlockSpec((1, tm, w), lambda i, j: (i, j, 0))
    tok_t = lambda r: pl.BlockSpec((1, r, tm), lambda i, j: (i, 0, j))
    sd = jax.ShapeDtypeStruct
    return pl.pallas_call(
        _inproj_kernel,
        grid=(b, t // tm),
        in_specs=[tok(D_MODEL),
                  pl.BlockSpec((1, N_MOD, D_MODEL), lambda i, j: (mod_row(i), 0, 0)),
                  const(1, D_MODEL), const(D_MODEL, 1536), const(1024, D_MODEL),
                  const(D_MODEL, LANES), const(D_MODEL, LANES), const(1, LANES)],
        out_specs=[tok(512), tok(512), tok(512), tok_t(512), tok_t(512), tok(LANES), tok_t(16)],
        out_shape=[sd((b, t, 512), BF16), sd((b, t, 512), BF16), sd((b, t, 512), BF16),
                   sd((b, 512, t), BF16), sd((b, 512, t), BF16), sd((b, t, LANES), F32),
                   sd((b, 16, t), F32)],
        compiler_params=_params("arbitrary", "arbitrary"),
        name="inproj",
    )(x, mod, g, wm, wt, wg1, wg2, bgr)


def _log_sigmoid(x):
    return jnp.minimum(x, 0.0) - jnp.log1p(jnp.exp(-jnp.abs(x)))


def _scan_unit(st, k, qt, vt, u_col, u_row, b_row, btot, mask, s_prev, m_prev, use_state):
    dh = HEAD_DIM
    n = st.shape[0]
    ub = jnp.where(mask, jnp.broadcast_to(u_col, (n, n)), -jnp.inf)
    z = jnp.maximum(m_prev, jnp.max(ub, axis=0, keepdims=True))
    p = (jnp.exp(ub - z) * st).astype(BF16)
    ones = jnp.ones((dh, n), BF16)
    tot = _dot(jnp.concatenate([vt, ones], axis=0), p)
    if use_state:
        tot = tot + jnp.exp(m_prev - z) * _dot(s_prev.astype(BF16), qt)
    floor = jnp.exp(-(b_row + z))
    h_t = tot[:dh] / jnp.maximum(jnp.abs(tot[dh:]), floor)
    g_row = btot + u_row
    m_new = jnp.maximum(btot + m_prev, jnp.max(g_row, axis=-1, keepdims=True))
    w_row = jnp.exp(g_row - m_new)
    vw = jnp.concatenate([(vt.astype(F32) * w_row).astype(BF16),
                          jnp.broadcast_to(w_row, (dh, n)).astype(BF16)], axis=0)
    s_new = jnp.exp(btot + m_prev - m_new) * s_prev + _dot(vw, k)
    return h_t.T, s_new, m_new


def _mlstm_kernel(*refs, nc, zero_init, emit_state):
    it = iter(refs)
    fwd_refs = tuple(next(it) for _ in range(5))
    bwd_refs = tuple(next(it) for _ in range(5)) if nc > 1 else fwd_refs
    if not zero_init:
        s0_ref, m0_ref = next(it), next(it)
    h_refs = (next(it), next(it))
    if emit_state:
        c_out, n_out, m_out = next(it), next(it), next(it)
    s_scr, m_scr = next(it), next(it)

    j = pl.program_id(1)
    n = CHUNK
    dh = HEAD_DIM

    @pl.when(j == 0)
    def _():
        if zero_init:
            s_scr[...] = jnp.zeros_like(s_scr)
            m_scr[...] = jnp.zeros_like(m_scr)
        else:
            s_scr[...] = s0_ref[0]
            m_scr[...] = m0_ref[0]

    rows = lax.broadcasted_iota(jnp.int32, (n, n), 0)
    cols = lax.broadcasted_iota(jnp.int32, (n, n), 1)
    le = rows <= cols
    ge = rows >= cols
    tri_le = le.astype(BF16)
    tri_ge = ge.astype(BF16)
    use_state = not (zero_init and nc == 1)

    def gate_terms(d):
        g_ref, gt_ref = (fwd_refs, bwd_refs)[d][3:5]
        gate = g_ref[0]
        gate_t = gt_ref[0]
        lf = _log_sigmoid(gate)
        lf_t = _log_sigmoid(gate_t)
        tri_c, tri_r = (tri_ge, tri_le) if d == 0 else (tri_le, tri_ge)
        bc = _dot(tri_c, jnp.concatenate(_split3(lf), axis=1))
        b_cols = bc[:, 0:128] + bc[:, 128:256] + bc[:, 256:384]
        br = _dot(jnp.concatenate(_split3(lf_t), axis=0), tri_r)
        b_rows = br[0:16] + br[16:32] + br[32:48]
        return gate, gate_t, b_cols, b_rows, jnp.sum(lf_t, axis=-1, keepdims=True)

    terms = [gate_terms(0), gate_terms(1)]
    hs = ([], [])
    for hd in range(HEADS):
        hsl = slice(hd * dh, (hd + 1) * dh)
        st = None
        for d in range(N_DIR):
            k_ref, qt_ref, vt_ref = (fwd_refs, bwd_refs)[d][0:3]
            gate, gate_t, b_cols, b_rows, tot_rows = terms[d]
            ci = d * 8 + hd
            cf = d * 8 + 4 + hd
            unit = d * HEADS + hd
            k = k_ref[0, :, hsl]
            qt = qt_ref[0, hsl, :]
            if st is None or nc > 1:
                st = _dot(k, qt)
            mask = le if d == 0 else ge
            h, s_new, m_new = _scan_unit(
                st, k, qt, vt_ref[0, hsl, :],
                gate[:, ci:ci + 1] - b_cols[:, cf:cf + 1],
                gate_t[ci:ci + 1, :] - b_rows[cf:cf + 1, :],
                b_rows[cf:cf + 1, :], tot_rows[cf:cf + 1, :],
                mask, s_scr[unit], m_scr[unit][:, 0:1], use_state)
            s_scr[unit] = s_new
            m_scr[unit] = jnp.broadcast_to(m_new, (1, LANES))
            hs[d].append(h)
    for d in range(N_DIR):
        h_refs[d][0] = jnp.concatenate(hs[d], axis=1).astype(BF16)

    if emit_state:
        @pl.when(j == nc - 1)
        def _():
            for unit in range(N_DIR * HEADS):
                s = s_scr[unit]
                c_out[0, unit] = s[:dh].T
                n_out[0, unit] = s[dh:dh + 1]
                m_out[0, unit] = m_scr[unit]


def _mlstm(k, qt, vt, gate, gate_t, s0, m0, emit_state):
    b, t, _ = k.shape
    nc = t // CHUNK
    zero_init = s0 is None
    nu = N_DIR * HEADS
    fwd = lambda w: pl.BlockSpec((1, CHUNK, w), lambda i, j: (i, j, 0))
    bwd = lambda w: pl.BlockSpec((1, CHUNK, w), lambda i, j: (i, nc - 1 - j, 0))
    fwd_t = lambda r: pl.BlockSpec((1, r, CHUNK), lambda i, j: (i, 0, j))
    bwd_t = lambda r: pl.BlockSpec((1, r, CHUNK), lambda i, j: (i, 0, nc - 1 - j))
    args = [k, qt, vt, gate, gate_t]
    in_specs = [fwd(512), fwd_t(512), fwd_t(512), fwd(LANES), fwd_t(16)]
    if nc > 1:
        args += [k, qt, vt, gate, gate_t]
        in_specs += [bwd(512), bwd_t(512), bwd_t(512), bwd(LANES), bwd_t(16)]
    if not zero_init:
        args += [s0, m0]
        in_specs += [pl.BlockSpec((1, nu, 2 * HEAD_DIM, HEAD_DIM), lambda i, j: (i, 0, 0, 0)),
                     pl.BlockSpec((1, nu, 1, LANES), lambda i, j: (i, 0, 0, 0))]
    sd = jax.ShapeDtypeStruct
    out_shape = [sd((b, t, 512), BF16), sd((b, t, 512), BF16)]
    out_specs = [fwd(512), bwd(512)]
    if emit_state:
        out_shape += [sd((b, nu, HEAD_DIM, HEAD_DIM), F32), sd((b, nu, 1, HEAD_DIM), F32),
                      sd((b, nu, 1, LANES), F32)]
        out_specs += [pl.BlockSpec((1, nu, HEAD_DIM, HEAD_DIM), lambda i, j: (i, 0, 0, 0)),
                      pl.BlockSpec((1, nu, 1, HEAD_DIM), lambda i, j: (i, 0, 0, 0)),
                      pl.BlockSpec((1, nu, 1, LANES), lambda i, j: (i, 0, 0, 0))]
    return pl.pallas_call(
        functools.partial(_mlstm_kernel, nc=nc, zero_init=zero_init, emit_state=emit_state),
        grid=(b, nc),
        in_specs=in_specs,
        out_specs=out_specs,
        out_shape=out_shape,
        scratch_shapes=[pltpu.VMEM((nu, 2 * HEAD_DIM, HEAD_DIM), F32),
                        pltpu.VMEM((nu, 1, LANES), F32)],
        compiler_params=_params("arbitrary", "arbitrary"),
        name="mlstm",
    )(*args)


def _outproj_kernel(x_ref, u_ref, hf_ref, hb_ref, o_ref, mod_ref, pm_ref, pinv_ref, wp_ref, ps_ref,
                    wo_ref, gpm_ref, gpf_ref, wr1_ref, wr2_ref, wr3_ref, x1_ref, h2_ref, xp_ref, lg_ref, slab_ref, *, tm):
    i = pl.program_id(1)
    x = x_ref[0]
    mod = mod_ref[0]
    row0 = pl.multiple_of(i * tm, tm)
    u_full = u_ref[0]
    u_tile = u_ref[0, pl.ds(row0, tm), :].astype(F32)
    diffs = []
    for g in range(POOL_GROUPS):
        sl = slice(g * POOL_GROUP_DIM, (g + 1) * POOL_GROUP_DIM)
        box = _dot(pm_ref[g], u_full[:, sl])
        diffs.append((box * pinv_ref[g] - u_tile[:, sl]).astype(BF16))
    yps = [_dot(jnp.concatenate(diffs[2 * p:2 * p + 2], axis=1), wp_ref[p]) for p in range(POOL_GROUPS // 2)]
    y_pool = jnp.concatenate(yps, axis=1) * ps_ref[...]
    hsum = hf_ref[0].astype(F32) + hb_ref[0].astype(F32)
    y_ml = jax.nn.sigmoid(o_ref[0].astype(F32)) * hsum
    mix = _dot(jnp.concatenate([y_pool, y_ml], axis=1).astype(BF16), wo_ref[...])
    x1 = x + mod[2:3] * _rmsnorm(mix, gpm_ref[...])
    x1_ref[0] = x1
    h2 = _rmsnorm(x1, gpf_ref[...]) * (1.0 + mod[4:5]) + mod[3:4]
    p1, p2, p3 = _split3(h2)
    h2_ref[0] = p1
    for cc in range(D_MODEL // LANES):
        slab_ref[:, cc, :] = h2[:, cc * LANES:(cc + 1) * LANES]
    xp_ref[0] = slab_ref[...].astype(BF16)
    ps3 = jnp.concatenate([p1, p2, p3], axis=0)
    r1 = _dot(ps3, wr1_ref[...])
    r2 = _dot(ps3[:2 * tm], wr2_ref[...])
    r3 = _dot(p1, wr3_ref[...])
    lg = r1[0:tm] + r1[tm:2 * tm] + r1[2 * tm:] + r2[0:tm] + r2[tm:] + r3
    lg_ref[0] = lg.T[0:N_EXPERTS]


def _outproj(x, u, hf, hb, o, mod, mod_row, pm, pinv, wp, ps, wo, gpm, gpf, wr1, wr2, wr3):
    b, t, _ = x.shape
    tm = min(t, 2 * TOKEN_TILE)
    const = lambda *shape: pl.BlockSpec(shape, lambda i, j: (0,) * len(shape))
    tok = lambda w: pl.BlockSpec((1, tm, w), lambda i, j: (i, j, 0))
    sd = jax.ShapeDtypeStruct
    return pl.pallas_call(
        functools.partial(_outproj_kernel, tm=tm),
        grid=(b, t // tm),
        in_specs=[tok(D_MODEL),
                  pl.BlockSpec((1, t, 512), lambda i, j: (i, 0, 0)),
                  tok(512), tok(512), tok(512),
                  pl.BlockSpec((1, N_MOD, D_MODEL), lambda i, j: (mod_row(i), 0, 0)),
                  pl.BlockSpec((POOL_GROUPS, tm, t), lambda i, j: (0, j, 0)),
                  pl.BlockSpec((POOL_GROUPS, tm, 1), lambda i, j: (0, j, 0)),
                  const(POOL_GROUPS // 2, 2 * POOL_GROUP_DIM, 2 * POOL_GROUP_DIM), const(1, POOL_WIDTH),
                  const(D_MODEL, D_MODEL), const(1, D_MODEL), const(1, D_MODEL),
                  const(D_MODEL, LANES), const(D_MODEL, LANES), const(D_MODEL, LANES)],
        out_specs=[tok(D_MODEL), tok(D_MODEL),
                   pl.BlockSpec((1, tm, D_MODEL // LANES, LANES), lambda i, j: (i, j, 0, 0)),
                   pl.BlockSpec((1, N_EXPERTS, tm), lambda i, j: (i, 0, j))],
        out_shape=[sd((b, t, D_MODEL), F32), sd((b, t, D_MODEL), BF16),
                   sd((b, t, D_MODEL // LANES, LANES), BF16), sd((b, N_EXPERTS, t), F32)],
        scratch_shapes=[pltpu.VMEM((tm, D_MODEL // LANES, LANES), F32)],
        compiler_params=_params("arbitrary", "arbitrary"),
        name="outproj",
    )(x, u, hf, hb, o, mod, pm, pinv, wp, ps, wo, gpm, gpf, wr1, wr2, wr3)


def _router_kernel(lg_ref, br_ref, comb_ref, sel_ref):
    s = jax.nn.sigmoid(lg_ref[...])
    biased = s + br_ref[...]
    gidx = lax.broadcasted_iota(jnp.int32, s.shape, 0)
    jidx = lax.broadcasted_iota(jnp.int32, s.shape, 1)
    neg = -jnp.inf
    m1 = jnp.max(biased, axis=1, keepdims=True)
    i1 = jnp.min(jnp.where(biased == m1, jidx, GROUP_SIZE), axis=1, keepdims=True)
    m2 = jnp.max(jnp.where(jidx == i1, neg, biased), axis=1, keepdims=True)
    gscore = m1 + m2
    gi = lax.broadcasted_iota(jnp.int32, gscore.shape, 0)
    gmask = jnp.zeros(gscore.shape, F32)
    cur = gscore
    for _ in range(TOPK_GROUPS):
        mx = jnp.max(cur, axis=0, keepdims=True)
        ix = jnp.min(jnp.where(cur == mx, gi, N_EXPERT_GROUPS), axis=0, keepdims=True)
        hit = gi == ix
        gmask = jnp.where(hit, 1.0, gmask)
        cur = jnp.where(hit, neg, cur)
    cur = jnp.where(gmask > 0, biased, neg)
    eidx = gidx * GROUP_SIZE + jidx
    selmask = jnp.zeros(s.shape, F32)
    for _ in range(TOP_K):
        mx = jnp.max(jnp.max(cur, axis=1, keepdims=True), axis=0, keepdims=True)
        ix = jnp.where(cur == mx, eidx, N_EXPERTS)
        ix = jnp.min(jnp.min(ix, axis=1, keepdims=True), axis=0, keepdims=True)
        hit = eidx == ix
        selmask = jnp.where(hit, 1.0, selmask)
        cur = jnp.where(hit, neg, cur)
    sel = selmask * s
    tot = jnp.sum(jnp.sum(sel, axis=1, keepdims=True), axis=0, keepdims=True)
    comb_ref[...] = sel / tot * ROUTED_SCALE
    sel_ref[...] = selmask


def _router(logits_t, b_router):
    t = logits_t.shape[1]
    tl = 1024
    shp = (N_EXPERT_GROUPS, GROUP_SIZE, t)
    blk = pl.BlockSpec((N_EXPERT_GROUPS, GROUP_SIZE, tl), lambda j: (0, 0, j))
    comb, sel = pl.pallas_call(
        _router_kernel,
        grid=(t // tl,),
        in_specs=[blk, pl.BlockSpec((N_EXPERT_GROUPS, GROUP_SIZE, 1), lambda j: (0, 0, 0))],
        out_specs=[blk, blk],
        out_shape=[jax.ShapeDtypeStruct(shp, F32), jax.ShapeDtypeStruct(shp, F32)],
        compiler_params=_params("arbitrary"),
        name="router",
    )(logits_t.reshape(shp), b_router.reshape(N_EXPERT_GROUPS, GROUP_SIZE, 1))
    return comb.reshape(N_EXPERTS, t), sel.reshape(N_EXPERTS, t)


def _plan_kernel(sel_ref, pos_ref, meta_ref, emeta_ref, *, n_meta):
    t = sel_ref.shape[1]
    tm = float(MOE_TILE)
    sel = sel_ref[...]
    selb = sel.astype(BF16)
    blk = 256
    rr = lax.broadcasted_iota(jnp.int32, (blk, blk), 0)
    cc = lax.broadcasted_iota(jnp.int32, (blk, blk), 1)
    before = (rr < cc).astype(BF16)
    carry = jnp.zeros((N_EXPERTS, 1), F32)
    ranks = []
    for b in range(t // blk):
        sb = selb[:, b * blk:(b + 1) * blk]
        ranks.append(_dot(sb, before) + carry)
        carry = carry + jnp.sum(sel[:, b * blk:(b + 1) * blk], axis=1, keepdims=True)
    rank = jnp.concatenate(ranks, axis=1)
    cnt = carry
    ntile = jnp.floor((cnt + (tm - 1.0)) * (1.0 / tm))
    er = lax.broadcasted_iota(jnp.int32, (N_EXPERTS, N_EXPERTS), 0)
    ec = lax.broadcasted_iota(jnp.int32, (N_EXPERTS, N_EXPERTS), 1)
    below = (ec < er).astype(BF16)
    tstart = _dot(below, jnp.broadcast_to(ntile, (N_EXPERTS, LANES)).astype(BF16))[:, 0:1]
    pos = tstart * tm + rank
    erank = _dot(below, selb)
    rows = []
    for k in range(TOP_K):
        hit = (sel > 0.0) & (erank == float(k))
        rows.append(jnp.sum(jnp.where(hit, pos, 0.0), axis=0, keepdims=True))
    rows += [jnp.zeros((1, t), F32)] * (8 - TOP_K)
    pos_ref[...] = jnp.concatenate(rows, axis=0).astype(jnp.int32)

    tau = lax.broadcasted_iota(jnp.int32, (N_EXPERTS, n_meta), 1).astype(F32)
    eidx = lax.broadcasted_iota(jnp.int32, (N_EXPERTS, n_meta), 0).astype(F32)
    te = jnp.sum(((tstart + ntile) <= tau).astype(F32), axis=0, keepdims=True)
    te = jnp.minimum(te, float(N_EXPERTS - 1))
    onehot = eidx == te
    cnt_t = jnp.sum(jnp.where(onehot, cnt, 0.0), axis=0, keepdims=True)
    ts_t = jnp.sum(jnp.where(onehot, tstart, 0.0), axis=0, keepdims=True)
    tr = jnp.clip(cnt_t - (tau[0:1] - ts_t) * tm, 0.0, tm)
    tf = jnp.where((tau[0:1] == ts_t) & (tr > 0.0), 1.0, 0.0)
    meta_ref[...] = jnp.concatenate([te, tr, tf] + [jnp.zeros((1, n_meta), F32)] * 5, axis=0).astype(jnp.int32)

    eye = (lax.broadcasted_iota(jnp.int32, (N_EXPERTS, LANES), 0)
           == lax.broadcasted_iota(jnp.int32, (N_EXPERTS, LANES), 1))
    as_row = lambda col: jnp.sum(jnp.where(eye, col, 0.0), axis=0, keepdims=True)
    emeta_ref[...] = jnp.concatenate([as_row(cnt), as_row(tstart), as_row(ntile)]
                                     + [jnp.zeros((1, LANES), F32)] * 5, axis=0).astype(jnp.int32)


def _plan(sel, n_meta):
    t = sel.shape[1]
    sd = jax.ShapeDtypeStruct
    return pl.pallas_call(
        functools.partial(_plan_kernel, n_meta=n_meta),
        out_shape=[sd((8, t), jnp.int32), sd((8, n_meta), jnp.int32), sd((8, LANES), jnp.int32)],
        compiler_params=pltpu.CompilerParams(vmem_limit_bytes=VMEM_LIMIT),
        name="plan",
    )(sel)


def _fill_slot_table(pos_ref, emeta_ref, tbl_ref, n_tokens):
    group = 32
    def pad_expert(e, carry):
        n_t = emeta_ref[2 * LANES + e]
        start = (emeta_ref[LANES + e] + n_t - 1) * MOE_TILE

        @pl.when(n_t > 0)
        def _():
            def put(i, c):
                for u in range(group):
                    tbl_ref[start + i * group + u] = n_tokens
                return c
            lax.fori_loop(0, MOE_TILE // group, put, 0)
        return carry
    lax.fori_loop(0, N_EXPERTS, pad_expert, 0)
    last = N_EXPERTS - 1
    def pad_tile(tile, carry):
        def put(i, c):
            for u in range(group):
                tbl_ref[tile * MOE_TILE + i * group + u] = n_tokens
            return c
        lax.fori_loop(0, MOE_TILE // group, put, 0)
        return carry
    lax.fori_loop(emeta_ref[LANES + last] + emeta_ref[2 * LANES + last], tbl_ref.shape[0] // MOE_TILE, pad_tile, 0)

    unroll = 8
    def scatter(i, carry):
        first = i * (8 * unroll)
        for u in range(unroll):
            for k in range(TOP_K):
                tbl_ref[pos_ref[first + (8 * u + k)]] = i * unroll + u
        return carry
    lax.fori_loop(0, n_tokens // unroll, scatter, 0)


def _experts_kernel(te_ref, tr_ref, pos_ref, emeta_ref, xc_ref, xs_ref, comb_ref, wgu_ref, wd_ref, acc_out,
                    tbl_ref, xbuf, acc, stage0, stage1, cst0, cst1, act0, act1, ybuf0, ybuf1,
                    wgu_b0, wgu_b1, wd_b0, wd_b1, sems):
    s = pl.program_id(0)
    n_tiles = te_ref.shape[0]
    tm = MOE_TILE
    nch = D_MODEL // LANES
    tile_at = lambda lag: jnp.clip(s - lag, 0, n_tiles - 1)
    t_g, t_1, t_2, t_3 = tile_at(0), tile_at(1), tile_at(2), tile_at(3)

    @pl.when(s == 0)
    def _():
        tc, ts = xc_ref.shape[0], xs_ref.shape[0]
        copies = (pltpu.make_async_copy(xc_ref, xbuf.at[pl.ds(0, tc)], sems.at[0]),
                  pltpu.make_async_copy(xs_ref, xbuf.at[pl.ds(tc, ts)], sems.at[1]))
        for cp in copies:
            cp.start()
        n_pad = xbuf.shape[0] - tc - ts
        xbuf[pl.ds(tc + ts, n_pad)] = jnp.zeros((n_pad,) + xbuf.shape[1:], BF16)
        for ref in (acc, stage0, stage1, cst0, cst1, act0, act1, ybuf0, ybuf1, wgu_b0, wgu_b1, wd_b0, wd_b1):
            ref[...] = jnp.zeros_like(ref)
        _fill_slot_table(pos_ref, emeta_ref, tbl_ref, tc + ts)
        for cp in copies:
            cp.wait()

    def gather(tile, stage, cst):
        base = tile * tm
        for j in range(tm):
            tok = tbl_ref[base + j]
            stage[pl.ds(j * nch, nch), :] = xbuf[tok].astype(F32)
            cst[pl.ds(j, 1), :] = comb_ref[pl.ds(tok, 1), :]

    def gate_up(tile, stage, cst, wgu_b, act):
        xb = jnp.concatenate([stage[pl.ds(cc, tm, stride=nch), :] for cc in range(nch)], axis=1).astype(BF16)
        gu = _dot(xb, wgu_b[...])
        lane = lax.broadcasted_iota(jnp.int32, (1, LANES), 1)
        w_col = jnp.sum(jnp.where(lane == te_ref[tile], cst[...], 0.0), axis=1, keepdims=True)
        act[...] = (_silu(gu[:, :EXPERT_DIM]) * gu[:, EXPERT_DIM:] * w_col).astype(BF16)

    def down(act, wd_b, ybuf):
        y = _dot(act[...], wd_b[...])
        for cc in range(nch):
            ybuf[cc * Y_PITCH:cc * Y_PITCH + tm, :] = y[:, cc * LANES:(cc + 1) * LANES]

    def scatter(tile, ybuf):
        base = tile * tm
        sc_n = 16
        for i in range(tm // sc_n):
            toks = [tbl_ref[base + i * sc_n + u] for u in range(sc_n)]
            olds = [acc[toks[u]] for u in range(sc_n)]
            news = [olds[u] + ybuf[pl.ds(i * sc_n + u, nch, stride=Y_PITCH), :] for u in range(sc_n)]
            for u in range(sc_n):
                acc[toks[u]] = news[u]

    busy = (tr_ref[t_g] + tr_ref[t_1] + tr_ref[t_2] + tr_ref[t_3]) > 0
    bufs = ((stage0, cst0, act0, ybuf0, wgu_b0, wd_b0), (stage1, cst1, act1, ybuf1, wgu_b1, wd_b1))
    for par in range(2):
        stage_p, cst_p, act_p, ybuf_p, wgu_p, wd_p = bufs[par]
        stage_q, cst_q, act_q, ybuf_q, wgu_q, wd_q = bufs[1 - par]

        @pl.when(busy & (s % 2 == par))
        def _():
            gather(t_g, stage_p, cst_p)
            wgu_p[...] = wgu_ref[0].astype(BF16)
            gate_up(t_1, stage_q, cst_q, wgu_q, act_q)
            wd_q[...] = wd_ref[0].astype(BF16)
            down(act_p, wd_p, ybuf_p)
            scatter(t_3, ybuf_q)

    @pl.when(s == pl.num_programs(0) - 1)
    def _():
        cp = pltpu.make_async_copy(acc, acc_out, sems.at[2])
        cp.start()
        cp.wait()


def _experts(te, tr, pos, emeta, xc, xs, comb, wgu, wd):
    n_tok = comb.shape[0]
    n_tiles = te.shape[0]
    tm = MOE_TILE
    nch = D_MODEL // LANES
    vm = pltpu.VMEM
    grid_spec = pltpu.PrefetchScalarGridSpec(
        num_scalar_prefetch=4,
        grid=(n_tiles + 3,),
        in_specs=[pl.BlockSpec(memory_space=pl.ANY), pl.BlockSpec(memory_space=pl.ANY),
                  pl.BlockSpec((n_tok, LANES), lambda s, *_: (0, 0), pipeline_mode=pl.Buffered(1)),
                  pl.BlockSpec((1, D_MODEL, 2 * EXPERT_DIM), lambda s, te, *_: (te[jnp.minimum(s, n_tiles - 1)], 0, 0)),
                  pl.BlockSpec((1, EXPERT_DIM, D_MODEL), lambda s, te, *_: (te[jnp.clip(s - 1, 0, n_tiles - 1)], 0, 0))],
        out_specs=pl.BlockSpec(memory_space=pl.ANY),
        scratch_shapes=[pltpu.SMEM((n_tiles * tm,), jnp.int32),
                        vm((n_tok, nch, LANES), BF16), vm((n_tok, nch, LANES), F32),
                        vm((tm * nch, LANES), F32), vm((tm * nch, LANES), F32),
                        vm((tm, LANES), F32), vm((tm, LANES), F32),
                        vm((tm, EXPERT_DIM), BF16), vm((tm, EXPERT_DIM), BF16),
                        vm((nch * Y_PITCH, LANES), F32), vm((nch * Y_PITCH, LANES), F32),
                        vm((D_MODEL, 2 * EXPERT_DIM), BF16), vm((D_MODEL, 2 * EXPERT_DIM), BF16),
                        vm((EXPERT_DIM, D_MODEL), BF16), vm((EXPERT_DIM, D_MODEL), BF16),
                        pltpu.SemaphoreType.DMA((3,))],
    )
    return pl.pallas_call(
        _experts_kernel,
        grid_spec=grid_spec,
        out_shape=jax.ShapeDtypeStruct((n_tok, nch, LANES), F32),
        compiler_params=pltpu.CompilerParams(dimension_semantics=("arbitrary",),
                                             vmem_limit_bytes=EXPERTS_VMEM_LIMIT),
        name="experts",
    )(te, tr, pos, emeta, xc, xs, comb, wgu, wd)


def _final_kernel(acc_ref, h_ref, x1_ref, mod_ref, wsg_ref, wsd_ref, gpo_ref, out_ref):
    routed = jnp.concatenate([acc_ref[:, cc, :] for cc in range(D_MODEL // LANES)], axis=1)
    gs = _dot(h_ref[...], wsg_ref[...])
    act = _silu(gs[:, :SHARED_DIM]) * gs[:, SHARED_DIM:]
    f = routed + _dot(act.astype(BF16), wsd_ref[...])
    out_ref[...] = x1_ref[...] + mod_ref[0][5:6] * _rmsnorm(f, gpo_ref[...])


def _final(acc, tile0, h2, x1, mod, mod_row, wsg, wsd, gpo):
    n = h2.shape[0]
    tm = FINAL_TILE
    const = lambda *shape: pl.BlockSpec(shape, lambda i: (0,) * len(shape))
    tok = lambda w: pl.BlockSpec((tm, w), lambda i: (i, 0))
    return pl.pallas_call(
        _final_kernel,
        grid=(n // tm,),
        in_specs=[pl.BlockSpec((tm, D_MODEL // LANES, LANES), lambda i: (i + tile0, 0, 0)),
                  tok(D_MODEL), tok(D_MODEL),
                  pl.BlockSpec((1, N_MOD, D_MODEL), lambda i: (mod_row(i), 0, 0)),
                  const(D_MODEL, 2 * SHARED_DIM), const(SHARED_DIM, D_MODEL), const(1, D_MODEL)],
        out_specs=tok(D_MODEL),
        out_shape=jax.ShapeDtypeStruct((n, D_MODEL), F32),
        compiler_params=_params("arbitrary"),
        name="final",
    )(acc, h2, x1, mod, wsg, wsd, gpo)


def _window_bounds(n, w):
    idx = np.arange(n)
    return np.clip(idx - w // 2, 0, n), np.clip(idx + w - w // 2, 0, n)


def _pool_operators(t, grid):
    mats, invs = [], []
    for w in POOL_WINDOWS:
        if grid:
            rlo, rhi = _window_bounds(t // GRID_W, w)
            clo, chi = _window_bounds(GRID_W, w)
            r = np.arange(t) // GRID_W
            c = np.arange(t) % GRID_W
            m = ((r[None, :] >= rlo[r][:, None]) & (r[None, :] < rhi[r][:, None])
                 & (c[None, :] >= clo[c][:, None]) & (c[None, :] < chi[c][:, None]))
            cnt = (rhi - rlo)[r] * (chi - clo)[c]
        else:
            lo, hi = _window_bounds(t, w)
            sidx = np.arange(t)
            m = (sidx[None, :] >= lo[:, None]) & (sidx[None, :] < hi[:, None])
            cnt = hi - lo
        mats.append(m.astype(np.float32))
        invs.append((1.0 / cnt.astype(np.float64)).astype(np.float32)[:, None])
    return jnp.asarray(np.stack(mats), BF16), jnp.asarray(np.stack(invs), F32)


def kernel(x_prompt, x_sample, state_C, state_n, state_m, c, c_ctx, w_ada, b_ada, g_pre_mix, w_in, b_gate,
           w_pool, pool_scale, w_out, g_post_mix, g_pre_ffn, w_router, b_router, w_expert_gu, w_expert_down,
           w_shared_gu, w_shared_down, g_post_ffn):
    b_ctx = x_prompt.shape[0]
    b_lat = x_sample.shape[0]
    nu = N_DIR * HEADS
    l = 0
    row = lambda a: a[l].reshape(1, -1).astype(F32)

    cvec = jnp.zeros((16, D_MODEL), F32).at[0].set(c_ctx.astype(F32)).at[1:1 + b_lat].set(c.astype(F32))
    mod = _mod_rows(cvec, w_ada[l], b_ada[l]).reshape(16, N_MOD, D_MODEL)

    w_in_l = w_in[l]
    p0 = POOL_WIDTH
    mw = MLSTM_WIDTH
    w_u, w_q, w_k, w_v, w_o = (w_in_l[:, lo:lo + 512] for lo in (0, p0, p0 + mw, p0 + 2 * mw, p0 + 3 * mw))
    wm = jnp.concatenate([w_u, w_k, w_o], axis=1).astype(BF16)
    wt = jnp.concatenate([w_q.T, w_v.T], axis=0).astype(BF16)
    wg_cols = w_in_l[:, p0 + 4 * mw:]
    wg1, wg2 = _split2(jnp.pad(wg_cols, ((0, 0), (0, LANES - GATE_COLS))))
    bg = b_gate[l].reshape(GATE_COLS).astype(F32)
    bgr = jnp.pad(bg, (0, LANES - GATE_COLS)).reshape(1, LANES)
    wpl = w_pool[l].astype(BF16)
    zg = jnp.zeros((POOL_GROUP_DIM, POOL_GROUP_DIM), BF16)
    wp = jnp.stack([jnp.block([[wpl[2 * p], zg], [zg, wpl[2 * p + 1]]]) for p in range(POOL_GROUPS // 2)])
    wo = w_out[l].astype(BF16)
    wr1, wr2, wr3 = _split3(jnp.pad(w_router[l].astype(F32), ((0, 0), (0, LANES - N_EXPERTS))))
    wsg = w_shared_gu[l].astype(BF16)
    wsd = w_shared_down[l].astype(BF16)

    def mixer(x, mod_row, grid, s0, m0, emit_state):
        t = x.shape[1]
        u, k, o, qt, vt, gate, gate_t = _inproj(x.astype(F32), mod, mod_row, row(g_pre_mix), wm, wt, wg1, wg2,
                                                bgr)
        outs = _mlstm(k, qt, vt, gate, gate_t, s0, m0, emit_state)
        hf, hb = outs[0], outs[1]
        pm, pinv = _pool_operators(t, grid)
        x1, h2, xp, lg = _outproj(x.astype(F32), u, hf, hb, o, mod, mod_row, pm, pinv, wp, row(pool_scale), wo,
                                  row(g_post_mix), row(g_pre_ffn), wr1, wr2, wr3)
        return x1, h2, xp, lg, outs[2:]

    ctx_row = lambda i: 0
    lat_row = lambda i: i + 1
    x1c, h2c, xpc, lgc, (c_new, n_new, m_new) = mixer(x_prompt, ctx_row, False, None, None, True)
    s0 = jnp.concatenate(
        [jnp.swapaxes(state_C[:, l].reshape(b_lat, nu, HEAD_DIM, HEAD_DIM).astype(F32), -1, -2),
         jnp.broadcast_to(state_n[:, l].reshape(b_lat, nu, 1, HEAD_DIM).astype(F32),
                          (b_lat, nu, HEAD_DIM, HEAD_DIM))], axis=-2)
    m0 = jnp.broadcast_to(state_m[:, l].reshape(b_lat, nu, 1, 1).astype(F32), (b_lat, nu, 1, LANES))
    x1s, h2s, xps, lgs, _ = mixer(x_sample, lat_row, True, s0, m0, False)

    tc = b_ctx * x_prompt.shape[1]
    ts = b_lat * x_sample.shape[1]
    n_tok = tc + ts
    lg_all = jnp.concatenate([lgc.transpose(1, 0, 2).reshape(N_EXPERTS, tc),
                              lgs.transpose(1, 0, 2).reshape(N_EXPERTS, ts)], axis=1)
    comb, sel = _router(lg_all, b_router[l].astype(F32))
    n_tiles = n_tok * TOP_K // MOE_TILE + N_EXPERTS
    n_meta = -(-n_tiles // LANES) * LANES
    pos, meta, emeta = _plan(sel, n_meta)

    comb_tok = jnp.pad(comb.T, ((0, 8), (0, LANES - N_EXPERTS)))
    slab = (D_MODEL // LANES, LANES)
    acc = _experts(meta[0, :n_tiles], meta[1, :n_tiles],
                   pos.T.reshape(-1), emeta.reshape(-1),
                   xpc.reshape((tc,) + slab), xps.reshape((ts,) + slab), comb_tok,
                   w_expert_gu[l], w_expert_down[l])

    fin = functools.partial(_final, wsg=wsg, wsd=wsd, gpo=row(g_post_ffn))
    tiles_per_lat = x_sample.shape[1] // FINAL_TILE
    yc = fin(acc, 0, h2c.reshape(tc, D_MODEL), x1c.reshape(tc, D_MODEL), mod, ctx_row)
    ys = fin(acc, tc // FINAL_TILE, h2s.reshape(ts, D_MODEL), x1s.reshape(ts, D_MODEL), mod,
             lambda i: i // tiles_per_lat + 1)

    new_c = c_new.reshape(b_ctx, 1, N_DIR, HEADS, HEAD_DIM, HEAD_DIM)
    new_n = n_new.reshape(b_ctx, 1, N_DIR, HEADS, HEAD_DIM)
    new_m = m_new[..., 0].reshape(b_ctx, 1, N_DIR, HEADS)
    return (yc.reshape(x_prompt.shape), ys.reshape(x_sample.shape), new_c, new_n, new_m)
```

```python
import functools

import jax
import jax.numpy as jnp
import numpy as np
from jax import lax
from jax.experimental import pallas as pl
from jax.experimental.pallas import tpu as pltpu

F32 = jnp.float32
BF16 = jnp.bfloat16

D_MODEL = 1024
GRID_W = 64
POOL_WIDTH = 512
POOL_GROUPS = 4
POOL_GROUP_DIM = 128
POOL_WINDOWS = (2, 4, 8, 16)
HEADS = 4
HEAD_DIM = 128
MLSTM_WIDTH = HEADS * HEAD_DIM
N_DIR = 2
GATE_COLS = N_DIR * 2 * HEADS
N_EXPERTS = 64
TOP_K = 6
N_EXPERT_GROUPS = 8
GROUP_SIZE = N_EXPERTS // N_EXPERT_GROUPS
TOPK_GROUPS = 4
EXPERT_DIM = 256
SHARED_DIM = 256
ROUTED_SCALE = 2.5
N_MOD = 6
EPS = 1e-6
K_SCALE = HEAD_DIM ** -0.5

LANES = 128
CHUNK = 256
TOKEN_TILE = 256
FINAL_TILE = 512
MOE_TILE = 256
Y_PITCH = MOE_TILE + 8
VMEM_LIMIT = 56 * 1024 * 1024
EXPERTS_VMEM_LIMIT = 58 * 1024 * 1024


def _split3(x):
    p1 = x.astype(BF16)
    r1 = x - p1.astype(F32)
    p2 = r1.astype(BF16)
    p3 = (r1 - p2.astype(F32)).astype(BF16)
    return p1, p2, p3


def _split2(x):
    p1 = x.astype(BF16)
    p2 = (x - p1.astype(F32)).astype(BF16)
    return p1, p2


def _dot(a, b):
    return jnp.dot(a, b, preferred_element_type=F32)


def _dot_nt(a, b):
    return lax.dot_general(a, b, (((1,), (1,)), ((), ())), preferred_element_type=F32)


def _rmsnorm(x, g):
    return x * lax.rsqrt(jnp.mean(x * x, axis=-1, keepdims=True) + EPS) * g


def _silu(x):
    return x * jax.nn.sigmoid(x)


def _params(*sem):
    return pltpu.CompilerParams(dimension_semantics=sem, vmem_limit_bytes=VMEM_LIMIT)


def _mod_kernel(c_ref, w_ref, b_ref, o_ref):
    a = _silu(c_ref[...])
    a_stack = jnp.concatenate(_split3(a), axis=0)
    w1, w2 = _split2(w_ref[...])
    r1 = _dot(a_stack, w1)
    r2 = _dot(a_stack[:32], w2)
    o_ref[...] = (r1[0:16] + r1[16:32] + r1[32:48] + r2[0:16] + r2[16:32]) + b_ref[...]


def _mod_rows(cvec, w_ada, b_ada):
    n = N_MOD * D_MODEL
    tn = 1536
    return pl.pallas_call(
        _mod_kernel,
        grid=(n // tn,),
        in_specs=[pl.BlockSpec((16, D_MODEL), lambda j: (0, 0)),
                  pl.BlockSpec((D_MODEL, tn), lambda j: (0, j)),
                  pl.BlockSpec((1, tn), lambda j: (0, j))],
        out_specs=pl.BlockSpec((16, tn), lambda j: (0, j)),
        out_shape=jax.ShapeDtypeStruct((16, n), F32),
        compiler_params=_params("arbitrary"),
        name="mod",
    )(cvec, w_ada, b_ada.reshape(1, n))


def _inproj_kernel(x_ref, mod_ref, g_ref, wm_ref, wt_ref, wg1_ref, wg2_ref, bgr_ref,
                   u_ref, k_ref, o_ref, qt_ref, vt_ref, gate_ref, gatet_ref):
    x = x_ref[0]
    mod = mod_ref[0]
    h = _rmsnorm(x, g_ref[...]) * (1.0 + mod[1:2]) + mod[0:1]
    h1, h2, h3 = _split3(h)
    z = _dot(h1, wm_ref[...])
    u_ref[0] = z[:, 0:512].astype(BF16)
    k_ref[0] = (z[:, 512:1024] * K_SCALE).astype(BF16)
    o_ref[0] = z[:, 1024:1536].astype(BF16)
    zt = _dot_nt(wt_ref[...], h1)
    qt_ref[0] = zt[0:512].astype(BF16)
    vt_ref[0] = zt[512:1024].astype(BF16)
    tm = x.shape[0]
    hs = jnp.concatenate([h1, h2, h3], axis=0)
    r1 = _dot(hs, wg1_ref[...])
    r2 = _dot(hs[:2 * tm], wg2_ref[...])
    gate = (r1[0:tm] + r1[tm:2 * tm] + r1[2 * tm:] + r2[0:tm] + r2[tm:]) + bgr_ref[...]
    gate_ref[0] = gate
    gatet_ref[0] = gate.T[0:16]


def _inproj(x, mod, mod_row, g, wm, wt, wg1, wg2, bgr):
    b, t, _ = x.shape
    tm = min(t, 2 * TOKEN_TILE)
    const = lambda *shape: pl.BlockSpec(shape, lambda i, j: (0,) * len(shape))
    tok = lambda w: pl.BlockSpec((1, tm, w), lambda i, j: (i, j, 0))
    tok_t = lambda r: pl.BlockSpec((1, r, tm), lambda i, j: (i, 0, j))
    sd = jax.ShapeDtypeStruct
    return pl.pallas_call(
        _inproj_kernel,
        grid=(b, t // tm),
        in_specs=[tok(D_MODEL),
                  pl.BlockSpec((1, N_MOD, D_MODEL), lambda i, j: (mod_row(i), 0, 0)),
                  const(1, D_MODEL), const(D_MODEL, 1536), const(1024, D_MODEL),
                  const(D_MODEL, LANES), const(D_MODEL, LANES), const(1, LANES)],
        out_specs=[tok(512), tok(512), tok(512), tok_t(512), tok_t(512), tok(LANES), tok_t(16)],
        out_shape=[sd((b, t, 512), BF16), sd((b, t, 512), BF16), sd((b, t, 512), BF16),
                   sd((b, 512, t), BF16), sd((b, 512, t), BF16), sd((b, t, LANES), F32),
                   sd((b, 16, t), F32)],
        compiler_params=_params("arbitrary", "arbitrary"),
        name="inproj",
    )(x, mod, g, wm, wt, wg1, wg2, bgr)


def _log_sigmoid(x):
    return jnp.minimum(x, 0.0) - jnp.log1p(jnp.exp(-jnp.abs(x)))


def _scan_unit(st, k, qt, vt, u_col, u_row, b_row, btot, mask, s_prev, m_prev, use_state):
    dh = HEAD_DIM
    n = st.shape[0]
    ub = jnp.where(mask, jnp.broadcast_to(u_col, (n, n)), -jnp.inf)
    z = jnp.maximum(m_prev, jnp.max(ub, axis=0, keepdims=True))
    p = (jnp.exp(ub - z) * st).astype(BF16)
    ones = jnp.ones((dh, n), BF16)
    tot = _dot(jnp.concatenate([vt, ones], axis=0), p)
    if use_state:
        tot = tot + jnp.exp(m_prev - z) * _dot(s_prev.astype(BF16), qt)
    floor = jnp.exp(-(b_row + z))
    h_t = tot[:dh] / jnp.maximum(jnp.abs(tot[dh:]), floor)
    g_row = btot + u_row
    m_new = jnp.maximum(btot + m_prev, jnp.max(g_row, axis=-1, keepdims=True))
    w_row = jnp.exp(g_row - m_new)
    vw = jnp.concatenate([(vt.astype(F32) * w_row).astype(BF16),
                          jnp.broadcast_to(w_row, (dh, n)).astype(BF16)], axis=0)
    s_new = jnp.exp(btot + m_prev - m_new) * s_prev + _dot(vw, k)
    return h_t.T, s_new, m_new


def _mlstm_kernel(*refs, nc, zero_init, emit_state):
    it = iter(refs)
    fwd_refs = tuple(next(it) for _ in range(5))
    bwd_refs = tuple(next(it) for _ in range(5)) if nc > 1 else fwd_refs
    if not zero_init:
        s0_ref, m0_ref = next(it), next(it)
    h_refs = (next(it), next(it))
    if emit_state:
        c_out, n_out, m_out = next(it), next(it), next(it)
    s_scr, m_scr = next(it), next(it)

    j = pl.program_id(1)
    n = CHUNK
    dh = HEAD_DIM

    @pl.when(j == 0)
    def _():
        if zero_init:
            s_scr[...] = jnp.zeros_like(s_scr)
            m_scr[...] = jnp.zeros_like(m_scr)
        else:
            s_scr[...] = s0_ref[0]
            m_scr[...] = m0_ref[0]

    rows = lax.broadcasted_iota(jnp.int32, (n, n), 0)
    cols = lax.broadcasted_iota(jnp.int32, (n, n), 1)
    le = rows <= cols
    ge = rows >= cols
    tri_le = le.astype(BF16)
    tri_ge = ge.astype(BF16)
    use_state = not (zero_init and nc == 1)

    def gate_terms(d):
        g_ref, gt_ref = (fwd_refs, bwd_refs)[d][3:5]
        gate = g_ref[0]
        gate_t = gt_ref[0]
        lf = _log_sigmoid(gate)
        lf_t = _log_sigmoid(gate_t)
        tri_c, tri_r = (tri_ge, tri_le) if d == 0 else (tri_le, tri_ge)
        bc = _dot(tri_c, jnp.concatenate(_split3(lf), axis=1))
        b_cols = bc[:, 0:128] + bc[:, 128:256] + bc[:, 256:384]
        br = _dot(jnp.concatenate(_split3(lf_t), axis=0), tri_r)
        b_rows = br[0:16] + br[16:32] + br[32:48]
        return gate, gate_t, b_cols, b_rows, jnp.sum(lf_t, axis=-1, keepdims=True)

    terms = [gate_terms(0), gate_terms(1)]
    hs = ([], [])
    for hd in range(HEADS):
        hsl = slice(hd * dh, (hd + 1) * dh)
        st = None
        for d in range(N_DIR):
            k_ref, qt_ref, vt_ref = (fwd_refs, bwd_refs)[d][0:3]
            gate, gate_t, b_cols, b_rows, tot_rows = terms[d]
            ci = d * 8 + hd
            cf = d * 8 + 4 + hd
            unit = d * HEADS + hd
            k = k_ref[0, :, hsl]
            qt = qt_ref[0, hsl, :]
            if st is None or nc > 1:
                st = _dot(k, qt)
            mask = le if d == 0 else ge
            h, s_new, m_new = _scan_unit(
                st, k, qt, vt_ref[0, hsl, :],
                gate[:, ci:ci + 1] - b_cols[:, cf:cf + 1],
                gate_t[ci:ci + 1, :] - b_rows[cf:cf + 1, :],
                b_rows[cf:cf + 1, :], tot_rows[cf:cf + 1, :],
                mask, s_scr[unit], m_scr[unit][:, 0:1], use_state)
            s_scr[unit] = s_new
            m_scr[unit] = jnp.broadcast_to(m_new, (1, LANES))
            hs[d].append(h)
    for d in range(N_DIR):
        h_refs[d][0] = jnp.concatenate(hs[d], axis=1).astype(BF16)

    if emit_state:
        @pl.when(j == nc - 1)
        def _():
            for unit in range(N_DIR * HEADS):
                s = s_scr[unit]
                c_out[0, unit] = s[:dh].T
                n_out[0, unit] = s[dh:dh + 1]
                m_out[0, unit] = m_scr[unit]


def _mlstm(k, qt, vt, gate, gate_t, s0, m0, emit_state):
    b, t, _ = k.shape
    nc = t // CHUNK
    zero_init = s0 is None
    nu = N_DIR * HEADS
    fwd = lambda w: pl.BlockSpec((1, CHUNK, w), lambda i, j: (i, j, 0))
    bwd = lambda w: pl.BlockSpec((1, CHUNK, w), lambda i, j: (i, nc - 1 - j, 0))
    fwd_t = lambda r: pl.BlockSpec((1, r, CHUNK), lambda i, j: (i, 0, j))
    bwd_t = lambda r: pl.BlockSpec((1, r, CHUNK), lambda i, j: (i, 0, nc - 1 - j))
    args = [k, qt, vt, gate, gate_t]
    in_specs = [fwd(512), fwd_t(512), fwd_t(512), fwd(LANES), fwd_t(16)]
    if nc > 1:
        args += [k, qt, vt, gate, gate_t]
        in_specs += [bwd(512), bwd_t(512), bwd_t(512), bwd(LANES), bwd_t(16)]
    if not zero_init:
        args += [s0, m0]
        in_specs += [pl.BlockSpec((1, nu, 2 * HEAD_DIM, HEAD_DIM), lambda i, j: (i, 0, 0, 0)),
                     pl.BlockSpec((1, nu, 1, LANES), lambda i, j: (i, 0, 0, 0))]
    sd = jax.ShapeDtypeStruct
    out_shape = [sd((b, t, 512), BF16), sd((b, t, 512), BF16)]
    out_specs = [fwd(512), bwd(512)]
    if emit_state:
        out_shape += [sd((b, nu, HEAD_DIM, HEAD_DIM), F32), sd((b, nu, 1, HEAD_DIM), F32),
                      sd((b, nu, 1, LANES), F32)]
        out_specs += [pl.BlockSpec((1, nu, HEAD_DIM, HEAD_DIM), lambda i, j: (i, 0, 0, 0)),
                      pl.BlockSpec((1, nu, 1, HEAD_DIM), lambda i, j: (i, 0, 0, 0)),
                      pl.BlockSpec((1, nu, 1, LANES), lambda i, j: (i, 0, 0, 0))]
    return pl.pallas_call(
        functools.partial(_mlstm_kernel, nc=nc, zero_init=zero_init, emit_state=emit_state),
        grid=(b, nc),
        in_specs=in_specs,
        out_specs=out_specs,
        out_shape=out_shape,
        scratch_shapes=[pltpu.VMEM((nu, 2 * HEAD_DIM, HEAD_DIM), F32),
                        pltpu.VMEM((nu, 1, LANES), F32)],
        compiler_params=_params("arbitrary", "arbitrary"),
        name="mlstm",
    )(*args)


def _outproj_kernel(x_ref, u_ref, hf_ref, hb_ref, o_ref, mod_ref, pinv_ref, wp_ref, ps_ref,
                    wo_ref, gpm_ref, gpf_ref, wr1_ref, wr2_ref, wr3_ref, x1_ref, h2_ref, xp_ref, lg_ref,
                    slab_ref, pm_ref, *, tm, grid):
    i = pl.program_id(1)
    t_all = u_ref.shape[1]

    @pl.when((pl.program_id(0) == 0) & (i == 0))
    def _():
        rows_per = 256
        for r0 in range(0, t_all, rows_per):
            tgt = lax.broadcasted_iota(jnp.int32, (rows_per, t_all), 0) + r0
            src = lax.broadcasted_iota(jnp.int32, (rows_per, t_all), 1)
            if grid:
                shift = GRID_W.bit_length() - 1
                offs = ((src >> shift) - (tgt >> shift), (src & (GRID_W - 1)) - (tgt & (GRID_W - 1)))
            else:
                offs = (src - tgt,)
            for g, w in enumerate(POOL_WINDOWS):
                inside = None
                for d in offs:
                    ok = (d >= -(w // 2)) & (d < w - w // 2)
                    inside = ok if inside is None else inside & ok
                pm_ref[g, r0:r0 + rows_per, :] = jnp.where(inside, 1.0, 0.0).astype(BF16)

    x = x_ref[0]
    mod = mod_ref[0]
    row0 = pl.multiple_of(i * tm, tm)
    u_full = u_ref[0]
    u_tile = u_ref[0, pl.ds(row0, tm), :].astype(F32)
    diffs = []
    for g in range(POOL_GROUPS):
        sl = slice(g * POOL_GROUP_DIM, (g + 1) * POOL_GROUP_DIM)
        box = _dot(pm_ref[g, pl.ds(row0, tm), :], u_full[:, sl])
        diffs.append((box * pinv_ref[g] - u_tile[:, sl]).astype(BF16))
    yps = [_dot(jnp.concatenate(diffs[2 * p:2 * p + 2], axis=1), wp_ref[p]) for p in range(POOL_GROUPS // 2)]
    y_pool = jnp.concatenate(yps, axis=1) * ps_ref[...]
    hsum = hf_ref[0].astype(F32) + hb_ref[0].astype(F32)
    y_ml = jax.nn.sigmoid(o_ref[0].astype(F32)) * hsum
    mix = _dot(jnp.concatenate([y_pool, y_ml], axis=1).astype(BF16), wo_ref[...])
    x1 = x + mod[2:3] * _rmsnorm(mix, gpm_ref[...])
    x1_ref[0] = x1
    h2 = _rmsnorm(x1, gpf_ref[...]) * (1.0 + mod[4:5]) + mod[3:4]
    p1, p2, p3 = _split3(h2)
    h2_ref[0] = p1
    for cc in range(D_MODEL // LANES):
        slab_ref[:, cc, :] = h2[:, cc * LANES:(cc + 1) * LANES]
    xp_ref[0] = slab_ref[...].astype(BF16)
    ps3 = jnp.concatenate([p1, p2, p3], axis=0)
    r1 = _dot(ps3, wr1_ref[...])
    r2 = _dot(ps3[:2 * tm], wr2_ref[...])
    r3 = _dot(p1, wr3_ref[...])
    lg = r1[0:tm] + r1[tm:2 * tm] + r1[2 * tm:] + r2[0:tm] + r2[tm:] + r3
    lg_ref[0] = lg.T[0:N_EXPERTS]


def _outproj(x, u, hf, hb, o, mod, mod_row, grid, pinv, wp, ps, wo, gpm, gpf, wr1, wr2, wr3):
    b, t, _ = x.shape
    tm = min(t, 2 * TOKEN_TILE)
    const = lambda *shape: pl.BlockSpec(shape, lambda i, j: (0,) * len(shape))
    tok = lambda w: pl.BlockSpec((1, tm, w), lambda i, j: (i, j, 0))
    sd = jax.ShapeDtypeStruct
    return pl.pallas_call(
        functools.partial(_outproj_kernel, tm=tm, grid=grid),
        grid=(b, t // tm),
        in_specs=[tok(D_MODEL),
                  pl.BlockSpec((1, t, 512), lambda i, j: (i, 0, 0)),
                  tok(512), tok(512), tok(512),
                  pl.BlockSpec((1, N_MOD, D_MODEL), lambda i, j: (mod_row(i), 0, 0)),
                  pl.BlockSpec((POOL_GROUPS, tm, 1), lambda i, j: (0, j, 0)),
                  const(POOL_GROUPS // 2, 2 * POOL_GROUP_DIM, 2 * POOL_GROUP_DIM), const(1, POOL_WIDTH),
                  const(D_MODEL, D_MODEL), const(1, D_MODEL), const(1, D_MODEL),
                  const(D_MODEL, LANES), const(D_MODEL, LANES), const(D_MODEL, LANES)],
        out_specs=[tok(D_MODEL), tok(D_MODEL),
                   pl.BlockSpec((1, tm, D_MODEL // LANES, LANES), lambda i, j: (i, j, 0, 0)),
                   pl.BlockSpec((1, N_EXPERTS, tm), lambda i, j: (i, 0, j))],
        out_shape=[sd((b, t, D_MODEL), F32), sd((b, t, D_MODEL), BF16),
                   sd((b, t, D_MODEL // LANES, LANES), BF16), sd((b, N_EXPERTS, t), F32)],
        scratch_shapes=[pltpu.VMEM((tm, D_MODEL // LANES, LANES), F32), pltpu.VMEM((POOL_GROUPS, t, t), BF16)],
        compiler_params=_params("arbitrary", "arbitrary"),
        name="outproj",
    )(x, u, hf, hb, o, mod, pinv, wp, ps, wo, gpm, gpf, wr1, wr2, wr3)


def _router_kernel(lg_ref, br_ref, comb_ref, sel_ref):
    s = jax.nn.sigmoid(lg_ref[...])
    biased = s + br_ref[...]
    gidx = lax.broadcasted_iota(jnp.int32, s.shape, 0)
    jidx = lax.broadcasted_iota(jnp.int32, s.shape, 1)
    neg = -jnp.inf
    m1 = jnp.max(biased, axis=1, keepdims=True)
    i1 = jnp.min(jnp.where(biased == m1, jidx, GROUP_SIZE), axis=1, keepdims=True)
    m2 = jnp.max(jnp.where(jidx == i1, neg, biased), axis=1, keepdims=True)
    gscore = m1 + m2
    gi = lax.broadcasted_iota(jnp.int32, gscore.shape, 0)
    gmask = jnp.zeros(gscore.shape, F32)
    cur = gscore
    for _ in range(TOPK_GROUPS):
        mx = jnp.max(cur, axis=0, keepdims=True)
        ix = jnp.min(jnp.where(cur == mx, gi, N_EXPERT_GROUPS), axis=0, keepdims=True)
        hit = gi == ix
        gmask = jnp.where(hit, 1.0, gmask)
        cur = jnp.where(hit, neg, cur)
    cur = jnp.where(gmask > 0, biased, neg)
    eidx = gidx * GROUP_SIZE + jidx
    selmask = jnp.zeros(s.shape, F32)
    for _ in range(TOP_K):
        mx = jnp.max(jnp.max(cur, axis=1, keepdims=True), axis=0, keepdims=True)
        ix = jnp.where(cur == mx, eidx, N_EXPERTS)
        ix = jnp.min(jnp.min(ix, axis=1, keepdims=True), axis=0, keepdims=True)
        hit = eidx == ix
        selmask = jnp.where(hit, 1.0, selmask)
        cur = jnp.where(hit, neg, cur)
    sel = selmask * s
    tot = jnp.sum(jnp.sum(sel, axis=1, keepdims=True), axis=0, keepdims=True)
    comb_ref[...] = sel / tot * ROUTED_SCALE
    sel_ref[...] = selmask


def _router(logits_t, b_router):
    t = logits_t.shape[1]
    tl = 1024
    shp = (N_EXPERT_GROUPS, GROUP_SIZE, t)
    blk = pl.BlockSpec((N_EXPERT_GROUPS, GROUP_SIZE, tl), lambda j: (0, 0, j))
    comb, sel = pl.pallas_call(
        _router_kernel,
        grid=(t // tl,),
        in_specs=[blk, pl.BlockSpec((N_EXPERT_GROUPS, GROUP_SIZE, 1), lambda j: (0, 0, 0))],
        out_specs=[blk, blk],
        out_shape=[jax.ShapeDtypeStruct(shp, F32), jax.ShapeDtypeStruct(shp, F32)],
        compiler_params=_params("arbitrary"),
        name="router",
    )(logits_t.reshape(shp), b_router.reshape(N_EXPERT_GROUPS, GROUP_SIZE, 1))
    return comb.reshape(N_EXPERTS, t), sel.reshape(N_EXPERTS, t)


def _plan_kernel(sel_ref, pos_ref, meta_ref, emeta_ref, *, n_meta):
    t = sel_ref.shape[1]
    tm = float(MOE_TILE)
    sel = sel_ref[...]
    selb = sel.astype(BF16)
    blk = 256
    rr = lax.broadcasted_iota(jnp.int32, (blk, blk), 0)
    cc = lax.broadcasted_iota(jnp.int32, (blk, blk), 1)
    before = (rr < cc).astype(BF16)
    carry = jnp.zeros((N_EXPERTS, 1), F32)
    ranks = []
    for b in range(t // blk):
        sb = selb[:, b * blk:(b + 1) * blk]
        ranks.append(_dot(sb, before) + carry)
        carry = carry + jnp.sum(sel[:, b * blk:(b + 1) * blk], axis=1, keepdims=True)
    rank = jnp.concatenate(ranks, axis=1)
    cnt = carry
    ntile = jnp.floor((cnt + (tm - 1.0)) * (1.0 / tm))
    er = lax.broadcasted_iota(jnp.int32, (N_EXPERTS, N_EXPERTS), 0)
    ec = lax.broadcasted_iota(jnp.int32, (N_EXPERTS, N_EXPERTS), 1)
    below = (ec < er).astype(BF16)
    tstart = _dot(below, jnp.broadcast_to(ntile, (N_EXPERTS, LANES)).astype(BF16))[:, 0:1]
    pos = tstart * tm + rank
    erank = _dot(below, selb)
    rows = []
    for k in range(TOP_K):
        hit = (sel > 0.0) & (erank == float(k))
        rows.append(jnp.sum(jnp.where(hit, pos, 0.0), axis=0, keepdims=True))
    rows += [jnp.zeros((1, t), F32)] * (8 - TOP_K)
    pos_ref[...] = jnp.concatenate(rows, axis=0).astype(jnp.int32)

    tau = lax.broadcasted_iota(jnp.int32, (N_EXPERTS, n_meta), 1).astype(F32)
    eidx = lax.broadcasted_iota(jnp.int32, (N_EXPERTS, n_meta), 0).astype(F32)
    te = jnp.sum(((tstart + ntile) <= tau).astype(F32), axis=0, keepdims=True)
    te = jnp.minimum(te, float(N_EXPERTS - 1))
    onehot = eidx == te
    cnt_t = jnp.sum(jnp.where(onehot, cnt, 0.0), axis=0, keepdims=True)
    ts_t = jnp.sum(jnp.where(onehot, tstart, 0.0), axis=0, keepdims=True)
    tr = jnp.clip(cnt_t - (tau[0:1] - ts_t) * tm, 0.0, tm)
    tf = jnp.where((tau[0:1] == ts_t) & (tr > 0.0), 1.0, 0.0)
    meta_ref[...] = jnp.concatenate([te, tr, tf] + [jnp.zeros((1, n_meta), F32)] * 5, axis=0).astype(jnp.int32)

    eye = (lax.broadcasted_iota(jnp.int32, (N_EXPERTS, LANES), 0)
           == lax.broadcasted_iota(jnp.int32, (N_EXPERTS, LANES), 1))
    as_row = lambda col: jnp.sum(jnp.where(eye, col, 0.0), axis=0, keepdims=True)
    emeta_ref[...] = jnp.concatenate([as_row(cnt), as_row(tstart), as_row(ntile)]
                                     + [jnp.zeros((1, LANES), F32)] * 5, axis=0).astype(jnp.int32)


def _plan(sel, n_meta):
    t = sel.shape[1]
    sd = jax.ShapeDtypeStruct
    return pl.pallas_call(
        functools.partial(_plan_kernel, n_meta=n_meta),
        out_shape=[sd((8, t), jnp.int32), sd((8, n_meta), jnp.int32), sd((8, LANES), jnp.int32)],
        compiler_params=pltpu.CompilerParams(vmem_limit_bytes=VMEM_LIMIT),
        name="plan",
    )(sel)


def _fill_slot_table(pos_ref, emeta_ref, tbl_ref, n_tokens):
    group = 32
    def pad_expert(e, carry):
        n_t = emeta_ref[2 * LANES + e]
        start = (emeta_ref[LANES + e] + n_t - 1) * MOE_TILE

        @pl.when(n_t > 0)
        def _():
            def put(i, c):
                for u in range(group):
                    tbl_ref[start + i * group + u] = n_tokens
                return c
            lax.fori_loop(0, MOE_TILE // group, put, 0)
        return carry
    lax.fori_loop(0, N_EXPERTS, pad_expert, 0)
    last = N_EXPERTS - 1
    def pad_tile(tile, carry):
        def put(i, c):
            for u in range(group):
                tbl_ref[tile * MOE_TILE + i * group + u] = n_tokens
            return c
        lax.fori_loop(0, MOE_TILE // group, put, 0)
        return carry
    lax.fori_loop(emeta_ref[LANES + last] + emeta_ref[2 * LANES + last], tbl_ref.shape[0] // MOE_TILE, pad_tile, 0)

    unroll = 8
    def scatter(i, carry):
        first = i * (8 * unroll)
        for u in range(unroll):
            for k in range(TOP_K):
                tbl_ref[pos_ref[first + (8 * u + k)]] = i * unroll + u
        return carry
    lax.fori_loop(0, n_tokens // unroll, scatter, 0)


def _experts_kernel(te_ref, tr_ref, pos_ref, emeta_ref, xc_ref, xs_ref, comb_ref, wgu_ref, wd_ref, acc_out,
                    tbl_ref, xbuf, acc, stage0, stage1, cst0, cst1, act0, act1, ybuf0, ybuf1,
                    wgu_b0, wgu_b1, wd_b0, wd_b1, sems):
    s = pl.program_id(0)
    n_tiles = te_ref.shape[0]
    tm = MOE_TILE
    nch = D_MODEL // LANES
    tile_at = lambda lag: jnp.clip(s - lag, 0, n_tiles - 1)
    t_g, t_1, t_2, t_3 = tile_at(0), tile_at(1), tile_at(2), tile_at(3)

    @pl.when(s == 0)
    def _():
        tc, ts = xc_ref.shape[0], xs_ref.shape[0]
        copies = (pltpu.make_async_copy(xc_ref, xbuf.at[pl.ds(0, tc)], sems.at[0]),
                  pltpu.make_async_copy(xs_ref, xbuf.at[pl.ds(tc, ts)], sems.at[1]))
        for cp in copies:
            cp.start()
        n_pad = xbuf.shape[0] - tc - ts
        xbuf[pl.ds(tc + ts, n_pad)] = jnp.zeros((n_pad,) + xbuf.shape[1:], BF16)
        for ref in (acc, stage0, stage1, cst0, cst1, act0, act1, ybuf0, ybuf1, wgu_b0, wgu_b1, wd_b0, wd_b1):
            ref[...] = jnp.zeros_like(ref)
        _fill_slot_table(pos_ref, emeta_ref, tbl_ref, tc + ts)
        for cp in copies:
            cp.wait()

    def gather(tile, stage, cst):
        base = tile * tm
        for j in range(tm):
            tok = tbl_ref[base + j]
            stage[pl.ds(j * nch, nch), :] = xbuf[tok].astype(F32)
            cst[pl.ds(j, 1), :] = comb_ref[pl.ds(tok, 1), :]

    def gate_up(tile, stage, cst, wgu_b, act):
        xb = jnp.concatenate([stage[pl.ds(cc, tm, stride=nch), :] for cc in range(nch)], axis=1).astype(BF16)
        gu = _dot(xb, wgu_b[...])
        lane = lax.broadcasted_iota(jnp.int32, (1, LANES), 1)
        w_col = jnp.sum(jnp.where(lane == te_ref[tile], cst[...], 0.0), axis=1, keepdims=True)
        act[...] = (_silu(gu[:, :EXPERT_DIM]) * gu[:, EXPERT_DIM:] * w_col).astype(BF16)

    def down(act, wd_b, ybuf):
        y = _dot(act[...], wd_b[...])
        for cc in range(nch):
            ybuf[cc * Y_PITCH:cc * Y_PITCH + tm, :] = y[:, cc * LANES:(cc + 1) * LANES]

    def scatter(tile, ybuf):
        base = tile * tm
        sc_n = 16
        for i in range(tm // sc_n):
            toks = [tbl_ref[base + i * sc_n + u] for u in range(sc_n)]
            olds = [acc[toks[u]] for u in range(sc_n)]
            news = [olds[u] + ybuf[pl.ds(i * sc_n + u, nch, stride=Y_PITCH), :] for u in range(sc_n)]
            for u in range(sc_n):
                acc[toks[u]] = news[u]

    busy = (tr_ref[t_g] + tr_ref[t_1] + tr_ref[t_2] + tr_ref[t_3]) > 0
    bufs = ((stage0, cst0, act0, ybuf0, wgu_b0, wd_b0), (stage1, cst1, act1, ybuf1, wgu_b1, wd_b1))
    for par in range(2):
        stage_p, cst_p, act_p, ybuf_p, wgu_p, wd_p = bufs[par]
        stage_q, cst_q, act_q, ybuf_q, wgu_q, wd_q = bufs[1 - par]

        @pl.when(busy & (s % 2 == par))
        def _():
            gather(t_g, stage_p, cst_p)
            wgu_p[...] = wgu_ref[0].astype(BF16)
            gate_up(t_1, stage_q, cst_q, wgu_q, act_q)
            wd_q[...] = wd_ref[0].astype(BF16)
            down(act_p, wd_p, ybuf_p)
            scatter(t_3, ybuf_q)

    @pl.when(s == pl.num_programs(0) - 1)
    def _():
        cp = pltpu.make_async_copy(acc, acc_out, sems.at[2])
        cp.start()
        cp.wait()


def _experts(te, tr, pos, emeta, xc, xs, comb, wgu, wd):
    n_tok = comb.shape[0]
    n_tiles = te.shape[0]
    tm = MOE_TILE
    nch = D_MODEL // LANES
    vm = pltpu.VMEM
    grid_spec = pltpu.PrefetchScalarGridSpec(
        num_scalar_prefetch=4,
        grid=(n_tiles + 3,),
        in_specs=[pl.BlockSpec(memory_space=pl.ANY), pl.BlockSpec(memory_space=pl.ANY),
                  pl.BlockSpec((n_tok, LANES), lambda s, *_: (0, 0), pipeline_mode=pl.Buffered(1)),
                  pl.BlockSpec((1, D_MODEL, 2 * EXPERT_DIM), lambda s, te, *_: (te[jnp.minimum(s, n_tiles - 1)], 0, 0)),
                  pl.BlockSpec((1, EXPERT_DIM, D_MODEL), lambda s, te, *_: (te[jnp.clip(s - 1, 0, n_tiles - 1)], 0, 0))],
        out_specs=pl.BlockSpec(memory_space=pl.ANY),
        scratch_shapes=[pltpu.SMEM((n_tiles * tm,), jnp.int32),
                        vm((n_tok, nch, LANES), BF16), vm((n_tok, nch, LANES), F32),
                        vm((tm * nch, LANES), F32), vm((tm * nch, LANES), F32),
                        vm((tm, LANES), F32), vm((tm, LANES), F32),
                        vm((tm, EXPERT_DIM), BF16), vm((tm, EXPERT_DIM), BF16),
                        vm((nch * Y_PITCH, LANES), F32), vm((nch * Y_PITCH, LANES), F32),
                        vm((D_MODEL, 2 * EXPERT_DIM), BF16), vm((D_MODEL, 2 * EXPERT_DIM), BF16),
                        vm((EXPERT_DIM, D_MODEL), BF16), vm((EXPERT_DIM, D_MODEL), BF16),
                        pltpu.SemaphoreType.DMA((3,))],
    )
    return pl.pallas_call(
        _experts_kernel,
        grid_spec=grid_spec,
        out_shape=jax.ShapeDtypeStruct((n_tok, nch, LANES), F32),
        compiler_params=pltpu.CompilerParams(dimension_semantics=("arbitrary",),
                                             vmem_limit_bytes=EXPERTS_VMEM_LIMIT),
        name="experts",
    )(te, tr, pos, emeta, xc, xs, comb, wgu, wd)


def _final_kernel(acc_ref, h_ref, x1_ref, mod_ref, wsg_ref, wsd_ref, gpo_ref, out_ref):
    routed = jnp.concatenate([acc_ref[:, cc, :] for cc in range(D_MODEL // LANES)], axis=1)
    gs = _dot(h_ref[...], wsg_ref[...])
    act = _silu(gs[:, :SHARED_DIM]) * gs[:, SHARED_DIM:]
    f = routed + _dot(act.astype(BF16), wsd_ref[...])
    out_ref[...] = x1_ref[...] + mod_ref[0][5:6] * _rmsnorm(f, gpo_ref[...])


def _final(acc, tile0, h2, x1, mod, mod_row, wsg, wsd, gpo):
    n = h2.shape[0]
    tm = FINAL_TILE
    const = lambda *shape: pl.BlockSpec(shape, lambda i: (0,) * len(shape))
    tok = lambda w: pl.BlockSpec((tm, w), lambda i: (i, 0))
    return pl.pallas_call(
        _final_kernel,
        grid=(n // tm,),
        in_specs=[pl.BlockSpec((tm, D_MODEL // LANES, LANES), lambda i: (i + tile0, 0, 0)),
                  tok(D_MODEL), tok(D_MODEL),
                  pl.BlockSpec((1, N_MOD, D_MODEL), lambda i: (mod_row(i), 0, 0)),
                  const(D_MODEL, 2 * SHARED_DIM), const(SHARED_DIM, D_MODEL), const(1, D_MODEL)],
        out_specs=tok(D_MODEL),
        out_shape=jax.ShapeDtypeStruct((n, D_MODEL), F32),
        compiler_params=_params("arbitrary"),
        name="final",
    )(acc, h2, x1, mod, wsg, wsd, gpo)


def _window_bounds(n, w):
    idx = np.arange(n)
    return np.clip(idx - w // 2, 0, n), np.clip(idx + w - w // 2, 0, n)


def _pool_inv_counts(t, grid):
    invs = []
    for w in POOL_WINDOWS:
        if grid:
            rlo, rhi = _window_bounds(t // GRID_W, w)
            clo, chi = _window_bounds(GRID_W, w)
            cnt = (rhi - rlo)[np.arange(t) // GRID_W] * (chi - clo)[np.arange(t) % GRID_W]
        else:
            lo, hi = _window_bounds(t, w)
            cnt = hi - lo
        invs.append((1.0 / cnt.astype(np.float64)).astype(np.float32)[:, None])
    return jnp.asarray(np.stack(invs), F32)


def kernel(x_prompt, x_sample, state_C, state_n, state_m, c, c_ctx, w_ada, b_ada, g_pre_mix, w_in, b_gate,
           w_pool, pool_scale, w_out, g_post_mix, g_pre_ffn, w_router, b_router, w_expert_gu, w_expert_down,
           w_shared_gu, w_shared_down, g_post_ffn):
    b_ctx = x_prompt.shape[0]
    b_lat = x_sample.shape[0]
    nu = N_DIR * HEADS
    l = 0
    row = lambda a: a[l].reshape(1, -1).astype(F32)

    cvec = jnp.zeros((16, D_MODEL), F32).at[0].set(c_ctx.astype(F32)).at[1:1 + b_lat].set(c.astype(F32))
    mod = _mod_rows(cvec, w_ada[l], b_ada[l]).reshape(16, N_MOD, D_MODEL)

    w_in_l = w_in[l]
    p0 = POOL_WIDTH
    mw = MLSTM_WIDTH
    w_u, w_q, w_k, w_v, w_o = (w_in_l[:, lo:lo + 512] for lo in (0, p0, p0 + mw, p0 + 2 * mw, p0 + 3 * mw))
    wm = jnp.concatenate([w_u, w_k, w_o], axis=1).astype(BF16)
    wt = jnp.concatenate([w_q.T, w_v.T], axis=0).astype(BF16)
    wg_cols = w_in_l[:, p0 + 4 * mw:]
    wg1, wg2 = _split2(jnp.pad(wg_cols, ((0, 0), (0, LANES - GATE_COLS))))
    bg = b_gate[l].reshape(GATE_COLS).astype(F32)
    bgr = jnp.pad(bg, (0, LANES - GATE_COLS)).reshape(1, LANES)
    wpl = w_pool[l].astype(BF16)
    zg = jnp.zeros((POOL_GROUP_DIM, POOL_GROUP_DIM), BF16)
    wp = jnp.stack([jnp.block([[wpl[2 * p], zg], [zg, wpl[2 * p + 1]]]) for p in range(POOL_GROUPS // 2)])
    wo = w_out[l].astype(BF16)
    wr1, wr2, wr3 = _split3(jnp.pad(w_router[l].astype(F32), ((0, 0), (0, LANES - N_EXPERTS))))
    wsg = w_shared_gu[l].astype(BF16)
    wsd = w_shared_down[l].astype(BF16)

    def mixer(x, mod_row, grid, s0, m0, emit_state):
        t = x.shape[1]
        u, k, o, qt, vt, gate, gate_t = _inproj(x.astype(F32), mod, mod_row, row(g_pre_mix), wm, wt, wg1, wg2,
                                                bgr)
        outs = _mlstm(k, qt, vt, gate, gate_t, s0, m0, emit_state)
        hf, hb = outs[0], outs[1]
        pinv = _pool_inv_counts(t, grid)
        x1, h2, xp, lg = _outproj(x.astype(F32), u, hf, hb, o, mod, mod_row, grid, pinv, wp, row(pool_scale), wo,
                                  row(g_post_mix), row(g_pre_ffn), wr1, wr2, wr3)
        return x1, h2, xp, lg, outs[2:]

    ctx_row = lambda i: 0
    lat_row = lambda i: i + 1
    x1c, h2c, xpc, lgc, (c_new, n_new, m_new) = mixer(x_prompt, ctx_row, False, None, None, True)
    s0 = jnp.concatenate(
        [jnp.swapaxes(state_C[:, l].reshape(b_lat, nu, HEAD_DIM, HEAD_DIM).astype(F32), -1, -2),
         jnp.broadcast_to(state_n[:, l].reshape(b_lat, nu, 1, HEAD_DIM).astype(F32),
                          (b_lat, nu, HEAD_DIM, HEAD_DIM))], axis=-2)
    m0 = jnp.broadcast_to(state_m[:, l].reshape(b_lat, nu, 1, 1).astype(F32), (b_lat, nu, 1, LANES))
    x1s, h2s, xps, lgs, _ = mixer(x_sample, lat_row, True, s0, m0, False)

    tc = b_ctx * x_prompt.shape[1]
    ts = b_lat * x_sample.shape[1]
    n_tok = tc + ts
    lg_all = jnp.concatenate([lgc.transpose(1, 0, 2).reshape(N_EXPERTS, tc),
                              lgs.transpose(1, 0, 2).reshape(N_EXPERTS, ts)], axis=1)
    comb, sel = _router(lg_all, b_router[l].astype(F32))
    n_tiles = n_tok * TOP_K // MOE_TILE + N_EXPERTS
    n_meta = -(-n_tiles // LANES) * LANES
    pos, meta, emeta = _plan(sel, n_meta)

    comb_tok = jnp.pad(comb.T, ((0, 8), (0, LANES - N_EXPERTS)))
    slab = (D_MODEL // LANES, LANES)
    acc = _experts(meta[0, :n_tiles], meta[1, :n_tiles],
                   pos.T.reshape(-1), emeta.reshape(-1),
                   xpc.reshape((tc,) + slab), xps.reshape((ts,) + slab), comb_tok,
                   w_expert_gu[l], w_expert_down[l])

    fin = functools.partial(_final, wsg=wsg, wsd=wsd, gpo=row(g_post_ffn))
    tiles_per_lat = x_sample.shape[1] // FINAL_TILE
    yc = fin(acc, 0, h2c.reshape(tc, D_MODEL), x1c.reshape(tc, D_MODEL), mod, ctx_row)
    ys = fin(acc, tc // FINAL_TILE, h2s.reshape(ts, D_MODEL), x1s.reshape(ts, D_MODEL), mod,
             lambda i: i // tiles_per_lat + 1)

    new_c = c_new.reshape(b_ctx, 1, N_DIR, HEADS, HEAD_DIM, HEAD_DIM)
    new_n = n_new.reshape(b_ctx, 1, N_DIR, HEADS, HEAD_DIM)
    new_m = m_new[..., 0].reshape(b_ctx, 1, N_DIR, HEADS)
    return (yc.reshape(x_prompt.shape), ys.reshape(x_sample.shape), new_c, new_n, new_m)
```

```python
import functools

import jax
import jax.numpy as jnp
import numpy as np
from jax import lax
from jax.experimental import pallas as pl
from jax.experimental.pallas import tpu as pltpu

F32 = jnp.float32
BF16 = jnp.bfloat16

D_MODEL = 1024
GRID_W = 64
POOL_WIDTH = 512
POOL_GROUPS = 4
POOL_GROUP_DIM = 128
POOL_WINDOWS = (2, 4, 8, 16)
HEADS = 4
HEAD_DIM = 128
MLSTM_WIDTH = HEADS * HEAD_DIM
N_DIR = 2
GATE_COLS = N_DIR * 2 * HEADS
N_EXPERTS = 64
TOP_K = 6
N_EXPERT_GROUPS = 8
GROUP_SIZE = N_EXPERTS // N_EXPERT_GROUPS
TOPK_GROUPS = 4
EXPERT_DIM = 256
SHARED_DIM = 256
ROUTED_SCALE = 2.5
N_MOD = 6
EPS = 1e-6
K_SCALE = HEAD_DIM ** -0.5

LANES = 128
CHUNK = 256
N_ROWS = 16
TOKEN_TILE = 256
FINAL_TILE = 512
MOE_TILE = 256
Y_PITCH = MOE_TILE + 8
VMEM_LIMIT = 56 * 1024 * 1024
EXPERTS_VMEM_LIMIT = 58 * 1024 * 1024


def _split3(x):
    p1 = x.astype(BF16)
    r1 = x - p1.astype(F32)
    p2 = r1.astype(BF16)
    p3 = (r1 - p2.astype(F32)).astype(BF16)
    return p1, p2, p3


def _split2(x):
    p1 = x.astype(BF16)
    p2 = (x - p1.astype(F32)).astype(BF16)
    return p1, p2


def _dot(a, b):
    return jnp.dot(a, b, preferred_element_type=F32)


def _dot_nt(a, b):
    return lax.dot_general(a, b, (((1,), (1,)), ((), ())), preferred_element_type=F32)


def _rmsnorm(x, g):
    return x * lax.rsqrt(jnp.mean(x * x, axis=-1, keepdims=True) + EPS) * g


def _silu(x):
    return x * jax.nn.sigmoid(x)


def _params(*sem):
    return pltpu.CompilerParams(dimension_semantics=sem, vmem_limit_bytes=VMEM_LIMIT)


def _mod_kernel(c_ref, w_ref, b_ref, o_ref):
    a = _silu(c_ref[...])
    a_stack = jnp.concatenate(_split3(a), axis=0)
    w1, w2 = _split2(w_ref[...])
    r1 = _dot(a_stack, w1)
    r2 = _dot(a_stack[:32], w2)
    o_ref[...] = (r1[0:16] + r1[16:32] + r1[32:48] + r2[0:16] + r2[16:32]) + b_ref[...]


def _mod_rows(cvec, w_ada, b_ada):
    n = N_MOD * D_MODEL
    tn = 1536
    return pl.pallas_call(
        _mod_kernel,
        grid=(n // tn,),
        in_specs=[pl.BlockSpec((16, D_MODEL), lambda j: (0, 0)),
                  pl.BlockSpec((D_MODEL, tn), lambda j: (0, j)),
                  pl.BlockSpec((1, tn), lambda j: (0, j))],
        out_specs=pl.BlockSpec((16, tn), lambda j: (0, j)),
        out_shape=jax.ShapeDtypeStruct((16, n), F32),
        compiler_params=_params("arbitrary"),
        name="mod",
    )(cvec, w_ada, b_ada.reshape(1, n))


def _inproj_kernel(x_ref, mod_ref, g_ref, wm_ref, wt_ref, wg_ref, bgr_ref,
                   u_ref, k_ref, o_ref, qt_ref, vt_ref, gate_ref, gatet_ref):
    x = x_ref[0]
    mod = mod_ref[0]
    h = _rmsnorm(x, g_ref[...]) * (1.0 + mod[1:2]) + mod[0:1]
    h1, h2, h3 = _split3(h)
    z = _dot(h1, wm_ref[...])
    u_ref[0] = z[:, 0:512].astype(BF16)
    k_ref[0] = (z[:, 512:1024] * K_SCALE).astype(BF16)
    o_ref[0] = z[:, 1024:1536].astype(BF16)
    zt = _dot_nt(wt_ref[...], h1)
    qt_ref[0] = zt[0:512].astype(BF16)
    vt_ref[0] = zt[512:1024].astype(BF16)
    tm = x.shape[0]
    r = _dot(jnp.concatenate([h1, h2, h3], axis=0), wg_ref[...])
    r12 = r[0:tm] + r[tm:2 * tm]
    gate = (r12 + r[2 * tm:]) + pltpu.roll(r12, LANES - GATE_COLS, axis=1) + bgr_ref[...]
    gate_ref[0] = gate
    gatet_ref[0] = gate.T[0:16]


def _inproj(x, mod, mod_row, g, wm, wt, wg, bgr):
    b, t, _ = x.shape
    tm = min(t, 2 * TOKEN_TILE)
    const = lambda *shape: pl.BlockSpec(shape, lambda i, j: (0,) * len(shape))
    tok = lambda w: pl.BlockSpec((1, tm, w), lambda i, j: (i, j, 0))
    tok_t = lambda r: pl.BlockSpec((1, r, tm), lambda i, j: (i, 0, j))
    sd = jax.ShapeDtypeStruct
    return pl.pallas_call(
        _inproj_kernel,
        grid=(b, t // tm),
        in_specs=[tok(D_MODEL),
                  pl.BlockSpec((1, N_MOD, D_MODEL), lambda i, j: (mod_row(i), 0, 0)),
                  const(1, D_MODEL), const(D_MODEL, 1536), const(1024, D_MODEL),
                  const(D_MODEL, LANES), const(1, LANES)],
        out_specs=[tok(512), tok(512), tok(512), tok_t(512), tok_t(512), tok(LANES), tok_t(16)],
        out_shape=[sd((b, t, 512), BF16), sd((b, t, 512), BF16), sd((b, t, 512), BF16),
                   sd((b, 512, t), BF16), sd((b, 512, t), BF16), sd((b, t, LANES), F32),
                   sd((b, 16, t), F32)],
        compiler_params=_params("arbitrary", "arbitrary"),
        name="inproj",
    )(x, mod, g, wm, wt, wg, bgr)


def _log_sigmoid(x):
    return jnp.minimum(x, 0.0) - jnp.log1p(jnp.exp(-jnp.abs(x)))


def _scan_unit(st, k, qt, vt, u_col, u_row, b_row, btot, mask, s_prev, m_prev, use_state):
    dh = HEAD_DIM
    n = st.shape[0]
    ub = jnp.where(mask, jnp.broadcast_to(u_col, (n, n)), -jnp.inf)
    z = jnp.maximum(m_prev, jnp.max(ub, axis=0, keepdims=True))
    p = (jnp.exp(ub - z) * st).astype(BF16)
    ones = jnp.ones((N_ROWS, n), BF16)
    tot = _dot(jnp.concatenate([vt, ones], axis=0), p)
    if use_state:
        tot = tot + jnp.exp(m_prev - z) * _dot(s_prev.astype(BF16), qt)
    floor = jnp.exp(-(b_row + z))
    h_t = tot[:dh] / jnp.maximum(jnp.abs(tot[dh:dh + 1]), floor)
    g_row = btot + u_row
    m_new = jnp.maximum(btot + m_prev, jnp.max(g_row, axis=-1, keepdims=True))
    w_row = jnp.exp(g_row - m_new)
    vw = jnp.concatenate([(vt.astype(F32) * w_row).astype(BF16),
                          jnp.broadcast_to(w_row, (N_ROWS, n)).astype(BF16)], axis=0)
    s_new = jnp.exp(btot + m_prev - m_new) * s_prev + _dot(vw, k)
    return h_t.T, s_new, m_new


def _mlstm_kernel(*refs, nc, zero_init, emit_state):
    it = iter(refs)
    fwd_refs = tuple(next(it) for _ in range(5))
    bwd_refs = tuple(next(it) for _ in range(5)) if nc > 1 else fwd_refs
    if not zero_init:
        s0_ref, m0_ref = next(it), next(it)
    h_refs = (next(it), next(it))
    if emit_state:
        c_out, n_out, m_out = next(it), next(it), next(it)
    s_scr, m_scr = next(it), next(it)

    j = pl.program_id(1)
    n = CHUNK
    dh = HEAD_DIM

    @pl.when(j == 0)
    def _():
        if zero_init:
            s_scr[...] = jnp.zeros_like(s_scr)
            m_scr[...] = jnp.zeros_like(m_scr)
        else:
            s_scr[...] = s0_ref[0]
            m_scr[...] = m0_ref[0]

    rows = lax.broadcasted_iota(jnp.int32, (n, n), 0)
    cols = lax.broadcasted_iota(jnp.int32, (n, n), 1)
    le = rows <= cols
    ge = rows >= cols
    tri_le = le.astype(BF16)
    tri_ge = ge.astype(BF16)
    use_state = not (zero_init and nc == 1)

    def gate_terms(d):
        g_ref, gt_ref = (fwd_refs, bwd_refs)[d][3:5]
        gate = g_ref[0]
        gate_t = gt_ref[0]
        lf = _log_sigmoid(gate)
        lf_t = _log_sigmoid(gate_t)
        tri_c, tri_r = (tri_ge, tri_le) if d == 0 else (tri_le, tri_ge)
        bc = _dot(tri_c, jnp.concatenate(_split3(lf), axis=1))
        b_cols = bc[:, 0:128] + bc[:, 128:256] + bc[:, 256:384]
        br = _dot(jnp.concatenate(_split3(lf_t), axis=0), tri_r)
        b_rows = br[0:16] + br[16:32] + br[32:48]
        return gate, gate_t, b_cols, b_rows, jnp.sum(lf_t, axis=-1, keepdims=True)

    terms = [gate_terms(0), gate_terms(1)]
    hs = ([], [])
    for hd in range(HEADS):
        hsl = slice(hd * dh, (hd + 1) * dh)
        st = None
        for d in range(N_DIR):
            k_ref, qt_ref, vt_ref = (fwd_refs, bwd_refs)[d][0:3]
            gate, gate_t, b_cols, b_rows, tot_rows = terms[d]
            ci = d * 8 + hd
            cf = d * 8 + 4 + hd
            unit = d * HEADS + hd
            k = k_ref[0, :, hsl]
            qt = qt_ref[0, hsl, :]
            if st is None or nc > 1:
                st = _dot(k, qt)
            mask = le if d == 0 else ge
            h, s_new, m_new = _scan_unit(
                st, k, qt, vt_ref[0, hsl, :],
                gate[:, ci:ci + 1] - b_cols[:, cf:cf + 1],
                gate_t[ci:ci + 1, :] - b_rows[cf:cf + 1, :],
                b_rows[cf:cf + 1, :], tot_rows[cf:cf + 1, :],
                mask, s_scr[unit], m_scr[unit][:, 0:1], use_state)
            s_scr[unit] = s_new
            m_scr[unit] = jnp.broadcast_to(m_new, (1, LANES))
            hs[d].append(h)
    for d in range(N_DIR):
        h_refs[d][0] = jnp.concatenate(hs[d], axis=1).astype(BF16)

    if emit_state:
        @pl.when(j == nc - 1)
        def _():
            for unit in range(N_DIR * HEADS):
                s = s_scr[unit]
                c_out[0, unit] = s[:dh].T
                n_out[0, unit] = s[dh:dh + 1]
                m_out[0, unit] = m_scr[unit]


def _mlstm(k, qt, vt, gate, gate_t, s0, m0, emit_state):
    b, t, _ = k.shape
    nc = t // CHUNK
    zero_init = s0 is None
    nu = N_DIR * HEADS
    fwd = lambda w: pl.BlockSpec((1, CHUNK, w), lambda i, j: (i, j, 0))
    bwd = lambda w: pl.BlockSpec((1, CHUNK, w), lambda i, j: (i, nc - 1 - j, 0))
    fwd_t = lambda r: pl.BlockSpec((1, r, CHUNK), lambda i, j: (i, 0, j))
    bwd_t = lambda r: pl.BlockSpec((1, r, CHUNK), lambda i, j: (i, 0, nc - 1 - j))
    args = [k, qt, vt, gate, gate_t]
    in_specs = [fwd(512), fwd_t(512), fwd_t(512), fwd(LANES), fwd_t(16)]
    if nc > 1:
        args += [k, qt, vt, gate, gate_t]
        in_specs += [bwd(512), bwd_t(512), bwd_t(512), bwd(LANES), bwd_t(16)]
    if not zero_init:
        args += [s0, m0]
        in_specs += [pl.BlockSpec((1, nu, HEAD_DIM + N_ROWS, HEAD_DIM), lambda i, j: (i, 0, 0, 0)),
                     pl.BlockSpec((1, nu, 1, LANES), lambda i, j: (i, 0, 0, 0))]
    sd = jax.ShapeDtypeStruct
    out_shape = [sd((b, t, 512), BF16), sd((b, t, 512), BF16)]
    out_specs = [fwd(512), bwd(512)]
    if emit_state:
        out_shape += [sd((b, nu, HEAD_DIM, HEAD_DIM), F32), sd((b, nu, 1, HEAD_DIM), F32),
                      sd((b, nu, 1, LANES), F32)]
        out_specs += [pl.BlockSpec((1, nu, HEAD_DIM, HEAD_DIM), lambda i, j: (i, 0, 0, 0)),
                      pl.BlockSpec((1, nu, 1, HEAD_DIM), lambda i, j: (i, 0, 0, 0)),
                      pl.BlockSpec((1, nu, 1, LANES), lambda i, j: (i, 0, 0, 0))]
    return pl.pallas_call(
        functools.partial(_mlstm_kernel, nc=nc, zero_init=zero_init, emit_state=emit_state),
        grid=(b, nc),
        in_specs=in_specs,
        out_specs=out_specs,
        out_shape=out_shape,
        scratch_shapes=[pltpu.VMEM((nu, HEAD_DIM + N_ROWS, HEAD_DIM), F32),
                        pltpu.VMEM((nu, 1, LANES), F32)],
        compiler_params=_params("arbitrary", "arbitrary"),
        name="mlstm",
    )(*args)


def _outproj_kernel(x_ref, u_ref, hf_ref, hb_ref, o_ref, mod_ref, pm_ref, pinv_ref, wp_ref, ps_ref,
                    wo_ref, gpm_ref, gpf_ref, wr_ref, x1_ref, h2_ref, xp_ref, lg_ref, slab_ref, *, tm):
    i = pl.program_id(1)
    x = x_ref[0]
    mod = mod_ref[0]
    row0 = pl.multiple_of(i * tm, tm)
    u_full = u_ref[0]
    u_tile = u_ref[0, pl.ds(row0, tm), :].astype(F32)
    diffs = []
    for g in range(POOL_GROUPS):
        sl = slice(g * POOL_GROUP_DIM, (g + 1) * POOL_GROUP_DIM)
        box = _dot(pm_ref[g], u_full[:, sl])
        diffs.append((box * pinv_ref[g] - u_tile[:, sl]).astype(BF16))
    yps = [_dot(jnp.concatenate(diffs[2 * p:2 * p + 2], axis=1), wp_ref[p]) for p in range(POOL_GROUPS // 2)]
    y_pool = jnp.concatenate(yps, axis=1) * ps_ref[...]
    hsum = hf_ref[0].astype(F32) + hb_ref[0].astype(F32)
    y_ml = jax.nn.sigmoid(o_ref[0].astype(F32)) * hsum
    mix = _dot(jnp.concatenate([y_pool, y_ml], axis=1).astype(BF16), wo_ref[...])
    x1 = x + mod[2:3] * _rmsnorm(mix, gpm_ref[...])
    x1_ref[0] = x1
    h2 = _rmsnorm(x1, gpf_ref[...]) * (1.0 + mod[4:5]) + mod[3:4]
    p1, p2, p3 = _split3(h2)
    h2_ref[0] = p1
    for cc in range(D_MODEL // LANES):
        slab_ref[:, cc, :] = h2[:, cc * LANES:(cc + 1) * LANES]
    xp_ref[0] = slab_ref[...].astype(BF16)
    r = _dot(jnp.concatenate([p1, p2, p3], axis=0), wr_ref[...])
    r12 = r[0:tm] + r[tm:2 * tm]
    ne = N_EXPERTS
    lg = (r12 + r[2 * tm:])[:, 0:ne] + r12[:, ne:2 * ne] + r[0:tm, 2 * ne:3 * ne]
    lg_ref[0] = jnp.concatenate([lg, jnp.zeros_like(lg)], axis=1).T[0:ne]


def _outproj(x, u, hf, hb, o, mod, mod_row, pm, pinv, wp, ps, wo, gpm, gpf, wr):
    b, t, _ = x.shape
    tm = min(t, 2 * TOKEN_TILE)
    const = lambda *shape: pl.BlockSpec(shape, lambda i, j: (0,) * len(shape))
    tok = lambda w: pl.BlockSpec((1, tm, w), lambda i, j: (i, j, 0))
    sd = jax.ShapeDtypeStruct
    return pl.pallas_call(
        functools.partial(_outproj_kernel, tm=tm),
        grid=(b, t // tm),
        in_specs=[tok(D_MODEL),
                  pl.BlockSpec((1, t, 512), lambda i, j: (i, 0, 0)),
                  tok(512), tok(512), tok(512),
                  pl.BlockSpec((1, N_MOD, D_MODEL), lambda i, j: (mod_row(i), 0, 0)),
                  pl.BlockSpec((POOL_GROUPS, tm, t), lambda i, j: (0, j, 0)),
                  pl.BlockSpec((POOL_GROUPS, tm, 1), lambda i, j: (0, j, 0)),
                  const(POOL_GROUPS // 2, 2 * POOL_GROUP_DIM, 2 * POOL_GROUP_DIM), const(1, POOL_WIDTH),
                  const(D_MODEL, D_MODEL), const(1, D_MODEL), const(1, D_MODEL),
                  const(D_MODEL, 2 * LANES)],
        out_specs=[tok(D_MODEL), tok(D_MODEL),
                   pl.BlockSpec((1, tm, D_MODEL // LANES, LANES), lambda i, j: (i, j, 0, 0)),
                   pl.BlockSpec((1, N_EXPERTS, tm), lambda i, j: (i, 0, j))],
        out_shape=[sd((b, t, D_MODEL), F32), sd((b, t, D_MODEL), BF16),
                   sd((b, t, D_MODEL // LANES, LANES), BF16), sd((b, N_EXPERTS, t), F32)],
        scratch_shapes=[pltpu.VMEM((tm, D_MODEL // LANES, LANES), F32)],
        compiler_params=_params("arbitrary", "arbitrary"),
        name="outproj",
    )(x, u, hf, hb, o, mod, pm, pinv, wp, ps, wo, gpm, gpf, wr)


def _router_kernel(lg_ref, br_ref, comb_ref, sel_ref):
    s = jax.nn.sigmoid(lg_ref[...])
    biased = s + br_ref[...]
    gidx = lax.broadcasted_iota(jnp.int32, s.shape, 0)
    jidx = lax.broadcasted_iota(jnp.int32, s.shape, 1)
    neg = -jnp.inf
    m1 = jnp.max(biased, axis=1, keepdims=True)
    i1 = jnp.min(jnp.where(biased == m1, jidx, GROUP_SIZE), axis=1, keepdims=True)
    m2 = jnp.max(jnp.where(jidx == i1, neg, biased), axis=1, keepdims=True)
    gscore = m1 + m2
    gi = lax.broadcasted_iota(jnp.int32, gscore.shape, 0)
    gmask = jnp.zeros(gscore.shape, F32)
    cur = gscore
    for _ in range(TOPK_GROUPS):
        mx = jnp.max(cur, axis=0, keepdims=True)
        ix = jnp.min(jnp.where(cur == mx, gi, N_EXPERT_GROUPS), axis=0, keepdims=True)
        hit = gi == ix
        gmask = jnp.where(hit, 1.0, gmask)
        cur = jnp.where(hit, neg, cur)
    cur = jnp.where(gmask > 0, biased, neg)
    eidx = gidx * GROUP_SIZE + jidx
    selmask = jnp.zeros(s.shape, F32)
    for _ in range(TOP_K):
        mx = jnp.max(jnp.max(cur, axis=1, keepdims=True), axis=0, keepdims=True)
        ix = jnp.where(cur == mx, eidx, N_EXPERTS)
        ix = jnp.min(jnp.min(ix, axis=1, keepdims=True), axis=0, keepdims=True)
        hit = eidx == ix
        selmask = jnp.where(hit, 1.0, selmask)
        cur = jnp.where(hit, neg, cur)
    sel = selmask * s
    tot = jnp.sum(jnp.sum(sel, axis=1, keepdims=True), axis=0, keepdims=True)
    comb_ref[...] = sel / tot * ROUTED_SCALE
    sel_ref[...] = selmask


def _router(logits_t, b_router):
    t = logits_t.shape[1]
    tl = 1024
    shp = (N_EXPERT_GROUPS, GROUP_SIZE, t)
    blk = pl.BlockSpec((N_EXPERT_GROUPS, GROUP_SIZE, tl), lambda j: (0, 0, j))
    comb, sel = pl.pallas_call(
        _router_kernel,
        grid=(t // tl,),
        in_specs=[blk, pl.BlockSpec((N_EXPERT_GROUPS, GROUP_SIZE, 1), lambda j: (0, 0, 0))],
        out_specs=[blk, blk],
        out_shape=[jax.ShapeDtypeStruct(shp, F32), jax.ShapeDtypeStruct(shp, F32)],
        compiler_params=_params("arbitrary"),
        name="router",
    )(logits_t.reshape(shp), b_router.reshape(N_EXPERT_GROUPS, GROUP_SIZE, 1))
    return comb.reshape(N_EXPERTS, t), sel.reshape(N_EXPERTS, t)


def _plan_kernel(sel_ref, pos_ref, meta_ref, emeta_ref, *, n_meta):
    t = sel_ref.shape[1]
    tm = float(MOE_TILE)
    sel = sel_ref[...]
    selb = sel.astype(BF16)
    blk = 256
    rr = lax.broadcasted_iota(jnp.int32, (blk, blk), 0)
    cc = lax.broadcasted_iota(jnp.int32, (blk, blk), 1)
    before = (rr < cc).astype(BF16)
    carry = jnp.zeros((N_EXPERTS, 1), F32)
    ranks = []
    for b in range(t // blk):
        sb = selb[:, b * blk:(b + 1) * blk]
        ranks.append(_dot(sb, before) + carry)
        carry = carry + jnp.sum(sel[:, b * blk:(b + 1) * blk], axis=1, keepdims=True)
    rank = jnp.concatenate(ranks, axis=1)
    cnt = carry
    ntile = jnp.floor((cnt + (tm - 1.0)) * (1.0 / tm))
    er = lax.broadcasted_iota(jnp.int32, (N_EXPERTS, N_EXPERTS), 0)
    ec = lax.broadcasted_iota(jnp.int32, (N_EXPERTS, N_EXPERTS), 1)
    below = (ec < er).astype(BF16)
    tstart = _dot(below, jnp.broadcast_to(ntile, (N_EXPERTS, LANES)).astype(BF16))[:, 0:1]
    pos = tstart * tm + rank
    erank = _dot(below, selb)
    rows = []
    for k in range(TOP_K):
        hit = (sel > 0.0) & (erank == float(k))
        rows.append(jnp.sum(jnp.where(hit, pos, 0.0), axis=0, keepdims=True))
    rows += [jnp.zeros((1, t), F32)] * (8 - TOP_K)
    pos_ref[...] = jnp.concatenate(rows, axis=0).astype(jnp.int32)

    tau = lax.broadcasted_iota(jnp.int32, (N_EXPERTS, n_meta), 1).astype(F32)
    eidx = lax.broadcasted_iota(jnp.int32, (N_EXPERTS, n_meta), 0).astype(F32)
    te = jnp.sum(((tstart + ntile) <= tau).astype(F32), axis=0, keepdims=True)
    te = jnp.minimum(te, float(N_EXPERTS - 1))
    onehot = eidx == te
    cnt_t = jnp.sum(jnp.where(onehot, cnt, 0.0), axis=0, keepdims=True)
    ts_t = jnp.sum(jnp.where(onehot, tstart, 0.0), axis=0, keepdims=True)
    tr = jnp.clip(cnt_t - (tau[0:1] - ts_t) * tm, 0.0, tm)
    tf = jnp.where((tau[0:1] == ts_t) & (tr > 0.0), 1.0, 0.0)
    meta_ref[...] = jnp.concatenate([te, tr, tf] + [jnp.zeros((1, n_meta), F32)] * 5, axis=0).astype(jnp.int32)

    eye = (lax.broadcasted_iota(jnp.int32, (N_EXPERTS, LANES), 0)
           == lax.broadcasted_iota(jnp.int32, (N_EXPERTS, LANES), 1))
    as_row = lambda col: jnp.sum(jnp.where(eye, col, 0.0), axis=0, keepdims=True)
    emeta_ref[...] = jnp.concatenate([as_row(cnt), as_row(tstart), as_row(ntile)]
                                     + [jnp.zeros((1, LANES), F32)] * 5, axis=0).astype(jnp.int32)


def _plan(sel, n_meta):
    t = sel.shape[1]
    sd = jax.ShapeDtypeStruct
    return pl.pallas_call(
        functools.partial(_plan_kernel, n_meta=n_meta),
        out_shape=[sd((8, t), jnp.int32), sd((8, n_meta), jnp.int32), sd((8, LANES), jnp.int32)],
        compiler_params=pltpu.CompilerParams(vmem_limit_bytes=VMEM_LIMIT),
        name="plan",
    )(sel)


def _fill_slot_table(pos_ref, emeta_ref, tbl_ref, n_tokens):
    group = 32
    def pad_expert(e, carry):
        n_t = emeta_ref[2 * LANES + e]
        start = (emeta_ref[LANES + e] + n_t - 1) * MOE_TILE

        @pl.when(n_t > 0)
        def _():
            def put(i, c):
                for u in range(group):
                    tbl_ref[start + i * group + u] = n_tokens
                return c
            lax.fori_loop(0, MOE_TILE // group, put, 0)
        return carry
    lax.fori_loop(0, N_EXPERTS, pad_expert, 0)
    last = N_EXPERTS - 1
    def pad_tile(tile, carry):
        def put(i, c):
            for u in range(group):
                tbl_ref[tile * MOE_TILE + i * group + u] = n_tokens
            return c
        lax.fori_loop(0, MOE_TILE // group, put, 0)
        return carry
    lax.fori_loop(emeta_ref[LANES + last] + emeta_ref[2 * LANES + last], tbl_ref.shape[0] // MOE_TILE, pad_tile, 0)

    unroll = 8
    def scatter(i, carry):
        first = i * (8 * unroll)
        for u in range(unroll):
            for k in range(TOP_K):
                tbl_ref[pos_ref[first + (8 * u + k)]] = i * unroll + u
        return carry
    lax.fori_loop(0, n_tokens // unroll, scatter, 0)


def _experts_kernel(te_ref, tr_ref, pos_ref, emeta_ref, xc_ref, xs_ref, comb_ref, wgu_ref, wd_ref, acc_out,
                    tbl_ref, xbuf, acc, stage0, stage1, cst0, cst1, act0, act1, ybuf0, ybuf1,
                    wgu_b0, wgu_b1, wd_b0, wd_b1, sems):
    s = pl.program_id(0)
    n_tiles = te_ref.shape[0]
    tm = MOE_TILE
    nch = D_MODEL // LANES
    tile_at = lambda lag: jnp.clip(s - lag, 0, n_tiles - 1)
    t_g, t_1, t_2, t_3 = tile_at(0), tile_at(1), tile_at(2), tile_at(3)

    @pl.when(s == 0)
    def _():
        tc, ts = xc_ref.shape[0], xs_ref.shape[0]
        copies = (pltpu.make_async_copy(xc_ref, xbuf.at[pl.ds(0, tc)], sems.at[0]),
                  pltpu.make_async_copy(xs_ref, xbuf.at[pl.ds(tc, ts)], sems.at[1]))
        for cp in copies:
            cp.start()
        n_pad = xbuf.shape[0] - tc - ts
        xbuf[pl.ds(tc + ts, n_pad)] = jnp.zeros((n_pad,) + xbuf.shape[1:], BF16)
        for ref in (acc, stage0, stage1, cst0, cst1, act0, act1, ybuf0, ybuf1, wgu_b0, wgu_b1, wd_b0, wd_b1):
            ref[...] = jnp.zeros_like(ref)
        _fill_slot_table(pos_ref, emeta_ref, tbl_ref, tc + ts)
        for cp in copies:
            cp.wait()

    def gather(tile, stage, cst):
        base = tile * tm
        for j in range(tm):
            tok = tbl_ref[base + j]
            stage[pl.ds(j * nch, nch), :] = xbuf[tok].astype(F32)
            cst[pl.ds(j, 1), :] = comb_ref[pl.ds(tok, 1), :]

    def gate_up(tile, stage, cst, wgu_b, act):
        xb = jnp.concatenate([stage[pl.ds(cc, tm, stride=nch), :] for cc in range(nch)], axis=1).astype(BF16)
        gu = _dot(xb, wgu_b[...])
        lane = lax.broadcasted_iota(jnp.int32, (1, LANES), 1)
        w_col = jnp.sum(jnp.where(lane == te_ref[tile], cst[...], 0.0), axis=1, keepdims=True)
        act[...] = (_silu(gu[:, :EXPERT_DIM]) * gu[:, EXPERT_DIM:] * w_col).astype(BF16)

    def down(act, wd_b, ybuf):
        y = _dot(act[...], wd_b[...])
        for cc in range(nch):
            ybuf[cc * Y_PITCH:cc * Y_PITCH + tm, :] = y[:, cc * LANES:(cc + 1) * LANES]

    def scatter(tile, ybuf):
        base = tile * tm
        sc_n = 16
        for i in range(tm // sc_n):
            toks = [tbl_ref[base + i * sc_n + u] for u in range(sc_n)]
            olds = [acc[toks[u]] for u in range(sc_n)]
            news = [olds[u] + ybuf[pl.ds(i * sc_n + u, nch, stride=Y_PITCH), :] for u in range(sc_n)]
            for u in range(sc_n):
                acc[toks[u]] = news[u]

    busy = (tr_ref[t_g] + tr_ref[t_1] + tr_ref[t_2] + tr_ref[t_3]) > 0
    bufs = ((stage0, cst0, act0, ybuf0, wgu_b0, wd_b0), (stage1, cst1, act1, ybuf1, wgu_b1, wd_b1))
    for par in range(2):
        stage_p, cst_p, act_p, ybuf_p, wgu_p, wd_p = bufs[par]
        stage_q, cst_q, act_q, ybuf_q, wgu_q, wd_q = bufs[1 - par]

        @pl.when(busy & (s % 2 == par))
        def _():
            gather(t_g, stage_p, cst_p)
            wgu_p[...] = wgu_ref[0].astype(BF16)
            gate_up(t_1, stage_q, cst_q, wgu_q, act_q)
            wd_q[...] = wd_ref[0].astype(BF16)
            down(act_p, wd_p, ybuf_p)
            scatter(t_3, ybuf_q)

    @pl.when(s == pl.num_programs(0) - 1)
    def _():
        cp = pltpu.make_async_copy(acc, acc_out, sems.at[2])
        cp.start()
        cp.wait()


def _experts(te, tr, pos, emeta, xc, xs, comb, wgu, wd):
    n_tok = comb.shape[0]
    n_tiles = te.shape[0]
    tm = MOE_TILE
    nch = D_MODEL // LANES
    vm = pltpu.VMEM
    grid_spec = pltpu.PrefetchScalarGridSpec(
        num_scalar_prefetch=4,
        grid=(n_tiles + 3,),
        in_specs=[pl.BlockSpec(memory_space=pl.ANY), pl.BlockSpec(memory_space=pl.ANY),
                  pl.BlockSpec((n_tok, LANES), lambda s, *_: (0, 0), pipeline_mode=pl.Buffered(1)),
                  pl.BlockSpec((1, D_MODEL, 2 * EXPERT_DIM), lambda s, te, *_: (te[jnp.minimum(s, n_tiles - 1)], 0, 0)),
                  pl.BlockSpec((1, EXPERT_DIM, D_MODEL), lambda s, te, *_: (te[jnp.clip(s - 1, 0, n_tiles - 1)], 0, 0))],
        out_specs=pl.BlockSpec(memory_space=pl.ANY),
        scratch_shapes=[pltpu.SMEM((n_tiles * tm,), jnp.int32),
                        vm((n_tok, nch, LANES), BF16), vm((n_tok, nch, LANES), F32),
                        vm((tm * nch, LANES), F32), vm((tm * nch, LANES), F32),
                        vm((tm, LANES), F32), vm((tm, LANES), F32),
                        vm((tm, EXPERT_DIM), BF16), vm((tm, EXPERT_DIM), BF16),
                        vm((nch * Y_PITCH, LANES), F32), vm((nch * Y_PITCH, LANES), F32),
                        vm((D_MODEL, 2 * EXPERT_DIM), BF16), vm((D_MODEL, 2 * EXPERT_DIM), BF16),
                        vm((EXPERT_DIM, D_MODEL), BF16), vm((EXPERT_DIM, D_MODEL), BF16),
                        pltpu.SemaphoreType.DMA((3,))],
    )
    return pl.pallas_call(
        _experts_kernel,
        grid_spec=grid_spec,
        out_shape=jax.ShapeDtypeStruct((n_tok, nch, LANES), F32),
        compiler_params=pltpu.CompilerParams(dimension_semantics=("arbitrary",),
                                             vmem_limit_bytes=EXPERTS_VMEM_LIMIT),
        name="experts",
    )(te, tr, pos, emeta, xc, xs, comb, wgu, wd)


def _final_kernel(acc_ref, h_ref, x1_ref, mod_ref, wsg_ref, wsd_ref, gpo_ref, out_ref):
    routed = jnp.concatenate([acc_ref[:, cc, :] for cc in range(D_MODEL // LANES)], axis=1)
    gs = _dot(h_ref[...], wsg_ref[...])
    act = _silu(gs[:, :SHARED_DIM]) * gs[:, SHARED_DIM:]
    f = routed + _dot(act.astype(BF16), wsd_ref[...])
    out_ref[...] = x1_ref[...] + mod_ref[0][5:6] * _rmsnorm(f, gpo_ref[...])


def _final(acc, tile0, h2, x1, mod, mod_row, wsg, wsd, gpo):
    n = h2.shape[0]
    tm = FINAL_TILE
    const = lambda *shape: pl.BlockSpec(shape, lambda i: (0,) * len(shape))
    tok = lambda w: pl.BlockSpec((tm, w), lambda i: (i, 0))
    return pl.pallas_call(
        _final_kernel,
        grid=(n // tm,),
        in_specs=[pl.BlockSpec((tm, D_MODEL // LANES, LANES), lambda i: (i + tile0, 0, 0)),
                  tok(D_MODEL), tok(D_MODEL),
                  pl.BlockSpec((1, N_MOD, D_MODEL), lambda i: (mod_row(i), 0, 0)),
                  const(D_MODEL, 2 * SHARED_DIM), const(SHARED_DIM, D_MODEL), const(1, D_MODEL)],
        out_specs=tok(D_MODEL),
        out_shape=jax.ShapeDtypeStruct((n, D_MODEL), F32),
        compiler_params=_params("arbitrary"),
        name="final",
    )(acc, h2, x1, mod, wsg, wsd, gpo)


def _window_bounds(n, w):
    idx = np.arange(n)
    return np.clip(idx - w // 2, 0, n), np.clip(idx + w - w // 2, 0, n)


def _pool_operators(t, grid):
    mats, invs = [], []
    for w in POOL_WINDOWS:
        if grid:
            rlo, rhi = _window_bounds(t // GRID_W, w)
            clo, chi = _window_bounds(GRID_W, w)
            r = np.arange(t) // GRID_W
            c = np.arange(t) % GRID_W
            m = ((r[None, :] >= rlo[r][:, None]) & (r[None, :] < rhi[r][:, None])
                 & (c[None, :] >= clo[c][:, None]) & (c[None, :] < chi[c][:, None]))
            cnt = (rhi - rlo)[r] * (chi - clo)[c]
        else:
            lo, hi = _window_bounds(t, w)
            sidx = np.arange(t)
            m = (sidx[None, :] >= lo[:, None]) & (sidx[None, :] < hi[:, None])
            cnt = hi - lo
        mats.append(m.astype(np.float32))
        invs.append((1.0 / cnt.astype(np.float64)).astype(np.float32)[:, None])
    return jnp.asarray(np.stack(mats), BF16), jnp.asarray(np.stack(invs), F32)


def kernel(x_prompt, x_sample, state_C, state_n, state_m, c, c_ctx, w_ada, b_ada, g_pre_mix, w_in, b_gate,
           w_pool, pool_scale, w_out, g_post_mix, g_pre_ffn, w_router, b_router, w_expert_gu, w_expert_down,
           w_shared_gu, w_shared_down, g_post_ffn):
    b_ctx = x_prompt.shape[0]
    b_lat = x_sample.shape[0]
    nu = N_DIR * HEADS
    l = 0
    row = lambda a: a[l].reshape(1, -1).astype(F32)

    cvec = jnp.zeros((16, D_MODEL), F32).at[0].set(c_ctx.astype(F32)).at[1:1 + b_lat].set(c.astype(F32))
    mod = _mod_rows(cvec, w_ada[l], b_ada[l]).reshape(16, N_MOD, D_MODEL)

    w_in_l = w_in[l]
    p0 = POOL_WIDTH
    mw = MLSTM_WIDTH
    w_u, w_q, w_k, w_v, w_o = (w_in_l[:, lo:lo + 512] for lo in (0, p0, p0 + mw, p0 + 2 * mw, p0 + 3 * mw))
    wm = jnp.concatenate([w_u, w_k, w_o], axis=1).astype(BF16)
    wt = jnp.concatenate([w_q.T, w_v.T], axis=0).astype(BF16)
    wg_cols = w_in_l[:, p0 + 4 * mw:]
    wg = jnp.pad(jnp.concatenate(_split2(wg_cols), axis=1), ((0, 0), (0, LANES - 2 * GATE_COLS)))
    bg = b_gate[l].reshape(GATE_COLS).astype(F32)
    bgr = jnp.pad(bg, (0, LANES - GATE_COLS)).reshape(1, LANES)
    wpl = w_pool[l].astype(BF16)
    zg = jnp.zeros((POOL_GROUP_DIM, POOL_GROUP_DIM), BF16)
    wp = jnp.stack([jnp.block([[wpl[2 * p], zg], [zg, wpl[2 * p + 1]]]) for p in range(POOL_GROUPS // 2)])
    wo = w_out[l].astype(BF16)
    wr = jnp.pad(jnp.concatenate(_split3(w_router[l].astype(F32)), axis=1), ((0, 0), (0, 2 * LANES - 3 * N_EXPERTS)))
    wsg = w_shared_gu[l].astype(BF16)
    wsd = w_shared_down[l].astype(BF16)

    def mixer(x, mod_row, grid, s0, m0, emit_state):
        t = x.shape[1]
        u, k, o, qt, vt, gate, gate_t = _inproj(x.astype(F32), mod, mod_row, row(g_pre_mix), wm, wt, wg, bgr)
        outs = _mlstm(k, qt, vt, gate, gate_t, s0, m0, emit_state)
        hf, hb = outs[0], outs[1]
        pm, pinv = _pool_operators(t, grid)
        x1, h2, xp, lg = _outproj(x.astype(F32), u, hf, hb, o, mod, mod_row, pm, pinv, wp, row(pool_scale), wo,
                                  row(g_post_mix), row(g_pre_ffn), wr)
        return x1, h2, xp, lg, outs[2:]

    ctx_row = lambda i: 0
    lat_row = lambda i: i + 1
    x1c, h2c, xpc, lgc, (c_new, n_new, m_new) = mixer(x_prompt, ctx_row, False, None, None, True)
    s0 = jnp.concatenate(
        [jnp.swapaxes(state_C[:, l].reshape(b_lat, nu, HEAD_DIM, HEAD_DIM).astype(F32), -1, -2),
         jnp.broadcast_to(state_n[:, l].reshape(b_lat, nu, 1, HEAD_DIM).astype(F32),
                          (b_lat, nu, N_ROWS, HEAD_DIM))], axis=-2)
    m0 = jnp.broadcast_to(state_m[:, l].reshape(b_lat, nu, 1, 1).astype(F32), (b_lat, nu, 1, LANES))
    x1s, h2s, xps, lgs, _ = mixer(x_sample, lat_row, True, s0, m0, False)

    tc = b_ctx * x_prompt.shape[1]
    ts = b_lat * x_sample.shape[1]
    n_tok = tc + ts
    lg_all = jnp.concatenate([lgc.transpose(1, 0, 2).reshape(N_EXPERTS, tc),
                              lgs.transpose(1, 0, 2).reshape(N_EXPERTS, ts)], axis=1)
    comb, sel = _router(lg_all, b_router[l].astype(F32))
    n_tiles = n_tok * TOP_K // MOE_TILE + N_EXPERTS
    n_meta = -(-n_tiles // LANES) * LANES
    pos, meta, emeta = _plan(sel, n_meta)

    comb_tok = jnp.pad(comb.T, ((0, 8), (0, LANES - N_EXPERTS)))
    slab = (D_MODEL // LANES, LANES)
    acc = _experts(meta[0, :n_tiles], meta[1, :n_tiles],
                   pos.T.reshape(-1), emeta.reshape(-1),
                   xpc.reshape((tc,) + slab), xps.reshape((ts,) + slab), comb_tok,
                   w_expert_gu[l], w_expert_down[l])

    fin = functools.partial(_final, wsg=wsg, wsd=wsd, gpo=row(g_post_ffn))
    tiles_per_lat = x_sample.shape[1] // FINAL_TILE
    yc = fin(acc, 0, h2c.reshape(tc, D_MODEL), x1c.reshape(tc, D_MODEL), mod, ctx_row)
    ys = fin(acc, tc // FINAL_TILE, h2s.reshape(ts, D_MODEL), x1s.reshape(ts, D_MODEL), mod,
             lambda i: i // tiles_per_lat + 1)

    new_c = c_new.reshape(b_ctx, 1, N_DIR, HEADS, HEAD_DIM, HEAD_DIM)
    new_n = n_new.reshape(b_ctx, 1, N_DIR, HEADS, HEAD_DIM)
    new_m = m_new[..., 0].reshape(b_ctx, 1, N_DIR, HEADS)
    return (yc.reshape(x_prompt.shape), ys.reshape(x_sample.shape), new_c, new_n, new_m)
```

```python
import functools

import jax
import jax.numpy as jnp
import numpy as np
from jax import lax
from jax.experimental import pallas as pl
from jax.experimental.pallas import tpu as pltpu

F32 = jnp.float32
BF16 = jnp.bfloat16

D_MODEL = 1024
GRID_W = 64
POOL_WIDTH = 512
POOL_GROUPS = 4
POOL_GROUP_DIM = 128
POOL_WINDOWS = (2, 4, 8, 16)
HEADS = 4
HEAD_DIM = 128
MLSTM_WIDTH = HEADS * HEAD_DIM
N_DIR = 2
GATE_COLS = N_DIR * 2 * HEADS
N_EXPERTS = 64
TOP_K = 6
N_EXPERT_GROUPS = 8
GROUP_SIZE = N_EXPERTS // N_EXPERT_GROUPS
TOPK_GROUPS = 4
EXPERT_DIM = 256
SHARED_DIM = 256
ROUTED_SCALE = 2.5
N_MOD = 6
EPS = 1e-6
K_SCALE = HEAD_DIM ** -0.5

LANES = 128
CHUNK = 256
N_ROWS = 16
TOKEN_TILE = 256
FINAL_TILE = 512
MOE_TILE = 320
Y_PITCH = MOE_TILE + 8
VMEM_LIMIT = 56 * 1024 * 1024
EXPERTS_VMEM_LIMIT = 58 * 1024 * 1024


def _split3(x):
    p1 = x.astype(BF16)
    r1 = x - p1.astype(F32)
    p2 = r1.astype(BF16)
    p3 = (r1 - p2.astype(F32)).astype(BF16)
    return p1, p2, p3


def _split2(x):
    p1 = x.astype(BF16)
    p2 = (x - p1.astype(F32)).astype(BF16)
    return p1, p2


def _dot(a, b):
    return jnp.dot(a, b, preferred_element_type=F32)


def _dot_nt(a, b):
    return lax.dot_general(a, b, (((1,), (1,)), ((), ())), preferred_element_type=F32)


def _rmsnorm(x, g):
    return x * lax.rsqrt(jnp.mean(x * x, axis=-1, keepdims=True) + EPS) * g


def _silu(x):
    return x * jax.nn.sigmoid(x)


def _params(*sem):
    return pltpu.CompilerParams(dimension_semantics=sem, vmem_limit_bytes=VMEM_LIMIT)


def _mod_kernel(c_ref, w_ref, b_ref, o_ref):
    a = _silu(c_ref[...])
    a_stack = jnp.concatenate(_split3(a), axis=0)
    w1, w2 = _split2(w_ref[...])
    r1 = _dot(a_stack, w1)
    r2 = _dot(a_stack[:32], w2)
    o_ref[...] = (r1[0:16] + r1[16:32] + r1[32:48] + r2[0:16] + r2[16:32]) + b_ref[...]


def _mod_rows(cvec, w_ada, b_ada):
    n = N_MOD * D_MODEL
    tn = 1536
    return pl.pallas_call(
        _mod_kernel,
        grid=(n // tn,),
        in_specs=[pl.BlockSpec((16, D_MODEL), lambda j: (0, 0)),
                  pl.BlockSpec((D_MODEL, tn), lambda j: (0, j)),
                  pl.BlockSpec((1, tn), lambda j: (0, j))],
        out_specs=pl.BlockSpec((16, tn), lambda j: (0, j)),
        out_shape=jax.ShapeDtypeStruct((16, n), F32),
        compiler_params=_params("arbitrary"),
        name="mod",
    )(cvec, w_ada, b_ada.reshape(1, n))


def _inproj_kernel(x_ref, mod_ref, g_ref, wm_ref, wt_ref, wg_ref, bgr_ref,
                   u_ref, k_ref, o_ref, qt_ref, vt_ref, gate_ref, gatet_ref):
    x = x_ref[0]
    mod = mod_ref[0]
    h = _rmsnorm(x, g_ref[...]) * (1.0 + mod[1:2]) + mod[0:1]
    h1, h2, h3 = _split3(h)
    z = _dot(h1, wm_ref[...])
    u_ref[0] = z[:, 0:512].astype(BF16)
    k_ref[0] = (z[:, 512:1024] * K_SCALE).astype(BF16)
    o_ref[0] = z[:, 1024:1536].astype(BF16)
    zt = _dot_nt(wt_ref[...], h1)
    qt_ref[0] = zt[0:512].astype(BF16)
    vt_ref[0] = zt[512:1024].astype(BF16)
    tm = x.shape[0]
    r = _dot(jnp.concatenate([h1, h2, h3], axis=0), wg_ref[...])
    r12 = r[0:tm] + r[tm:2 * tm]
    gate = (r12 + r[2 * tm:]) + pltpu.roll(r12, LANES - GATE_COLS, axis=1) + bgr_ref[...]
    gate_ref[0] = gate
    gatet_ref[0] = gate.T[0:16]


def _inproj(x, mod, mod_row, g, wm, wt, wg, bgr):
    b, t, _ = x.shape
    tm = min(t, 2 * TOKEN_TILE)
    const = lambda *shape: pl.BlockSpec(shape, lambda i, j: (0,) * len(shape))
    tok = lambda w: pl.BlockSpec((1, tm, w), lambda i, j: (i, j, 0))
    tok_t = lambda r: pl.BlockSpec((1, r, tm), lambda i, j: (i, 0, j))
    sd = jax.ShapeDtypeStruct
    return pl.pallas_call(
        _inproj_kernel,
        grid=(b, t // tm),
        in_specs=[tok(D_MODEL),
                  pl.BlockSpec((1, N_MOD, D_MODEL), lambda i, j: (mod_row(i), 0, 0)),
                  const(1, D_MODEL), const(D_MODEL, 1536), const(1024, D_MODEL),
                  const(D_MODEL, LANES), const(1, LANES)],
        out_specs=[tok(512), tok(512), tok(512), tok_t(512), tok_t(512), tok(LANES), tok_t(16)],
        out_shape=[sd((b, t, 512), BF16), sd((b, t, 512), BF16), sd((b, t, 512), BF16),
                   sd((b, 512, t), BF16), sd((b, 512, t), BF16), sd((b, t, LANES), F32),
                   sd((b, 16, t), F32)],
        compiler_params=_params("arbitrary", "arbitrary"),
        name="inproj",
    )(x, mod, g, wm, wt, wg, bgr)


def _log_sigmoid(x):
    return jnp.minimum(x, 0.0) - jnp.log1p(jnp.exp(-jnp.abs(x)))


def _scan_unit(st, k, qt, vt, u_col, u_row, b_row, btot, mask, s_prev, m_prev, use_state):
    dh = HEAD_DIM
    n = st.shape[0]
    ub = jnp.where(mask, jnp.broadcast_to(u_col, (n, n)), -jnp.inf)
    z = jnp.maximum(m_prev, jnp.max(ub, axis=0, keepdims=True))
    p = (jnp.exp(ub - z) * st).astype(BF16)
    ones = jnp.ones((N_ROWS, n), BF16)
    tot = _dot(jnp.concatenate([vt, ones], axis=0), p)
    if use_state:
        tot = tot + jnp.exp(m_prev - z) * _dot(s_prev.astype(BF16), qt)
    floor = jnp.exp(-(b_row + z))
    h_t = tot[:dh] / jnp.maximum(jnp.abs(tot[dh:dh + 1]), floor)
    g_row = btot + u_row
    m_new = jnp.maximum(btot + m_prev, jnp.max(g_row, axis=-1, keepdims=True))
    w_row = jnp.exp(g_row - m_new)
    vw = jnp.concatenate([(vt.astype(F32) * w_row).astype(BF16),
                          jnp.broadcast_to(w_row, (N_ROWS, n)).astype(BF16)], axis=0)
    s_new = jnp.exp(btot + m_prev - m_new) * s_prev + _dot(vw, k)
    return h_t.T, s_new, m_new


def _mlstm_kernel(*refs, nc, zero_init, emit_state):
    it = iter(refs)
    fwd_refs = tuple(next(it) for _ in range(5))
    bwd_refs = tuple(next(it) for _ in range(5)) if nc > 1 else fwd_refs
    if not zero_init:
        s0_ref, m0_ref = next(it), next(it)
    h_refs = (next(it), next(it))
    if emit_state:
        c_out, n_out, m_out = next(it), next(it), next(it)
    s_scr, m_scr = next(it), next(it)

    j = pl.program_id(1)
    n = CHUNK
    dh = HEAD_DIM

    @pl.when(j == 0)
    def _():
        if zero_init:
            s_scr[...] = jnp.zeros_like(s_scr)
            m_scr[...] = jnp.zeros_like(m_scr)
        else:
            s_scr[...] = s0_ref[0]
            m_scr[...] = m0_ref[0]

    rows = lax.broadcasted_iota(jnp.int32, (n, n), 0)
    cols = lax.broadcasted_iota(jnp.int32, (n, n), 1)
    le = rows <= cols
    ge = rows >= cols
    tri_le = le.astype(BF16)
    tri_ge = ge.astype(BF16)
    use_state = not (zero_init and nc == 1)

    def gate_terms(d):
        g_ref, gt_ref = (fwd_refs, bwd_refs)[d][3:5]
        gate = g_ref[0]
        gate_t = gt_ref[0]
        lf = _log_sigmoid(gate)
        lf_t = _log_sigmoid(gate_t)
        tri_c, tri_r = (tri_ge, tri_le) if d == 0 else (tri_le, tri_ge)
        bc = _dot(tri_c, jnp.concatenate(_split3(lf), axis=1))
        b_cols = bc[:, 0:128] + bc[:, 128:256] + bc[:, 256:384]
        br = _dot(jnp.concatenate(_split3(lf_t), axis=0), tri_r)
        b_rows = br[0:16] + br[16:32] + br[32:48]
        return gate, gate_t, b_cols, b_rows, jnp.sum(lf_t, axis=-1, keepdims=True)

    terms = [gate_terms(0), gate_terms(1)]
    hs = ([], [])
    for hd in range(HEADS):
        hsl = slice(hd * dh, (hd + 1) * dh)
        st = None
        for d in range(N_DIR):
            k_ref, qt_ref, vt_ref = (fwd_refs, bwd_refs)[d][0:3]
            gate, gate_t, b_cols, b_rows, tot_rows = terms[d]
            ci = d * 8 + hd
            cf = d * 8 + 4 + hd
            unit = d * HEADS + hd
            k = k_ref[0, :, hsl]
            qt = qt_ref[0, hsl, :]
            if st is None or nc > 1:
                st = _dot(k, qt)
            mask = le if d == 0 else ge
            h, s_new, m_new = _scan_unit(
                st, k, qt, vt_ref[0, hsl, :],
                gate[:, ci:ci + 1] - b_cols[:, cf:cf + 1],
                gate_t[ci:ci + 1, :] - b_rows[cf:cf + 1, :],
                b_rows[cf:cf + 1, :], tot_rows[cf:cf + 1, :],
                mask, s_scr[unit], m_scr[unit][:, 0:1], use_state)
            s_scr[unit] = s_new
            m_scr[unit] = jnp.broadcast_to(m_new, (1, LANES))
            hs[d].append(h)
    for d in range(N_DIR):
        h_refs[d][0] = jnp.concatenate(hs[d], axis=1).astype(BF16)

    if emit_state:
        @pl.when(j == nc - 1)
        def _():
            for unit in range(N_DIR * HEADS):
                s = s_scr[unit]
                c_out[0, unit] = s[:dh].T
                n_out[0, unit] = s[dh:dh + 1]
                m_out[0, unit] = m_scr[unit]


def _mlstm(k, qt, vt, gate, gate_t, s0, m0, emit_state):
    b, t, _ = k.shape
    nc = t // CHUNK
    zero_init = s0 is None
    nu = N_DIR * HEADS
    fwd = lambda w: pl.BlockSpec((1, CHUNK, w), lambda i, j: (i, j, 0))
    bwd = lambda w: pl.BlockSpec((1, CHUNK, w), lambda i, j: (i, nc - 1 - j, 0))
    fwd_t = lambda r: pl.BlockSpec((1, r, CHUNK), lambda i, j: (i, 0, j))
    bwd_t = lambda r: pl.BlockSpec((1, r, CHUNK), lambda i, j: (i, 0, nc - 1 - j))
    args = [k, qt, vt, gate, gate_t]
    in_specs = [fwd(512), fwd_t(512), fwd_t(512), fwd(LANES), fwd_t(16)]
    if nc > 1:
        args += [k, qt, vt, gate, gate_t]
        in_specs += [bwd(512), bwd_t(512), bwd_t(512), bwd(LANES), bwd_t(16)]
    if not zero_init:
        args += [s0, m0]
        in_specs += [pl.BlockSpec((1, nu, HEAD_DIM + N_ROWS, HEAD_DIM), lambda i, j: (i, 0, 0, 0)),
                     pl.BlockSpec((1, nu, 1, LANES), lambda i, j: (i, 0, 0, 0))]
    sd = jax.ShapeDtypeStruct
    out_shape = [sd((b, t, 512), BF16), sd((b, t, 512), BF16)]
    out_specs = [fwd(512), bwd(512)]
    if emit_state:
        out_shape += [sd((b, nu, HEAD_DIM, HEAD_DIM), F32), sd((b, nu, 1, HEAD_DIM), F32),
                      sd((b, nu, 1, LANES), F32)]
        out_specs += [pl.BlockSpec((1, nu, HEAD_DIM, HEAD_DIM), lambda i, j: (i, 0, 0, 0)),
                      pl.BlockSpec((1, nu, 1, HEAD_DIM), lambda i, j: (i, 0, 0, 0)),
                      pl.BlockSpec((1, nu, 1, LANES), lambda i, j: (i, 0, 0, 0))]
    return pl.pallas_call(
        functools.partial(_mlstm_kernel, nc=nc, zero_init=zero_init, emit_state=emit_state),
        grid=(b, nc),
        in_specs=in_specs,
        out_specs=out_specs,
        out_shape=out_shape,
        scratch_shapes=[pltpu.VMEM((nu, HEAD_DIM + N_ROWS, HEAD_DIM), F32),
                        pltpu.VMEM((nu, 1, LANES), F32)],
        compiler_params=_params("arbitrary", "arbitrary"),
        name="mlstm",
    )(*args)


def _outproj_kernel(x_ref, u_ref, hf_ref, hb_ref, o_ref, mod_ref, pm_ref, pinv_ref, wp_ref, ps_ref,
                    wo_ref, gpm_ref, gpf_ref, wr_ref, x1_ref, h2_ref, xp_ref, lg_ref, slab_ref, *, tm):
    i = pl.program_id(1)
    x = x_ref[0]
    mod = mod_ref[0]
    row0 = pl.multiple_of(i * tm, tm)
    u_full = u_ref[0]
    u_tile = u_ref[0, pl.ds(row0, tm), :].astype(F32)
    diffs = []
    for g in range(POOL_GROUPS):
        sl = slice(g * POOL_GROUP_DIM, (g + 1) * POOL_GROUP_DIM)
        box = _dot(pm_ref[g], u_full[:, sl])
        diffs.append((box * pinv_ref[g] - u_tile[:, sl]).astype(BF16))
    yps = [_dot(jnp.concatenate(diffs[2 * p:2 * p + 2], axis=1), wp_ref[p]) for p in range(POOL_GROUPS // 2)]
    y_pool = jnp.concatenate(yps, axis=1) * ps_ref[...]
    hsum = hf_ref[0].astype(F32) + hb_ref[0].astype(F32)
    y_ml = jax.nn.sigmoid(o_ref[0].astype(F32)) * hsum
    mix = _dot(jnp.concatenate([y_pool, y_ml], axis=1).astype(BF16), wo_ref[...])
    x1 = x + mod[2:3] * _rmsnorm(mix, gpm_ref[...])
    x1_ref[0] = x1
    h2 = _rmsnorm(x1, gpf_ref[...]) * (1.0 + mod[4:5]) + mod[3:4]
    p1, p2, p3 = _split3(h2)
    h2_ref[0] = p1
    for cc in range(D_MODEL // LANES):
        slab_ref[:, cc, :] = h2[:, cc * LANES:(cc + 1) * LANES]
    xp_ref[0] = slab_ref[...].astype(BF16)
    r = _dot(jnp.concatenate([p1, p2, p3], axis=0), wr_ref[...])
    r12 = r[0:tm] + r[tm:2 * tm]
    ne = N_EXPERTS
    lg = (r12 + r[2 * tm:])[:, 0:ne] + r12[:, ne:2 * ne] + r[0:tm, 2 * ne:3 * ne]
    lg_ref[0] = jnp.concatenate([lg, jnp.zeros_like(lg)], axis=1).T[0:ne]


def _outproj(x, u, hf, hb, o, mod, mod_row, pm, pinv, wp, ps, wo, gpm, gpf, wr):
    b, t, _ = x.shape
    tm = min(t, 2 * TOKEN_TILE)
    const = lambda *shape: pl.BlockSpec(shape, lambda i, j: (0,) * len(shape))
    tok = lambda w: pl.BlockSpec((1, tm, w), lambda i, j: (i, j, 0))
    sd = jax.ShapeDtypeStruct
    return pl.pallas_call(
        functools.partial(_outproj_kernel, tm=tm),
        grid=(b, t // tm),
        in_specs=[tok(D_MODEL),
                  pl.BlockSpec((1, t, 512), lambda i, j: (i, 0, 0)),
                  tok(512), tok(512), tok(512),
                  pl.BlockSpec((1, N_MOD, D_MODEL), lambda i, j: (mod_row(i), 0, 0)),
                  pl.BlockSpec((POOL_GROUPS, tm, t), lambda i, j: (0, j, 0)),
                  pl.BlockSpec((POOL_GROUPS, tm, 1), lambda i, j: (0, j, 0)),
                  const(POOL_GROUPS // 2, 2 * POOL_GROUP_DIM, 2 * POOL_GROUP_DIM), const(1, POOL_WIDTH),
                  const(D_MODEL, D_MODEL), const(1, D_MODEL), const(1, D_MODEL),
                  const(D_MODEL, 2 * LANES)],
        out_specs=[tok(D_MODEL), tok(D_MODEL),
                   pl.BlockSpec((1, tm, D_MODEL // LANES, LANES), lambda i, j: (i, j, 0, 0)),
                   pl.BlockSpec((1, N_EXPERTS, tm), lambda i, j: (i, 0, j))],
        out_shape=[sd((b, t, D_MODEL), F32), sd((b, t, D_MODEL), BF16),
                   sd((b, t, D_MODEL // LANES, LANES), BF16), sd((b, N_EXPERTS, t), F32)],
        scratch_shapes=[pltpu.VMEM((tm, D_MODEL // LANES, LANES), F32)],
        compiler_params=_params("arbitrary", "arbitrary"),
        name="outproj",
    )(x, u, hf, hb, o, mod, pm, pinv, wp, ps, wo, gpm, gpf, wr)


def _router_kernel(lg_ref, br_ref, comb_ref, sel_ref):
    s = jax.nn.sigmoid(lg_ref[...])
    biased = s + br_ref[...]
    gidx = lax.broadcasted_iota(jnp.int32, s.shape, 0)
    jidx = lax.broadcasted_iota(jnp.int32, s.shape, 1)
    neg = -jnp.inf
    m1 = jnp.max(biased, axis=1, keepdims=True)
    i1 = jnp.min(jnp.where(biased == m1, jidx, GROUP_SIZE), axis=1, keepdims=True)
    m2 = jnp.max(jnp.where(jidx == i1, neg, biased), axis=1, keepdims=True)
    gscore = m1 + m2
    gi = lax.broadcasted_iota(jnp.int32, gscore.shape, 0)
    gmask = jnp.zeros(gscore.shape, F32)
    cur = gscore
    for _ in range(TOPK_GROUPS):
        mx = jnp.max(cur, axis=0, keepdims=True)
        ix = jnp.min(jnp.where(cur == mx, gi, N_EXPERT_GROUPS), axis=0, keepdims=True)
        hit = gi == ix
        gmask = jnp.where(hit, 1.0, gmask)
        cur = jnp.where(hit, neg, cur)
    cur = jnp.where(gmask > 0, biased, neg)
    eidx = gidx * GROUP_SIZE + jidx
    selmask = jnp.zeros(s.shape, F32)
    for _ in range(TOP_K):
        mx = jnp.max(jnp.max(cur, axis=1, keepdims=True), axis=0, keepdims=True)
        ix = jnp.where(cur == mx, eidx, N_EXPERTS)
        ix = jnp.min(jnp.min(ix, axis=1, keepdims=True), axis=0, keepdims=True)
        hit = eidx == ix
        selmask = jnp.where(hit, 1.0, selmask)
        cur = jnp.where(hit, neg, cur)
    sel = selmask * s
    tot = jnp.sum(jnp.sum(sel, axis=1, keepdims=True), axis=0, keepdims=True)
    comb_ref[...] = sel / tot * ROUTED_SCALE
    sel_ref[...] = selmask


def _router(logits_t, b_router):
    t = logits_t.shape[1]
    tl = 1024
    shp = (N_EXPERT_GROUPS, GROUP_SIZE, t)
    blk = pl.BlockSpec((N_EXPERT_GROUPS, GROUP_SIZE, tl), lambda j: (0, 0, j))
    comb, sel = pl.pallas_call(
        _router_kernel,
        grid=(t // tl,),
        in_specs=[blk, pl.BlockSpec((N_EXPERT_GROUPS, GROUP_SIZE, 1), lambda j: (0, 0, 0))],
        out_specs=[blk, blk],
        out_shape=[jax.ShapeDtypeStruct(shp, F32), jax.ShapeDtypeStruct(shp, F32)],
        compiler_params=_params("arbitrary"),
        name="router",
    )(logits_t.reshape(shp), b_router.reshape(N_EXPERT_GROUPS, GROUP_SIZE, 1))
    return comb.reshape(N_EXPERTS, t), sel.reshape(N_EXPERTS, t)


def _plan_kernel(sel_ref, pos_ref, meta_ref, emeta_ref, *, n_meta):
    t = sel_ref.shape[1]
    tm = float(MOE_TILE)
    sel = sel_ref[...]
    selb = sel.astype(BF16)
    blk = 256
    rr = lax.broadcasted_iota(jnp.int32, (blk, blk), 0)
    cc = lax.broadcasted_iota(jnp.int32, (blk, blk), 1)
    before = (rr < cc).astype(BF16)
    carry = jnp.zeros((N_EXPERTS, 1), F32)
    ranks = []
    for b in range(t // blk):
        sb = selb[:, b * blk:(b + 1) * blk]
        ranks.append(_dot(sb, before) + carry)
        carry = carry + jnp.sum(sel[:, b * blk:(b + 1) * blk], axis=1, keepdims=True)
    rank = jnp.concatenate(ranks, axis=1)
    cnt = carry
    ntile = jnp.floor((cnt + (tm - 0.5)) * (1.0 / tm))
    er = lax.broadcasted_iota(jnp.int32, (N_EXPERTS, N_EXPERTS), 0)
    ec = lax.broadcasted_iota(jnp.int32, (N_EXPERTS, N_EXPERTS), 1)
    below = (ec < er).astype(BF16)
    tstart = _dot(below, jnp.broadcast_to(ntile, (N_EXPERTS, LANES)).astype(BF16))[:, 0:1]
    pos = tstart * tm + rank
    erank = _dot(below, selb)
    rows = []
    for k in range(TOP_K):
        hit = (sel > 0.0) & (erank == float(k))
        rows.append(jnp.sum(jnp.where(hit, pos, 0.0), axis=0, keepdims=True))
    rows += [jnp.zeros((1, t), F32)] * (8 - TOP_K)
    pos_ref[...] = jnp.concatenate(rows, axis=0).astype(jnp.int32)

    tau = lax.broadcasted_iota(jnp.int32, (N_EXPERTS, n_meta), 1).astype(F32)
    eidx = lax.broadcasted_iota(jnp.int32, (N_EXPERTS, n_meta), 0).astype(F32)
    te = jnp.sum(((tstart + ntile) <= tau).astype(F32), axis=0, keepdims=True)
    te = jnp.minimum(te, float(N_EXPERTS - 1))
    onehot = eidx == te
    cnt_t = jnp.sum(jnp.where(onehot, cnt, 0.0), axis=0, keepdims=True)
    ts_t = jnp.sum(jnp.where(onehot, tstart, 0.0), axis=0, keepdims=True)
    tr = jnp.clip(cnt_t - (tau[0:1] - ts_t) * tm, 0.0, tm)
    tf = jnp.where((tau[0:1] == ts_t) & (tr > 0.0), 1.0, 0.0)
    meta_ref[...] = jnp.concatenate([te, tr, tf] + [jnp.zeros((1, n_meta), F32)] * 5, axis=0).astype(jnp.int32)

    eye = (lax.broadcasted_iota(jnp.int32, (N_EXPERTS, LANES), 0)
           == lax.broadcasted_iota(jnp.int32, (N_EXPERTS, LANES), 1))
    as_row = lambda col: jnp.sum(jnp.where(eye, col, 0.0), axis=0, keepdims=True)
    emeta_ref[...] = jnp.concatenate([as_row(cnt), as_row(tstart), as_row(ntile)]
                                     + [jnp.zeros((1, LANES), F32)] * 5, axis=0).astype(jnp.int32)


def _plan(sel, n_meta):
    t = sel.shape[1]
    sd = jax.ShapeDtypeStruct
    return pl.pallas_call(
        functools.partial(_plan_kernel, n_meta=n_meta),
        out_shape=[sd((8, t), jnp.int32), sd((8, n_meta), jnp.int32), sd((8, LANES), jnp.int32)],
        compiler_params=pltpu.CompilerParams(vmem_limit_bytes=VMEM_LIMIT),
        name="plan",
    )(sel)


def _fill_slot_table(pos_ref, emeta_ref, tbl_ref, n_tokens):
    group = 32
    def pad_expert(e, carry):
        n_t = emeta_ref[2 * LANES + e]
        start = (emeta_ref[LANES + e] + n_t - 1) * MOE_TILE

        @pl.when(n_t > 0)
        def _():
            def put(i, c):
                for u in range(group):
                    tbl_ref[start + i * group + u] = n_tokens
                return c
            lax.fori_loop(0, MOE_TILE // group, put, 0)
        return carry
    lax.fori_loop(0, N_EXPERTS, pad_expert, 0)
    last = N_EXPERTS - 1
    def pad_tile(tile, carry):
        def put(i, c):
            for u in range(group):
                tbl_ref[tile * MOE_TILE + i * group + u] = n_tokens
            return c
        lax.fori_loop(0, MOE_TILE // group, put, 0)
        return carry
    lax.fori_loop(emeta_ref[LANES + last] + emeta_ref[2 * LANES + last], tbl_ref.shape[0] // MOE_TILE, pad_tile, 0)

    unroll = 8
    def scatter(i, carry):
        first = i * (8 * unroll)
        for u in range(unroll):
            for k in range(TOP_K):
                tbl_ref[pos_ref[first + (8 * u + k)]] = i * unroll + u
        return carry
    lax.fori_loop(0, n_tokens // unroll, scatter, 0)


def _experts_kernel(te_ref, tr_ref, pos_ref, emeta_ref, xc_ref, xs_ref, comb_ref, wgu_ref, wd_ref, acc_out,
                    tbl_ref, xbuf, acc, stage0, stage1, cst0, cst1, act0, act1, ybuf0, ybuf1,
                    wgu_b0, wgu_b1, wd_b0, wd_b1, sems):
    s = pl.program_id(0)
    n_tiles = te_ref.shape[0]
    tm = MOE_TILE
    nch = D_MODEL // LANES
    tile_at = lambda lag: jnp.clip(s - lag, 0, n_tiles - 1)
    t_g, t_1, t_2, t_3 = tile_at(0), tile_at(1), tile_at(2), tile_at(3)

    @pl.when(s == 0)
    def _():
        tc, ts = xc_ref.shape[0], xs_ref.shape[0]
        copies = (pltpu.make_async_copy(xc_ref, xbuf.at[pl.ds(0, tc)], sems.at[0]),
                  pltpu.make_async_copy(xs_ref, xbuf.at[pl.ds(tc, ts)], sems.at[1]))
        for cp in copies:
            cp.start()
        n_pad = xbuf.shape[0] - tc - ts
        xbuf[pl.ds(tc + ts, n_pad)] = jnp.zeros((n_pad,) + xbuf.shape[1:], BF16)
        for ref in (acc, stage0, stage1, cst0, cst1, act0, act1, ybuf0, ybuf1, wgu_b0, wgu_b1, wd_b0, wd_b1):
            ref[...] = jnp.zeros_like(ref)
        _fill_slot_table(pos_ref, emeta_ref, tbl_ref, tc + ts)
        for cp in copies:
            cp.wait()

    def gather(tile, stage, cst):
        base = tile * tm
        for j in range(tm):
            tok = tbl_ref[base + j]
            stage[pl.ds(j * nch, nch), :] = xbuf[tok].astype(F32)
            cst[pl.ds(j, 1), :] = comb_ref[pl.ds(tok, 1), :]

    def gate_up(tile, stage, cst, wgu_b, act):
        xb = jnp.concatenate([stage[pl.ds(cc, tm, stride=nch), :] for cc in range(nch)], axis=1).astype(BF16)
        gu = _dot(xb, wgu_b[...])
        lane = lax.broadcasted_iota(jnp.int32, (1, LANES), 1)
        w_col = jnp.sum(jnp.where(lane == te_ref[tile], cst[...], 0.0), axis=1, keepdims=True)
        act[...] = (_silu(gu[:, :EXPERT_DIM]) * gu[:, EXPERT_DIM:] * w_col).astype(BF16)

    def down(act, wd_b, ybuf):
        y = _dot(act[...], wd_b[...])
        for cc in range(nch):
            ybuf[cc * Y_PITCH:cc * Y_PITCH + tm, :] = y[:, cc * LANES:(cc + 1) * LANES]

    def scatter(tile, ybuf):
        base = tile * tm
        sc_n = 16
        for i in range(tm // sc_n):
            toks = [tbl_ref[base + i * sc_n + u] for u in range(sc_n)]
            olds = [acc[toks[u]] for u in range(sc_n)]
            news = [olds[u] + ybuf[pl.ds(i * sc_n + u, nch, stride=Y_PITCH), :] for u in range(sc_n)]
            for u in range(sc_n):
                acc[toks[u]] = news[u]

    busy = (tr_ref[t_g] + tr_ref[t_1] + tr_ref[t_2] + tr_ref[t_3]) > 0
    bufs = ((stage0, cst0, act0, ybuf0, wgu_b0, wd_b0), (stage1, cst1, act1, ybuf1, wgu_b1, wd_b1))
    for par in range(2):
        stage_p, cst_p, act_p, ybuf_p, wgu_p, wd_p = bufs[par]
        stage_q, cst_q, act_q, ybuf_q, wgu_q, wd_q = bufs[1 - par]

        @pl.when(busy & (s % 2 == par))
        def _():
            gather(t_g, stage_p, cst_p)
            wgu_p[...] = wgu_ref[0].astype(BF16)
            gate_up(t_1, stage_q, cst_q, wgu_q, act_q)
            wd_q[...] = wd_ref[0].astype(BF16)
            down(act_p, wd_p, ybuf_p)
            scatter(t_3, ybuf_q)

    @pl.when(s == pl.num_programs(0) - 1)
    def _():
        cp = pltpu.make_async_copy(acc, acc_out, sems.at[2])
        cp.start()
        cp.wait()


def _experts(te, tr, pos, emeta, xc, xs, comb, wgu, wd):
    n_tok = comb.shape[0]
    n_tiles = te.shape[0]
    tm = MOE_TILE
    nch = D_MODEL // LANES
    vm = pltpu.VMEM
    grid_spec = pltpu.PrefetchScalarGridSpec(
        num_scalar_prefetch=4,
        grid=(n_tiles + 3,),
        in_specs=[pl.BlockSpec(memory_space=pl.ANY), pl.BlockSpec(memory_space=pl.ANY),
                  pl.BlockSpec((n_tok, LANES), lambda s, *_: (0, 0), pipeline_mode=pl.Buffered(1)),
                  pl.BlockSpec((1, D_MODEL, 2 * EXPERT_DIM), lambda s, te, *_: (te[jnp.minimum(s, n_tiles - 1)], 0, 0)),
                  pl.BlockSpec((1, EXPERT_DIM, D_MODEL), lambda s, te, *_: (te[jnp.clip(s - 1, 0, n_tiles - 1)], 0, 0))],
        out_specs=pl.BlockSpec(memory_space=pl.ANY),
        scratch_shapes=[pltpu.SMEM((n_tiles * tm,), jnp.int32),
                        vm((n_tok, nch, LANES), BF16), vm((n_tok, nch, LANES), F32),
                        vm((tm * nch, LANES), F32), vm((tm * nch, LANES), F32),
                        vm((tm, LANES), F32), vm((tm, LANES), F32),
                        vm((tm, EXPERT_DIM), BF16), vm((tm, EXPERT_DIM), BF16),
                        vm((nch * Y_PITCH, LANES), F32), vm((nch * Y_PITCH, LANES), F32),
                        vm((D_MODEL, 2 * EXPERT_DIM), BF16), vm((D_MODEL, 2 * EXPERT_DIM), BF16),
                        vm((EXPERT_DIM, D_MODEL), BF16), vm((EXPERT_DIM, D_MODEL), BF16),
                        pltpu.SemaphoreType.DMA((3,))],
    )
    return pl.pallas_call(
        _experts_kernel,
        grid_spec=grid_spec,
        out_shape=jax.ShapeDtypeStruct((n_tok, nch, LANES), F32),
        compiler_params=pltpu.CompilerParams(dimension_semantics=("arbitrary",),
                                             vmem_limit_bytes=EXPERTS_VMEM_LIMIT),
        name="experts",
    )(te, tr, pos, emeta, xc, xs, comb, wgu, wd)


def _final_kernel(acc_ref, h_ref, x1_ref, mod_ref, wsg_ref, wsd_ref, gpo_ref, out_ref):
    routed = jnp.concatenate([acc_ref[:, cc, :] for cc in range(D_MODEL // LANES)], axis=1)
    gs = _dot(h_ref[...], wsg_ref[...])
    act = _silu(gs[:, :SHARED_DIM]) * gs[:, SHARED_DIM:]
    f = routed + _dot(act.astype(BF16), wsd_ref[...])
    out_ref[...] = x1_ref[...] + mod_ref[0][5:6] * _rmsnorm(f, gpo_ref[...])


def _final(acc, tile0, h2, x1, mod, mod_row, wsg, wsd, gpo):
    n = h2.shape[0]
    tm = FINAL_TILE
    const = lambda *shape: pl.BlockSpec(shape, lambda i: (0,) * len(shape))
    tok = lambda w: pl.BlockSpec((tm, w), lambda i: (i, 0))
    return pl.pallas_call(
        _final_kernel,
        grid=(n // tm,),
        in_specs=[pl.BlockSpec((tm, D_MODEL // LANES, LANES), lambda i: (i + tile0, 0, 0)),
                  tok(D_MODEL), tok(D_MODEL),
                  pl.BlockSpec((1, N_MOD, D_MODEL), lambda i: (mod_row(i), 0, 0)),
                  const(D_MODEL, 2 * SHARED_DIM), const(SHARED_DIM, D_MODEL), const(1, D_MODEL)],
        out_specs=tok(D_MODEL),
        out_shape=jax.ShapeDtypeStruct((n, D_MODEL), F32),
        compiler_params=_params("arbitrary"),
        name="final",
    )(acc, h2, x1, mod, wsg, wsd, gpo)


def _window_bounds(n, w):
    idx = np.arange(n)
    return np.clip(idx - w // 2, 0, n), np.clip(idx + w - w // 2, 0, n)


def _pool_operators(t, grid):
    mats, invs = [], []
    for w in POOL_WINDOWS:
        if grid:
            rlo, rhi = _window_bounds(t // GRID_W, w)
            clo, chi = _window_bounds(GRID_W, w)
            r = np.arange(t) // GRID_W
            c = np.arange(t) % GRID_W
            m = ((r[None, :] >= rlo[r][:, None]) & (r[None, :] < rhi[r][:, None])
                 & (c[None, :] >= clo[c][:, None]) & (c[None, :] < chi[c][:, None]))
            cnt = (rhi - rlo)[r] * (chi - clo)[c]
        else:
            lo, hi = _window_bounds(t, w)
            sidx = np.arange(t)
            m = (sidx[None, :] >= lo[:, None]) & (sidx[None, :] < hi[:, None])
            cnt = hi - lo
        mats.append(m.astype(np.float32))
        invs.append((1.0 / cnt.astype(np.float64)).astype(np.float32)[:, None])
    return jnp.asarray(np.stack(mats), BF16), jnp.asarray(np.stack(invs), F32)


def kernel(x_prompt, x_sample, state_C, state_n, state_m, c, c_ctx, w_ada, b_ada, g_pre_mix, w_in, b_gate,
           w_pool, pool_scale, w_out, g_post_mix, g_pre_ffn, w_router, b_router, w_expert_gu, w_expert_down,
           w_shared_gu, w_shared_down, g_post_ffn):
    b_ctx = x_prompt.shape[0]
    b_lat = x_sample.shape[0]
    nu = N_DIR * HEADS
    l = 0
    row = lambda a: a[l].reshape(1, -1).astype(F32)

    cvec = jnp.zeros((16, D_MODEL), F32).at[0].set(c_ctx.astype(F32)).at[1:1 + b_lat].set(c.astype(F32))
    mod = _mod_rows(cvec, w_ada[l], b_ada[l]).reshape(16, N_MOD, D_MODEL)

    w_in_l = w_in[l]
    p0 = POOL_WIDTH
    mw = MLSTM_WIDTH
    w_u, w_q, w_k, w_v, w_o = (w_in_l[:, lo:lo + 512] for lo in (0, p0, p0 + mw, p0 + 2 * mw, p0 + 3 * mw))
    wm = jnp.concatenate([w_u, w_k, w_o], axis=1).astype(BF16)
    wt = jnp.concatenate([w_q.T, w_v.T], axis=0).astype(BF16)
    wg_cols = w_in_l[:, p0 + 4 * mw:]
    wg = jnp.pad(jnp.concatenate(_split2(wg_cols), axis=1), ((0, 0), (0, LANES - 2 * GATE_COLS)))
    bg = b_gate[l].reshape(GATE_COLS).astype(F32)
    bgr = jnp.pad(bg, (0, LANES - GATE_COLS)).reshape(1, LANES)
    wpl = w_pool[l].astype(BF16)
    zg = jnp.zeros((POOL_GROUP_DIM, POOL_GROUP_DIM), BF16)
    wp = jnp.stack([jnp.block([[wpl[2 * p], zg], [zg, wpl[2 * p + 1]]]) for p in range(POOL_GROUPS // 2)])
    wo = w_out[l].astype(BF16)
    wr = jnp.pad(jnp.concatenate(_split3(w_router[l].astype(F32)), axis=1), ((0, 0), (0, 2 * LANES - 3 * N_EXPERTS)))
    wsg = w_shared_gu[l].astype(BF16)
    wsd = w_shared_down[l].astype(BF16)

    def mixer(x, mod_row, grid, s0, m0, emit_state):
        t = x.shape[1]
        u, k, o, qt, vt, gate, gate_t = _inproj(x.astype(F32), mod, mod_row, row(g_pre_mix), wm, wt, wg, bgr)
        outs = _mlstm(k, qt, vt, gate, gate_t, s0, m0, emit_state)
        hf, hb = outs[0], outs[1]
        pm, pinv = _pool_operators(t, grid)
        x1, h2, xp, lg = _outproj(x.astype(F32), u, hf, hb, o, mod, mod_row, pm, pinv, wp, row(pool_scale), wo,
                                  row(g_post_mix), row(g_pre_ffn), wr)
        return x1, h2, xp, lg, outs[2:]

    ctx_row = lambda i: 0
    lat_row = lambda i: i + 1
    x1c, h2c, xpc, lgc, (c_new, n_new, m_new) = mixer(x_prompt, ctx_row, False, None, None, True)
    s0 = jnp.concatenate(
        [jnp.swapaxes(state_C[:, l].reshape(b_lat, nu, HEAD_DIM, HEAD_DIM).astype(F32), -1, -2),
         jnp.broadcast_to(state_n[:, l].reshape(b_lat, nu, 1, HEAD_DIM).astype(F32),
                          (b_lat, nu, N_ROWS, HEAD_DIM))], axis=-2)
    m0 = jnp.broadcast_to(state_m[:, l].reshape(b_lat, nu, 1, 1).astype(F32), (b_lat, nu, 1, LANES))
    x1s, h2s, xps, lgs, _ = mixer(x_sample, lat_row, True, s0, m0, False)

    tc = b_ctx * x_prompt.shape[1]
    ts = b_lat * x_sample.shape[1]
    n_tok = tc + ts
    lg_all = jnp.concatenate([lgc.transpose(1, 0, 2).reshape(N_EXPERTS, tc),
                              lgs.transpose(1, 0, 2).reshape(N_EXPERTS, ts)], axis=1)
    comb, sel = _router(lg_all, b_router[l].astype(F32))
    n_tiles = n_tok * TOP_K // MOE_TILE + N_EXPERTS
    n_meta = -(-n_tiles // LANES) * LANES
    pos, meta, emeta = _plan(sel, n_meta)

    comb_tok = jnp.pad(comb.T, ((0, 8), (0, LANES - N_EXPERTS)))
    slab = (D_MODEL // LANES, LANES)
    acc = _experts(meta[0, :n_tiles], meta[1, :n_tiles],
                   pos.T.reshape(-1), emeta.reshape(-1),
                   xpc.reshape((tc,) + slab), xps.reshape((ts,) + slab), comb_tok,
                   w_expert_gu[l], w_expert_down[l])

    fin = functools.partial(_final, wsg=wsg, wsd=wsd, gpo=row(g_post_ffn))
    tiles_per_lat = x_sample.shape[1] // FINAL_TILE
    yc = fin(acc, 0, h2c.reshape(tc, D_MODEL), x1c.reshape(tc, D_MODEL), mod, ctx_row)
    ys = fin(acc, tc // FINAL_TILE, h2s.reshape(ts, D_MODEL), x1s.reshape(ts, D_MODEL), mod,
             lambda i: i // tiles_per_lat + 1)

    new_c = c_new.reshape(b_ctx, 1, N_DIR, HEADS, HEAD_DIM, HEAD_DIM)
    new_n = n_new.reshape(b_ctx, 1, N_DIR, HEADS, HEAD_DIM)
    new_m = m_new[..., 0].reshape(b_ctx, 1, N_DIR, HEADS)
    return (yc.reshape(x_prompt.shape), ys.reshape(x_sample.shape), new_c, new_n, new_m)
```

```python
import functools

import jax
import jax.numpy as jnp
import numpy as np
from jax import lax
from jax.experimental import pallas as pl
from jax.experimental.pallas import tpu as pltpu

F32 = jnp.float32
BF16 = jnp.bfloat16

D_MODEL = 1024
GRID_W = 64
POOL_WIDTH = 512
POOL_GROUPS = 4
POOL_GROUP_DIM = 128
POOL_WINDOWS = (2, 4, 8, 16)
HEADS = 4
HEAD_DIM = 128
MLSTM_WIDTH = HEADS * HEAD_DIM
N_DIR = 2
GATE_COLS = N_DIR * 2 * HEADS
N_EXPERTS = 64
TOP_K = 6
N_EXPERT_GROUPS = 8
GROUP_SIZE = N_EXPERTS // N_EXPERT_GROUPS
TOPK_GROUPS = 4
EXPERT_DIM = 256
SHARED_DIM = 256
ROUTED_SCALE = 2.5
N_MOD = 6
EPS = 1e-6
K_SCALE = HEAD_DIM ** -0.5

LANES = 128
CHUNK = 256
N_ROWS = 16
TOKEN_TILE = 256
FINAL_TILE = 512
MOE_TILE = 320
Y_PITCH = MOE_TILE + 8
VMEM_LIMIT = 56 * 1024 * 1024
EXPERTS_VMEM_LIMIT = 58 * 1024 * 1024


def _split3(x):
    p1 = x.astype(BF16)
    r1 = x - p1.astype(F32)
    p2 = r1.astype(BF16)
    p3 = (r1 - p2.astype(F32)).astype(BF16)
    return p1, p2, p3


def _split2(x):
    p1 = x.astype(BF16)
    p2 = (x - p1.astype(F32)).astype(BF16)
    return p1, p2


def _dot(a, b):
    return jnp.dot(a, b, preferred_element_type=F32)


def _dot_nt(a, b):
    return lax.dot_general(a, b, (((1,), (1,)), ((), ())), preferred_element_type=F32)


def _rmsnorm(x, g):
    return x * lax.rsqrt(jnp.mean(x * x, axis=-1, keepdims=True) + EPS) * g


def _silu(x):
    return x * jax.nn.sigmoid(x)


def _token_rows_from_slabs(read_block, n_tok):
    nch = D_MODEL // LANES
    cols = [[] for _ in range(nch)]
    for g in range(n_tok // 8):
        blk = jnp.swapaxes(read_block(g), 0, 1)
        for cc in range(nch):
            cols[cc].append(blk[cc])
    return jnp.concatenate([jnp.concatenate(c, axis=0) for c in cols], axis=1)


def _params(*sem):
    return pltpu.CompilerParams(dimension_semantics=sem, vmem_limit_bytes=VMEM_LIMIT)


def _mod_kernel(c_ref, w_ref, b_ref, o_ref):
    a = _silu(c_ref[...])
    a_stack = jnp.concatenate(_split3(a), axis=0)
    w1, w2 = _split2(w_ref[...])
    r1 = _dot(a_stack, w1)
    r2 = _dot(a_stack[:32], w2)
    o_ref[...] = (r1[0:16] + r1[16:32] + r1[32:48] + r2[0:16] + r2[16:32]) + b_ref[...]


def _mod_rows(cvec, w_ada, b_ada):
    n = N_MOD * D_MODEL
    tn = 1536
    return pl.pallas_call(
        _mod_kernel,
        grid=(n // tn,),
        in_specs=[pl.BlockSpec((16, D_MODEL), lambda j: (0, 0)),
                  pl.BlockSpec((D_MODEL, tn), lambda j: (0, j)),
                  pl.BlockSpec((1, tn), lambda j: (0, j))],
        out_specs=pl.BlockSpec((16, tn), lambda j: (0, j)),
        out_shape=jax.ShapeDtypeStruct((16, n), F32),
        compiler_params=_params("arbitrary"),
        name="mod",
    )(cvec, w_ada, b_ada.reshape(1, n))


def _inproj_kernel(x_ref, mod_ref, g_ref, wm_ref, wt_ref, wg_ref, bgr_ref,
                   u_ref, k_ref, o_ref, qt_ref, vt_ref, gate_ref, gatet_ref):
    x = x_ref[0]
    mod = mod_ref[0]
    h = _rmsnorm(x, g_ref[...]) * (1.0 + mod[1:2]) + mod[0:1]
    h1, h2, h3 = _split3(h)
    z = _dot(h1, wm_ref[...])
    u_ref[0] = z[:, 0:512].astype(BF16)
    k_ref[0] = (z[:, 512:1024] * K_SCALE).astype(BF16)
    o_ref[0] = z[:, 1024:1536].astype(BF16)
    zt = _dot_nt(wt_ref[...], h1)
    qt_ref[0] = zt[0:512].astype(BF16)
    vt_ref[0] = zt[512:1024].astype(BF16)
    tm = x.shape[0]
    r = _dot(jnp.concatenate([h1, h2, h3], axis=0), wg_ref[...])
    r12 = r[0:tm] + r[tm:2 * tm]
    gate = (r12 + r[2 * tm:]) + pltpu.roll(r12, LANES - GATE_COLS, axis=1) + bgr_ref[...]
    gate_ref[0] = gate
    gatet_ref[0] = gate.T[0:16]


def _inproj(x, mod, mod_row, g, wm, wt, wg, bgr):
    b, t, _ = x.shape
    tm = min(t, 2 * TOKEN_TILE)
    const = lambda *shape: pl.BlockSpec(shape, lambda i, j: (0,) * len(shape))
    tok = lambda w: pl.BlockSpec((1, tm, w), lambda i, j: (i, j, 0))
    tok_t = lambda r: pl.BlockSpec((1, r, tm), lambda i, j: (i, 0, j))
    sd = jax.ShapeDtypeStruct
    return pl.pallas_call(
        _inproj_kernel,
        grid=(b, t // tm),
        in_specs=[tok(D_MODEL),
                  pl.BlockSpec((1, N_MOD, D_MODEL), lambda i, j: (mod_row(i), 0, 0)),
                  const(1, D_MODEL), const(D_MODEL, 1536), const(1024, D_MODEL),
                  const(D_MODEL, LANES), const(1, LANES)],
        out_specs=[tok(512), tok(512), tok(512), tok_t(512), tok_t(512), tok(LANES), tok_t(16)],
        out_shape=[sd((b, t, 512), BF16), sd((b, t, 512), BF16), sd((b, t, 512), BF16),
                   sd((b, 512, t), BF16), sd((b, 512, t), BF16), sd((b, t, LANES), F32),
                   sd((b, 16, t), F32)],
        compiler_params=_params("arbitrary", "arbitrary"),
        name="inproj",
    )(x, mod, g, wm, wt, wg, bgr)


def _log_sigmoid(x):
    return jnp.minimum(x, 0.0) - jnp.log1p(jnp.exp(-jnp.abs(x)))


def _scan_unit(st, k, qt, vt, u_col, u_row, b_row, btot, mask, s_prev, m_prev, use_state):
    dh = HEAD_DIM
    n = st.shape[0]
    ub = jnp.where(mask, jnp.broadcast_to(u_col, (n, n)), -jnp.inf)
    z = jnp.maximum(m_prev, jnp.max(ub, axis=0, keepdims=True))
    p = (jnp.exp(ub - z) * st).astype(BF16)
    ones = jnp.ones((N_ROWS, n), BF16)
    tot = _dot(jnp.concatenate([vt, ones], axis=0), p)
    if use_state:
        tot = tot + jnp.exp(m_prev - z) * _dot(s_prev.astype(BF16), qt)
    floor = jnp.exp(-(b_row + z))
    h_t = tot[:dh] / jnp.maximum(jnp.abs(tot[dh:dh + 1]), floor)
    g_row = btot + u_row
    m_new = jnp.maximum(btot + m_prev, jnp.max(g_row, axis=-1, keepdims=True))
    w_row = jnp.exp(g_row - m_new)
    vw = jnp.concatenate([(vt.astype(F32) * w_row).astype(BF16),
                          jnp.broadcast_to(w_row, (N_ROWS, n)).astype(BF16)], axis=0)
    s_new = jnp.exp(btot + m_prev - m_new) * s_prev + _dot(vw, k)
    return h_t.T, s_new, m_new


def _mlstm_kernel(*refs, nc, zero_init, emit_state):
    it = iter(refs)
    fwd_refs = tuple(next(it) for _ in range(5))
    bwd_refs = tuple(next(it) for _ in range(5)) if nc > 1 else fwd_refs
    if not zero_init:
        s0_ref, m0_ref = next(it), next(it)
    h_refs = (next(it), next(it))
    if emit_state:
        c_out, n_out, m_out = next(it), next(it), next(it)
    s_scr, m_scr = next(it), next(it)

    j = pl.program_id(1)
    n = CHUNK
    dh = HEAD_DIM

    @pl.when(j == 0)
    def _():
        if zero_init:
            s_scr[...] = jnp.zeros_like(s_scr)
            m_scr[...] = jnp.zeros_like(m_scr)
        else:
            s_scr[...] = s0_ref[0]
            m_scr[...] = m0_ref[0]

    rows = lax.broadcasted_iota(jnp.int32, (n, n), 0)
    cols = lax.broadcasted_iota(jnp.int32, (n, n), 1)
    le = rows <= cols
    ge = rows >= cols
    tri_le = le.astype(BF16)
    tri_ge = ge.astype(BF16)
    use_state = not (zero_init and nc == 1)

    def gate_terms(d):
        g_ref, gt_ref = (fwd_refs, bwd_refs)[d][3:5]
        gate = g_ref[0]
        gate_t = gt_ref[0]
        lf = _log_sigmoid(gate)
        lf_t = _log_sigmoid(gate_t)
        tri_c, tri_r = (tri_ge, tri_le) if d == 0 else (tri_le, tri_ge)
        bc = _dot(tri_c, jnp.concatenate(_split3(lf), axis=1))
        b_cols = bc[:, 0:128] + bc[:, 128:256] + bc[:, 256:384]
        br = _dot(jnp.concatenate(_split3(lf_t), axis=0), tri_r)
        b_rows = br[0:16] + br[16:32] + br[32:48]
        return gate, gate_t, b_cols, b_rows, jnp.sum(lf_t, axis=-1, keepdims=True)

    terms = [gate_terms(0), gate_terms(1)]
    hs = ([], [])
    for hd in range(HEADS):
        hsl = slice(hd * dh, (hd + 1) * dh)
        st = None
        for d in range(N_DIR):
            k_ref, qt_ref, vt_ref = (fwd_refs, bwd_refs)[d][0:3]
            gate, gate_t, b_cols, b_rows, tot_rows = terms[d]
            ci = d * 8 + hd
            cf = d * 8 + 4 + hd
            unit = d * HEADS + hd
            k = k_ref[0, :, hsl]
            qt = qt_ref[0, hsl, :]
            if st is None or nc > 1:
                st = _dot(k, qt)
            mask = le if d == 0 else ge
            h, s_new, m_new = _scan_unit(
                st, k, qt, vt_ref[0, hsl, :],
                gate[:, ci:ci + 1] - b_cols[:, cf:cf + 1],
                gate_t[ci:ci + 1, :] - b_rows[cf:cf + 1, :],
                b_rows[cf:cf + 1, :], tot_rows[cf:cf + 1, :],
                mask, s_scr[unit], m_scr[unit][:, 0:1], use_state)
            s_scr[unit] = s_new
            m_scr[unit] = jnp.broadcast_to(m_new, (1, LANES))
            hs[d].append(h)
    for d in range(N_DIR):
        h_refs[d][0] = jnp.concatenate(hs[d], axis=1).astype(BF16)

    if emit_state:
        @pl.when(j == nc - 1)
        def _():
            for unit in range(N_DIR * HEADS):
                s = s_scr[unit]
                c_out[0, unit] = s[:dh].T
                n_out[0, unit] = s[dh:dh + 1]
                m_out[0, unit] = m_scr[unit]


def _mlstm(k, qt, vt, gate, gate_t, s0, m0, emit_state):
    b, t, _ = k.shape
    nc = t // CHUNK
    zero_init = s0 is None
    nu = N_DIR * HEADS
    fwd = lambda w: pl.BlockSpec((1, CHUNK, w), lambda i, j: (i, j, 0))
    bwd = lambda w: pl.BlockSpec((1, CHUNK, w), lambda i, j: (i, nc - 1 - j, 0))
    fwd_t = lambda r: pl.BlockSpec((1, r, CHUNK), lambda i, j: (i, 0, j))
    bwd_t = lambda r: pl.BlockSpec((1, r, CHUNK), lambda i, j: (i, 0, nc - 1 - j))
    args = [k, qt, vt, gate, gate_t]
    in_specs = [fwd(512), fwd_t(512), fwd_t(512), fwd(LANES), fwd_t(16)]
    if nc > 1:
        args += [k, qt, vt, gate, gate_t]
        in_specs += [bwd(512), bwd_t(512), bwd_t(512), bwd(LANES), bwd_t(16)]
    if not zero_init:
        args += [s0, m0]
        in_specs += [pl.BlockSpec((1, nu, HEAD_DIM + N_ROWS, HEAD_DIM), lambda i, j: (i, 0, 0, 0)),
                     pl.BlockSpec((1, nu, 1, LANES), lambda i, j: (i, 0, 0, 0))]
    sd = jax.ShapeDtypeStruct
    out_shape = [sd((b, t, 512), BF16), sd((b, t, 512), BF16)]
    out_specs = [fwd(512), bwd(512)]
    if emit_state:
        out_shape += [sd((b, nu, HEAD_DIM, HEAD_DIM), F32), sd((b, nu, 1, HEAD_DIM), F32),
                      sd((b, nu, 1, LANES), F32)]
        out_specs += [pl.BlockSpec((1, nu, HEAD_DIM, HEAD_DIM), lambda i, j: (i, 0, 0, 0)),
                      pl.BlockSpec((1, nu, 1, HEAD_DIM), lambda i, j: (i, 0, 0, 0)),
                      pl.BlockSpec((1, nu, 1, LANES), lambda i, j: (i, 0, 0, 0))]
    return pl.pallas_call(
        functools.partial(_mlstm_kernel, nc=nc, zero_init=zero_init, emit_state=emit_state),
        grid=(b, nc),
        in_specs=in_specs,
        out_specs=out_specs,
        out_shape=out_shape,
        scratch_shapes=[pltpu.VMEM((nu, HEAD_DIM + N_ROWS, HEAD_DIM), F32),
                        pltpu.VMEM((nu, 1, LANES), F32)],
        compiler_params=_params("arbitrary", "arbitrary"),
        name="mlstm",
    )(*args)


def _outproj_kernel(x_ref, u_ref, hf_ref, hb_ref, o_ref, mod_ref, pm_ref, pinv_ref, wp_ref, ps_ref,
                    wo_ref, gpm_ref, gpf_ref, wr_ref, x1_ref, h2_ref, xp_ref, lg_ref, *, tm):
    i = pl.program_id(1)
    x = x_ref[0]
    mod = mod_ref[0]
    row0 = pl.multiple_of(i * tm, tm)
    u_full = u_ref[0]
    u_tile = u_ref[0, pl.ds(row0, tm), :].astype(F32)
    diffs = []
    for g in range(POOL_GROUPS):
        sl = slice(g * POOL_GROUP_DIM, (g + 1) * POOL_GROUP_DIM)
        box = _dot(pm_ref[g], u_full[:, sl])
        diffs.append((box * pinv_ref[g] - u_tile[:, sl]).astype(BF16))
    yps = [_dot(jnp.concatenate(diffs[2 * p:2 * p + 2], axis=1), wp_ref[p]) for p in range(POOL_GROUPS // 2)]
    y_pool = jnp.concatenate(yps, axis=1) * ps_ref[...]
    hsum = hf_ref[0].astype(F32) + hb_ref[0].astype(F32)
    y_ml = jax.nn.sigmoid(o_ref[0].astype(F32)) * hsum
    mix = _dot(jnp.concatenate([y_pool, y_ml], axis=1).astype(BF16), wo_ref[...])
    x1 = x + mod[2:3] * _rmsnorm(mix, gpm_ref[...])
    x1_ref[0] = x1
    h2 = _rmsnorm(x1, gpf_ref[...]) * (1.0 + mod[4:5]) + mod[3:4]
    p1, p2, p3 = _split3(h2)
    h2_ref[0] = p1
    nch = D_MODEL // LANES
    for g in range(tm // 8):
        cols = jnp.stack([h2[g * 8:(g + 1) * 8, cc * LANES:(cc + 1) * LANES] for cc in range(nch)], axis=0)
        xp_ref[0, g * 8:(g + 1) * 8] = jnp.swapaxes(cols, 0, 1).astype(BF16)
    r = _dot(jnp.concatenate([p1, p2, p3], axis=0), wr_ref[...])
    r12 = r[0:tm] + r[tm:2 * tm]
    ne = N_EXPERTS
    lg = (r12 + r[2 * tm:])[:, 0:ne] + r12[:, ne:2 * ne] + r[0:tm, 2 * ne:3 * ne]
    lg_ref[0] = jnp.concatenate([lg, jnp.zeros_like(lg)], axis=1).T[0:ne]


def _outproj(x, u, hf, hb, o, mod, mod_row, pm, pinv, wp, ps, wo, gpm, gpf, wr):
    b, t, _ = x.shape
    tm = min(t, 2 * TOKEN_TILE)
    const = lambda *shape: pl.BlockSpec(shape, lambda i, j: (0,) * len(shape))
    tok = lambda w: pl.BlockSpec((1, tm, w), lambda i, j: (i, j, 0))
    sd = jax.ShapeDtypeStruct
    return pl.pallas_call(
        functools.partial(_outproj_kernel, tm=tm),
        grid=(b, t // tm),
        in_specs=[tok(D_MODEL),
                  pl.BlockSpec((1, t, 512), lambda i, j: (i, 0, 0)),
                  tok(512), tok(512), tok(512),
                  pl.BlockSpec((1, N_MOD, D_MODEL), lambda i, j: (mod_row(i), 0, 0)),
                  pl.BlockSpec((POOL_GROUPS, tm, t), lambda i, j: (0, j, 0)),
                  pl.BlockSpec((POOL_GROUPS, tm, 1), lambda i, j: (0, j, 0)),
                  const(POOL_GROUPS // 2, 2 * POOL_GROUP_DIM, 2 * POOL_GROUP_DIM), const(1, POOL_WIDTH),
                  const(D_MODEL, D_MODEL), const(1, D_MODEL), const(1, D_MODEL),
                  const(D_MODEL, 2 * LANES)],
        out_specs=[tok(D_MODEL), tok(D_MODEL),
                   pl.BlockSpec((1, tm, D_MODEL // LANES, LANES), lambda i, j: (i, j, 0, 0)),
                   pl.BlockSpec((1, N_EXPERTS, tm), lambda i, j: (i, 0, j))],
        out_shape=[sd((b, t, D_MODEL), F32), sd((b, t, D_MODEL), BF16),
                   sd((b, t, D_MODEL // LANES, LANES), BF16), sd((b, N_EXPERTS, t), F32)],
        compiler_params=_params("arbitrary", "arbitrary"),
        name="outproj",
    )(x, u, hf, hb, o, mod, pm, pinv, wp, ps, wo, gpm, gpf, wr)


def _router_kernel(lg_ref, br_ref, comb_ref, sel_ref):
    s = jax.nn.sigmoid(lg_ref[...])
    biased = s + br_ref[...]
    gidx = lax.broadcasted_iota(jnp.int32, s.shape, 0)
    jidx = lax.broadcasted_iota(jnp.int32, s.shape, 1)
    neg = -jnp.inf
    m1 = jnp.max(biased, axis=1, keepdims=True)
    i1 = jnp.min(jnp.where(biased == m1, jidx, GROUP_SIZE), axis=1, keepdims=True)
    m2 = jnp.max(jnp.where(jidx == i1, neg, biased), axis=1, keepdims=True)
    gscore = m1 + m2
    gi = lax.broadcasted_iota(jnp.int32, gscore.shape, 0)
    gmask = jnp.zeros(gscore.shape, F32)
    cur = gscore
    for _ in range(TOPK_GROUPS):
        mx = jnp.max(cur, axis=0, keepdims=True)
        ix = jnp.min(jnp.where(cur == mx, gi, N_EXPERT_GROUPS), axis=0, keepdims=True)
        hit = gi == ix
        gmask = jnp.where(hit, 1.0, gmask)
        cur = jnp.where(hit, neg, cur)
    cur = jnp.where(gmask > 0, biased, neg)
    eidx = gidx * GROUP_SIZE + jidx
    selmask = jnp.zeros(s.shape, F32)
    for _ in range(TOP_K):
        mx = jnp.max(jnp.max(cur, axis=1, keepdims=True), axis=0, keepdims=True)
        ix = jnp.where(cur == mx, eidx, N_EXPERTS)
        ix = jnp.min(jnp.min(ix, axis=1, keepdims=True), axis=0, keepdims=True)
        hit = eidx == ix
        selmask = jnp.where(hit, 1.0, selmask)
        cur = jnp.where(hit, neg, cur)
    sel = selmask * s
    tot = jnp.sum(jnp.sum(sel, axis=1, keepdims=True), axis=0, keepdims=True)
    comb_ref[...] = sel / tot * ROUTED_SCALE
    sel_ref[...] = selmask


def _router(logits_t, b_router):
    t = logits_t.shape[1]
    tl = 1024
    shp = (N_EXPERT_GROUPS, GROUP_SIZE, t)
    blk = pl.BlockSpec((N_EXPERT_GROUPS, GROUP_SIZE, tl), lambda j: (0, 0, j))
    comb, sel = pl.pallas_call(
        _router_kernel,
        grid=(t // tl,),
        in_specs=[blk, pl.BlockSpec((N_EXPERT_GROUPS, GROUP_SIZE, 1), lambda j: (0, 0, 0))],
        out_specs=[blk, blk],
        out_shape=[jax.ShapeDtypeStruct(shp, F32), jax.ShapeDtypeStruct(shp, F32)],
        compiler_params=_params("arbitrary"),
        name="router",
    )(logits_t.reshape(shp), b_router.reshape(N_EXPERT_GROUPS, GROUP_SIZE, 1))
    return comb.reshape(N_EXPERTS, t), sel.reshape(N_EXPERTS, t)


def _plan_kernel(sel_ref, pos_ref, meta_ref, emeta_ref, *, n_meta):
    t = sel_ref.shape[1]
    tm = float(MOE_TILE)
    sel = sel_ref[...]
    selb = sel.astype(BF16)
    blk = 256
    rr = lax.broadcasted_iota(jnp.int32, (blk, blk), 0)
    cc = lax.broadcasted_iota(jnp.int32, (blk, blk), 1)
    before = (rr < cc).astype(BF16)
    carry = jnp.zeros((N_EXPERTS, 1), F32)
    ranks = []
    for b in range(t // blk):
        sb = selb[:, b * blk:(b + 1) * blk]
        ranks.append(_dot(sb, before) + carry)
        carry = carry + jnp.sum(sel[:, b * blk:(b + 1) * blk], axis=1, keepdims=True)
    rank = jnp.concatenate(ranks, axis=1)
    cnt = carry
    ntile = jnp.floor((cnt + (tm - 0.5)) * (1.0 / tm))
    er = lax.broadcasted_iota(jnp.int32, (N_EXPERTS, N_EXPERTS), 0)
    ec = lax.broadcasted_iota(jnp.int32, (N_EXPERTS, N_EXPERTS), 1)
    below = (ec < er).astype(BF16)
    tstart = _dot(below, jnp.broadcast_to(ntile, (N_EXPERTS, LANES)).astype(BF16))[:, 0:1]
    pos = tstart * tm + rank
    erank = _dot(below, selb)
    rows = []
    for k in range(TOP_K):
        hit = (sel > 0.0) & (erank == float(k))
        rows.append(jnp.sum(jnp.where(hit, pos, 0.0), axis=0, keepdims=True))
    rows += [jnp.zeros((1, t), F32)] * (8 - TOP_K)
    pos_ref[...] = jnp.concatenate(rows, axis=0).astype(jnp.int32)

    tau = lax.broadcasted_iota(jnp.int32, (N_EXPERTS, n_meta), 1).astype(F32)
    eidx = lax.broadcasted_iota(jnp.int32, (N_EXPERTS, n_meta), 0).astype(F32)
    te = jnp.sum(((tstart + ntile) <= tau).astype(F32), axis=0, keepdims=True)
    te = jnp.minimum(te, float(N_EXPERTS - 1))
    onehot = eidx == te
    cnt_t = jnp.sum(jnp.where(onehot, cnt, 0.0), axis=0, keepdims=True)
    ts_t = jnp.sum(jnp.where(onehot, tstart, 0.0), axis=0, keepdims=True)
    tr = jnp.clip(cnt_t - (tau[0:1] - ts_t) * tm, 0.0, tm)
    tf = jnp.where((tau[0:1] == ts_t) & (tr > 0.0), 1.0, 0.0)
    meta_ref[...] = jnp.concatenate([te, tr, tf] + [jnp.zeros((1, n_meta), F32)] * 5, axis=0).astype(jnp.int32)

    eye = (lax.broadcasted_iota(jnp.int32, (N_EXPERTS, LANES), 0)
           == lax.broadcasted_iota(jnp.int32, (N_EXPERTS, LANES), 1))
    as_row = lambda col: jnp.sum(jnp.where(eye, col, 0.0), axis=0, keepdims=True)
    emeta_ref[...] = jnp.concatenate([as_row(cnt), as_row(tstart), as_row(ntile)]
                                     + [jnp.zeros((1, LANES), F32)] * 5, axis=0).astype(jnp.int32)


def _plan(sel, n_meta):
    t = sel.shape[1]
    sd = jax.ShapeDtypeStruct
    return pl.pallas_call(
        functools.partial(_plan_kernel, n_meta=n_meta),
        out_shape=[sd((8, t), jnp.int32), sd((8, n_meta), jnp.int32), sd((8, LANES), jnp.int32)],
        compiler_params=pltpu.CompilerParams(vmem_limit_bytes=VMEM_LIMIT),
        name="plan",
    )(sel)


def _fill_slot_table(pos_ref, emeta_ref, tbl_ref, n_tokens):
    group = 32
    def pad_expert(e, carry):
        n_t = emeta_ref[2 * LANES + e]
        start = (emeta_ref[LANES + e] + n_t - 1) * MOE_TILE

        @pl.when(n_t > 0)
        def _():
            def put(i, c):
                for u in range(group):
                    tbl_ref[start + i * group + u] = n_tokens
                return c
            lax.fori_loop(0, MOE_TILE // group, put, 0)
        return carry
    lax.fori_loop(0, N_EXPERTS, pad_expert, 0)
    last = N_EXPERTS - 1
    def pad_tile(tile, carry):
        def put(i, c):
            for u in range(group):
                tbl_ref[tile * MOE_TILE + i * group + u] = n_tokens
            return c
        lax.fori_loop(0, MOE_TILE // group, put, 0)
        return carry
    lax.fori_loop(emeta_ref[LANES + last] + emeta_ref[2 * LANES + last], tbl_ref.shape[0] // MOE_TILE, pad_tile, 0)

    unroll = 8
    def scatter(i, carry):
        first = i * (8 * unroll)
        for u in range(unroll):
            for k in range(TOP_K):
                tbl_ref[pos_ref[first + (8 * u + k)]] = i * unroll + u
        return carry
    lax.fori_loop(0, n_tokens // unroll, scatter, 0)


def _experts_kernel(te_ref, tr_ref, pos_ref, emeta_ref, xc_ref, xs_ref, comb_ref, wgu_ref, wd_ref, acc_out,
                    tbl_ref, xbuf, acc, stage0, stage1, cst0, cst1, act0, act1, ybuf0, ybuf1,
                    wgu_b0, wgu_b1, wd_b0, wd_b1, sems):
    s = pl.program_id(0)
    n_tiles = te_ref.shape[0]
    tm = MOE_TILE
    nch = D_MODEL // LANES
    tile_at = lambda lag: jnp.clip(s - lag, 0, n_tiles - 1)
    t_g, t_1, t_2, t_3 = tile_at(0), tile_at(1), tile_at(2), tile_at(3)

    @pl.when(s == 0)
    def _():
        tc, ts = xc_ref.shape[0], xs_ref.shape[0]
        copies = (pltpu.make_async_copy(xc_ref, xbuf.at[pl.ds(0, tc)], sems.at[0]),
                  pltpu.make_async_copy(xs_ref, xbuf.at[pl.ds(tc, ts)], sems.at[1]))
        for cp in copies:
            cp.start()
        n_pad = xbuf.shape[0] - tc - ts
        xbuf[pl.ds(tc + ts, n_pad)] = jnp.zeros((n_pad,) + xbuf.shape[1:], BF16)
        for ref in (acc, stage0, stage1, cst0, cst1, act0, act1, ybuf0, ybuf1, wgu_b0, wgu_b1, wd_b0, wd_b1):
            ref[...] = jnp.zeros_like(ref)
        _fill_slot_table(pos_ref, emeta_ref, tbl_ref, tc + ts)
        for cp in copies:
            cp.wait()

    def gather(tile, stage, cst):
        base = tile * tm
        for j in range(tm):
            tok = tbl_ref[base + j]
            stage[pl.ds(j * nch, nch), :] = xbuf[tok].astype(F32)
            cst[pl.ds(j, 1), :] = comb_ref[pl.ds(tok, 1), :]

    def gate_up(tile, stage, cst, wgu_b, act):
        xb = _token_rows_from_slabs(
            lambda g: stage[g * 8 * nch:(g + 1) * 8 * nch, :].reshape(8, nch, LANES), tm).astype(BF16)
        gu = _dot(xb, wgu_b[...])
        lane = lax.broadcasted_iota(jnp.int32, (1, LANES), 1)
        w_col = jnp.sum(jnp.where(lane == te_ref[tile], cst[...], 0.0), axis=1, keepdims=True)
        act[...] = (_silu(gu[:, :EXPERT_DIM]) * gu[:, EXPERT_DIM:] * w_col).astype(BF16)

    def down(act, wd_b, ybuf):
        y = _dot(act[...], wd_b[...])
        for cc in range(nch):
            ybuf[cc * Y_PITCH:cc * Y_PITCH + tm, :] = y[:, cc * LANES:(cc + 1) * LANES]

    def scatter(tile, ybuf):
        base = tile * tm
        sc_n = 16
        for i in range(tm // sc_n):
            toks = [tbl_ref[base + i * sc_n + u] for u in range(sc_n)]
            olds = [acc[toks[u]] for u in range(sc_n)]
            news = [olds[u] + ybuf[pl.ds(i * sc_n + u, nch, stride=Y_PITCH), :] for u in range(sc_n)]
            for u in range(sc_n):
                acc[toks[u]] = news[u]

    busy = (tr_ref[t_g] + tr_ref[t_1] + tr_ref[t_2] + tr_ref[t_3]) > 0
    bufs = ((stage0, cst0, act0, ybuf0, wgu_b0, wd_b0), (stage1, cst1, act1, ybuf1, wgu_b1, wd_b1))
    for par in range(2):
        stage_p, cst_p, act_p, ybuf_p, wgu_p, wd_p = bufs[par]
        stage_q, cst_q, act_q, ybuf_q, wgu_q, wd_q = bufs[1 - par]

        @pl.when(busy & (s % 2 == par))
        def _():
            gather(t_g, stage_p, cst_p)
            wgu_p[...] = wgu_ref[0].astype(BF16)
            gate_up(t_1, stage_q, cst_q, wgu_q, act_q)
            wd_q[...] = wd_ref[0].astype(BF16)
            down(act_p, wd_p, ybuf_p)
            scatter(t_3, ybuf_q)

    @pl.when(s == pl.num_programs(0) - 1)
    def _():
        cp = pltpu.make_async_copy(acc, acc_out, sems.at[2])
        cp.start()
        cp.wait()


def _experts(te, tr, pos, emeta, xc, xs, comb, wgu, wd):
    n_tok = comb.shape[0]
    n_tiles = te.shape[0]
    tm = MOE_TILE
    nch = D_MODEL // LANES
    vm = pltpu.VMEM
    grid_spec = pltpu.PrefetchScalarGridSpec(
        num_scalar_prefetch=4,
        grid=(n_tiles + 3,),
        in_specs=[pl.BlockSpec(memory_space=pl.ANY), pl.BlockSpec(memory_space=pl.ANY),
                  pl.BlockSpec((n_tok, LANES), lambda s, *_: (0, 0), pipeline_mode=pl.Buffered(1)),
                  pl.BlockSpec((1, D_MODEL, 2 * EXPERT_DIM), lambda s, te, *_: (te[jnp.minimum(s, n_tiles - 1)], 0, 0)),
                  pl.BlockSpec((1, EXPERT_DIM, D_MODEL), lambda s, te, *_: (te[jnp.clip(s - 1, 0, n_tiles - 1)], 0, 0))],
        out_specs=pl.BlockSpec(memory_space=pl.ANY),
        scratch_shapes=[pltpu.SMEM((n_tiles * tm,), jnp.int32),
                        vm((n_tok, nch, LANES), BF16), vm((n_tok, nch, LANES), F32),
                        vm((tm * nch, LANES), F32), vm((tm * nch, LANES), F32),
                        vm((tm, LANES), F32), vm((tm, LANES), F32),
                        vm((tm, EXPERT_DIM), BF16), vm((tm, EXPERT_DIM), BF16),
                        vm((nch * Y_PITCH, LANES), F32), vm((nch * Y_PITCH, LANES), F32),
                        vm((D_MODEL, 2 * EXPERT_DIM), BF16), vm((D_MODEL, 2 * EXPERT_DIM), BF16),
                        vm((EXPERT_DIM, D_MODEL), BF16), vm((EXPERT_DIM, D_MODEL), BF16),
                        pltpu.SemaphoreType.DMA((3,))],
    )
    return pl.pallas_call(
        _experts_kernel,
        grid_spec=grid_spec,
        out_shape=jax.ShapeDtypeStruct((n_tok, nch, LANES), F32),
        compiler_params=pltpu.CompilerParams(dimension_semantics=("arbitrary",),
                                             vmem_limit_bytes=EXPERTS_VMEM_LIMIT),
        name="experts",
    )(te, tr, pos, emeta, xc, xs, comb, wgu, wd)


def _final_kernel(acc_ref, h_ref, x1_ref, mod_ref, wsg_ref, wsd_ref, gpo_ref, out_ref):
    routed = _token_rows_from_slabs(lambda g: acc_ref[g * 8:(g + 1) * 8], acc_ref.shape[0])
    gs = _dot(h_ref[...], wsg_ref[...])
    act = _silu(gs[:, :SHARED_DIM]) * gs[:, SHARED_DIM:]
    f = routed + _dot(act.astype(BF16), wsd_ref[...])
    out_ref[...] = x1_ref[...] + mod_ref[0][5:6] * _rmsnorm(f, gpo_ref[...])


def _final(acc, tile0, h2, x1, mod, mod_row, wsg, wsd, gpo):
    n = h2.shape[0]
    tm = FINAL_TILE
    const = lambda *shape: pl.BlockSpec(shape, lambda i: (0,) * len(shape))
    tok = lambda w: pl.BlockSpec((tm, w), lambda i: (i, 0))
    return pl.pallas_call(
        _final_kernel,
        grid=(n // tm,),
        in_specs=[pl.BlockSpec((tm, D_MODEL // LANES, LANES), lambda i: (i + tile0, 0, 0)),
                  tok(D_MODEL), tok(D_MODEL),
                  pl.BlockSpec((1, N_MOD, D_MODEL), lambda i: (mod_row(i), 0, 0)),
                  const(D_MODEL, 2 * SHARED_DIM), const(SHARED_DIM, D_MODEL), const(1, D_MODEL)],
        out_specs=tok(D_MODEL),
        out_shape=jax.ShapeDtypeStruct((n, D_MODEL), F32),
        compiler_params=_params("arbitrary"),
        name="final",
    )(acc, h2, x1, mod, wsg, wsd, gpo)


def _window_bounds(n, w):
    idx = np.arange(n)
    return np.clip(idx - w // 2, 0, n), np.clip(idx + w - w // 2, 0, n)


def _pool_operators(t, grid):
    mats, invs = [], []
    for w in POOL_WINDOWS:
        if grid:
            rlo, rhi = _window_bounds(t // GRID_W, w)
            clo, chi = _window_bounds(GRID_W, w)
            r = np.arange(t) // GRID_W
            c = np.arange(t) % GRID_W
            m = ((r[None, :] >= rlo[r][:, None]) & (r[None, :] < rhi[r][:, None])
                 & (c[None, :] >= clo[c][:, None]) & (c[None, :] < chi[c][:, None]))
            cnt = (rhi - rlo)[r] * (chi - clo)[c]
        else:
            lo, hi = _window_bounds(t, w)
            sidx = np.arange(t)
            m = (sidx[None, :] >= lo[:, None]) & (sidx[None, :] < hi[:, None])
            cnt = hi - lo
        mats.append(m.astype(np.float32))
        invs.append((1.0 / cnt.astype(np.float64)).astype(np.float32)[:, None])
    return jnp.asarray(np.stack(mats), BF16), jnp.asarray(np.stack(invs), F32)


def kernel(x_prompt, x_sample, state_C, state_n, state_m, c, c_ctx, w_ada, b_ada, g_pre_mix, w_in, b_gate,
           w_pool, pool_scale, w_out, g_post_mix, g_pre_ffn, w_router, b_router, w_expert_gu, w_expert_down,
           w_shared_gu, w_shared_down, g_post_ffn):
    b_ctx = x_prompt.shape[0]
    b_lat = x_sample.shape[0]
    nu = N_DIR * HEADS
    l = 0
    row = lambda a: a[l].reshape(1, -1).astype(F32)

    cvec = jnp.zeros((16, D_MODEL), F32).at[0].set(c_ctx.astype(F32)).at[1:1 + b_lat].set(c.astype(F32))
    mod = _mod_rows(cvec, w_ada[l], b_ada[l]).reshape(16, N_MOD, D_MODEL)

    w_in_l = w_in[l]
    p0 = POOL_WIDTH
    mw = MLSTM_WIDTH
    w_u, w_q, w_k, w_v, w_o = (w_in_l[:, lo:lo + 512] for lo in (0, p0, p0 + mw, p0 + 2 * mw, p0 + 3 * mw))
    wm = jnp.concatenate([w_u, w_k, w_o], axis=1).astype(BF16)
    wt = jnp.concatenate([w_q.T, w_v.T], axis=0).astype(BF16)
    wg_cols = w_in_l[:, p0 + 4 * mw:]
    wg = jnp.pad(jnp.concatenate(_split2(wg_cols), axis=1), ((0, 0), (0, LANES - 2 * GATE_COLS)))
    bg = b_gate[l].reshape(GATE_COLS).astype(F32)
    bgr = jnp.pad(bg, (0, LANES - GATE_COLS)).reshape(1, LANES)
    wpl = w_pool[l].astype(BF16)
    zg = jnp.zeros((POOL_GROUP_DIM, POOL_GROUP_DIM), BF16)
    wp = jnp.stack([jnp.block([[wpl[2 * p], zg], [zg, wpl[2 * p + 1]]]) for p in range(POOL_GROUPS // 2)])
    wo = w_out[l].astype(BF16)
    wr = jnp.pad(jnp.concatenate(_split3(w_router[l].astype(F32)), axis=1), ((0, 0), (0, 2 * LANES - 3 * N_EXPERTS)))
    wsg = w_shared_gu[l].astype(BF16)
    wsd = w_shared_down[l].astype(BF16)

    def mixer(x, mod_row, grid, s0, m0, emit_state):
        t = x.shape[1]
        u, k, o, qt, vt, gate, gate_t = _inproj(x.astype(F32), mod, mod_row, row(g_pre_mix), wm, wt, wg, bgr)
        outs = _mlstm(k, qt, vt, gate, gate_t, s0, m0, emit_state)
        hf, hb = outs[0], outs[1]
        pm, pinv = _pool_operators(t, grid)
        x1, h2, xp, lg = _outproj(x.astype(F32), u, hf, hb, o, mod, mod_row, pm, pinv, wp, row(pool_scale), wo,
                                  row(g_post_mix), row(g_pre_ffn), wr)
        return x1, h2, xp, lg, outs[2:]

    ctx_row = lambda i: 0
    lat_row = lambda i: i + 1
    x1c, h2c, xpc, lgc, (c_new, n_new, m_new) = mixer(x_prompt, ctx_row, False, None, None, True)
    s0 = jnp.concatenate(
        [jnp.swapaxes(state_C[:, l].reshape(b_lat, nu, HEAD_DIM, HEAD_DIM).astype(F32), -1, -2),
         jnp.broadcast_to(state_n[:, l].reshape(b_lat, nu, 1, HEAD_DIM).astype(F32),
                          (b_lat, nu, N_ROWS, HEAD_DIM))], axis=-2)
    m0 = jnp.broadcast_to(state_m[:, l].reshape(b_lat, nu, 1, 1).astype(F32), (b_lat, nu, 1, LANES))
    x1s, h2s, xps, lgs, _ = mixer(x_sample, lat_row, True, s0, m0, False)

    tc = b_ctx * x_prompt.shape[1]
    ts = b_lat * x_sample.shape[1]
    n_tok = tc + ts
    lg_all = jnp.concatenate([lgc.transpose(1, 0, 2).reshape(N_EXPERTS, tc),
                              lgs.transpose(1, 0, 2).reshape(N_EXPERTS, ts)], axis=1)
    comb, sel = _router(lg_all, b_router[l].astype(F32))
    n_tiles = n_tok * TOP_K // MOE_TILE + N_EXPERTS
    n_meta = -(-n_tiles // LANES) * LANES
    pos, meta, emeta = _plan(sel, n_meta)

    comb_tok = jnp.pad(comb.T, ((0, 8), (0, LANES - N_EXPERTS)))
    slab = (D_MODEL // LANES, LANES)
    acc = _experts(meta[0, :n_tiles], meta[1, :n_tiles],
                   pos.T.reshape(-1), emeta.reshape(-1),
                   xpc.reshape((tc,) + slab), xps.reshape((ts,) + slab), comb_tok,
                   w_expert_gu[l], w_expert_down[l])

    fin = functools.partial(_final, wsg=wsg, wsd=wsd, gpo=row(g_post_ffn))
    tiles_per_lat = x_sample.shape[1] // FINAL_TILE
    yc = fin(acc, 0, h2c.reshape(tc, D_MODEL), x1c.reshape(tc, D_MODEL), mod, ctx_row)
    ys = fin(acc, tc // FINAL_TILE, h2s.reshape(ts, D_MODEL), x1s.reshape(ts, D_MODEL), mod,
             lambda i: i // tiles_per_lat + 1)

    new_c = c_new.reshape(b_ctx, 1, N_DIR, HEADS, HEAD_DIM, HEAD_DIM)
    new_n = n_new.reshape(b_ctx, 1, N_DIR, HEADS, HEAD_DIM)
    new_m = m_new[..., 0].reshape(b_ctx, 1, N_DIR, HEADS)
    return (yc.reshape(x_prompt.shape), ys.reshape(x_sample.shape), new_c, new_n, new_m)
```

```python
import functools

import jax
import jax.numpy as jnp
import numpy as np
from jax import lax
from jax.experimental import pallas as pl
from jax.experimental.pallas import tpu as pltpu

F32 = jnp.float32
BF16 = jnp.bfloat16

D_MODEL = 1024
GRID_W = 64
POOL_WIDTH = 512
POOL_GROUPS = 4
POOL_GROUP_DIM = 128
POOL_WINDOWS = (2, 4, 8, 16)
HEADS = 4
HEAD_DIM = 128
MLSTM_WIDTH = HEADS * HEAD_DIM
N_DIR = 2
GATE_COLS = N_DIR * 2 * HEADS
N_EXPERTS = 64
TOP_K = 6
N_EXPERT_GROUPS = 8
GROUP_SIZE = N_EXPERTS // N_EXPERT_GROUPS
TOPK_GROUPS = 4
EXPERT_DIM = 256
SHARED_DIM = 256
ROUTED_SCALE = 2.5
N_MOD = 6
EPS = 1e-6
K_SCALE = HEAD_DIM ** -0.5

LANES = 128
CHUNK = 256
N_ROWS = 16
TOKEN_TILE = 256
FINAL_TILE = 512
MOE_TILE = 320
Y_PITCH = MOE_TILE + 8
VMEM_LIMIT = 56 * 1024 * 1024
EXPERTS_VMEM_LIMIT = 58 * 1024 * 1024


def _split3(x):
    p1 = x.astype(BF16)
    r1 = x - p1.astype(F32)
    p2 = r1.astype(BF16)
    p3 = (r1 - p2.astype(F32)).astype(BF16)
    return p1, p2, p3


def _split2(x):
    p1 = x.astype(BF16)
    p2 = (x - p1.astype(F32)).astype(BF16)
    return p1, p2


def _dot(a, b):
    return jnp.dot(a, b, preferred_element_type=F32)


def _dot_nt(a, b):
    return lax.dot_general(a, b, (((1,), (1,)), ((), ())), preferred_element_type=F32)


def _rmsnorm(x, g):
    return x * lax.rsqrt(jnp.mean(x * x, axis=-1, keepdims=True) + EPS) * g


def _silu(x):
    return x * jax.nn.sigmoid(x)


def _token_rows_from_slabs(read_block, n_tok):
    nch = D_MODEL // LANES
    cols = [[] for _ in range(nch)]
    for g in range(n_tok // 8):
        blk = jnp.swapaxes(read_block(g), 0, 1)
        for cc in range(nch):
            cols[cc].append(blk[cc])
    return jnp.concatenate([jnp.concatenate(c, axis=0) for c in cols], axis=1)


def _params(*sem):
    return pltpu.CompilerParams(dimension_semantics=sem, vmem_limit_bytes=VMEM_LIMIT)


def _mod_kernel(c_ref, w_ref, b_ref, o_ref):
    a = _silu(c_ref[...])
    a_stack = jnp.concatenate(_split3(a), axis=0)
    w1, w2 = _split2(w_ref[...])
    r1 = _dot(a_stack, w1)
    r2 = _dot(a_stack[:32], w2)
    o_ref[...] = (r1[0:16] + r1[16:32] + r1[32:48] + r2[0:16] + r2[16:32]) + b_ref[...]


def _mod_rows(cvec, w_ada, b_ada):
    n = N_MOD * D_MODEL
    tn = 1536
    return pl.pallas_call(
        _mod_kernel,
        grid=(n // tn,),
        in_specs=[pl.BlockSpec((16, D_MODEL), lambda j: (0, 0)),
                  pl.BlockSpec((D_MODEL, tn), lambda j: (0, j)),
                  pl.BlockSpec((1, tn), lambda j: (0, j))],
        out_specs=pl.BlockSpec((16, tn), lambda j: (0, j)),
        out_shape=jax.ShapeDtypeStruct((16, n), F32),
        compiler_params=_params("arbitrary"),
        name="mod",
    )(cvec, w_ada, b_ada.reshape(1, n))


def _inproj_kernel(x_ref, mod_ref, g_ref, wm_ref, wt_ref, wg_ref, bgr_ref,
                   u_ref, k_ref, o_ref, qt_ref, vt_ref, gate_ref, gatet_ref):
    x = x_ref[0]
    mod = mod_ref[0]
    h = _rmsnorm(x, g_ref[...]) * (1.0 + mod[1:2]) + mod[0:1]
    h1, h2, h3 = _split3(h)
    z = _dot(h1, wm_ref[...])
    u_ref[0] = z[:, 0:512].astype(BF16)
    k_ref[0] = (z[:, 512:1024] * K_SCALE).astype(BF16)
    o_ref[0] = z[:, 1024:1536].astype(BF16)
    zt = _dot_nt(wt_ref[...], h1)
    qt_ref[0] = zt[0:512].astype(BF16)
    vt_ref[0] = zt[512:1024].astype(BF16)
    tm = x.shape[0]
    r = _dot(jnp.concatenate([h1, h2, h3], axis=0), wg_ref[...])
    r12 = r[0:tm] + r[tm:2 * tm]
    gate = (r12 + r[2 * tm:]) + pltpu.roll(r12, LANES - GATE_COLS, axis=1) + bgr_ref[...]
    gate_ref[0] = gate
    gatet_ref[0] = gate.T[0:16]


def _inproj(x, mod, mod_row, g, wm, wt, wg, bgr):
    b, t, _ = x.shape
    tm = min(t, 2 * TOKEN_TILE)
    const = lambda *shape: pl.BlockSpec(shape, lambda i, j: (0,) * len(shape))
    tok = lambda w: pl.BlockSpec((1, tm, w), lambda i, j: (i, j, 0))
    tok_t = lambda r: pl.BlockSpec((1, r, tm), lambda i, j: (i, 0, j))
    sd = jax.ShapeDtypeStruct
    return pl.pallas_call(
        _inproj_kernel,
        grid=(b, t // tm),
        in_specs=[tok(D_MODEL),
                  pl.BlockSpec((1, N_MOD, D_MODEL), lambda i, j: (mod_row(i), 0, 0)),
                  const(1, D_MODEL), const(D_MODEL, 1536), const(1024, D_MODEL),
                  const(D_MODEL, LANES), const(1, LANES)],
        out_specs=[tok(512), tok(512), tok(512), tok_t(512), tok_t(512), tok(LANES), tok_t(16)],
        out_shape=[sd((b, t, 512), BF16), sd((b, t, 512), BF16), sd((b, t, 512), BF16),
                   sd((b, 512, t), BF16), sd((b, 512, t), BF16), sd((b, t, LANES), F32),
                   sd((b, 16, t), F32)],
        compiler_params=_params("arbitrary", "arbitrary"),
        name="inproj",
    )(x, mod, g, wm, wt, wg, bgr)


def _log_sigmoid(x):
    return jnp.minimum(x, 0.0) - jnp.log1p(jnp.exp(-jnp.abs(x)))


def _scan_unit(st, k, qt, vt, u_col, u_row, b_row, btot, mask, s_prev, m_prev, use_state):
    dh = HEAD_DIM
    n = st.shape[0]
    ub = jnp.where(mask, jnp.broadcast_to(u_col, (n, n)), -jnp.inf)
    z = jnp.maximum(m_prev, jnp.max(ub, axis=0, keepdims=True))
    p = (jnp.exp(ub - z) * st).astype(BF16)
    ones = jnp.ones((N_ROWS, n), BF16)
    tot = _dot(jnp.concatenate([vt, ones], axis=0), p)
    if use_state:
        tot = tot + jnp.exp(m_prev - z) * _dot(s_prev.astype(BF16), qt)
    floor = jnp.exp(-(b_row + z))
    h_t = tot[:dh] / jnp.maximum(jnp.abs(tot[dh:dh + 1]), floor)
    g_row = btot + u_row
    m_new = jnp.maximum(btot + m_prev, jnp.max(g_row, axis=-1, keepdims=True))
    w_row = jnp.exp(g_row - m_new)
    vw = jnp.concatenate([(vt.astype(F32) * w_row).astype(BF16),
                          jnp.broadcast_to(w_row, (N_ROWS, n)).astype(BF16)], axis=0)
    s_new = jnp.exp(btot + m_prev - m_new) * s_prev + _dot(vw, k)
    return h_t.T, s_new, m_new


def _mlstm_kernel(*refs, nc, zero_init, emit_state):
    it = iter(refs)
    fwd_refs = tuple(next(it) for _ in range(5))
    bwd_refs = tuple(next(it) for _ in range(5)) if nc > 1 else fwd_refs
    if not zero_init:
        s0_ref, m0_ref = next(it), next(it)
    h_refs = (next(it), next(it))
    if emit_state:
        c_out, n_out, m_out = next(it), next(it), next(it)
    s_scr, m_scr = next(it), next(it)

    j = pl.program_id(1)
    n = CHUNK
    dh = HEAD_DIM

    @pl.when(j == 0)
    def _():
        if zero_init:
            s_scr[...] = jnp.zeros_like(s_scr)
            m_scr[...] = jnp.zeros_like(m_scr)
        else:
            s_scr[...] = s0_ref[0]
            m_scr[...] = m0_ref[0]

    rows = lax.broadcasted_iota(jnp.int32, (n, n), 0)
    cols = lax.broadcasted_iota(jnp.int32, (n, n), 1)
    le = rows <= cols
    ge = rows >= cols
    tri_le = le.astype(BF16)
    tri_ge = ge.astype(BF16)
    use_state = not (zero_init and nc == 1)

    def gate_terms(d):
        g_ref, gt_ref = (fwd_refs, bwd_refs)[d][3:5]
        gate = g_ref[0]
        gate_t = gt_ref[0]
        lf = _log_sigmoid(gate)
        lf_t = _log_sigmoid(gate_t)
        tri_c, tri_r = (tri_ge, tri_le) if d == 0 else (tri_le, tri_ge)
        bc = _dot(tri_c, jnp.concatenate(_split3(lf), axis=1))
        b_cols = bc[:, 0:128] + bc[:, 128:256] + bc[:, 256:384]
        br = _dot(jnp.concatenate(_split3(lf_t), axis=0), tri_r)
        b_rows = br[0:16] + br[16:32] + br[32:48]
        return gate, gate_t, b_cols, b_rows, jnp.sum(lf_t, axis=-1, keepdims=True)

    terms = [gate_terms(0), gate_terms(1)]
    hs = ([], [])
    for hd in range(HEADS):
        hsl = slice(hd * dh, (hd + 1) * dh)
        st = None
        for d in range(N_DIR):
            k_ref, qt_ref, vt_ref = (fwd_refs, bwd_refs)[d][0:3]
            gate, gate_t, b_cols, b_rows, tot_rows = terms[d]
            ci = d * 8 + hd
            cf = d * 8 + 4 + hd
            unit = d * HEADS + hd
            k = k_ref[0, :, hsl]
            qt = qt_ref[0, hsl, :]
            if st is None or nc > 1:
                st = _dot(k, qt)
            mask = le if d == 0 else ge
            h, s_new, m_new = _scan_unit(
                st, k, qt, vt_ref[0, hsl, :],
                gate[:, ci:ci + 1] - b_cols[:, cf:cf + 1],
                gate_t[ci:ci + 1, :] - b_rows[cf:cf + 1, :],
                b_rows[cf:cf + 1, :], tot_rows[cf:cf + 1, :],
                mask, s_scr[unit], m_scr[unit][:, 0:1], use_state)
            s_scr[unit] = s_new
            m_scr[unit] = jnp.broadcast_to(m_new, (1, LANES))
            hs[d].append(h)
    for d in range(N_DIR):
        h_refs[d][0] = jnp.concatenate(hs[d], axis=1).astype(BF16)

    if emit_state:
        @pl.when(j == nc - 1)
        def _():
            for unit in range(N_DIR * HEADS):
                s = s_scr[unit]
                c_out[0, unit] = s[:dh].T
                n_out[0, unit] = s[dh:dh + 1]
                m_out[0, unit] = m_scr[unit]


def _mlstm(k, qt, vt, gate, gate_t, s0, m0, emit_state):
    b, t, _ = k.shape
    nc = t // CHUNK
    zero_init = s0 is None
    nu = N_DIR * HEADS
    fwd = lambda w: pl.BlockSpec((1, CHUNK, w), lambda i, j: (i, j, 0))
    bwd = lambda w: pl.BlockSpec((1, CHUNK, w), lambda i, j: (i, nc - 1 - j, 0))
    fwd_t = lambda r: pl.BlockSpec((1, r, CHUNK), lambda i, j: (i, 0, j))
    bwd_t = lambda r: pl.BlockSpec((1, r, CHUNK), lambda i, j: (i, 0, nc - 1 - j))
    args = [k, qt, vt, gate, gate_t]
    in_specs = [fwd(512), fwd_t(512), fwd_t(512), fwd(LANES), fwd_t(16)]
    if nc > 1:
        args += [k, qt, vt, gate, gate_t]
        in_specs += [bwd(512), bwd_t(512), bwd_t(512), bwd(LANES), bwd_t(16)]
    if not zero_init:
        args += [s0, m0]
        in_specs += [pl.BlockSpec((1, nu, HEAD_DIM + N_ROWS, HEAD_DIM), lambda i, j: (i, 0, 0, 0)),
                     pl.BlockSpec((1, nu, 1, LANES), lambda i, j: (i, 0, 0, 0))]
    sd = jax.ShapeDtypeStruct
    out_shape = [sd((b, t, 512), BF16), sd((b, t, 512), BF16)]
    out_specs = [fwd(512), bwd(512)]
    if emit_state:
        out_shape += [sd((b, nu, HEAD_DIM, HEAD_DIM), F32), sd((b, nu, 1, HEAD_DIM), F32),
                      sd((b, nu, 1, LANES), F32)]
        out_specs += [pl.BlockSpec((1, nu, HEAD_DIM, HEAD_DIM), lambda i, j: (i, 0, 0, 0)),
                      pl.BlockSpec((1, nu, 1, HEAD_DIM), lambda i, j: (i, 0, 0, 0)),
                      pl.BlockSpec((1, nu, 1, LANES), lambda i, j: (i, 0, 0, 0))]
    return pl.pallas_call(
        functools.partial(_mlstm_kernel, nc=nc, zero_init=zero_init, emit_state=emit_state),
        grid=(b, nc),
        in_specs=in_specs,
        out_specs=out_specs,
        out_shape=out_shape,
        scratch_shapes=[pltpu.VMEM((nu, HEAD_DIM + N_ROWS, HEAD_DIM), F32),
                        pltpu.VMEM((nu, 1, LANES), F32)],
        compiler_params=_params("arbitrary", "arbitrary"),
        name="mlstm",
    )(*args)


def _outproj_kernel(x_ref, u_ref, hf_ref, hb_ref, o_ref, mod_ref, pm_ref, pinv_ref, wp_ref, ps_ref,
                    wo_ref, gpm_ref, gpf_ref, wr_ref, x1_ref, h2_ref, xp_ref, lg_ref, *, tm):
    i = pl.program_id(1)
    x = x_ref[0]
    mod = mod_ref[0]
    row0 = pl.multiple_of(i * tm, tm)
    u_full = u_ref[0]
    u_tile = u_ref[0, pl.ds(row0, tm), :].astype(F32)
    diffs = []
    for g in range(POOL_GROUPS):
        sl = slice(g * POOL_GROUP_DIM, (g + 1) * POOL_GROUP_DIM)
        box = _dot(pm_ref[g], u_full[:, sl])
        diffs.append((box * pinv_ref[g] - u_tile[:, sl]).astype(BF16))
    yps = [_dot(jnp.concatenate(diffs[2 * p:2 * p + 2], axis=1), wp_ref[p]) for p in range(POOL_GROUPS // 2)]
    y_pool = jnp.concatenate(yps, axis=1) * ps_ref[...]
    hsum = hf_ref[0].astype(F32) + hb_ref[0].astype(F32)
    y_ml = jax.nn.sigmoid(o_ref[0].astype(F32)) * hsum
    mix = _dot(jnp.concatenate([y_pool, y_ml], axis=1).astype(BF16), wo_ref[...])
    x1 = x + mod[2:3] * _rmsnorm(mix, gpm_ref[...])
    x1_ref[0] = x1
    h2 = _rmsnorm(x1, gpf_ref[...]) * (1.0 + mod[4:5]) + mod[3:4]
    p1, p2, p3 = _split3(h2)
    h2_ref[0] = p1
    nch = D_MODEL // LANES
    for g in range(tm // 8):
        cols = jnp.stack([h2[g * 8:(g + 1) * 8, cc * LANES:(cc + 1) * LANES] for cc in range(nch)], axis=0)
        xp_ref[0, g * 8:(g + 1) * 8] = jnp.swapaxes(cols, 0, 1).astype(BF16)
    r = _dot(jnp.concatenate([p1, p2, p3], axis=0), wr_ref[...])
    r12 = r[0:tm] + r[tm:2 * tm]
    ne = N_EXPERTS
    lg = (r12 + r[2 * tm:])[:, 0:ne] + r12[:, ne:2 * ne] + r[0:tm, 2 * ne:3 * ne]
    lg_ref[...] = jnp.concatenate([lg, jnp.zeros_like(lg)], axis=1).T[0:ne]


def _outproj(x, u, hf, hb, o, mod, mod_row, pm, pinv, wp, ps, wo, gpm, gpf, wr):
    b, t, _ = x.shape
    tm = min(t, 2 * TOKEN_TILE)
    const = lambda *shape: pl.BlockSpec(shape, lambda i, j: (0,) * len(shape))
    tok = lambda w: pl.BlockSpec((1, tm, w), lambda i, j: (i, j, 0))
    sd = jax.ShapeDtypeStruct
    return pl.pallas_call(
        functools.partial(_outproj_kernel, tm=tm),
        grid=(b, t // tm),
        in_specs=[tok(D_MODEL),
                  pl.BlockSpec((1, t, 512), lambda i, j: (i, 0, 0)),
                  tok(512), tok(512), tok(512),
                  pl.BlockSpec((1, N_MOD, D_MODEL), lambda i, j: (mod_row(i), 0, 0)),
                  pl.BlockSpec((POOL_GROUPS, tm, t), lambda i, j: (0, j, 0)),
                  pl.BlockSpec((POOL_GROUPS, tm, 1), lambda i, j: (0, j, 0)),
                  const(POOL_GROUPS // 2, 2 * POOL_GROUP_DIM, 2 * POOL_GROUP_DIM), const(1, POOL_WIDTH),
                  const(D_MODEL, D_MODEL), const(1, D_MODEL), const(1, D_MODEL),
                  const(D_MODEL, 2 * LANES)],
        out_specs=[tok(D_MODEL), tok(D_MODEL),
                   pl.BlockSpec((1, tm, D_MODEL // LANES, LANES), lambda i, j: (i, j, 0, 0)),
                   pl.BlockSpec((N_EXPERTS, tm), lambda i, j: (0, i * (t // tm) + j))],
        out_shape=[sd((b, t, D_MODEL), F32), sd((b, t, D_MODEL), BF16),
                   sd((b, t, D_MODEL // LANES, LANES), BF16), sd((N_EXPERTS, b * t), F32)],
        compiler_params=_params("arbitrary", "arbitrary"),
        name="outproj",
    )(x, u, hf, hb, o, mod, pm, pinv, wp, ps, wo, gpm, gpf, wr)


def _router_kernel(lg_ref, br_ref, comb_ref, sel_ref, *, n_blocks):
    ng, gs = N_EXPERT_GROUPS, GROUP_SIZE
    neg = -jnp.inf
    lg = jnp.swapaxes(lg_ref[...], 0, 1)
    br = br_ref[...]
    s = [jax.nn.sigmoid(lg[j]) for j in range(gs)]
    biased = [s[j] + br[:, j, :] for j in range(gs)]
    fold = lambda op, xs: functools.reduce(op, xs)
    m1 = fold(jnp.maximum, biased)
    i1 = fold(jnp.minimum, [jnp.where(biased[j] == m1, j, gs) for j in range(gs)])
    m2 = fold(jnp.maximum, [jnp.where(i1 == j, neg, biased[j]) for j in range(gs)])
    cur = m1 + m2
    gi = lax.broadcasted_iota(jnp.int32, cur.shape, 0)
    gmask = jnp.zeros(cur.shape, F32)
    for _ in range(TOPK_GROUPS):
        mx = jnp.max(cur, axis=0, keepdims=True)
        ix = jnp.min(jnp.where(cur == mx, gi, ng), axis=0, keepdims=True)
        hit = gi == ix
        gmask = jnp.where(hit, 1.0, gmask)
        cur = jnp.where(hit, neg, cur)
    cand = [jnp.where(gmask > 0, biased[j], neg) for j in range(gs)]
    eidx = [gi * gs + j for j in range(gs)]
    selm = [jnp.zeros(cur.shape, F32) for _ in range(gs)]
    for _ in range(TOP_K):
        mx = jnp.max(fold(jnp.maximum, cand), axis=0, keepdims=True)
        ix = jnp.min(fold(jnp.minimum, [jnp.where(cand[j] == mx, eidx[j], N_EXPERTS) for j in range(gs)]),
                     axis=0, keepdims=True)
        for j in range(gs):
            hit = eidx[j] == ix
            selm[j] = jnp.where(hit, 1.0, selm[j])
            cand[j] = jnp.where(hit, neg, cand[j])
    sel = [selm[j] * s[j] for j in range(gs)]
    tot = jnp.sum(fold(jnp.add, sel), axis=0, keepdims=True)
    comb = [sel[j] / tot * ROUTED_SCALE for j in range(gs)]
    sel_ref[...] = jnp.swapaxes(jnp.stack(selm, axis=0), 0, 1)
    comb_e = jnp.swapaxes(jnp.stack(comb, axis=0), 0, 1).reshape(N_EXPERTS, -1)
    comb_t = jnp.concatenate([comb_e, jnp.zeros_like(comb_e)], axis=0).T
    comb_ref[...] = jnp.where(pl.program_id(0) < n_blocks, comb_t, 0.0)


def _router(logits_t, b_router):
    t = logits_t.shape[1]
    tl = 1024
    nb = t // tl
    shp = (N_EXPERT_GROUPS, GROUP_SIZE, t)
    blk = pl.BlockSpec((N_EXPERT_GROUPS, GROUP_SIZE, tl), lambda j: (0, 0, jnp.minimum(j, nb - 1)))
    comb, sel = pl.pallas_call(
        functools.partial(_router_kernel, n_blocks=nb),
        grid=(nb + 1,),
        in_specs=[blk, pl.BlockSpec((N_EXPERT_GROUPS, GROUP_SIZE, 1), lambda j: (0, 0, 0))],
        out_specs=[pl.BlockSpec((tl, LANES), lambda j: (j, 0)), blk],
        out_shape=[jax.ShapeDtypeStruct((t + tl, LANES), F32), jax.ShapeDtypeStruct(shp, F32)],
        compiler_params=_params("arbitrary"),
        name="router",
    )(logits_t.reshape(shp), b_router.reshape(N_EXPERT_GROUPS, GROUP_SIZE, 1))
    return comb, sel.reshape(N_EXPERTS, t)


def _plan_kernel(sel_ref, pos_ref, meta_ref, emeta_ref, *, n_meta):
    t = sel_ref.shape[1]
    tm = float(MOE_TILE)
    sel = sel_ref[...]
    selb = sel.astype(BF16)
    blk = 256
    rr = lax.broadcasted_iota(jnp.int32, (blk, blk), 0)
    cc = lax.broadcasted_iota(jnp.int32, (blk, blk), 1)
    before = (rr < cc).astype(BF16)
    carry = jnp.zeros((N_EXPERTS, 1), F32)
    ranks = []
    for b in range(t // blk):
        sb = selb[:, b * blk:(b + 1) * blk]
        ranks.append(_dot(sb, before) + carry)
        carry = carry + jnp.sum(sel[:, b * blk:(b + 1) * blk], axis=1, keepdims=True)
    rank = jnp.concatenate(ranks, axis=1)
    cnt = carry
    ntile = jnp.floor((cnt + (tm - 0.5)) * (1.0 / tm))
    er = lax.broadcasted_iota(jnp.int32, (N_EXPERTS, N_EXPERTS), 0)
    ec = lax.broadcasted_iota(jnp.int32, (N_EXPERTS, N_EXPERTS), 1)
    below = (ec < er).astype(BF16)
    tstart = _dot(below, jnp.broadcast_to(ntile, (N_EXPERTS, LANES)).astype(BF16))[:, 0:1]
    pos = tstart * tm + rank
    erank = _dot(below, selb)
    rows = []
    for k in range(TOP_K):
        hit = (sel > 0.0) & (erank == float(k))
        rows.append(jnp.sum(jnp.where(hit, pos, 0.0), axis=0, keepdims=True))
    rows += [jnp.zeros((1, t), F32)] * (8 - TOP_K)
    pos_ref[...] = jnp.concatenate(rows, axis=0).astype(jnp.int32)

    tau = lax.broadcasted_iota(jnp.int32, (N_EXPERTS, n_meta), 1).astype(F32)
    eidx = lax.broadcasted_iota(jnp.int32, (N_EXPERTS, n_meta), 0).astype(F32)
    te = jnp.sum(((tstart + ntile) <= tau).astype(F32), axis=0, keepdims=True)
    te = jnp.minimum(te, float(N_EXPERTS - 1))
    onehot = eidx == te
    cnt_t = jnp.sum(jnp.where(onehot, cnt, 0.0), axis=0, keepdims=True)
    ts_t = jnp.sum(jnp.where(onehot, tstart, 0.0), axis=0, keepdims=True)
    tr = jnp.clip(cnt_t - (tau[0:1] - ts_t) * tm, 0.0, tm)
    tf = jnp.where((tau[0:1] == ts_t) & (tr > 0.0), 1.0, 0.0)
    meta_ref[...] = jnp.concatenate([te, tr, tf] + [jnp.zeros((1, n_meta), F32)] * 5, axis=0).astype(jnp.int32)

    eye = (lax.broadcasted_iota(jnp.int32, (N_EXPERTS, LANES), 0)
           == lax.broadcasted_iota(jnp.int32, (N_EXPERTS, LANES), 1))
    as_row = lambda col: jnp.sum(jnp.where(eye, col, 0.0), axis=0, keepdims=True)
    emeta_ref[...] = jnp.concatenate([as_row(cnt), as_row(tstart), as_row(ntile)]
                                     + [jnp.zeros((1, LANES), F32)] * 5, axis=0).astype(jnp.int32)


def _plan(sel, n_meta):
    t = sel.shape[1]
    sd = jax.ShapeDtypeStruct
    return pl.pallas_call(
        functools.partial(_plan_kernel, n_meta=n_meta),
        out_shape=[sd((8, t), jnp.int32), sd((8, n_meta), jnp.int32), sd((8, LANES), jnp.int32)],
        compiler_params=pltpu.CompilerParams(vmem_limit_bytes=VMEM_LIMIT),
        name="plan",
    )(sel)


def _fill_slot_table(pos_ref, emeta_ref, tbl_ref, n_tokens):
    group = 32
    def pad_expert(e, carry):
        n_t = emeta_ref[2 * LANES + e]
        start = (emeta_ref[LANES + e] + n_t - 1) * MOE_TILE

        @pl.when(n_t > 0)
        def _():
            def put(i, c):
                for u in range(group):
                    tbl_ref[start + i * group + u] = n_tokens
                return c
            lax.fori_loop(0, MOE_TILE // group, put, 0)
        return carry
    lax.fori_loop(0, N_EXPERTS, pad_expert, 0)
    last = N_EXPERTS - 1
    def pad_tile(tile, carry):
        def put(i, c):
            for u in range(group):
                tbl_ref[tile * MOE_TILE + i * group + u] = n_tokens
            return c
        lax.fori_loop(0, MOE_TILE // group, put, 0)
        return carry
    lax.fori_loop(emeta_ref[LANES + last] + emeta_ref[2 * LANES + last], tbl_ref.shape[0] // MOE_TILE, pad_tile, 0)

    unroll = 8
    def scatter(i, carry):
        first = i * (8 * unroll)
        for u in range(unroll):
            for k in range(TOP_K):
                tbl_ref[pos_ref[first + (8 * u + k)]] = i * unroll + u
        return carry
    lax.fori_loop(0, n_tokens // unroll, scatter, 0)


def _experts_kernel(te_ref, tr_ref, pos_ref, emeta_ref, xc_ref, xs_ref, comb_ref, wgu_ref, wd_ref, acc_out,
                    tbl_ref, xbuf, acc, stage0, stage1, cst0, cst1, act0, act1, ybuf0, ybuf1,
                    wgu_b0, wgu_b1, wd_b0, wd_b1, sems):
    s = pl.program_id(0)
    n_tiles = te_ref.shape[0]
    tm = MOE_TILE
    nch = D_MODEL // LANES
    tile_at = lambda lag: jnp.clip(s - lag, 0, n_tiles - 1)
    t_g, t_1, t_2, t_3 = tile_at(0), tile_at(1), tile_at(2), tile_at(3)

    @pl.when(s == 0)
    def _():
        tc, ts = xc_ref.shape[0], xs_ref.shape[0]
        copies = (pltpu.make_async_copy(xc_ref, xbuf.at[pl.ds(0, tc)], sems.at[0]),
                  pltpu.make_async_copy(xs_ref, xbuf.at[pl.ds(tc, ts)], sems.at[1]))
        for cp in copies:
            cp.start()
        n_pad = xbuf.shape[0] - tc - ts
        xbuf[pl.ds(tc + ts, n_pad)] = jnp.zeros((n_pad,) + xbuf.shape[1:], BF16)
        for ref in (acc, stage0, stage1, cst0, cst1, act0, act1, ybuf0, ybuf1, wgu_b0, wgu_b1, wd_b0, wd_b1):
            ref[...] = jnp.zeros_like(ref)
        _fill_slot_table(pos_ref, emeta_ref, tbl_ref, tc + ts)
        for cp in copies:
            cp.wait()

    def gather(tile, stage, cst):
        base = tile * tm
        for j in range(tm):
            tok = tbl_ref[base + j]
            stage[pl.ds(j * nch, nch), :] = xbuf[tok].astype(F32)
            cst[pl.ds(j, 1), :] = comb_ref[pl.ds(tok, 1), :]

    def gate_up(tile, stage, cst, wgu_b, act):
        xb = _token_rows_from_slabs(
            lambda g: stage[g * 8 * nch:(g + 1) * 8 * nch, :].reshape(8, nch, LANES), tm).astype(BF16)
        gu = _dot(xb, wgu_b[...])
        lane = lax.broadcasted_iota(jnp.int32, (1, LANES), 1)
        w_col = jnp.sum(jnp.where(lane == te_ref[tile], cst[...], 0.0), axis=1, keepdims=True)
        act[...] = (_silu(gu[:, :EXPERT_DIM]) * gu[:, EXPERT_DIM:] * w_col).astype(BF16)

    def down(act, wd_b, ybuf):
        y = _dot(act[...], wd_b[...])
        for cc in range(nch):
            ybuf[cc * Y_PITCH:cc * Y_PITCH + tm, :] = y[:, cc * LANES:(cc + 1) * LANES]

    def scatter(tile, ybuf):
        base = tile * tm
        sc_n = 16
        for i in range(tm // sc_n):
            toks = [tbl_ref[base + i * sc_n + u] for u in range(sc_n)]
            olds = [acc[toks[u]] for u in range(sc_n)]
            news = [olds[u] + ybuf[pl.ds(i * sc_n + u, nch, stride=Y_PITCH), :] for u in range(sc_n)]
            for u in range(sc_n):
                acc[toks[u]] = news[u]

    busy = (tr_ref[t_g] + tr_ref[t_1] + tr_ref[t_2] + tr_ref[t_3]) > 0
    bufs = ((stage0, cst0, act0, ybuf0, wgu_b0, wd_b0), (stage1, cst1, act1, ybuf1, wgu_b1, wd_b1))
    for par in range(2):
        stage_p, cst_p, act_p, ybuf_p, wgu_p, wd_p = bufs[par]
        stage_q, cst_q, act_q, ybuf_q, wgu_q, wd_q = bufs[1 - par]

        @pl.when(busy & (s % 2 == par))
        def _():
            gather(t_g, stage_p, cst_p)
            wgu_p[...] = wgu_ref[0].astype(BF16)
            gate_up(t_1, stage_q, cst_q, wgu_q, act_q)
            wd_q[...] = wd_ref[0].astype(BF16)
            down(act_p, wd_p, ybuf_p)
            scatter(t_3, ybuf_q)

    @pl.when(s == pl.num_programs(0) - 1)
    def _():
        cp = pltpu.make_async_copy(acc, acc_out, sems.at[2])
        cp.start()
        cp.wait()


def _experts(te, tr, pos, emeta, xc, xs, comb, wgu, wd):
    n_tok = xc.shape[0] + xs.shape[0] + 8
    n_tiles = te.shape[0]
    tm = MOE_TILE
    nch = D_MODEL // LANES
    vm = pltpu.VMEM
    grid_spec = pltpu.PrefetchScalarGridSpec(
        num_scalar_prefetch=4,
        grid=(n_tiles + 3,),
        in_specs=[pl.BlockSpec(memory_space=pl.ANY), pl.BlockSpec(memory_space=pl.ANY),
                  pl.BlockSpec((n_tok, LANES), lambda s, *_: (0, 0), pipeline_mode=pl.Buffered(1)),
                  pl.BlockSpec((1, D_MODEL, 2 * EXPERT_DIM), lambda s, te, *_: (te[jnp.minimum(s, n_tiles - 1)], 0, 0)),
                  pl.BlockSpec((1, EXPERT_DIM, D_MODEL), lambda s, te, *_: (te[jnp.clip(s - 1, 0, n_tiles - 1)], 0, 0))],
        out_specs=pl.BlockSpec(memory_space=pl.ANY),
        scratch_shapes=[pltpu.SMEM((n_tiles * tm,), jnp.int32),
                        vm((n_tok, nch, LANES), BF16), vm((n_tok, nch, LANES), F32),
                        vm((tm * nch, LANES), F32), vm((tm * nch, LANES), F32),
                        vm((tm, LANES), F32), vm((tm, LANES), F32),
                        vm((tm, EXPERT_DIM), BF16), vm((tm, EXPERT_DIM), BF16),
                        vm((nch * Y_PITCH, LANES), F32), vm((nch * Y_PITCH, LANES), F32),
                        vm((D_MODEL, 2 * EXPERT_DIM), BF16), vm((D_MODEL, 2 * EXPERT_DIM), BF16),
                        vm((EXPERT_DIM, D_MODEL), BF16), vm((EXPERT_DIM, D_MODEL), BF16),
                        pltpu.SemaphoreType.DMA((3,))],
    )
    return pl.pallas_call(
        _experts_kernel,
        grid_spec=grid_spec,
        out_shape=jax.ShapeDtypeStruct((n_tok, nch, LANES), F32),
        compiler_params=pltpu.CompilerParams(dimension_semantics=("arbitrary",),
                                             vmem_limit_bytes=EXPERTS_VMEM_LIMIT),
        name="experts",
    )(te, tr, pos, emeta, xc, xs, comb, wgu, wd)


def _final_kernel(acc_ref, h_ref, x1_ref, mod_ref, wsg_ref, wsd_ref, gpo_ref, out_ref):
    routed = _token_rows_from_slabs(lambda g: acc_ref[g * 8:(g + 1) * 8], acc_ref.shape[0])
    gs = _dot(h_ref[...], wsg_ref[...])
    act = _silu(gs[:, :SHARED_DIM]) * gs[:, SHARED_DIM:]
    f = routed + _dot(act.astype(BF16), wsd_ref[...])
    out_ref[...] = x1_ref[...] + mod_ref[0][5:6] * _rmsnorm(f, gpo_ref[...])


def _final(acc, tile0, h2, x1, mod, mod_row, wsg, wsd, gpo):
    n = h2.shape[0]
    tm = FINAL_TILE
    const = lambda *shape: pl.BlockSpec(shape, lambda i: (0,) * len(shape))
    tok = lambda w: pl.BlockSpec((tm, w), lambda i: (i, 0))
    return pl.pallas_call(
        _final_kernel,
        grid=(n // tm,),
        in_specs=[pl.BlockSpec((tm, D_MODEL // LANES, LANES), lambda i: (i + tile0, 0, 0)),
                  tok(D_MODEL), tok(D_MODEL),
                  pl.BlockSpec((1, N_MOD, D_MODEL), lambda i: (mod_row(i), 0, 0)),
                  const(D_MODEL, 2 * SHARED_DIM), const(SHARED_DIM, D_MODEL), const(1, D_MODEL)],
        out_specs=tok(D_MODEL),
        out_shape=jax.ShapeDtypeStruct((n, D_MODEL), F32),
        compiler_params=_params("arbitrary"),
        name="final",
    )(acc, h2, x1, mod, wsg, wsd, gpo)


def _window_bounds(n, w):
    idx = np.arange(n)
    return np.clip(idx - w // 2, 0, n), np.clip(idx + w - w // 2, 0, n)


def _pool_operators(t, grid):
    mats, invs = [], []
    for w in POOL_WINDOWS:
        if grid:
            rlo, rhi = _window_bounds(t // GRID_W, w)
            clo, chi = _window_bounds(GRID_W, w)
            r = np.arange(t) // GRID_W
            c = np.arange(t) % GRID_W
            m = ((r[None, :] >= rlo[r][:, None]) & (r[None, :] < rhi[r][:, None])
                 & (c[None, :] >= clo[c][:, None]) & (c[None, :] < chi[c][:, None]))
            cnt = (rhi - rlo)[r] * (chi - clo)[c]
        else:
            lo, hi = _window_bounds(t, w)
            sidx = np.arange(t)
            m = (sidx[None, :] >= lo[:, None]) & (sidx[None, :] < hi[:, None])
            cnt = hi - lo
        mats.append(m.astype(np.float32))
        invs.append((1.0 / cnt.astype(np.float64)).astype(np.float32)[:, None])
    return jnp.asarray(np.stack(mats), BF16), jnp.asarray(np.stack(invs), F32)


def kernel(x_prompt, x_sample, state_C, state_n, state_m, c, c_ctx, w_ada, b_ada, g_pre_mix, w_in, b_gate,
           w_pool, pool_scale, w_out, g_post_mix, g_pre_ffn, w_router, b_router, w_expert_gu, w_expert_down,
           w_shared_gu, w_shared_down, g_post_ffn):
    b_ctx = x_prompt.shape[0]
    b_lat = x_sample.shape[0]
    nu = N_DIR * HEADS
    l = 0
    row = lambda a: a[l].reshape(1, -1).astype(F32)

    cvec = jnp.zeros((16, D_MODEL), F32).at[0].set(c_ctx.astype(F32)).at[1:1 + b_lat].set(c.astype(F32))
    mod = _mod_rows(cvec, w_ada[l], b_ada[l]).reshape(16, N_MOD, D_MODEL)

    w_in_l = w_in[l]
    p0 = POOL_WIDTH
    mw = MLSTM_WIDTH
    w_u, w_q, w_k, w_v, w_o = (w_in_l[:, lo:lo + 512] for lo in (0, p0, p0 + mw, p0 + 2 * mw, p0 + 3 * mw))
    wm = jnp.concatenate([w_u, w_k, w_o], axis=1).astype(BF16)
    wt = jnp.concatenate([w_q.T, w_v.T], axis=0).astype(BF16)
    wg_cols = w_in_l[:, p0 + 4 * mw:]
    wg = jnp.pad(jnp.concatenate(_split2(wg_cols), axis=1), ((0, 0), (0, LANES - 2 * GATE_COLS)))
    bg = b_gate[l].reshape(GATE_COLS).astype(F32)
    bgr = jnp.pad(bg, (0, LANES - GATE_COLS)).reshape(1, LANES)
    wpl = w_pool[l].astype(BF16)
    zg = jnp.zeros((POOL_GROUP_DIM, POOL_GROUP_DIM), BF16)
    wp = jnp.stack([jnp.block([[wpl[2 * p], zg], [zg, wpl[2 * p + 1]]]) for p in range(POOL_GROUPS // 2)])
    wo = w_out[l].astype(BF16)
    wr = jnp.pad(jnp.concatenate(_split3(w_router[l].astype(F32)), axis=1), ((0, 0), (0, 2 * LANES - 3 * N_EXPERTS)))
    wsg = w_shared_gu[l].astype(BF16)
    wsd = w_shared_down[l].astype(BF16)

    def mixer(x, mod_row, grid, s0, m0, emit_state):
        t = x.shape[1]
        u, k, o, qt, vt, gate, gate_t = _inproj(x.astype(F32), mod, mod_row, row(g_pre_mix), wm, wt, wg, bgr)
        outs = _mlstm(k, qt, vt, gate, gate_t, s0, m0, emit_state)
        hf, hb = outs[0], outs[1]
        pm, pinv = _pool_operators(t, grid)
        x1, h2, xp, lg = _outproj(x.astype(F32), u, hf, hb, o, mod, mod_row, pm, pinv, wp, row(pool_scale), wo,
                                  row(g_post_mix), row(g_pre_ffn), wr)
        return x1, h2, xp, lg, outs[2:]

    ctx_row = lambda i: 0
    lat_row = lambda i: i + 1
    x1c, h2c, xpc, lgc, (c_new, n_new, m_new) = mixer(x_prompt, ctx_row, False, None, None, True)
    s0 = jnp.concatenate(
        [jnp.swapaxes(state_C[:, l].reshape(b_lat, nu, HEAD_DIM, HEAD_DIM).astype(F32), -1, -2),
         jnp.broadcast_to(state_n[:, l].reshape(b_lat, nu, 1, HEAD_DIM).astype(F32),
                          (b_lat, nu, N_ROWS, HEAD_DIM))], axis=-2)
    m0 = jnp.broadcast_to(state_m[:, l].reshape(b_lat, nu, 1, 1).astype(F32), (b_lat, nu, 1, LANES))
    x1s, h2s, xps, lgs, _ = mixer(x_sample, lat_row, True, s0, m0, False)

    tc = b_ctx * x_prompt.shape[1]
    ts = b_lat * x_sample.shape[1]
    n_tok = tc + ts
    lg_all = jnp.concatenate([lgc, lgs], axis=1)
    comb_tok, sel = _router(lg_all, b_router[l].astype(F32))
    n_tiles = n_tok * TOP_K // MOE_TILE + N_EXPERTS
    n_meta = -(-n_tiles // LANES) * LANES
    pos, meta, emeta = _plan(sel, n_meta)

    slab = (D_MODEL // LANES, LANES)
    acc = _experts(meta[0, :n_tiles], meta[1, :n_tiles],
                   pos.T.reshape(-1), emeta.reshape(-1),
                   xpc.reshape((tc,) + slab), xps.reshape((ts,) + slab), comb_tok,
                   w_expert_gu[l], w_expert_down[l])

    fin = functools.partial(_final, wsg=wsg, wsd=wsd, gpo=row(g_post_ffn))
    tiles_per_lat = x_sample.shape[1] // FINAL_TILE
    yc = fin(acc, 0, h2c.reshape(tc, D_MODEL), x1c.reshape(tc, D_MODEL), mod, ctx_row)
    ys = fin(acc, tc // FINAL_TILE, h2s.reshape(ts, D_MODEL), x1s.reshape(ts, D_MODEL), mod,
             lambda i: i // tiles_per_lat + 1)

    new_c = c_new.reshape(b_ctx, 1, N_DIR, HEADS, HEAD_DIM, HEAD_DIM)
    new_n = n_new.reshape(b_ctx, 1, N_DIR, HEADS, HEAD_DIM)
    new_m = m_new[..., 0].reshape(b_ctx, 1, N_DIR, HEADS)
    return (yc.reshape(x_prompt.shape), ys.reshape(x_sample.shape), new_c, new_n, new_m)
```

```python
import functools

import jax
import jax.numpy as jnp
import numpy as np
from jax import lax
from jax.experimental import pallas as pl
from jax.experimental.pallas import tpu as pltpu

F32 = jnp.float32
BF16 = jnp.bfloat16

D_MODEL = 1024
GRID_W = 64
POOL_WIDTH = 512
POOL_GROUPS = 4
POOL_GROUP_DIM = 128
POOL_WINDOWS = (2, 4, 8, 16)
HEADS = 4
HEAD_DIM = 128
MLSTM_WIDTH = HEADS * HEAD_DIM
N_DIR = 2
GATE_COLS = N_DIR * 2 * HEADS
N_EXPERTS = 64
TOP_K = 6
N_EXPERT_GROUPS = 8
GROUP_SIZE = N_EXPERTS // N_EXPERT_GROUPS
TOPK_GROUPS = 4
EXPERT_DIM = 256
SHARED_DIM = 256
ROUTED_SCALE = 2.5
N_MOD = 6
EPS = 1e-6
K_SCALE = HEAD_DIM ** -0.5

LANES = 128
CHUNK = 256
N_ROWS = 16
TOKEN_TILE = 256
FINAL_TILE = 512
MOE_TILE = 320
Y_PITCH = MOE_TILE + 8
VMEM_LIMIT = 56 * 1024 * 1024
EXPERTS_VMEM_LIMIT = 58 * 1024 * 1024


def _split3(x):
    p1 = x.astype(BF16)
    r1 = x - p1.astype(F32)
    p2 = r1.astype(BF16)
    p3 = (r1 - p2.astype(F32)).astype(BF16)
    return p1, p2, p3


def _split2(x):
    p1 = x.astype(BF16)
    p2 = (x - p1.astype(F32)).astype(BF16)
    return p1, p2


def _dot(a, b):
    return jnp.dot(a, b, preferred_element_type=F32)


def _dot_nt(a, b):
    return lax.dot_general(a, b, (((1,), (1,)), ((), ())), preferred_element_type=F32)


def _rmsnorm(x, g):
    return x * lax.rsqrt(jnp.mean(x * x, axis=-1, keepdims=True) + EPS) * g


def _silu(x):
    return x * jax.nn.sigmoid(x)


def _token_rows_from_slabs(read_block, n_tok):
    nch = D_MODEL // LANES
    cols = [[] for _ in range(nch)]
    for g in range(n_tok // 8):
        blk = jnp.swapaxes(read_block(g), 0, 1)
        for cc in range(nch):
            cols[cc].append(blk[cc])
    return jnp.concatenate([jnp.concatenate(c, axis=0) for c in cols], axis=1)


def _params(*sem):
    return pltpu.CompilerParams(dimension_semantics=sem, vmem_limit_bytes=VMEM_LIMIT)


def _mod_kernel(c_ref, w_ref, b_ref, o_ref):
    a = _silu(c_ref[...])
    a_stack = jnp.concatenate(_split3(a), axis=0)
    w1, w2 = _split2(w_ref[...])
    r1 = _dot(a_stack, w1)
    r2 = _dot(a_stack[:32], w2)
    o_ref[...] = (r1[0:16] + r1[16:32] + r1[32:48] + r2[0:16] + r2[16:32]) + b_ref[...]


def _mod_rows(cvec, w_ada, b_ada):
    n = N_MOD * D_MODEL
    tn = 1536
    return pl.pallas_call(
        _mod_kernel,
        grid=(n // tn,),
        in_specs=[pl.BlockSpec((16, D_MODEL), lambda j: (0, 0)),
                  pl.BlockSpec((D_MODEL, tn), lambda j: (0, j)),
                  pl.BlockSpec((1, tn), lambda j: (0, j))],
        out_specs=pl.BlockSpec((16, tn), lambda j: (0, j)),
        out_shape=jax.ShapeDtypeStruct((16, n), F32),
        compiler_params=_params("arbitrary"),
        name="mod",
    )(cvec, w_ada, b_ada.reshape(1, n))


def _inproj_kernel(x_ref, mod_ref, g_ref, wm_ref, wt_ref, wg_ref, bgr_ref,
                   u_ref, k_ref, o_ref, qt_ref, vt_ref, gate_ref, gatet_ref):
    bs, tm, _ = x_ref.shape
    rows = bs * tm
    x = x_ref[...].reshape(rows, D_MODEL)
    mod = mod_ref[0]
    h = _rmsnorm(x, g_ref[...]) * (1.0 + mod[1:2]) + mod[0:1]
    h1, h2, h3 = _split3(h)
    z = _dot(h1, wm_ref[...])
    u_ref[...] = z[:, 0:512].astype(BF16).reshape(bs, tm, 512)
    k_ref[...] = (z[:, 512:1024] * K_SCALE).astype(BF16).reshape(bs, tm, 512)
    o_ref[...] = z[:, 1024:1536].astype(BF16).reshape(bs, tm, 512)
    zt = _dot_nt(wt_ref[...], h1).astype(BF16)
    r = _dot(jnp.concatenate([h1, h2, h3], axis=0), wg_ref[...])
    r12 = r[0:rows] + r[rows:2 * rows]
    gate = (r12 + r[2 * rows:]) + pltpu.roll(r12, LANES - GATE_COLS, axis=1) + bgr_ref[...]
    gate_ref[...] = gate.reshape(bs, tm, LANES)
    gate_t = gate.T
    for bb in range(bs):
        cols = slice(bb * tm, (bb + 1) * tm)
        qt_ref[bb] = zt[0:512, cols]
        vt_ref[bb] = zt[512:1024, cols]
        gatet_ref[bb] = gate_t[0:16, cols]


def _inproj(x, mod, mod_row, g, wm, wt, wg, bgr):
    b, t, _ = x.shape
    tm = min(t, 2 * TOKEN_TILE)
    bs = 2 * TOKEN_TILE // tm if mod_row(1) == mod_row(0) else 1
    const = lambda *shape: pl.BlockSpec(shape, lambda i, j: (0,) * len(shape))
    tok = lambda w: pl.BlockSpec((bs, tm, w), lambda i, j: (i, j, 0))
    tok_t = lambda r: pl.BlockSpec((bs, r, tm), lambda i, j: (i, 0, j))
    sd = jax.ShapeDtypeStruct
    return pl.pallas_call(
        _inproj_kernel,
        grid=(b // bs, t // tm),
        in_specs=[tok(D_MODEL),
                  pl.BlockSpec((1, N_MOD, D_MODEL), lambda i, j: (mod_row(i * bs), 0, 0)),
                  const(1, D_MODEL), const(D_MODEL, 1536), const(1024, D_MODEL),
                  const(D_MODEL, LANES), const(1, LANES)],
        out_specs=[tok(512), tok(512), tok(512), tok_t(512), tok_t(512), tok(LANES), tok_t(16)],
        out_shape=[sd((b, t, 512), BF16), sd((b, t, 512), BF16), sd((b, t, 512), BF16),
                   sd((b, 512, t), BF16), sd((b, 512, t), BF16), sd((b, t, LANES), F32),
                   sd((b, 16, t), F32)],
        compiler_params=_params("arbitrary", "arbitrary"),
        name="inproj",
    )(x, mod, g, wm, wt, wg, bgr)


def _log_sigmoid(x):
    return jnp.minimum(x, 0.0) - jnp.log1p(jnp.exp(-jnp.abs(x)))


def _scan_unit(st, k, qt, vt, u_col, u_row, b_row, btot, mask, s_prev, m_prev, use_state):
    dh = HEAD_DIM
    n = st.shape[0]
    ub = jnp.where(mask, jnp.broadcast_to(u_col, (n, n)), -jnp.inf)
    z = jnp.maximum(m_prev, jnp.max(ub, axis=0, keepdims=True))
    p = (jnp.exp(ub - z) * st).astype(BF16)
    ones = jnp.ones((N_ROWS, n), BF16)
    tot = _dot(jnp.concatenate([vt, ones], axis=0), p)
    if use_state:
        tot = tot + jnp.exp(m_prev - z) * _dot(s_prev.astype(BF16), qt)
    floor = jnp.exp(-(b_row + z))
    h_t = tot[:dh] / jnp.maximum(jnp.abs(tot[dh:dh + 1]), floor)
    g_row = btot + u_row
    m_new = jnp.maximum(btot + m_prev, jnp.max(g_row, axis=-1, keepdims=True))
    w_row = jnp.exp(g_row - m_new)
    vw = jnp.concatenate([(vt.astype(F32) * w_row).astype(BF16),
                          jnp.broadcast_to(w_row, (N_ROWS, n)).astype(BF16)], axis=0)
    s_new = jnp.exp(btot + m_prev - m_new) * s_prev + _dot(vw, k)
    return h_t.T, s_new, m_new


def _mlstm_kernel(*refs, nc, zero_init, emit_state):
    it = iter(refs)
    fwd_refs = tuple(next(it) for _ in range(5))
    bwd_refs = tuple(next(it) for _ in range(5)) if nc > 1 else fwd_refs
    if not zero_init:
        s0_ref, m0_ref = next(it), next(it)
    h_refs = (next(it), next(it))
    if emit_state:
        c_out, n_out, m_out = next(it), next(it), next(it)
    s_scr, m_scr = next(it), next(it)

    j = pl.program_id(1)
    n = CHUNK
    dh = HEAD_DIM

    @pl.when(j == 0)
    def _():
        if zero_init:
            s_scr[...] = jnp.zeros_like(s_scr)
            m_scr[...] = jnp.zeros_like(m_scr)
        else:
            s_scr[...] = s0_ref[0]
            m_scr[...] = m0_ref[0]

    rows = lax.broadcasted_iota(jnp.int32, (n, n), 0)
    cols = lax.broadcasted_iota(jnp.int32, (n, n), 1)
    le = rows <= cols
    ge = rows >= cols
    tri_le = le.astype(BF16)
    tri_ge = ge.astype(BF16)
    use_state = not (zero_init and nc == 1)

    def gate_terms(d):
        g_ref, gt_ref = (fwd_refs, bwd_refs)[d][3:5]
        gate = g_ref[0]
        gate_t = gt_ref[0]
        lf = _log_sigmoid(gate)
        lf_t = _log_sigmoid(gate_t)
        tri_c, tri_r = (tri_ge, tri_le) if d == 0 else (tri_le, tri_ge)
        bc = _dot(tri_c, jnp.concatenate(_split3(lf), axis=1))
        b_cols = bc[:, 0:128] + bc[:, 128:256] + bc[:, 256:384]
        br = _dot(jnp.concatenate(_split3(lf_t), axis=0), tri_r)
        b_rows = br[0:16] + br[16:32] + br[32:48]
        return gate, gate_t, b_cols, b_rows, jnp.sum(lf_t, axis=-1, keepdims=True)

    terms = [gate_terms(0), gate_terms(1)]
    hs = ([], [])
    for hd in range(HEADS):
        hsl = slice(hd * dh, (hd + 1) * dh)
        st = None
        for d in range(N_DIR):
            k_ref, qt_ref, vt_ref = (fwd_refs, bwd_refs)[d][0:3]
            gate, gate_t, b_cols, b_rows, tot_rows = terms[d]
            ci = d * 8 + hd
            cf = d * 8 + 4 + hd
            unit = d * HEADS + hd
            k = k_ref[0, :, hsl]
            qt = qt_ref[0, hsl, :]
            if st is None or nc > 1:
                st = _dot(k, qt)
            mask = le if d == 0 else ge
            h, s_new, m_new = _scan_unit(
                st, k, qt, vt_ref[0, hsl, :],
                gate[:, ci:ci + 1] - b_cols[:, cf:cf + 1],
                gate_t[ci:ci + 1, :] - b_rows[cf:cf + 1, :],
                b_rows[cf:cf + 1, :], tot_rows[cf:cf + 1, :],
                mask, s_scr[unit], m_scr[unit][:, 0:1], use_state)
            s_scr[unit] = s_new
            m_scr[unit] = jnp.broadcast_to(m_new, (1, LANES))
            hs[d].append(h)
    for d in range(N_DIR):
        h_refs[d][0] = jnp.concatenate(hs[d], axis=1).astype(BF16)

    if emit_state:
        @pl.when(j == nc - 1)
        def _():
            for unit in range(N_DIR * HEADS):
                s = s_scr[unit]
                c_out[0, unit] = s[:dh].T
                n_out[0, unit] = s[dh:dh + 1]
                m_out[0, unit] = m_scr[unit]


def _mlstm(k, qt, vt, gate, gate_t, s0, m0, emit_state):
    b, t, _ = k.shape
    nc = t // CHUNK
    zero_init = s0 is None
    nu = N_DIR * HEADS
    fwd = lambda w: pl.BlockSpec((1, CHUNK, w), lambda i, j: (i, j, 0))
    bwd = lambda w: pl.BlockSpec((1, CHUNK, w), lambda i, j: (i, nc - 1 - j, 0))
    fwd_t = lambda r: pl.BlockSpec((1, r, CHUNK), lambda i, j: (i, 0, j))
    bwd_t = lambda r: pl.BlockSpec((1, r, CHUNK), lambda i, j: (i, 0, nc - 1 - j))
    args = [k, qt, vt, gate, gate_t]
    in_specs = [fwd(512), fwd_t(512), fwd_t(512), fwd(LANES), fwd_t(16)]
    if nc > 1:
        args += [k, qt, vt, gate, gate_t]
        in_specs += [bwd(512), bwd_t(512), bwd_t(512), bwd(LANES), bwd_t(16)]
    if not zero_init:
        args += [s0, m0]
        in_specs += [pl.BlockSpec((1, nu, HEAD_DIM + N_ROWS, HEAD_DIM), lambda i, j: (i, 0, 0, 0)),
                     pl.BlockSpec((1, nu, 1, LANES), lambda i, j: (i, 0, 0, 0))]
    sd = jax.ShapeDtypeStruct
    out_shape = [sd((b, t, 512), BF16), sd((b, t, 512), BF16)]
    out_specs = [fwd(512), bwd(512)]
    if emit_state:
        out_shape += [sd((b, nu, HEAD_DIM, HEAD_DIM), F32), sd((b, nu, 1, HEAD_DIM), F32),
                      sd((b, nu, 1, LANES), F32)]
        out_specs += [pl.BlockSpec((1, nu, HEAD_DIM, HEAD_DIM), lambda i, j: (i, 0, 0, 0)),
                      pl.BlockSpec((1, nu, 1, HEAD_DIM), lambda i, j: (i, 0, 0, 0)),
                      pl.BlockSpec((1, nu, 1, LANES), lambda i, j: (i, 0, 0, 0))]
    return pl.pallas_call(
        functools.partial(_mlstm_kernel, nc=nc, zero_init=zero_init, emit_state=emit_state),
        grid=(b, nc),
        in_specs=in_specs,
        out_specs=out_specs,
        out_shape=out_shape,
        scratch_shapes=[pltpu.VMEM((nu, HEAD_DIM + N_ROWS, HEAD_DIM), F32),
                        pltpu.VMEM((nu, 1, LANES), F32)],
        compiler_params=_params("arbitrary", "arbitrary"),
        name="mlstm",
    )(*args)


def _outproj_kernel(x_ref, u_ref, hf_ref, hb_ref, o_ref, mod_ref, pm_ref, pinv_ref, wp_ref, ps_ref,
                    wo_ref, gpm_ref, gpf_ref, wr_ref, x1_ref, h2_ref, xp_ref, lg_ref):
    bs, tm, _ = x_ref.shape
    rows = bs * tm
    x = x_ref[...].reshape(rows, D_MODEL)
    mod = mod_ref[0]
    row0 = pl.multiple_of(pl.program_id(1) * tm, tm)
    diffs = []
    for g in range(POOL_GROUPS):
        sl = slice(g * POOL_GROUP_DIM, (g + 1) * POOL_GROUP_DIM)
        per_seq = []
        for bb in range(bs):
            box = _dot(pm_ref[g], u_ref[bb, :, sl])
            per_seq.append(box * pinv_ref[g] - u_ref[bb, pl.ds(row0, tm), sl].astype(F32))
        diffs.append(jnp.concatenate(per_seq, axis=0).astype(BF16))
    yps = [_dot(jnp.concatenate(diffs[2 * p:2 * p + 2], axis=1), wp_ref[p]) for p in range(POOL_GROUPS // 2)]
    y_pool = jnp.concatenate(yps, axis=1) * ps_ref[...]
    seq_rows = lambda ref: ref[...].reshape(rows, ref.shape[-1]).astype(F32)
    y_ml = jax.nn.sigmoid(seq_rows(o_ref)) * (seq_rows(hf_ref) + seq_rows(hb_ref))
    mix = _dot(jnp.concatenate([y_pool, y_ml], axis=1).astype(BF16), wo_ref[...])
    x1 = x + mod[2:3] * _rmsnorm(mix, gpm_ref[...])
    x1_ref[...] = x1.reshape(bs, tm, D_MODEL)
    h2 = _rmsnorm(x1, gpf_ref[...]) * (1.0 + mod[4:5]) + mod[3:4]
    p1, p2, p3 = _split3(h2)
    h2_ref[...] = p1.reshape(bs, tm, D_MODEL)
    nch = D_MODEL // LANES
    for g in range(rows // 8):
        cols = jnp.stack([h2[g * 8:(g + 1) * 8, cc * LANES:(cc + 1) * LANES] for cc in range(nch)], axis=0)
        bb, r0 = divmod(g * 8, tm)
        xp_ref[bb, r0:r0 + 8] = jnp.swapaxes(cols, 0, 1).astype(BF16)
    r = _dot(jnp.concatenate([p1, p2, p3], axis=0), wr_ref[...])
    r12 = r[0:rows] + r[rows:2 * rows]
    ne = N_EXPERTS
    lg = (r12 + r[2 * rows:])[:, 0:ne] + r12[:, ne:2 * ne] + r[0:rows, 2 * ne:3 * ne]
    lg_ref[...] = jnp.concatenate([lg, jnp.zeros_like(lg)], axis=1).T[0:ne]


def _outproj(x, u, hf, hb, o, mod, mod_row, pm, pinv, wp, ps, wo, gpm, gpf, wr):
    b, t, _ = x.shape
    tm = min(t, 2 * TOKEN_TILE)
    bs = 2 * TOKEN_TILE // tm if mod_row(1) == mod_row(0) else 1
    const = lambda *shape: pl.BlockSpec(shape, lambda i, j: (0,) * len(shape))
    tok = lambda w: pl.BlockSpec((bs, tm, w), lambda i, j: (i, j, 0))
    sd = jax.ShapeDtypeStruct
    return pl.pallas_call(
        _outproj_kernel,
        grid=(b // bs, t // tm),
        in_specs=[tok(D_MODEL),
                  pl.BlockSpec((bs, t, 512), lambda i, j: (i, 0, 0)),
                  tok(512), tok(512), tok(512),
                  pl.BlockSpec((1, N_MOD, D_MODEL), lambda i, j: (mod_row(i * bs), 0, 0)),
                  pl.BlockSpec((POOL_GROUPS, tm, t), lambda i, j: (0, j, 0)),
                  pl.BlockSpec((POOL_GROUPS, tm, 1), lambda i, j: (0, j, 0)),
                  const(POOL_GROUPS // 2, 2 * POOL_GROUP_DIM, 2 * POOL_GROUP_DIM), const(1, POOL_WIDTH),
                  const(D_MODEL, D_MODEL), const(1, D_MODEL), const(1, D_MODEL),
                  const(D_MODEL, 2 * LANES)],
        out_specs=[tok(D_MODEL), tok(D_MODEL),
                   pl.BlockSpec((bs, tm, D_MODEL // LANES, LANES), lambda i, j: (i, j, 0, 0)),
                   pl.BlockSpec((N_EXPERTS, bs * tm), lambda i, j: (0, i * (t // tm) + j))],
        out_shape=[sd((b, t, D_MODEL), F32), sd((b, t, D_MODEL), BF16),
                   sd((b, t, D_MODEL // LANES, LANES), BF16), sd((N_EXPERTS, b * t), F32)],
        compiler_params=_params("arbitrary", "arbitrary"),
        name="outproj",
    )(x, u, hf, hb, o, mod, pm, pinv, wp, ps, wo, gpm, gpf, wr)


def _router_kernel(lg_ref, br_ref, comb_ref, sel_ref, *, n_blocks):
    ng, gs = N_EXPERT_GROUPS, GROUP_SIZE
    neg = -jnp.inf
    lg = jnp.swapaxes(lg_ref[...], 0, 1)
    br = br_ref[...]
    s = [jax.nn.sigmoid(lg[j]) for j in range(gs)]
    biased = [s[j] + br[:, j, :] for j in range(gs)]
    fold = lambda op, xs: functools.reduce(op, xs)
    m1 = fold(jnp.maximum, biased)
    i1 = fold(jnp.minimum, [jnp.where(biased[j] == m1, j, gs) for j in range(gs)])
    m2 = fold(jnp.maximum, [jnp.where(i1 == j, neg, biased[j]) for j in range(gs)])
    cur = m1 + m2
    gi = lax.broadcasted_iota(jnp.int32, cur.shape, 0)
    gmask = jnp.zeros(cur.shape, F32)
    for _ in range(TOPK_GROUPS):
        mx = jnp.max(cur, axis=0, keepdims=True)
        ix = jnp.min(jnp.where(cur == mx, gi, ng), axis=0, keepdims=True)
        hit = gi == ix
        gmask = jnp.where(hit, 1.0, gmask)
        cur = jnp.where(hit, neg, cur)
    cand = [jnp.where(gmask > 0, biased[j], neg) for j in range(gs)]
    eidx = [gi * gs + j for j in range(gs)]
    selm = [jnp.zeros(cur.shape, F32) for _ in range(gs)]
    for _ in range(TOP_K):
        mx = jnp.max(fold(jnp.maximum, cand), axis=0, keepdims=True)
        ix = jnp.min(fold(jnp.minimum, [jnp.where(cand[j] == mx, eidx[j], N_EXPERTS) for j in range(gs)]),
                     axis=0, keepdims=True)
        for j in range(gs):
            hit = eidx[j] == ix
            selm[j] = jnp.where(hit, 1.0, selm[j])
            cand[j] = jnp.where(hit, neg, cand[j])
    sel = [selm[j] * s[j] for j in range(gs)]
    tot = jnp.sum(fold(jnp.add, sel), axis=0, keepdims=True)
    comb = [sel[j] / tot * ROUTED_SCALE for j in range(gs)]
    sel_ref[...] = jnp.swapaxes(jnp.stack(selm, axis=0), 0, 1)
    comb_e = jnp.swapaxes(jnp.stack(comb, axis=0), 0, 1).reshape(N_EXPERTS, -1)
    comb_t = jnp.concatenate([comb_e, jnp.zeros_like(comb_e)], axis=0).T
    comb_ref[...] = jnp.where(pl.program_id(0) < n_blocks, comb_t, 0.0)


def _router(logits_t, b_router):
    t = logits_t.shape[1]
    tl = 1024
    nb = t // tl
    shp = (N_EXPERT_GROUPS, GROUP_SIZE, t)
    blk = pl.BlockSpec((N_EXPERT_GROUPS, GROUP_SIZE, tl), lambda j: (0, 0, jnp.minimum(j, nb - 1)))
    comb, sel = pl.pallas_call(
        functools.partial(_router_kernel, n_blocks=nb),
        grid=(nb + 1,),
        in_specs=[blk, pl.BlockSpec((N_EXPERT_GROUPS, GROUP_SIZE, 1), lambda j: (0, 0, 0))],
        out_specs=[pl.BlockSpec((tl, LANES), lambda j: (j, 0)), blk],
        out_shape=[jax.ShapeDtypeStruct((t + tl, LANES), F32), jax.ShapeDtypeStruct(shp, F32)],
        compiler_params=_params("arbitrary"),
        name="router",
    )(logits_t.reshape(shp), b_router.reshape(N_EXPERT_GROUPS, GROUP_SIZE, 1))
    return comb, sel.reshape(N_EXPERTS, t)


def _plan_kernel(sel_ref, pos_ref, meta_ref, emeta_ref, *, n_meta):
    t = sel_ref.shape[1]
    tm = float(MOE_TILE)
    sel = sel_ref[...]
    selb = sel.astype(BF16)
    blk = 256
    rr = lax.broadcasted_iota(jnp.int32, (blk, blk), 0)
    cc = lax.broadcasted_iota(jnp.int32, (blk, blk), 1)
    before = (rr < cc).astype(BF16)
    carry = jnp.zeros((N_EXPERTS, 1), F32)
    ranks = []
    for b in range(t // blk):
        sb = selb[:, b * blk:(b + 1) * blk]
        ranks.append(_dot(sb, before) + carry)
        carry = carry + jnp.sum(sel[:, b * blk:(b + 1) * blk], axis=1, keepdims=True)
    rank = jnp.concatenate(ranks, axis=1)
    cnt = carry
    ntile = jnp.floor((cnt + (tm - 0.5)) * (1.0 / tm))
    er = lax.broadcasted_iota(jnp.int32, (N_EXPERTS, N_EXPERTS), 0)
    ec = lax.broadcasted_iota(jnp.int32, (N_EXPERTS, N_EXPERTS), 1)
    below = (ec < er).astype(BF16)
    tstart = _dot(below, jnp.broadcast_to(ntile, (N_EXPERTS, LANES)).astype(BF16))[:, 0:1]
    pos = tstart * tm + rank
    erank = _dot(below, selb)
    rows = []
    for k in range(TOP_K):
        hit = (sel > 0.0) & (erank == float(k))
        rows.append(jnp.sum(jnp.where(hit, pos, 0.0), axis=0, keepdims=True))
    rows += [jnp.zeros((1, t), F32)] * (8 - TOP_K)
    pos_ref[...] = jnp.concatenate(rows, axis=0).astype(jnp.int32)

    tau = lax.broadcasted_iota(jnp.int32, (N_EXPERTS, n_meta), 1).astype(F32)
    eidx = lax.broadcasted_iota(jnp.int32, (N_EXPERTS, n_meta), 0).astype(F32)
    te = jnp.sum(((tstart + ntile) <= tau).astype(F32), axis=0, keepdims=True)
    te = jnp.minimum(te, float(N_EXPERTS - 1))
    onehot = eidx == te
    cnt_t = jnp.sum(jnp.where(onehot, cnt, 0.0), axis=0, keepdims=True)
    ts_t = jnp.sum(jnp.where(onehot, tstart, 0.0), axis=0, keepdims=True)
    tr = jnp.clip(cnt_t - (tau[0:1] - ts_t) * tm, 0.0, tm)
    tf = jnp.where((tau[0:1] == ts_t) & (tr > 0.0), 1.0, 0.0)
    meta_ref[...] = jnp.concatenate([te, tr, tf] + [jnp.zeros((1, n_meta), F32)] * 5, axis=0).astype(jnp.int32)

    eye = (lax.broadcasted_iota(jnp.int32, (N_EXPERTS, LANES), 0)
           == lax.broadcasted_iota(jnp.int32, (N_EXPERTS, LANES), 1))
    as_row = lambda col: jnp.sum(jnp.where(eye, col, 0.0), axis=0, keepdims=True)
    emeta_ref[...] = jnp.concatenate([as_row(cnt), as_row(tstart), as_row(ntile)]
                                     + [jnp.zeros((1, LANES), F32)] * 5, axis=0).astype(jnp.int32)


def _plan(sel, n_meta):
    t = sel.shape[1]
    sd = jax.ShapeDtypeStruct
    return pl.pallas_call(
        functools.partial(_plan_kernel, n_meta=n_meta),
        out_shape=[sd((8, t), jnp.int32), sd((8, n_meta), jnp.int32), sd((8, LANES), jnp.int32)],
        compiler_params=pltpu.CompilerParams(vmem_limit_bytes=VMEM_LIMIT),
        name="plan",
    )(sel)


def _fill_slot_table(pos_ref, emeta_ref, tbl_ref, n_tokens):
    group = 32
    def pad_expert(e, carry):
        n_t = emeta_ref[2 * LANES + e]
        start = (emeta_ref[LANES + e] + n_t - 1) * MOE_TILE

        @pl.when(n_t > 0)
        def _():
            def put(i, c):
                for u in range(group):
                    tbl_ref[start + i * group + u] = n_tokens
                return c
            lax.fori_loop(0, MOE_TILE // group, put, 0)
        return carry
    lax.fori_loop(0, N_EXPERTS, pad_expert, 0)
    last = N_EXPERTS - 1
    def pad_tile(tile, carry):
        def put(i, c):
            for u in range(group):
                tbl_ref[tile * MOE_TILE + i * group + u] = n_tokens
            return c
        lax.fori_loop(0, MOE_TILE // group, put, 0)
        return carry
    lax.fori_loop(emeta_ref[LANES + last] + emeta_ref[2 * LANES + last], tbl_ref.shape[0] // MOE_TILE, pad_tile, 0)

    unroll = 8
    def scatter(i, carry):
        first = i * (8 * unroll)
        for u in range(unroll):
            for k in range(TOP_K):
                tbl_ref[pos_ref[first + (8 * u + k)]] = i * unroll + u
        return carry
    lax.fori_loop(0, n_tokens // unroll, scatter, 0)


def _experts_kernel(te_ref, tr_ref, pos_ref, emeta_ref, xc_ref, xs_ref, comb_ref, wgu_ref, wd_ref, acc_out,
                    tbl_ref, xbuf, acc, stage0, stage1, cst0, cst1, act0, act1, ybuf0, ybuf1,
                    wgu_b0, wgu_b1, wd_b0, wd_b1, sems):
    s = pl.program_id(0)
    n_tiles = te_ref.shape[0]
    tm = MOE_TILE
    nch = D_MODEL // LANES
    tile_at = lambda lag: jnp.clip(s - lag, 0, n_tiles - 1)
    t_g, t_1, t_2, t_3 = tile_at(0), tile_at(1), tile_at(2), tile_at(3)

    @pl.when(s == 0)
    def _():
        tc, ts = xc_ref.shape[0], xs_ref.shape[0]
        copies = (pltpu.make_async_copy(xc_ref, xbuf.at[pl.ds(0, tc)], sems.at[0]),
                  pltpu.make_async_copy(xs_ref, xbuf.at[pl.ds(tc, ts)], sems.at[1]))
        for cp in copies:
            cp.start()
        n_pad = xbuf.shape[0] - tc - ts
        xbuf[pl.ds(tc + ts, n_pad)] = jnp.zeros((n_pad,) + xbuf.shape[1:], BF16)
        for ref in (acc, stage0, stage1, cst0, cst1, act0, act1, ybuf0, ybuf1, wgu_b0, wgu_b1, wd_b0, wd_b1):
            ref[...] = jnp.zeros_like(ref)
        _fill_slot_table(pos_ref, emeta_ref, tbl_ref, tc + ts)
        for cp in copies:
            cp.wait()

    def gather(tile, stage, cst):
        base = tile * tm
        for j in range(tm):
            tok = tbl_ref[base + j]
            stage[pl.ds(j * nch, nch), :] = xbuf[tok].astype(F32)
            cst[pl.ds(j, 1), :] = comb_ref[pl.ds(tok, 1), :]

    def gate_up(tile, stage, cst, wgu_b, act):
        xb = _token_rows_from_slabs(
            lambda g: stage[g * 8 * nch:(g + 1) * 8 * nch, :].reshape(8, nch, LANES), tm).astype(BF16)
        gu = _dot(xb, wgu_b[...])
        lane = lax.broadcasted_iota(jnp.int32, (1, LANES), 1)
        w_col = jnp.sum(jnp.where(lane == te_ref[tile], cst[...], 0.0), axis=1, keepdims=True)
        act[...] = (_silu(gu[:, :EXPERT_DIM]) * gu[:, EXPERT_DIM:] * w_col).astype(BF16)

    def down(act, wd_b, ybuf):
        y = _dot(act[...], wd_b[...])
        for cc in range(nch):
            ybuf[cc * Y_PITCH:cc * Y_PITCH + tm, :] = y[:, cc * LANES:(cc + 1) * LANES]

    def scatter(tile, ybuf):
        base = tile * tm
        sc_n = 16
        for i in range(tm // sc_n):
            toks = [tbl_ref[base + i * sc_n + u] for u in range(sc_n)]
            olds = [acc[toks[u]] for u in range(sc_n)]
            news = [olds[u] + ybuf[pl.ds(i * sc_n + u, nch, stride=Y_PITCH), :] for u in range(sc_n)]
            for u in range(sc_n):
                acc[toks[u]] = news[u]

    busy = (tr_ref[t_g] + tr_ref[t_1] + tr_ref[t_2] + tr_ref[t_3]) > 0
    bufs = ((stage0, cst0, act0, ybuf0, wgu_b0, wd_b0), (stage1, cst1, act1, ybuf1, wgu_b1, wd_b1))
    for par in range(2):
        stage_p, cst_p, act_p, ybuf_p, wgu_p, wd_p = bufs[par]
        stage_q, cst_q, act_q, ybuf_q, wgu_q, wd_q = bufs[1 - par]

        @pl.when(busy & (s % 2 == par))
        def _():
            gather(t_g, stage_p, cst_p)
            wgu_p[...] = wgu_ref[0].astype(BF16)
            gate_up(t_1, stage_q, cst_q, wgu_q, act_q)
            wd_q[...] = wd_ref[0].astype(BF16)
            down(act_p, wd_p, ybuf_p)
            scatter(t_3, ybuf_q)

    @pl.when(s == pl.num_programs(0) - 1)
    def _():
        cp = pltpu.make_async_copy(acc, acc_out, sems.at[2])
        cp.start()
        cp.wait()


def _experts(te, tr, pos, emeta, xc, xs, comb, wgu, wd):
    n_tok = xc.shape[0] + xs.shape[0] + 8
    n_tiles = te.shape[0]
    tm = MOE_TILE
    nch = D_MODEL // LANES
    vm = pltpu.VMEM
    grid_spec = pltpu.PrefetchScalarGridSpec(
        num_scalar_prefetch=4,
        grid=(n_tiles + 3,),
        in_specs=[pl.BlockSpec(memory_space=pl.ANY), pl.BlockSpec(memory_space=pl.ANY),
                  pl.BlockSpec((n_tok, LANES), lambda s, *_: (0, 0), pipeline_mode=pl.Buffered(1)),
                  pl.BlockSpec((1, D_MODEL, 2 * EXPERT_DIM), lambda s, te, *_: (te[jnp.minimum(s, n_tiles - 1)], 0, 0)),
                  pl.BlockSpec((1, EXPERT_DIM, D_MODEL), lambda s, te, *_: (te[jnp.clip(s - 1, 0, n_tiles - 1)], 0, 0))],
        out_specs=pl.BlockSpec(memory_space=pl.ANY),
        scratch_shapes=[pltpu.SMEM((n_tiles * tm,), jnp.int32),
                        vm((n_tok, nch, LANES), BF16), vm((n_tok, nch, LANES), F32),
                        vm((tm * nch, LANES), F32), vm((tm * nch, LANES), F32),
                        vm((tm, LANES), F32), vm((tm, LANES), F32),
                        vm((tm, EXPERT_DIM), BF16), vm((tm, EXPERT_DIM), BF16),
                        vm((nch * Y_PITCH, LANES), F32), vm((nch * Y_PITCH, LANES), F32),
                        vm((D_MODEL, 2 * EXPERT_DIM), BF16), vm((D_MODEL, 2 * EXPERT_DIM), BF16),
                        vm((EXPERT_DIM, D_MODEL), BF16), vm((EXPERT_DIM, D_MODEL), BF16),
                        pltpu.SemaphoreType.DMA((3,))],
    )
    return pl.pallas_call(
        _experts_kernel,
        grid_spec=grid_spec,
        out_shape=jax.ShapeDtypeStruct((n_tok, nch, LANES), F32),
        compiler_params=pltpu.CompilerParams(dimension_semantics=("arbitrary",),
                                             vmem_limit_bytes=EXPERTS_VMEM_LIMIT),
        name="experts",
    )(te, tr, pos, emeta, xc, xs, comb, wgu, wd)


def _final_kernel(acc_ref, h_ref, x1_ref, mod_ref, wsg_ref, wsd_ref, gpo_ref, out_ref):
    routed = _token_rows_from_slabs(lambda g: acc_ref[g * 8:(g + 1) * 8], acc_ref.shape[0])
    gs = _dot(h_ref[...], wsg_ref[...])
    act = _silu(gs[:, :SHARED_DIM]) * gs[:, SHARED_DIM:]
    f = routed + _dot(act.astype(BF16), wsd_ref[...])
    out_ref[...] = x1_ref[...] + mod_ref[0][5:6] * _rmsnorm(f, gpo_ref[...])


def _final(acc, tile0, h2, x1, mod, mod_row, wsg, wsd, gpo):
    n = h2.shape[0]
    tm = FINAL_TILE
    const = lambda *shape: pl.BlockSpec(shape, lambda i: (0,) * len(shape))
    tok = lambda w: pl.BlockSpec((tm, w), lambda i: (i, 0))
    return pl.pallas_call(
        _final_kernel,
        grid=(n // tm,),
        in_specs=[pl.BlockSpec((tm, D_MODEL // LANES, LANES), lambda i: (i + tile0, 0, 0)),
                  tok(D_MODEL), tok(D_MODEL),
                  pl.BlockSpec((1, N_MOD, D_MODEL), lambda i: (mod_row(i), 0, 0)),
                  const(D_MODEL, 2 * SHARED_DIM), const(SHARED_DIM, D_MODEL), const(1, D_MODEL)],
        out_specs=tok(D_MODEL),
        out_shape=jax.ShapeDtypeStruct((n, D_MODEL), F32),
        compiler_params=_params("arbitrary"),
        name="final",
    )(acc, h2, x1, mod, wsg, wsd, gpo)


def _window_bounds(n, w):
    idx = np.arange(n)
    return np.clip(idx - w // 2, 0, n), np.clip(idx + w - w // 2, 0, n)


def _pool_operators(t, grid):
    mats, invs = [], []
    for w in POOL_WINDOWS:
        if grid:
            rlo, rhi = _window_bounds(t // GRID_W, w)
            clo, chi = _window_bounds(GRID_W, w)
            r = np.arange(t) // GRID_W
            c = np.arange(t) % GRID_W
            m = ((r[None, :] >= rlo[r][:, None]) & (r[None, :] < rhi[r][:, None])
                 & (c[None, :] >= clo[c][:, None]) & (c[None, :] < chi[c][:, None]))
            cnt = (rhi - rlo)[r] * (chi - clo)[c]
        else:
            lo, hi = _window_bounds(t, w)
            sidx = np.arange(t)
            m = (sidx[None, :] >= lo[:, None]) & (sidx[None, :] < hi[:, None])
            cnt = hi - lo
        mats.append(m.astype(np.float32))
        invs.append((1.0 / cnt.astype(np.float64)).astype(np.float32)[:, None])
    return jnp.asarray(np.stack(mats), BF16), jnp.asarray(np.stack(invs), F32)


def kernel(x_prompt, x_sample, state_C, state_n, state_m, c, c_ctx, w_ada, b_ada, g_pre_mix, w_in, b_gate,
           w_pool, pool_scale, w_out, g_post_mix, g_pre_ffn, w_router, b_router, w_expert_gu, w_expert_down,
           w_shared_gu, w_shared_down, g_post_ffn):
    b_ctx = x_prompt.shape[0]
    b_lat = x_sample.shape[0]
    nu = N_DIR * HEADS
    l = 0
    row = lambda a: a[l].reshape(1, -1).astype(F32)

    cvec = jnp.zeros((16, D_MODEL), F32).at[0].set(c_ctx.astype(F32)).at[1:1 + b_lat].set(c.astype(F32))
    mod = _mod_rows(cvec, w_ada[l], b_ada[l]).reshape(16, N_MOD, D_MODEL)

    w_in_l = w_in[l]
    p0 = POOL_WIDTH
    mw = MLSTM_WIDTH
    w_u, w_q, w_k, w_v, w_o = (w_in_l[:, lo:lo + 512] for lo in (0, p0, p0 + mw, p0 + 2 * mw, p0 + 3 * mw))
    wm = jnp.concatenate([w_u, w_k, w_o], axis=1).astype(BF16)
    wt = jnp.concatenate([w_q.T, w_v.T], axis=0).astype(BF16)
    wg_cols = w_in_l[:, p0 + 4 * mw:]
    wg = jnp.pad(jnp.concatenate(_split2(wg_cols), axis=1), ((0, 0), (0, LANES - 2 * GATE_COLS)))
    bg = b_gate[l].reshape(GATE_COLS).astype(F32)
    bgr = jnp.pad(bg, (0, LANES - GATE_COLS)).reshape(1, LANES)
    wpl = w_pool[l].astype(BF16)
    zg = jnp.zeros((POOL_GROUP_DIM, POOL_GROUP_DIM), BF16)
    wp = jnp.stack([jnp.block([[wpl[2 * p], zg], [zg, wpl[2 * p + 1]]]) for p in range(POOL_GROUPS // 2)])
    wo = w_out[l].astype(BF16)
    wr = jnp.pad(jnp.concatenate(_split3(w_router[l].astype(F32)), axis=1), ((0, 0), (0, 2 * LANES - 3 * N_EXPERTS)))
    wsg = w_shared_gu[l].astype(BF16)
    wsd = w_shared_down[l].astype(BF16)

    def mixer(x, mod_row, grid, s0, m0, emit_state):
        t = x.shape[1]
        u, k, o, qt, vt, gate, gate_t = _inproj(x.astype(F32), mod, mod_row, row(g_pre_mix), wm, wt, wg, bgr)
        outs = _mlstm(k, qt, vt, gate, gate_t, s0, m0, emit_state)
        hf, hb = outs[0], outs[1]
        pm, pinv = _pool_operators(t, grid)
        x1, h2, xp, lg = _outproj(x.astype(F32), u, hf, hb, o, mod, mod_row, pm, pinv, wp, row(pool_scale), wo,
                                  row(g_post_mix), row(g_pre_ffn), wr)
        return x1, h2, xp, lg, outs[2:]

    ctx_row = lambda i: 0
    lat_row = lambda i: i + 1
    x1c, h2c, xpc, lgc, (c_new, n_new, m_new) = mixer(x_prompt, ctx_row, False, None, None, True)
    s0 = jnp.concatenate(
        [jnp.swapaxes(state_C[:, l].reshape(b_lat, nu, HEAD_DIM, HEAD_DIM).astype(F32), -1, -2),
         jnp.broadcast_to(state_n[:, l].reshape(b_lat, nu, 1, HEAD_DIM).astype(F32),
                          (b_lat, nu, N_ROWS, HEAD_DIM))], axis=-2)
    m0 = jnp.broadcast_to(state_m[:, l].reshape(b_lat, nu, 1, 1).astype(F32), (b_lat, nu, 1, LANES))
    x1s, h2s, xps, lgs, _ = mixer(x_sample, lat_row, True, s0, m0, False)

    tc = b_ctx * x_prompt.shape[1]
    ts = b_lat * x_sample.shape[1]
    n_tok = tc + ts
    lg_all = jnp.concatenate([lgc, lgs], axis=1)
    comb_tok, sel = _router(lg_all, b_router[l].astype(F32))
    n_tiles = n_tok * TOP_K // MOE_TILE + N_EXPERTS
    n_meta = -(-n_tiles // LANES) * LANES
    pos, meta, emeta = _plan(sel, n_meta)

    slab = (D_MODEL // LANES, LANES)
    acc = _experts(meta[0, :n_tiles], meta[1, :n_tiles],
                   pos.T.reshape(-1), emeta.reshape(-1),
                   xpc.reshape((tc,) + slab), xps.reshape((ts,) + slab), comb_tok,
                   w_expert_gu[l], w_expert_down[l])

    fin = functools.partial(_final, wsg=wsg, wsd=wsd, gpo=row(g_post_ffn))
    tiles_per_lat = x_sample.shape[1] // FINAL_TILE
    yc = fin(acc, 0, h2c.reshape(tc, D_MODEL), x1c.reshape(tc, D_MODEL), mod, ctx_row)
    ys = fin(acc, tc // FINAL_TILE, h2s.reshape(ts, D_MODEL), x1s.reshape(ts, D_MODEL), mod,
             lambda i: i // tiles_per_lat + 1)

    new_c = c_new.reshape(b_ctx, 1, N_DIR, HEADS, HEAD_DIM, HEAD_DIM)
    new_n = n_new.reshape(b_ctx, 1, N_DIR, HEADS, HEAD_DIM)
    new_m = m_new[..., 0].reshape(b_ctx, 1, N_DIR, HEADS)
    return (yc.reshape(x_prompt.shape), ys.reshape(x_sample.shape), new_c, new_n, new_m)
```

```python
import functools

import jax
import jax.numpy as jnp
import numpy as np
from jax import lax
from jax.experimental import pallas as pl
from jax.experimental.pallas import tpu as pltpu

F32 = jnp.float32
BF16 = jnp.bfloat16

D_MODEL = 1024
GRID_W = 64
POOL_WIDTH = 512
POOL_GROUPS = 4
POOL_GROUP_DIM = 128
POOL_WINDOWS = (2, 4, 8, 16)
HEADS = 4
HEAD_DIM = 128
MLSTM_WIDTH = HEADS * HEAD_DIM
N_DIR = 2
GATE_COLS = N_DIR * 2 * HEADS
N_EXPERTS = 64
TOP_K = 6
N_EXPERT_GROUPS = 8
GROUP_SIZE = N_EXPERTS // N_EXPERT_GROUPS
TOPK_GROUPS = 4
EXPERT_DIM = 256
SHARED_DIM = 256
ROUTED_SCALE = 2.5
N_MOD = 6
EPS = 1e-6
K_SCALE = HEAD_DIM ** -0.5

LANES = 128
CHUNK = 256
N_ROWS = 16
TOKEN_TILE = 256
FINAL_TILE = 1024
MOE_TILE = 320
Y_PITCH = MOE_TILE + 8
VMEM_LIMIT = 56 * 1024 * 1024
EXPERTS_VMEM_LIMIT = 58 * 1024 * 1024


def _split3(x):
    p1 = x.astype(BF16)
    r1 = x - p1.astype(F32)
    p2 = r1.astype(BF16)
    p3 = (r1 - p2.astype(F32)).astype(BF16)
    return p1, p2, p3


def _split2(x):
    p1 = x.astype(BF16)
    p2 = (x - p1.astype(F32)).astype(BF16)
    return p1, p2


def _dot(a, b):
    return jnp.dot(a, b, preferred_element_type=F32)


def _dot_nt(a, b):
    return lax.dot_general(a, b, (((1,), (1,)), ((), ())), preferred_element_type=F32)


def _rmsnorm(x, g):
    return x * lax.rsqrt(jnp.mean(x * x, axis=-1, keepdims=True) + EPS) * g


def _silu(x):
    return x * jax.nn.sigmoid(x)


def _token_rows_from_slabs(read_block, n_tok):
    nch = D_MODEL // LANES
    cols = [[] for _ in range(nch)]
    for g in range(n_tok // 8):
        blk = jnp.swapaxes(read_block(g), 0, 1)
        for cc in range(nch):
            cols[cc].append(blk[cc])
    return jnp.concatenate([jnp.concatenate(c, axis=0) for c in cols], axis=1)


def _params(*sem):
    return pltpu.CompilerParams(dimension_semantics=sem, vmem_limit_bytes=VMEM_LIMIT)


def _mod_kernel(c_ref, w_ref, b_ref, o_ref):
    a = _silu(c_ref[...])
    a_stack = jnp.concatenate(_split3(a), axis=0)
    w1, w2 = _split2(w_ref[...])
    r1 = _dot(a_stack, w1)
    r2 = _dot(a_stack[:32], w2)
    o_ref[...] = (r1[0:16] + r1[16:32] + r1[32:48] + r2[0:16] + r2[16:32]) + b_ref[...]


def _mod_rows(cvec, w_ada, b_ada):
    n = N_MOD * D_MODEL
    tn = 1536
    return pl.pallas_call(
        _mod_kernel,
        grid=(n // tn,),
        in_specs=[pl.BlockSpec((16, D_MODEL), lambda j: (0, 0)),
                  pl.BlockSpec((D_MODEL, tn), lambda j: (0, j)),
                  pl.BlockSpec((1, tn), lambda j: (0, j))],
        out_specs=pl.BlockSpec((16, tn), lambda j: (0, j)),
        out_shape=jax.ShapeDtypeStruct((16, n), F32),
        compiler_params=_params("arbitrary"),
        name="mod",
    )(cvec, w_ada, b_ada.reshape(1, n))


def _inproj_kernel(x_ref, mod_ref, g_ref, wm_ref, wt_ref, wg_ref, bgr_ref,
                   u_ref, k_ref, o_ref, qt_ref, vt_ref, gate_ref, gatet_ref):
    bs, tm, _ = x_ref.shape
    rows = bs * tm
    x = x_ref[...].reshape(rows, D_MODEL)
    mod = mod_ref[0]
    h = _rmsnorm(x, g_ref[...]) * (1.0 + mod[1:2]) + mod[0:1]
    h1, h2, h3 = _split3(h)
    z = _dot(h1, wm_ref[...])
    u_ref[...] = z[:, 0:512].astype(BF16).reshape(bs, tm, 512)
    k_ref[...] = (z[:, 512:1024] * K_SCALE).astype(BF16).reshape(bs, tm, 512)
    o_ref[...] = z[:, 1024:1536].astype(BF16).reshape(bs, tm, 512)
    zt = _dot_nt(wt_ref[...], h1).astype(BF16)
    r = _dot(jnp.concatenate([h1, h2, h3], axis=0), wg_ref[...])
    r12 = r[0:rows] + r[rows:2 * rows]
    gate = (r12 + r[2 * rows:]) + pltpu.roll(r12, LANES - GATE_COLS, axis=1) + bgr_ref[...]
    gate_ref[...] = gate.reshape(bs, tm, LANES)
    gate_t = gate.T
    for bb in range(bs):
        cols = slice(bb * tm, (bb + 1) * tm)
        qt_ref[bb] = zt[0:512, cols]
        vt_ref[bb] = zt[512:1024, cols]
        gatet_ref[bb] = gate_t[0:16, cols]


def _inproj(x, mod, mod_row, g, wm, wt, wg, bgr):
    b, t, _ = x.shape
    tm = min(t, 2 * TOKEN_TILE)
    bs = 2 * TOKEN_TILE // tm if mod_row(1) == mod_row(0) else 1
    const = lambda *shape: pl.BlockSpec(shape, lambda i, j: (0,) * len(shape))
    tok = lambda w: pl.BlockSpec((bs, tm, w), lambda i, j: (i, j, 0))
    tok_t = lambda r: pl.BlockSpec((bs, r, tm), lambda i, j: (i, 0, j))
    sd = jax.ShapeDtypeStruct
    return pl.pallas_call(
        _inproj_kernel,
        grid=(b // bs, t // tm),
        in_specs=[tok(D_MODEL),
                  pl.BlockSpec((1, N_MOD, D_MODEL), lambda i, j: (mod_row(i * bs), 0, 0)),
                  const(1, D_MODEL), const(D_MODEL, 1536), const(1024, D_MODEL),
                  const(D_MODEL, LANES), const(1, LANES)],
        out_specs=[tok(512), tok(512), tok(512), tok_t(512), tok_t(512), tok(LANES), tok_t(16)],
        out_shape=[sd((b, t, 512), BF16), sd((b, t, 512), BF16), sd((b, t, 512), BF16),
                   sd((b, 512, t), BF16), sd((b, 512, t), BF16), sd((b, t, LANES), F32),
                   sd((b, 16, t), F32)],
        compiler_params=_params("arbitrary", "arbitrary"),
        name="inproj",
    )(x, mod, g, wm, wt, wg, bgr)


def _log_sigmoid(x):
    return jnp.minimum(x, 0.0) - jnp.log1p(jnp.exp(-jnp.abs(x)))


def _scan_unit(st, k, qt, vt, u_col, u_row, b_row, btot, mask, s_prev, m_prev, use_state):
    dh = HEAD_DIM
    n = st.shape[0]
    ub = jnp.where(mask, jnp.broadcast_to(u_col, (n, n)), -jnp.inf)
    z = jnp.maximum(m_prev, jnp.max(ub, axis=0, keepdims=True))
    p = (jnp.exp(ub - z) * st).astype(BF16)
    ones = jnp.ones((N_ROWS, n), BF16)
    tot = _dot(jnp.concatenate([vt, ones], axis=0), p)
    if use_state:
        tot = tot + jnp.exp(m_prev - z) * _dot(s_prev.astype(BF16), qt)
    floor = jnp.exp(-(b_row + z))
    h_t = tot[:dh] / jnp.maximum(jnp.abs(tot[dh:dh + 1]), floor)
    g_row = btot + u_row
    m_new = jnp.maximum(btot + m_prev, jnp.max(g_row, axis=-1, keepdims=True))
    w_row = jnp.exp(g_row - m_new)
    vw = jnp.concatenate([(vt.astype(F32) * w_row).astype(BF16),
                          jnp.broadcast_to(w_row, (N_ROWS, n)).astype(BF16)], axis=0)
    s_new = jnp.exp(btot + m_prev - m_new) * s_prev + _dot(vw, k)
    return h_t.T, s_new, m_new


def _mlstm_kernel(*refs, nc, zero_init, emit_state):
    it = iter(refs)
    fwd_refs = tuple(next(it) for _ in range(5))
    bwd_refs = tuple(next(it) for _ in range(5)) if nc > 1 else fwd_refs
    if not zero_init:
        s0_ref, m0_ref = next(it), next(it)
    h_refs = (next(it), next(it))
    if emit_state:
        c_out, n_out, m_out = next(it), next(it), next(it)
    s_scr, m_scr = next(it), next(it)

    j = pl.program_id(1)
    n = CHUNK
    dh = HEAD_DIM

    @pl.when(j == 0)
    def _():
        if zero_init:
            s_scr[...] = jnp.zeros_like(s_scr)
            m_scr[...] = jnp.zeros_like(m_scr)
        else:
            s_scr[...] = s0_ref[0]
            m_scr[...] = m0_ref[0]

    rows = lax.broadcasted_iota(jnp.int32, (n, n), 0)
    cols = lax.broadcasted_iota(jnp.int32, (n, n), 1)
    le = rows <= cols
    ge = rows >= cols
    tri_le = le.astype(BF16)
    tri_ge = ge.astype(BF16)
    use_state = not (zero_init and nc == 1)

    def gate_terms(d):
        g_ref, gt_ref = (fwd_refs, bwd_refs)[d][3:5]
        gate = g_ref[0]
        gate_t = gt_ref[0]
        lf = _log_sigmoid(gate)
        lf_t = _log_sigmoid(gate_t)
        tri_c, tri_r = (tri_ge, tri_le) if d == 0 else (tri_le, tri_ge)
        bc = _dot(tri_c, jnp.concatenate(_split3(lf), axis=1))
        b_cols = bc[:, 0:128] + bc[:, 128:256] + bc[:, 256:384]
        br = _dot(jnp.concatenate(_split3(lf_t), axis=0), tri_r)
        b_rows = br[0:16] + br[16:32] + br[32:48]
        return gate, gate_t, b_cols, b_rows, jnp.sum(lf_t, axis=-1, keepdims=True)

    terms = [gate_terms(0), gate_terms(1)]
    hs = ([], [])
    for hd in range(HEADS):
        hsl = slice(hd * dh, (hd + 1) * dh)
        st = None
        for d in range(N_DIR):
            k_ref, qt_ref, vt_ref = (fwd_refs, bwd_refs)[d][0:3]
            gate, gate_t, b_cols, b_rows, tot_rows = terms[d]
            ci = d * 8 + hd
            cf = d * 8 + 4 + hd
            unit = d * HEADS + hd
            k = k_ref[0, :, hsl]
            qt = qt_ref[0, hsl, :]
            if st is None or nc > 1:
                st = _dot(k, qt)
            mask = le if d == 0 else ge
            h, s_new, m_new = _scan_unit(
                st, k, qt, vt_ref[0, hsl, :],
                gate[:, ci:ci + 1] - b_cols[:, cf:cf + 1],
                gate_t[ci:ci + 1, :] - b_rows[cf:cf + 1, :],
                b_rows[cf:cf + 1, :], tot_rows[cf:cf + 1, :],
                mask, s_scr[unit], m_scr[unit][:, 0:1], use_state)
            s_scr[unit] = s_new
            m_scr[unit] = jnp.broadcast_to(m_new, (1, LANES))
            hs[d].append(h)
    for d in range(N_DIR):
        h_refs[d][0] = jnp.concatenate(hs[d], axis=1).astype(BF16)

    if emit_state:
        @pl.when(j == nc - 1)
        def _():
            for unit in range(N_DIR * HEADS):
                s = s_scr[unit]
                c_out[0, unit] = s[:dh].T
                n_out[0, unit] = s[dh:dh + 1]
                m_out[0, unit] = m_scr[unit]


def _mlstm(k, qt, vt, gate, gate_t, s0, m0, emit_state):
    b, t, _ = k.shape
    nc = t // CHUNK
    zero_init = s0 is None
    nu = N_DIR * HEADS
    fwd = lambda w: pl.BlockSpec((1, CHUNK, w), lambda i, j: (i, j, 0))
    bwd = lambda w: pl.BlockSpec((1, CHUNK, w), lambda i, j: (i, nc - 1 - j, 0))
    fwd_t = lambda r: pl.BlockSpec((1, r, CHUNK), lambda i, j: (i, 0, j))
    bwd_t = lambda r: pl.BlockSpec((1, r, CHUNK), lambda i, j: (i, 0, nc - 1 - j))
    args = [k, qt, vt, gate, gate_t]
    in_specs = [fwd(512), fwd_t(512), fwd_t(512), fwd(LANES), fwd_t(16)]
    if nc > 1:
        args += [k, qt, vt, gate, gate_t]
        in_specs += [bwd(512), bwd_t(512), bwd_t(512), bwd(LANES), bwd_t(16)]
    if not zero_init:
        args += [s0, m0]
        in_specs += [pl.BlockSpec((1, nu, HEAD_DIM + N_ROWS, HEAD_DIM), lambda i, j: (i, 0, 0, 0)),
                     pl.BlockSpec((1, nu, 1, LANES), lambda i, j: (i, 0, 0, 0))]
    sd = jax.ShapeDtypeStruct
    out_shape = [sd((b, t, 512), BF16), sd((b, t, 512), BF16)]
    out_specs = [fwd(512), bwd(512)]
    if emit_state:
        out_shape += [sd((b, nu, HEAD_DIM, HEAD_DIM), F32), sd((b, nu, 1, HEAD_DIM), F32),
                      sd((b, nu, 1, LANES), F32)]
        out_specs += [pl.BlockSpec((1, nu, HEAD_DIM, HEAD_DIM), lambda i, j: (i, 0, 0, 0)),
                      pl.BlockSpec((1, nu, 1, HEAD_DIM), lambda i, j: (i, 0, 0, 0)),
                      pl.BlockSpec((1, nu, 1, LANES), lambda i, j: (i, 0, 0, 0))]
    return pl.pallas_call(
        functools.partial(_mlstm_kernel, nc=nc, zero_init=zero_init, emit_state=emit_state),
        grid=(b, nc),
        in_specs=in_specs,
        out_specs=out_specs,
        out_shape=out_shape,
        scratch_shapes=[pltpu.VMEM((nu, HEAD_DIM + N_ROWS, HEAD_DIM), F32),
                        pltpu.VMEM((nu, 1, LANES), F32)],
        compiler_params=_params("arbitrary", "arbitrary"),
        name="mlstm",
    )(*args)


def _outproj_kernel(x_ref, u_ref, hf_ref, hb_ref, o_ref, mod_ref, pm_ref, pinv_ref, wp_ref, ps_ref,
                    wo_ref, gpm_ref, gpf_ref, wr_ref, x1_ref, h2_ref, xp_ref, lg_ref):
    bs, tm, _ = x_ref.shape
    rows = bs * tm
    x = x_ref[...].reshape(rows, D_MODEL)
    mod = mod_ref[0]
    row0 = pl.multiple_of(pl.program_id(1) * tm, tm)
    diffs = []
    for g in range(POOL_GROUPS):
        sl = slice(g * POOL_GROUP_DIM, (g + 1) * POOL_GROUP_DIM)
        per_seq = []
        for bb in range(bs):
            box = _dot(pm_ref[g], u_ref[bb, :, sl])
            per_seq.append(box * pinv_ref[g] - u_ref[bb, pl.ds(row0, tm), sl].astype(F32))
        diffs.append(jnp.concatenate(per_seq, axis=0).astype(BF16))
    yps = [_dot(jnp.concatenate(diffs[2 * p:2 * p + 2], axis=1), wp_ref[p]) for p in range(POOL_GROUPS // 2)]
    y_pool = jnp.concatenate(yps, axis=1) * ps_ref[...]
    seq_rows = lambda ref: ref[...].reshape(rows, ref.shape[-1]).astype(F32)
    y_ml = jax.nn.sigmoid(seq_rows(o_ref)) * (seq_rows(hf_ref) + seq_rows(hb_ref))
    mix = _dot(jnp.concatenate([y_pool, y_ml], axis=1).astype(BF16), wo_ref[...])
    x1 = x + mod[2:3] * _rmsnorm(mix, gpm_ref[...])
    x1_ref[...] = x1.reshape(bs, tm, D_MODEL)
    h2 = _rmsnorm(x1, gpf_ref[...]) * (1.0 + mod[4:5]) + mod[3:4]
    p1, p2, p3 = _split3(h2)
    h2_ref[...] = p1.reshape(bs, tm, D_MODEL)
    nch = D_MODEL // LANES
    for g in range(rows // 8):
        cols = jnp.stack([h2[g * 8:(g + 1) * 8, cc * LANES:(cc + 1) * LANES] for cc in range(nch)], axis=0)
        bb, r0 = divmod(g * 8, tm)
        xp_ref[bb, r0:r0 + 8] = jnp.swapaxes(cols, 0, 1).astype(BF16)
    r = _dot(jnp.concatenate([p1, p2, p3], axis=0), wr_ref[...])
    r12 = r[0:rows] + r[rows:2 * rows]
    ne = N_EXPERTS
    lg = (r12 + r[2 * rows:])[:, 0:ne] + r12[:, ne:2 * ne] + r[0:rows, 2 * ne:3 * ne]
    lg_ref[...] = jnp.concatenate([lg, jnp.zeros_like(lg)], axis=1).T[0:ne]


def _outproj(x, u, hf, hb, o, mod, mod_row, pm, pinv, wp, ps, wo, gpm, gpf, wr):
    b, t, _ = x.shape
    tm = min(t, 2 * TOKEN_TILE)
    bs = 2 * TOKEN_TILE // tm if mod_row(1) == mod_row(0) else 1
    const = lambda *shape: pl.BlockSpec(shape, lambda i, j: (0,) * len(shape))
    tok = lambda w: pl.BlockSpec((bs, tm, w), lambda i, j: (i, j, 0))
    sd = jax.ShapeDtypeStruct
    return pl.pallas_call(
        _outproj_kernel,
        grid=(b // bs, t // tm),
        in_specs=[tok(D_MODEL),
                  pl.BlockSpec((bs, t, 512), lambda i, j: (i, 0, 0)),
                  tok(512), tok(512), tok(512),
                  pl.BlockSpec((1, N_MOD, D_MODEL), lambda i, j: (mod_row(i * bs), 0, 0)),
                  pl.BlockSpec((POOL_GROUPS, tm, t), lambda i, j: (0, j, 0)),
                  pl.BlockSpec((POOL_GROUPS, tm, 1), lambda i, j: (0, j, 0)),
                  const(POOL_GROUPS // 2, 2 * POOL_GROUP_DIM, 2 * POOL_GROUP_DIM), const(1, POOL_WIDTH),
                  const(D_MODEL, D_MODEL), const(1, D_MODEL), const(1, D_MODEL),
                  const(D_MODEL, 2 * LANES)],
        out_specs=[tok(D_MODEL), tok(D_MODEL),
                   pl.BlockSpec((bs, tm, D_MODEL // LANES, LANES), lambda i, j: (i, j, 0, 0)),
                   pl.BlockSpec((N_EXPERTS, bs * tm), lambda i, j: (0, i * (t // tm) + j))],
        out_shape=[sd((b, t, D_MODEL), F32), sd((b, t, D_MODEL), BF16),
                   sd((b, t, D_MODEL // LANES, LANES), BF16), sd((N_EXPERTS, b * t), F32)],
        compiler_params=_params("arbitrary", "arbitrary"),
        name="outproj",
    )(x, u, hf, hb, o, mod, pm, pinv, wp, ps, wo, gpm, gpf, wr)


def _router_kernel(lg_ref, br_ref, comb_ref, sel_ref, *, n_blocks):
    ng, gs = N_EXPERT_GROUPS, GROUP_SIZE
    neg = -jnp.inf
    lg = jnp.swapaxes(lg_ref[...], 0, 1)
    br = br_ref[...]
    s = [jax.nn.sigmoid(lg[j]) for j in range(gs)]
    biased = [s[j] + br[:, j, :] for j in range(gs)]
    fold = lambda op, xs: functools.reduce(op, xs)
    m1 = fold(jnp.maximum, biased)
    i1 = fold(jnp.minimum, [jnp.where(biased[j] == m1, j, gs) for j in range(gs)])
    m2 = fold(jnp.maximum, [jnp.where(i1 == j, neg, biased[j]) for j in range(gs)])
    cur = m1 + m2
    gi = lax.broadcasted_iota(jnp.int32, cur.shape, 0)
    gmask = jnp.zeros(cur.shape, F32)
    for _ in range(TOPK_GROUPS):
        mx = jnp.max(cur, axis=0, keepdims=True)
        ix = jnp.min(jnp.where(cur == mx, gi, ng), axis=0, keepdims=True)
        hit = gi == ix
        gmask = jnp.where(hit, 1.0, gmask)
        cur = jnp.where(hit, neg, cur)
    cand = [jnp.where(gmask > 0, biased[j], neg) for j in range(gs)]
    eidx = [gi * gs + j for j in range(gs)]
    selm = [jnp.zeros(cur.shape, F32) for _ in range(gs)]
    for _ in range(TOP_K):
        mx = jnp.max(fold(jnp.maximum, cand), axis=0, keepdims=True)
        ix = jnp.min(fold(jnp.minimum, [jnp.where(cand[j] == mx, eidx[j], N_EXPERTS) for j in range(gs)]),
                     axis=0, keepdims=True)
        for j in range(gs):
            hit = eidx[j] == ix
            selm[j] = jnp.where(hit, 1.0, selm[j])
            cand[j] = jnp.where(hit, neg, cand[j])
    sel = [selm[j] * s[j] for j in range(gs)]
    tot = jnp.sum(fold(jnp.add, sel), axis=0, keepdims=True)
    comb = [sel[j] / tot * ROUTED_SCALE for j in range(gs)]
    sel_ref[...] = jnp.swapaxes(jnp.stack(selm, axis=0), 0, 1)
    comb_e = jnp.swapaxes(jnp.stack(comb, axis=0), 0, 1).reshape(N_EXPERTS, -1)
    comb_t = jnp.concatenate([comb_e, jnp.zeros_like(comb_e)], axis=0).T
    comb_ref[...] = jnp.where(pl.program_id(0) < n_blocks, comb_t, 0.0)


def _router(logits_t, b_router):
    t = logits_t.shape[1]
    tl = 1024
    nb = t // tl
    shp = (N_EXPERT_GROUPS, GROUP_SIZE, t)
    blk = pl.BlockSpec((N_EXPERT_GROUPS, GROUP_SIZE, tl), lambda j: (0, 0, jnp.minimum(j, nb - 1)))
    comb, sel = pl.pallas_call(
        functools.partial(_router_kernel, n_blocks=nb),
        grid=(nb + 1,),
        in_specs=[blk, pl.BlockSpec((N_EXPERT_GROUPS, GROUP_SIZE, 1), lambda j: (0, 0, 0))],
        out_specs=[pl.BlockSpec((tl, LANES), lambda j: (j, 0)), blk],
        out_shape=[jax.ShapeDtypeStruct((t + tl, LANES), F32), jax.ShapeDtypeStruct(shp, F32)],
        compiler_params=_params("arbitrary"),
        name="router",
    )(logits_t.reshape(shp), b_router.reshape(N_EXPERT_GROUPS, GROUP_SIZE, 1))
    return comb, sel.reshape(N_EXPERTS, t)


def _plan_kernel(sel_ref, pos_ref, meta_ref, emeta_ref, *, n_meta):
    t = sel_ref.shape[1]
    tm = float(MOE_TILE)
    sel = sel_ref[...]
    selb = sel.astype(BF16)
    blk = 256
    rr = lax.broadcasted_iota(jnp.int32, (blk, blk), 0)
    cc = lax.broadcasted_iota(jnp.int32, (blk, blk), 1)
    before = (rr < cc).astype(BF16)
    carry = jnp.zeros((N_EXPERTS, 1), F32)
    ranks = []
    for b in range(t // blk):
        sb = selb[:, b * blk:(b + 1) * blk]
        ranks.append(_dot(sb, before) + carry)
        carry = carry + jnp.sum(sel[:, b * blk:(b + 1) * blk], axis=1, keepdims=True)
    rank = jnp.concatenate(ranks, axis=1)
    cnt = carry
    ntile = jnp.floor((cnt + (tm - 0.5)) * (1.0 / tm))
    er = lax.broadcasted_iota(jnp.int32, (N_EXPERTS, N_EXPERTS), 0)
    ec = lax.broadcasted_iota(jnp.int32, (N_EXPERTS, N_EXPERTS), 1)
    below = (ec < er).astype(BF16)
    tstart = _dot(below, jnp.broadcast_to(ntile, (N_EXPERTS, LANES)).astype(BF16))[:, 0:1]
    pos = tstart * tm + rank
    erank = _dot(below, selb)
    rows = []
    for k in range(TOP_K):
        hit = (sel > 0.0) & (erank == float(k))
        rows.append(jnp.sum(jnp.where(hit, pos, 0.0), axis=0, keepdims=True))
    rows += [jnp.zeros((1, t), F32)] * (8 - TOP_K)
    pos_ref[...] = jnp.concatenate(rows, axis=0).astype(jnp.int32)

    tau = lax.broadcasted_iota(jnp.int32, (N_EXPERTS, n_meta), 1).astype(F32)
    eidx = lax.broadcasted_iota(jnp.int32, (N_EXPERTS, n_meta), 0).astype(F32)
    te = jnp.sum(((tstart + ntile) <= tau).astype(F32), axis=0, keepdims=True)
    te = jnp.minimum(te, float(N_EXPERTS - 1))
    onehot = eidx == te
    cnt_t = jnp.sum(jnp.where(onehot, cnt, 0.0), axis=0, keepdims=True)
    ts_t = jnp.sum(jnp.where(onehot, tstart, 0.0), axis=0, keepdims=True)
    tr = jnp.clip(cnt_t - (tau[0:1] - ts_t) * tm, 0.0, tm)
    tf = jnp.where((tau[0:1] == ts_t) & (tr > 0.0), 1.0, 0.0)
    meta_ref[...] = jnp.concatenate([te, tr, tf] + [jnp.zeros((1, n_meta), F32)] * 5, axis=0).astype(jnp.int32)

    eye = (lax.broadcasted_iota(jnp.int32, (N_EXPERTS, LANES), 0)
           == lax.broadcasted_iota(jnp.int32, (N_EXPERTS, LANES), 1))
    as_row = lambda col: jnp.sum(jnp.where(eye, col, 0.0), axis=0, keepdims=True)
    total = jnp.broadcast_to(jnp.sum(ntile, axis=0, keepdims=True), (1, LANES))
    emeta_ref[...] = jnp.concatenate([as_row(cnt), as_row(tstart), as_row(ntile), total]
                                     + [jnp.zeros((1, LANES), F32)] * 4, axis=0).astype(jnp.int32)


def _plan(sel, n_meta):
    t = sel.shape[1]
    sd = jax.ShapeDtypeStruct
    return pl.pallas_call(
        functools.partial(_plan_kernel, n_meta=n_meta),
        out_shape=[sd((8, t), jnp.int32), sd((8, n_meta), jnp.int32), sd((8, LANES), jnp.int32)],
        compiler_params=pltpu.CompilerParams(vmem_limit_bytes=VMEM_LIMIT),
        name="plan",
    )(sel)


def _fill_slot_table(pos_ref, emeta_ref, tbl_ref, n_tokens):
    group = 32
    def pad_expert(e, carry):
        n_t = emeta_ref[2 * LANES + e]
        start = (emeta_ref[LANES + e] + n_t - 1) * MOE_TILE

        @pl.when(n_t > 0)
        def _():
            def put(i, c):
                for u in range(group):
                    tbl_ref[start + i * group + u] = n_tokens
                return c
            lax.fori_loop(0, MOE_TILE // group, put, 0)
        return carry
    lax.fori_loop(0, N_EXPERTS, pad_expert, 0)
    last = N_EXPERTS - 1
    def pad_tile(tile, carry):
        def put(i, c):
            for u in range(group):
                tbl_ref[tile * MOE_TILE + i * group + u] = n_tokens
            return c
        lax.fori_loop(0, MOE_TILE // group, put, 0)
        return carry
    lax.fori_loop(emeta_ref[LANES + last] + emeta_ref[2 * LANES + last], tbl_ref.shape[0] // MOE_TILE, pad_tile, 0)

    unroll = 8
    def scatter(i, carry):
        first = i * (8 * unroll)
        for u in range(unroll):
            for k in range(TOP_K):
                tbl_ref[pos_ref[first + (8 * u + k)]] = i * unroll + u
        return carry
    lax.fori_loop(0, n_tokens // unroll, scatter, 0)


def _experts_kernel(te_ref, tr_ref, pos_ref, emeta_ref, xc_ref, xs_ref, comb_ref, wgu_ref, wd_ref, acc_out,
                    tbl_ref, xbuf, acc, stage0, stage1, cst0, cst1, act0, act1, ybuf0, ybuf1,
                    wgu_b0, wgu_b1, wd_b0, wd_b1, sems):
    s = pl.program_id(0)
    n_tiles = te_ref.shape[0]
    tm = MOE_TILE
    nch = D_MODEL // LANES
    tile_at = lambda lag: jnp.clip(s - lag, 0, n_tiles - 1)
    t_g, t_1, t_2, t_3 = tile_at(0), tile_at(1), tile_at(2), tile_at(3)

    @pl.when(s == 0)
    def _():
        tc, ts = xc_ref.shape[0], xs_ref.shape[0]
        copies = (pltpu.make_async_copy(xc_ref, xbuf.at[pl.ds(0, tc)], sems.at[0]),
                  pltpu.make_async_copy(xs_ref, xbuf.at[pl.ds(tc, ts)], sems.at[1]))
        for cp in copies:
            cp.start()
        n_pad = xbuf.shape[0] - tc - ts
        xbuf[pl.ds(tc + ts, n_pad)] = jnp.zeros((n_pad,) + xbuf.shape[1:], BF16)
        for ref in (acc, stage0, stage1, cst0, cst1, act0, act1, ybuf0, ybuf1, wgu_b0, wgu_b1, wd_b0, wd_b1):
            ref[...] = jnp.zeros_like(ref)
        _fill_slot_table(pos_ref, emeta_ref, tbl_ref, tc + ts)
        for cp in copies:
            cp.wait()

    def gather(tile, stage, cst):
        base = tile * tm
        for j in range(tm):
            tok = tbl_ref[base + j]
            stage[pl.ds(j * nch, nch), :] = xbuf[tok].astype(F32)
            cst[pl.ds(j, 1), :] = comb_ref[pl.ds(tok, 1), :]

    def gate_up(tile, stage, cst, wgu_b, act):
        xb = _token_rows_from_slabs(
            lambda g: stage[g * 8 * nch:(g + 1) * 8 * nch, :].reshape(8, nch, LANES), tm).astype(BF16)
        gu = _dot(xb, wgu_b[...])
        lane = lax.broadcasted_iota(jnp.int32, (1, LANES), 1)
        w_col = jnp.sum(jnp.where(lane == te_ref[tile], cst[...], 0.0), axis=1, keepdims=True)
        act[...] = (_silu(gu[:, :EXPERT_DIM]) * gu[:, EXPERT_DIM:] * w_col).astype(BF16)

    def down(act, wd_b, ybuf):
        y = _dot(act[...], wd_b[...])
        for cc in range(nch):
            ybuf[cc * Y_PITCH:cc * Y_PITCH + tm, :] = y[:, cc * LANES:(cc + 1) * LANES]

    def scatter(tile, ybuf):
        base = tile * tm
        sc_n = 16
        for i in range(tm // sc_n):
            toks = [tbl_ref[base + i * sc_n + u] for u in range(sc_n)]
            olds = [acc[toks[u]] for u in range(sc_n)]
            news = [olds[u] + ybuf[pl.ds(i * sc_n + u, nch, stride=Y_PITCH), :] for u in range(sc_n)]
            for u in range(sc_n):
                acc[toks[u]] = news[u]

    busy = (tr_ref[t_g] + tr_ref[t_1] + tr_ref[t_2] + tr_ref[t_3]) > 0
    bufs = ((stage0, cst0, act0, ybuf0, wgu_b0, wd_b0), (stage1, cst1, act1, ybuf1, wgu_b1, wd_b1))
    for par in range(2):
        stage_p, cst_p, act_p, ybuf_p, wgu_p, wd_p = bufs[par]
        stage_q, cst_q, act_q, ybuf_q, wgu_q, wd_q = bufs[1 - par]

        @pl.when(busy & (s % 2 == par))
        def _():
            gather(t_g, stage_p, cst_p)
            wgu_p[...] = wgu_ref[0].astype(BF16)
            gate_up(t_1, stage_q, cst_q, wgu_q, act_q)
            wd_q[...] = wd_ref[0].astype(BF16)
            down(act_p, wd_p, ybuf_p)
            scatter(t_3, ybuf_q)

    @pl.when(s == pl.num_programs(0) - 1)
    def _():
        cp = pltpu.make_async_copy(acc, acc_out, sems.at[2])
        cp.start()
        cp.wait()


def _experts(te, tr, pos, emeta, xc, xs, comb, wgu, wd, n_used):
    n_tok = xc.shape[0] + xs.shape[0] + 8
    n_tiles = te.shape[0]
    tm = MOE_TILE
    nch = D_MODEL // LANES
    vm = pltpu.VMEM
    grid_spec = pltpu.PrefetchScalarGridSpec(
        num_scalar_prefetch=4,
        grid=(n_used + 3,),
        in_specs=[pl.BlockSpec(memory_space=pl.ANY), pl.BlockSpec(memory_space=pl.ANY),
                  pl.BlockSpec((n_tok, LANES), lambda s, *_: (0, 0), pipeline_mode=pl.Buffered(1)),
                  pl.BlockSpec((1, D_MODEL, 2 * EXPERT_DIM), lambda s, te, *_: (te[jnp.minimum(s, n_tiles - 1)], 0, 0)),
                  pl.BlockSpec((1, EXPERT_DIM, D_MODEL), lambda s, te, *_: (te[jnp.clip(s - 1, 0, n_tiles - 1)], 0, 0))],
        out_specs=pl.BlockSpec(memory_space=pl.ANY),
        scratch_shapes=[pltpu.SMEM((n_tiles * tm,), jnp.int32),
                        vm((n_tok, nch, LANES), BF16), vm((n_tok, nch, LANES), F32),
                        vm((tm * nch, LANES), F32), vm((tm * nch, LANES), F32),
                        vm((tm, LANES), F32), vm((tm, LANES), F32),
                        vm((tm, EXPERT_DIM), BF16), vm((tm, EXPERT_DIM), BF16),
                        vm((nch * Y_PITCH, LANES), F32), vm((nch * Y_PITCH, LANES), F32),
                        vm((D_MODEL, 2 * EXPERT_DIM), BF16), vm((D_MODEL, 2 * EXPERT_DIM), BF16),
                        vm((EXPERT_DIM, D_MODEL), BF16), vm((EXPERT_DIM, D_MODEL), BF16),
                        pltpu.SemaphoreType.DMA((3,))],
    )
    return pl.pallas_call(
        _experts_kernel,
        grid_spec=grid_spec,
        out_shape=jax.ShapeDtypeStruct((n_tok, nch, LANES), F32),
        compiler_params=pltpu.CompilerParams(dimension_semantics=("arbitrary",),
                                             vmem_limit_bytes=EXPERTS_VMEM_LIMIT),
        name="experts",
    )(te, tr, pos, emeta, xc, xs, comb, wgu, wd)


def _final_kernel(acc_ref, h_ref, x1_ref, mod_ref, wsg_ref, wsd_ref, gpo_ref, out_ref):
    routed = _token_rows_from_slabs(lambda g: acc_ref[g * 8:(g + 1) * 8], acc_ref.shape[0])
    gs = _dot(h_ref[...], wsg_ref[...])
    act = _silu(gs[:, :SHARED_DIM]) * gs[:, SHARED_DIM:]
    f = routed + _dot(act.astype(BF16), wsd_ref[...])
    out_ref[...] = x1_ref[...] + mod_ref[0][5:6] * _rmsnorm(f, gpo_ref[...])


def _final(acc, tile0, h2, x1, mod, mod_row, wsg, wsd, gpo):
    n = h2.shape[0]
    tm = FINAL_TILE
    const = lambda *shape: pl.BlockSpec(shape, lambda i: (0,) * len(shape))
    tok = lambda w: pl.BlockSpec((tm, w), lambda i: (i, 0))
    return pl.pallas_call(
        _final_kernel,
        grid=(n // tm,),
        in_specs=[pl.BlockSpec((tm, D_MODEL // LANES, LANES), lambda i: (i + tile0, 0, 0)),
                  tok(D_MODEL), tok(D_MODEL),
                  pl.BlockSpec((1, N_MOD, D_MODEL), lambda i: (mod_row(i), 0, 0)),
                  const(D_MODEL, 2 * SHARED_DIM), const(SHARED_DIM, D_MODEL), const(1, D_MODEL)],
        out_specs=tok(D_MODEL),
        out_shape=jax.ShapeDtypeStruct((n, D_MODEL), F32),
        compiler_params=_params("arbitrary"),
        name="final",
    )(acc, h2, x1, mod, wsg, wsd, gpo)


def _window_bounds(n, w):
    idx = np.arange(n)
    return np.clip(idx - w // 2, 0, n), np.clip(idx + w - w // 2, 0, n)


def _pool_operators(t, grid):
    mats, invs = [], []
    for w in POOL_WINDOWS:
        if grid:
            rlo, rhi = _window_bounds(t // GRID_W, w)
            clo, chi = _window_bounds(GRID_W, w)
            r = np.arange(t) // GRID_W
            c = np.arange(t) % GRID_W
            m = ((r[None, :] >= rlo[r][:, None]) & (r[None, :] < rhi[r][:, None])
                 & (c[None, :] >= clo[c][:, None]) & (c[None, :] < chi[c][:, None]))
            cnt = (rhi - rlo)[r] * (chi - clo)[c]
        else:
            lo, hi = _window_bounds(t, w)
            sidx = np.arange(t)
            m = (sidx[None, :] >= lo[:, None]) & (sidx[None, :] < hi[:, None])
            cnt = hi - lo
        mats.append(m.astype(np.float32))
        invs.append((1.0 / cnt.astype(np.float64)).astype(np.float32)[:, None])
    return jnp.asarray(np.stack(mats), BF16), jnp.asarray(np.stack(invs), F32)


def kernel(x_prompt, x_sample, state_C, state_n, state_m, c, c_ctx, w_ada, b_ada, g_pre_mix, w_in, b_gate,
           w_pool, pool_scale, w_out, g_post_mix, g_pre_ffn, w_router, b_router, w_expert_gu, w_expert_down,
           w_shared_gu, w_shared_down, g_post_ffn):
    b_ctx = x_prompt.shape[0]
    b_lat = x_sample.shape[0]
    nu = N_DIR * HEADS
    l = 0
    row = lambda a: a[l].reshape(1, -1).astype(F32)

    cvec = jnp.zeros((16, D_MODEL), F32).at[0].set(c_ctx.astype(F32)).at[1:1 + b_lat].set(c.astype(F32))
    mod = _mod_rows(cvec, w_ada[l], b_ada[l]).reshape(16, N_MOD, D_MODEL)

    w_in_l = w_in[l]
    p0 = POOL_WIDTH
    mw = MLSTM_WIDTH
    w_u, w_q, w_k, w_v, w_o = (w_in_l[:, lo:lo + 512] for lo in (0, p0, p0 + mw, p0 + 2 * mw, p0 + 3 * mw))
    wm = jnp.concatenate([w_u, w_k, w_o], axis=1).astype(BF16)
    wt = jnp.concatenate([w_q.T, w_v.T], axis=0).astype(BF16)
    wg_cols = w_in_l[:, p0 + 4 * mw:]
    wg = jnp.pad(jnp.concatenate(_split2(wg_cols), axis=1), ((0, 0), (0, LANES - 2 * GATE_COLS)))
    bg = b_gate[l].reshape(GATE_COLS).astype(F32)
    bgr = jnp.pad(bg, (0, LANES - GATE_COLS)).reshape(1, LANES)
    wpl = w_pool[l].astype(BF16)
    zg = jnp.zeros((POOL_GROUP_DIM, POOL_GROUP_DIM), BF16)
    wp = jnp.stack([jnp.block([[wpl[2 * p], zg], [zg, wpl[2 * p + 1]]]) for p in range(POOL_GROUPS // 2)])
    wo = w_out[l].astype(BF16)
    wr = jnp.pad(jnp.concatenate(_split3(w_router[l].astype(F32)), axis=1), ((0, 0), (0, 2 * LANES - 3 * N_EXPERTS)))
    wsg = w_shared_gu[l].astype(BF16)
    wsd = w_shared_down[l].astype(BF16)

    def mixer(x, mod_row, grid, s0, m0, emit_state):
        t = x.shape[1]
        u, k, o, qt, vt, gate, gate_t = _inproj(x.astype(F32), mod, mod_row, row(g_pre_mix), wm, wt, wg, bgr)
        outs = _mlstm(k, qt, vt, gate, gate_t, s0, m0, emit_state)
        hf, hb = outs[0], outs[1]
        pm, pinv = _pool_operators(t, grid)
        x1, h2, xp, lg = _outproj(x.astype(F32), u, hf, hb, o, mod, mod_row, pm, pinv, wp, row(pool_scale), wo,
                                  row(g_post_mix), row(g_pre_ffn), wr)
        return x1, h2, xp, lg, outs[2:]

    ctx_row = lambda i: 0
    lat_row = lambda i: i + 1
    x1c, h2c, xpc, lgc, (c_new, n_new, m_new) = mixer(x_prompt, ctx_row, False, None, None, True)
    s0 = jnp.concatenate(
        [jnp.swapaxes(state_C[:, l].reshape(b_lat, nu, HEAD_DIM, HEAD_DIM).astype(F32), -1, -2),
         jnp.broadcast_to(state_n[:, l].reshape(b_lat, nu, 1, HEAD_DIM).astype(F32),
                          (b_lat, nu, N_ROWS, HEAD_DIM))], axis=-2)
    m0 = jnp.broadcast_to(state_m[:, l].reshape(b_lat, nu, 1, 1).astype(F32), (b_lat, nu, 1, LANES))
    x1s, h2s, xps, lgs, _ = mixer(x_sample, lat_row, True, s0, m0, False)

    tc = b_ctx * x_prompt.shape[1]
    ts = b_lat * x_sample.shape[1]
    n_tok = tc + ts
    lg_all = jnp.concatenate([lgc, lgs], axis=1)
    comb_tok, sel = _router(lg_all, b_router[l].astype(F32))
    n_tiles = n_tok * TOP_K // MOE_TILE + N_EXPERTS
    n_meta = -(-n_tiles // LANES) * LANES
    pos, meta, emeta = _plan(sel, n_meta)

    slab = (D_MODEL // LANES, LANES)
    acc = _experts(meta[0, :n_tiles], meta[1, :n_tiles],
                   pos.T.reshape(-1), emeta.reshape(-1),
                   xpc.reshape((tc,) + slab), xps.reshape((ts,) + slab), comb_tok,
                   w_expert_gu[l], w_expert_down[l], emeta[3, 0])

    fin = functools.partial(_final, wsg=wsg, wsd=wsd, gpo=row(g_post_ffn))
    tiles_per_lat = x_sample.shape[1] // FINAL_TILE
    yc = fin(acc, 0, h2c.reshape(tc, D_MODEL), x1c.reshape(tc, D_MODEL), mod, ctx_row)
    ys = fin(acc, tc // FINAL_TILE, h2s.reshape(ts, D_MODEL), x1s.reshape(ts, D_MODEL), mod,
             lambda i: i // tiles_per_lat + 1)

    new_c = c_new.reshape(b_ctx, 1, N_DIR, HEADS, HEAD_DIM, HEAD_DIM)
    new_n = n_new.reshape(b_ctx, 1, N_DIR, HEADS, HEAD_DIM)
    new_m = m_new[..., 0].reshape(b_ctx, 1, N_DIR, HEADS)
    return (yc.reshape(x_prompt.shape), ys.reshape(x_sample.shape), new_c, new_n, new_m)
```

```python
import functools

import jax
import jax.numpy as jnp
import numpy as np
from jax import lax
from jax.experimental import pallas as pl
from jax.experimental.pallas import tpu as pltpu
from jax.experimental.pallas import tpu_sc as plsc

F32 = jnp.float32
BF16 = jnp.bfloat16

D_MODEL = 1024
GRID_W = 64
POOL_WIDTH = 512
POOL_GROUPS = 4
POOL_GROUP_DIM = 128
POOL_WINDOWS = (2, 4, 8, 16)
HEADS = 4
HEAD_DIM = 128
MLSTM_WIDTH = HEADS * HEAD_DIM
N_DIR = 2
GATE_COLS = N_DIR * 2 * HEADS
N_EXPERTS = 64
TOP_K = 6
N_EXPERT_GROUPS = 8
GROUP_SIZE = N_EXPERTS // N_EXPERT_GROUPS
TOPK_GROUPS = 4
EXPERT_DIM = 256
SHARED_DIM = 256
ROUTED_SCALE = 2.5
N_MOD = 6
EPS = 1e-6
K_SCALE = HEAD_DIM ** -0.5

LANES = 128
CHUNK = 256
N_ROWS = 16
TOKEN_TILE = 256
FINAL_TILE = 512
MOE_TILE = 320
Y_PITCH = MOE_TILE + 8
VMEM_LIMIT = 56 * 1024 * 1024
SC_CORES = 2
SC_SUBCORES = 16
SC_LANES = 16
EXPERTS_VMEM_LIMIT = 58 * 1024 * 1024


def _split3(x):
    p1 = x.astype(BF16)
    r1 = x - p1.astype(F32)
    p2 = r1.astype(BF16)
    p3 = (r1 - p2.astype(F32)).astype(BF16)
    return p1, p2, p3


def _split2(x):
    p1 = x.astype(BF16)
    p2 = (x - p1.astype(F32)).astype(BF16)
    return p1, p2


def _dot(a, b):
    return jnp.dot(a, b, preferred_element_type=F32)


def _dot_nt(a, b):
    return lax.dot_general(a, b, (((1,), (1,)), ((), ())), preferred_element_type=F32)


def _rmsnorm(x, g):
    return x * lax.rsqrt(jnp.mean(x * x, axis=-1, keepdims=True) + EPS) * g


def _silu(x):
    return x * jax.nn.sigmoid(x)


def _token_rows_from_slabs(read_block, n_tok):
    nch = D_MODEL // LANES
    cols = [[] for _ in range(nch)]
    for g in range(n_tok // 8):
        blk = jnp.swapaxes(read_block(g), 0, 1)
        for cc in range(nch):
            cols[cc].append(blk[cc])
    return jnp.concatenate([jnp.concatenate(c, axis=0) for c in cols], axis=1)


def _params(*sem):
    return pltpu.CompilerParams(dimension_semantics=sem, vmem_limit_bytes=VMEM_LIMIT)


def _mod_kernel(c_ref, w_ref, b_ref, o_ref):
    a = _silu(c_ref[...])
    a_stack = jnp.concatenate(_split3(a), axis=0)
    w1, w2 = _split2(w_ref[...])
    r1 = _dot(a_stack, w1)
    r2 = _dot(a_stack[:32], w2)
    o_ref[...] = (r1[0:16] + r1[16:32] + r1[32:48] + r2[0:16] + r2[16:32]) + b_ref[...]


def _mod_rows(cvec, w_ada, b_ada):
    n = N_MOD * D_MODEL
    tn = 1536
    return pl.pallas_call(
        _mod_kernel,
        grid=(n // tn,),
        in_specs=[pl.BlockSpec((16, D_MODEL), lambda j: (0, 0)),
                  pl.BlockSpec((D_MODEL, tn), lambda j: (0, j)),
                  pl.BlockSpec((1, tn), lambda j: (0, j))],
        out_specs=pl.BlockSpec((16, tn), lambda j: (0, j)),
        out_shape=jax.ShapeDtypeStruct((16, n), F32),
        compiler_params=_params("arbitrary"),
        name="mod",
    )(cvec, w_ada, b_ada.reshape(1, n))


def _inproj_kernel(x_ref, mod_ref, g_ref, wm_ref, wt_ref, wg_ref, bgr_ref,
                   u_ref, k_ref, o_ref, qt_ref, vt_ref, gate_ref, gatet_ref):
    bs, tm, _ = x_ref.shape
    rows = bs * tm
    x = x_ref[...].reshape(rows, D_MODEL)
    mod = mod_ref[0]
    h = _rmsnorm(x, g_ref[...]) * (1.0 + mod[1:2]) + mod[0:1]
    h1, h2, h3 = _split3(h)
    z = _dot(h1, wm_ref[...])
    u_ref[...] = z[:, 0:512].astype(BF16).reshape(bs, tm, 512)
    k_ref[...] = (z[:, 512:1024] * K_SCALE).astype(BF16).reshape(bs, tm, 512)
    o_ref[...] = z[:, 1024:1536].astype(BF16).reshape(bs, tm, 512)
    zt = _dot_nt(wt_ref[...], h1).astype(BF16)
    r = _dot(jnp.concatenate([h1, h2, h3], axis=0), wg_ref[...])
    r12 = r[0:rows] + r[rows:2 * rows]
    gate = (r12 + r[2 * rows:]) + pltpu.roll(r12, LANES - GATE_COLS, axis=1) + bgr_ref[...]
    gate_ref[...] = gate.reshape(bs, tm, LANES)
    gate_t = gate.T
    for bb in range(bs):
        cols = slice(bb * tm, (bb + 1) * tm)
        qt_ref[bb] = zt[0:512, cols]
        vt_ref[bb] = zt[512:1024, cols]
        gatet_ref[bb] = gate_t[0:16, cols]


def _inproj(x, mod, mod_row, g, wm, wt, wg, bgr):
    b, t, _ = x.shape
    tm = min(t, 2 * TOKEN_TILE)
    bs = 2 * TOKEN_TILE // tm if mod_row(1) == mod_row(0) else 1
    const = lambda *shape: pl.BlockSpec(shape, lambda i, j: (0,) * len(shape))
    tok = lambda w: pl.BlockSpec((bs, tm, w), lambda i, j: (i, j, 0))
    tok_t = lambda r: pl.BlockSpec((bs, r, tm), lambda i, j: (i, 0, j))
    sd = jax.ShapeDtypeStruct
    return pl.pallas_call(
        _inproj_kernel,
        grid=(b // bs, t // tm),
        in_specs=[tok(D_MODEL),
                  pl.BlockSpec((1, N_MOD, D_MODEL), lambda i, j: (mod_row(i * bs), 0, 0)),
                  const(1, D_MODEL), const(D_MODEL, 1536), const(1024, D_MODEL),
                  const(D_MODEL, LANES), const(1, LANES)],
        out_specs=[tok(512), tok(512), tok(512), tok_t(512), tok_t(512), tok(LANES), tok_t(16)],
        out_shape=[sd((b, t, 512), BF16), sd((b, t, 512), BF16), sd((b, t, 512), BF16),
                   sd((b, 512, t), BF16), sd((b, 512, t), BF16), sd((b, t, LANES), F32),
                   sd((b, 16, t), F32)],
        compiler_params=_params("arbitrary", "arbitrary"),
        name="inproj",
    )(x, mod, g, wm, wt, wg, bgr)


def _log_sigmoid(x):
    return jnp.minimum(x, 0.0) - jnp.log1p(jnp.exp(-jnp.abs(x)))


def _scan_unit(st, k, qt, vt, u_col, u_row, b_row, btot, mask, s_prev, m_prev, use_state):
    dh = HEAD_DIM
    n = st.shape[0]
    ub = jnp.where(mask, jnp.broadcast_to(u_col, (n, n)), -jnp.inf)
    z = jnp.maximum(m_prev, jnp.max(ub, axis=0, keepdims=True))
    p = (jnp.exp(ub - z) * st).astype(BF16)
    ones = jnp.ones((N_ROWS, n), BF16)
    tot = _dot(jnp.concatenate([vt, ones], axis=0), p)
    if use_state:
        tot = tot + jnp.exp(m_prev - z) * _dot(s_prev.astype(BF16), qt)
    floor = jnp.exp(-(b_row + z))
    h_t = tot[:dh] / jnp.maximum(jnp.abs(tot[dh:dh + 1]), floor)
    g_row = btot + u_row
    m_new = jnp.maximum(btot + m_prev, jnp.max(g_row, axis=-1, keepdims=True))
    w_row = jnp.exp(g_row - m_new)
    vw = jnp.concatenate([(vt.astype(F32) * w_row).astype(BF16),
                          jnp.broadcast_to(w_row, (N_ROWS, n)).astype(BF16)], axis=0)
    s_new = jnp.exp(btot + m_prev - m_new) * s_prev + _dot(vw, k)
    return h_t.T, s_new, m_new


def _mlstm_kernel(*refs, nc, zero_init, emit_state):
    it = iter(refs)
    fwd_refs = tuple(next(it) for _ in range(5))
    bwd_refs = tuple(next(it) for _ in range(5)) if nc > 1 else fwd_refs
    if not zero_init:
        s0_ref, m0_ref = next(it), next(it)
    h_refs = (next(it), next(it))
    if emit_state:
        c_out, n_out, m_out = next(it), next(it), next(it)
    s_scr, m_scr = next(it), next(it)

    j = pl.program_id(1)
    n = CHUNK
    dh = HEAD_DIM

    @pl.when(j == 0)
    def _():
        if zero_init:
            s_scr[...] = jnp.zeros_like(s_scr)
            m_scr[...] = jnp.zeros_like(m_scr)
        else:
            s_scr[...] = s0_ref[0]
            m_scr[...] = m0_ref[0]

    rows = lax.broadcasted_iota(jnp.int32, (n, n), 0)
    cols = lax.broadcasted_iota(jnp.int32, (n, n), 1)
    le = rows <= cols
    ge = rows >= cols
    tri_le = le.astype(BF16)
    tri_ge = ge.astype(BF16)
    use_state = not (zero_init and nc == 1)

    def gate_terms(d):
        g_ref, gt_ref = (fwd_refs, bwd_refs)[d][3:5]
        gate = g_ref[0]
        gate_t = gt_ref[0]
        lf = _log_sigmoid(gate)
        lf_t = _log_sigmoid(gate_t)
        tri_c, tri_r = (tri_ge, tri_le) if d == 0 else (tri_le, tri_ge)
        bc = _dot(tri_c, jnp.concatenate(_split3(lf), axis=1))
        b_cols = bc[:, 0:128] + bc[:, 128:256] + bc[:, 256:384]
        br = _dot(jnp.concatenate(_split3(lf_t), axis=0), tri_r)
        b_rows = br[0:16] + br[16:32] + br[32:48]
        return gate, gate_t, b_cols, b_rows, jnp.sum(lf_t, axis=-1, keepdims=True)

    terms = [gate_terms(0), gate_terms(1)]
    hs = ([], [])
    for hd in range(HEADS):
        hsl = slice(hd * dh, (hd + 1) * dh)
        st = None
        for d in range(N_DIR):
            k_ref, qt_ref, vt_ref = (fwd_refs, bwd_refs)[d][0:3]
            gate, gate_t, b_cols, b_rows, tot_rows = terms[d]
            ci = d * 8 + hd
            cf = d * 8 + 4 + hd
            unit = d * HEADS + hd
            k = k_ref[0, :, hsl]
            qt = qt_ref[0, hsl, :]
            if st is None or nc > 1:
                st = _dot(k, qt)
            mask = le if d == 0 else ge
            h, s_new, m_new = _scan_unit(
                st, k, qt, vt_ref[0, hsl, :],
                gate[:, ci:ci + 1] - b_cols[:, cf:cf + 1],
                gate_t[ci:ci + 1, :] - b_rows[cf:cf + 1, :],
                b_rows[cf:cf + 1, :], tot_rows[cf:cf + 1, :],
                mask, s_scr[unit], m_scr[unit][:, 0:1], use_state)
            s_scr[unit] = s_new
            m_scr[unit] = jnp.broadcast_to(m_new, (1, LANES))
            hs[d].append(h)
    for d in range(N_DIR):
        h_refs[d][0] = jnp.concatenate(hs[d], axis=1).astype(BF16)

    if emit_state:
        @pl.when(j == nc - 1)
        def _():
            for unit in range(N_DIR * HEADS):
                s = s_scr[unit]
                c_out[0, unit] = s[:dh].T
                n_out[0, unit] = s[dh:dh + 1]
                m_out[0, unit] = m_scr[unit]


def _mlstm(k, qt, vt, gate, gate_t, s0, m0, emit_state):
    b, t, _ = k.shape
    nc = t // CHUNK
    zero_init = s0 is None
    nu = N_DIR * HEADS
    fwd = lambda w: pl.BlockSpec((1, CHUNK, w), lambda i, j: (i, j, 0))
    bwd = lambda w: pl.BlockSpec((1, CHUNK, w), lambda i, j: (i, nc - 1 - j, 0))
    fwd_t = lambda r: pl.BlockSpec((1, r, CHUNK), lambda i, j: (i, 0, j))
    bwd_t = lambda r: pl.BlockSpec((1, r, CHUNK), lambda i, j: (i, 0, nc - 1 - j))
    args = [k, qt, vt, gate, gate_t]
    in_specs = [fwd(512), fwd_t(512), fwd_t(512), fwd(LANES), fwd_t(16)]
    if nc > 1:
        args += [k, qt, vt, gate, gate_t]
        in_specs += [bwd(512), bwd_t(512), bwd_t(512), bwd(LANES), bwd_t(16)]
    if not zero_init:
        args += [s0, m0]
        in_specs += [pl.BlockSpec((1, nu, HEAD_DIM + N_ROWS, HEAD_DIM), lambda i, j: (i, 0, 0, 0)),
                     pl.BlockSpec((1, nu, 1, LANES), lambda i, j: (i, 0, 0, 0))]
    sd = jax.ShapeDtypeStruct
    out_shape = [sd((b, t, 512), BF16), sd((b, t, 512), BF16)]
    out_specs = [fwd(512), bwd(512)]
    if emit_state:
        out_shape += [sd((b, nu, HEAD_DIM, HEAD_DIM), F32), sd((b, nu, 1, HEAD_DIM), F32),
                      sd((b, nu, 1, LANES), F32)]
        out_specs += [pl.BlockSpec((1, nu, HEAD_DIM, HEAD_DIM), lambda i, j: (i, 0, 0, 0)),
                      pl.BlockSpec((1, nu, 1, HEAD_DIM), lambda i, j: (i, 0, 0, 0)),
                      pl.BlockSpec((1, nu, 1, LANES), lambda i, j: (i, 0, 0, 0))]
    return pl.pallas_call(
        functools.partial(_mlstm_kernel, nc=nc, zero_init=zero_init, emit_state=emit_state),
        grid=(b, nc),
        in_specs=in_specs,
        out_specs=out_specs,
        out_shape=out_shape,
        scratch_shapes=[pltpu.VMEM((nu, HEAD_DIM + N_ROWS, HEAD_DIM), F32),
                        pltpu.VMEM((nu, 1, LANES), F32)],
        compiler_params=_params("arbitrary", "arbitrary"),
        name="mlstm",
    )(*args)


def _outproj_kernel(x_ref, u_ref, hf_ref, hb_ref, o_ref, mod_ref, pm_ref, pinv_ref, wp_ref, ps_ref,
                    wo_ref, gpm_ref, gpf_ref, wr_ref, x1_ref, h2_ref, xp_ref, lg_ref):
    bs, tm, _ = x_ref.shape
    rows = bs * tm
    x = x_ref[...].reshape(rows, D_MODEL)
    mod = mod_ref[0]
    row0 = pl.multiple_of(pl.program_id(1) * tm, tm)
    diffs = []
    for g in range(POOL_GROUPS):
        sl = slice(g * POOL_GROUP_DIM, (g + 1) * POOL_GROUP_DIM)
        per_seq = []
        for bb in range(bs):
            box = _dot(pm_ref[g], u_ref[bb, :, sl])
            per_seq.append(box * pinv_ref[g] - u_ref[bb, pl.ds(row0, tm), sl].astype(F32))
        diffs.append(jnp.concatenate(per_seq, axis=0).astype(BF16))
    yps = [_dot(jnp.concatenate(diffs[2 * p:2 * p + 2], axis=1), wp_ref[p]) for p in range(POOL_GROUPS // 2)]
    y_pool = jnp.concatenate(yps, axis=1) * ps_ref[...]
    seq_rows = lambda ref: ref[...].reshape(rows, ref.shape[-1]).astype(F32)
    y_ml = jax.nn.sigmoid(seq_rows(o_ref)) * (seq_rows(hf_ref) + seq_rows(hb_ref))
    mix = _dot(jnp.concatenate([y_pool, y_ml], axis=1).astype(BF16), wo_ref[...])
    x1 = x + mod[2:3] * _rmsnorm(mix, gpm_ref[...])
    x1_ref[...] = x1.reshape(bs, tm, D_MODEL)
    h2 = _rmsnorm(x1, gpf_ref[...]) * (1.0 + mod[4:5]) + mod[3:4]
    p1, p2, p3 = _split3(h2)
    h2_ref[...] = p1.reshape(bs, tm, D_MODEL)
    nch = D_MODEL // LANES
    for g in range(rows // 8):
        cols = jnp.stack([h2[g * 8:(g + 1) * 8, cc * LANES:(cc + 1) * LANES] for cc in range(nch)], axis=0)
        bb, r0 = divmod(g * 8, tm)
        xp_ref[bb, r0:r0 + 8] = jnp.swapaxes(cols, 0, 1).astype(BF16)
    r = _dot(jnp.concatenate([p1, p2, p3], axis=0), wr_ref[...])
    r12 = r[0:rows] + r[rows:2 * rows]
    ne = N_EXPERTS
    lg = (r12 + r[2 * rows:])[:, 0:ne] + r12[:, ne:2 * ne] + r[0:rows, 2 * ne:3 * ne]
    lg_ref[...] = jnp.concatenate([lg, jnp.zeros_like(lg)], axis=1).T[0:ne]


def _outproj(x, u, hf, hb, o, mod, mod_row, pm, pinv, wp, ps, wo, gpm, gpf, wr):
    b, t, _ = x.shape
    tm = min(t, 2 * TOKEN_TILE)
    bs = 2 * TOKEN_TILE // tm if mod_row(1) == mod_row(0) else 1
    const = lambda *shape: pl.BlockSpec(shape, lambda i, j: (0,) * len(shape))
    tok = lambda w: pl.BlockSpec((bs, tm, w), lambda i, j: (i, j, 0))
    sd = jax.ShapeDtypeStruct
    return pl.pallas_call(
        _outproj_kernel,
        grid=(b // bs, t // tm),
        in_specs=[tok(D_MODEL),
                  pl.BlockSpec((bs, t, 512), lambda i, j: (i, 0, 0)),
                  tok(512), tok(512), tok(512),
                  pl.BlockSpec((1, N_MOD, D_MODEL), lambda i, j: (mod_row(i * bs), 0, 0)),
                  pl.BlockSpec((POOL_GROUPS, tm, t), lambda i, j: (0, j, 0)),
                  pl.BlockSpec((POOL_GROUPS, tm, 1), lambda i, j: (0, j, 0)),
                  const(POOL_GROUPS // 2, 2 * POOL_GROUP_DIM, 2 * POOL_GROUP_DIM), const(1, POOL_WIDTH),
                  const(D_MODEL, D_MODEL), const(1, D_MODEL), const(1, D_MODEL),
                  const(D_MODEL, 2 * LANES)],
        out_specs=[tok(D_MODEL), tok(D_MODEL),
                   pl.BlockSpec((bs, tm, D_MODEL // LANES, LANES), lambda i, j: (i, j, 0, 0)),
                   pl.BlockSpec((N_EXPERTS, bs * tm), lambda i, j: (0, i * (t // tm) + j))],
        out_shape=[sd((b, t, D_MODEL), F32), sd((b, t, D_MODEL), BF16),
                   sd((b, t, D_MODEL // LANES, LANES), BF16), sd((N_EXPERTS, b * t), F32)],
        compiler_params=_params("arbitrary", "arbitrary"),
        name="outproj",
    )(x, u, hf, hb, o, mod, pm, pinv, wp, ps, wo, gpm, gpf, wr)


def _router_kernel(lg_ref, br_ref, comb_ref, sel_ref, *, n_blocks):
    ng, gs = N_EXPERT_GROUPS, GROUP_SIZE
    neg = -jnp.inf
    lg = jnp.swapaxes(lg_ref[...], 0, 1)
    br = br_ref[...]
    s = [jax.nn.sigmoid(lg[j]) for j in range(gs)]
    biased = [s[j] + br[:, j, :] for j in range(gs)]
    fold = lambda op, xs: functools.reduce(op, xs)
    m1 = fold(jnp.maximum, biased)
    i1 = fold(jnp.minimum, [jnp.where(biased[j] == m1, j, gs) for j in range(gs)])
    m2 = fold(jnp.maximum, [jnp.where(i1 == j, neg, biased[j]) for j in range(gs)])
    cur = m1 + m2
    gi = lax.broadcasted_iota(jnp.int32, cur.shape, 0)
    gmask = jnp.zeros(cur.shape, F32)
    for _ in range(TOPK_GROUPS):
        mx = jnp.max(cur, axis=0, keepdims=True)
        ix = jnp.min(jnp.where(cur == mx, gi, ng), axis=0, keepdims=True)
        hit = gi == ix
        gmask = jnp.where(hit, 1.0, gmask)
        cur = jnp.where(hit, neg, cur)
    cand = [jnp.where(gmask > 0, biased[j], neg) for j in range(gs)]
    eidx = [gi * gs + j for j in range(gs)]
    selm = [jnp.zeros(cur.shape, F32) for _ in range(gs)]
    for _ in range(TOP_K):
        mx = jnp.max(fold(jnp.maximum, cand), axis=0, keepdims=True)
        ix = jnp.min(fold(jnp.minimum, [jnp.where(cand[j] == mx, eidx[j], N_EXPERTS) for j in range(gs)]),
                     axis=0, keepdims=True)
        for j in range(gs):
            hit = eidx[j] == ix
            selm[j] = jnp.where(hit, 1.0, selm[j])
            cand[j] = jnp.where(hit, neg, cand[j])
    sel = [selm[j] * s[j] for j in range(gs)]
    tot = jnp.sum(fold(jnp.add, sel), axis=0, keepdims=True)
    comb = [sel[j] / tot * ROUTED_SCALE for j in range(gs)]
    sel_ref[...] = jnp.swapaxes(jnp.stack(selm, axis=0), 0, 1)
    comb_e = jnp.swapaxes(jnp.stack(comb, axis=0), 0, 1).reshape(N_EXPERTS, -1)
    comb_t = jnp.concatenate([comb_e, jnp.zeros_like(comb_e)], axis=0).T
    comb_ref[...] = jnp.where(pl.program_id(0) < n_blocks, comb_t, 0.0)


def _router(logits_t, b_router):
    t = logits_t.shape[1]
    tl = 1024
    nb = t // tl
    shp = (N_EXPERT_GROUPS, GROUP_SIZE, t)
    blk = pl.BlockSpec((N_EXPERT_GROUPS, GROUP_SIZE, tl), lambda j: (0, 0, jnp.minimum(j, nb - 1)))
    comb, sel = pl.pallas_call(
        functools.partial(_router_kernel, n_blocks=nb),
        grid=(nb + 1,),
        in_specs=[blk, pl.BlockSpec((N_EXPERT_GROUPS, GROUP_SIZE, 1), lambda j: (0, 0, 0))],
        out_specs=[pl.BlockSpec((tl, LANES), lambda j: (j, 0)), blk],
        out_shape=[jax.ShapeDtypeStruct((t + tl, LANES), F32), jax.ShapeDtypeStruct(shp, F32)],
        compiler_params=_params("arbitrary"),
        name="router",
    )(logits_t.reshape(shp), b_router.reshape(N_EXPERT_GROUPS, GROUP_SIZE, 1))
    return comb, sel.reshape(N_EXPERTS, t)


def _plan_kernel(sel_ref, pos_ref, meta_ref, *, n_meta):
    t = sel_ref.shape[1]
    tm = float(MOE_TILE)
    sel = sel_ref[...]
    selb = sel.astype(BF16)
    blk = 256
    rr = lax.broadcasted_iota(jnp.int32, (blk, blk), 0)
    cc = lax.broadcasted_iota(jnp.int32, (blk, blk), 1)
    before = (rr < cc).astype(BF16)
    carry = jnp.zeros((N_EXPERTS, 1), F32)
    ranks = []
    for b in range(t // blk):
        sb = selb[:, b * blk:(b + 1) * blk]
        ranks.append(_dot(sb, before) + carry)
        carry = carry + jnp.sum(sel[:, b * blk:(b + 1) * blk], axis=1, keepdims=True)
    rank = jnp.concatenate(ranks, axis=1)
    cnt = carry
    ntile = jnp.floor((cnt + (tm - 0.5)) * (1.0 / tm))
    er = lax.broadcasted_iota(jnp.int32, (N_EXPERTS, N_EXPERTS), 0)
    ec = lax.broadcasted_iota(jnp.int32, (N_EXPERTS, N_EXPERTS), 1)
    below = (ec < er).astype(BF16)
    tstart = _dot(below, jnp.broadcast_to(ntile, (N_EXPERTS, LANES)).astype(BF16))[:, 0:1]
    pos = tstart * tm + rank
    erank = _dot(below, selb)
    rows = []
    for k in range(TOP_K):
        hit = (sel > 0.0) & (erank == float(k))
        rows.append(jnp.sum(jnp.where(hit, pos, 0.0), axis=0, keepdims=True))
    rows += [jnp.zeros((1, t), F32)] * (8 - TOP_K)
    pos_ref[...] = jnp.concatenate(rows, axis=0).astype(jnp.int32)

    tau = lax.broadcasted_iota(jnp.int32, (N_EXPERTS, n_meta), 1).astype(F32)
    eidx = lax.broadcasted_iota(jnp.int32, (N_EXPERTS, n_meta), 0).astype(F32)
    te = jnp.sum(((tstart + ntile) <= tau).astype(F32), axis=0, keepdims=True)
    te = jnp.minimum(te, float(N_EXPERTS - 1))
    onehot = eidx == te
    cnt_t = jnp.sum(jnp.where(onehot, cnt, 0.0), axis=0, keepdims=True)
    ts_t = jnp.sum(jnp.where(onehot, tstart, 0.0), axis=0, keepdims=True)
    tr = jnp.clip(cnt_t - (tau[0:1] - ts_t) * tm, 0.0, tm)
    tf = jnp.where((tau[0:1] == ts_t) & (tr > 0.0), 1.0, 0.0)
    meta_ref[...] = jnp.concatenate([te, tr, tf] + [jnp.zeros((1, n_meta), F32)] * 5, axis=0).astype(jnp.int32)


def _plan(sel, n_meta):
    t = sel.shape[1]
    sd = jax.ShapeDtypeStruct
    return pl.pallas_call(
        functools.partial(_plan_kernel, n_meta=n_meta),
        out_shape=[sd((8, t), jnp.int32), sd((8, n_meta), jnp.int32)],
        compiler_params=pltpu.CompilerParams(vmem_limit_bytes=VMEM_LIMIT),
        name="plan",
    )(sel)


def _slot_table(pos, n_slots, n_tokens):
    t = pos.shape[1]
    workers = SC_CORES * SC_SUBCORES
    per = -(-n_slots // (workers * SC_LANES)) * SC_LANES
    mesh = plsc.VectorSubcoreMesh(core_axis_name="core", subcore_axis_name="subcore")

    def body(pos_hbm, out_hbm, pos_v, tbl_v):
        lo = (lax.axis_index("core") * SC_SUBCORES + lax.axis_index("subcore")) * per
        pltpu.sync_copy(pos_hbm.at[pl.ds(0, TOP_K * t)], pos_v)
        dummy = jnp.full((SC_LANES,), n_tokens, jnp.int32)

        @plsc.parallel_loop(0, per, SC_LANES, unroll=8)
        def _(i):
            tbl_v[pl.ds(i, SC_LANES)] = dummy

        lane = lax.iota(jnp.int32, SC_LANES)
        for k in range(TOP_K):
            @plsc.parallel_loop(0, t, SC_LANES, unroll=8)
            def _(i):
                p = pos_v[pl.ds(k * t + i, SC_LANES)] - lo
                mine = (p >= 0) & (p < per)
                plsc.store_scatter(tbl_v, [jnp.where(mine, p, 0)], lane + i, mask=mine)
        pltpu.sync_copy(tbl_v, out_hbm.at[pl.ds(lo, per)])

    build = pl.kernel(body, out_type=jax.ShapeDtypeStruct((workers * per,), jnp.int32), mesh=mesh,
                      scratch_types=[pltpu.VMEM((TOP_K * t,), jnp.int32), pltpu.VMEM((per,), jnp.int32)],
                      compiler_params=pltpu.CompilerParams(needs_layout_passes=False), name="slot_table")
    return build(pos.reshape(-1))


def _experts_kernel(te_ref, tr_ref, xc_ref, xs_ref, slots_ref, comb_ref, wgu_ref, wd_ref, acc_out,
                    tbl_ref, xbuf, acc, stage0, stage1, cst0, cst1, act0, act1, ybuf0, ybuf1,
                    wgu_b0, wgu_b1, wd_b0, wd_b1, sems):
    s = pl.program_id(0)
    n_tiles = te_ref.shape[0]
    tm = MOE_TILE
    nch = D_MODEL // LANES
    tile_at = lambda lag: jnp.clip(s - lag, 0, n_tiles - 1)
    t_g, t_1, t_2, t_3 = tile_at(0), tile_at(1), tile_at(2), tile_at(3)

    @pl.when(s == 0)
    def _():
        tc, ts = xc_ref.shape[0], xs_ref.shape[0]
        copies = (pltpu.make_async_copy(xc_ref, xbuf.at[pl.ds(0, tc)], sems.at[0]),
                  pltpu.make_async_copy(xs_ref, xbuf.at[pl.ds(tc, ts)], sems.at[1]))
        tbl_init = pltpu.make_async_copy(slots_ref, tbl_ref, sems.at[3])
        tbl_init.start()
        for cp in copies:
            cp.start()
        n_pad = xbuf.shape[0] - tc - ts
        xbuf[pl.ds(tc + ts, n_pad)] = jnp.zeros((n_pad,) + xbuf.shape[1:], BF16)
        for ref in (acc, stage0, stage1, cst0, cst1, act0, act1, ybuf0, ybuf1, wgu_b0, wgu_b1, wd_b0, wd_b1):
            ref[...] = jnp.zeros_like(ref)
        tbl_init.wait()
        for cp in copies:
            cp.wait()

    def gather(tile, stage, cst):
        base = tile * tm
        for j in range(tm):
            tok = tbl_ref[base + j]
            stage[pl.ds(j * nch, nch), :] = xbuf[tok].astype(F32)
            cst[pl.ds(j, 1), :] = comb_ref[pl.ds(tok, 1), :]

    def gate_up(tile, stage, cst, wgu_b, act):
        xb = _token_rows_from_slabs(
            lambda g: stage[g * 8 * nch:(g + 1) * 8 * nch, :].reshape(8, nch, LANES), tm).astype(BF16)
        gu = _dot(xb, wgu_b[...])
        lane = lax.broadcasted_iota(jnp.int32, (1, LANES), 1)
        w_col = jnp.sum(jnp.where(lane == te_ref[tile], cst[...], 0.0), axis=1, keepdims=True)
        act[...] = (_silu(gu[:, :EXPERT_DIM]) * gu[:, EXPERT_DIM:] * w_col).astype(BF16)

    def down(act, wd_b, ybuf):
        y = _dot(act[...], wd_b[...])
        for cc in range(nch):
            ybuf[cc * Y_PITCH:cc * Y_PITCH + tm, :] = y[:, cc * LANES:(cc + 1) * LANES]

    def scatter(tile, ybuf):
        base = tile * tm
        sc_n = 16
        for i in range(tm // sc_n):
            toks = [tbl_ref[base + i * sc_n + u] for u in range(sc_n)]
            olds = [acc[toks[u]] for u in range(sc_n)]
            news = [olds[u] + ybuf[pl.ds(i * sc_n + u, nch, stride=Y_PITCH), :] for u in range(sc_n)]
            for u in range(sc_n):
                acc[toks[u]] = news[u]

    busy = (tr_ref[t_g] + tr_ref[t_1] + tr_ref[t_2] + tr_ref[t_3]) > 0
    bufs = ((stage0, cst0, act0, ybuf0, wgu_b0, wd_b0), (stage1, cst1, act1, ybuf1, wgu_b1, wd_b1))
    for par in range(2):
        stage_p, cst_p, act_p, ybuf_p, wgu_p, wd_p = bufs[par]
        stage_q, cst_q, act_q, ybuf_q, wgu_q, wd_q = bufs[1 - par]

        @pl.when(busy & (s % 2 == par))
        def _():
            gather(t_g, stage_p, cst_p)
            wgu_p[...] = wgu_ref[0].astype(BF16)
            gate_up(t_1, stage_q, cst_q, wgu_q, act_q)
            wd_q[...] = wd_ref[0].astype(BF16)
            down(act_p, wd_p, ybuf_p)
            scatter(t_3, ybuf_q)

    @pl.when(s == pl.num_programs(0) - 1)
    def _():
        cp = pltpu.make_async_copy(acc, acc_out, sems.at[2])
        cp.start()
        cp.wait()


def _experts(te, tr, xc, xs, slots, comb, wgu, wd):
    n_tok = xc.shape[0] + xs.shape[0] + 8
    n_tiles = te.shape[0]
    tm = MOE_TILE
    nch = D_MODEL // LANES
    vm = pltpu.VMEM
    grid_spec = pltpu.PrefetchScalarGridSpec(
        num_scalar_prefetch=2,
        grid=(n_tiles + 3,),
        in_specs=[pl.BlockSpec(memory_space=pl.ANY), pl.BlockSpec(memory_space=pl.ANY),
                  pl.BlockSpec(memory_space=pl.ANY),
                  pl.BlockSpec((n_tok, LANES), lambda s, *_: (0, 0), pipeline_mode=pl.Buffered(1)),
                  pl.BlockSpec((1, D_MODEL, 2 * EXPERT_DIM), lambda s, te, *_: (te[jnp.minimum(s, n_tiles - 1)], 0, 0)),
                  pl.BlockSpec((1, EXPERT_DIM, D_MODEL), lambda s, te, *_: (te[jnp.clip(s - 1, 0, n_tiles - 1)], 0, 0))],
        out_specs=pl.BlockSpec(memory_space=pl.ANY),
        scratch_shapes=[pltpu.SMEM(slots.shape, jnp.int32),
                        vm((n_tok, nch, LANES), BF16), vm((n_tok, nch, LANES), F32),
                        vm((tm * nch, LANES), F32), vm((tm * nch, LANES), F32),
                        vm((tm, LANES), F32), vm((tm, LANES), F32),
                        vm((tm, EXPERT_DIM), BF16), vm((tm, EXPERT_DIM), BF16),
                        vm((nch * Y_PITCH, LANES), F32), vm((nch * Y_PITCH, LANES), F32),
                        vm((D_MODEL, 2 * EXPERT_DIM), BF16), vm((D_MODEL, 2 * EXPERT_DIM), BF16),
                        vm((EXPERT_DIM, D_MODEL), BF16), vm((EXPERT_DIM, D_MODEL), BF16),
                        pltpu.SemaphoreType.DMA((4,))],
    )
    return pl.pallas_call(
        _experts_kernel,
        grid_spec=grid_spec,
        out_shape=jax.ShapeDtypeStruct((n_tok, nch, LANES), F32),
        compiler_params=pltpu.CompilerParams(dimension_semantics=("arbitrary",),
                                             vmem_limit_bytes=EXPERTS_VMEM_LIMIT),
        name="experts",
    )(te, tr, xc, xs, slots, comb, wgu, wd)


def _final_kernel(acc_ref, h_ref, x1_ref, mod_ref, wsg_ref, wsd_ref, gpo_ref, out_ref):
    routed = _token_rows_from_slabs(lambda g: acc_ref[g * 8:(g + 1) * 8], acc_ref.shape[0])
    gs = _dot(h_ref[...], wsg_ref[...])
    act = _silu(gs[:, :SHARED_DIM]) * gs[:, SHARED_DIM:]
    f = routed + _dot(act.astype(BF16), wsd_ref[...])
    out_ref[...] = x1_ref[...] + mod_ref[0][5:6] * _rmsnorm(f, gpo_ref[...])


def _final(acc, tile0, h2, x1, mod, mod_row, wsg, wsd, gpo):
    n = h2.shape[0]
    tm = FINAL_TILE
    const = lambda *shape: pl.BlockSpec(shape, lambda i: (0,) * len(shape))
    tok = lambda w: pl.BlockSpec((tm, w), lambda i: (i, 0))
    return pl.pallas_call(
        _final_kernel,
        grid=(n // tm,),
        in_specs=[pl.BlockSpec((tm, D_MODEL // LANES, LANES), lambda i: (i + tile0, 0, 0)),
                  tok(D_MODEL), tok(D_MODEL),
                  pl.BlockSpec((1, N_MOD, D_MODEL), lambda i: (mod_row(i), 0, 0)),
                  const(D_MODEL, 2 * SHARED_DIM), const(SHARED_DIM, D_MODEL), const(1, D_MODEL)],
        out_specs=tok(D_MODEL),
        out_shape=jax.ShapeDtypeStruct((n, D_MODEL), F32),
        compiler_params=_params("arbitrary"),
        name="final",
    )(acc, h2, x1, mod, wsg, wsd, gpo)


def _window_bounds(n, w):
    idx = np.arange(n)
    return np.clip(idx - w // 2, 0, n), np.clip(idx + w - w // 2, 0, n)


def _pool_operators(t, grid):
    mats, invs = [], []
    for w in POOL_WINDOWS:
        if grid:
            rlo, rhi = _window_bounds(t // GRID_W, w)
            clo, chi = _window_bounds(GRID_W, w)
            r = np.arange(t) // GRID_W
            c = np.arange(t) % GRID_W
            m = ((r[None, :] >= rlo[r][:, None]) & (r[None, :] < rhi[r][:, None])
                 & (c[None, :] >= clo[c][:, None]) & (c[None, :] < chi[c][:, None]))
            cnt = (rhi - rlo)[r] * (chi - clo)[c]
        else:
            lo, hi = _window_bounds(t, w)
            sidx = np.arange(t)
            m = (sidx[None, :] >= lo[:, None]) & (sidx[None, :] < hi[:, None])
            cnt = hi - lo
        mats.append(m.astype(np.float32))
        invs.append((1.0 / cnt.astype(np.float64)).astype(np.float32)[:, None])
    return jnp.asarray(np.stack(mats), BF16), jnp.asarray(np.stack(invs), F32)


def kernel(x_prompt, x_sample, state_C, state_n, state_m, c, c_ctx, w_ada, b_ada, g_pre_mix, w_in, b_gate,
           w_pool, pool_scale, w_out, g_post_mix, g_pre_ffn, w_router, b_router, w_expert_gu, w_expert_down,
           w_shared_gu, w_shared_down, g_post_ffn):
    b_ctx = x_prompt.shape[0]
    b_lat = x_sample.shape[0]
    nu = N_DIR * HEADS
    l = 0
    row = lambda a: a[l].reshape(1, -1).astype(F32)

    cvec = jnp.zeros((16, D_MODEL), F32).at[0].set(c_ctx.astype(F32)).at[1:1 + b_lat].set(c.astype(F32))
    mod = _mod_rows(cvec, w_ada[l], b_ada[l]).reshape(16, N_MOD, D_MODEL)

    w_in_l = w_in[l]
    p0 = POOL_WIDTH
    mw = MLSTM_WIDTH
    w_u, w_q, w_k, w_v, w_o = (w_in_l[:, lo:lo + 512] for lo in (0, p0, p0 + mw, p0 + 2 * mw, p0 + 3 * mw))
    wm = jnp.concatenate([w_u, w_k, w_o], axis=1).astype(BF16)
    wt = jnp.concatenate([w_q.T, w_v.T], axis=0).astype(BF16)
    wg_cols = w_in_l[:, p0 + 4 * mw:]
    wg = jnp.pad(jnp.concatenate(_split2(wg_cols), axis=1), ((0, 0), (0, LANES - 2 * GATE_COLS)))
    bg = b_gate[l].reshape(GATE_COLS).astype(F32)
    bgr = jnp.pad(bg, (0, LANES - GATE_COLS)).reshape(1, LANES)
    wpl = w_pool[l].astype(BF16)
    zg = jnp.zeros((POOL_GROUP_DIM, POOL_GROUP_DIM), BF16)
    wp = jnp.stack([jnp.block([[wpl[2 * p], zg], [zg, wpl[2 * p + 1]]]) for p in range(POOL_GROUPS // 2)])
    wo = w_out[l].astype(BF16)
    wr = jnp.pad(jnp.concatenate(_split3(w_router[l].astype(F32)), axis=1), ((0, 0), (0, 2 * LANES - 3 * N_EXPERTS)))
    wsg = w_shared_gu[l].astype(BF16)
    wsd = w_shared_down[l].astype(BF16)

    def mixer(x, mod_row, grid, s0, m0, emit_state):
        t = x.shape[1]
        u, k, o, qt, vt, gate, gate_t = _inproj(x.astype(F32), mod, mod_row, row(g_pre_mix), wm, wt, wg, bgr)
        outs = _mlstm(k, qt, vt, gate, gate_t, s0, m0, emit_state)
        hf, hb = outs[0], outs[1]
        pm, pinv = _pool_operators(t, grid)
        x1, h2, xp, lg = _outproj(x.astype(F32), u, hf, hb, o, mod, mod_row, pm, pinv, wp, row(pool_scale), wo,
                                  row(g_post_mix), row(g_pre_ffn), wr)
        return x1, h2, xp, lg, outs[2:]

    ctx_row = lambda i: 0
    lat_row = lambda i: i + 1
    x1c, h2c, xpc, lgc, (c_new, n_new, m_new) = mixer(x_prompt, ctx_row, False, None, None, True)
    s0 = jnp.concatenate(
        [jnp.swapaxes(state_C[:, l].reshape(b_lat, nu, HEAD_DIM, HEAD_DIM).astype(F32), -1, -2),
         jnp.broadcast_to(state_n[:, l].reshape(b_lat, nu, 1, HEAD_DIM).astype(F32),
                          (b_lat, nu, N_ROWS, HEAD_DIM))], axis=-2)
    m0 = jnp.broadcast_to(state_m[:, l].reshape(b_lat, nu, 1, 1).astype(F32), (b_lat, nu, 1, LANES))
    x1s, h2s, xps, lgs, _ = mixer(x_sample, lat_row, True, s0, m0, False)

    tc = b_ctx * x_prompt.shape[1]
    ts = b_lat * x_sample.shape[1]
    n_tok = tc + ts
    lg_all = jnp.concatenate([lgc, lgs], axis=1)
    comb_tok, sel = _router(lg_all, b_router[l].astype(F32))
    n_tiles = n_tok * TOP_K // MOE_TILE + N_EXPERTS
    n_meta = -(-n_tiles // LANES) * LANES
    pos, meta = _plan(sel, n_meta)

    slab = (D_MODEL // LANES, LANES)
    acc = _experts(meta[0, :n_tiles], meta[1, :n_tiles],
                   xpc.reshape((tc,) + slab), xps.reshape((ts,) + slab),
                   _slot_table(pos, n_tiles * MOE_TILE, n_tok), comb_tok,
                   w_expert_gu[l], w_expert_down[l])

    fin = functools.partial(_final, wsg=wsg, wsd=wsd, gpo=row(g_post_ffn))
    tiles_per_lat = x_sample.shape[1] // FINAL_TILE
    yc = fin(acc, 0, h2c.reshape(tc, D_MODEL), x1c.reshape(tc, D_MODEL), mod, ctx_row)
    ys = fin(acc, tc // FINAL_TILE, h2s.reshape(ts, D_MODEL), x1s.reshape(ts, D_MODEL), mod,
             lambda i: i // tiles_per_lat + 1)

    new_c = c_new.reshape(b_ctx, 1, N_DIR, HEADS, HEAD_DIM, HEAD_DIM)
    new_n = n_new.reshape(b_ctx, 1, N_DIR, HEADS, HEAD_DIM)
    new_m = m_new[..., 0].reshape(b_ctx, 1, N_DIR, HEADS)
    return (yc.reshape(x_prompt.shape), ys.reshape(x_sample.shape), new_c, new_n, new_m)
```

```python
import functools

import jax
import jax.numpy as jnp
import numpy as np
from jax import lax
from jax.experimental import pallas as pl
from jax.experimental.pallas import tpu as pltpu
from jax.experimental.pallas import tpu_sc as plsc

F32 = jnp.float32
BF16 = jnp.bfloat16

D_MODEL = 1024
GRID_W = 64
POOL_WIDTH = 512
POOL_GROUPS = 4
POOL_GROUP_DIM = 128
POOL_WINDOWS = (2, 4, 8, 16)
HEADS = 4
HEAD_DIM = 128
MLSTM_WIDTH = HEADS * HEAD_DIM
N_DIR = 2
GATE_COLS = N_DIR * 2 * HEADS
N_EXPERTS = 64
TOP_K = 6
N_EXPERT_GROUPS = 8
GROUP_SIZE = N_EXPERTS // N_EXPERT_GROUPS
TOPK_GROUPS = 4
EXPERT_DIM = 256
SHARED_DIM = 256
ROUTED_SCALE = 2.5
N_MOD = 6
EPS = 1e-6
K_SCALE = HEAD_DIM ** -0.5

LANES = 128
CHUNK = 256
N_ROWS = 16
TOKEN_TILE = 256
FINAL_TILE = 512
MOE_TILE = 320
Y_PITCH = MOE_TILE + 8
VMEM_LIMIT = 56 * 1024 * 1024
SC_CORES = 2
SC_SUBCORES = 16
SC_LANES = 16
EXPERTS_VMEM_LIMIT = 58 * 1024 * 1024


def _split3(x):
    p1 = x.astype(BF16)
    r1 = x - p1.astype(F32)
    p2 = r1.astype(BF16)
    p3 = (r1 - p2.astype(F32)).astype(BF16)
    return p1, p2, p3


def _split2(x):
    p1 = x.astype(BF16)
    p2 = (x - p1.astype(F32)).astype(BF16)
    return p1, p2


def _dot(a, b):
    return jnp.dot(a, b, preferred_element_type=F32)


def _dot_nt(a, b):
    return lax.dot_general(a, b, (((1,), (1,)), ((), ())), preferred_element_type=F32)


def _rmsnorm(x, g):
    return x * lax.rsqrt(jnp.mean(x * x, axis=-1, keepdims=True) + EPS) * g


def _silu(x):
    return x * jax.nn.sigmoid(x)


def _token_rows_from_slabs(read_block, n_tok):
    nch = D_MODEL // LANES
    cols = [[] for _ in range(nch)]
    for g in range(n_tok // 8):
        blk = jnp.swapaxes(read_block(g), 0, 1)
        for cc in range(nch):
            cols[cc].append(blk[cc])
    return jnp.concatenate([jnp.concatenate(c, axis=0) for c in cols], axis=1)


def _params(*sem):
    return pltpu.CompilerParams(dimension_semantics=sem, vmem_limit_bytes=VMEM_LIMIT)


def _mod_kernel(c_ref, w_ref, b_ref, o_ref):
    a = _silu(c_ref[...])
    a_stack = jnp.concatenate(_split3(a), axis=0)
    w1, w2 = _split2(w_ref[...])
    r1 = _dot(a_stack, w1)
    r2 = _dot(a_stack[:32], w2)
    o_ref[...] = (r1[0:16] + r1[16:32] + r1[32:48] + r2[0:16] + r2[16:32]) + b_ref[...]


def _mod_rows(cvec, w_ada, b_ada):
    n = N_MOD * D_MODEL
    tn = 1536
    return pl.pallas_call(
        _mod_kernel,
        grid=(n // tn,),
        in_specs=[pl.BlockSpec((16, D_MODEL), lambda j: (0, 0)),
                  pl.BlockSpec((D_MODEL, tn), lambda j: (0, j)),
                  pl.BlockSpec((1, tn), lambda j: (0, j))],
        out_specs=pl.BlockSpec((16, tn), lambda j: (0, j)),
        out_shape=jax.ShapeDtypeStruct((16, n), F32),
        compiler_params=_params("arbitrary"),
        name="mod",
    )(cvec, w_ada, b_ada.reshape(1, n))


def _inproj_kernel(x_ref, mod_ref, g_ref, wm_ref, wt_ref, wg_ref, bgr_ref,
                   u_ref, k_ref, o_ref, qt_ref, vt_ref, gate_ref, gatet_ref):
    bs, tm, _ = x_ref.shape
    rows = bs * tm
    x = x_ref[...].reshape(rows, D_MODEL)
    mod = mod_ref[0]
    h = _rmsnorm(x, g_ref[...]) * (1.0 + mod[1:2]) + mod[0:1]
    h1, h2, h3 = _split3(h)
    z = _dot(h1, wm_ref[...])
    u_ref[...] = z[:, 0:512].astype(BF16).reshape(bs, tm, 512)
    k_ref[...] = (z[:, 512:1024] * K_SCALE).astype(BF16).reshape(bs, tm, 512)
    o_ref[...] = z[:, 1024:1536].astype(BF16).reshape(bs, tm, 512)
    zt = _dot_nt(wt_ref[...], h1).astype(BF16)
    r = _dot(jnp.concatenate([h1, h2, h3], axis=0), wg_ref[...])
    r12 = r[0:rows] + r[rows:2 * rows]
    gate = (r12 + r[2 * rows:]) + pltpu.roll(r12, LANES - GATE_COLS, axis=1) + bgr_ref[...]
    gate_ref[...] = gate.reshape(bs, tm, LANES)
    gate_t = gate.T
    for bb in range(bs):
        cols = slice(bb * tm, (bb + 1) * tm)
        qt_ref[bb] = zt[0:512, cols]
        vt_ref[bb] = zt[512:1024, cols]
        gatet_ref[bb] = gate_t[0:16, cols]


def _inproj(x, mod, mod_row, g, wm, wt, wg, bgr):
    b, t, _ = x.shape
    tm = min(t, 2 * TOKEN_TILE)
    bs = 2 * TOKEN_TILE // tm if mod_row(1) == mod_row(0) else 1
    const = lambda *shape: pl.BlockSpec(shape, lambda i, j: (0,) * len(shape))
    tok = lambda w: pl.BlockSpec((bs, tm, w), lambda i, j: (i, j, 0))
    tok_t = lambda r: pl.BlockSpec((bs, r, tm), lambda i, j: (i, 0, j))
    sd = jax.ShapeDtypeStruct
    return pl.pallas_call(
        _inproj_kernel,
        grid=(b // bs, t // tm),
        in_specs=[tok(D_MODEL),
                  pl.BlockSpec((1, N_MOD, D_MODEL), lambda i, j: (mod_row(i * bs), 0, 0)),
                  const(1, D_MODEL), const(D_MODEL, 1536), const(1024, D_MODEL),
                  const(D_MODEL, LANES), const(1, LANES)],
        out_specs=[tok(512), tok(512), tok(512), tok_t(512), tok_t(512), tok(LANES), tok_t(16)],
        out_shape=[sd((b, t, 512), BF16), sd((b, t, 512), BF16), sd((b, t, 512), BF16),
                   sd((b, 512, t), BF16), sd((b, 512, t), BF16), sd((b, t, LANES), F32),
                   sd((b, 16, t), F32)],
        compiler_params=_params("arbitrary", "arbitrary"),
        name="inproj",
    )(x, mod, g, wm, wt, wg, bgr)


def _log_sigmoid(x):
    return jnp.minimum(x, 0.0) - jnp.log1p(jnp.exp(-jnp.abs(x)))


def _scan_unit(st, k, qt, vt, u_col, u_row, b_row, btot, mask, s_prev, m_prev, use_state):
    dh = HEAD_DIM
    n = st.shape[0]
    ub = jnp.where(mask, jnp.broadcast_to(u_col, (n, n)), -jnp.inf)
    z = jnp.maximum(m_prev, jnp.max(ub, axis=0, keepdims=True))
    p = (jnp.exp(ub - z) * st).astype(BF16)
    ones = jnp.ones((N_ROWS, n), BF16)
    tot = _dot(jnp.concatenate([vt, ones], axis=0), p)
    if use_state:
        tot = tot + jnp.exp(m_prev - z) * _dot(s_prev.astype(BF16), qt)
    floor = jnp.exp(-(b_row + z))
    h_t = tot[:dh] / jnp.maximum(jnp.abs(tot[dh:dh + 1]), floor)
    g_row = btot + u_row
    m_new = jnp.maximum(btot + m_prev, jnp.max(g_row, axis=-1, keepdims=True))
    w_row = jnp.exp(g_row - m_new)
    vw = jnp.concatenate([(vt.astype(F32) * w_row).astype(BF16),
                          jnp.broadcast_to(w_row, (N_ROWS, n)).astype(BF16)], axis=0)
    s_new = jnp.exp(btot + m_prev - m_new) * s_prev + _dot(vw, k)
    return h_t.T, s_new, m_new


def _mlstm_kernel(*refs, nc, zero_init, emit_state):
    it = iter(refs)
    fwd_refs = tuple(next(it) for _ in range(5))
    bwd_refs = tuple(next(it) for _ in range(5)) if nc > 1 else fwd_refs
    if not zero_init:
        s0_ref, m0_ref = next(it), next(it)
    h_refs = (next(it), next(it))
    if emit_state:
        c_out, n_out, m_out = next(it), next(it), next(it)
    s_scr, m_scr = next(it), next(it)

    j = pl.program_id(1)
    n = CHUNK
    dh = HEAD_DIM

    @pl.when(j == 0)
    def _():
        if zero_init:
            s_scr[...] = jnp.zeros_like(s_scr)
            m_scr[...] = jnp.zeros_like(m_scr)
        else:
            s_scr[...] = s0_ref[0]
            m_scr[...] = m0_ref[0]

    rows = lax.broadcasted_iota(jnp.int32, (n, n), 0)
    cols = lax.broadcasted_iota(jnp.int32, (n, n), 1)
    le = rows <= cols
    ge = rows >= cols
    tri_le = le.astype(BF16)
    tri_ge = ge.astype(BF16)
    use_state = not (zero_init and nc == 1)

    def gate_terms(d):
        g_ref, gt_ref = (fwd_refs, bwd_refs)[d][3:5]
        gate = g_ref[0]
        gate_t = gt_ref[0]
        lf = _log_sigmoid(gate)
        lf_t = _log_sigmoid(gate_t)
        tri_c, tri_r = (tri_ge, tri_le) if d == 0 else (tri_le, tri_ge)
        bc = _dot(tri_c, jnp.concatenate(_split3(lf), axis=1))
        b_cols = bc[:, 0:128] + bc[:, 128:256] + bc[:, 256:384]
        br = _dot(jnp.concatenate(_split3(lf_t), axis=0), tri_r)
        b_rows = br[0:16] + br[16:32] + br[32:48]
        return gate, gate_t, b_cols, b_rows, jnp.sum(lf_t, axis=-1, keepdims=True)

    terms = [gate_terms(0), gate_terms(1)]
    hs = ([], [])
    for hd in range(HEADS):
        hsl = slice(hd * dh, (hd + 1) * dh)
        st = None
        for d in range(N_DIR):
            k_ref, qt_ref, vt_ref = (fwd_refs, bwd_refs)[d][0:3]
            gate, gate_t, b_cols, b_rows, tot_rows = terms[d]
            ci = d * 8 + hd
            cf = d * 8 + 4 + hd
            unit = d * HEADS + hd
            k = k_ref[0, :, hsl]
            qt = qt_ref[0, hsl, :]
            if st is None or nc > 1:
                st = _dot(k, qt)
            mask = le if d == 0 else ge
            h, s_new, m_new = _scan_unit(
                st, k, qt, vt_ref[0, hsl, :],
                gate[:, ci:ci + 1] - b_cols[:, cf:cf + 1],
                gate_t[ci:ci + 1, :] - b_rows[cf:cf + 1, :],
                b_rows[cf:cf + 1, :], tot_rows[cf:cf + 1, :],
                mask, s_scr[unit], m_scr[unit][:, 0:1], use_state)
            s_scr[unit] = s_new
            m_scr[unit] = jnp.broadcast_to(m_new, (1, LANES))
            hs[d].append(h)
    for d in range(N_DIR):
        h_refs[d][0] = jnp.concatenate(hs[d], axis=1).astype(BF16)

    if emit_state:
        @pl.when(j == nc - 1)
        def _():
            for unit in range(N_DIR * HEADS):
                s = s_scr[unit]
                c_out[0, unit] = s[:dh].T
                n_out[0, unit] = s[dh:dh + 1]
                m_out[0, unit] = m_scr[unit]


def _mlstm(k, qt, vt, gate, gate_t, s0, m0, emit_state):
    b, t, _ = k.shape
    nc = t // CHUNK
    zero_init = s0 is None
    nu = N_DIR * HEADS
    fwd = lambda w: pl.BlockSpec((1, CHUNK, w), lambda i, j: (i, j, 0))
    bwd = lambda w: pl.BlockSpec((1, CHUNK, w), lambda i, j: (i, nc - 1 - j, 0))
    fwd_t = lambda r: pl.BlockSpec((1, r, CHUNK), lambda i, j: (i, 0, j))
    bwd_t = lambda r: pl.BlockSpec((1, r, CHUNK), lambda i, j: (i, 0, nc - 1 - j))
    args = [k, qt, vt, gate, gate_t]
    in_specs = [fwd(512), fwd_t(512), fwd_t(512), fwd(LANES), fwd_t(16)]
    if nc > 1:
        args += [k, qt, vt, gate, gate_t]
        in_specs += [bwd(512), bwd_t(512), bwd_t(512), bwd(LANES), bwd_t(16)]
    if not zero_init:
        args += [s0, m0]
        in_specs += [pl.BlockSpec((1, nu, HEAD_DIM + N_ROWS, HEAD_DIM), lambda i, j: (i, 0, 0, 0)),
                     pl.BlockSpec((1, nu, 1, LANES), lambda i, j: (i, 0, 0, 0))]
    sd = jax.ShapeDtypeStruct
    out_shape = [sd((b, t, 512), BF16), sd((b, t, 512), BF16)]
    out_specs = [fwd(512), bwd(512)]
    if emit_state:
        out_shape += [sd((b, nu, HEAD_DIM, HEAD_DIM), F32), sd((b, nu, 1, HEAD_DIM), F32),
                      sd((b, nu, 1, LANES), F32)]
        out_specs += [pl.BlockSpec((1, nu, HEAD_DIM, HEAD_DIM), lambda i, j: (i, 0, 0, 0)),
                      pl.BlockSpec((1, nu, 1, HEAD_DIM), lambda i, j: (i, 0, 0, 0)),
                      pl.BlockSpec((1, nu, 1, LANES), lambda i, j: (i, 0, 0, 0))]
    return pl.pallas_call(
        functools.partial(_mlstm_kernel, nc=nc, zero_init=zero_init, emit_state=emit_state),
        grid=(b, nc),
        in_specs=in_specs,
        out_specs=out_specs,
        out_shape=out_shape,
        scratch_shapes=[pltpu.VMEM((nu, HEAD_DIM + N_ROWS, HEAD_DIM), F32),
                        pltpu.VMEM((nu, 1, LANES), F32)],
        compiler_params=_params("arbitrary", "arbitrary"),
        name="mlstm",
    )(*args)


def _outproj_kernel(x_ref, u_ref, hf_ref, hb_ref, o_ref, mod_ref, pm_ref, pinv_ref, wp_ref, ps_ref,
                    wo_ref, gpm_ref, gpf_ref, wr_ref, x1_ref, h2_ref, xp_ref, lg_ref):
    bs, tm, _ = x_ref.shape
    rows = bs * tm
    x = x_ref[...].reshape(rows, D_MODEL)
    mod = mod_ref[0]
    row0 = pl.multiple_of(pl.program_id(1) * tm, tm)
    diffs = []
    for g in range(POOL_GROUPS):
        sl = slice(g * POOL_GROUP_DIM, (g + 1) * POOL_GROUP_DIM)
        per_seq = []
        for bb in range(bs):
            box = _dot(pm_ref[g], u_ref[bb, :, sl])
            per_seq.append(box * pinv_ref[g] - u_ref[bb, pl.ds(row0, tm), sl].astype(F32))
        diffs.append(jnp.concatenate(per_seq, axis=0).astype(BF16))
    yps = [_dot(jnp.concatenate(diffs[2 * p:2 * p + 2], axis=1), wp_ref[p]) for p in range(POOL_GROUPS // 2)]
    y_pool = jnp.concatenate(yps, axis=1) * ps_ref[...]
    seq_rows = lambda ref: ref[...].reshape(rows, ref.shape[-1]).astype(F32)
    y_ml = jax.nn.sigmoid(seq_rows(o_ref)) * (seq_rows(hf_ref) + seq_rows(hb_ref))
    mix = _dot(jnp.concatenate([y_pool, y_ml], axis=1).astype(BF16), wo_ref[...])
    x1 = x + mod[2:3] * _rmsnorm(mix, gpm_ref[...])
    x1_ref[...] = x1.reshape(bs, tm, D_MODEL)
    h2 = _rmsnorm(x1, gpf_ref[...]) * (1.0 + mod[4:5]) + mod[3:4]
    p1, p2, p3 = _split3(h2)
    h2_ref[...] = p1.reshape(bs, tm, D_MODEL)
    nch = D_MODEL // LANES
    for g in range(rows // 8):
        cols = jnp.stack([h2[g * 8:(g + 1) * 8, cc * LANES:(cc + 1) * LANES] for cc in range(nch)], axis=0)
        bb, r0 = divmod(g * 8, tm)
        xp_ref[bb, r0:r0 + 8] = jnp.swapaxes(cols, 0, 1).astype(BF16)
    r = _dot(jnp.concatenate([p1, p2, p3], axis=0), wr_ref[...])
    r12 = r[0:rows] + r[rows:2 * rows]
    ne = N_EXPERTS
    lg = (r12 + r[2 * rows:])[:, 0:ne] + r12[:, ne:2 * ne] + r[0:rows, 2 * ne:3 * ne]
    lg_ref[...] = jnp.concatenate([lg, jnp.zeros_like(lg)], axis=1).T[0:ne]


def _outproj(x, u, hf, hb, o, mod, mod_row, pm, pinv, wp, ps, wo, gpm, gpf, wr):
    b, t, _ = x.shape
    tm = min(t, 2 * TOKEN_TILE)
    bs = 2 * TOKEN_TILE // tm if mod_row(1) == mod_row(0) else 1
    const = lambda *shape: pl.BlockSpec(shape, lambda i, j: (0,) * len(shape))
    tok = lambda w: pl.BlockSpec((bs, tm, w), lambda i, j: (i, j, 0))
    sd = jax.ShapeDtypeStruct
    return pl.pallas_call(
        _outproj_kernel,
        grid=(b // bs, t // tm),
        in_specs=[tok(D_MODEL),
                  pl.BlockSpec((bs, t, 512), lambda i, j: (i, 0, 0)),
                  tok(512), tok(512), tok(512),
                  pl.BlockSpec((1, N_MOD, D_MODEL), lambda i, j: (mod_row(i * bs), 0, 0)),
                  pl.BlockSpec((POOL_GROUPS, tm, t), lambda i, j: (0, j, 0)),
                  pl.BlockSpec((POOL_GROUPS, tm, 1), lambda i, j: (0, j, 0)),
                  const(POOL_GROUPS // 2, 2 * POOL_GROUP_DIM, 2 * POOL_GROUP_DIM), const(1, POOL_WIDTH),
                  const(D_MODEL, D_MODEL), const(1, D_MODEL), const(1, D_MODEL),
                  const(D_MODEL, 2 * LANES)],
        out_specs=[tok(D_MODEL), tok(D_MODEL),
                   pl.BlockSpec((bs, tm, D_MODEL // LANES, LANES), lambda i, j: (i, j, 0, 0)),
                   pl.BlockSpec((N_EXPERTS, bs * tm), lambda i, j: (0, i * (t // tm) + j))],
        out_shape=[sd((b, t, D_MODEL), F32), sd((b, t, D_MODEL), BF16),
                   sd((b, t, D_MODEL // LANES, LANES), BF16), sd((N_EXPERTS, b * t), F32)],
        compiler_params=_params("arbitrary", "arbitrary"),
        name="outproj",
    )(x, u, hf, hb, o, mod, pm, pinv, wp, ps, wo, gpm, gpf, wr)


def _router_kernel(lg_ref, br_ref, comb_ref, sel_ref, *, n_blocks):
    ng, gs = N_EXPERT_GROUPS, GROUP_SIZE
    neg = -jnp.inf
    lg = jnp.swapaxes(lg_ref[...], 0, 1)
    br = br_ref[...]
    s = [jax.nn.sigmoid(lg[j]) for j in range(gs)]
    biased = [s[j] + br[:, j, :] for j in range(gs)]
    fold = lambda op, xs: functools.reduce(op, xs)
    m1 = fold(jnp.maximum, biased)
    i1 = fold(jnp.minimum, [jnp.where(biased[j] == m1, j, gs) for j in range(gs)])
    m2 = fold(jnp.maximum, [jnp.where(i1 == j, neg, biased[j]) for j in range(gs)])
    cur = m1 + m2
    gi = lax.broadcasted_iota(jnp.int32, cur.shape, 0)
    gmask = jnp.zeros(cur.shape, F32)
    for _ in range(TOPK_GROUPS):
        mx = jnp.max(cur, axis=0, keepdims=True)
        ix = jnp.min(jnp.where(cur == mx, gi, ng), axis=0, keepdims=True)
        hit = gi == ix
        gmask = jnp.where(hit, 1.0, gmask)
        cur = jnp.where(hit, neg, cur)
    cand = [jnp.where(gmask > 0, biased[j], neg) for j in range(gs)]
    eidx = [gi * gs + j for j in range(gs)]
    selm = [jnp.zeros(cur.shape, F32) for _ in range(gs)]
    for _ in range(TOP_K):
        mx = jnp.max(fold(jnp.maximum, cand), axis=0, keepdims=True)
        ix = jnp.min(fold(jnp.minimum, [jnp.where(cand[j] == mx, eidx[j], N_EXPERTS) for j in range(gs)]),
                     axis=0, keepdims=True)
        for j in range(gs):
            hit = eidx[j] == ix
            selm[j] = jnp.where(hit, 1.0, selm[j])
            cand[j] = jnp.where(hit, neg, cand[j])
    sel = [selm[j] * s[j] for j in range(gs)]
    tot = jnp.sum(fold(jnp.add, sel), axis=0, keepdims=True)
    comb = [sel[j] / tot * ROUTED_SCALE for j in range(gs)]
    sel_ref[...] = jnp.swapaxes(jnp.stack(selm, axis=0), 0, 1)
    comb_e = jnp.swapaxes(jnp.stack(comb, axis=0), 0, 1).reshape(N_EXPERTS, -1)
    comb_t = jnp.concatenate([comb_e, jnp.zeros_like(comb_e)], axis=0).T
    comb_ref[...] = jnp.where(pl.program_id(0) < n_blocks, comb_t, 0.0)


def _router(logits_t, b_router):
    t = logits_t.shape[1]
    tl = 1024
    nb = t // tl
    shp = (N_EXPERT_GROUPS, GROUP_SIZE, t)
    blk = pl.BlockSpec((N_EXPERT_GROUPS, GROUP_SIZE, tl), lambda j: (0, 0, jnp.minimum(j, nb - 1)))
    comb, sel = pl.pallas_call(
        functools.partial(_router_kernel, n_blocks=nb),
        grid=(nb + 1,),
        in_specs=[blk, pl.BlockSpec((N_EXPERT_GROUPS, GROUP_SIZE, 1), lambda j: (0, 0, 0))],
        out_specs=[pl.BlockSpec((tl, LANES), lambda j: (j, 0)), blk],
        out_shape=[jax.ShapeDtypeStruct((t + tl, LANES), F32), jax.ShapeDtypeStruct(shp, F32)],
        compiler_params=_params("arbitrary"),
        name="router",
    )(logits_t.reshape(shp), b_router.reshape(N_EXPERT_GROUPS, GROUP_SIZE, 1))
    return comb, sel.reshape(N_EXPERTS, t)


def _plan_kernel(sel_ref, pos_ref, meta_ref, *, n_meta):
    t = sel_ref.shape[1]
    tm = float(MOE_TILE)
    sel = sel_ref[...]
    selb = sel.astype(BF16)
    blk = 256
    rr = lax.broadcasted_iota(jnp.int32, (blk, blk), 0)
    cc = lax.broadcasted_iota(jnp.int32, (blk, blk), 1)
    before = (rr < cc).astype(BF16)
    carry = jnp.zeros((N_EXPERTS, 1), F32)
    ranks = []
    for b in range(t // blk):
        sb = selb[:, b * blk:(b + 1) * blk]
        ranks.append(_dot(sb, before) + carry)
        carry = carry + jnp.sum(sel[:, b * blk:(b + 1) * blk], axis=1, keepdims=True)
    rank = jnp.concatenate(ranks, axis=1)
    cnt = carry
    ntile = jnp.floor((cnt + (tm - 0.5)) * (1.0 / tm))
    er = lax.broadcasted_iota(jnp.int32, (N_EXPERTS, N_EXPERTS), 0)
    ec = lax.broadcasted_iota(jnp.int32, (N_EXPERTS, N_EXPERTS), 1)
    below = (ec < er).astype(BF16)
    tstart = _dot(below, jnp.broadcast_to(ntile, (N_EXPERTS, LANES)).astype(BF16))[:, 0:1]
    pos = tstart * tm + rank
    erank = _dot(below, selb)
    rows = []
    for k in range(TOP_K):
        hit = (sel > 0.0) & (erank == float(k))
        rows.append(jnp.sum(jnp.where(hit, pos, 0.0), axis=0, keepdims=True))
    rows += [jnp.zeros((1, t), F32)] * (8 - TOP_K)
    pos_ref[...] = jnp.concatenate(rows, axis=0).astype(jnp.int32)

    tau = lax.broadcasted_iota(jnp.int32, (N_EXPERTS, n_meta), 1).astype(F32)
    eidx = lax.broadcasted_iota(jnp.int32, (N_EXPERTS, n_meta), 0).astype(F32)
    te = jnp.sum(((tstart + ntile) <= tau).astype(F32), axis=0, keepdims=True)
    te = jnp.minimum(te, float(N_EXPERTS - 1))
    onehot = eidx == te
    cnt_t = jnp.sum(jnp.where(onehot, cnt, 0.0), axis=0, keepdims=True)
    ts_t = jnp.sum(jnp.where(onehot, tstart, 0.0), axis=0, keepdims=True)
    tr = jnp.clip(cnt_t - (tau[0:1] - ts_t) * tm, 0.0, tm)
    tf = jnp.where((tau[0:1] == ts_t) & (tr > 0.0), 1.0, 0.0)
    meta_ref[...] = jnp.concatenate([te, tr, tf] + [jnp.zeros((1, n_meta), F32)] * 5, axis=0).astype(jnp.int32)


def _plan(sel, n_meta):
    t = sel.shape[1]
    sd = jax.ShapeDtypeStruct
    return pl.pallas_call(
        functools.partial(_plan_kernel, n_meta=n_meta),
        out_shape=[sd((8, t), jnp.int32), sd((8, n_meta), jnp.int32)],
        compiler_params=pltpu.CompilerParams(vmem_limit_bytes=VMEM_LIMIT),
        name="plan",
    )(sel)


def _slot_table(pos, n_slots, n_tokens):
    t = pos.shape[1]
    workers = SC_CORES * SC_SUBCORES
    per = -(-n_slots // (workers * SC_LANES)) * SC_LANES
    mesh = plsc.VectorSubcoreMesh(core_axis_name="core", subcore_axis_name="subcore")

    def body(pos_hbm, out_hbm, pos_v, tbl_v):
        lo = (lax.axis_index("core") * SC_SUBCORES + lax.axis_index("subcore")) * per
        for k in range(TOP_K):
            pltpu.sync_copy(pos_hbm.at[k], pos_v.at[pl.ds(k * t, t)])
        dummy = jnp.full((SC_LANES,), n_tokens, jnp.int32)

        @plsc.parallel_loop(0, per, SC_LANES, unroll=8)
        def _(i):
            tbl_v[pl.ds(i, SC_LANES)] = dummy

        lane = lax.iota(jnp.int32, SC_LANES)
        for k in range(TOP_K):
            @plsc.parallel_loop(0, t, SC_LANES, unroll=8)
            def _(i):
                p = pos_v[pl.ds(k * t + i, SC_LANES)] - lo
                mine = (p >= 0) & (p < per)
                plsc.store_scatter(tbl_v, [jnp.where(mine, p, 0)], lane + i, mask=mine)
        pltpu.sync_copy(tbl_v, out_hbm.at[pl.ds(lo, per)])

    build = pl.kernel(body, out_type=jax.ShapeDtypeStruct((workers * per,), jnp.int32), mesh=mesh,
                      scratch_types=[pltpu.VMEM((TOP_K * t,), jnp.int32), pltpu.VMEM((per,), jnp.int32)],
                      compiler_params=pltpu.CompilerParams(needs_layout_passes=False), name="slot_table")
    return build(pos)


def _experts_kernel(te_ref, tr_ref, xc_ref, xs_ref, slots_ref, comb_ref, wgu_ref, wd_ref, acc_out,
                    tbl_ref, xbuf, acc, stage0, stage1, cst0, cst1, act0, act1, ybuf0, ybuf1,
                    wgu_b0, wgu_b1, wd_b0, wd_b1, sems):
    s = pl.program_id(0)
    n_tiles = te_ref.shape[0]
    tm = MOE_TILE
    nch = D_MODEL // LANES
    tile_at = lambda lag: jnp.clip(s - lag, 0, n_tiles - 1)
    t_g, t_1, t_2, t_3 = tile_at(0), tile_at(1), tile_at(2), tile_at(3)

    @pl.when(s == 0)
    def _():
        tc, ts = xc_ref.shape[0], xs_ref.shape[0]
        copies = (pltpu.make_async_copy(xc_ref, xbuf.at[pl.ds(0, tc)], sems.at[0]),
                  pltpu.make_async_copy(xs_ref, xbuf.at[pl.ds(tc, ts)], sems.at[1]))
        tbl_init = pltpu.make_async_copy(slots_ref, tbl_ref, sems.at[3])
        tbl_init.start()
        for cp in copies:
            cp.start()
        n_pad = xbuf.shape[0] - tc - ts
        xbuf[pl.ds(tc + ts, n_pad)] = jnp.zeros((n_pad,) + xbuf.shape[1:], BF16)
        for ref in (acc, stage0, stage1, cst0, cst1, act0, act1, ybuf0, ybuf1, wgu_b0, wgu_b1, wd_b0, wd_b1):
            ref[...] = jnp.zeros_like(ref)
        tbl_init.wait()
        for cp in copies:
            cp.wait()

    def gather(tile, stage, cst):
        base = tile * tm
        for j in range(tm):
            tok = tbl_ref[base + j]
            stage[pl.ds(j * nch, nch), :] = xbuf[tok].astype(F32)
            cst[pl.ds(j, 1), :] = comb_ref[pl.ds(tok, 1), :]

    def gate_up(tile, stage, cst, wgu_b, act):
        xb = _token_rows_from_slabs(
            lambda g: stage[g * 8 * nch:(g + 1) * 8 * nch, :].reshape(8, nch, LANES), tm).astype(BF16)
        gu = _dot(xb, wgu_b[...])
        lane = lax.broadcasted_iota(jnp.int32, (1, LANES), 1)
        w_col = jnp.sum(jnp.where(lane == te_ref[tile], cst[...], 0.0), axis=1, keepdims=True)
        act[...] = (_silu(gu[:, :EXPERT_DIM]) * gu[:, EXPERT_DIM:] * w_col).astype(BF16)

    def down(act, wd_b, ybuf):
        y = _dot(act[...], wd_b[...])
        for cc in range(nch):
            ybuf[cc * Y_PITCH:cc * Y_PITCH + tm, :] = y[:, cc * LANES:(cc + 1) * LANES]

    def scatter(tile, ybuf):
        base = tile * tm
        sc_n = 16
        for i in range(tm // sc_n):
            toks = [tbl_ref[base + i * sc_n + u] for u in range(sc_n)]
            olds = [acc[toks[u]] for u in range(sc_n)]
            news = [olds[u] + ybuf[pl.ds(i * sc_n + u, nch, stride=Y_PITCH), :] for u in range(sc_n)]
            for u in range(sc_n):
                acc[toks[u]] = news[u]

    busy = (tr_ref[t_g] + tr_ref[t_1] + tr_ref[t_2] + tr_ref[t_3]) > 0
    bufs = ((stage0, cst0, act0, ybuf0, wgu_b0, wd_b0), (stage1, cst1, act1, ybuf1, wgu_b1, wd_b1))
    for par in range(2):
        stage_p, cst_p, act_p, ybuf_p, wgu_p, wd_p = bufs[par]
        stage_q, cst_q, act_q, ybuf_q, wgu_q, wd_q = bufs[1 - par]

        @pl.when(busy & (s % 2 == par))
        def _():
            gather(t_g, stage_p, cst_p)
            wgu_p[...] = wgu_ref[0].astype(BF16)
            gate_up(t_1, stage_q, cst_q, wgu_q, act_q)
            wd_q[...] = wd_ref[0].astype(BF16)
            down(act_p, wd_p, ybuf_p)
            scatter(t_3, ybuf_q)

    @pl.when(s == pl.num_programs(0) - 1)
    def _():
        cp = pltpu.make_async_copy(acc, acc_out, sems.at[2])
        cp.start()
        cp.wait()


def _experts(te, tr, xc, xs, slots, comb, wgu, wd):
    n_tok = xc.shape[0] + xs.shape[0] + 8
    n_tiles = te.shape[0]
    tm = MOE_TILE
    nch = D_MODEL // LANES
    vm = pltpu.VMEM
    grid_spec = pltpu.PrefetchScalarGridSpec(
        num_scalar_prefetch=2,
        grid=(n_tiles + 3,),
        in_specs=[pl.BlockSpec(memory_space=pl.ANY), pl.BlockSpec(memory_space=pl.ANY),
                  pl.BlockSpec(memory_space=pl.ANY),
                  pl.BlockSpec((n_tok, LANES), lambda s, *_: (0, 0), pipeline_mode=pl.Buffered(1)),
                  pl.BlockSpec((1, D_MODEL, 2 * EXPERT_DIM), lambda s, te, *_: (te[jnp.minimum(s, n_tiles - 1)], 0, 0)),
                  pl.BlockSpec((1, EXPERT_DIM, D_MODEL), lambda s, te, *_: (te[jnp.clip(s - 1, 0, n_tiles - 1)], 0, 0))],
        out_specs=pl.BlockSpec(memory_space=pl.ANY),
        scratch_shapes=[pltpu.SMEM(slots.shape, jnp.int32),
                        vm((n_tok, nch, LANES), BF16), vm((n_tok, nch, LANES), F32),
                        vm((tm * nch, LANES), F32), vm((tm * nch, LANES), F32),
                        vm((tm, LANES), F32), vm((tm, LANES), F32),
                        vm((tm, EXPERT_DIM), BF16), vm((tm, EXPERT_DIM), BF16),
                        vm((nch * Y_PITCH, LANES), F32), vm((nch * Y_PITCH, LANES), F32),
                        vm((D_MODEL, 2 * EXPERT_DIM), BF16), vm((D_MODEL, 2 * EXPERT_DIM), BF16),
                        vm((EXPERT_DIM, D_MODEL), BF16), vm((EXPERT_DIM, D_MODEL), BF16),
                        pltpu.SemaphoreType.DMA((4,))],
    )
    return pl.pallas_call(
        _experts_kernel,
        grid_spec=grid_spec,
        out_shape=jax.ShapeDtypeStruct((n_tok, nch, LANES), F32),
        compiler_params=pltpu.CompilerParams(dimension_semantics=("arbitrary",),
                                             vmem_limit_bytes=EXPERTS_VMEM_LIMIT),
        name="experts",
    )(te, tr, xc, xs, slots, comb, wgu, wd)


def _final_kernel(acc_ref, h_ref, x1_ref, mod_ref, wsg_ref, wsd_ref, gpo_ref, out_ref):
    routed = _token_rows_from_slabs(lambda g: acc_ref[g * 8:(g + 1) * 8], acc_ref.shape[0])
    gs = _dot(h_ref[...], wsg_ref[...])
    act = _silu(gs[:, :SHARED_DIM]) * gs[:, SHARED_DIM:]
    f = routed + _dot(act.astype(BF16), wsd_ref[...])
    out_ref[...] = x1_ref[...] + mod_ref[0][5:6] * _rmsnorm(f, gpo_ref[...])


def _final(acc, tile0, h2, x1, mod, mod_row, wsg, wsd, gpo):
    n = h2.shape[0]
    tm = FINAL_TILE
    const = lambda *shape: pl.BlockSpec(shape, lambda i: (0,) * len(shape))
    tok = lambda w: pl.BlockSpec((tm, w), lambda i: (i, 0))
    return pl.pallas_call(
        _final_kernel,
        grid=(n // tm,),
        in_specs=[pl.BlockSpec((tm, D_MODEL // LANES, LANES), lambda i: (i + tile0, 0, 0)),
                  tok(D_MODEL), tok(D_MODEL),
                  pl.BlockSpec((1, N_MOD, D_MODEL), lambda i: (mod_row(i), 0, 0)),
                  const(D_MODEL, 2 * SHARED_DIM), const(SHARED_DIM, D_MODEL), const(1, D_MODEL)],
        out_specs=tok(D_MODEL),
        out_shape=jax.ShapeDtypeStruct((n, D_MODEL), F32),
        compiler_params=_params("arbitrary"),
        name="final",
    )(acc, h2, x1, mod, wsg, wsd, gpo)


def _window_bounds(n, w):
    idx = np.arange(n)
    return np.clip(idx - w // 2, 0, n), np.clip(idx + w - w // 2, 0, n)


def _pool_operators(t, grid):
    mats, invs = [], []
    for w in POOL_WINDOWS:
        if grid:
            rlo, rhi = _window_bounds(t // GRID_W, w)
            clo, chi = _window_bounds(GRID_W, w)
            r = np.arange(t) // GRID_W
            c = np.arange(t) % GRID_W
            m = ((r[None, :] >= rlo[r][:, None]) & (r[None, :] < rhi[r][:, None])
                 & (c[None, :] >= clo[c][:, None]) & (c[None, :] < chi[c][:, None]))
            cnt = (rhi - rlo)[r] * (chi - clo)[c]
        else:
            lo, hi = _window_bounds(t, w)
            sidx = np.arange(t)
            m = (sidx[None, :] >= lo[:, None]) & (sidx[None, :] < hi[:, None])
            cnt = hi - lo
        mats.append(m.astype(np.float32))
        invs.append((1.0 / cnt.astype(np.float64)).astype(np.float32)[:, None])
    return jnp.asarray(np.stack(mats), BF16), jnp.asarray(np.stack(invs), F32)


def kernel(x_prompt, x_sample, state_C, state_n, state_m, c, c_ctx, w_ada, b_ada, g_pre_mix, w_in, b_gate,
           w_pool, pool_scale, w_out, g_post_mix, g_pre_ffn, w_router, b_router, w_expert_gu, w_expert_down,
           w_shared_gu, w_shared_down, g_post_ffn):
    b_ctx = x_prompt.shape[0]
    b_lat = x_sample.shape[0]
    nu = N_DIR * HEADS
    l = 0
    row = lambda a: a[l].reshape(1, -1).astype(F32)

    cvec = jnp.zeros((16, D_MODEL), F32).at[0].set(c_ctx.astype(F32)).at[1:1 + b_lat].set(c.astype(F32))
    mod = _mod_rows(cvec, w_ada[l], b_ada[l]).reshape(16, N_MOD, D_MODEL)

    w_in_l = w_in[l]
    p0 = POOL_WIDTH
    mw = MLSTM_WIDTH
    w_u, w_q, w_k, w_v, w_o = (w_in_l[:, lo:lo + 512] for lo in (0, p0, p0 + mw, p0 + 2 * mw, p0 + 3 * mw))
    wm = jnp.concatenate([w_u, w_k, w_o], axis=1).astype(BF16)
    wt = jnp.concatenate([w_q.T, w_v.T], axis=0).astype(BF16)
    wg_cols = w_in_l[:, p0 + 4 * mw:]
    wg = jnp.pad(jnp.concatenate(_split2(wg_cols), axis=1), ((0, 0), (0, LANES - 2 * GATE_COLS)))
    bg = b_gate[l].reshape(GATE_COLS).astype(F32)
    bgr = jnp.pad(bg, (0, LANES - GATE_COLS)).reshape(1, LANES)
    wpl = w_pool[l].astype(BF16)
    zg = jnp.zeros((POOL_GROUP_DIM, POOL_GROUP_DIM), BF16)
    wp = jnp.stack([jnp.block([[wpl[2 * p], zg], [zg, wpl[2 * p + 1]]]) for p in range(POOL_GROUPS // 2)])
    wo = w_out[l].astype(BF16)
    wr = jnp.pad(jnp.concatenate(_split3(w_router[l].astype(F32)), axis=1), ((0, 0), (0, 2 * LANES - 3 * N_EXPERTS)))
    wsg = w_shared_gu[l].astype(BF16)
    wsd = w_shared_down[l].astype(BF16)

    def mixer(x, mod_row, grid, s0, m0, emit_state):
        t = x.shape[1]
        u, k, o, qt, vt, gate, gate_t = _inproj(x.astype(F32), mod, mod_row, row(g_pre_mix), wm, wt, wg, bgr)
        outs = _mlstm(k, qt, vt, gate, gate_t, s0, m0, emit_state)
        hf, hb = outs[0], outs[1]
        pm, pinv = _pool_operators(t, grid)
        x1, h2, xp, lg = _outproj(x.astype(F32), u, hf, hb, o, mod, mod_row, pm, pinv, wp, row(pool_scale), wo,
                                  row(g_post_mix), row(g_pre_ffn), wr)
        return x1, h2, xp, lg, outs[2:]

    ctx_row = lambda i: 0
    lat_row = lambda i: i + 1
    x1c, h2c, xpc, lgc, (c_new, n_new, m_new) = mixer(x_prompt, ctx_row, False, None, None, True)
    s0 = jnp.concatenate(
        [jnp.swapaxes(state_C[:, l].reshape(b_lat, nu, HEAD_DIM, HEAD_DIM).astype(F32), -1, -2),
         jnp.broadcast_to(state_n[:, l].reshape(b_lat, nu, 1, HEAD_DIM).astype(F32),
                          (b_lat, nu, N_ROWS, HEAD_DIM))], axis=-2)
    m0 = jnp.broadcast_to(state_m[:, l].reshape(b_lat, nu, 1, 1).astype(F32), (b_lat, nu, 1, LANES))
    x1s, h2s, xps, lgs, _ = mixer(x_sample, lat_row, True, s0, m0, False)

    tc = b_ctx * x_prompt.shape[1]
    ts = b_lat * x_sample.shape[1]
    n_tok = tc + ts
    lg_all = jnp.concatenate([lgc, lgs], axis=1)
    comb_tok, sel = _router(lg_all, b_router[l].astype(F32))
    n_tiles = n_tok * TOP_K // MOE_TILE + N_EXPERTS
    n_meta = -(-n_tiles // LANES) * LANES
    pos, meta = _plan(sel, n_meta)

    slab = (D_MODEL // LANES, LANES)
    acc = _experts(meta[0, :n_tiles], meta[1, :n_tiles],
                   xpc.reshape((tc,) + slab), xps.reshape((ts,) + slab),
                   _slot_table(pos, n_tiles * MOE_TILE, n_tok), comb_tok,
                   w_expert_gu[l], w_expert_down[l])

    fin = functools.partial(_final, wsg=wsg, wsd=wsd, gpo=row(g_post_ffn))
    tiles_per_lat = x_sample.shape[1] // FINAL_TILE
    yc = fin(acc, 0, h2c.reshape(tc, D_MODEL), x1c.reshape(tc, D_MODEL), mod, ctx_row)
    ys = fin(acc, tc // FINAL_TILE, h2s.reshape(ts, D_MODEL), x1s.reshape(ts, D_MODEL), mod,
             lambda i: i // tiles_per_lat + 1)

    new_c = c_new.reshape(b_ctx, 1, N_DIR, HEADS, HEAD_DIM, HEAD_DIM)
    new_n = n_new.reshape(b_ctx, 1, N_DIR, HEADS, HEAD_DIM)
    new_m = m_new[..., 0].reshape(b_ctx, 1, N_DIR, HEADS)
    return (yc.reshape(x_prompt.shape), ys.reshape(x_sample.shape), new_c, new_n, new_m)
```

```python
import functools

import jax
import jax.numpy as jnp
import numpy as np
from jax import lax
from jax.experimental import pallas as pl
from jax.experimental.pallas import tpu as pltpu
from jax.experimental.pallas import tpu_sc as plsc

F32 = jnp.float32
BF16 = jnp.bfloat16

D_MODEL = 1024
GRID_W = 64
POOL_WIDTH = 512
POOL_GROUPS = 4
POOL_GROUP_DIM = 128
POOL_WINDOWS = (2, 4, 8, 16)
HEADS = 4
HEAD_DIM = 128
MLSTM_WIDTH = HEADS * HEAD_DIM
N_DIR = 2
GATE_COLS = N_DIR * 2 * HEADS
N_EXPERTS = 64
TOP_K = 6
N_EXPERT_GROUPS = 8
GROUP_SIZE = N_EXPERTS // N_EXPERT_GROUPS
TOPK_GROUPS = 4
EXPERT_DIM = 256
SHARED_DIM = 256
ROUTED_SCALE = 2.5
N_MOD = 6
EPS = 1e-6
K_SCALE = HEAD_DIM ** -0.5

LANES = 128
CHUNK = 256
N_ROWS = 16
TOKEN_TILE = 256
FINAL_TILE = 512
MOE_TILE = 320
Y_PITCH = MOE_TILE + 8
VMEM_LIMIT = 56 * 1024 * 1024
SC_CORES = 2
SC_SUBCORES = 16
SC_LANES = 16
EXPERTS_VMEM_LIMIT = 58 * 1024 * 1024


def _split3(x):
    p1 = x.astype(BF16)
    r1 = x - p1.astype(F32)
    p2 = r1.astype(BF16)
    p3 = (r1 - p2.astype(F32)).astype(BF16)
    return p1, p2, p3


def _split2(x):
    p1 = x.astype(BF16)
    p2 = (x - p1.astype(F32)).astype(BF16)
    return p1, p2


def _dot(a, b):
    return jnp.dot(a, b, preferred_element_type=F32)


def _dot_nt(a, b):
    return lax.dot_general(a, b, (((1,), (1,)), ((), ())), preferred_element_type=F32)


def _rmsnorm(x, g):
    return x * lax.rsqrt(jnp.mean(x * x, axis=-1, keepdims=True) + EPS) * g


def _silu(x):
    return x * jax.nn.sigmoid(x)


def _token_rows_from_slabs(read_block, n_tok):
    nch = D_MODEL // LANES
    cols = [[] for _ in range(nch)]
    for g in range(n_tok // 8):
        blk = jnp.swapaxes(read_block(g), 0, 1)
        for cc in range(nch):
            cols[cc].append(blk[cc])
    return jnp.concatenate([jnp.concatenate(c, axis=0) for c in cols], axis=1)


def _params(*sem):
    return pltpu.CompilerParams(dimension_semantics=sem, vmem_limit_bytes=VMEM_LIMIT)


def _mod_kernel(c_ref, w_ref, b_ref, o_ref):
    a = _silu(c_ref[...])
    a_stack = jnp.concatenate(_split3(a), axis=0)
    w1, w2 = _split2(w_ref[...])
    r1 = _dot(a_stack, w1)
    r2 = _dot(a_stack[:32], w2)
    o_ref[...] = (r1[0:16] + r1[16:32] + r1[32:48] + r2[0:16] + r2[16:32]) + b_ref[...]


def _mod_rows(cvec, w_ada, b_ada):
    n = N_MOD * D_MODEL
    tn = 1536
    return pl.pallas_call(
        _mod_kernel,
        grid=(n // tn,),
        in_specs=[pl.BlockSpec((16, D_MODEL), lambda j: (0, 0)),
                  pl.BlockSpec((D_MODEL, tn), lambda j: (0, j)),
                  pl.BlockSpec((1, tn), lambda j: (0, j))],
        out_specs=pl.BlockSpec((16, tn), lambda j: (0, j)),
        out_shape=jax.ShapeDtypeStruct((16, n), F32),
        compiler_params=_params("arbitrary"),
        name="mod",
    )(cvec, w_ada, b_ada.reshape(1, n))


def _inproj_kernel(x_ref, mod_ref, g_ref, wm_ref, wt_ref, wg_ref, bgr_ref,
                   u_ref, k_ref, o_ref, qt_ref, vt_ref, gate_ref, gatet_ref):
    bs, tm, _ = x_ref.shape
    rows = bs * tm
    x = x_ref[...].reshape(rows, D_MODEL)
    mod = mod_ref[0]
    h = _rmsnorm(x, g_ref[...]) * (1.0 + mod[1:2]) + mod[0:1]
    h1, h2, h3 = _split3(h)
    z = _dot(h1, wm_ref[...])
    u_ref[...] = z[:, 0:512].astype(BF16).reshape(bs, tm, 512)
    k_ref[...] = (z[:, 512:1024] * K_SCALE).astype(BF16).reshape(bs, tm, 512)
    o_ref[...] = z[:, 1024:1536].astype(BF16).reshape(bs, tm, 512)
    zt = _dot_nt(wt_ref[...], h1).astype(BF16)
    r = _dot(jnp.concatenate([h1, h2, h3], axis=0), wg_ref[...])
    r12 = r[0:rows] + r[rows:2 * rows]
    gate = (r12 + r[2 * rows:]) + pltpu.roll(r12, LANES - GATE_COLS, axis=1) + bgr_ref[...]
    gate_ref[...] = gate.reshape(bs, tm, LANES)
    gate_t = gate.T
    for bb in range(bs):
        cols = slice(bb * tm, (bb + 1) * tm)
        qt_ref[bb] = zt[0:512, cols]
        vt_ref[bb] = zt[512:1024, cols]
        gatet_ref[bb] = gate_t[0:16, cols]


def _inproj(x, mod, mod_row, g, wm, wt, wg, bgr):
    b, t, _ = x.shape
    tm = min(t, 2 * TOKEN_TILE)
    bs = 2 * TOKEN_TILE // tm if mod_row(1) == mod_row(0) else 1
    const = lambda *shape: pl.BlockSpec(shape, lambda i, j: (0,) * len(shape))
    tok = lambda w: pl.BlockSpec((bs, tm, w), lambda i, j: (i, j, 0))
    tok_t = lambda r: pl.BlockSpec((bs, r, tm), lambda i, j: (i, 0, j))
    sd = jax.ShapeDtypeStruct
    return pl.pallas_call(
        _inproj_kernel,
        grid=(b // bs, t // tm),
        in_specs=[tok(D_MODEL),
                  pl.BlockSpec((1, N_MOD, D_MODEL), lambda i, j: (mod_row(i * bs), 0, 0)),
                  const(1, D_MODEL), const(D_MODEL, 1536), const(1024, D_MODEL),
                  const(D_MODEL, LANES), const(1, LANES)],
        out_specs=[tok(512), tok(512), tok(512), tok_t(512), tok_t(512), tok(LANES), tok_t(16)],
        out_shape=[sd((b, t, 512), BF16), sd((b, t, 512), BF16), sd((b, t, 512), BF16),
                   sd((b, 512, t), BF16), sd((b, 512, t), BF16), sd((b, t, LANES), F32),
                   sd((b, 16, t), F32)],
        compiler_params=_params("arbitrary", "arbitrary"),
        name="inproj",
    )(x, mod, g, wm, wt, wg, bgr)


def _log_sigmoid(x):
    return jnp.minimum(x, 0.0) - jnp.log1p(jnp.exp(-jnp.abs(x)))


def _scan_unit(st, k, qt, vt, u_col, u_row, b_row, btot, mask, s_prev, m_prev, use_state):
    dh = HEAD_DIM
    n = st.shape[0]
    ub = jnp.where(mask, jnp.broadcast_to(u_col, (n, n)), -jnp.inf)
    z = jnp.maximum(m_prev, jnp.max(ub, axis=0, keepdims=True))
    p = (jnp.exp(ub - z) * st).astype(BF16)
    ones = jnp.ones((N_ROWS, n), BF16)
    tot = _dot(jnp.concatenate([vt, ones], axis=0), p)
    if use_state:
        tot = tot + jnp.exp(m_prev - z) * _dot(s_prev.astype(BF16), qt)
    floor = jnp.exp(-(b_row + z))
    h_t = tot[:dh] / jnp.maximum(jnp.abs(tot[dh:dh + 1]), floor)
    g_row = btot + u_row
    m_new = jnp.maximum(btot + m_prev, jnp.max(g_row, axis=-1, keepdims=True))
    w_row = jnp.exp(g_row - m_new)
    vw = jnp.concatenate([(vt.astype(F32) * w_row).astype(BF16),
                          jnp.broadcast_to(w_row, (N_ROWS, n)).astype(BF16)], axis=0)
    s_new = jnp.exp(btot + m_prev - m_new) * s_prev + _dot(vw, k)
    return h_t.T, s_new, m_new


def _mlstm_kernel(*refs, nc, zero_init, emit_state):
    it = iter(refs)
    fwd_refs = tuple(next(it) for _ in range(5))
    bwd_refs = tuple(next(it) for _ in range(5)) if nc > 1 else fwd_refs
    if not zero_init:
        s0_ref, m0_ref = next(it), next(it)
    h_refs = (next(it), next(it))
    if emit_state:
        c_out, n_out, m_out = next(it), next(it), next(it)
    s_scr, m_scr = next(it), next(it)

    j = pl.program_id(1)
    n = CHUNK
    dh = HEAD_DIM

    @pl.when(j == 0)
    def _():
        if zero_init:
            s_scr[...] = jnp.zeros_like(s_scr)
            m_scr[...] = jnp.zeros_like(m_scr)
        else:
            s_scr[...] = s0_ref[0]
            m_scr[...] = m0_ref[0]

    rows = lax.broadcasted_iota(jnp.int32, (n, n), 0)
    cols = lax.broadcasted_iota(jnp.int32, (n, n), 1)
    le = rows <= cols
    ge = rows >= cols
    tri_le = le.astype(BF16)
    tri_ge = ge.astype(BF16)
    use_state = not (zero_init and nc == 1)

    def gate_terms(d):
        g_ref, gt_ref = (fwd_refs, bwd_refs)[d][3:5]
        gate = g_ref[0]
        gate_t = gt_ref[0]
        lf = _log_sigmoid(gate)
        lf_t = _log_sigmoid(gate_t)
        tri_c, tri_r = (tri_ge, tri_le) if d == 0 else (tri_le, tri_ge)
        bc = _dot(tri_c, jnp.concatenate(_split3(lf), axis=1))
        b_cols = bc[:, 0:128] + bc[:, 128:256] + bc[:, 256:384]
        br = _dot(jnp.concatenate(_split3(lf_t), axis=0), tri_r)
        b_rows = br[0:16] + br[16:32] + br[32:48]
        return gate, gate_t, b_cols, b_rows, jnp.sum(lf_t, axis=-1, keepdims=True)

    terms = [gate_terms(0), gate_terms(1)]
    hs = ([], [])
    for hd in range(HEADS):
        hsl = slice(hd * dh, (hd + 1) * dh)
        st = None
        for d in range(N_DIR):
            k_ref, qt_ref, vt_ref = (fwd_refs, bwd_refs)[d][0:3]
            gate, gate_t, b_cols, b_rows, tot_rows = terms[d]
            ci = d * 8 + hd
            cf = d * 8 + 4 + hd
            unit = d * HEADS + hd
            k = k_ref[0, :, hsl]
            qt = qt_ref[0, hsl, :]
            if st is None or nc > 1:
                st = _dot(k, qt)
            mask = le if d == 0 else ge
            h, s_new, m_new = _scan_unit(
                st, k, qt, vt_ref[0, hsl, :],
                gate[:, ci:ci + 1] - b_cols[:, cf:cf + 1],
                gate_t[ci:ci + 1, :] - b_rows[cf:cf + 1, :],
                b_rows[cf:cf + 1, :], tot_rows[cf:cf + 1, :],
                mask, s_scr[unit], m_scr[unit][:, 0:1], use_state)
            s_scr[unit] = s_new
            m_scr[unit] = jnp.broadcast_to(m_new, (1, LANES))
            hs[d].append(h)
    for d in range(N_DIR):
        h_refs[d][0] = jnp.concatenate(hs[d], axis=1).astype(BF16)

    if emit_state:
        @pl.when(j == nc - 1)
        def _():
            for unit in range(N_DIR * HEADS):
                s = s_scr[unit]
                c_out[0, unit] = s[:dh].T
                n_out[0, unit] = s[dh:dh + 1]
                m_out[0, unit] = m_scr[unit]


def _mlstm(k, qt, vt, gate, gate_t, s0, m0, emit_state):
    b, t, _ = k.shape
    nc = t // CHUNK
    zero_init = s0 is None
    nu = N_DIR * HEADS
    fwd = lambda w: pl.BlockSpec((1, CHUNK, w), lambda i, j: (i, j, 0))
    bwd = lambda w: pl.BlockSpec((1, CHUNK, w), lambda i, j: (i, nc - 1 - j, 0))
    fwd_t = lambda r: pl.BlockSpec((1, r, CHUNK), lambda i, j: (i, 0, j))
    bwd_t = lambda r: pl.BlockSpec((1, r, CHUNK), lambda i, j: (i, 0, nc - 1 - j))
    args = [k, qt, vt, gate, gate_t]
    in_specs = [fwd(512), fwd_t(512), fwd_t(512), fwd(LANES), fwd_t(16)]
    if nc > 1:
        args += [k, qt, vt, gate, gate_t]
        in_specs += [bwd(512), bwd_t(512), bwd_t(512), bwd(LANES), bwd_t(16)]
    if not zero_init:
        args += [s0, m0]
        in_specs += [pl.BlockSpec((1, nu, HEAD_DIM + N_ROWS, HEAD_DIM), lambda i, j: (i, 0, 0, 0)),
                     pl.BlockSpec((1, nu, 1, LANES), lambda i, j: (i, 0, 0, 0))]
    sd = jax.ShapeDtypeStruct
    out_shape = [sd((b, t, 512), BF16), sd((b, t, 512), BF16)]
    out_specs = [fwd(512), bwd(512)]
    if emit_state:
        out_shape += [sd((b, nu, HEAD_DIM, HEAD_DIM), F32), sd((b, nu, 1, HEAD_DIM), F32),
                      sd((b, nu, 1, LANES), F32)]
        out_specs += [pl.BlockSpec((1, nu, HEAD_DIM, HEAD_DIM), lambda i, j: (i, 0, 0, 0)),
                      pl.BlockSpec((1, nu, 1, HEAD_DIM), lambda i, j: (i, 0, 0, 0)),
                      pl.BlockSpec((1, nu, 1, LANES), lambda i, j: (i, 0, 0, 0))]
    return pl.pallas_call(
        functools.partial(_mlstm_kernel, nc=nc, zero_init=zero_init, emit_state=emit_state),
        grid=(b, nc),
        in_specs=in_specs,
        out_specs=out_specs,
        out_shape=out_shape,
        scratch_shapes=[pltpu.VMEM((nu, HEAD_DIM + N_ROWS, HEAD_DIM), F32),
                        pltpu.VMEM((nu, 1, LANES), F32)],
        compiler_params=_params("arbitrary", "arbitrary"),
        name="mlstm",
    )(*args)


def _outproj_kernel(x_ref, u_ref, hf_ref, hb_ref, o_ref, mod_ref, pm_ref, pinv_ref, wp_ref, ps_ref,
                    wo_ref, gpm_ref, gpf_ref, wr_ref, x1_ref, h2_ref, xp_ref, lg_ref):
    bs, tm, _ = x_ref.shape
    rows = bs * tm
    x = x_ref[...].reshape(rows, D_MODEL)
    mod = mod_ref[0]
    row0 = pl.multiple_of(pl.program_id(1) * tm, tm)
    diffs = []
    for g in range(POOL_GROUPS):
        sl = slice(g * POOL_GROUP_DIM, (g + 1) * POOL_GROUP_DIM)
        per_seq = []
        for bb in range(bs):
            box = _dot(pm_ref[g], u_ref[bb, :, sl])
            per_seq.append(box * pinv_ref[g] - u_ref[bb, pl.ds(row0, tm), sl].astype(F32))
        diffs.append(jnp.concatenate(per_seq, axis=0).astype(BF16))
    yps = [_dot(jnp.concatenate(diffs[2 * p:2 * p + 2], axis=1), wp_ref[p]) for p in range(POOL_GROUPS // 2)]
    y_pool = jnp.concatenate(yps, axis=1) * ps_ref[...]
    seq_rows = lambda ref: ref[...].reshape(rows, ref.shape[-1]).astype(F32)
    y_ml = jax.nn.sigmoid(seq_rows(o_ref)) * (seq_rows(hf_ref) + seq_rows(hb_ref))
    mix = _dot(jnp.concatenate([y_pool, y_ml], axis=1).astype(BF16), wo_ref[...])
    x1 = x + mod[2:3] * _rmsnorm(mix, gpm_ref[...])
    x1_ref[...] = x1.reshape(bs, tm, D_MODEL)
    h2 = _rmsnorm(x1, gpf_ref[...]) * (1.0 + mod[4:5]) + mod[3:4]
    p1, p2, p3 = _split3(h2)
    h2_ref[...] = p1.reshape(bs, tm, D_MODEL)
    nch = D_MODEL // LANES
    for g in range(rows // 8):
        cols = jnp.stack([h2[g * 8:(g + 1) * 8, cc * LANES:(cc + 1) * LANES] for cc in range(nch)], axis=0)
        bb, r0 = divmod(g * 8, tm)
        xp_ref[bb, r0:r0 + 8] = jnp.swapaxes(cols, 0, 1).astype(BF16)
    r = _dot(jnp.concatenate([p1, p2, p3], axis=0), wr_ref[...])
    r12 = r[0:rows] + r[rows:2 * rows]
    ne = N_EXPERTS
    lg = (r12 + r[2 * rows:])[:, 0:ne] + r12[:, ne:2 * ne] + r[0:rows, 2 * ne:3 * ne]
    lg_ref[...] = jnp.concatenate([lg, jnp.zeros_like(lg)], axis=1).T[0:ne]


def _outproj(x, u, hf, hb, o, mod, mod_row, pm, pinv, wp, ps, wo, gpm, gpf, wr):
    b, t, _ = x.shape
    tm = min(t, 2 * TOKEN_TILE)
    bs = 2 * TOKEN_TILE // tm if mod_row(1) == mod_row(0) else 1
    const = lambda *shape: pl.BlockSpec(shape, lambda i, j: (0,) * len(shape))
    tok = lambda w: pl.BlockSpec((bs, tm, w), lambda i, j: (i, j, 0))
    sd = jax.ShapeDtypeStruct
    return pl.pallas_call(
        _outproj_kernel,
        grid=(b // bs, t // tm),
        in_specs=[tok(D_MODEL),
                  pl.BlockSpec((bs, t, 512), lambda i, j: (i, 0, 0)),
                  tok(512), tok(512), tok(512),
                  pl.BlockSpec((1, N_MOD, D_MODEL), lambda i, j: (mod_row(i * bs), 0, 0)),
                  pl.BlockSpec((POOL_GROUPS, tm, t), lambda i, j: (0, j, 0)),
                  pl.BlockSpec((POOL_GROUPS, tm, 1), lambda i, j: (0, j, 0)),
                  const(POOL_GROUPS // 2, 2 * POOL_GROUP_DIM, 2 * POOL_GROUP_DIM), const(1, POOL_WIDTH),
                  const(D_MODEL, D_MODEL), const(1, D_MODEL), const(1, D_MODEL),
                  const(D_MODEL, 2 * LANES)],
        out_specs=[tok(D_MODEL), tok(D_MODEL),
                   pl.BlockSpec((bs, tm, D_MODEL // LANES, LANES), lambda i, j: (i, j, 0, 0)),
                   pl.BlockSpec((N_EXPERTS, bs * tm), lambda i, j: (0, i * (t // tm) + j))],
        out_shape=[sd((b, t, D_MODEL), F32), sd((b, t, D_MODEL), BF16),
                   sd((b, t, D_MODEL // LANES, LANES), BF16), sd((N_EXPERTS, b * t), F32)],
        compiler_params=_params("arbitrary", "arbitrary"),
        name="outproj",
    )(x, u, hf, hb, o, mod, pm, pinv, wp, ps, wo, gpm, gpf, wr)


def _router_kernel(lg_ref, br_ref, comb_ref, sel_ref, *, n_blocks):
    ng, gs = N_EXPERT_GROUPS, GROUP_SIZE
    neg = -jnp.inf
    lg = jnp.swapaxes(lg_ref[...], 0, 1)
    br = br_ref[...]
    s = [jax.nn.sigmoid(lg[j]) for j in range(gs)]
    biased = [s[j] + br[:, j, :] for j in range(gs)]
    fold = lambda op, xs: functools.reduce(op, xs)
    m1 = fold(jnp.maximum, biased)
    i1 = fold(jnp.minimum, [jnp.where(biased[j] == m1, j, gs) for j in range(gs)])
    m2 = fold(jnp.maximum, [jnp.where(i1 == j, neg, biased[j]) for j in range(gs)])
    cur = m1 + m2
    gi = lax.broadcasted_iota(jnp.int32, cur.shape, 0)
    gmask = jnp.zeros(cur.shape, F32)
    for _ in range(TOPK_GROUPS):
        mx = jnp.max(cur, axis=0, keepdims=True)
        ix = jnp.min(jnp.where(cur == mx, gi, ng), axis=0, keepdims=True)
        hit = gi == ix
        gmask = jnp.where(hit, 1.0, gmask)
        cur = jnp.where(hit, neg, cur)
    cand = [jnp.where(gmask > 0, biased[j], neg) for j in range(gs)]
    eidx = [gi * gs + j for j in range(gs)]
    selm = [jnp.zeros(cur.shape, F32) for _ in range(gs)]
    for _ in range(TOP_K):
        mx = jnp.max(fold(jnp.maximum, cand), axis=0, keepdims=True)
        ix = jnp.min(fold(jnp.minimum, [jnp.where(cand[j] == mx, eidx[j], N_EXPERTS) for j in range(gs)]),
                     axis=0, keepdims=True)
        for j in range(gs):
            hit = eidx[j] == ix
            selm[j] = jnp.where(hit, 1.0, selm[j])
            cand[j] = jnp.where(hit, neg, cand[j])
    sel = [selm[j] * s[j] for j in range(gs)]
    tot = jnp.sum(fold(jnp.add, sel), axis=0, keepdims=True)
    comb = [sel[j] / tot * ROUTED_SCALE for j in range(gs)]
    sel_ref[...] = jnp.swapaxes(jnp.stack(selm, axis=0), 0, 1)
    comb_e = jnp.swapaxes(jnp.stack(comb, axis=0), 0, 1).reshape(N_EXPERTS, -1)
    comb_t = jnp.concatenate([comb_e, jnp.zeros_like(comb_e)], axis=0).T
    comb_ref[...] = jnp.where(pl.program_id(0) < n_blocks, comb_t, 0.0)


def _router(logits_t, b_router):
    t = logits_t.shape[1]
    tl = 1024
    nb = t // tl
    shp = (N_EXPERT_GROUPS, GROUP_SIZE, t)
    blk = pl.BlockSpec((N_EXPERT_GROUPS, GROUP_SIZE, tl), lambda j: (0, 0, jnp.minimum(j, nb - 1)))
    comb, sel = pl.pallas_call(
        functools.partial(_router_kernel, n_blocks=nb),
        grid=(nb + 1,),
        in_specs=[blk, pl.BlockSpec((N_EXPERT_GROUPS, GROUP_SIZE, 1), lambda j: (0, 0, 0))],
        out_specs=[pl.BlockSpec((tl, LANES), lambda j: (j, 0)), blk],
        out_shape=[jax.ShapeDtypeStruct((t + tl, LANES), F32), jax.ShapeDtypeStruct(shp, F32)],
        compiler_params=_params("arbitrary"),
        name="router",
    )(logits_t.reshape(shp), b_router.reshape(N_EXPERT_GROUPS, GROUP_SIZE, 1))
    return comb, sel.reshape(N_EXPERTS, t)


def _plan_kernel(sel_ref, pos_ref, meta_ref, *, n_meta):
    t = sel_ref.shape[1]
    tm = float(MOE_TILE)
    sel = sel_ref[...]
    selb = sel.astype(BF16)
    blk = 256
    rr = lax.broadcasted_iota(jnp.int32, (blk, blk), 0)
    cc = lax.broadcasted_iota(jnp.int32, (blk, blk), 1)
    before = (rr < cc).astype(BF16)
    carry = jnp.zeros((N_EXPERTS, 1), F32)
    ranks = []
    for b in range(t // blk):
        sb = selb[:, b * blk:(b + 1) * blk]
        ranks.append(_dot(sb, before) + carry)
        carry = carry + jnp.sum(sel[:, b * blk:(b + 1) * blk], axis=1, keepdims=True)
    rank = jnp.concatenate(ranks, axis=1)
    cnt = carry
    ntile = jnp.floor((cnt + (tm - 0.5)) * (1.0 / tm))
    er = lax.broadcasted_iota(jnp.int32, (N_EXPERTS, N_EXPERTS), 0)
    ec = lax.broadcasted_iota(jnp.int32, (N_EXPERTS, N_EXPERTS), 1)
    below = (ec < er).astype(BF16)
    tstart = _dot(below, jnp.broadcast_to(ntile, (N_EXPERTS, LANES)).astype(BF16))[:, 0:1]
    pos = tstart * tm + rank
    erank = _dot(below, selb)
    rows = []
    for k in range(TOP_K):
        hit = (sel > 0.0) & (erank == float(k))
        rows.append(jnp.sum(jnp.where(hit, pos, 0.0), axis=0, keepdims=True))
    rows += [jnp.zeros((1, t), F32)] * (8 - TOP_K)
    pos_ref[...] = jnp.concatenate(rows, axis=0).astype(jnp.int32)

    tau = lax.broadcasted_iota(jnp.int32, (N_EXPERTS, n_meta), 1).astype(F32)
    eidx = lax.broadcasted_iota(jnp.int32, (N_EXPERTS, n_meta), 0).astype(F32)
    te = jnp.sum(((tstart + ntile) <= tau).astype(F32), axis=0, keepdims=True)
    te = jnp.minimum(te, float(N_EXPERTS - 1))
    onehot = eidx == te
    cnt_t = jnp.sum(jnp.where(onehot, cnt, 0.0), axis=0, keepdims=True)
    ts_t = jnp.sum(jnp.where(onehot, tstart, 0.0), axis=0, keepdims=True)
    tr = jnp.clip(cnt_t - (tau[0:1] - ts_t) * tm, 0.0, tm)
    tf = jnp.where((tau[0:1] == ts_t) & (tr > 0.0), 1.0, 0.0)
    meta_ref[...] = jnp.concatenate([te, tr, tf] + [jnp.zeros((1, n_meta), F32)] * 5, axis=0).astype(jnp.int32)


def _plan(sel, n_meta):
    t = sel.shape[1]
    sd = jax.ShapeDtypeStruct
    return pl.pallas_call(
        functools.partial(_plan_kernel, n_meta=n_meta),
        out_shape=[sd((8, t), jnp.int32), sd((8, n_meta), jnp.int32)],
        compiler_params=pltpu.CompilerParams(vmem_limit_bytes=VMEM_LIMIT),
        name="plan",
    )(sel)


def _slot_table(pos, n_slots, n_tokens):
    t = pos.shape[1]
    workers = SC_CORES * SC_SUBCORES
    per = -(-n_slots // (workers * SC_LANES)) * SC_LANES
    mesh = plsc.VectorSubcoreMesh(core_axis_name="core", subcore_axis_name="subcore")

    def body(pos_hbm, out_hbm, pos_v, tbl_v):
        lo = (lax.axis_index("core") * SC_SUBCORES + lax.axis_index("subcore")) * per
        pltpu.sync_copy(pos_hbm.at[pl.ds(0, TOP_K * t)], pos_v)
        dummy = jnp.full((SC_LANES,), n_tokens, jnp.int32)

        @plsc.parallel_loop(0, per, SC_LANES, unroll=8)
        def _(i):
            tbl_v[pl.ds(i, SC_LANES)] = dummy

        lane = lax.iota(jnp.int32, SC_LANES)
        for k in range(TOP_K):
            @plsc.parallel_loop(0, t, SC_LANES, unroll=8)
            def _(i):
                p = pos_v[pl.ds(k * t + i, SC_LANES)] - lo
                mine = (p >= 0) & (p < per)
                plsc.store_scatter(tbl_v, [jnp.where(mine, p, 0)], lane + i, mask=mine)
        pltpu.sync_copy(tbl_v, out_hbm.at[pl.ds(lo, per)])

    build = pl.kernel(body, out_type=jax.ShapeDtypeStruct((workers * per,), jnp.int32), mesh=mesh,
                      scratch_types=[pltpu.VMEM((TOP_K * t,), jnp.int32), pltpu.VMEM((per,), jnp.int32)],
                      compiler_params=pltpu.CompilerParams(needs_layout_passes=False), name="slot_table")
    return build(pos.reshape(-1))


def _experts_kernel(te_ref, tr_ref, xc_ref, xs_ref, slots_ref, comb_ref, wgu_ref, wd_ref, acc_out,
                    tbl_ref, xbuf, acc, stage0, stage1, cst0, cst1, act0, act1, ybuf0, ybuf1, sems):
    s = pl.program_id(0)
    n_tiles = te_ref.shape[0]
    tm = MOE_TILE
    nch = D_MODEL // LANES
    tile_at = lambda lag: jnp.clip(s - lag, 0, n_tiles - 1)
    t_g, t_1, t_2, t_3 = tile_at(0), tile_at(1), tile_at(2), tile_at(3)

    @pl.when(s == 0)
    def _():
        tc, ts = xc_ref.shape[0], xs_ref.shape[0]
        copies = (pltpu.make_async_copy(xc_ref, xbuf.at[pl.ds(0, tc)], sems.at[0]),
                  pltpu.make_async_copy(xs_ref, xbuf.at[pl.ds(tc, ts)], sems.at[1]))
        tbl_init = pltpu.make_async_copy(slots_ref, tbl_ref, sems.at[3])
        tbl_init.start()
        for cp in copies:
            cp.start()
        n_pad = xbuf.shape[0] - tc - ts
        xbuf[pl.ds(tc + ts, n_pad)] = jnp.zeros((n_pad,) + xbuf.shape[1:], BF16)
        for ref in (acc, stage0, stage1, cst0, cst1, act0, act1, ybuf0, ybuf1):
            ref[...] = jnp.zeros_like(ref)
        tbl_init.wait()
        for cp in copies:
            cp.wait()

    def gather(tile, stage, cst):
        base = tile * tm
        for j in range(tm):
            tok = tbl_ref[base + j]
            stage[pl.ds(j * nch, nch), :] = xbuf[tok].astype(F32)
            cst[pl.ds(j, 1), :] = comb_ref[pl.ds(tok, 1), :]

    def gate_up(tile, stage, cst, act):
        xb = _token_rows_from_slabs(
            lambda g: stage[g * 8 * nch:(g + 1) * 8 * nch, :].reshape(8, nch, LANES), tm).astype(BF16)
        gu = _dot(xb, wgu_ref[0].astype(BF16))
        lane = lax.broadcasted_iota(jnp.int32, (1, LANES), 1)
        w_col = jnp.sum(jnp.where(lane == te_ref[tile], cst[...], 0.0), axis=1, keepdims=True)
        act[...] = (_silu(gu[:, :EXPERT_DIM]) * gu[:, EXPERT_DIM:] * w_col).astype(BF16)

    def down(act, ybuf):
        y = _dot(act[...], wd_ref[0].astype(BF16))
        for cc in range(nch):
            ybuf[cc * Y_PITCH:cc * Y_PITCH + tm, :] = y[:, cc * LANES:(cc + 1) * LANES]

    def scatter(tile, ybuf):
        base = tile * tm
        sc_n = 16
        for i in range(tm // sc_n):
            toks = [tbl_ref[base + i * sc_n + u] for u in range(sc_n)]
            olds = [acc[toks[u]] for u in range(sc_n)]
            news = [olds[u] + ybuf[pl.ds(i * sc_n + u, nch, stride=Y_PITCH), :] for u in range(sc_n)]
            for u in range(sc_n):
                acc[toks[u]] = news[u]

    busy = (tr_ref[t_g] + tr_ref[t_1] + tr_ref[t_2] + tr_ref[t_3]) > 0
    bufs = ((stage0, cst0, act0, ybuf0), (stage1, cst1, act1, ybuf1))
    for par in range(2):
        stage_p, cst_p, act_p, ybuf_p = bufs[par]
        stage_q, cst_q, act_q, ybuf_q = bufs[1 - par]

        @pl.when(busy & (s % 2 == par))
        def _():
            gather(t_g, stage_p, cst_p)
            gate_up(t_1, stage_q, cst_q, act_q)
            down(act_p, ybuf_p)
            scatter(t_3, ybuf_q)

    @pl.when(s == pl.num_programs(0) - 1)
    def _():
        cp = pltpu.make_async_copy(acc, acc_out, sems.at[2])
        cp.start()
        cp.wait()


def _experts(te, tr, xc, xs, slots, comb, wgu, wd):
    n_tok = xc.shape[0] + xs.shape[0] + 8
    n_tiles = te.shape[0]
    tm = MOE_TILE
    nch = D_MODEL // LANES
    vm = pltpu.VMEM
    grid_spec = pltpu.PrefetchScalarGridSpec(
        num_scalar_prefetch=2,
        grid=(n_tiles + 3,),
        in_specs=[pl.BlockSpec(memory_space=pl.ANY), pl.BlockSpec(memory_space=pl.ANY),
                  pl.BlockSpec(memory_space=pl.ANY),
                  pl.BlockSpec((n_tok, LANES), lambda s, *_: (0, 0), pipeline_mode=pl.Buffered(1)),
                  pl.BlockSpec((1, D_MODEL, 2 * EXPERT_DIM), lambda s, te, *_: (te[jnp.clip(s - 1, 0, n_tiles - 1)], 0, 0)),
                  pl.BlockSpec((1, EXPERT_DIM, D_MODEL), lambda s, te, *_: (te[jnp.clip(s - 2, 0, n_tiles - 1)], 0, 0))],
        out_specs=pl.BlockSpec(memory_space=pl.ANY),
        scratch_shapes=[pltpu.SMEM(slots.shape, jnp.int32),
                        vm((n_tok, nch, LANES), BF16), vm((n_tok, nch, LANES), F32),
                        vm((tm * nch, LANES), F32), vm((tm * nch, LANES), F32),
                        vm((tm, LANES), F32), vm((tm, LANES), F32),
                        vm((tm, EXPERT_DIM), BF16), vm((tm, EXPERT_DIM), BF16),
                        vm((nch * Y_PITCH, LANES), F32), vm((nch * Y_PITCH, LANES), F32),
                        pltpu.SemaphoreType.DMA((4,))],
    )
    return pl.pallas_call(
        _experts_kernel,
        grid_spec=grid_spec,
        out_shape=jax.ShapeDtypeStruct((n_tok, nch, LANES), F32),
        compiler_params=pltpu.CompilerParams(dimension_semantics=("arbitrary",),
                                             vmem_limit_bytes=EXPERTS_VMEM_LIMIT),
        name="experts",
    )(te, tr, xc, xs, slots, comb, wgu, wd)


def _final_kernel(acc_ref, h_ref, x1_ref, mod_ref, wsg_ref, wsd_ref, gpo_ref, out_ref):
    routed = _token_rows_from_slabs(lambda g: acc_ref[g * 8:(g + 1) * 8], acc_ref.shape[0])
    gs = _dot(h_ref[...], wsg_ref[...])
    act = _silu(gs[:, :SHARED_DIM]) * gs[:, SHARED_DIM:]
    f = routed + _dot(act.astype(BF16), wsd_ref[...])
    out_ref[...] = x1_ref[...] + mod_ref[0][5:6] * _rmsnorm(f, gpo_ref[...])


def _final(acc, tile0, h2, x1, mod, mod_row, wsg, wsd, gpo):
    n = h2.shape[0]
    tm = FINAL_TILE
    const = lambda *shape: pl.BlockSpec(shape, lambda i: (0,) * len(shape))
    tok = lambda w: pl.BlockSpec((tm, w), lambda i: (i, 0))
    return pl.pallas_call(
        _final_kernel,
        grid=(n // tm,),
        in_specs=[pl.BlockSpec((tm, D_MODEL // LANES, LANES), lambda i: (i + tile0, 0, 0)),
                  tok(D_MODEL), tok(D_MODEL),
                  pl.BlockSpec((1, N_MOD, D_MODEL), lambda i: (mod_row(i), 0, 0)),
                  const(D_MODEL, 2 * SHARED_DIM), const(SHARED_DIM, D_MODEL), const(1, D_MODEL)],
        out_specs=tok(D_MODEL),
        out_shape=jax.ShapeDtypeStruct((n, D_MODEL), F32),
        compiler_params=_params("arbitrary"),
        name="final",
    )(acc, h2, x1, mod, wsg, wsd, gpo)


def _window_bounds(n, w):
    idx = np.arange(n)
    return np.clip(idx - w // 2, 0, n), np.clip(idx + w - w // 2, 0, n)


def _pool_operators(t, grid):
    mats, invs = [], []
    for w in POOL_WINDOWS:
        if grid:
            rlo, rhi = _window_bounds(t // GRID_W, w)
            clo, chi = _window_bounds(GRID_W, w)
            r = np.arange(t) // GRID_W
            c = np.arange(t) % GRID_W
            m = ((r[None, :] >= rlo[r][:, None]) & (r[None, :] < rhi[r][:, None])
                 & (c[None, :] >= clo[c][:, None]) & (c[None, :] < chi[c][:, None]))
            cnt = (rhi - rlo)[r] * (chi - clo)[c]
        else:
            lo, hi = _window_bounds(t, w)
            sidx = np.arange(t)
            m = (sidx[None, :] >= lo[:, None]) & (sidx[None, :] < hi[:, None])
            cnt = hi - lo
        mats.append(m.astype(np.float32))
        invs.append((1.0 / cnt.astype(np.float64)).astype(np.float32)[:, None])
    return jnp.asarray(np.stack(mats), BF16), jnp.asarray(np.stack(invs), F32)


def kernel(x_prompt, x_sample, state_C, state_n, state_m, c, c_ctx, w_ada, b_ada, g_pre_mix, w_in, b_gate,
           w_pool, pool_scale, w_out, g_post_mix, g_pre_ffn, w_router, b_router, w_expert_gu, w_expert_down,
           w_shared_gu, w_shared_down, g_post_ffn):
    b_ctx = x_prompt.shape[0]
    b_lat = x_sample.shape[0]
    nu = N_DIR * HEADS
    l = 0
    row = lambda a: a[l].reshape(1, -1).astype(F32)

    cvec = jnp.zeros((16, D_MODEL), F32).at[0].set(c_ctx.astype(F32)).at[1:1 + b_lat].set(c.astype(F32))
    mod = _mod_rows(cvec, w_ada[l], b_ada[l]).reshape(16, N_MOD, D_MODEL)

    w_in_l = w_in[l]
    p0 = POOL_WIDTH
    mw = MLSTM_WIDTH
    w_u, w_q, w_k, w_v, w_o = (w_in_l[:, lo:lo + 512] for lo in (0, p0, p0 + mw, p0 + 2 * mw, p0 + 3 * mw))
    wm = jnp.concatenate([w_u, w_k, w_o], axis=1).astype(BF16)
    wt = jnp.concatenate([w_q.T, w_v.T], axis=0).astype(BF16)
    wg_cols = w_in_l[:, p0 + 4 * mw:]
    wg = jnp.pad(jnp.concatenate(_split2(wg_cols), axis=1), ((0, 0), (0, LANES - 2 * GATE_COLS)))
    bg = b_gate[l].reshape(GATE_COLS).astype(F32)
    bgr = jnp.pad(bg, (0, LANES - GATE_COLS)).reshape(1, LANES)
    wpl = w_pool[l].astype(BF16)
    zg = jnp.zeros((POOL_GROUP_DIM, POOL_GROUP_DIM), BF16)
    wp = jnp.stack([jnp.block([[wpl[2 * p], zg], [zg, wpl[2 * p + 1]]]) for p in range(POOL_GROUPS // 2)])
    wo = w_out[l].astype(BF16)
    wr = jnp.pad(jnp.concatenate(_split3(w_router[l].astype(F32)), axis=1), ((0, 0), (0, 2 * LANES - 3 * N_EXPERTS)))
    wsg = w_shared_gu[l].astype(BF16)
    wsd = w_shared_down[l].astype(BF16)

    def mixer(x, mod_row, grid, s0, m0, emit_state):
        t = x.shape[1]
        u, k, o, qt, vt, gate, gate_t = _inproj(x.astype(F32), mod, mod_row, row(g_pre_mix), wm, wt, wg, bgr)
        outs = _mlstm(k, qt, vt, gate, gate_t, s0, m0, emit_state)
        hf, hb = outs[0], outs[1]
        pm, pinv = _pool_operators(t, grid)
        x1, h2, xp, lg = _outproj(x.astype(F32), u, hf, hb, o, mod, mod_row, pm, pinv, wp, row(pool_scale), wo,
                                  row(g_post_mix), row(g_pre_ffn), wr)
        return x1, h2, xp, lg, outs[2:]

    ctx_row = lambda i: 0
    lat_row = lambda i: i + 1
    x1c, h2c, xpc, lgc, (c_new, n_new, m_new) = mixer(x_prompt, ctx_row, False, None, None, True)
    s0 = jnp.concatenate(
        [jnp.swapaxes(state_C[:, l].reshape(b_lat, nu, HEAD_DIM, HEAD_DIM).astype(F32), -1, -2),
         jnp.broadcast_to(state_n[:, l].reshape(b_lat, nu, 1, HEAD_DIM).astype(F32),
                          (b_lat, nu, N_ROWS, HEAD_DIM))], axis=-2)
    m0 = jnp.broadcast_to(state_m[:, l].reshape(b_lat, nu, 1, 1).astype(F32), (b_lat, nu, 1, LANES))
    x1s, h2s, xps, lgs, _ = mixer(x_sample, lat_row, True, s0, m0, False)

    tc = b_ctx * x_prompt.shape[1]
    ts = b_lat * x_sample.shape[1]
    n_tok = tc + ts
    lg_all = jnp.concatenate([lgc, lgs], axis=1)
    comb_tok, sel = _router(lg_all, b_router[l].astype(F32))
    n_tiles = n_tok * TOP_K // MOE_TILE + N_EXPERTS
    n_meta = -(-n_tiles // LANES) * LANES
    pos, meta = _plan(sel, n_meta)

    slab = (D_MODEL // LANES, LANES)
    acc = _experts(meta[0, :n_tiles], meta[1, :n_tiles],
                   xpc.reshape((tc,) + slab), xps.reshape((ts,) + slab),
                   _slot_table(pos, n_tiles * MOE_TILE, n_tok), comb_tok,
                   w_expert_gu[l], w_expert_down[l])

    fin = functools.partial(_final, wsg=wsg, wsd=wsd, gpo=row(g_post_ffn))
    tiles_per_lat = x_sample.shape[1] // FINAL_TILE
    yc = fin(acc, 0, h2c.reshape(tc, D_MODEL), x1c.reshape(tc, D_MODEL), mod, ctx_row)
    ys = fin(acc, tc // FINAL_TILE, h2s.reshape(ts, D_MODEL), x1s.reshape(ts, D_MODEL), mod,
             lambda i: i // tiles_per_lat + 1)

    new_c = c_new.reshape(b_ctx, 1, N_DIR, HEADS, HEAD_DIM, HEAD_DIM)
    new_n = n_new.reshape(b_ctx, 1, N_DIR, HEADS, HEAD_DIM)
    new_m = m_new[..., 0].reshape(b_ctx, 1, N_DIR, HEADS)
    return (yc.reshape(x_prompt.shape), ys.reshape(x_sample.shape), new_c, new_n, new_m)
```

```python
import functools
import math

import jax
import jax.numpy as jnp
import numpy as np
from jax import lax
from jax.experimental import pallas as pl
from jax.experimental.pallas import tpu as pltpu
from jax.experimental.pallas import tpu_sc as plsc

F32 = jnp.float32
BF16 = jnp.bfloat16

D_MODEL = 1024
GRID_W = 64
POOL_WIDTH = 512
POOL_GROUPS = 4
POOL_GROUP_DIM = 128
POOL_WINDOWS = (2, 4, 8, 16)
HEADS = 4
HEAD_DIM = 128
MLSTM_WIDTH = HEADS * HEAD_DIM
N_DIR = 2
GATE_COLS = N_DIR * 2 * HEADS
N_EXPERTS = 64
TOP_K = 6
N_EXPERT_GROUPS = 8
GROUP_SIZE = N_EXPERTS // N_EXPERT_GROUPS
TOPK_GROUPS = 4
EXPERT_DIM = 256
SHARED_DIM = 256
ROUTED_SCALE = 2.5
N_MOD = 6
EPS = 1e-6
K_SCALE = HEAD_DIM ** -0.5

LANES = 128
CHUNK = 256
N_ROWS = 16
TOKEN_TILE = 256
FINAL_TILE = 512
MOE_TILE = 320
Y_PITCH = MOE_TILE + 8
VMEM_LIMIT = 56 * 1024 * 1024
SC_CORES = 2
SC_SUBCORES = 16
SC_LANES = 16
EXPERTS_VMEM_LIMIT = 58 * 1024 * 1024


def _split3(x):
    p1 = x.astype(BF16)
    r1 = x - p1.astype(F32)
    p2 = r1.astype(BF16)
    p3 = (r1 - p2.astype(F32)).astype(BF16)
    return p1, p2, p3


def _split2(x):
    p1 = x.astype(BF16)
    p2 = (x - p1.astype(F32)).astype(BF16)
    return p1, p2


def _dot(a, b):
    return jnp.dot(a, b, preferred_element_type=F32)


def _dot_nt(a, b):
    return lax.dot_general(a, b, (((1,), (1,)), ((), ())), preferred_element_type=F32)


def _rmsnorm(x, g):
    return x * lax.rsqrt(jnp.mean(x * x, axis=-1, keepdims=True) + EPS) * g


def _silu(x):
    return x * jax.nn.sigmoid(x)


def _token_rows_from_slabs(read_block, n_tok):
    nch = D_MODEL // LANES
    cols = [[] for _ in range(nch)]
    for g in range(n_tok // 8):
        blk = jnp.swapaxes(read_block(g), 0, 1)
        for cc in range(nch):
            cols[cc].append(blk[cc])
    return jnp.concatenate([jnp.concatenate(c, axis=0) for c in cols], axis=1)


def _params(*sem):
    return pltpu.CompilerParams(dimension_semantics=sem, vmem_limit_bytes=VMEM_LIMIT)


def _mod_kernel(c_ref, w_ref, b_ref, o_ref):
    a = _silu(c_ref[...])
    a_stack = jnp.concatenate(_split3(a), axis=0)
    w1, w2 = _split2(w_ref[...])
    r1 = _dot(a_stack, w1)
    r2 = _dot(a_stack[:32], w2)
    o_ref[...] = (r1[0:16] + r1[16:32] + r1[32:48] + r2[0:16] + r2[16:32]) + b_ref[...]


def _mod_rows(cvec, w_ada, b_ada):
    n = N_MOD * D_MODEL
    tn = 1536
    return pl.pallas_call(
        _mod_kernel,
        grid=(n // tn,),
        in_specs=[pl.BlockSpec((16, D_MODEL), lambda j: (0, 0)),
                  pl.BlockSpec((D_MODEL, tn), lambda j: (0, j)),
                  pl.BlockSpec((1, tn), lambda j: (0, j))],
        out_specs=pl.BlockSpec((16, tn), lambda j: (0, j)),
        out_shape=jax.ShapeDtypeStruct((16, n), F32),
        compiler_params=_params("arbitrary"),
        name="mod",
    )(cvec, w_ada, b_ada.reshape(1, n))


def _inproj_kernel(x_ref, mod_ref, g_ref, wm_ref, wt_ref, wg_ref, bgr_ref,
                   u_ref, k_ref, o_ref, qt_ref, vt_ref, gate_ref, gatet_ref):
    bs, tm, _ = x_ref.shape
    rows = bs * tm
    x = x_ref[...].reshape(rows, D_MODEL)
    mod = mod_ref[0]
    h = _rmsnorm(x, g_ref[...]) * (1.0 + mod[1:2]) + mod[0:1]
    h1, h2, h3 = _split3(h)
    z = _dot(h1, wm_ref[...])
    u_ref[...] = z[:, 0:512].astype(BF16).reshape(bs, tm, 512)
    k_ref[...] = (z[:, 512:1024] * K_SCALE).astype(BF16).reshape(bs, tm, 512)
    o_ref[...] = z[:, 1024:1536].astype(BF16).reshape(bs, tm, 512)
    zt = _dot_nt(wt_ref[...], h1).astype(BF16)
    r = _dot(jnp.concatenate([h1, h2, h3], axis=0), wg_ref[...])
    r12 = r[0:rows] + r[rows:2 * rows]
    gate = (r12 + r[2 * rows:]) + pltpu.roll(r12, LANES - GATE_COLS, axis=1) + bgr_ref[...]
    gate_ref[...] = gate.reshape(bs, tm, LANES)
    gate_t = gate.T
    for bb in range(bs):
        cols = slice(bb * tm, (bb + 1) * tm)
        qt_ref[bb] = zt[0:512, cols]
        vt_ref[bb] = zt[512:1024, cols]
        gatet_ref[bb] = gate_t[0:16, cols]


def _inproj(x, mod, mod_row, g, wm, wt, wg, bgr):
    b, t, _ = x.shape
    tm = min(t, 2 * TOKEN_TILE)
    bs = 2 * TOKEN_TILE // tm if mod_row(1) == mod_row(0) else 1
    const = lambda *shape: pl.BlockSpec(shape, lambda i, j: (0,) * len(shape))
    tok = lambda w: pl.BlockSpec((bs, tm, w), lambda i, j: (i, j, 0))
    tok_t = lambda r: pl.BlockSpec((bs, r, tm), lambda i, j: (i, 0, j))
    sd = jax.ShapeDtypeStruct
    return pl.pallas_call(
        _inproj_kernel,
        grid=(b // bs, t // tm),
        in_specs=[tok(D_MODEL),
                  pl.BlockSpec((1, N_MOD, D_MODEL), lambda i, j: (mod_row(i * bs), 0, 0)),
                  const(1, D_MODEL), const(D_MODEL, 1536), const(1024, D_MODEL),
                  const(D_MODEL, LANES), const(1, LANES)],
        out_specs=[tok(512), tok(512), tok(512), tok_t(512), tok_t(512), tok(LANES), tok_t(16)],
        out_shape=[sd((b, t, 512), BF16), sd((b, t, 512), BF16), sd((b, t, 512), BF16),
                   sd((b, 512, t), BF16), sd((b, 512, t), BF16), sd((b, t, LANES), F32),
                   sd((b, 16, t), F32)],
        compiler_params=_params("arbitrary", "arbitrary"),
        name="inproj",
    )(x, mod, g, wm, wt, wg, bgr)


def _log_sigmoid(x):
    return jnp.minimum(x, 0.0) - jnp.log1p(jnp.exp(-jnp.abs(x)))


def _scan_unit(st, k, qt, vt, u_col, u_row, b_row, btot, mask, s_prev, m_prev, use_state):
    dh = HEAD_DIM
    n = st.shape[0]
    ub = jnp.where(mask, jnp.broadcast_to(u_col, (n, n)), -jnp.inf)
    z = jnp.maximum(m_prev, jnp.max(ub, axis=0, keepdims=True))
    p = (jnp.exp(ub - z) * st).astype(BF16)
    ones = jnp.ones((N_ROWS, n), BF16)
    tot = _dot(jnp.concatenate([vt, ones], axis=0), p)
    if use_state:
        tot = tot + jnp.exp(m_prev - z) * _dot(s_prev.astype(BF16), qt)
    floor = jnp.exp(-(b_row + z))
    h_t = tot[:dh] / jnp.maximum(jnp.abs(tot[dh:dh + 1]), floor)
    g_row = btot + u_row
    m_new = jnp.maximum(btot + m_prev, jnp.max(g_row, axis=-1, keepdims=True))
    w_row = jnp.exp(g_row - m_new)
    vw = jnp.concatenate([(vt.astype(F32) * w_row).astype(BF16),
                          jnp.broadcast_to(w_row, (N_ROWS, n)).astype(BF16)], axis=0)
    s_new = jnp.exp(btot + m_prev - m_new) * s_prev + _dot(vw, k)
    return h_t.T, s_new, m_new


def _mlstm_kernel(*refs, nc, zero_init, emit_state):
    it = iter(refs)
    fwd_refs = tuple(next(it) for _ in range(5))
    bwd_refs = tuple(next(it) for _ in range(5)) if nc > 1 else fwd_refs
    if not zero_init:
        s0_ref, m0_ref = next(it), next(it)
    h_refs = (next(it), next(it))
    if emit_state:
        c_out, n_out, m_out = next(it), next(it), next(it)
    s_scr, m_scr = next(it), next(it)

    j = pl.program_id(1)
    n = CHUNK
    dh = HEAD_DIM

    @pl.when(j == 0)
    def _():
        if zero_init:
            s_scr[...] = jnp.zeros_like(s_scr)
            m_scr[...] = jnp.zeros_like(m_scr)
        else:
            s_scr[...] = s0_ref[0]
            m_scr[...] = m0_ref[0]

    rows = lax.broadcasted_iota(jnp.int32, (n, n), 0)
    cols = lax.broadcasted_iota(jnp.int32, (n, n), 1)
    le = rows <= cols
    ge = rows >= cols
    tri_le = le.astype(BF16)
    tri_ge = ge.astype(BF16)
    use_state = not (zero_init and nc == 1)

    def gate_terms(d):
        g_ref, gt_ref = (fwd_refs, bwd_refs)[d][3:5]
        gate = g_ref[0]
        gate_t = gt_ref[0]
        lf = _log_sigmoid(gate)
        lf_t = _log_sigmoid(gate_t)
        tri_c, tri_r = (tri_ge, tri_le) if d == 0 else (tri_le, tri_ge)
        bc = _dot(tri_c, jnp.concatenate(_split3(lf), axis=1))
        b_cols = bc[:, 0:128] + bc[:, 128:256] + bc[:, 256:384]
        br = _dot(jnp.concatenate(_split3(lf_t), axis=0), tri_r)
        b_rows = br[0:16] + br[16:32] + br[32:48]
        return gate, gate_t, b_cols, b_rows, jnp.sum(lf_t, axis=-1, keepdims=True)

    terms = [gate_terms(0), gate_terms(1)]
    hs = ([], [])
    for hd in range(HEADS):
        hsl = slice(hd * dh, (hd + 1) * dh)
        st = None
        for d in range(N_DIR):
            k_ref, qt_ref, vt_ref = (fwd_refs, bwd_refs)[d][0:3]
            gate, gate_t, b_cols, b_rows, tot_rows = terms[d]
            ci = d * 8 + hd
            cf = d * 8 + 4 + hd
            unit = d * HEADS + hd
            k = k_ref[0, :, hsl]
            qt = qt_ref[0, hsl, :]
            if st is None or nc > 1:
                st = _dot(k, qt)
            mask = le if d == 0 else ge
            h, s_new, m_new = _scan_unit(
                st, k, qt, vt_ref[0, hsl, :],
                gate[:, ci:ci + 1] - b_cols[:, cf:cf + 1],
                gate_t[ci:ci + 1, :] - b_rows[cf:cf + 1, :],
                b_rows[cf:cf + 1, :], tot_rows[cf:cf + 1, :],
                mask, s_scr[unit], m_scr[unit][:, 0:1], use_state)
            s_scr[unit] = s_new
            m_scr[unit] = jnp.broadcast_to(m_new, (1, LANES))
            hs[d].append(h)
    for d in range(N_DIR):
        h_refs[d][0] = jnp.concatenate(hs[d], axis=1).astype(BF16)

    if emit_state:
        @pl.when(j == nc - 1)
        def _():
            for unit in range(N_DIR * HEADS):
                s = s_scr[unit]
                c_out[0, unit] = s[:dh].T
                n_out[0, unit] = s[dh:dh + 1]
                m_out[0, unit] = m_scr[unit]


def _mlstm(k, qt, vt, gate, gate_t, s0, m0, emit_state):
    b, t, _ = k.shape
    nc = t // CHUNK
    zero_init = s0 is None
    nu = N_DIR * HEADS
    fwd = lambda w: pl.BlockSpec((1, CHUNK, w), lambda i, j: (i, j, 0))
    bwd = lambda w: pl.BlockSpec((1, CHUNK, w), lambda i, j: (i, nc - 1 - j, 0))
    fwd_t = lambda r: pl.BlockSpec((1, r, CHUNK), lambda i, j: (i, 0, j))
    bwd_t = lambda r: pl.BlockSpec((1, r, CHUNK), lambda i, j: (i, 0, nc - 1 - j))
    args = [k, qt, vt, gate, gate_t]
    in_specs = [fwd(512), fwd_t(512), fwd_t(512), fwd(LANES), fwd_t(16)]
    if nc > 1:
        args += [k, qt, vt, gate, gate_t]
        in_specs += [bwd(512), bwd_t(512), bwd_t(512), bwd(LANES), bwd_t(16)]
    if not zero_init:
        args += [s0, m0]
        in_specs += [pl.BlockSpec((1, nu, HEAD_DIM + N_ROWS, HEAD_DIM), lambda i, j: (i, 0, 0, 0)),
                     pl.BlockSpec((1, nu, 1, LANES), lambda i, j: (i, 0, 0, 0))]
    sd = jax.ShapeDtypeStruct
    out_shape = [sd((b, t, 512), BF16), sd((b, t, 512), BF16)]
    out_specs = [fwd(512), bwd(512)]
    if emit_state:
        out_shape += [sd((b, nu, HEAD_DIM, HEAD_DIM), F32), sd((b, nu, 1, HEAD_DIM), F32),
                      sd((b, nu, 1, LANES), F32)]
        out_specs += [pl.BlockSpec((1, nu, HEAD_DIM, HEAD_DIM), lambda i, j: (i, 0, 0, 0)),
                      pl.BlockSpec((1, nu, 1, HEAD_DIM), lambda i, j: (i, 0, 0, 0)),
                      pl.BlockSpec((1, nu, 1, LANES), lambda i, j: (i, 0, 0, 0))]
    return pl.pallas_call(
        functools.partial(_mlstm_kernel, nc=nc, zero_init=zero_init, emit_state=emit_state),
        grid=(b, nc),
        in_specs=in_specs,
        out_specs=out_specs,
        out_shape=out_shape,
        scratch_shapes=[pltpu.VMEM((nu, HEAD_DIM + N_ROWS, HEAD_DIM), F32),
                        pltpu.VMEM((nu, 1, LANES), F32)],
        compiler_params=_params("arbitrary", "arbitrary"),
        name="mlstm",
    )(*args)


def _outproj_kernel(x_ref, u_ref, hf_ref, hb_ref, o_ref, mod_ref, pm_ref, pinv_ref, wp_ref, ps_ref,
                    wo_ref, gpm_ref, gpf_ref, wr_ref, x1_ref, h2_ref, xp_ref, lg_ref):
    bs, tm, _ = x_ref.shape
    rows = bs * tm
    x = x_ref[...].reshape(rows, D_MODEL)
    mod = mod_ref[0]
    row0 = pl.multiple_of(pl.program_id(1) * tm, tm)
    diffs = []
    for g in range(POOL_GROUPS):
        sl = slice(g * POOL_GROUP_DIM, (g + 1) * POOL_GROUP_DIM)
        per_seq = []
        for bb in range(bs):
            box = _dot(pm_ref[g], u_ref[bb, :, sl])
            per_seq.append(box * pinv_ref[g] - u_ref[bb, pl.ds(row0, tm), sl].astype(F32))
        diffs.append(jnp.concatenate(per_seq, axis=0).astype(BF16))
    yps = [_dot(jnp.concatenate(diffs[2 * p:2 * p + 2], axis=1), wp_ref[p]) for p in range(POOL_GROUPS // 2)]
    y_pool = jnp.concatenate(yps, axis=1) * ps_ref[...]
    seq_rows = lambda ref: ref[...].reshape(rows, ref.shape[-1]).astype(F32)
    y_ml = jax.nn.sigmoid(seq_rows(o_ref)) * (seq_rows(hf_ref) + seq_rows(hb_ref))
    mix = _dot(jnp.concatenate([y_pool, y_ml], axis=1).astype(BF16), wo_ref[...])
    x1 = x + mod[2:3] * _rmsnorm(mix, gpm_ref[...])
    x1_ref[...] = x1.reshape(bs, tm, D_MODEL)
    h2 = _rmsnorm(x1, gpf_ref[...]) * (1.0 + mod[4:5]) + mod[3:4]
    p1, p2, p3 = _split3(h2)
    h2_ref[...] = p1.reshape(bs, tm, D_MODEL)
    nch = D_MODEL // LANES
    for g in range(rows // 8):
        cols = jnp.stack([h2[g * 8:(g + 1) * 8, cc * LANES:(cc + 1) * LANES] for cc in range(nch)], axis=0)
        bb, r0 = divmod(g * 8, tm)
        xp_ref[bb, r0:r0 + 8] = jnp.swapaxes(cols, 0, 1).astype(BF16)
    r = _dot(jnp.concatenate([p1, p2, p3], axis=0), wr_ref[...])
    r12 = r[0:rows] + r[rows:2 * rows]
    ne = N_EXPERTS
    lg = (r12 + r[2 * rows:])[:, 0:ne] + r12[:, ne:2 * ne] + r[0:rows, 2 * ne:3 * ne]
    lg_ref[...] = jnp.concatenate([lg, jnp.zeros_like(lg)], axis=1).T[0:ne]


def _outproj(x, u, hf, hb, o, mod, mod_row, pm, pinv, wp, ps, wo, gpm, gpf, wr):
    b, t, _ = x.shape
    tm = min(t, 2 * TOKEN_TILE)
    bs = 2 * TOKEN_TILE // tm if mod_row(1) == mod_row(0) else 1
    const = lambda *shape: pl.BlockSpec(shape, lambda i, j: (0,) * len(shape))
    tok = lambda w: pl.BlockSpec((bs, tm, w), lambda i, j: (i, j, 0))
    sd = jax.ShapeDtypeStruct
    return pl.pallas_call(
        _outproj_kernel,
        grid=(b // bs, t // tm),
        in_specs=[tok(D_MODEL),
                  pl.BlockSpec((bs, t, 512), lambda i, j: (i, 0, 0)),
                  tok(512), tok(512), tok(512),
                  pl.BlockSpec((1, N_MOD, D_MODEL), lambda i, j: (mod_row(i * bs), 0, 0)),
                  pl.BlockSpec((POOL_GROUPS, tm, t), lambda i, j: (0, j, 0)),
                  pl.BlockSpec((POOL_GROUPS, tm, 1), lambda i, j: (0, j, 0)),
                  const(POOL_GROUPS // 2, 2 * POOL_GROUP_DIM, 2 * POOL_GROUP_DIM), const(1, POOL_WIDTH),
                  const(D_MODEL, D_MODEL), const(1, D_MODEL), const(1, D_MODEL),
                  const(D_MODEL, 2 * LANES)],
        out_specs=[tok(D_MODEL), tok(D_MODEL),
                   pl.BlockSpec((bs, tm, D_MODEL // LANES, LANES), lambda i, j: (i, j, 0, 0)),
                   pl.BlockSpec((N_EXPERTS, bs * tm), lambda i, j: (0, i * (t // tm) + j))],
        out_shape=[sd((b, t, D_MODEL), F32), sd((b, t, D_MODEL), BF16),
                   sd((b, t, D_MODEL // LANES, LANES), BF16), sd((N_EXPERTS, b * t), F32)],
        compiler_params=_params("arbitrary", "arbitrary"),
        name="outproj",
    )(x, u, hf, hb, o, mod, pm, pinv, wp, ps, wo, gpm, gpf, wr)


def _router_kernel(lg_ref, br_ref, comb_ref, sel_ref):
    ng, gs = N_EXPERT_GROUPS, GROUP_SIZE
    neg = -jnp.inf
    lg = jnp.swapaxes(lg_ref[...], 0, 1)
    br = br_ref[...]
    s = [jax.nn.sigmoid(lg[j]) for j in range(gs)]
    biased = [s[j] + br[:, j, :] for j in range(gs)]
    fold = lambda op, xs: functools.reduce(op, xs)
    m1 = fold(jnp.maximum, biased)
    i1 = fold(jnp.minimum, [jnp.where(biased[j] == m1, j, gs) for j in range(gs)])
    m2 = fold(jnp.maximum, [jnp.where(i1 == j, neg, biased[j]) for j in range(gs)])
    cur = m1 + m2
    gi = lax.broadcasted_iota(jnp.int32, cur.shape, 0)
    gmask = jnp.zeros(cur.shape, F32)
    for _ in range(TOPK_GROUPS):
        mx = jnp.max(cur, axis=0, keepdims=True)
        ix = jnp.min(jnp.where(cur == mx, gi, ng), axis=0, keepdims=True)
        hit = gi == ix
        gmask = jnp.where(hit, 1.0, gmask)
        cur = jnp.where(hit, neg, cur)
    cand = [jnp.where(gmask > 0, biased[j], neg) for j in range(gs)]
    eidx = [gi * gs + j for j in range(gs)]
    selm = [jnp.zeros(cur.shape, F32) for _ in range(gs)]
    for _ in range(TOP_K):
        mx = jnp.max(fold(jnp.maximum, cand), axis=0, keepdims=True)
        ix = jnp.min(fold(jnp.minimum, [jnp.where(cand[j] == mx, eidx[j], N_EXPERTS) for j in range(gs)]),
                     axis=0, keepdims=True)
        for j in range(gs):
            hit = eidx[j] == ix
            selm[j] = jnp.where(hit, 1.0, selm[j])
            cand[j] = jnp.where(hit, neg, cand[j])
    sel = [selm[j] * s[j] for j in range(gs)]
    tot = jnp.sum(fold(jnp.add, sel), axis=0, keepdims=True)
    comb = [sel[j] / tot * ROUTED_SCALE for j in range(gs)]
    sel_ref[...] = jnp.swapaxes(jnp.stack(selm, axis=0), 0, 1)
    comb_ref[...] = jnp.swapaxes(jnp.stack(comb, axis=0), 0, 1)


def _router(logits_t, b_router):
    t = logits_t.shape[1]
    tl = 1024
    shp = (N_EXPERT_GROUPS, GROUP_SIZE, t)
    blk = pl.BlockSpec((N_EXPERT_GROUPS, GROUP_SIZE, tl), lambda j: (0, 0, j))
    comb, sel = pl.pallas_call(
        _router_kernel,
        grid=(t // tl,),
        in_specs=[blk, pl.BlockSpec((N_EXPERT_GROUPS, GROUP_SIZE, 1), lambda j: (0, 0, 0))],
        out_specs=[blk, blk],
        out_shape=[jax.ShapeDtypeStruct(shp, F32), jax.ShapeDtypeStruct(shp, F32)],
        compiler_params=_params("arbitrary"),
        name="router",
    )(logits_t.reshape(shp), b_router.reshape(N_EXPERT_GROUPS, GROUP_SIZE, 1))
    return comb.reshape(N_EXPERTS, t), sel.reshape(N_EXPERTS, t)


def _plan_kernel(sel_ref, comb_ref, pos_ref, wk_ref, meta_ref, *, n_meta):
    t = sel_ref.shape[1]
    tm = float(MOE_TILE)
    sel = sel_ref[...]
    selb = sel.astype(BF16)
    blk = 256
    rr = lax.broadcasted_iota(jnp.int32, (blk, blk), 0)
    cc = lax.broadcasted_iota(jnp.int32, (blk, blk), 1)
    before = (rr < cc).astype(BF16)
    carry = jnp.zeros((N_EXPERTS, 1), F32)
    ranks = []
    for b in range(t // blk):
        sb = selb[:, b * blk:(b + 1) * blk]
        ranks.append(_dot(sb, before) + carry)
        carry = carry + jnp.sum(sel[:, b * blk:(b + 1) * blk], axis=1, keepdims=True)
    rank = jnp.concatenate(ranks, axis=1)
    cnt = carry
    ntile = jnp.floor((cnt + (tm - 0.5)) * (1.0 / tm))
    er = lax.broadcasted_iota(jnp.int32, (N_EXPERTS, N_EXPERTS), 0)
    ec = lax.broadcasted_iota(jnp.int32, (N_EXPERTS, N_EXPERTS), 1)
    below = (ec < er).astype(BF16)
    tstart = _dot(below, jnp.broadcast_to(ntile, (N_EXPERTS, LANES)).astype(BF16))[:, 0:1]
    pos = tstart * tm + rank
    erank = _dot(below, selb)
    comb = comb_ref[...]
    rows, wrows = [], []
    for k in range(TOP_K):
        hit = (sel > 0.0) & (erank == float(k))
        rows.append(jnp.sum(jnp.where(hit, pos, 0.0), axis=0, keepdims=True))
        wrows.append(jnp.sum(jnp.where(hit, comb, 0.0), axis=0, keepdims=True))
    pad = [jnp.zeros((1, t), F32)] * (8 - TOP_K)
    pos_ref[...] = jnp.concatenate(rows + pad, axis=0).astype(jnp.int32)
    wk_ref[...] = jnp.concatenate(wrows + pad, axis=0)

    tau = lax.broadcasted_iota(jnp.int32, (N_EXPERTS, n_meta), 1).astype(F32)
    eidx = lax.broadcasted_iota(jnp.int32, (N_EXPERTS, n_meta), 0).astype(F32)
    te = jnp.sum(((tstart + ntile) <= tau).astype(F32), axis=0, keepdims=True)
    te = jnp.minimum(te, float(N_EXPERTS - 1))
    onehot = eidx == te
    cnt_t = jnp.sum(jnp.where(onehot, cnt, 0.0), axis=0, keepdims=True)
    ts_t = jnp.sum(jnp.where(onehot, tstart, 0.0), axis=0, keepdims=True)
    tr = jnp.clip(cnt_t - (tau[0:1] - ts_t) * tm, 0.0, tm)
    tf = jnp.where((tau[0:1] == ts_t) & (tr > 0.0), 1.0, 0.0)
    meta_ref[...] = jnp.concatenate([te, tr, tf] + [jnp.zeros((1, n_meta), F32)] * 5, axis=0).astype(jnp.int32)


def _plan(sel, comb, n_meta):
    t = sel.shape[1]
    sd = jax.ShapeDtypeStruct
    return pl.pallas_call(
        functools.partial(_plan_kernel, n_meta=n_meta),
        out_shape=[sd((8, t), jnp.int32), sd((8, t), F32), sd((8, n_meta), jnp.int32)],
        compiler_params=pltpu.CompilerParams(vmem_limit_bytes=VMEM_LIMIT),
        name="plan",
    )(sel, comb)


def _slot_tables(pos, wk, n_slots, n_tokens):
    t = pos.shape[1]
    workers = SC_CORES * SC_SUBCORES
    unroll = 8
    unit = math.lcm(SC_LANES * unroll, MOE_TILE // math.gcd(MOE_TILE, workers))
    per = -(-n_slots // (workers * unit)) * unit
    mesh = plsc.VectorSubcoreMesh(core_axis_name="core", subcore_axis_name="subcore")

    def body(pos_hbm, wk_hbm, tok_hbm, wgt_hbm, pos_v, wk_v, tok_v, wgt_v):
        lo = (lax.axis_index("core") * SC_SUBCORES + lax.axis_index("subcore")) * per
        pltpu.sync_copy(pos_hbm.at[pl.ds(0, TOP_K * t)], pos_v)
        pltpu.sync_copy(wk_hbm.at[pl.ds(0, TOP_K * t)], wk_v)
        dummy = jnp.full((SC_LANES,), n_tokens, jnp.int32)
        zero = jnp.zeros((SC_LANES,), F32)

        @plsc.parallel_loop(0, per, SC_LANES, unroll=unroll)
        def _(i):
            tok_v[pl.ds(i, SC_LANES)] = dummy
            wgt_v[pl.ds(i, SC_LANES)] = zero

        lane = lax.iota(jnp.int32, SC_LANES)
        for k in range(TOP_K):
            @plsc.parallel_loop(0, t, SC_LANES, unroll=unroll)
            def _(i):
                p = pos_v[pl.ds(k * t + i, SC_LANES)] - lo
                mine = (p >= 0) & (p < per)
                idx = jnp.where(mine, p, 0)
                plsc.store_scatter(tok_v, [idx], lane + i, mask=mine)
                plsc.store_scatter(wgt_v, [idx], wk_v[pl.ds(k * t + i, SC_LANES)], mask=mine)
        pltpu.sync_copy(tok_v, tok_hbm.at[pl.ds(lo, per)])
        pltpu.sync_copy(wgt_v, wgt_hbm.at[pl.ds(lo, per)])

    sd = jax.ShapeDtypeStruct
    build = pl.kernel(body, out_type=(sd((workers * per,), jnp.int32), sd((workers * per,), F32)), mesh=mesh,
                      scratch_types=[pltpu.VMEM((TOP_K * t,), jnp.int32), pltpu.VMEM((TOP_K * t,), F32),
                                     pltpu.VMEM((per,), jnp.int32), pltpu.VMEM((per,), F32)],
                      compiler_params=pltpu.CompilerParams(needs_layout_passes=False), name="slot_tables")
    return build(pos.reshape(-1), wk.reshape(-1))


def _experts_kernel(te_ref, tr_ref, xc_ref, xs_ref, slots_ref, wslot_ref, wgu_ref, wd_ref, acc_out,
                    tbl_ref, xbuf, acc, stage0, stage1, act0, act1, ybuf0, ybuf1, sems):
    s = pl.program_id(0)
    n_tiles = te_ref.shape[0]
    tm = MOE_TILE
    nch = D_MODEL // LANES
    tile_at = lambda lag: jnp.clip(s - lag, 0, n_tiles - 1)
    t_g, t_1, t_2, t_3 = tile_at(0), tile_at(1), tile_at(2), tile_at(3)

    @pl.when(s == 0)
    def _():
        tc, ts = xc_ref.shape[0], xs_ref.shape[0]
        copies = (pltpu.make_async_copy(xc_ref, xbuf.at[pl.ds(0, tc)], sems.at[0]),
                  pltpu.make_async_copy(xs_ref, xbuf.at[pl.ds(tc, ts)], sems.at[1]))
        tbl_init = pltpu.make_async_copy(slots_ref, tbl_ref, sems.at[3])
        tbl_init.start()
        for cp in copies:
            cp.start()
        n_pad = xbuf.shape[0] - tc - ts
        xbuf[pl.ds(tc + ts, n_pad)] = jnp.zeros((n_pad,) + xbuf.shape[1:], BF16)
        for ref in (acc, stage0, stage1, act0, act1, ybuf0, ybuf1):
            ref[...] = jnp.zeros_like(ref)
        tbl_init.wait()
        for cp in copies:
            cp.wait()

    def gather(tile, stage):
        base = tile * tm
        for j in range(tm):
            stage[pl.ds(j * nch, nch), :] = xbuf[tbl_ref[base + j]].astype(F32)

    def gate_up(stage, act):
        xb = _token_rows_from_slabs(
            lambda g: stage[g * 8 * nch:(g + 1) * 8 * nch, :].reshape(8, nch, LANES), tm).astype(BF16)
        gu = _dot(xb, wgu_ref[0].astype(BF16))
        w_row = jnp.broadcast_to(wslot_ref[0], (8, tm))
        w_col = jnp.concatenate([w_row[:, c:c + LANES].T for c in range(0, tm, LANES)], axis=0)[:, 0:1]
        act[...] = (_silu(gu[:, :EXPERT_DIM]) * gu[:, EXPERT_DIM:] * w_col).astype(BF16)

    def down(act, ybuf):
        y = _dot(act[...], wd_ref[0].astype(BF16))
        for cc in range(nch):
            ybuf[cc * Y_PITCH:cc * Y_PITCH + tm, :] = y[:, cc * LANES:(cc + 1) * LANES]

    def scatter(tile, ybuf):
        base = tile * tm
        sc_n = 16
        for i in range(tm // sc_n):
            toks = [tbl_ref[base + i * sc_n + u] for u in range(sc_n)]
            olds = [acc[toks[u]] for u in range(sc_n)]
            news = [olds[u] + ybuf[pl.ds(i * sc_n + u, nch, stride=Y_PITCH), :] for u in range(sc_n)]
            for u in range(sc_n):
                acc[toks[u]] = news[u]

    busy = (tr_ref[t_g] + tr_ref[t_1] + tr_ref[t_2] + tr_ref[t_3]) > 0
    bufs = ((stage0, act0, ybuf0), (stage1, act1, ybuf1))
    for par in range(2):
        stage_p, act_p, ybuf_p = bufs[par]
        stage_q, act_q, ybuf_q = bufs[1 - par]

        @pl.when(busy & (s % 2 == par))
        def _():
            gather(t_g, stage_p)
            gate_up(stage_q, act_q)
            down(act_p, ybuf_p)
            scatter(t_3, ybuf_q)

    @pl.when(s == pl.num_programs(0) - 1)
    def _():
        cp = pltpu.make_async_copy(acc, acc_out, sems.at[2])
        cp.start()
        cp.wait()


def _experts(te, tr, xc, xs, slots, wslot, wgu, wd):
    n_tok = xc.shape[0] + xs.shape[0] + 8
    n_tiles = te.shape[0]
    tm = MOE_TILE
    nch = D_MODEL // LANES
    vm = pltpu.VMEM
    grid_spec = pltpu.PrefetchScalarGridSpec(
        num_scalar_prefetch=2,
        grid=(n_tiles + 3,),
        in_specs=[pl.BlockSpec(memory_space=pl.ANY), pl.BlockSpec(memory_space=pl.ANY),
                  pl.BlockSpec(memory_space=pl.ANY),
                  pl.BlockSpec((1, 1, tm), lambda s, *_: (jnp.clip(s - 1, 0, n_tiles - 1), 0, 0)),
                  pl.BlockSpec((1, D_MODEL, 2 * EXPERT_DIM), lambda s, te, *_: (te[jnp.clip(s - 1, 0, n_tiles - 1)], 0, 0)),
                  pl.BlockSpec((1, EXPERT_DIM, D_MODEL), lambda s, te, *_: (te[jnp.clip(s - 2, 0, n_tiles - 1)], 0, 0))],
        out_specs=pl.BlockSpec(memory_space=pl.ANY),
        scratch_shapes=[pltpu.SMEM(slots.shape, jnp.int32),
                        vm((n_tok, nch, LANES), BF16), vm((n_tok, nch, LANES), F32),
                        vm((tm * nch, LANES), F32), vm((tm * nch, LANES), F32),
                        vm((tm, EXPERT_DIM), BF16), vm((tm, EXPERT_DIM), BF16),
                        vm((nch * Y_PITCH, LANES), F32), vm((nch * Y_PITCH, LANES), F32),
                        pltpu.SemaphoreType.DMA((4,))],
    )
    return pl.pallas_call(
        _experts_kernel,
        grid_spec=grid_spec,
        out_shape=jax.ShapeDtypeStruct((n_tok, nch, LANES), F32),
        compiler_params=pltpu.CompilerParams(dimension_semantics=("arbitrary",),
                                             vmem_limit_bytes=EXPERTS_VMEM_LIMIT),
        name="experts",
    )(te, tr, xc, xs, slots, wslot, wgu, wd)


def _final_kernel(acc_ref, h_ref, x1_ref, mod_ref, wsg_ref, wsd_ref, gpo_ref, out_ref):
    routed = _token_rows_from_slabs(lambda g: acc_ref[g * 8:(g + 1) * 8], acc_ref.shape[0])
    gs = _dot(h_ref[...], wsg_ref[...])
    act = _silu(gs[:, :SHARED_DIM]) * gs[:, SHARED_DIM:]
    f = routed + _dot(act.astype(BF16), wsd_ref[...])
    out_ref[...] = x1_ref[...] + mod_ref[0][5:6] * _rmsnorm(f, gpo_ref[...])


def _final(acc, tile0, h2, x1, mod, mod_row, wsg, wsd, gpo):
    n = h2.shape[0]
    tm = FINAL_TILE
    const = lambda *shape: pl.BlockSpec(shape, lambda i: (0,) * len(shape))
    tok = lambda w: pl.BlockSpec((tm, w), lambda i: (i, 0))
    return pl.pallas_call(
        _final_kernel,
        grid=(n // tm,),
        in_specs=[pl.BlockSpec((tm, D_MODEL // LANES, LANES), lambda i: (i + tile0, 0, 0)),
                  tok(D_MODEL), tok(D_MODEL),
                  pl.BlockSpec((1, N_MOD, D_MODEL), lambda i: (mod_row(i), 0, 0)),
                  const(D_MODEL, 2 * SHARED_DIM), const(SHARED_DIM, D_MODEL), const(1, D_MODEL)],
        out_specs=tok(D_MODEL),
        out_shape=jax.ShapeDtypeStruct((n, D_MODEL), F32),
        compiler_params=_params("arbitrary"),
        name="final",
    )(acc, h2, x1, mod, wsg, wsd, gpo)


def _window_bounds(n, w):
    idx = np.arange(n)
    return np.clip(idx - w // 2, 0, n), np.clip(idx + w - w // 2, 0, n)


def _pool_operators(t, grid):
    mats, invs = [], []
    for w in POOL_WINDOWS:
        if grid:
            rlo, rhi = _window_bounds(t // GRID_W, w)
            clo, chi = _window_bounds(GRID_W, w)
            r = np.arange(t) // GRID_W
            c = np.arange(t) % GRID_W
            m = ((r[None, :] >= rlo[r][:, None]) & (r[None, :] < rhi[r][:, None])
                 & (c[None, :] >= clo[c][:, None]) & (c[None, :] < chi[c][:, None]))
            cnt = (rhi - rlo)[r] * (chi - clo)[c]
        else:
            lo, hi = _window_bounds(t, w)
            sidx = np.arange(t)
            m = (sidx[None, :] >= lo[:, None]) & (sidx[None, :] < hi[:, None])
            cnt = hi - lo
        mats.append(m.astype(np.float32))
        invs.append((1.0 / cnt.astype(np.float64)).astype(np.float32)[:, None])
    return jnp.asarray(np.stack(mats), BF16), jnp.asarray(np.stack(invs), F32)


def kernel(x_prompt, x_sample, state_C, state_n, state_m, c, c_ctx, w_ada, b_ada, g_pre_mix, w_in, b_gate,
           w_pool, pool_scale, w_out, g_post_mix, g_pre_ffn, w_router, b_router, w_expert_gu, w_expert_down,
           w_shared_gu, w_shared_down, g_post_ffn):
    b_ctx = x_prompt.shape[0]
    b_lat = x_sample.shape[0]
    nu = N_DIR * HEADS
    l = 0
    row = lambda a: a[l].reshape(1, -1).astype(F32)

    cvec = jnp.zeros((16, D_MODEL), F32).at[0].set(c_ctx.astype(F32)).at[1:1 + b_lat].set(c.astype(F32))
    mod = _mod_rows(cvec, w_ada[l], b_ada[l]).reshape(16, N_MOD, D_MODEL)

    w_in_l = w_in[l]
    p0 = POOL_WIDTH
    mw = MLSTM_WIDTH
    w_u, w_q, w_k, w_v, w_o = (w_in_l[:, lo:lo + 512] for lo in (0, p0, p0 + mw, p0 + 2 * mw, p0 + 3 * mw))
    wm = jnp.concatenate([w_u, w_k, w_o], axis=1).astype(BF16)
    wt = jnp.concatenate([w_q.T, w_v.T], axis=0).astype(BF16)
    wg_cols = w_in_l[:, p0 + 4 * mw:]
    wg = jnp.pad(jnp.concatenate(_split2(wg_cols), axis=1), ((0, 0), (0, LANES - 2 * GATE_COLS)))
    bg = b_gate[l].reshape(GATE_COLS).astype(F32)
    bgr = jnp.pad(bg, (0, LANES - GATE_COLS)).reshape(1, LANES)
    wpl = w_pool[l].astype(BF16)
    zg = jnp.zeros((POOL_GROUP_DIM, POOL_GROUP_DIM), BF16)
    wp = jnp.stack([jnp.block([[wpl[2 * p], zg], [zg, wpl[2 * p + 1]]]) for p in range(POOL_GROUPS // 2)])
    wo = w_out[l].astype(BF16)
    wr = jnp.pad(jnp.concatenate(_split3(w_router[l].astype(F32)), axis=1), ((0, 0), (0, 2 * LANES - 3 * N_EXPERTS)))
    wsg = w_shared_gu[l].astype(BF16)
    wsd = w_shared_down[l].astype(BF16)

    def mixer(x, mod_row, grid, s0, m0, emit_state):
        t = x.shape[1]
        u, k, o, qt, vt, gate, gate_t = _inproj(x.astype(F32), mod, mod_row, row(g_pre_mix), wm, wt, wg, bgr)
        outs = _mlstm(k, qt, vt, gate, gate_t, s0, m0, emit_state)
        hf, hb = outs[0], outs[1]
        pm, pinv = _pool_operators(t, grid)
        x1, h2, xp, lg = _outproj(x.astype(F32), u, hf, hb, o, mod, mod_row, pm, pinv, wp, row(pool_scale), wo,
                                  row(g_post_mix), row(g_pre_ffn), wr)
        return x1, h2, xp, lg, outs[2:]

    ctx_row = lambda i: 0
    lat_row = lambda i: i + 1
    x1c, h2c, xpc, lgc, (c_new, n_new, m_new) = mixer(x_prompt, ctx_row, False, None, None, True)
    s0 = jnp.concatenate(
        [jnp.swapaxes(state_C[:, l].reshape(b_lat, nu, HEAD_DIM, HEAD_DIM).astype(F32), -1, -2),
         jnp.broadcast_to(state_n[:, l].reshape(b_lat, nu, 1, HEAD_DIM).astype(F32),
                          (b_lat, nu, N_ROWS, HEAD_DIM))], axis=-2)
    m0 = jnp.broadcast_to(state_m[:, l].reshape(b_lat, nu, 1, 1).astype(F32), (b_lat, nu, 1, LANES))
    x1s, h2s, xps, lgs, _ = mixer(x_sample, lat_row, True, s0, m0, False)

    tc = b_ctx * x_prompt.shape[1]
    ts = b_lat * x_sample.shape[1]
    n_tok = tc + ts
    lg_all = jnp.concatenate([lgc, lgs], axis=1)
    comb, sel = _router(lg_all, b_router[l].astype(F32))
    n_tiles = n_tok * TOP_K // MOE_TILE + N_EXPERTS
    n_meta = -(-n_tiles // LANES) * LANES
    pos, wk, meta = _plan(sel, comb, n_meta)
    slots, wslot = _slot_tables(pos, wk, n_tiles * MOE_TILE, n_tok)

    slab = (D_MODEL // LANES, LANES)
    acc = _experts(meta[0, :n_tiles], meta[1, :n_tiles],
                   xpc.reshape((tc,) + slab), xps.reshape((ts,) + slab),
                   slots, wslot.reshape(-1, 1, MOE_TILE),
                   w_expert_gu[l], w_expert_down[l])

    fin = functools.partial(_final, wsg=wsg, wsd=wsd, gpo=row(g_post_ffn))
    tiles_per_lat = x_sample.shape[1] // FINAL_TILE
    yc = fin(acc, 0, h2c.reshape(tc, D_MODEL), x1c.reshape(tc, D_MODEL), mod, ctx_row)
    ys = fin(acc, tc // FINAL_TILE, h2s.reshape(ts, D_MODEL), x1s.reshape(ts, D_MODEL), mod,
             lambda i: i // tiles_per_lat + 1)

    new_c = c_new.reshape(b_ctx, 1, N_DIR, HEADS, HEAD_DIM, HEAD_DIM)
    new_n = n_new.reshape(b_ctx, 1, N_DIR, HEADS, HEAD_DIM)
    new_m = m_new[..., 0].reshape(b_ctx, 1, N_DIR, HEADS)
    return (yc.reshape(x_prompt.shape), ys.reshape(x_sample.shape), new_c, new_n, new_m)
```

```python
import functools
import math

import jax
import jax.numpy as jnp
import numpy as np
from jax import lax
from jax.experimental import pallas as pl
from jax.experimental.pallas import tpu as pltpu
from jax.experimental.pallas import tpu_sc as plsc

F32 = jnp.float32
BF16 = jnp.bfloat16

D_MODEL = 1024
GRID_W = 64
POOL_WIDTH = 512
POOL_GROUPS = 4
POOL_GROUP_DIM = 128
POOL_WINDOWS = (2, 4, 8, 16)
HEADS = 4
HEAD_DIM = 128
MLSTM_WIDTH = HEADS * HEAD_DIM
N_DIR = 2
GATE_COLS = N_DIR * 2 * HEADS
N_EXPERTS = 64
TOP_K = 6
N_EXPERT_GROUPS = 8
GROUP_SIZE = N_EXPERTS // N_EXPERT_GROUPS
TOPK_GROUPS = 4
EXPERT_DIM = 256
SHARED_DIM = 256
ROUTED_SCALE = 2.5
N_MOD = 6
EPS = 1e-6
K_SCALE = HEAD_DIM ** -0.5

LANES = 128
CHUNK = 256
N_ROWS = 16
TOKEN_TILE = 256
FINAL_TILE = 512
MOE_TILE = 320
Y_PITCH = MOE_TILE + 8
VMEM_LIMIT = 56 * 1024 * 1024
SC_CORES = 2
SC_SUBCORES = 16
SC_LANES = 16
EXPERTS_VMEM_LIMIT = 58 * 1024 * 1024


def _split3(x):
    p1 = x.astype(BF16)
    r1 = x - p1.astype(F32)
    p2 = r1.astype(BF16)
    p3 = (r1 - p2.astype(F32)).astype(BF16)
    return p1, p2, p3


def _split2(x):
    p1 = x.astype(BF16)
    p2 = (x - p1.astype(F32)).astype(BF16)
    return p1, p2


def _dot(a, b):
    return jnp.dot(a, b, preferred_element_type=F32)


def _dot_nt(a, b):
    return lax.dot_general(a, b, (((1,), (1,)), ((), ())), preferred_element_type=F32)


def _rmsnorm(x, g):
    return x * lax.rsqrt(jnp.mean(x * x, axis=-1, keepdims=True) + EPS) * g


def _silu(x):
    return x * jax.nn.sigmoid(x)


def _token_rows_from_slabs(read_block, n_tok):
    nch = D_MODEL // LANES
    cols = [[] for _ in range(nch)]
    for g in range(n_tok // 8):
        blk = jnp.swapaxes(read_block(g), 0, 1)
        for cc in range(nch):
            cols[cc].append(blk[cc])
    return jnp.concatenate([jnp.concatenate(c, axis=0) for c in cols], axis=1)


def _params(*sem):
    return pltpu.CompilerParams(dimension_semantics=sem, vmem_limit_bytes=VMEM_LIMIT)


def _mod_kernel(c_ref, w_ref, b_ref, o_ref):
    a = _silu(c_ref[...])
    a_stack = jnp.concatenate(_split3(a), axis=0)
    w1, w2 = _split2(w_ref[...])
    r1 = _dot(a_stack, w1)
    r2 = _dot(a_stack[:32], w2)
    o_ref[...] = (r1[0:16] + r1[16:32] + r1[32:48] + r2[0:16] + r2[16:32]) + b_ref[...]


def _mod_rows(cvec, w_ada, b_ada):
    n = N_MOD * D_MODEL
    tn = 1536
    return pl.pallas_call(
        _mod_kernel,
        grid=(n // tn,),
        in_specs=[pl.BlockSpec((16, D_MODEL), lambda j: (0, 0)),
                  pl.BlockSpec((D_MODEL, tn), lambda j: (0, j)),
                  pl.BlockSpec((1, tn), lambda j: (0, j))],
        out_specs=pl.BlockSpec((16, tn), lambda j: (0, j)),
        out_shape=jax.ShapeDtypeStruct((16, n), F32),
        compiler_params=_params("arbitrary"),
        name="mod",
    )(cvec, w_ada, b_ada.reshape(1, n))


def _inproj_kernel(x_ref, mod_ref, g_ref, wm_ref, wt_ref, wg_ref, bgr_ref,
                   u_ref, k_ref, o_ref, qt_ref, vt_ref, gate_ref, gatet_ref):
    bs, tm, _ = x_ref.shape
    rows = bs * tm
    x = x_ref[...].reshape(rows, D_MODEL)
    mod = mod_ref[0]
    h = _rmsnorm(x, g_ref[...]) * (1.0 + mod[1:2]) + mod[0:1]
    h1, h2, h3 = _split3(h)
    z = _dot(h1, wm_ref[...])
    u_ref[...] = z[:, 0:512].astype(BF16).reshape(bs, tm, 512)
    k_ref[...] = (z[:, 512:1024] * K_SCALE).astype(BF16).reshape(bs, tm, 512)
    o_ref[...] = z[:, 1024:1536].astype(BF16).reshape(bs, tm, 512)
    zt = _dot_nt(wt_ref[...], h1).astype(BF16)
    r = _dot(jnp.concatenate([h1, h2, h3], axis=0), wg_ref[...])
    r12 = r[0:rows] + r[rows:2 * rows]
    gate = (r12 + r[2 * rows:]) + pltpu.roll(r12, LANES - GATE_COLS, axis=1) + bgr_ref[...]
    gate_ref[...] = gate.reshape(bs, tm, LANES)
    gate_t = gate.T
    for bb in range(bs):
        cols = slice(bb * tm, (bb + 1) * tm)
        qt_ref[bb] = zt[0:512, cols]
        vt_ref[bb] = zt[512:1024, cols]
        gatet_ref[bb] = gate_t[0:16, cols]


def _inproj(x, mod, mod_row, g, wm, wt, wg, bgr):
    b, t, _ = x.shape
    tm = min(t, 2 * TOKEN_TILE)
    bs = 2 * TOKEN_TILE // tm if mod_row(1) == mod_row(0) else 1
    const = lambda *shape: pl.BlockSpec(shape, lambda i, j: (0,) * len(shape))
    tok = lambda w: pl.BlockSpec((bs, tm, w), lambda i, j: (i, j, 0))
    tok_t = lambda r: pl.BlockSpec((bs, r, tm), lambda i, j: (i, 0, j))
    sd = jax.ShapeDtypeStruct
    return pl.pallas_call(
        _inproj_kernel,
        grid=(b // bs, t // tm),
        in_specs=[tok(D_MODEL),
                  pl.BlockSpec((1, N_MOD, D_MODEL), lambda i, j: (mod_row(i * bs), 0, 0)),
                  const(1, D_MODEL), const(D_MODEL, 1536), const(1024, D_MODEL),
                  const(D_MODEL, LANES), const(1, LANES)],
        out_specs=[tok(512), tok(512), tok(512), tok_t(512), tok_t(512), tok(LANES), tok_t(16)],
        out_shape=[sd((b, t, 512), BF16), sd((b, t, 512), BF16), sd((b, t, 512), BF16),
                   sd((b, 512, t), BF16), sd((b, 512, t), BF16), sd((b, t, LANES), F32),
                   sd((b, 16, t), F32)],
        compiler_params=_params("arbitrary", "arbitrary"),
        name="inproj",
    )(x, mod, g, wm, wt, wg, bgr)


def _log_sigmoid(x):
    return jnp.minimum(x, 0.0) - jnp.log1p(jnp.exp(-jnp.abs(x)))


def _scan_unit(st, k, qt, vt, u_col, u_row, b_row, btot, mask, s_prev, m_prev, use_state):
    dh = HEAD_DIM
    n = st.shape[0]
    ub = jnp.where(mask, jnp.broadcast_to(u_col, (n, n)), -jnp.inf)
    z = jnp.maximum(m_prev, jnp.max(ub, axis=0, keepdims=True))
    p = (jnp.exp(ub - z) * st).astype(BF16)
    ones = jnp.ones((N_ROWS, n), BF16)
    tot = _dot(jnp.concatenate([vt, ones], axis=0), p)
    if use_state:
        tot = tot + jnp.exp(m_prev - z) * _dot(s_prev.astype(BF16), qt)
    floor = jnp.exp(-(b_row + z))
    h_t = tot[:dh] / jnp.maximum(jnp.abs(tot[dh:dh + 1]), floor)
    g_row = btot + u_row
    m_new = jnp.maximum(btot + m_prev, jnp.max(g_row, axis=-1, keepdims=True))
    w_row = jnp.exp(g_row - m_new)
    vw = jnp.concatenate([(vt.astype(F32) * w_row).astype(BF16),
                          jnp.broadcast_to(w_row, (N_ROWS, n)).astype(BF16)], axis=0)
    s_new = jnp.exp(btot + m_prev - m_new) * s_prev + _dot(vw, k)
    return h_t.T, s_new, m_new


def _mlstm_kernel(*refs, nc, zero_init, emit_state):
    it = iter(refs)
    fwd_refs = tuple(next(it) for _ in range(5))
    bwd_refs = tuple(next(it) for _ in range(5)) if nc > 1 else fwd_refs
    if not zero_init:
        s0_ref, m0_ref = next(it), next(it)
    h_refs = (next(it), next(it))
    if emit_state:
        c_out, n_out, m_out = next(it), next(it), next(it)
    s_scr, m_scr = next(it), next(it)

    j = pl.program_id(1)
    n = CHUNK
    dh = HEAD_DIM

    @pl.when(j == 0)
    def _():
        if zero_init:
            s_scr[...] = jnp.zeros_like(s_scr)
            m_scr[...] = jnp.zeros_like(m_scr)
        else:
            s_scr[...] = s0_ref[0]
            m_scr[...] = m0_ref[0]

    rows = lax.broadcasted_iota(jnp.int32, (n, n), 0)
    cols = lax.broadcasted_iota(jnp.int32, (n, n), 1)
    le = rows <= cols
    ge = rows >= cols
    tri_le = le.astype(BF16)
    tri_ge = ge.astype(BF16)
    use_state = not (zero_init and nc == 1)

    def gate_terms(d):
        g_ref, gt_ref = (fwd_refs, bwd_refs)[d][3:5]
        gate = g_ref[0]
        gate_t = gt_ref[0]
        lf = _log_sigmoid(gate)
        lf_t = _log_sigmoid(gate_t)
        tri_c, tri_r = (tri_ge, tri_le) if d == 0 else (tri_le, tri_ge)
        bc = _dot(tri_c, jnp.concatenate(_split3(lf), axis=1))
        b_cols = bc[:, 0:128] + bc[:, 128:256] + bc[:, 256:384]
        br = _dot(jnp.concatenate(_split3(lf_t), axis=0), tri_r)
        b_rows = br[0:16] + br[16:32] + br[32:48]
        return gate, gate_t, b_cols, b_rows, jnp.sum(lf_t, axis=-1, keepdims=True)

    terms = [gate_terms(0), gate_terms(1)]
    hs = ([], [])
    for hd in range(HEADS):
        hsl = slice(hd * dh, (hd + 1) * dh)
        st = None
        for d in range(N_DIR):
            k_ref, qt_ref, vt_ref = (fwd_refs, bwd_refs)[d][0:3]
            gate, gate_t, b_cols, b_rows, tot_rows = terms[d]
            ci = d * 8 + hd
            cf = d * 8 + 4 + hd
            unit = d * HEADS + hd
            k = k_ref[0, :, hsl]
            qt = qt_ref[0, hsl, :]
            if st is None or nc > 1:
                st = _dot(k, qt)
            mask = le if d == 0 else ge
            h, s_new, m_new = _scan_unit(
                st, k, qt, vt_ref[0, hsl, :],
                gate[:, ci:ci + 1] - b_cols[:, cf:cf + 1],
                gate_t[ci:ci + 1, :] - b_rows[cf:cf + 1, :],
                b_rows[cf:cf + 1, :], tot_rows[cf:cf + 1, :],
                mask, s_scr[unit], m_scr[unit][:, 0:1], use_state)
            s_scr[unit] = s_new
            m_scr[unit] = jnp.broadcast_to(m_new, (1, LANES))
            hs[d].append(h)
    for d in range(N_DIR):
        h_refs[d][0] = jnp.concatenate(hs[d], axis=1).astype(BF16)

    if emit_state:
        @pl.when(j == nc - 1)
        def _():
            for unit in range(N_DIR * HEADS):
                s = s_scr[unit]
                c_out[0, unit] = s[:dh].T
                n_out[0, unit] = s[dh:dh + 1]
                m_out[0, unit] = m_scr[unit]


def _mlstm(k, qt, vt, gate, gate_t, s0, m0, emit_state):
    b, t, _ = k.shape
    nc = t // CHUNK
    zero_init = s0 is None
    nu = N_DIR * HEADS
    fwd = lambda w: pl.BlockSpec((1, CHUNK, w), lambda i, j: (i, j, 0))
    bwd = lambda w: pl.BlockSpec((1, CHUNK, w), lambda i, j: (i, nc - 1 - j, 0))
    fwd_t = lambda r: pl.BlockSpec((1, r, CHUNK), lambda i, j: (i, 0, j))
    bwd_t = lambda r: pl.BlockSpec((1, r, CHUNK), lambda i, j: (i, 0, nc - 1 - j))
    args = [k, qt, vt, gate, gate_t]
    in_specs = [fwd(512), fwd_t(512), fwd_t(512), fwd(LANES), fwd_t(16)]
    if nc > 1:
        args += [k, qt, vt, gate, gate_t]
        in_specs += [bwd(512), bwd_t(512), bwd_t(512), bwd(LANES), bwd_t(16)]
    if not zero_init:
        args += [s0, m0]
        in_specs += [pl.BlockSpec((1, nu, HEAD_DIM + N_ROWS, HEAD_DIM), lambda i, j: (i, 0, 0, 0)),
                     pl.BlockSpec((1, nu, 1, LANES), lambda i, j: (i, 0, 0, 0))]
    sd = jax.ShapeDtypeStruct
    out_shape = [sd((b, t, 512), BF16), sd((b, t, 512), BF16)]
    out_specs = [fwd(512), bwd(512)]
    if emit_state:
        out_shape += [sd((b, nu, HEAD_DIM, HEAD_DIM), F32), sd((b, nu, 1, HEAD_DIM), F32),
                      sd((b, nu, 1, LANES), F32)]
        out_specs += [pl.BlockSpec((1, nu, HEAD_DIM, HEAD_DIM), lambda i, j: (i, 0, 0, 0)),
                      pl.BlockSpec((1, nu, 1, HEAD_DIM), lambda i, j: (i, 0, 0, 0)),
                      pl.BlockSpec((1, nu, 1, LANES), lambda i, j: (i, 0, 0, 0))]
    return pl.pallas_call(
        functools.partial(_mlstm_kernel, nc=nc, zero_init=zero_init, emit_state=emit_state),
        grid=(b, nc),
        in_specs=in_specs,
        out_specs=out_specs,
        out_shape=out_shape,
        scratch_shapes=[pltpu.VMEM((nu, HEAD_DIM + N_ROWS, HEAD_DIM), F32),
                        pltpu.VMEM((nu, 1, LANES), F32)],
        compiler_params=_params("arbitrary", "arbitrary"),
        name="mlstm",
    )(*args)


def _outproj_kernel(x_ref, u_ref, hf_ref, hb_ref, o_ref, mod_ref, pm_ref, pinv_ref, wp_ref, ps_ref,
                    wo_ref, gpm_ref, gpf_ref, wr_ref, x1_ref, h2_ref, xp_ref, lg_ref):
    bs, tm, _ = x_ref.shape
    rows = bs * tm
    x = x_ref[...].reshape(rows, D_MODEL)
    mod = mod_ref[0]
    row0 = pl.multiple_of(pl.program_id(1) * tm, tm)
    diffs = []
    for g in range(POOL_GROUPS):
        sl = slice(g * POOL_GROUP_DIM, (g + 1) * POOL_GROUP_DIM)
        per_seq = []
        for bb in range(bs):
            box = _dot(pm_ref[g], u_ref[bb, :, sl])
            per_seq.append(box * pinv_ref[g] - u_ref[bb, pl.ds(row0, tm), sl].astype(F32))
        diffs.append(jnp.concatenate(per_seq, axis=0).astype(BF16))
    yps = [_dot(jnp.concatenate(diffs[2 * p:2 * p + 2], axis=1), wp_ref[p]) for p in range(POOL_GROUPS // 2)]
    y_pool = jnp.concatenate(yps, axis=1) * ps_ref[...]
    seq_rows = lambda ref: ref[...].reshape(rows, ref.shape[-1]).astype(F32)
    y_ml = jax.nn.sigmoid(seq_rows(o_ref)) * (seq_rows(hf_ref) + seq_rows(hb_ref))
    mix = _dot(jnp.concatenate([y_pool, y_ml], axis=1).astype(BF16), wo_ref[...])
    x1 = x + mod[2:3] * _rmsnorm(mix, gpm_ref[...])
    x1_ref[...] = x1.reshape(bs, tm, D_MODEL)
    h2 = _rmsnorm(x1, gpf_ref[...]) * (1.0 + mod[4:5]) + mod[3:4]
    p1, p2, p3 = _split3(h2)
    h2_ref[...] = p1.reshape(bs, tm, D_MODEL)
    nch = D_MODEL // LANES
    for g in range(rows // 8):
        cols = jnp.stack([h2[g * 8:(g + 1) * 8, cc * LANES:(cc + 1) * LANES] for cc in range(nch)], axis=0)
        bb, r0 = divmod(g * 8, tm)
        xp_ref[bb, r0:r0 + 8] = jnp.swapaxes(cols, 0, 1).astype(BF16)
    r = _dot(jnp.concatenate([p1, p2, p3], axis=0), wr_ref[...])
    r12 = r[0:rows] + r[rows:2 * rows]
    ne = N_EXPERTS
    lg = (r12 + r[2 * rows:])[:, 0:ne] + r12[:, ne:2 * ne] + r[0:rows, 2 * ne:3 * ne]
    lg_ref[...] = jnp.concatenate([lg, jnp.zeros_like(lg)], axis=1).T[0:ne]


def _outproj(x, u, hf, hb, o, mod, mod_row, pm, pinv, wp, ps, wo, gpm, gpf, wr):
    b, t, _ = x.shape
    tm = min(t, 2 * TOKEN_TILE)
    bs = 2 * TOKEN_TILE // tm if mod_row(1) == mod_row(0) else 1
    const = lambda *shape: pl.BlockSpec(shape, lambda i, j: (0,) * len(shape))
    tok = lambda w: pl.BlockSpec((bs, tm, w), lambda i, j: (i, j, 0))
    sd = jax.ShapeDtypeStruct
    return pl.pallas_call(
        _outproj_kernel,
        grid=(b // bs, t // tm),
        in_specs=[tok(D_MODEL),
                  pl.BlockSpec((bs, t, 512), lambda i, j: (i, 0, 0)),
                  tok(512), tok(512), tok(512),
                  pl.BlockSpec((1, N_MOD, D_MODEL), lambda i, j: (mod_row(i * bs), 0, 0)),
                  pl.BlockSpec((POOL_GROUPS, tm, t), lambda i, j: (0, j, 0)),
                  pl.BlockSpec((POOL_GROUPS, tm, 1), lambda i, j: (0, j, 0)),
                  const(POOL_GROUPS // 2, 2 * POOL_GROUP_DIM, 2 * POOL_GROUP_DIM), const(1, POOL_WIDTH),
                  const(D_MODEL, D_MODEL), const(1, D_MODEL), const(1, D_MODEL),
                  const(D_MODEL, 2 * LANES)],
        out_specs=[tok(D_MODEL), tok(D_MODEL),
                   pl.BlockSpec((bs, tm, D_MODEL // LANES, LANES), lambda i, j: (i, j, 0, 0)),
                   pl.BlockSpec((N_EXPERTS, bs * tm), lambda i, j: (0, i * (t // tm) + j))],
        out_shape=[sd((b, t, D_MODEL), F32), sd((b, t, D_MODEL), BF16),
                   sd((b, t, D_MODEL // LANES, LANES), BF16), sd((N_EXPERTS, b * t), F32)],
        compiler_params=_params("arbitrary", "arbitrary"),
        name="outproj",
    )(x, u, hf, hb, o, mod, pm, pinv, wp, ps, wo, gpm, gpf, wr)


def _router_kernel(lg_ref, br_ref, comb_ref, sel_ref):
    ng, gs = N_EXPERT_GROUPS, GROUP_SIZE
    neg = -jnp.inf
    lg = jnp.swapaxes(lg_ref[...], 0, 1)
    br = br_ref[...]
    s = [jax.nn.sigmoid(lg[j]) for j in range(gs)]
    biased = [s[j] + br[:, j, :] for j in range(gs)]
    fold = lambda op, xs: functools.reduce(op, xs)
    m1 = fold(jnp.maximum, biased)
    i1 = fold(jnp.minimum, [jnp.where(biased[j] == m1, j, gs) for j in range(gs)])
    m2 = fold(jnp.maximum, [jnp.where(i1 == j, neg, biased[j]) for j in range(gs)])
    cur = m1 + m2
    gi = lax.broadcasted_iota(jnp.int32, cur.shape, 0)
    gmask = jnp.zeros(cur.shape, F32)
    for _ in range(TOPK_GROUPS):
        mx = jnp.max(cur, axis=0, keepdims=True)
        ix = jnp.min(jnp.where(cur == mx, gi, ng), axis=0, keepdims=True)
        hit = gi == ix
        gmask = jnp.where(hit, 1.0, gmask)
        cur = jnp.where(hit, neg, cur)
    cand = [jnp.where(gmask > 0, biased[j], neg) for j in range(gs)]
    eidx = [gi * gs + j for j in range(gs)]
    selm = [jnp.zeros(cur.shape, F32) for _ in range(gs)]
    for _ in range(TOP_K):
        mx = jnp.max(fold(jnp.maximum, cand), axis=0, keepdims=True)
        ix = jnp.min(fold(jnp.minimum, [jnp.where(cand[j] == mx, eidx[j], N_EXPERTS) for j in range(gs)]),
                     axis=0, keepdims=True)
        for j in range(gs):
            hit = eidx[j] == ix
            selm[j] = jnp.where(hit, 1.0, selm[j])
            cand[j] = jnp.where(hit, neg, cand[j])
    sel = [selm[j] * s[j] for j in range(gs)]
    tot = jnp.sum(fold(jnp.add, sel), axis=0, keepdims=True)
    comb = [sel[j] / tot * ROUTED_SCALE for j in range(gs)]
    sel_ref[...] = jnp.swapaxes(jnp.stack(selm, axis=0), 0, 1)
    comb_ref[...] = jnp.swapaxes(jnp.stack(comb, axis=0), 0, 1)


def _router(logits_t, b_router):
    t = logits_t.shape[1]
    tl = 1024
    shp = (N_EXPERT_GROUPS, GROUP_SIZE, t)
    blk = pl.BlockSpec((N_EXPERT_GROUPS, GROUP_SIZE, tl), lambda j: (0, 0, j))
    comb, sel = pl.pallas_call(
        _router_kernel,
        grid=(t // tl,),
        in_specs=[blk, pl.BlockSpec((N_EXPERT_GROUPS, GROUP_SIZE, 1), lambda j: (0, 0, 0))],
        out_specs=[blk, blk],
        out_shape=[jax.ShapeDtypeStruct(shp, F32), jax.ShapeDtypeStruct(shp, F32)],
        compiler_params=_params("arbitrary"),
        name="router",
    )(logits_t.reshape(shp), b_router.reshape(N_EXPERT_GROUPS, GROUP_SIZE, 1))
    return comb.reshape(N_EXPERTS, t), sel.reshape(N_EXPERTS, t)


def _plan_kernel(sel_ref, comb_ref, pos_ref, wk_ref, meta_ref, *, n_meta):
    t = sel_ref.shape[1]
    tm = float(MOE_TILE)
    sel = sel_ref[...]
    selb = sel.astype(BF16)
    blk = 256
    rr = lax.broadcasted_iota(jnp.int32, (blk, blk), 0)
    cc = lax.broadcasted_iota(jnp.int32, (blk, blk), 1)
    before = (rr < cc).astype(BF16)
    carry = jnp.zeros((N_EXPERTS, 1), F32)
    ranks = []
    for b in range(t // blk):
        sb = selb[:, b * blk:(b + 1) * blk]
        ranks.append(_dot(sb, before) + carry)
        carry = carry + jnp.sum(sel[:, b * blk:(b + 1) * blk], axis=1, keepdims=True)
    rank = jnp.concatenate(ranks, axis=1)
    cnt = carry
    ntile = jnp.floor((cnt + (tm - 0.5)) * (1.0 / tm))
    er = lax.broadcasted_iota(jnp.int32, (N_EXPERTS, N_EXPERTS), 0)
    ec = lax.broadcasted_iota(jnp.int32, (N_EXPERTS, N_EXPERTS), 1)
    below = (ec < er).astype(BF16)
    tstart = _dot(below, jnp.broadcast_to(ntile, (N_EXPERTS, LANES)).astype(BF16))[:, 0:1]
    pos = tstart * tm + rank
    erank = _dot(below, selb)
    comb = comb_ref[...]
    rows, wrows = [], []
    for k in range(TOP_K):
        hit = (sel > 0.0) & (erank == float(k))
        rows.append(jnp.sum(jnp.where(hit, pos, 0.0), axis=0, keepdims=True))
        wrows.append(jnp.sum(jnp.where(hit, comb, 0.0), axis=0, keepdims=True))
    pad = [jnp.zeros((1, t), F32)] * (8 - TOP_K)
    pos_ref[...] = jnp.concatenate(rows + pad, axis=0).astype(jnp.int32)
    wk_ref[...] = jnp.concatenate(wrows + pad, axis=0)

    tau = lax.broadcasted_iota(jnp.int32, (N_EXPERTS, n_meta), 1).astype(F32)
    eidx = lax.broadcasted_iota(jnp.int32, (N_EXPERTS, n_meta), 0).astype(F32)
    te = jnp.sum(((tstart + ntile) <= tau).astype(F32), axis=0, keepdims=True)
    te = jnp.minimum(te, float(N_EXPERTS - 1))
    onehot = eidx == te
    cnt_t = jnp.sum(jnp.where(onehot, cnt, 0.0), axis=0, keepdims=True)
    ts_t = jnp.sum(jnp.where(onehot, tstart, 0.0), axis=0, keepdims=True)
    tr = jnp.clip(cnt_t - (tau[0:1] - ts_t) * tm, 0.0, tm)
    tf = jnp.where((tau[0:1] == ts_t) & (tr > 0.0), 1.0, 0.0)
    meta_ref[...] = jnp.concatenate([te, tr, tf] + [jnp.zeros((1, n_meta), F32)] * 5, axis=0).astype(jnp.int32)


def _plan(sel, comb, n_meta):
    t = sel.shape[1]
    sd = jax.ShapeDtypeStruct
    return pl.pallas_call(
        functools.partial(_plan_kernel, n_meta=n_meta),
        out_shape=[sd((8, t), jnp.int32), sd((8, t), F32), sd((8, n_meta), jnp.int32)],
        compiler_params=pltpu.CompilerParams(vmem_limit_bytes=VMEM_LIMIT),
        name="plan",
    )(sel, comb)


def _slot_tables(pos, wk, n_slots, n_tokens):
    t = pos.shape[1]
    workers = SC_CORES * SC_SUBCORES
    unroll = 8
    unit = math.lcm(SC_LANES * unroll, MOE_TILE // math.gcd(MOE_TILE, workers))
    per = -(-n_slots // (workers * unit)) * unit
    mesh = plsc.VectorSubcoreMesh(core_axis_name="core", subcore_axis_name="subcore")

    def body(pos_hbm, wk_hbm, tok_hbm, wgt_hbm, pos_v, wk_v, tok_v, wgt_v, sems):
        lo = (lax.axis_index("core") * SC_SUBCORES + lax.axis_index("subcore")) * per
        loads = (pltpu.async_copy(pos_hbm.at[pl.ds(0, TOP_K * t)], pos_v, sems.at[0]),
                 pltpu.async_copy(wk_hbm.at[pl.ds(0, TOP_K * t)], wk_v, sems.at[1]))
        dummy = jnp.full((SC_LANES,), n_tokens, jnp.int32)
        zero = jnp.zeros((SC_LANES,), F32)

        @plsc.parallel_loop(0, per, SC_LANES, unroll=unroll)
        def _(i):
            tok_v[pl.ds(i, SC_LANES)] = dummy
            wgt_v[pl.ds(i, SC_LANES)] = zero

        for cp in loads:
            cp.wait()
        lane = lax.iota(jnp.int32, SC_LANES)
        for k in range(TOP_K):
            @plsc.parallel_loop(0, t, SC_LANES, unroll=unroll)
            def _(i):
                p = pos_v[pl.ds(k * t + i, SC_LANES)] - lo
                mine = (p >= 0) & (p < per)
                idx = jnp.where(mine, p, 0)
                plsc.store_scatter(tok_v, [idx], lane + i, mask=mine)
                plsc.store_scatter(wgt_v, [idx], wk_v[pl.ds(k * t + i, SC_LANES)], mask=mine)
        stores = (pltpu.async_copy(tok_v, tok_hbm.at[pl.ds(lo, per)], sems.at[0]),
                  pltpu.async_copy(wgt_v, wgt_hbm.at[pl.ds(lo, per)], sems.at[1]))
        for cp in stores:
            cp.wait()

    sd = jax.ShapeDtypeStruct
    build = pl.kernel(body, out_type=(sd((workers * per,), jnp.int32), sd((workers * per,), F32)), mesh=mesh,
                      scratch_types=[pltpu.VMEM((TOP_K * t,), jnp.int32), pltpu.VMEM((TOP_K * t,), F32),
                                     pltpu.VMEM((per,), jnp.int32), pltpu.VMEM((per,), F32),
                                     pltpu.SemaphoreType.DMA((2,))],
                      compiler_params=pltpu.CompilerParams(needs_layout_passes=False), name="slot_tables")
    return build(pos.reshape(-1), wk.reshape(-1))


def _experts_kernel(te_ref, tr_ref, xc_ref, xs_ref, slots_ref, wslot_ref, wgu_ref, wd_ref, acc_out,
                    tbl_ref, xbuf, acc, stage0, stage1, act0, act1, ybuf0, ybuf1, sems):
    s = pl.program_id(0)
    n_tiles = te_ref.shape[0]
    tm = MOE_TILE
    nch = D_MODEL // LANES
    tile_at = lambda lag: jnp.clip(s - lag, 0, n_tiles - 1)
    t_g, t_1, t_2, t_3 = tile_at(0), tile_at(1), tile_at(2), tile_at(3)

    @pl.when(s == 0)
    def _():
        tc, ts = xc_ref.shape[0], xs_ref.shape[0]
        copies = (pltpu.make_async_copy(xc_ref, xbuf.at[pl.ds(0, tc)], sems.at[0]),
                  pltpu.make_async_copy(xs_ref, xbuf.at[pl.ds(tc, ts)], sems.at[1]))
        tbl_init = pltpu.make_async_copy(slots_ref, tbl_ref, sems.at[3])
        tbl_init.start()
        for cp in copies:
            cp.start()
        n_pad = xbuf.shape[0] - tc - ts
        xbuf[pl.ds(tc + ts, n_pad)] = jnp.zeros((n_pad,) + xbuf.shape[1:], BF16)
        for ref in (acc, stage0, stage1, act0, act1, ybuf0, ybuf1):
            ref[...] = jnp.zeros_like(ref)
        tbl_init.wait()
        for cp in copies:
            cp.wait()

    def gather(tile, stage):
        base = tile * tm
        for j in range(tm):
            stage[pl.ds(j * nch, nch), :] = xbuf[tbl_ref[base + j]].astype(F32)

    def gate_up(tile, stage, act):
        xb = _token_rows_from_slabs(
            lambda g: stage[g * 8 * nch:(g + 1) * 8 * nch, :].reshape(8, nch, LANES), tm).astype(BF16)
        gu = _dot(xb, wgu_ref[0].astype(BF16))
        w_row = jnp.broadcast_to(wslot_ref[tile], (8, tm))
        w_col = jnp.concatenate([w_row[:, c:c + LANES].T for c in range(0, tm, LANES)], axis=0)[:, 0:1]
        act[...] = (_silu(gu[:, :EXPERT_DIM]) * gu[:, EXPERT_DIM:] * w_col).astype(BF16)

    def down(act, ybuf):
        y = _dot(act[...], wd_ref[0].astype(BF16))
        for cc in range(nch):
            ybuf[cc * Y_PITCH:cc * Y_PITCH + tm, :] = y[:, cc * LANES:(cc + 1) * LANES]

    def scatter(tile, ybuf):
        base = tile * tm
        sc_n = 16
        for i in range(tm // sc_n):
            toks = [tbl_ref[base + i * sc_n + u] for u in range(sc_n)]
            olds = [acc[toks[u]] for u in range(sc_n)]
            news = [olds[u] + ybuf[pl.ds(i * sc_n + u, nch, stride=Y_PITCH), :] for u in range(sc_n)]
            for u in range(sc_n):
                acc[toks[u]] = news[u]

    busy = (tr_ref[t_g] + tr_ref[t_1] + tr_ref[t_2] + tr_ref[t_3]) > 0
    bufs = ((stage0, act0, ybuf0), (stage1, act1, ybuf1))
    for par in range(2):
        stage_p, act_p, ybuf_p = bufs[par]
        stage_q, act_q, ybuf_q = bufs[1 - par]

        @pl.when(busy & (s % 2 == par))
        def _():
            gather(t_g, stage_p)
            gate_up(t_1, stage_q, act_q)
            down(act_p, ybuf_p)
            scatter(t_3, ybuf_q)

    @pl.when(s == pl.num_programs(0) - 1)
    def _():
        cp = pltpu.make_async_copy(acc, acc_out, sems.at[2])
        cp.start()
        cp.wait()


def _experts(te, tr, xc, xs, slots, wslot, wgu, wd):
    n_tok = xc.shape[0] + xs.shape[0] + 8
    n_tiles = te.shape[0]
    tm = MOE_TILE
    nch = D_MODEL // LANES
    vm = pltpu.VMEM
    grid_spec = pltpu.PrefetchScalarGridSpec(
        num_scalar_prefetch=2,
        grid=(n_tiles + 3,),
        in_specs=[pl.BlockSpec(memory_space=pl.ANY), pl.BlockSpec(memory_space=pl.ANY),
                  pl.BlockSpec(memory_space=pl.ANY),
                  pl.BlockSpec(wslot.shape, lambda s, *_: (0, 0, 0), pipeline_mode=pl.Buffered(1)),
                  pl.BlockSpec((1, D_MODEL, 2 * EXPERT_DIM), lambda s, te, *_: (te[jnp.clip(s - 1, 0, n_tiles - 1)], 0, 0)),
                  pl.BlockSpec((1, EXPERT_DIM, D_MODEL), lambda s, te, *_: (te[jnp.clip(s - 2, 0, n_tiles - 1)], 0, 0))],
        out_specs=pl.BlockSpec(memory_space=pl.ANY),
        scratch_shapes=[pltpu.SMEM(slots.shape, jnp.int32),
                        vm((n_tok, nch, LANES), BF16), vm((n_tok, nch, LANES), F32),
                        vm((tm * nch, LANES), F32), vm((tm * nch, LANES), F32),
                        vm((tm, EXPERT_DIM), BF16), vm((tm, EXPERT_DIM), BF16),
                        vm((nch * Y_PITCH, LANES), F32), vm((nch * Y_PITCH, LANES), F32),
                        pltpu.SemaphoreType.DMA((4,))],
    )
    return pl.pallas_call(
        _experts_kernel,
        grid_spec=grid_spec,
        out_shape=jax.ShapeDtypeStruct((n_tok, nch, LANES), F32),
        compiler_params=pltpu.CompilerParams(dimension_semantics=("arbitrary",),
                                             vmem_limit_bytes=EXPERTS_VMEM_LIMIT),
        name="experts",
    )(te, tr, xc, xs, slots, wslot, wgu, wd)


def _final_kernel(acc_ref, h_ref, x1_ref, mod_ref, wsg_ref, wsd_ref, gpo_ref, out_ref):
    routed = _token_rows_from_slabs(lambda g: acc_ref[g * 8:(g + 1) * 8], acc_ref.shape[0])
    gs = _dot(h_ref[...], wsg_ref[...])
    act = _silu(gs[:, :SHARED_DIM]) * gs[:, SHARED_DIM:]
    f = routed + _dot(act.astype(BF16), wsd_ref[...])
    out_ref[...] = x1_ref[...] + mod_ref[0][5:6] * _rmsnorm(f, gpo_ref[...])


def _final(acc, tile0, h2, x1, mod, mod_row, wsg, wsd, gpo):
    n = h2.shape[0]
    tm = FINAL_TILE
    const = lambda *shape: pl.BlockSpec(shape, lambda i: (0,) * len(shape))
    tok = lambda w: pl.BlockSpec((tm, w), lambda i: (i, 0))
    return pl.pallas_call(
        _final_kernel,
        grid=(n // tm,),
        in_specs=[pl.BlockSpec((tm, D_MODEL // LANES, LANES), lambda i: (i + tile0, 0, 0)),
                  tok(D_MODEL), tok(D_MODEL),
                  pl.BlockSpec((1, N_MOD, D_MODEL), lambda i: (mod_row(i), 0, 0)),
                  const(D_MODEL, 2 * SHARED_DIM), const(SHARED_DIM, D_MODEL), const(1, D_MODEL)],
        out_specs=tok(D_MODEL),
        out_shape=jax.ShapeDtypeStruct((n, D_MODEL), F32),
        compiler_params=_params("arbitrary"),
        name="final",
    )(acc, h2, x1, mod, wsg, wsd, gpo)


def _window_bounds(n, w):
    idx = np.arange(n)
    return np.clip(idx - w // 2, 0, n), np.clip(idx + w - w // 2, 0, n)


def _pool_operators(t, grid):
    mats, invs = [], []
    for w in POOL_WINDOWS:
        if grid:
            rlo, rhi = _window_bounds(t // GRID_W, w)
            clo, chi = _window_bounds(GRID_W, w)
            r = np.arange(t) // GRID_W
            c = np.arange(t) % GRID_W
            m = ((r[None, :] >= rlo[r][:, None]) & (r[None, :] < rhi[r][:, None])
                 & (c[None, :] >= clo[c][:, None]) & (c[None, :] < chi[c][:, None]))
            cnt = (rhi - rlo)[r] * (chi - clo)[c]
        else:
            lo, hi = _window_bounds(t, w)
            sidx = np.arange(t)
            m = (sidx[None, :] >= lo[:, None]) & (sidx[None, :] < hi[:, None])
            cnt = hi - lo
        mats.append(m.astype(np.float32))
        invs.append((1.0 / cnt.astype(np.float64)).astype(np.float32)[:, None])
    return jnp.asarray(np.stack(mats), BF16), jnp.asarray(np.stack(invs), F32)


def kernel(x_prompt, x_sample, state_C, state_n, state_m, c, c_ctx, w_ada, b_ada, g_pre_mix, w_in, b_gate,
           w_pool, pool_scale, w_out, g_post_mix, g_pre_ffn, w_router, b_router, w_expert_gu, w_expert_down,
           w_shared_gu, w_shared_down, g_post_ffn):
    b_ctx = x_prompt.shape[0]
    b_lat = x_sample.shape[0]
    nu = N_DIR * HEADS
    l = 0
    row = lambda a: a[l].reshape(1, -1).astype(F32)

    cvec = jnp.zeros((16, D_MODEL), F32).at[0].set(c_ctx.astype(F32)).at[1:1 + b_lat].set(c.astype(F32))
    mod = _mod_rows(cvec, w_ada[l], b_ada[l]).reshape(16, N_MOD, D_MODEL)

    w_in_l = w_in[l]
    p0 = POOL_WIDTH
    mw = MLSTM_WIDTH
    w_u, w_q, w_k, w_v, w_o = (w_in_l[:, lo:lo + 512] for lo in (0, p0, p0 + mw, p0 + 2 * mw, p0 + 3 * mw))
    wm = jnp.concatenate([w_u, w_k, w_o], axis=1).astype(BF16)
    wt = jnp.concatenate([w_q.T, w_v.T], axis=0).astype(BF16)
    wg_cols = w_in_l[:, p0 + 4 * mw:]
    wg = jnp.pad(jnp.concatenate(_split2(wg_cols), axis=1), ((0, 0), (0, LANES - 2 * GATE_COLS)))
    bg = b_gate[l].reshape(GATE_COLS).astype(F32)
    bgr = jnp.pad(bg, (0, LANES - GATE_COLS)).reshape(1, LANES)
    wpl = w_pool[l].astype(BF16)
    zg = jnp.zeros((POOL_GROUP_DIM, POOL_GROUP_DIM), BF16)
    wp = jnp.stack([jnp.block([[wpl[2 * p], zg], [zg, wpl[2 * p + 1]]]) for p in range(POOL_GROUPS // 2)])
    wo = w_out[l].astype(BF16)
    wr = jnp.pad(jnp.concatenate(_split3(w_router[l].astype(F32)), axis=1), ((0, 0), (0, 2 * LANES - 3 * N_EXPERTS)))
    wsg = w_shared_gu[l].astype(BF16)
    wsd = w_shared_down[l].astype(BF16)

    def mixer(x, mod_row, grid, s0, m0, emit_state):
        t = x.shape[1]
        u, k, o, qt, vt, gate, gate_t = _inproj(x.astype(F32), mod, mod_row, row(g_pre_mix), wm, wt, wg, bgr)
        outs = _mlstm(k, qt, vt, gate, gate_t, s0, m0, emit_state)
        hf, hb = outs[0], outs[1]
        pm, pinv = _pool_operators(t, grid)
        x1, h2, xp, lg = _outproj(x.astype(F32), u, hf, hb, o, mod, mod_row, pm, pinv, wp, row(pool_scale), wo,
                                  row(g_post_mix), row(g_pre_ffn), wr)
        return x1, h2, xp, lg, outs[2:]

    ctx_row = lambda i: 0
    lat_row = lambda i: i + 1
    x1c, h2c, xpc, lgc, (c_new, n_new, m_new) = mixer(x_prompt, ctx_row, False, None, None, True)
    s0 = jnp.concatenate(
        [jnp.swapaxes(state_C[:, l].reshape(b_lat, nu, HEAD_DIM, HEAD_DIM).astype(F32), -1, -2),
         jnp.broadcast_to(state_n[:, l].reshape(b_lat, nu, 1, HEAD_DIM).astype(F32),
                          (b_lat, nu, N_ROWS, HEAD_DIM))], axis=-2)
    m0 = jnp.broadcast_to(state_m[:, l].reshape(b_lat, nu, 1, 1).astype(F32), (b_lat, nu, 1, LANES))
    x1s, h2s, xps, lgs, _ = mixer(x_sample, lat_row, True, s0, m0, False)

    tc = b_ctx * x_prompt.shape[1]
    ts = b_lat * x_sample.shape[1]
    n_tok = tc + ts
    lg_all = jnp.concatenate([lgc, lgs], axis=1)
    comb, sel = _router(lg_all, b_router[l].astype(F32))
    n_tiles = n_tok * TOP_K // MOE_TILE + N_EXPERTS
    n_meta = -(-n_tiles // LANES) * LANES
    pos, wk, meta = _plan(sel, comb, n_meta)
    slots, wslot = _slot_tables(pos, wk, n_tiles * MOE_TILE, n_tok)

    slab = (D_MODEL // LANES, LANES)
    acc = _experts(meta[0, :n_tiles], meta[1, :n_tiles],
                   xpc.reshape((tc,) + slab), xps.reshape((ts,) + slab),
                   slots, wslot.reshape(-1, 1, MOE_TILE),
                   w_expert_gu[l], w_expert_down[l])

    fin = functools.partial(_final, wsg=wsg, wsd=wsd, gpo=row(g_post_ffn))
    tiles_per_lat = x_sample.shape[1] // FINAL_TILE
    yc = fin(acc, 0, h2c.reshape(tc, D_MODEL), x1c.reshape(tc, D_MODEL), mod, ctx_row)
    ys = fin(acc, tc // FINAL_TILE, h2s.reshape(ts, D_MODEL), x1s.reshape(ts, D_MODEL), mod,
             lambda i: i // tiles_per_lat + 1)

    new_c = c_new.reshape(b_ctx, 1, N_DIR, HEADS, HEAD_DIM, HEAD_DIM)
    new_n = n_new.reshape(b_ctx, 1, N_DIR, HEADS, HEAD_DIM)
    new_m = m_new[..., 0].reshape(b_ctx, 1, N_DIR, HEADS)
    return (yc.reshape(x_prompt.shape), ys.reshape(x_sample.shape), new_c, new_n, new_m)
```

```python
import functools

import jax
import jax.numpy as jnp
import numpy as np
from jax import lax
from jax.experimental import pallas as pl
from jax.experimental.pallas import tpu as pltpu
from jax.experimental.pallas import tpu_sc as plsc

F32 = jnp.float32
BF16 = jnp.bfloat16

D_MODEL = 1024
GRID_W = 64
POOL_WIDTH = 512
POOL_GROUPS = 4
POOL_GROUP_DIM = 128
POOL_WINDOWS = (2, 4, 8, 16)
HEADS = 4
HEAD_DIM = 128
MLSTM_WIDTH = HEADS * HEAD_DIM
N_DIR = 2
GATE_COLS = N_DIR * 2 * HEADS
N_EXPERTS = 64
TOP_K = 6
N_EXPERT_GROUPS = 8
GROUP_SIZE = N_EXPERTS // N_EXPERT_GROUPS
TOPK_GROUPS = 4
EXPERT_DIM = 256
SHARED_DIM = 256
ROUTED_SCALE = 2.5
N_MOD = 6
EPS = 1e-6
K_SCALE = HEAD_DIM ** -0.5

LANES = 128
CHUNK = 256
N_ROWS = 16
TOKEN_TILE = 256
FINAL_TILE = 512
MOE_TILE = 320
Y_PITCH = MOE_TILE + 8
VMEM_LIMIT = 56 * 1024 * 1024
EXPERTS_VMEM_LIMIT = 58 * 1024 * 1024
SC_CORES = 2
SC_SUBCORES = 16
SC_LANES = 16


def _split3(x):
    p1 = x.astype(BF16)
    r1 = x - p1.astype(F32)
    p2 = r1.astype(BF16)
    p3 = (r1 - p2.astype(F32)).astype(BF16)
    return p1, p2, p3


def _split2(x):
    p1 = x.astype(BF16)
    p2 = (x - p1.astype(F32)).astype(BF16)
    return p1, p2


def _dot(a, b):
    return jnp.dot(a, b, preferred_element_type=F32)


def _dot_nt(a, b):
    return lax.dot_general(a, b, (((1,), (1,)), ((), ())), preferred_element_type=F32)


def _rmsnorm(x, g):
    return x * lax.rsqrt(jnp.mean(x * x, axis=-1, keepdims=True) + EPS) * g


def _silu(x):
    return x * jax.nn.sigmoid(x)


def _token_rows_from_slabs(read_block, n_tok):
    nch = D_MODEL // LANES
    cols = [[] for _ in range(nch)]
    for g in range(n_tok // 8):
        blk = jnp.swapaxes(read_block(g), 0, 1)
        for cc in range(nch):
            cols[cc].append(blk[cc])
    return jnp.concatenate([jnp.concatenate(c, axis=0) for c in cols], axis=1)


def _params(*sem):
    return pltpu.CompilerParams(dimension_semantics=sem, vmem_limit_bytes=VMEM_LIMIT)


def _mod_kernel(c_ref, w_ref, b_ref, o_ref):
    a = _silu(c_ref[...])
    a_stack = jnp.concatenate(_split3(a), axis=0)
    w1, w2 = _split2(w_ref[...])
    r1 = _dot(a_stack, w1)
    r2 = _dot(a_stack[:32], w2)
    o_ref[...] = (r1[0:16] + r1[16:32] + r1[32:48] + r2[0:16] + r2[16:32]) + b_ref[...]


def _mod_rows(cvec, w_ada, b_ada):
    n = N_MOD * D_MODEL
    tn = 1536
    return pl.pallas_call(
        _mod_kernel,
        grid=(n // tn,),
        in_specs=[pl.BlockSpec((16, D_MODEL), lambda j: (0, 0)),
                  pl.BlockSpec((D_MODEL, tn), lambda j: (0, j)),
                  pl.BlockSpec((1, tn), lambda j: (0, j))],
        out_specs=pl.BlockSpec((16, tn), lambda j: (0, j)),
        out_shape=jax.ShapeDtypeStruct((16, n), F32),
        compiler_params=_params("arbitrary"),
        name="mod",
    )(cvec, w_ada, b_ada.reshape(1, n))


def _inproj_kernel(x_ref, mod_ref, g_ref, wm_ref, wt_ref, wg_ref, bgr_ref,
                   u_ref, k_ref, o_ref, qt_ref, vt_ref, gate_ref, gatet_ref):
    bs, tm, _ = x_ref.shape
    rows = bs * tm
    x = x_ref[...].reshape(rows, D_MODEL)
    mod = mod_ref[0]
    h = _rmsnorm(x, g_ref[...]) * (1.0 + mod[1:2]) + mod[0:1]
    h1, h2, h3 = _split3(h)
    z = _dot(h1, wm_ref[...])
    u_ref[...] = z[:, 0:512].astype(BF16).reshape(bs, tm, 512)
    k_ref[...] = (z[:, 512:1024] * K_SCALE).astype(BF16).reshape(bs, tm, 512)
    o_ref[...] = z[:, 1024:1536].astype(BF16).reshape(bs, tm, 512)
    zt = _dot_nt(wt_ref[...], h1).astype(BF16)
    r = _dot(jnp.concatenate([h1, h2, h3], axis=0), wg_ref[...])
    r12 = r[0:rows] + r[rows:2 * rows]
    gate = (r12 + r[2 * rows:]) + pltpu.roll(r12, LANES - GATE_COLS, axis=1) + bgr_ref[...]
    gate_ref[...] = gate.reshape(bs, tm, LANES)
    gate_t = gate.T
    for bb in range(bs):
        cols = slice(bb * tm, (bb + 1) * tm)
        qt_ref[bb] = zt[0:512, cols]
        vt_ref[bb] = zt[512:1024, cols]
        gatet_ref[bb] = gate_t[0:16, cols]


def _inproj(x, mod, mod_row, g, wm, wt, wg, bgr):
    b, t, _ = x.shape
    tm = min(t, 2 * TOKEN_TILE)
    bs = 2 * TOKEN_TILE // tm if mod_row(1) == mod_row(0) else 1
    const = lambda *shape: pl.BlockSpec(shape, lambda i, j: (0,) * len(shape))
    tok = lambda w: pl.BlockSpec((bs, tm, w), lambda i, j: (i, j, 0))
    tok_t = lambda r: pl.BlockSpec((bs, r, tm), lambda i, j: (i, 0, j))
    sd = jax.ShapeDtypeStruct
    return pl.pallas_call(
        _inproj_kernel,
        grid=(b // bs, t // tm),
        in_specs=[tok(D_MODEL),
                  pl.BlockSpec((1, N_MOD, D_MODEL), lambda i, j: (mod_row(i * bs), 0, 0)),
                  const(1, D_MODEL), const(D_MODEL, 1536), const(1024, D_MODEL),
                  const(D_MODEL, LANES), const(1, LANES)],
        out_specs=[tok(512), tok(512), tok(512), tok_t(512), tok_t(512), tok(LANES), tok_t(16)],
        out_shape=[sd((b, t, 512), BF16), sd((b, t, 512), BF16), sd((b, t, 512), BF16),
                   sd((b, 512, t), BF16), sd((b, 512, t), BF16), sd((b, t, LANES), F32),
                   sd((b, 16, t), F32)],
        compiler_params=_params("arbitrary", "arbitrary"),
        name="inproj",
    )(x, mod, g, wm, wt, wg, bgr)


def _log_sigmoid(x):
    return jnp.minimum(x, 0.0) - jnp.log1p(jnp.exp(-jnp.abs(x)))


def _scan_unit(st, k, qt, vt, u_col, u_row, b_row, btot, mask, s_prev, m_prev, use_state):
    dh = HEAD_DIM
    n = st.shape[0]
    ub = jnp.where(mask, jnp.broadcast_to(u_col, (n, n)), -jnp.inf)
    z = jnp.maximum(m_prev, jnp.max(ub, axis=0, keepdims=True))
    p = (jnp.exp(ub - z) * st).astype(BF16)
    ones = jnp.ones((N_ROWS, n), BF16)
    tot = _dot(jnp.concatenate([vt, ones], axis=0), p)
    if use_state:
        tot = tot + jnp.exp(m_prev - z) * _dot(s_prev.astype(BF16), qt)
    floor = jnp.exp(-(b_row + z))
    h_t = tot[:dh] / jnp.maximum(jnp.abs(tot[dh:dh + 1]), floor)
    g_row = btot + u_row
    m_new = jnp.maximum(btot + m_prev, jnp.max(g_row, axis=-1, keepdims=True))
    w_row = jnp.exp(g_row - m_new)
    vw = jnp.concatenate([(vt.astype(F32) * w_row).astype(BF16),
                          jnp.broadcast_to(w_row, (N_ROWS, n)).astype(BF16)], axis=0)
    s_new = jnp.exp(btot + m_prev - m_new) * s_prev + _dot(vw, k)
    return h_t.T, s_new, m_new


def _mlstm_kernel(*refs, nc, zero_init, emit_state):
    it = iter(refs)
    fwd_refs = tuple(next(it) for _ in range(5))
    bwd_refs = tuple(next(it) for _ in range(5)) if nc > 1 else fwd_refs
    if not zero_init:
        s0_ref, m0_ref = next(it), next(it)
    h_refs = (next(it), next(it))
    if emit_state:
        c_out, n_out, m_out = next(it), next(it), next(it)
    s_scr, m_scr = next(it), next(it)

    j = pl.program_id(1)
    n = CHUNK
    dh = HEAD_DIM

    @pl.when(j == 0)
    def _():
        if zero_init:
            s_scr[...] = jnp.zeros_like(s_scr)
            m_scr[...] = jnp.zeros_like(m_scr)
        else:
            s_scr[...] = s0_ref[0]
            m_scr[...] = m0_ref[0]

    rows = lax.broadcasted_iota(jnp.int32, (n, n), 0)
    cols = lax.broadcasted_iota(jnp.int32, (n, n), 1)
    le = rows <= cols
    ge = rows >= cols
    tri_le = le.astype(BF16)
    tri_ge = ge.astype(BF16)
    use_state = not (zero_init and nc == 1)

    def gate_terms(d):
        g_ref, gt_ref = (fwd_refs, bwd_refs)[d][3:5]
        gate = g_ref[0]
        gate_t = gt_ref[0]
        lf = _log_sigmoid(gate)
        lf_t = _log_sigmoid(gate_t)
        tri_c, tri_r = (tri_ge, tri_le) if d == 0 else (tri_le, tri_ge)
        bc = _dot(tri_c, jnp.concatenate(_split3(lf), axis=1))
        b_cols = bc[:, 0:128] + bc[:, 128:256] + bc[:, 256:384]
        br = _dot(jnp.concatenate(_split3(lf_t), axis=0), tri_r)
        b_rows = br[0:16] + br[16:32] + br[32:48]
        return gate, gate_t, b_cols, b_rows, jnp.sum(lf_t, axis=-1, keepdims=True)

    terms = [gate_terms(0), gate_terms(1)]
    hs = ([], [])
    for hd in range(HEADS):
        hsl = slice(hd * dh, (hd + 1) * dh)
        st = None
        for d in range(N_DIR):
            k_ref, qt_ref, vt_ref = (fwd_refs, bwd_refs)[d][0:3]
            gate, gate_t, b_cols, b_rows, tot_rows = terms[d]
            ci = d * 8 + hd
            cf = d * 8 + 4 + hd
            unit = d * HEADS + hd
            k = k_ref[0, :, hsl]
            qt = qt_ref[0, hsl, :]
            if st is None or nc > 1:
                st = _dot(k, qt)
            mask = le if d == 0 else ge
            h, s_new, m_new = _scan_unit(
                st, k, qt, vt_ref[0, hsl, :],
                gate[:, ci:ci + 1] - b_cols[:, cf:cf + 1],
                gate_t[ci:ci + 1, :] - b_rows[cf:cf + 1, :],
                b_rows[cf:cf + 1, :], tot_rows[cf:cf + 1, :],
                mask, s_scr[unit], m_scr[unit][:, 0:1], use_state)
            s_scr[unit] = s_new
            m_scr[unit] = jnp.broadcast_to(m_new, (1, LANES))
            hs[d].append(h)
    for d in range(N_DIR):
        h_refs[d][0] = jnp.concatenate(hs[d], axis=1).astype(BF16)

    if emit_state:
        @pl.when(j == nc - 1)
        def _():
            for unit in range(N_DIR * HEADS):
                s = s_scr[unit]
                c_out[0, unit] = s[:dh].T
                n_out[0, unit] = s[dh:dh + 1]
                m_out[0, unit] = m_scr[unit]


def _mlstm(k, qt, vt, gate, gate_t, s0, m0, emit_state):
    b, t, _ = k.shape
    nc = t // CHUNK
    zero_init = s0 is None
    nu = N_DIR * HEADS
    fwd = lambda w: pl.BlockSpec((1, CHUNK, w), lambda i, j: (i, j, 0))
    bwd = lambda w: pl.BlockSpec((1, CHUNK, w), lambda i, j: (i, nc - 1 - j, 0))
    fwd_t = lambda r: pl.BlockSpec((1, r, CHUNK), lambda i, j: (i, 0, j))
    bwd_t = lambda r: pl.BlockSpec((1, r, CHUNK), lambda i, j: (i, 0, nc - 1 - j))
    args = [k, qt, vt, gate, gate_t]
    in_specs = [fwd(512), fwd_t(512), fwd_t(512), fwd(LANES), fwd_t(16)]
    if nc > 1:
        args += [k, qt, vt, gate, gate_t]
        in_specs += [bwd(512), bwd_t(512), bwd_t(512), bwd(LANES), bwd_t(16)]
    if not zero_init:
        args += [s0, m0]
        in_specs += [pl.BlockSpec((1, nu, HEAD_DIM + N_ROWS, HEAD_DIM), lambda i, j: (i, 0, 0, 0)),
                     pl.BlockSpec((1, nu, 1, LANES), lambda i, j: (i, 0, 0, 0))]
    sd = jax.ShapeDtypeStruct
    out_shape = [sd((b, t, 512), BF16), sd((b, t, 512), BF16)]
    out_specs = [fwd(512), bwd(512)]
    if emit_state:
        out_shape += [sd((b, nu, HEAD_DIM, HEAD_DIM), F32), sd((b, nu, 1, HEAD_DIM), F32),
                      sd((b, nu, 1, LANES), F32)]
        out_specs += [pl.BlockSpec((1, nu, HEAD_DIM, HEAD_DIM), lambda i, j: (i, 0, 0, 0)),
                      pl.BlockSpec((1, nu, 1, HEAD_DIM), lambda i, j: (i, 0, 0, 0)),
                      pl.BlockSpec((1, nu, 1, LANES), lambda i, j: (i, 0, 0, 0))]
    return pl.pallas_call(
        functools.partial(_mlstm_kernel, nc=nc, zero_init=zero_init, emit_state=emit_state),
        grid=(b, nc),
        in_specs=in_specs,
        out_specs=out_specs,
        out_shape=out_shape,
        scratch_shapes=[pltpu.VMEM((nu, HEAD_DIM + N_ROWS, HEAD_DIM), F32),
                        pltpu.VMEM((nu, 1, LANES), F32)],
        compiler_params=_params("arbitrary", "arbitrary"),
        name="mlstm",
    )(*args)


def _outproj_kernel(x_ref, u_ref, hf_ref, hb_ref, o_ref, mod_ref, pm_ref, pinv_ref, wp_ref, ps_ref,
                    wo_ref, gpm_ref, gpf_ref, wr_ref, x1_ref, h2_ref, xp_ref, lg_ref):
    bs, tm, _ = x_ref.shape
    rows = bs * tm
    x = x_ref[...].reshape(rows, D_MODEL)
    mod = mod_ref[0]
    row0 = pl.multiple_of(pl.program_id(1) * tm, tm)
    diffs = []
    for g in range(POOL_GROUPS):
        sl = slice(g * POOL_GROUP_DIM, (g + 1) * POOL_GROUP_DIM)
        per_seq = []
        for bb in range(bs):
            box = _dot(pm_ref[g], u_ref[bb, :, sl])
            per_seq.append(box * pinv_ref[g] - u_ref[bb, pl.ds(row0, tm), sl].astype(F32))
        diffs.append(jnp.concatenate(per_seq, axis=0).astype(BF16))
    yps = [_dot(jnp.concatenate(diffs[2 * p:2 * p + 2], axis=1), wp_ref[p]) for p in range(POOL_GROUPS // 2)]
    y_pool = jnp.concatenate(yps, axis=1) * ps_ref[...]
    seq_rows = lambda ref: ref[...].reshape(rows, ref.shape[-1]).astype(F32)
    y_ml = jax.nn.sigmoid(seq_rows(o_ref)) * (seq_rows(hf_ref) + seq_rows(hb_ref))
    mix = _dot(jnp.concatenate([y_pool, y_ml], axis=1).astype(BF16), wo_ref[...])
    x1 = x + mod[2:3] * _rmsnorm(mix, gpm_ref[...])
    x1_ref[...] = x1.reshape(bs, tm, D_MODEL)
    h2 = _rmsnorm(x1, gpf_ref[...]) * (1.0 + mod[4:5]) + mod[3:4]
    p1, p2, p3 = _split3(h2)
    h2_ref[...] = p1.reshape(bs, tm, D_MODEL)
    nch = D_MODEL // LANES
    for g in range(rows // 8):
        cols = jnp.stack([h2[g * 8:(g + 1) * 8, cc * LANES:(cc + 1) * LANES] for cc in range(nch)], axis=0)
        bb, r0 = divmod(g * 8, tm)
        xp_ref[bb, r0:r0 + 8] = jnp.swapaxes(cols, 0, 1).astype(BF16)
    r = _dot(jnp.concatenate([p1, p2, p3], axis=0), wr_ref[...])
    r12 = r[0:rows] + r[rows:2 * rows]
    ne = N_EXPERTS
    lg = (r12 + r[2 * rows:])[:, 0:ne] + r12[:, ne:2 * ne] + r[0:rows, 2 * ne:3 * ne]
    lg_ref[...] = jnp.concatenate([lg, jnp.zeros_like(lg)], axis=1).T[0:ne]


def _outproj(x, u, hf, hb, o, mod, mod_row, pm, pinv, wp, ps, wo, gpm, gpf, wr):
    b, t, _ = x.shape
    tm = min(t, 2 * TOKEN_TILE)
    bs = 2 * TOKEN_TILE // tm if mod_row(1) == mod_row(0) else 1
    const = lambda *shape: pl.BlockSpec(shape, lambda i, j: (0,) * len(shape))
    tok = lambda w: pl.BlockSpec((bs, tm, w), lambda i, j: (i, j, 0))
    sd = jax.ShapeDtypeStruct
    return pl.pallas_call(
        _outproj_kernel,
        grid=(b // bs, t // tm),
        in_specs=[tok(D_MODEL),
                  pl.BlockSpec((bs, t, 512), lambda i, j: (i, 0, 0)),
                  tok(512), tok(512), tok(512),
                  pl.BlockSpec((1, N_MOD, D_MODEL), lambda i, j: (mod_row(i * bs), 0, 0)),
                  pl.BlockSpec((POOL_GROUPS, tm, t), lambda i, j: (0, j, 0)),
                  pl.BlockSpec((POOL_GROUPS, tm, 1), lambda i, j: (0, j, 0)),
                  const(POOL_GROUPS // 2, 2 * POOL_GROUP_DIM, 2 * POOL_GROUP_DIM), const(1, POOL_WIDTH),
                  const(D_MODEL, D_MODEL), const(1, D_MODEL), const(1, D_MODEL),
                  const(D_MODEL, 2 * LANES)],
        out_specs=[tok(D_MODEL), tok(D_MODEL),
                   pl.BlockSpec((bs, tm, D_MODEL // LANES, LANES), lambda i, j: (i, j, 0, 0)),
                   pl.BlockSpec((N_EXPERTS, bs * tm), lambda i, j: (0, i * (t // tm) + j))],
        out_shape=[sd((b, t, D_MODEL), F32), sd((b, t, D_MODEL), BF16),
                   sd((b, t, D_MODEL // LANES, LANES), BF16), sd((N_EXPERTS, b * t), F32)],
        compiler_params=_params("arbitrary", "arbitrary"),
        name="outproj",
    )(x, u, hf, hb, o, mod, pm, pinv, wp, ps, wo, gpm, gpf, wr)


def _router_kernel(lg_ref, br_ref, comb_ref, sel_ref, *, n_blocks):
    ng, gs = N_EXPERT_GROUPS, GROUP_SIZE
    neg = -jnp.inf
    lg = jnp.swapaxes(lg_ref[...], 0, 1)
    br = br_ref[...]
    s = [jax.nn.sigmoid(lg[j]) for j in range(gs)]
    biased = [s[j] + br[:, j, :] for j in range(gs)]
    fold = lambda op, xs: functools.reduce(op, xs)
    m1 = fold(jnp.maximum, biased)
    i1 = fold(jnp.minimum, [jnp.where(biased[j] == m1, j, gs) for j in range(gs)])
    m2 = fold(jnp.maximum, [jnp.where(i1 == j, neg, biased[j]) for j in range(gs)])
    cur = m1 + m2
    gi = lax.broadcasted_iota(jnp.int32, cur.shape, 0)
    gmask = jnp.zeros(cur.shape, F32)
    for _ in range(TOPK_GROUPS):
        mx = jnp.max(cur, axis=0, keepdims=True)
        ix = jnp.min(jnp.where(cur == mx, gi, ng), axis=0, keepdims=True)
        hit = gi == ix
        gmask = jnp.where(hit, 1.0, gmask)
        cur = jnp.where(hit, neg, cur)
    cand = [jnp.where(gmask > 0, biased[j], neg) for j in range(gs)]
    eidx = [gi * gs + j for j in range(gs)]
    selm = [jnp.zeros(cur.shape, F32) for _ in range(gs)]
    for _ in range(TOP_K):
        mx = jnp.max(fold(jnp.maximum, cand), axis=0, keepdims=True)
        ix = jnp.min(fold(jnp.minimum, [jnp.where(cand[j] == mx, eidx[j], N_EXPERTS) for j in range(gs)]),
                     axis=0, keepdims=True)
        for j in range(gs):
            hit = eidx[j] == ix
            selm[j] = jnp.where(hit, 1.0, selm[j])
            cand[j] = jnp.where(hit, neg, cand[j])
    sel = [selm[j] * s[j] for j in range(gs)]
    tot = jnp.sum(fold(jnp.add, sel), axis=0, keepdims=True)
    comb = [sel[j] / tot * ROUTED_SCALE for j in range(gs)]
    sel_ref[...] = jnp.swapaxes(jnp.stack(selm, axis=0), 0, 1)
    comb_e = jnp.swapaxes(jnp.stack(comb, axis=0), 0, 1).reshape(N_EXPERTS, -1)
    comb_t = jnp.concatenate([comb_e, jnp.zeros_like(comb_e)], axis=0).T
    comb_ref[...] = jnp.where(pl.program_id(0) < n_blocks, comb_t, 0.0)


def _router(logits_t, b_router):
    t = logits_t.shape[1]
    tl = 1024
    nb = t // tl
    shp = (N_EXPERT_GROUPS, GROUP_SIZE, t)
    blk = pl.BlockSpec((N_EXPERT_GROUPS, GROUP_SIZE, tl), lambda j: (0, 0, jnp.minimum(j, nb - 1)))
    comb, sel = pl.pallas_call(
        functools.partial(_router_kernel, n_blocks=nb),
        grid=(nb + 1,),
        in_specs=[blk, pl.BlockSpec((N_EXPERT_GROUPS, GROUP_SIZE, 1), lambda j: (0, 0, 0))],
        out_specs=[pl.BlockSpec((tl, LANES), lambda j: (j, 0)), blk],
        out_shape=[jax.ShapeDtypeStruct((t + tl, LANES), F32), jax.ShapeDtypeStruct(shp, F32)],
        compiler_params=_params("arbitrary"),
        name="router",
    )(logits_t.reshape(shp), b_router.reshape(N_EXPERT_GROUPS, GROUP_SIZE, 1))
    return comb, sel.reshape(N_EXPERTS, t)


def _plan_kernel(sel_ref, pos_ref, meta_ref, *, n_meta):
    t = sel_ref.shape[1]
    tm = float(MOE_TILE)
    sel = sel_ref[...]
    selb = sel.astype(BF16)
    blk = 256
    rr = lax.broadcasted_iota(jnp.int32, (blk, blk), 0)
    cc = lax.broadcasted_iota(jnp.int32, (blk, blk), 1)
    before = (rr < cc).astype(BF16)
    carry = jnp.zeros((N_EXPERTS, 1), F32)
    ranks = []
    for b in range(t // blk):
        sb = selb[:, b * blk:(b + 1) * blk]
        ranks.append(_dot(sb, before) + carry)
        carry = carry + jnp.sum(sel[:, b * blk:(b + 1) * blk], axis=1, keepdims=True)
    rank = jnp.concatenate(ranks, axis=1)
    cnt = carry
    ntile = jnp.floor((cnt + (tm - 0.5)) * (1.0 / tm))
    er = lax.broadcasted_iota(jnp.int32, (N_EXPERTS, N_EXPERTS), 0)
    ec = lax.broadcasted_iota(jnp.int32, (N_EXPERTS, N_EXPERTS), 1)
    below = (ec < er).astype(BF16)
    tstart = _dot(below, jnp.broadcast_to(ntile, (N_EXPERTS, LANES)).astype(BF16))[:, 0:1]
    pos = tstart * tm + rank
    erank = _dot(below, selb)
    rows = []
    for k in range(TOP_K):
        hit = (sel > 0.0) & (erank == float(k))
        rows.append(jnp.sum(jnp.where(hit, pos, 0.0), axis=0, keepdims=True))
    rows += [jnp.zeros((1, t), F32)] * (8 - TOP_K)
    pos_ref[...] = jnp.concatenate(rows, axis=0).astype(jnp.int32)

    tau = lax.broadcasted_iota(jnp.int32, (N_EXPERTS, n_meta), 1).astype(F32)
    eidx = lax.broadcasted_iota(jnp.int32, (N_EXPERTS, n_meta), 0).astype(F32)
    te = jnp.sum(((tstart + ntile) <= tau).astype(F32), axis=0, keepdims=True)
    te = jnp.minimum(te, float(N_EXPERTS - 1))
    onehot = eidx == te
    cnt_t = jnp.sum(jnp.where(onehot, cnt, 0.0), axis=0, keepdims=True)
    ts_t = jnp.sum(jnp.where(onehot, tstart, 0.0), axis=0, keepdims=True)
    tr = jnp.clip(cnt_t - (tau[0:1] - ts_t) * tm, 0.0, tm)
    tf = jnp.where((tau[0:1] == ts_t) & (tr > 0.0), 1.0, 0.0)
    meta_ref[...] = jnp.concatenate([te, tr, tf] + [jnp.zeros((1, n_meta), F32)] * 5, axis=0).astype(jnp.int32)


def _plan(sel, n_meta):
    t = sel.shape[1]
    sd = jax.ShapeDtypeStruct
    return pl.pallas_call(
        functools.partial(_plan_kernel, n_meta=n_meta),
        out_shape=[sd((8, t), jnp.int32), sd((8, n_meta), jnp.int32)],
        compiler_params=pltpu.CompilerParams(vmem_limit_bytes=VMEM_LIMIT),
        name="plan",
    )(sel)


def _slot_table(pos, n_slots, n_tokens):
    t = pos.shape[1]
    workers = SC_CORES * SC_SUBCORES
    per = -(-n_slots // (workers * SC_LANES)) * SC_LANES
    mesh = plsc.VectorSubcoreMesh(core_axis_name="core", subcore_axis_name="subcore")

    def body(pos_hbm, out_hbm, pos_v, tbl_v):
        lo = (lax.axis_index("core") * SC_SUBCORES + lax.axis_index("subcore")) * per
        pltpu.sync_copy(pos_hbm.at[pl.ds(0, TOP_K * t)], pos_v)
        dummy = jnp.full((SC_LANES,), n_tokens, jnp.int32)

        @plsc.parallel_loop(0, per, SC_LANES, unroll=8)
        def _(i):
            tbl_v[pl.ds(i, SC_LANES)] = dummy

        lane = lax.iota(jnp.int32, SC_LANES)
        for k in range(TOP_K):
            @plsc.parallel_loop(0, t, SC_LANES, unroll=8)
            def _(i):
                p = pos_v[pl.ds(k * t + i, SC_LANES)] - lo
                mine = (p >= 0) & (p < per)
                plsc.store_scatter(tbl_v, [jnp.where(mine, p, 0)], lane + i, mask=mine)
        pltpu.sync_copy(tbl_v, out_hbm.at[pl.ds(lo, per)])

    build = pl.kernel(body, out_type=jax.ShapeDtypeStruct((workers * per,), jnp.int32), mesh=mesh,
                      scratch_types=[pltpu.VMEM((TOP_K * t,), jnp.int32), pltpu.VMEM((per,), jnp.int32)],
                      compiler_params=pltpu.CompilerParams(needs_layout_passes=False), name="slot_table")
    return build(pos.reshape(-1))


def _experts_kernel(te_ref, tr_ref, xc_ref, xs_ref, slots_ref, comb_ref, wgu_ref, wd_ref, acc_out,
                    tbl_ref, xbuf, acc, stage0, stage1, cst0, cst1, act0, act1, ybuf0, ybuf1, sems):
    s = pl.program_id(0)
    n_tiles = te_ref.shape[0]
    tm = MOE_TILE
    nch = D_MODEL // LANES
    tile_at = lambda lag: jnp.clip(s - lag, 0, n_tiles - 1)
    t_g, t_1, t_2, t_3 = tile_at(0), tile_at(1), tile_at(2), tile_at(3)

    @pl.when(s == 0)
    def _():
        tc, ts = xc_ref.shape[0], xs_ref.shape[0]
        copies = (pltpu.make_async_copy(xc_ref, xbuf.at[pl.ds(0, tc)], sems.at[0]),
                  pltpu.make_async_copy(xs_ref, xbuf.at[pl.ds(tc, ts)], sems.at[1]))
        tbl_init = pltpu.make_async_copy(slots_ref, tbl_ref, sems.at[3])
        tbl_init.start()
        for cp in copies:
            cp.start()
        n_pad = xbuf.shape[0] - tc - ts
        xbuf[pl.ds(tc + ts, n_pad)] = jnp.zeros((n_pad,) + xbuf.shape[1:], BF16)
        for ref in (acc, stage0, stage1, cst0, cst1, act0, act1, ybuf0, ybuf1):
            ref[...] = jnp.zeros_like(ref)
        tbl_init.wait()
        for cp in copies:
            cp.wait()

    def gather(tile, stage, cst):
        base = tile * tm
        for j in range(tm):
            tok = tbl_ref[base + j]
            stage[pl.ds(j * nch, nch), :] = xbuf[tok].astype(F32)
            cst[pl.ds(j, 1), :] = comb_ref[pl.ds(tok, 1), :]

    def gate_up(tile, stage, cst, act):
        xb = _token_rows_from_slabs(
            lambda g: stage[g * 8 * nch:(g + 1) * 8 * nch, :].reshape(8, nch, LANES), tm).astype(BF16)
        gu = _dot(xb, wgu_ref[0].astype(BF16))
        lane = lax.broadcasted_iota(jnp.int32, (1, LANES), 1)
        w_col = jnp.sum(jnp.where(lane == te_ref[tile], cst[...], 0.0), axis=1, keepdims=True)
        act[...] = (_silu(gu[:, :EXPERT_DIM]) * gu[:, EXPERT_DIM:] * w_col).astype(BF16)

    def down(act, ybuf):
        y = _dot(act[...], wd_ref[0].astype(BF16))
        for cc in range(nch):
            ybuf[cc * Y_PITCH:cc * Y_PITCH + tm, :] = y[:, cc * LANES:(cc + 1) * LANES]

    def scatter(tile, ybuf):
        base = tile * tm
        sc_n = 16
        for i in range(tm // sc_n):
            toks = [tbl_ref[base + i * sc_n + u] for u in range(sc_n)]
            olds = [acc[toks[u]] for u in range(sc_n)]
            news = [olds[u] + ybuf[pl.ds(i * sc_n + u, nch, stride=Y_PITCH), :] for u in range(sc_n)]
            for u in range(sc_n):
                acc[toks[u]] = news[u]

    busy = (tr_ref[t_g] + tr_ref[t_1] + tr_ref[t_2] + tr_ref[t_3]) > 0
    bufs = ((stage0, cst0, act0, ybuf0), (stage1, cst1, act1, ybuf1))
    for par in range(2):
        stage_p, cst_p, act_p, ybuf_p = bufs[par]
        stage_q, cst_q, act_q, ybuf_q = bufs[1 - par]

        @pl.when(busy & (s % 2 == par))
        def _():
            gather(t_g, stage_p, cst_p)
            gate_up(t_1, stage_q, cst_q, act_q)
            down(act_p, ybuf_p)
            scatter(t_3, ybuf_q)

    @pl.when(s == pl.num_programs(0) - 1)
    def _():
        cp = pltpu.make_async_copy(acc, acc_out, sems.at[2])
        cp.start()
        cp.wait()


def _experts(te, tr, xc, xs, slots, comb, wgu, wd):
    n_tok = xc.shape[0] + xs.shape[0] + 8
    n_tiles = te.shape[0]
    tm = MOE_TILE
    nch = D_MODEL // LANES
    vm = pltpu.VMEM
    grid_spec = pltpu.PrefetchScalarGridSpec(
        num_scalar_prefetch=2,
        grid=(n_tiles + 3,),
        in_specs=[pl.BlockSpec(memory_space=pl.ANY), pl.BlockSpec(memory_space=pl.ANY),
                  pl.BlockSpec(memory_space=pl.ANY),
                  pl.BlockSpec((n_tok, LANES), lambda s, *_: (0, 0), pipeline_mode=pl.Buffered(1)),
                  pl.BlockSpec((1, D_MODEL, 2 * EXPERT_DIM), lambda s, te, *_: (te[jnp.clip(s - 1, 0, n_tiles - 1)], 0, 0)),
                  pl.BlockSpec((1, EXPERT_DIM, D_MODEL), lambda s, te, *_: (te[jnp.clip(s - 2, 0, n_tiles - 1)], 0, 0))],
        out_specs=pl.BlockSpec(memory_space=pl.ANY),
        scratch_shapes=[pltpu.SMEM(slots.shape, jnp.int32),
                        vm((n_tok, nch, LANES), BF16), vm((n_tok, nch, LANES), F32),
                        vm((tm * nch, LANES), F32), vm((tm * nch, LANES), F32),
                        vm((tm, LANES), F32), vm((tm, LANES), F32),
                        vm((tm, EXPERT_DIM), BF16), vm((tm, EXPERT_DIM), BF16),
                        vm((nch * Y_PITCH, LANES), F32), vm((nch * Y_PITCH, LANES), F32),
                        pltpu.SemaphoreType.DMA((4,))],
    )
    return pl.pallas_call(
        _experts_kernel,
        grid_spec=grid_spec,
        out_shape=jax.ShapeDtypeStruct((n_tok, nch, LANES), F32),
        compiler_params=pltpu.CompilerParams(dimension_semantics=("arbitrary",),
                                             vmem_limit_bytes=EXPERTS_VMEM_LIMIT),
        name="experts",
    )(te, tr, xc, xs, slots, comb, wgu, wd)


def _final_kernel(acc_ref, h_ref, x1_ref, mod_ref, wsg_ref, wsd_ref, gpo_ref, out_ref):
    routed = _token_rows_from_slabs(lambda g: acc_ref[g * 8:(g + 1) * 8], acc_ref.shape[0])
    gs = _dot(h_ref[...], wsg_ref[...])
    act = _silu(gs[:, :SHARED_DIM]) * gs[:, SHARED_DIM:]
    f = routed + _dot(act.astype(BF16), wsd_ref[...])
    out_ref[...] = x1_ref[...] + mod_ref[0][5:6] * _rmsnorm(f, gpo_ref[...])


def _final(acc, tile0, h2, x1, mod, mod_row, wsg, wsd, gpo):
    n = h2.shape[0]
    tm = FINAL_TILE
    const = lambda *shape: pl.BlockSpec(shape, lambda i: (0,) * len(shape))
    tok = lambda w: pl.BlockSpec((tm, w), lambda i: (i, 0))
    return pl.pallas_call(
        _final_kernel,
        grid=(n // tm,),
        in_specs=[pl.BlockSpec((tm, D_MODEL // LANES, LANES), lambda i: (i + tile0, 0, 0)),
                  tok(D_MODEL), tok(D_MODEL),
                  pl.BlockSpec((1, N_MOD, D_MODEL), lambda i: (mod_row(i), 0, 0)),
                  const(D_MODEL, 2 * SHARED_DIM), const(SHARED_DIM, D_MODEL), const(1, D_MODEL)],
        out_specs=tok(D_MODEL),
        out_shape=jax.ShapeDtypeStruct((n, D_MODEL), F32),
        compiler_params=_params("arbitrary"),
        name="final",
    )(acc, h2, x1, mod, wsg, wsd, gpo)


def _window_bounds(n, w):
    idx = np.arange(n)
    return np.clip(idx - w // 2, 0, n), np.clip(idx + w - w // 2, 0, n)


def _pool_operators(t, grid):
    mats, invs = [], []
    for w in POOL_WINDOWS:
        if grid:
            rlo, rhi = _window_bounds(t // GRID_W, w)
            clo, chi = _window_bounds(GRID_W, w)
            r = np.arange(t) // GRID_W
            c = np.arange(t) % GRID_W
            m = ((r[None, :] >= rlo[r][:, None]) & (r[None, :] < rhi[r][:, None])
                 & (c[None, :] >= clo[c][:, None]) & (c[None, :] < chi[c][:, None]))
            cnt = (rhi - rlo)[r] * (chi - clo)[c]
        else:
            lo, hi = _window_bounds(t, w)
            sidx = np.arange(t)
            m = (sidx[None, :] >= lo[:, None]) & (sidx[None, :] < hi[:, None])
            cnt = hi - lo
        mats.append(m.astype(np.float32))
        invs.append((1.0 / cnt.astype(np.float64)).astype(np.float32)[:, None])
    return jnp.asarray(np.stack(mats), BF16), jnp.asarray(np.stack(invs), F32)


def kernel(x_prompt, x_sample, state_C, state_n, state_m, c, c_ctx, w_ada, b_ada, g_pre_mix, w_in, b_gate,
           w_pool, pool_scale, w_out, g_post_mix, g_pre_ffn, w_router, b_router, w_expert_gu, w_expert_down,
           w_shared_gu, w_shared_down, g_post_ffn):
    b_ctx = x_prompt.shape[0]
    b_lat = x_sample.shape[0]
    nu = N_DIR * HEADS
    l = 0
    row = lambda a: a[l].reshape(1, -1).astype(F32)

    cvec = jnp.zeros((16, D_MODEL), F32).at[0].set(c_ctx.astype(F32)).at[1:1 + b_lat].set(c.astype(F32))
    mod = _mod_rows(cvec, w_ada[l], b_ada[l]).reshape(16, N_MOD, D_MODEL)

    w_in_l = w_in[l]
    p0 = POOL_WIDTH
    mw = MLSTM_WIDTH
    w_u, w_q, w_k, w_v, w_o = (w_in_l[:, lo:lo + 512] for lo in (0, p0, p0 + mw, p0 + 2 * mw, p0 + 3 * mw))
    wm = jnp.concatenate([w_u, w_k, w_o], axis=1).astype(BF16)
    wt = jnp.concatenate([w_q.T, w_v.T], axis=0).astype(BF16)
    wg_cols = w_in_l[:, p0 + 4 * mw:]
    wg = jnp.pad(jnp.concatenate(_split2(wg_cols), axis=1), ((0, 0), (0, LANES - 2 * GATE_COLS)))
    bg = b_gate[l].reshape(GATE_COLS).astype(F32)
    bgr = jnp.pad(bg, (0, LANES - GATE_COLS)).reshape(1, LANES)
    wpl = w_pool[l].astype(BF16)
    zg = jnp.zeros((POOL_GROUP_DIM, POOL_GROUP_DIM), BF16)
    wp = jnp.stack([jnp.block([[wpl[2 * p], zg], [zg, wpl[2 * p + 1]]]) for p in range(POOL_GROUPS // 2)])
    wo = w_out[l].astype(BF16)
    wr = jnp.pad(jnp.concatenate(_split3(w_router[l].astype(F32)), axis=1), ((0, 0), (0, 2 * LANES - 3 * N_EXPERTS)))
    wsg = w_shared_gu[l].astype(BF16)
    wsd = w_shared_down[l].astype(BF16)

    def mixer(x, mod_row, grid, s0, m0, emit_state):
        t = x.shape[1]
        u, k, o, qt, vt, gate, gate_t = _inproj(x.astype(F32), mod, mod_row, row(g_pre_mix), wm, wt, wg, bgr)
        outs = _mlstm(k, qt, vt, gate, gate_t, s0, m0, emit_state)
        hf, hb = outs[0], outs[1]
        pm, pinv = _pool_operators(t, grid)
        x1, h2, xp, lg = _outproj(x.astype(F32), u, hf, hb, o, mod, mod_row, pm, pinv, wp, row(pool_scale), wo,
                                  row(g_post_mix), row(g_pre_ffn), wr)
        return x1, h2, xp, lg, outs[2:]

    ctx_row = lambda i: 0
    lat_row = lambda i: i + 1
    x1c, h2c, xpc, lgc, (c_new, n_new, m_new) = mixer(x_prompt, ctx_row, False, None, None, True)
    s0 = jnp.concatenate(
        [jnp.swapaxes(state_C[:, l].reshape(b_lat, nu, HEAD_DIM, HEAD_DIM).astype(F32), -1, -2),
         jnp.broadcast_to(state_n[:, l].reshape(b_lat, nu, 1, HEAD_DIM).astype(F32),
                          (b_lat, nu, N_ROWS, HEAD_DIM))], axis=-2)
    m0 = jnp.broadcast_to(state_m[:, l].reshape(b_lat, nu, 1, 1).astype(F32), (b_lat, nu, 1, LANES))
    x1s, h2s, xps, lgs, _ = mixer(x_sample, lat_row, True, s0, m0, False)

    tc = b_ctx * x_prompt.shape[1]
    ts = b_lat * x_sample.shape[1]
    n_tok = tc + ts
    lg_all = jnp.concatenate([lgc, lgs], axis=1)
    comb_tok, sel = _router(lg_all, b_router[l].astype(F32))
    n_tiles = n_tok * TOP_K // MOE_TILE + N_EXPERTS
    n_meta = -(-n_tiles // LANES) * LANES
    pos, meta = _plan(sel, n_meta)

    slab = (D_MODEL // LANES, LANES)
    acc = _experts(meta[0, :n_tiles], meta[1, :n_tiles],
                   xpc.reshape((tc,) + slab), xps.reshape((ts,) + slab),
                   _slot_table(pos, n_tiles * MOE_TILE, n_tok), comb_tok,
                   w_expert_gu[l], w_expert_down[l])

    fin = functools.partial(_final, wsg=wsg, wsd=wsd, gpo=row(g_post_ffn))
    tiles_per_lat = x_sample.shape[1] // FINAL_TILE
    yc = fin(acc, 0, h2c.reshape(tc, D_MODEL), x1c.reshape(tc, D_MODEL), mod, ctx_row)
    ys = fin(acc, tc // FINAL_TILE, h2s.reshape(ts, D_MODEL), x1s.reshape(ts, D_MODEL), mod,
             lambda i: i // tiles_per_lat + 1)

    new_c = c_new.reshape(b_ctx, 1, N_DIR, HEADS, HEAD_DIM, HEAD_DIM)
    new_n = n_new.reshape(b_ctx, 1, N_DIR, HEADS, HEAD_DIM)
    new_m = m_new[..., 0].reshape(b_ctx, 1, N_DIR, HEADS)
    return (yc.reshape(x_prompt.shape), ys.reshape(x_sample.shape), new_c, new_n, new_m)
```

```python
import functools

import jax
import jax.numpy as jnp
import numpy as np
from jax import lax
from jax.experimental import pallas as pl
from jax.experimental.pallas import tpu as pltpu
from jax.experimental.pallas import tpu_sc as plsc

F32 = jnp.float32
BF16 = jnp.bfloat16

D_MODEL = 1024
GRID_W = 64
POOL_WIDTH = 512
POOL_GROUPS = 4
POOL_GROUP_DIM = 128
POOL_WINDOWS = (2, 4, 8, 16)
HEADS = 4
HEAD_DIM = 128
MLSTM_WIDTH = HEADS * HEAD_DIM
N_DIR = 2
GATE_COLS = N_DIR * 2 * HEADS
N_EXPERTS = 64
TOP_K = 6
N_EXPERT_GROUPS = 8
GROUP_SIZE = N_EXPERTS // N_EXPERT_GROUPS
TOPK_GROUPS = 4
EXPERT_DIM = 256
SHARED_DIM = 256
ROUTED_SCALE = 2.5
N_MOD = 6
EPS = 1e-6
K_SCALE = HEAD_DIM ** -0.5

LANES = 128
CHUNK = 256
N_ROWS = 16
TOKEN_TILE = 256
FINAL_TILE = 512
MOE_TILE = 320
Y_PITCH = MOE_TILE + 8
VMEM_LIMIT = 56 * 1024 * 1024
EXPERTS_VMEM_LIMIT = 58 * 1024 * 1024
SC_CORES = 2
SC_SUBCORES = 16
SC_LANES = 16


def _split3(x):
    p1 = x.astype(BF16)
    r1 = x - p1.astype(F32)
    p2 = r1.astype(BF16)
    p3 = (r1 - p2.astype(F32)).astype(BF16)
    return p1, p2, p3


def _split2(x):
    p1 = x.astype(BF16)
    p2 = (x - p1.astype(F32)).astype(BF16)
    return p1, p2


def _dot(a, b):
    return jnp.dot(a, b, preferred_element_type=F32)


def _dot_nt(a, b):
    return lax.dot_general(a, b, (((1,), (1,)), ((), ())), preferred_element_type=F32)


def _rmsnorm(x, g):
    return x * lax.rsqrt(jnp.mean(x * x, axis=-1, keepdims=True) + EPS) * g


def _silu(x):
    return x * jax.nn.sigmoid(x)


def _token_rows_from_slabs(read_block, n_tok):
    nch = D_MODEL // LANES
    cols = [[] for _ in range(nch)]
    for g in range(n_tok // 8):
        blk = jnp.swapaxes(read_block(g), 0, 1)
        for cc in range(nch):
            cols[cc].append(blk[cc])
    return jnp.concatenate([jnp.concatenate(c, axis=0) for c in cols], axis=1)


def _params(*sem):
    return pltpu.CompilerParams(dimension_semantics=sem, vmem_limit_bytes=VMEM_LIMIT)


def _mod_kernel(c_ref, w_ref, b_ref, o_ref):
    a = _silu(c_ref[...])
    a_stack = jnp.concatenate(_split3(a), axis=0)
    w1, w2 = _split2(w_ref[...])
    r1 = _dot(a_stack, w1)
    r2 = _dot(a_stack[:32], w2)
    o_ref[...] = (r1[0:16] + r1[16:32] + r1[32:48] + r2[0:16] + r2[16:32]) + b_ref[...]


def _mod_rows(cvec, w_ada, b_ada):
    n = N_MOD * D_MODEL
    tn = 1536
    return pl.pallas_call(
        _mod_kernel,
        grid=(n // tn,),
        in_specs=[pl.BlockSpec((16, D_MODEL), lambda j: (0, 0)),
                  pl.BlockSpec((D_MODEL, tn), lambda j: (0, j)),
                  pl.BlockSpec((1, tn), lambda j: (0, j))],
        out_specs=pl.BlockSpec((16, tn), lambda j: (0, j)),
        out_shape=jax.ShapeDtypeStruct((16, n), F32),
        compiler_params=_params("arbitrary"),
        name="mod",
    )(cvec, w_ada, b_ada.reshape(1, n))


def _inproj_kernel(x_ref, mod_ref, g_ref, wm_ref, wt_ref, wg_ref, bgr_ref,
                   u_ref, k_ref, o_ref, qt_ref, vt_ref, gate_ref, gatet_ref):
    bs, tm, _ = x_ref.shape
    rows = bs * tm
    x = x_ref[...].reshape(rows, D_MODEL)
    mod = mod_ref[0]
    h = _rmsnorm(x, g_ref[...]) * (1.0 + mod[1:2]) + mod[0:1]
    h1, h2, h3 = _split3(h)
    z = _dot(h1, wm_ref[...])
    u_ref[...] = z[:, 0:512].astype(BF16).reshape(bs, tm, 512)
    k_ref[...] = (z[:, 512:1024] * K_SCALE).astype(BF16).reshape(bs, tm, 512)
    o_ref[...] = z[:, 1024:1536].astype(BF16).reshape(bs, tm, 512)
    zt = _dot_nt(wt_ref[...], h1).astype(BF16)
    r = _dot(jnp.concatenate([h1, h2, h3], axis=0), wg_ref[...])
    r12 = r[0:rows] + r[rows:2 * rows]
    gate = (r12 + r[2 * rows:]) + pltpu.roll(r12, LANES - GATE_COLS, axis=1) + bgr_ref[...]
    gate_ref[...] = gate.reshape(bs, tm, LANES)
    gate_t = gate.T
    for bb in range(bs):
        cols = slice(bb * tm, (bb + 1) * tm)
        qt_ref[bb] = zt[0:512, cols]
        vt_ref[bb] = zt[512:1024, cols]
        gatet_ref[bb] = gate_t[0:16, cols]


def _inproj(x, mod, mod_row, g, wm, wt, wg, bgr):
    b, t, _ = x.shape
    tm = min(t, 2 * TOKEN_TILE)
    bs = 2 * TOKEN_TILE // tm if mod_row(1) == mod_row(0) else 1
    const = lambda *shape: pl.BlockSpec(shape, lambda i, j: (0,) * len(shape))
    tok = lambda w: pl.BlockSpec((bs, tm, w), lambda i, j: (i, j, 0))
    tok_t = lambda r: pl.BlockSpec((bs, r, tm), lambda i, j: (i, 0, j))
    sd = jax.ShapeDtypeStruct
    return pl.pallas_call(
        _inproj_kernel,
        grid=(b // bs, t // tm),
        in_specs=[tok(D_MODEL),
                  pl.BlockSpec((1, N_MOD, D_MODEL), lambda i, j: (mod_row(i * bs), 0, 0)),
                  const(1, D_MODEL), const(D_MODEL, 1536), const(1024, D_MODEL),
                  const(D_MODEL, LANES), const(1, LANES)],
        out_specs=[tok(512), tok(512), tok(512), tok_t(512), tok_t(512), tok(LANES), tok_t(16)],
        out_shape=[sd((b, t, 512), BF16), sd((b, t, 512), BF16), sd((b, t, 512), BF16),
                   sd((b, 512, t), BF16), sd((b, 512, t), BF16), sd((b, t, LANES), F32),
                   sd((b, 16, t), F32)],
        compiler_params=_params("arbitrary", "arbitrary"),
        name="inproj",
    )(x, mod, g, wm, wt, wg, bgr)


def _log_sigmoid(x):
    return jnp.minimum(x, 0.0) - jnp.log1p(jnp.exp(-jnp.abs(x)))


def _scan_unit(st, k, qt, vt, u_col, u_row, b_row, btot, mask, s_prev, m_prev, use_state):
    dh = HEAD_DIM
    n = st.shape[0]
    ub = jnp.where(mask, jnp.broadcast_to(u_col, (n, n)), -jnp.inf)
    z = jnp.maximum(m_prev, jnp.max(ub, axis=0, keepdims=True))
    p = (jnp.exp(ub - z) * st).astype(BF16)
    ones = jnp.ones((N_ROWS, n), BF16)
    tot = _dot(jnp.concatenate([vt, ones], axis=0), p)
    if use_state:
        tot = tot + jnp.exp(m_prev - z) * _dot(s_prev.astype(BF16), qt)
    floor = jnp.exp(-(b_row + z))
    h_t = tot[:dh] / jnp.maximum(jnp.abs(tot[dh:dh + 1]), floor)
    g_row = btot + u_row
    m_new = jnp.maximum(btot + m_prev, jnp.max(g_row, axis=-1, keepdims=True))
    w_row = jnp.exp(g_row - m_new)
    vw = jnp.concatenate([(vt.astype(F32) * w_row).astype(BF16),
                          jnp.broadcast_to(w_row, (N_ROWS, n)).astype(BF16)], axis=0)
    s_new = jnp.exp(btot + m_prev - m_new) * s_prev + _dot(vw, k)
    return h_t.T, s_new, m_new


def _mlstm_kernel(*refs, nc, zero_init, emit_state):
    it = iter(refs)
    fwd_refs = tuple(next(it) for _ in range(5))
    bwd_refs = tuple(next(it) for _ in range(5)) if nc > 1 else fwd_refs
    if not zero_init:
        s0_ref, m0_ref = next(it), next(it)
    h_refs = (next(it), next(it))
    if emit_state:
        c_out, n_out, m_out = next(it), next(it), next(it)
    s_scr, m_scr = next(it), next(it)

    j = pl.program_id(1)
    n = CHUNK
    dh = HEAD_DIM

    @pl.when(j == 0)
    def _():
        if zero_init:
            s_scr[...] = jnp.zeros_like(s_scr)
            m_scr[...] = jnp.zeros_like(m_scr)
        else:
            s_scr[...] = s0_ref[0]
            m_scr[...] = m0_ref[0]

    rows = lax.broadcasted_iota(jnp.int32, (n, n), 0)
    cols = lax.broadcasted_iota(jnp.int32, (n, n), 1)
    le = rows <= cols
    ge = rows >= cols
    tri_le = le.astype(BF16)
    tri_ge = ge.astype(BF16)
    use_state = not (zero_init and nc == 1)

    def gate_terms(d):
        g_ref, gt_ref = (fwd_refs, bwd_refs)[d][3:5]
        gate = g_ref[0]
        gate_t = gt_ref[0]
        lf = _log_sigmoid(gate)
        lf_t = _log_sigmoid(gate_t)
        tri_c, tri_r = (tri_ge, tri_le) if d == 0 else (tri_le, tri_ge)
        bc = _dot(tri_c, jnp.concatenate(_split3(lf), axis=1))
        b_cols = bc[:, 0:128] + bc[:, 128:256] + bc[:, 256:384]
        br = _dot(jnp.concatenate(_split3(lf_t), axis=0), tri_r)
        b_rows = br[0:16] + br[16:32] + br[32:48]
        return gate, gate_t, b_cols, b_rows, jnp.sum(lf_t, axis=-1, keepdims=True)

    terms = [gate_terms(0), gate_terms(1)]
    hs = ([], [])
    for hd in range(HEADS):
        hsl = slice(hd * dh, (hd + 1) * dh)
        st = None
        for d in range(N_DIR):
            k_ref, qt_ref, vt_ref = (fwd_refs, bwd_refs)[d][0:3]
            gate, gate_t, b_cols, b_rows, tot_rows = terms[d]
            ci = d * 8 + hd
            cf = d * 8 + 4 + hd
            unit = d * HEADS + hd
            k = k_ref[0, :, hsl]
            qt = qt_ref[0, hsl, :]
            if st is None or nc > 1:
                st = _dot(k, qt)
            mask = le if d == 0 else ge
            h, s_new, m_new = _scan_unit(
                st, k, qt, vt_ref[0, hsl, :],
                gate[:, ci:ci + 1] - b_cols[:, cf:cf + 1],
                gate_t[ci:ci + 1, :] - b_rows[cf:cf + 1, :],
                b_rows[cf:cf + 1, :], tot_rows[cf:cf + 1, :],
                mask, s_scr[unit], m_scr[unit][:, 0:1], use_state)
            s_scr[unit] = s_new
            m_scr[unit] = jnp.broadcast_to(m_new, (1, LANES))
            hs[d].append(h)
    for d in range(N_DIR):
        h_refs[d][0] = jnp.concatenate(hs[d], axis=1).astype(BF16)

    if emit_state:
        @pl.when(j == nc - 1)
        def _():
            for unit in range(N_DIR * HEADS):
                s = s_scr[unit]
                c_out[0, unit] = s[:dh].T
                n_out[0, unit] = s[dh:dh + 1]
                m_out[0, unit] = m_scr[unit]


def _mlstm(k, qt, vt, gate, gate_t, s0, m0, emit_state):
    b, t, _ = k.shape
    nc = t // CHUNK
    zero_init = s0 is None
    nu = N_DIR * HEADS
    fwd = lambda w: pl.BlockSpec((1, CHUNK, w), lambda i, j: (i, j, 0))
    bwd = lambda w: pl.BlockSpec((1, CHUNK, w), lambda i, j: (i, nc - 1 - j, 0))
    fwd_t = lambda r: pl.BlockSpec((1, r, CHUNK), lambda i, j: (i, 0, j))
    bwd_t = lambda r: pl.BlockSpec((1, r, CHUNK), lambda i, j: (i, 0, nc - 1 - j))
    args = [k, qt, vt, gate, gate_t]
    in_specs = [fwd(512), fwd_t(512), fwd_t(512), fwd(LANES), fwd_t(16)]
    if nc > 1:
        args += [k, qt, vt, gate, gate_t]
        in_specs += [bwd(512), bwd_t(512), bwd_t(512), bwd(LANES), bwd_t(16)]
    if not zero_init:
        args += [s0, m0]
        in_specs += [pl.BlockSpec((1, nu, HEAD_DIM + N_ROWS, HEAD_DIM), lambda i, j: (i, 0, 0, 0)),
                     pl.BlockSpec((1, nu, 1, LANES), lambda i, j: (i, 0, 0, 0))]
    sd = jax.ShapeDtypeStruct
    out_shape = [sd((b, t, 512), BF16), sd((b, t, 512), BF16)]
    out_specs = [fwd(512), bwd(512)]
    if emit_state:
        out_shape += [sd((b, nu, HEAD_DIM, HEAD_DIM), F32), sd((b, nu, 1, HEAD_DIM), F32),
                      sd((b, nu, 1, LANES), F32)]
        out_specs += [pl.BlockSpec((1, nu, HEAD_DIM, HEAD_DIM), lambda i, j: (i, 0, 0, 0)),
                      pl.BlockSpec((1, nu, 1, HEAD_DIM), lambda i, j: (i, 0, 0, 0)),
                      pl.BlockSpec((1, nu, 1, LANES), lambda i, j: (i, 0, 0, 0))]
    return pl.pallas_call(
        functools.partial(_mlstm_kernel, nc=nc, zero_init=zero_init, emit_state=emit_state),
        grid=(b, nc),
        in_specs=in_specs,
        out_specs=out_specs,
        out_shape=out_shape,
        scratch_shapes=[pltpu.VMEM((nu, HEAD_DIM + N_ROWS, HEAD_DIM), F32),
                        pltpu.VMEM((nu, 1, LANES), F32)],
        compiler_params=_params("arbitrary", "arbitrary"),
        name="mlstm",
    )(*args)


def _outproj_kernel(x_ref, u_ref, hf_ref, hb_ref, o_ref, mod_ref, pm_ref, pinv_ref, wp_ref, ps_ref,
                    wo_ref, gpm_ref, gpf_ref, wr_ref, x1_ref, h2_ref, xp_ref, lg_ref):
    bs, tm, _ = x_ref.shape
    rows = bs * tm
    x = x_ref[...].reshape(rows, D_MODEL)
    mod = mod_ref[0]
    row0 = pl.multiple_of(pl.program_id(1) * tm, tm)
    diffs = []
    for g in range(POOL_GROUPS):
        sl = slice(g * POOL_GROUP_DIM, (g + 1) * POOL_GROUP_DIM)
        per_seq = []
        for bb in range(bs):
            box = _dot(pm_ref[g], u_ref[bb, :, sl])
            per_seq.append(box * pinv_ref[g] - u_ref[bb, pl.ds(row0, tm), sl].astype(F32))
        diffs.append(jnp.concatenate(per_seq, axis=0).astype(BF16))
    yps = [_dot(jnp.concatenate(diffs[2 * p:2 * p + 2], axis=1), wp_ref[p]) for p in range(POOL_GROUPS // 2)]
    y_pool = jnp.concatenate(yps, axis=1) * ps_ref[...]
    seq_rows = lambda ref: ref[...].reshape(rows, ref.shape[-1]).astype(F32)
    y_ml = jax.nn.sigmoid(seq_rows(o_ref)) * (seq_rows(hf_ref) + seq_rows(hb_ref))
    mix = _dot(jnp.concatenate([y_pool, y_ml], axis=1).astype(BF16), wo_ref[...])
    x1 = x + mod[2:3] * _rmsnorm(mix, gpm_ref[...])
    x1_ref[...] = x1.reshape(bs, tm, D_MODEL)
    h2 = _rmsnorm(x1, gpf_ref[...]) * (1.0 + mod[4:5]) + mod[3:4]
    p1, p2, p3 = _split3(h2)
    h2_ref[...] = p1.reshape(bs, tm, D_MODEL)
    nch = D_MODEL // LANES
    for g in range(rows // 8):
        cols = jnp.stack([h2[g * 8:(g + 1) * 8, cc * LANES:(cc + 1) * LANES] for cc in range(nch)], axis=0)
        bb, r0 = divmod(g * 8, tm)
        xp_ref[bb, r0:r0 + 8] = jnp.swapaxes(cols, 0, 1).astype(BF16)
    r = _dot(jnp.concatenate([p1, p2, p3], axis=0), wr_ref[...])
    r12 = r[0:rows] + r[rows:2 * rows]
    ne = N_EXPERTS
    lg = (r12 + r[2 * rows:])[:, 0:ne] + r12[:, ne:2 * ne] + r[0:rows, 2 * ne:3 * ne]
    lg_ref[...] = jnp.concatenate([lg, jnp.zeros_like(lg)], axis=1).T[0:ne]


def _outproj(x, u, hf, hb, o, mod, mod_row, pm, pinv, wp, ps, wo, gpm, gpf, wr):
    b, t, _ = x.shape
    tm = min(t, 2 * TOKEN_TILE)
    bs = 2 * TOKEN_TILE // tm if mod_row(1) == mod_row(0) else 1
    const = lambda *shape: pl.BlockSpec(shape, lambda i, j: (0,) * len(shape))
    tok = lambda w: pl.BlockSpec((bs, tm, w), lambda i, j: (i, j, 0))
    sd = jax.ShapeDtypeStruct
    return pl.pallas_call(
        _outproj_kernel,
        grid=(b // bs, t // tm),
        in_specs=[tok(D_MODEL),
                  pl.BlockSpec((bs, t, 512), lambda i, j: (i, 0, 0)),
                  tok(512), tok(512), tok(512),
                  pl.BlockSpec((1, N_MOD, D_MODEL), lambda i, j: (mod_row(i * bs), 0, 0)),
                  pl.BlockSpec((POOL_GROUPS, tm, t), lambda i, j: (0, j, 0)),
                  pl.BlockSpec((POOL_GROUPS, tm, 1), lambda i, j: (0, j, 0)),
                  const(POOL_GROUPS // 2, 2 * POOL_GROUP_DIM, 2 * POOL_GROUP_DIM), const(1, POOL_WIDTH),
                  const(D_MODEL, D_MODEL), const(1, D_MODEL), const(1, D_MODEL),
                  const(D_MODEL, 2 * LANES)],
        out_specs=[tok(D_MODEL), tok(D_MODEL),
                   pl.BlockSpec((bs, tm, D_MODEL // LANES, LANES), lambda i, j: (i, j, 0, 0)),
                   pl.BlockSpec((N_EXPERTS, bs * tm), lambda i, j: (0, i * (t // tm) + j))],
        out_shape=[sd((b, t, D_MODEL), F32), sd((b, t, D_MODEL), BF16),
                   sd((b, t, D_MODEL // LANES, LANES), BF16), sd((N_EXPERTS, b * t), F32)],
        compiler_params=_params("arbitrary", "arbitrary"),
        name="outproj",
    )(x, u, hf, hb, o, mod, pm, pinv, wp, ps, wo, gpm, gpf, wr)


def _router_kernel(lg_ref, br_ref, comb_ref, sel_ref, *, n_blocks):
    ng, gs = N_EXPERT_GROUPS, GROUP_SIZE
    neg = -jnp.inf
    lg = jnp.swapaxes(lg_ref[...], 0, 1)
    br = br_ref[...]
    s = [jax.nn.sigmoid(lg[j]) for j in range(gs)]
    biased = [s[j] + br[:, j, :] for j in range(gs)]
    fold = lambda op, xs: functools.reduce(op, xs)
    m1 = fold(jnp.maximum, biased)
    i1 = fold(jnp.minimum, [jnp.where(biased[j] == m1, j, gs) for j in range(gs)])
    m2 = fold(jnp.maximum, [jnp.where(i1 == j, neg, biased[j]) for j in range(gs)])
    cur = m1 + m2
    gi = lax.broadcasted_iota(jnp.int32, cur.shape, 0)
    gmask = jnp.zeros(cur.shape, F32)
    for _ in range(TOPK_GROUPS):
        mx = jnp.max(cur, axis=0, keepdims=True)
        ix = jnp.min(jnp.where(cur == mx, gi, ng), axis=0, keepdims=True)
        hit = gi == ix
        gmask = jnp.where(hit, 1.0, gmask)
        cur = jnp.where(hit, neg, cur)
    cand = [jnp.where(gmask > 0, biased[j], neg) for j in range(gs)]
    eidx = [gi * gs + j for j in range(gs)]
    selm = [jnp.zeros(cur.shape, F32) for _ in range(gs)]
    for _ in range(TOP_K):
        mx = jnp.max(fold(jnp.maximum, cand), axis=0, keepdims=True)
        ix = jnp.min(fold(jnp.minimum, [jnp.where(cand[j] == mx, eidx[j], N_EXPERTS) for j in range(gs)]),
                     axis=0, keepdims=True)
        for j in range(gs):
            hit = eidx[j] == ix
            selm[j] = jnp.where(hit, 1.0, selm[j])
            cand[j] = jnp.where(hit, neg, cand[j])
    sel = [selm[j] * s[j] for j in range(gs)]
    tot = jnp.sum(fold(jnp.add, sel), axis=0, keepdims=True)
    comb = [sel[j] / tot * ROUTED_SCALE for j in range(gs)]
    sel_ref[...] = jnp.swapaxes(jnp.stack(selm, axis=0), 0, 1)
    comb_e = jnp.swapaxes(jnp.stack(comb, axis=0), 0, 1).reshape(N_EXPERTS, -1)
    comb_t = jnp.concatenate([comb_e, jnp.zeros_like(comb_e)], axis=0).T
    comb_ref[...] = jnp.where(pl.program_id(0) < n_blocks, comb_t, 0.0)


def _router(logits_t, b_router):
    t = logits_t.shape[1]
    tl = 1024
    nb = t // tl
    shp = (N_EXPERT_GROUPS, GROUP_SIZE, t)
    blk = pl.BlockSpec((N_EXPERT_GROUPS, GROUP_SIZE, tl), lambda j: (0, 0, jnp.minimum(j, nb - 1)))
    comb, sel = pl.pallas_call(
        functools.partial(_router_kernel, n_blocks=nb),
        grid=(nb + 1,),
        in_specs=[blk, pl.BlockSpec((N_EXPERT_GROUPS, GROUP_SIZE, 1), lambda j: (0, 0, 0))],
        out_specs=[pl.BlockSpec((tl, LANES), lambda j: (j, 0)), blk],
        out_shape=[jax.ShapeDtypeStruct((t + tl, LANES), F32), jax.ShapeDtypeStruct(shp, F32)],
        compiler_params=_params("arbitrary"),
        name="router",
    )(logits_t.reshape(shp), b_router.reshape(N_EXPERT_GROUPS, GROUP_SIZE, 1))
    return comb, sel.reshape(N_EXPERTS, t)


def _plan_kernel(sel_ref, pos_ref, meta_ref, *, n_meta):
    t = sel_ref.shape[1]
    tm = float(MOE_TILE)
    sel = sel_ref[...]
    selb = sel.astype(BF16)
    blk = 256
    rr = lax.broadcasted_iota(jnp.int32, (blk, blk), 0)
    cc = lax.broadcasted_iota(jnp.int32, (blk, blk), 1)
    before = (rr < cc).astype(BF16)
    carry = jnp.zeros((N_EXPERTS, 1), F32)
    ranks = []
    for b in range(t // blk):
        sb = selb[:, b * blk:(b + 1) * blk]
        ranks.append(_dot(sb, before) + carry)
        carry = carry + jnp.sum(sel[:, b * blk:(b + 1) * blk], axis=1, keepdims=True)
    rank = jnp.concatenate(ranks, axis=1)
    cnt = carry
    ntile = jnp.floor((cnt + (tm - 0.5)) * (1.0 / tm))
    er = lax.broadcasted_iota(jnp.int32, (N_EXPERTS, N_EXPERTS), 0)
    ec = lax.broadcasted_iota(jnp.int32, (N_EXPERTS, N_EXPERTS), 1)
    below = (ec < er).astype(BF16)
    tstart = _dot(below, jnp.broadcast_to(ntile, (N_EXPERTS, LANES)).astype(BF16))[:, 0:1]
    pos = tstart * tm + rank
    erank = _dot(below, selb)
    rows = []
    for k in range(TOP_K):
        hit = (sel > 0.0) & (erank == float(k))
        rows.append(jnp.sum(jnp.where(hit, pos, 0.0), axis=0, keepdims=True))
    rows += [jnp.zeros((1, t), F32)] * (8 - TOP_K)
    pos_ref[...] = jnp.concatenate(rows, axis=0).astype(jnp.int32)

    tau = lax.broadcasted_iota(jnp.int32, (N_EXPERTS, n_meta), 1).astype(F32)
    eidx = lax.broadcasted_iota(jnp.int32, (N_EXPERTS, n_meta), 0).astype(F32)
    te = jnp.sum(((tstart + ntile) <= tau).astype(F32), axis=0, keepdims=True)
    te = jnp.minimum(te, float(N_EXPERTS - 1))
    onehot = eidx == te
    cnt_t = jnp.sum(jnp.where(onehot, cnt, 0.0), axis=0, keepdims=True)
    ts_t = jnp.sum(jnp.where(onehot, tstart, 0.0), axis=0, keepdims=True)
    tr = jnp.clip(cnt_t - (tau[0:1] - ts_t) * tm, 0.0, tm)
    tf = jnp.where((tau[0:1] == ts_t) & (tr > 0.0), 1.0, 0.0)
    meta_ref[...] = jnp.concatenate([te, tr, tf] + [jnp.zeros((1, n_meta), F32)] * 5, axis=0).astype(jnp.int32)


def _plan(sel, n_meta):
    t = sel.shape[1]
    sd = jax.ShapeDtypeStruct
    return pl.pallas_call(
        functools.partial(_plan_kernel, n_meta=n_meta),
        out_shape=[sd((8, t), jnp.int32), sd((8, n_meta), jnp.int32)],
        compiler_params=pltpu.CompilerParams(vmem_limit_bytes=VMEM_LIMIT),
        name="plan",
    )(sel)


def _slot_table(pos, n_slots, n_tokens):
    t = pos.shape[1]
    workers = SC_CORES * SC_SUBCORES
    per = -(-n_slots // (workers * SC_LANES)) * SC_LANES
    mesh = plsc.VectorSubcoreMesh(core_axis_name="core", subcore_axis_name="subcore")

    def body(pos_hbm, out_hbm, pos_v, tbl_v):
        lo = (lax.axis_index("core") * SC_SUBCORES + lax.axis_index("subcore")) * per
        pltpu.sync_copy(pos_hbm.at[pl.ds(0, TOP_K * t)], pos_v)
        dummy = jnp.full((SC_LANES,), n_tokens, jnp.int32)

        @plsc.parallel_loop(0, per, SC_LANES, unroll=8)
        def _(i):
            tbl_v[pl.ds(i, SC_LANES)] = dummy

        lane = lax.iota(jnp.int32, SC_LANES)
        for k in range(TOP_K):
            @plsc.parallel_loop(0, t, SC_LANES, unroll=8)
            def _(i):
                p = pos_v[pl.ds(k * t + i, SC_LANES)] - lo
                mine = (p >= 0) & (p < per)
                plsc.store_scatter(tbl_v, [jnp.where(mine, p, 0)], lane + i, mask=mine)
        pltpu.sync_copy(tbl_v, out_hbm.at[pl.ds(lo, per)])

    build = pl.kernel(body, out_type=jax.ShapeDtypeStruct((workers * per,), jnp.int32), mesh=mesh,
                      scratch_types=[pltpu.VMEM((TOP_K * t,), jnp.int32), pltpu.VMEM((per,), jnp.int32)],
                      compiler_params=pltpu.CompilerParams(needs_layout_passes=False), name="slot_table")
    return build(pos.reshape(-1))


def _experts_kernel(te_ref, tr_ref, xc_ref, xs_ref, slots_ref, comb_ref, wg_ref, wu_ref, wd0_ref, wd1_ref, acc_out,
                    tbl_ref, xbuf, acc, stage0, stage1, cst0, cst1, act0, act1, ybuf0, ybuf1, sems):
    s = pl.program_id(0)
    n_tiles = te_ref.shape[0]
    tm = MOE_TILE
    nch = D_MODEL // LANES
    tile_at = lambda lag: jnp.clip(s - lag, 0, n_tiles - 1)
    t_g, t_1, t_2, t_3 = tile_at(0), tile_at(1), tile_at(2), tile_at(3)

    @pl.when(s == 0)
    def _():
        tc, ts = xc_ref.shape[0], xs_ref.shape[0]
        copies = (pltpu.make_async_copy(xc_ref, xbuf.at[pl.ds(0, tc)], sems.at[0]),
                  pltpu.make_async_copy(xs_ref, xbuf.at[pl.ds(tc, ts)], sems.at[1]))
        tbl_init = pltpu.make_async_copy(slots_ref, tbl_ref, sems.at[3])
        tbl_init.start()
        for cp in copies:
            cp.start()
        n_pad = xbuf.shape[0] - tc - ts
        xbuf[pl.ds(tc + ts, n_pad)] = jnp.zeros((n_pad,) + xbuf.shape[1:], BF16)
        for ref in (acc, stage0, stage1, cst0, cst1, act0, act1, ybuf0, ybuf1):
            ref[...] = jnp.zeros_like(ref)
        tbl_init.wait()
        for cp in copies:
            cp.wait()

    def gather(tile, stage, cst):
        base = tile * tm
        for j in range(tm):
            tok = tbl_ref[base + j]
            stage[pl.ds(j * nch, nch), :] = xbuf[tok].astype(F32)
            cst[pl.ds(j, 1), :] = comb_ref[pl.ds(tok, 1), :]

    def gate_up(tile, stage, cst, act):
        xb = _token_rows_from_slabs(
            lambda g: stage[g * 8 * nch:(g + 1) * 8 * nch, :].reshape(8, nch, LANES), tm).astype(BF16)
        gate = _dot(xb, wg_ref[0].astype(BF16))
        up = _dot(xb, wu_ref[0].astype(BF16))
        lane = lax.broadcasted_iota(jnp.int32, (1, LANES), 1)
        w_col = jnp.sum(jnp.where(lane == te_ref[tile], cst[...], 0.0), axis=1, keepdims=True)
        act[...] = (_silu(gate) * up * w_col).astype(BF16)

    def down(act, ybuf):
        a = act[...]
        for hh, wd_ref in enumerate((wd0_ref, wd1_ref)):
            y = _dot(a, wd_ref[0].astype(BF16))
            for cc in range(nch // 2):
                row = (hh * (nch // 2) + cc) * Y_PITCH
                ybuf[row:row + tm, :] = y[:, cc * LANES:(cc + 1) * LANES]

    def scatter(tile, ybuf):
        base = tile * tm
        sc_n = 16
        for i in range(tm // sc_n):
            toks = [tbl_ref[base + i * sc_n + u] for u in range(sc_n)]
            olds = [acc[toks[u]] for u in range(sc_n)]
            news = [olds[u] + ybuf[pl.ds(i * sc_n + u, nch, stride=Y_PITCH), :] for u in range(sc_n)]
            for u in range(sc_n):
                acc[toks[u]] = news[u]

    busy = (tr_ref[t_g] + tr_ref[t_1] + tr_ref[t_2] + tr_ref[t_3]) > 0
    bufs = ((stage0, cst0, act0, ybuf0), (stage1, cst1, act1, ybuf1))
    for par in range(2):
        stage_p, cst_p, act_p, ybuf_p = bufs[par]
        stage_q, cst_q, act_q, ybuf_q = bufs[1 - par]

        @pl.when(busy & (s % 2 == par))
        def _():
            gather(t_g, stage_p, cst_p)
            gate_up(t_1, stage_q, cst_q, act_q)
            down(act_p, ybuf_p)
            scatter(t_3, ybuf_q)

    @pl.when(s == pl.num_programs(0) - 1)
    def _():
        cp = pltpu.make_async_copy(acc, acc_out, sems.at[2])
        cp.start()
        cp.wait()


def _experts(te, tr, xc, xs, slots, comb, wgu, wd):
    n_tok = xc.shape[0] + xs.shape[0] + 8
    n_tiles = te.shape[0]
    tm = MOE_TILE
    nch = D_MODEL // LANES
    vm = pltpu.VMEM
    grid_spec = pltpu.PrefetchScalarGridSpec(
        num_scalar_prefetch=2,
        grid=(n_tiles + 3,),
        in_specs=[pl.BlockSpec(memory_space=pl.ANY), pl.BlockSpec(memory_space=pl.ANY),
                  pl.BlockSpec(memory_space=pl.ANY),
                  pl.BlockSpec((n_tok, LANES), lambda s, *_: (0, 0), pipeline_mode=pl.Buffered(1)),
                  pl.BlockSpec((1, D_MODEL, EXPERT_DIM), lambda s, te, *_: (te[jnp.clip(s - 1, 0, n_tiles - 1)], 0, 0)),
                  pl.BlockSpec((1, D_MODEL, EXPERT_DIM), lambda s, te, *_: (te[jnp.clip(s - 1, 0, n_tiles - 1)], 0, 1)),
                  pl.BlockSpec((1, EXPERT_DIM, D_MODEL // 2), lambda s, te, *_: (te[jnp.clip(s - 2, 0, n_tiles - 1)], 0, 0)),
                  pl.BlockSpec((1, EXPERT_DIM, D_MODEL // 2), lambda s, te, *_: (te[jnp.clip(s - 2, 0, n_tiles - 1)], 0, 1))],
        out_specs=pl.BlockSpec(memory_space=pl.ANY),
        scratch_shapes=[pltpu.SMEM(slots.shape, jnp.int32),
                        vm((n_tok, nch, LANES), BF16), vm((n_tok, nch, LANES), F32),
                        vm((tm * nch, LANES), F32), vm((tm * nch, LANES), F32),
                        vm((tm, LANES), F32), vm((tm, LANES), F32),
                        vm((tm, EXPERT_DIM), BF16), vm((tm, EXPERT_DIM), BF16),
                        vm((nch * Y_PITCH, LANES), F32), vm((nch * Y_PITCH, LANES), F32),
                        pltpu.SemaphoreType.DMA((4,))],
    )
    return pl.pallas_call(
        _experts_kernel,
        grid_spec=grid_spec,
        out_shape=jax.ShapeDtypeStruct((n_tok, nch, LANES), F32),
        compiler_params=pltpu.CompilerParams(dimension_semantics=("arbitrary",),
                                             vmem_limit_bytes=EXPERTS_VMEM_LIMIT),
        name="experts",
    )(te, tr, xc, xs, slots, comb, wgu, wgu, wd, wd)


def _final_kernel(acc_ref, h_ref, x1_ref, mod_ref, wsg_ref, wsd_ref, gpo_ref, out_ref):
    routed = _token_rows_from_slabs(lambda g: acc_ref[g * 8:(g + 1) * 8], acc_ref.shape[0])
    gs = _dot(h_ref[...], wsg_ref[...])
    act = _silu(gs[:, :SHARED_DIM]) * gs[:, SHARED_DIM:]
    f = routed + _dot(act.astype(BF16), wsd_ref[...])
    out_ref[...] = x1_ref[...] + mod_ref[0][5:6] * _rmsnorm(f, gpo_ref[...])


def _final(acc, tile0, h2, x1, mod, mod_row, wsg, wsd, gpo):
    n = h2.shape[0]
    tm = FINAL_TILE
    const = lambda *shape: pl.BlockSpec(shape, lambda i: (0,) * len(shape))
    tok = lambda w: pl.BlockSpec((tm, w), lambda i: (i, 0))
    return pl.pallas_call(
        _final_kernel,
        grid=(n // tm,),
        in_specs=[pl.BlockSpec((tm, D_MODEL // LANES, LANES), lambda i: (i + tile0, 0, 0)),
                  tok(D_MODEL), tok(D_MODEL),
                  pl.BlockSpec((1, N_MOD, D_MODEL), lambda i: (mod_row(i), 0, 0)),
                  const(D_MODEL, 2 * SHARED_DIM), const(SHARED_DIM, D_MODEL), const(1, D_MODEL)],
        out_specs=tok(D_MODEL),
        out_shape=jax.ShapeDtypeStruct((n, D_MODEL), F32),
        compiler_params=_params("arbitrary"),
        name="final",
    )(acc, h2, x1, mod, wsg, wsd, gpo)


def _window_bounds(n, w):
    idx = np.arange(n)
    return np.clip(idx - w // 2, 0, n), np.clip(idx + w - w // 2, 0, n)


def _pool_operators(t, grid):
    mats, invs = [], []
    for w in POOL_WINDOWS:
        if grid:
            rlo, rhi = _window_bounds(t // GRID_W, w)
            clo, chi = _window_bounds(GRID_W, w)
            r = np.arange(t) // GRID_W
            c = np.arange(t) % GRID_W
            m = ((r[None, :] >= rlo[r][:, None]) & (r[None, :] < rhi[r][:, None])
                 & (c[None, :] >= clo[c][:, None]) & (c[None, :] < chi[c][:, None]))
            cnt = (rhi - rlo)[r] * (chi - clo)[c]
        else:
            lo, hi = _window_bounds(t, w)
            sidx = np.arange(t)
            m = (sidx[None, :] >= lo[:, None]) & (sidx[None, :] < hi[:, None])
            cnt = hi - lo
        mats.append(m.astype(np.float32))
        invs.append((1.0 / cnt.astype(np.float64)).astype(np.float32)[:, None])
    return jnp.asarray(np.stack(mats), BF16), jnp.asarray(np.stack(invs), F32)


def kernel(x_prompt, x_sample, state_C, state_n, state_m, c, c_ctx, w_ada, b_ada, g_pre_mix, w_in, b_gate,
           w_pool, pool_scale, w_out, g_post_mix, g_pre_ffn, w_router, b_router, w_expert_gu, w_expert_down,
           w_shared_gu, w_shared_down, g_post_ffn):
    b_ctx = x_prompt.shape[0]
    b_lat = x_sample.shape[0]
    nu = N_DIR * HEADS
    l = 0
    row = lambda a: a[l].reshape(1, -1).astype(F32)

    cvec = jnp.zeros((16, D_MODEL), F32).at[0].set(c_ctx.astype(F32)).at[1:1 + b_lat].set(c.astype(F32))
    mod = _mod_rows(cvec, w_ada[l], b_ada[l]).reshape(16, N_MOD, D_MODEL)

    w_in_l = w_in[l]
    p0 = POOL_WIDTH
    mw = MLSTM_WIDTH
    w_u, w_q, w_k, w_v, w_o = (w_in_l[:, lo:lo + 512] for lo in (0, p0, p0 + mw, p0 + 2 * mw, p0 + 3 * mw))
    wm = jnp.concatenate([w_u, w_k, w_o], axis=1).astype(BF16)
    wt = jnp.concatenate([w_q.T, w_v.T], axis=0).astype(BF16)
    wg_cols = w_in_l[:, p0 + 4 * mw:]
    wg = jnp.pad(jnp.concatenate(_split2(wg_cols), axis=1), ((0, 0), (0, LANES - 2 * GATE_COLS)))
    bg = b_gate[l].reshape(GATE_COLS).astype(F32)
    bgr = jnp.pad(bg, (0, LANES - GATE_COLS)).reshape(1, LANES)
    wpl = w_pool[l].astype(BF16)
    zg = jnp.zeros((POOL_GROUP_DIM, POOL_GROUP_DIM), BF16)
    wp = jnp.stack([jnp.block([[wpl[2 * p], zg], [zg, wpl[2 * p + 1]]]) for p in range(POOL_GROUPS // 2)])
    wo = w_out[l].astype(BF16)
    wr = jnp.pad(jnp.concatenate(_split3(w_router[l].astype(F32)), axis=1), ((0, 0), (0, 2 * LANES - 3 * N_EXPERTS)))
    wsg = w_shared_gu[l].astype(BF16)
    wsd = w_shared_down[l].astype(BF16)

    def mixer(x, mod_row, grid, s0, m0, emit_state):
        t = x.shape[1]
        u, k, o, qt, vt, gate, gate_t = _inproj(x.astype(F32), mod, mod_row, row(g_pre_mix), wm, wt, wg, bgr)
        outs = _mlstm(k, qt, vt, gate, gate_t, s0, m0, emit_state)
        hf, hb = outs[0], outs[1]
        pm, pinv = _pool_operators(t, grid)
        x1, h2, xp, lg = _outproj(x.astype(F32), u, hf, hb, o, mod, mod_row, pm, pinv, wp, row(pool_scale), wo,
                                  row(g_post_mix), row(g_pre_ffn), wr)
        return x1, h2, xp, lg, outs[2:]

    ctx_row = lambda i: 0
    lat_row = lambda i: i + 1
    x1c, h2c, xpc, lgc, (c_new, n_new, m_new) = mixer(x_prompt, ctx_row, False, None, None, True)
    s0 = jnp.concatenate(
        [jnp.swapaxes(state_C[:, l].reshape(b_lat, nu, HEAD_DIM, HEAD_DIM).astype(F32), -1, -2),
         jnp.broadcast_to(state_n[:, l].reshape(b_lat, nu, 1, HEAD_DIM).astype(F32),
                          (b_lat, nu, N_ROWS, HEAD_DIM))], axis=-2)
    m0 = jnp.broadcast_to(state_m[:, l].reshape(b_lat, nu, 1, 1).astype(F32), (b_lat, nu, 1, LANES))
    x1s, h2s, xps, lgs, _ = mixer(x_sample, lat_row, True, s0, m0, False)

    tc = b_ctx * x_prompt.shape[1]
    ts = b_lat * x_sample.shape[1]
    n_tok = tc + ts
    lg_all = jnp.concatenate([lgc, lgs], axis=1)
    comb_tok, sel = _router(lg_all, b_router[l].astype(F32))
    n_tiles = n_tok * TOP_K // MOE_TILE + N_EXPERTS
    n_meta = -(-n_tiles // LANES) * LANES
    pos, meta = _plan(sel, n_meta)

    slab = (D_MODEL // LANES, LANES)
    acc = _experts(meta[0, :n_tiles], meta[1, :n_tiles],
                   xpc.reshape((tc,) + slab), xps.reshape((ts,) + slab),
                   _slot_table(pos, n_tiles * MOE_TILE, n_tok), comb_tok,
                   w_expert_gu[l], w_expert_down[l])

    fin = functools.partial(_final, wsg=wsg, wsd=wsd, gpo=row(g_post_ffn))
    tiles_per_lat = x_sample.shape[1] // FINAL_TILE
    yc = fin(acc, 0, h2c.reshape(tc, D_MODEL), x1c.reshape(tc, D_MODEL), mod, ctx_row)
    ys = fin(acc, tc // FINAL_TILE, h2s.reshape(ts, D_MODEL), x1s.reshape(ts, D_MODEL), mod,
             lambda i: i // tiles_per_lat + 1)

    new_c = c_new.reshape(b_ctx, 1, N_DIR, HEADS, HEAD_DIM, HEAD_DIM)
    new_n = n_new.reshape(b_ctx, 1, N_DIR, HEADS, HEAD_DIM)
    new_m = m_new[..., 0].reshape(b_ctx, 1, N_DIR, HEADS)
    return (yc.reshape(x_prompt.shape), ys.reshape(x_sample.shape), new_c, new_n, new_m)
```

```python
import functools

import jax
import jax.numpy as jnp
import numpy as np
from jax import lax
from jax.experimental import pallas as pl
from jax.experimental.pallas import tpu as pltpu
from jax.experimental.pallas import tpu_sc as plsc

F32 = jnp.float32
BF16 = jnp.bfloat16

D_MODEL = 1024
GRID_W = 64
POOL_WIDTH = 512
POOL_GROUPS = 4
POOL_GROUP_DIM = 128
POOL_WINDOWS = (2, 4, 8, 16)
HEADS = 4
HEAD_DIM = 128
MLSTM_WIDTH = HEADS * HEAD_DIM
N_DIR = 2
GATE_COLS = N_DIR * 2 * HEADS
N_EXPERTS = 64
TOP_K = 6
N_EXPERT_GROUPS = 8
GROUP_SIZE = N_EXPERTS // N_EXPERT_GROUPS
TOPK_GROUPS = 4
EXPERT_DIM = 256
SHARED_DIM = 256
ROUTED_SCALE = 2.5
N_MOD = 6
EPS = 1e-6
K_SCALE = HEAD_DIM ** -0.5

LANES = 128
CHUNK = 256
N_ROWS = 16
TOKEN_TILE = 256
FINAL_TILE = 512
MOE_TILE = 320
Y_PITCH = MOE_TILE + 4
VMEM_LIMIT = 56 * 1024 * 1024
EXPERTS_VMEM_LIMIT = 58 * 1024 * 1024
SC_CORES = 2
SC_SUBCORES = 16
SC_LANES = 16


def _split3(x):
    p1 = x.astype(BF16)
    r1 = x - p1.astype(F32)
    p2 = r1.astype(BF16)
    p3 = (r1 - p2.astype(F32)).astype(BF16)
    return p1, p2, p3


def _split2(x):
    p1 = x.astype(BF16)
    p2 = (x - p1.astype(F32)).astype(BF16)
    return p1, p2


def _dot(a, b):
    return jnp.dot(a, b, preferred_element_type=F32)


def _dot_nt(a, b):
    return lax.dot_general(a, b, (((1,), (1,)), ((), ())), preferred_element_type=F32)


def _rmsnorm(x, g):
    return x * lax.rsqrt(jnp.mean(x * x, axis=-1, keepdims=True) + EPS) * g


def _silu(x):
    return x * jax.nn.sigmoid(x)


def _token_rows_from_slabs(read_block, n_tok):
    nch = D_MODEL // LANES
    cols = [[] for _ in range(nch)]
    for g in range(n_tok // 8):
        blk = jnp.swapaxes(read_block(g), 0, 1)
        for cc in range(nch):
            cols[cc].append(blk[cc])
    return jnp.concatenate([jnp.concatenate(c, axis=0) for c in cols], axis=1)


def _params(*sem):
    return pltpu.CompilerParams(dimension_semantics=sem, vmem_limit_bytes=VMEM_LIMIT)


def _mod_kernel(c_ref, w_ref, b_ref, o_ref):
    a = _silu(c_ref[...])
    a_stack = jnp.concatenate(_split3(a), axis=0)
    w1, w2 = _split2(w_ref[...])
    r1 = _dot(a_stack, w1)
    r2 = _dot(a_stack[:32], w2)
    o_ref[...] = (r1[0:16] + r1[16:32] + r1[32:48] + r2[0:16] + r2[16:32]) + b_ref[...]


def _mod_rows(cvec, w_ada, b_ada):
    n = N_MOD * D_MODEL
    tn = 1536
    return pl.pallas_call(
        _mod_kernel,
        grid=(n // tn,),
        in_specs=[pl.BlockSpec((16, D_MODEL), lambda j: (0, 0)),
                  pl.BlockSpec((D_MODEL, tn), lambda j: (0, j)),
                  pl.BlockSpec((1, tn), lambda j: (0, j))],
        out_specs=pl.BlockSpec((16, tn), lambda j: (0, j)),
        out_shape=jax.ShapeDtypeStruct((16, n), F32),
        compiler_params=_params("arbitrary"),
        name="mod",
    )(cvec, w_ada, b_ada.reshape(1, n))


def _inproj_kernel(x_ref, mod_ref, g_ref, wm_ref, wt_ref, wg_ref, bgr_ref,
                   u_ref, k_ref, o_ref, qt_ref, vt_ref, gate_ref, gatet_ref):
    bs, tm, _ = x_ref.shape
    rows = bs * tm
    x = x_ref[...].reshape(rows, D_MODEL)
    mod = mod_ref[0]
    h = _rmsnorm(x, g_ref[...]) * (1.0 + mod[1:2]) + mod[0:1]
    h1, h2, h3 = _split3(h)
    z = _dot(h1, wm_ref[...])
    u_ref[...] = z[:, 0:512].astype(BF16).reshape(bs, tm, 512)
    k_ref[...] = (z[:, 512:1024] * K_SCALE).astype(BF16).reshape(bs, tm, 512)
    o_ref[...] = z[:, 1024:1536].astype(BF16).reshape(bs, tm, 512)
    zt = _dot_nt(wt_ref[...], h1).astype(BF16)
    r = _dot(jnp.concatenate([h1, h2, h3], axis=0), wg_ref[...])
    r12 = r[0:rows] + r[rows:2 * rows]
    gate = (r12 + r[2 * rows:]) + pltpu.roll(r12, LANES - GATE_COLS, axis=1) + bgr_ref[...]
    gate_ref[...] = gate.reshape(bs, tm, LANES)
    gate_t = gate.T
    for bb in range(bs):
        cols = slice(bb * tm, (bb + 1) * tm)
        qt_ref[bb] = zt[0:512, cols]
        vt_ref[bb] = zt[512:1024, cols]
        gatet_ref[bb] = gate_t[0:16, cols]


def _inproj(x, mod, mod_row, g, wm, wt, wg, bgr):
    b, t, _ = x.shape
    tm = min(t, 2 * TOKEN_TILE)
    bs = 2 * TOKEN_TILE // tm if mod_row(1) == mod_row(0) else 1
    const = lambda *shape: pl.BlockSpec(shape, lambda i, j: (0,) * len(shape))
    tok = lambda w: pl.BlockSpec((bs, tm, w), lambda i, j: (i, j, 0))
    tok_t = lambda r: pl.BlockSpec((bs, r, tm), lambda i, j: (i, 0, j))
    sd = jax.ShapeDtypeStruct
    return pl.pallas_call(
        _inproj_kernel,
        grid=(b // bs, t // tm),
        in_specs=[tok(D_MODEL),
                  pl.BlockSpec((1, N_MOD, D_MODEL), lambda i, j: (mod_row(i * bs), 0, 0)),
                  const(1, D_MODEL), const(D_MODEL, 1536), const(1024, D_MODEL),
                  const(D_MODEL, LANES), const(1, LANES)],
        out_specs=[tok(512), tok(512), tok(512), tok_t(512), tok_t(512), tok(LANES), tok_t(16)],
        out_shape=[sd((b, t, 512), BF16), sd((b, t, 512), BF16), sd((b, t, 512), BF16),
                   sd((b, 512, t), BF16), sd((b, 512, t), BF16), sd((b, t, LANES), F32),
                   sd((b, 16, t), F32)],
        compiler_params=_params("arbitrary", "arbitrary"),
        name="inproj",
    )(x, mod, g, wm, wt, wg, bgr)


def _log_sigmoid(x):
    return jnp.minimum(x, 0.0) - jnp.log1p(jnp.exp(-jnp.abs(x)))


def _scan_unit(st, k, qt, vt, u_col, u_row, b_row, btot, mask, s_prev, m_prev, use_state):
    dh = HEAD_DIM
    n = st.shape[0]
    ub = jnp.where(mask, jnp.broadcast_to(u_col, (n, n)), -jnp.inf)
    z = jnp.maximum(m_prev, jnp.max(ub, axis=0, keepdims=True))
    p = (jnp.exp(ub - z) * st).astype(BF16)
    ones = jnp.ones((N_ROWS, n), BF16)
    tot = _dot(jnp.concatenate([vt, ones], axis=0), p)
    if use_state:
        tot = tot + jnp.exp(m_prev - z) * _dot(s_prev.astype(BF16), qt)
    floor = jnp.exp(-(b_row + z))
    h_t = tot[:dh] / jnp.maximum(jnp.abs(tot[dh:dh + 1]), floor)
    g_row = btot + u_row
    m_new = jnp.maximum(btot + m_prev, jnp.max(g_row, axis=-1, keepdims=True))
    w_row = jnp.exp(g_row - m_new)
    vw = jnp.concatenate([(vt.astype(F32) * w_row).astype(BF16),
                          jnp.broadcast_to(w_row, (N_ROWS, n)).astype(BF16)], axis=0)
    s_new = jnp.exp(btot + m_prev - m_new) * s_prev + _dot(vw, k)
    return h_t.T, s_new, m_new


def _mlstm_kernel(*refs, nc, zero_init, emit_state):
    it = iter(refs)
    fwd_refs = tuple(next(it) for _ in range(5))
    bwd_refs = tuple(next(it) for _ in range(5)) if nc > 1 else fwd_refs
    if not zero_init:
        s0_ref, m0_ref = next(it), next(it)
    h_refs = (next(it), next(it))
    if emit_state:
        c_out, n_out, m_out = next(it), next(it), next(it)
    s_scr, m_scr = next(it), next(it)

    j = pl.program_id(1)
    n = CHUNK
    dh = HEAD_DIM

    @pl.when(j == 0)
    def _():
        if zero_init:
            s_scr[...] = jnp.zeros_like(s_scr)
            m_scr[...] = jnp.zeros_like(m_scr)
        else:
            s_scr[...] = s0_ref[0]
            m_scr[...] = m0_ref[0]

    rows = lax.broadcasted_iota(jnp.int32, (n, n), 0)
    cols = lax.broadcasted_iota(jnp.int32, (n, n), 1)
    le = rows <= cols
    ge = rows >= cols
    tri_le = le.astype(BF16)
    tri_ge = ge.astype(BF16)
    use_state = not (zero_init and nc == 1)

    def gate_terms(d):
        g_ref, gt_ref = (fwd_refs, bwd_refs)[d][3:5]
        gate = g_ref[0]
        gate_t = gt_ref[0]
        lf = _log_sigmoid(gate)
        lf_t = _log_sigmoid(gate_t)
        tri_c, tri_r = (tri_ge, tri_le) if d == 0 else (tri_le, tri_ge)
        bc = _dot(tri_c, jnp.concatenate(_split3(lf), axis=1))
        b_cols = bc[:, 0:128] + bc[:, 128:256] + bc[:, 256:384]
        br = _dot(jnp.concatenate(_split3(lf_t), axis=0), tri_r)
        b_rows = br[0:16] + br[16:32] + br[32:48]
        return gate, gate_t, b_cols, b_rows, jnp.sum(lf_t, axis=-1, keepdims=True)

    terms = [gate_terms(0), gate_terms(1)]
    hs = ([], [])
    for hd in range(HEADS):
        hsl = slice(hd * dh, (hd + 1) * dh)
        st = None
        for d in range(N_DIR):
            k_ref, qt_ref, vt_ref = (fwd_refs, bwd_refs)[d][0:3]
            gate, gate_t, b_cols, b_rows, tot_rows = terms[d]
            ci = d * 8 + hd
            cf = d * 8 + 4 + hd
            unit = d * HEADS + hd
            k = k_ref[0, :, hsl]
            qt = qt_ref[0, hsl, :]
            if st is None or nc > 1:
                st = _dot(k, qt)
            mask = le if d == 0 else ge
            h, s_new, m_new = _scan_unit(
                st, k, qt, vt_ref[0, hsl, :],
                gate[:, ci:ci + 1] - b_cols[:, cf:cf + 1],
                gate_t[ci:ci + 1, :] - b_rows[cf:cf + 1, :],
                b_rows[cf:cf + 1, :], tot_rows[cf:cf + 1, :],
                mask, s_scr[unit], m_scr[unit][:, 0:1], use_state)
            s_scr[unit] = s_new
            m_scr[unit] = jnp.broadcast_to(m_new, (1, LANES))
            hs[d].append(h)
    for d in range(N_DIR):
        h_refs[d][0] = jnp.concatenate(hs[d], axis=1).astype(BF16)

    if emit_state:
        @pl.when(j == nc - 1)
        def _():
            for unit in range(N_DIR * HEADS):
                s = s_scr[unit]
                c_out[0, unit] = s[:dh].T
                n_out[0, unit] = s[dh:dh + 1]
                m_out[0, unit] = m_scr[unit]


def _mlstm(k, qt, vt, gate, gate_t, s0, m0, emit_state):
    b, t, _ = k.shape
    nc = t // CHUNK
    zero_init = s0 is None
    nu = N_DIR * HEADS
    fwd = lambda w: pl.BlockSpec((1, CHUNK, w), lambda i, j: (i, j, 0))
    bwd = lambda w: pl.BlockSpec((1, CHUNK, w), lambda i, j: (i, nc - 1 - j, 0))
    fwd_t = lambda r: pl.BlockSpec((1, r, CHUNK), lambda i, j: (i, 0, j))
    bwd_t = lambda r: pl.BlockSpec((1, r, CHUNK), lambda i, j: (i, 0, nc - 1 - j))
    args = [k, qt, vt, gate, gate_t]
    in_specs = [fwd(512), fwd_t(512), fwd_t(512), fwd(LANES), fwd_t(16)]
    if nc > 1:
        args += [k, qt, vt, gate, gate_t]
        in_specs += [bwd(512), bwd_t(512), bwd_t(512), bwd(LANES), bwd_t(16)]
    if not zero_init:
        args += [s0, m0]
        in_specs += [pl.BlockSpec((1, nu, HEAD_DIM + N_ROWS, HEAD_DIM), lambda i, j: (i, 0, 0, 0)),
                     pl.BlockSpec((1, nu, 1, LANES), lambda i, j: (i, 0, 0, 0))]
    sd = jax.ShapeDtypeStruct
    out_shape = [sd((b, t, 512), BF16), sd((b, t, 512), BF16)]
    out_specs = [fwd(512), bwd(512)]
    if emit_state:
        out_shape += [sd((b, nu, HEAD_DIM, HEAD_DIM), F32), sd((b, nu, 1, HEAD_DIM), F32),
                      sd((b, nu, 1, LANES), F32)]
        out_specs += [pl.BlockSpec((1, nu, HEAD_DIM, HEAD_DIM), lambda i, j: (i, 0, 0, 0)),
                      pl.BlockSpec((1, nu, 1, HEAD_DIM), lambda i, j: (i, 0, 0, 0)),
                      pl.BlockSpec((1, nu, 1, LANES), lambda i, j: (i, 0, 0, 0))]
    return pl.pallas_call(
        functools.partial(_mlstm_kernel, nc=nc, zero_init=zero_init, emit_state=emit_state),
        grid=(b, nc),
        in_specs=in_specs,
        out_specs=out_specs,
        out_shape=out_shape,
        scratch_shapes=[pltpu.VMEM((nu, HEAD_DIM + N_ROWS, HEAD_DIM), F32),
                        pltpu.VMEM((nu, 1, LANES), F32)],
        compiler_params=_params("arbitrary", "arbitrary"),
        name="mlstm",
    )(*args)


def _outproj_kernel(x_ref, u_ref, hf_ref, hb_ref, o_ref, mod_ref, pm_ref, pinv_ref, wp_ref, ps_ref,
                    wo_ref, gpm_ref, gpf_ref, wr_ref, x1_ref, h2_ref, xp_ref, lg_ref):
    bs, tm, _ = x_ref.shape
    rows = bs * tm
    x = x_ref[...].reshape(rows, D_MODEL)
    mod = mod_ref[0]
    row0 = pl.multiple_of(pl.program_id(1) * tm, tm)
    diffs = []
    for g in range(POOL_GROUPS):
        sl = slice(g * POOL_GROUP_DIM, (g + 1) * POOL_GROUP_DIM)
        per_seq = []
        for bb in range(bs):
            box = _dot(pm_ref[g], u_ref[bb, :, sl])
            per_seq.append(box * pinv_ref[g] - u_ref[bb, pl.ds(row0, tm), sl].astype(F32))
        diffs.append(jnp.concatenate(per_seq, axis=0).astype(BF16))
    yps = [_dot(jnp.concatenate(diffs[2 * p:2 * p + 2], axis=1), wp_ref[p]) for p in range(POOL_GROUPS // 2)]
    y_pool = jnp.concatenate(yps, axis=1) * ps_ref[...]
    seq_rows = lambda ref: ref[...].reshape(rows, ref.shape[-1]).astype(F32)
    y_ml = jax.nn.sigmoid(seq_rows(o_ref)) * (seq_rows(hf_ref) + seq_rows(hb_ref))
    mix = _dot(jnp.concatenate([y_pool, y_ml], axis=1).astype(BF16), wo_ref[...])
    x1 = x + mod[2:3] * _rmsnorm(mix, gpm_ref[...])
    x1_ref[...] = x1.reshape(bs, tm, D_MODEL)
    h2 = _rmsnorm(x1, gpf_ref[...]) * (1.0 + mod[4:5]) + mod[3:4]
    p1, p2, p3 = _split3(h2)
    h2_ref[...] = p1.reshape(bs, tm, D_MODEL)
    nch = D_MODEL // LANES
    for g in range(rows // 8):
        cols = jnp.stack([h2[g * 8:(g + 1) * 8, cc * LANES:(cc + 1) * LANES] for cc in range(nch)], axis=0)
        bb, r0 = divmod(g * 8, tm)
        xp_ref[bb, r0:r0 + 8] = jnp.swapaxes(cols, 0, 1).astype(BF16)
    r = _dot(jnp.concatenate([p1, p2, p3], axis=0), wr_ref[...])
    r12 = r[0:rows] + r[rows:2 * rows]
    ne = N_EXPERTS
    lg = (r12 + r[2 * rows:])[:, 0:ne] + r12[:, ne:2 * ne] + r[0:rows, 2 * ne:3 * ne]
    lg_ref[...] = jnp.concatenate([lg, jnp.zeros_like(lg)], axis=1).T[0:ne]


def _outproj(x, u, hf, hb, o, mod, mod_row, pm, pinv, wp, ps, wo, gpm, gpf, wr):
    b, t, _ = x.shape
    tm = min(t, 2 * TOKEN_TILE)
    bs = 2 * TOKEN_TILE // tm if mod_row(1) == mod_row(0) else 1
    const = lambda *shape: pl.BlockSpec(shape, lambda i, j: (0,) * len(shape))
    tok = lambda w: pl.BlockSpec((bs, tm, w), lambda i, j: (i, j, 0))
    sd = jax.ShapeDtypeStruct
    return pl.pallas_call(
        _outproj_kernel,
        grid=(b // bs, t // tm),
        in_specs=[tok(D_MODEL),
                  pl.BlockSpec((bs, t, 512), lambda i, j: (i, 0, 0)),
                  tok(512), tok(512), tok(512),
                  pl.BlockSpec((1, N_MOD, D_MODEL), lambda i, j: (mod_row(i * bs), 0, 0)),
                  pl.BlockSpec((POOL_GROUPS, tm, t), lambda i, j: (0, j, 0)),
                  pl.BlockSpec((POOL_GROUPS, tm, 1), lambda i, j: (0, j, 0)),
                  const(POOL_GROUPS // 2, 2 * POOL_GROUP_DIM, 2 * POOL_GROUP_DIM), const(1, POOL_WIDTH),
                  const(D_MODEL, D_MODEL), const(1, D_MODEL), const(1, D_MODEL),
                  const(D_MODEL, 2 * LANES)],
        out_specs=[tok(D_MODEL), tok(D_MODEL),
                   pl.BlockSpec((bs, tm, D_MODEL // LANES, LANES), lambda i, j: (i, j, 0, 0)),
                   pl.BlockSpec((N_EXPERTS, bs * tm), lambda i, j: (0, i * (t // tm) + j))],
        out_shape=[sd((b, t, D_MODEL), F32), sd((b, t, D_MODEL), BF16),
                   sd((b, t, D_MODEL // LANES, LANES), BF16), sd((N_EXPERTS, b * t), F32)],
        compiler_params=_params("arbitrary", "arbitrary"),
        name="outproj",
    )(x, u, hf, hb, o, mod, pm, pinv, wp, ps, wo, gpm, gpf, wr)


def _router_kernel(lg_ref, br_ref, comb_ref, sel_ref, *, n_blocks):
    ng, gs = N_EXPERT_GROUPS, GROUP_SIZE
    neg = -jnp.inf
    lg = jnp.swapaxes(lg_ref[...], 0, 1)
    br = br_ref[...]
    s = [jax.nn.sigmoid(lg[j]) for j in range(gs)]
    biased = [s[j] + br[:, j, :] for j in range(gs)]
    fold = lambda op, xs: functools.reduce(op, xs)
    m1 = fold(jnp.maximum, biased)
    i1 = fold(jnp.minimum, [jnp.where(biased[j] == m1, j, gs) for j in range(gs)])
    m2 = fold(jnp.maximum, [jnp.where(i1 == j, neg, biased[j]) for j in range(gs)])
    cur = m1 + m2
    gi = lax.broadcasted_iota(jnp.int32, cur.shape, 0)
    gmask = jnp.zeros(cur.shape, F32)
    for _ in range(TOPK_GROUPS):
        mx = jnp.max(cur, axis=0, keepdims=True)
        ix = jnp.min(jnp.where(cur == mx, gi, ng), axis=0, keepdims=True)
        hit = gi == ix
        gmask = jnp.where(hit, 1.0, gmask)
        cur = jnp.where(hit, neg, cur)
    cand = [jnp.where(gmask > 0, biased[j], neg) for j in range(gs)]
    eidx = [gi * gs + j for j in range(gs)]
    selm = [jnp.zeros(cur.shape, F32) for _ in range(gs)]
    for _ in range(TOP_K):
        mx = jnp.max(fold(jnp.maximum, cand), axis=0, keepdims=True)
        ix = jnp.min(fold(jnp.minimum, [jnp.where(cand[j] == mx, eidx[j], N_EXPERTS) for j in range(gs)]),
                     axis=0, keepdims=True)
        for j in range(gs):
            hit = eidx[j] == ix
            selm[j] = jnp.where(hit, 1.0, selm[j])
            cand[j] = jnp.where(hit, neg, cand[j])
    sel = [selm[j] * s[j] for j in range(gs)]
    tot = jnp.sum(fold(jnp.add, sel), axis=0, keepdims=True)
    comb = [sel[j] / tot * ROUTED_SCALE for j in range(gs)]
    sel_ref[...] = jnp.swapaxes(jnp.stack(selm, axis=0), 0, 1)
    comb_e = jnp.swapaxes(jnp.stack(comb, axis=0), 0, 1).reshape(N_EXPERTS, -1)
    comb_t = jnp.concatenate([comb_e, jnp.zeros_like(comb_e)], axis=0).T
    comb_ref[...] = jnp.where(pl.program_id(0) < n_blocks, comb_t, 0.0)


def _router(logits_t, b_router):
    t = logits_t.shape[1]
    tl = 1024
    nb = t // tl
    shp = (N_EXPERT_GROUPS, GROUP_SIZE, t)
    blk = pl.BlockSpec((N_EXPERT_GROUPS, GROUP_SIZE, tl), lambda j: (0, 0, jnp.minimum(j, nb - 1)))
    comb, sel = pl.pallas_call(
        functools.partial(_router_kernel, n_blocks=nb),
        grid=(nb + 1,),
        in_specs=[blk, pl.BlockSpec((N_EXPERT_GROUPS, GROUP_SIZE, 1), lambda j: (0, 0, 0))],
        out_specs=[pl.BlockSpec((tl, LANES), lambda j: (j, 0)), blk],
        out_shape=[jax.ShapeDtypeStruct((t + tl, LANES), F32), jax.ShapeDtypeStruct(shp, F32)],
        compiler_params=_params("arbitrary"),
        name="router",
    )(logits_t.reshape(shp), b_router.reshape(N_EXPERT_GROUPS, GROUP_SIZE, 1))
    return comb, sel.reshape(N_EXPERTS, t)


def _plan_kernel(sel_ref, pos_ref, meta_ref, *, n_meta):
    t = sel_ref.shape[1]
    tm = float(MOE_TILE)
    sel = sel_ref[...]
    selb = sel.astype(BF16)
    blk = 256
    rr = lax.broadcasted_iota(jnp.int32, (blk, blk), 0)
    cc = lax.broadcasted_iota(jnp.int32, (blk, blk), 1)
    before = (rr < cc).astype(BF16)
    carry = jnp.zeros((N_EXPERTS, 1), F32)
    ranks = []
    for b in range(t // blk):
        sb = selb[:, b * blk:(b + 1) * blk]
        ranks.append(_dot(sb, before) + carry)
        carry = carry + jnp.sum(sel[:, b * blk:(b + 1) * blk], axis=1, keepdims=True)
    rank = jnp.concatenate(ranks, axis=1)
    cnt = carry
    ntile = jnp.floor((cnt + (tm - 0.5)) * (1.0 / tm))
    er = lax.broadcasted_iota(jnp.int32, (N_EXPERTS, N_EXPERTS), 0)
    ec = lax.broadcasted_iota(jnp.int32, (N_EXPERTS, N_EXPERTS), 1)
    below = (ec < er).astype(BF16)
    tstart = _dot(below, jnp.broadcast_to(ntile, (N_EXPERTS, LANES)).astype(BF16))[:, 0:1]
    pos = tstart * tm + rank
    erank = _dot(below, selb)
    rows = []
    for k in range(TOP_K):
        hit = (sel > 0.0) & (erank == float(k))
        rows.append(jnp.sum(jnp.where(hit, pos, 0.0), axis=0, keepdims=True))
    rows += [jnp.zeros((1, t), F32)] * (8 - TOP_K)
    pos_ref[...] = jnp.concatenate(rows, axis=0).astype(jnp.int32)

    tau = lax.broadcasted_iota(jnp.int32, (N_EXPERTS, n_meta), 1).astype(F32)
    eidx = lax.broadcasted_iota(jnp.int32, (N_EXPERTS, n_meta), 0).astype(F32)
    te = jnp.sum(((tstart + ntile) <= tau).astype(F32), axis=0, keepdims=True)
    te = jnp.minimum(te, float(N_EXPERTS - 1))
    onehot = eidx == te
    cnt_t = jnp.sum(jnp.where(onehot, cnt, 0.0), axis=0, keepdims=True)
    ts_t = jnp.sum(jnp.where(onehot, tstart, 0.0), axis=0, keepdims=True)
    tr = jnp.clip(cnt_t - (tau[0:1] - ts_t) * tm, 0.0, tm)
    tf = jnp.where((tau[0:1] == ts_t) & (tr > 0.0), 1.0, 0.0)
    meta_ref[...] = jnp.concatenate([te, tr, tf] + [jnp.zeros((1, n_meta), F32)] * 5, axis=0).astype(jnp.int32)


def _plan(sel, n_meta):
    t = sel.shape[1]
    sd = jax.ShapeDtypeStruct
    return pl.pallas_call(
        functools.partial(_plan_kernel, n_meta=n_meta),
        out_shape=[sd((8, t), jnp.int32), sd((8, n_meta), jnp.int32)],
        compiler_params=pltpu.CompilerParams(vmem_limit_bytes=VMEM_LIMIT),
        name="plan",
    )(sel)


def _slot_table(pos, n_slots, n_tokens):
    t = pos.shape[1]
    workers = SC_CORES * SC_SUBCORES
    per = -(-n_slots // (workers * SC_LANES)) * SC_LANES
    mesh = plsc.VectorSubcoreMesh(core_axis_name="core", subcore_axis_name="subcore")

    def body(pos_hbm, out_hbm, pos_v, tbl_v):
        lo = (lax.axis_index("core") * SC_SUBCORES + lax.axis_index("subcore")) * per
        pltpu.sync_copy(pos_hbm.at[pl.ds(0, TOP_K * t)], pos_v)
        dummy = jnp.full((SC_LANES,), n_tokens, jnp.int32)

        @plsc.parallel_loop(0, per, SC_LANES, unroll=8)
        def _(i):
            tbl_v[pl.ds(i, SC_LANES)] = dummy

        lane = lax.iota(jnp.int32, SC_LANES)
        for k in range(TOP_K):
            @plsc.parallel_loop(0, t, SC_LANES, unroll=8)
            def _(i):
                p = pos_v[pl.ds(k * t + i, SC_LANES)] - lo
                mine = (p >= 0) & (p < per)
                plsc.store_scatter(tbl_v, [jnp.where(mine, p, 0)], lane + i, mask=mine)
        pltpu.sync_copy(tbl_v, out_hbm.at[pl.ds(lo, per)])

    build = pl.kernel(body, out_type=jax.ShapeDtypeStruct((workers * per,), jnp.int32), mesh=mesh,
                      scratch_types=[pltpu.VMEM((TOP_K * t,), jnp.int32), pltpu.VMEM((per,), jnp.int32)],
                      compiler_params=pltpu.CompilerParams(needs_layout_passes=False), name="slot_table")
    return build(pos.reshape(-1))


def _experts_kernel(te_ref, tr_ref, xc_ref, xs_ref, slots_ref, comb_ref, wgu_ref, wd_ref, acc_out,
                    tbl_ref, xbuf, acc, stage0, stage1, cst0, cst1, act0, act1, ybuf0, ybuf1, sems):
    s = pl.program_id(0)
    n_tiles = te_ref.shape[0]
    tm = MOE_TILE
    nch = D_MODEL // LANES
    tile_at = lambda lag: jnp.clip(s - lag, 0, n_tiles - 1)
    t_g, t_1, t_2, t_3 = tile_at(0), tile_at(1), tile_at(2), tile_at(3)

    @pl.when(s == 0)
    def _():
        tc, ts = xc_ref.shape[0], xs_ref.shape[0]
        copies = (pltpu.make_async_copy(xc_ref, xbuf.at[pl.ds(0, tc)], sems.at[0]),
                  pltpu.make_async_copy(xs_ref, xbuf.at[pl.ds(tc, ts)], sems.at[1]))
        tbl_init = pltpu.make_async_copy(slots_ref, tbl_ref, sems.at[3])
        tbl_init.start()
        for cp in copies:
            cp.start()
        n_pad = xbuf.shape[0] - tc - ts
        xbuf[pl.ds(tc + ts, n_pad)] = jnp.zeros((n_pad,) + xbuf.shape[1:], BF16)
        for ref in (acc, stage0, stage1, cst0, cst1, act0, act1, ybuf0, ybuf1):
            ref[...] = jnp.zeros_like(ref)
        tbl_init.wait()
        for cp in copies:
            cp.wait()

    def gather(tile, stage, cst):
        base = tile * tm
        for j in range(tm):
            tok = tbl_ref[base + j]
            stage[pl.ds(j * nch, nch), :] = xbuf[tok].astype(F32)
            cst[pl.ds(j, 1), :] = comb_ref[pl.ds(tok, 1), :]

    def gate_up(tile, stage, cst, act):
        xb = _token_rows_from_slabs(
            lambda g: stage[g * 8 * nch:(g + 1) * 8 * nch, :].reshape(8, nch, LANES), tm).astype(BF16)
        gu = _dot(xb, wgu_ref[0].astype(BF16))
        lane = lax.broadcasted_iota(jnp.int32, (1, LANES), 1)
        w_col = jnp.sum(jnp.where(lane == te_ref[tile], cst[...], 0.0), axis=1, keepdims=True)
        act[...] = (_silu(gu[:, :EXPERT_DIM]) * gu[:, EXPERT_DIM:] * w_col).astype(BF16)

    def down(act, ybuf):
        y = _dot(act[...], wd_ref[0].astype(BF16))
        for cc in range(nch):
            ybuf[cc * Y_PITCH:cc * Y_PITCH + tm, :] = y[:, cc * LANES:(cc + 1) * LANES]

    def scatter(tile, ybuf):
        base = tile * tm
        sc_n = 16
        for i in range(tm // sc_n):
            toks = [tbl_ref[base + i * sc_n + u] for u in range(sc_n)]
            olds = [acc[toks[u]] for u in range(sc_n)]
            news = [olds[u] + ybuf[pl.ds(i * sc_n + u, nch, stride=Y_PITCH), :] for u in range(sc_n)]
            for u in range(sc_n):
                acc[toks[u]] = news[u]

    busy = (tr_ref[t_g] + tr_ref[t_1] + tr_ref[t_2] + tr_ref[t_3]) > 0
    bufs = ((stage0, cst0, act0, ybuf0), (stage1, cst1, act1, ybuf1))
    for par in range(2):
        stage_p, cst_p, act_p, ybuf_p = bufs[par]
        stage_q, cst_q, act_q, ybuf_q = bufs[1 - par]

        @pl.when(busy & (s % 2 == par))
        def _():
            gather(t_g, stage_p, cst_p)
            gate_up(t_1, stage_q, cst_q, act_q)
            down(act_p, ybuf_p)
            scatter(t_3, ybuf_q)

    @pl.when(s == pl.num_programs(0) - 1)
    def _():
        cp = pltpu.make_async_copy(acc, acc_out, sems.at[2])
        cp.start()
        cp.wait()


def _experts(te, tr, xc, xs, slots, comb, wgu, wd):
    n_tok = xc.shape[0] + xs.shape[0] + 8
    n_tiles = te.shape[0]
    tm = MOE_TILE
    nch = D_MODEL // LANES
    vm = pltpu.VMEM
    grid_spec = pltpu.PrefetchScalarGridSpec(
        num_scalar_prefetch=2,
        grid=(n_tiles + 3,),
        in_specs=[pl.BlockSpec(memory_space=pl.ANY), pl.BlockSpec(memory_space=pl.ANY),
                  pl.BlockSpec(memory_space=pl.ANY),
                  pl.BlockSpec((n_tok, LANES), lambda s, *_: (0, 0), pipeline_mode=pl.Buffered(1)),
                  pl.BlockSpec((1, D_MODEL, 2 * EXPERT_DIM), lambda s, te, *_: (te[jnp.clip(s - 1, 0, n_tiles - 1)], 0, 0)),
                  pl.BlockSpec((1, EXPERT_DIM, D_MODEL), lambda s, te, *_: (te[jnp.clip(s - 2, 0, n_tiles - 1)], 0, 0))],
        out_specs=pl.BlockSpec(memory_space=pl.ANY),
        scratch_shapes=[pltpu.SMEM(slots.shape, jnp.int32),
                        vm((n_tok, nch, LANES), BF16), vm((n_tok, nch, LANES), F32),
                        vm((tm * nch, LANES), F32), vm((tm * nch, LANES), F32),
                        vm((tm, LANES), F32), vm((tm, LANES), F32),
                        vm((tm, EXPERT_DIM), BF16), vm((tm, EXPERT_DIM), BF16),
                        vm((nch * Y_PITCH, LANES), F32), vm((nch * Y_PITCH, LANES), F32),
                        pltpu.SemaphoreType.DMA((4,))],
    )
    return pl.pallas_call(
        _experts_kernel,
        grid_spec=grid_spec,
        out_shape=jax.ShapeDtypeStruct((n_tok, nch, LANES), F32),
        compiler_params=pltpu.CompilerParams(dimension_semantics=("arbitrary",),
                                             vmem_limit_bytes=EXPERTS_VMEM_LIMIT),
        name="experts",
    )(te, tr, xc, xs, slots, comb, wgu, wd)


def _final_kernel(acc_ref, h_ref, x1_ref, mod_ref, wsg_ref, wsd_ref, gpo_ref, out_ref):
    routed = _token_rows_from_slabs(lambda g: acc_ref[g * 8:(g + 1) * 8], acc_ref.shape[0])
    gs = _dot(h_ref[...], wsg_ref[...])
    act = _silu(gs[:, :SHARED_DIM]) * gs[:, SHARED_DIM:]
    f = routed + _dot(act.astype(BF16), wsd_ref[...])
    out_ref[...] = x1_ref[...] + mod_ref[0][5:6] * _rmsnorm(f, gpo_ref[...])


def _final(acc, tile0, h2, x1, mod, mod_row, wsg, wsd, gpo):
    n = h2.shape[0]
    tm = FINAL_TILE
    const = lambda *shape: pl.BlockSpec(shape, lambda i: (0,) * len(shape))
    tok = lambda w: pl.BlockSpec((tm, w), lambda i: (i, 0))
    return pl.pallas_call(
        _final_kernel,
        grid=(n // tm,),
        in_specs=[pl.BlockSpec((tm, D_MODEL // LANES, LANES), lambda i: (i + tile0, 0, 0)),
                  tok(D_MODEL), tok(D_MODEL),
                  pl.BlockSpec((1, N_MOD, D_MODEL), lambda i: (mod_row(i), 0, 0)),
                  const(D_MODEL, 2 * SHARED_DIM), const(SHARED_DIM, D_MODEL), const(1, D_MODEL)],
        out_specs=tok(D_MODEL),
        out_shape=jax.ShapeDtypeStruct((n, D_MODEL), F32),
        compiler_params=_params("arbitrary"),
        name="final",
    )(acc, h2, x1, mod, wsg, wsd, gpo)


def _window_bounds(n, w):
    idx = np.arange(n)
    return np.clip(idx - w // 2, 0, n), np.clip(idx + w - w // 2, 0, n)


def _pool_operators(t, grid):
    mats, invs = [], []
    for w in POOL_WINDOWS:
        if grid:
            rlo, rhi = _window_bounds(t // GRID_W, w)
            clo, chi = _window_bounds(GRID_W, w)
            r = np.arange(t) // GRID_W
            c = np.arange(t) % GRID_W
            m = ((r[None, :] >= rlo[r][:, None]) & (r[None, :] < rhi[r][:, None])
                 & (c[None, :] >= clo[c][:, None]) & (c[None, :] < chi[c][:, None]))
            cnt = (rhi - rlo)[r] * (chi - clo)[c]
        else:
            lo, hi = _window_bounds(t, w)
            sidx = np.arange(t)
            m = (sidx[None, :] >= lo[:, None]) & (sidx[None, :] < hi[:, None])
            cnt = hi - lo
        mats.append(m.astype(np.float32))
        invs.append((1.0 / cnt.astype(np.float64)).astype(np.float32)[:, None])
    return jnp.asarray(np.stack(mats), BF16), jnp.asarray(np.stack(invs), F32)


def kernel(x_prompt, x_sample, state_C, state_n, state_m, c, c_ctx, w_ada, b_ada, g_pre_mix, w_in, b_gate,
           w_pool, pool_scale, w_out, g_post_mix, g_pre_ffn, w_router, b_router, w_expert_gu, w_expert_down,
           w_shared_gu, w_shared_down, g_post_ffn):
    b_ctx = x_prompt.shape[0]
    b_lat = x_sample.shape[0]
    nu = N_DIR * HEADS
    l = 0
    row = lambda a: a[l].reshape(1, -1).astype(F32)

    cvec = jnp.zeros((16, D_MODEL), F32).at[0].set(c_ctx.astype(F32)).at[1:1 + b_lat].set(c.astype(F32))
    mod = _mod_rows(cvec, w_ada[l], b_ada[l]).reshape(16, N_MOD, D_MODEL)

    w_in_l = w_in[l]
    p0 = POOL_WIDTH
    mw = MLSTM_WIDTH
    w_u, w_q, w_k, w_v, w_o = (w_in_l[:, lo:lo + 512] for lo in (0, p0, p0 + mw, p0 + 2 * mw, p0 + 3 * mw))
    wm = jnp.concatenate([w_u, w_k, w_o], axis=1).astype(BF16)
    wt = jnp.concatenate([w_q.T, w_v.T], axis=0).astype(BF16)
    wg_cols = w_in_l[:, p0 + 4 * mw:]
    wg = jnp.pad(jnp.concatenate(_split2(wg_cols), axis=1), ((0, 0), (0, LANES - 2 * GATE_COLS)))
    bg = b_gate[l].reshape(GATE_COLS).astype(F32)
    bgr = jnp.pad(bg, (0, LANES - GATE_COLS)).reshape(1, LANES)
    wpl = w_pool[l].astype(BF16)
    zg = jnp.zeros((POOL_GROUP_DIM, POOL_GROUP_DIM), BF16)
    wp = jnp.stack([jnp.block([[wpl[2 * p], zg], [zg, wpl[2 * p + 1]]]) for p in range(POOL_GROUPS // 2)])
    wo = w_out[l].astype(BF16)
    wr = jnp.pad(jnp.concatenate(_split3(w_router[l].astype(F32)), axis=1), ((0, 0), (0, 2 * LANES - 3 * N_EXPERTS)))
    wsg = w_shared_gu[l].astype(BF16)
    wsd = w_shared_down[l].astype(BF16)

    def mixer(x, mod_row, grid, s0, m0, emit_state):
        t = x.shape[1]
        u, k, o, qt, vt, gate, gate_t = _inproj(x.astype(F32), mod, mod_row, row(g_pre_mix), wm, wt, wg, bgr)
        outs = _mlstm(k, qt, vt, gate, gate_t, s0, m0, emit_state)
        hf, hb = outs[0], outs[1]
        pm, pinv = _pool_operators(t, grid)
        x1, h2, xp, lg = _outproj(x.astype(F32), u, hf, hb, o, mod, mod_row, pm, pinv, wp, row(pool_scale), wo,
                                  row(g_post_mix), row(g_pre_ffn), wr)
        return x1, h2, xp, lg, outs[2:]

    ctx_row = lambda i: 0
    lat_row = lambda i: i + 1
    x1c, h2c, xpc, lgc, (c_new, n_new, m_new) = mixer(x_prompt, ctx_row, False, None, None, True)
    s0 = jnp.concatenate(
        [jnp.swapaxes(state_C[:, l].reshape(b_lat, nu, HEAD_DIM, HEAD_DIM).astype(F32), -1, -2),
         jnp.broadcast_to(state_n[:, l].reshape(b_lat, nu, 1, HEAD_DIM).astype(F32),
                          (b_lat, nu, N_ROWS, HEAD_DIM))], axis=-2)
    m0 = jnp.broadcast_to(state_m[:, l].reshape(b_lat, nu, 1, 1).astype(F32), (b_lat, nu, 1, LANES))
    x1s, h2s, xps, lgs, _ = mixer(x_sample, lat_row, True, s0, m0, False)

    tc = b_ctx * x_prompt.shape[1]
    ts = b_lat * x_sample.shape[1]
    n_tok = tc + ts
    lg_all = jnp.concatenate([lgc, lgs], axis=1)
    comb_tok, sel = _router(lg_all, b_router[l].astype(F32))
    n_tiles = n_tok * TOP_K // MOE_TILE + N_EXPERTS
    n_meta = -(-n_tiles // LANES) * LANES
    pos, meta = _plan(sel, n_meta)

    slab = (D_MODEL // LANES, LANES)
    acc = _experts(meta[0, :n_tiles], meta[1, :n_tiles],
                   xpc.reshape((tc,) + slab), xps.reshape((ts,) + slab),
                   _slot_table(pos, n_tiles * MOE_TILE, n_tok), comb_tok,
                   w_expert_gu[l], w_expert_down[l])

    fin = functools.partial(_final, wsg=wsg, wsd=wsd, gpo=row(g_post_ffn))
    tiles_per_lat = x_sample.shape[1] // FINAL_TILE
    yc = fin(acc, 0, h2c.reshape(tc, D_MODEL), x1c.reshape(tc, D_MODEL), mod, ctx_row)
    ys = fin(acc, tc // FINAL_TILE, h2s.reshape(ts, D_MODEL), x1s.reshape(ts, D_MODEL), mod,
             lambda i: i // tiles_per_lat + 1)

    new_c = c_new.reshape(b_ctx, 1, N_DIR, HEADS, HEAD_DIM, HEAD_DIM)
    new_n = n_new.reshape(b_ctx, 1, N_DIR, HEADS, HEAD_DIM)
    new_m = m_new[..., 0].reshape(b_ctx, 1, N_DIR, HEADS)
    return (yc.reshape(x_prompt.shape), ys.reshape(x_sample.shape), new_c, new_n, new_m)
```

```python
import functools

import jax
import jax.numpy as jnp
import numpy as np
from jax import lax
from jax.experimental import pallas as pl
from jax.experimental.pallas import tpu as pltpu
from jax.experimental.pallas import tpu_sc as plsc

F32 = jnp.float32
BF16 = jnp.bfloat16

D_MODEL = 1024
GRID_W = 64
POOL_WIDTH = 512
POOL_GROUPS = 4
POOL_GROUP_DIM = 128
POOL_WINDOWS = (2, 4, 8, 16)
HEADS = 4
HEAD_DIM = 128
MLSTM_WIDTH = HEADS * HEAD_DIM
N_DIR = 2
GATE_COLS = N_DIR * 2 * HEADS
N_EXPERTS = 64
TOP_K = 6
N_EXPERT_GROUPS = 8
GROUP_SIZE = N_EXPERTS // N_EXPERT_GROUPS
TOPK_GROUPS = 4
EXPERT_DIM = 256
SHARED_DIM = 256
ROUTED_SCALE = 2.5
N_MOD = 6
EPS = 1e-6
K_SCALE = HEAD_DIM ** -0.5

LANES = 128
CHUNK = 256
N_ROWS = 16
TOKEN_TILE = 256
FINAL_TILE = 512
MOE_TILE = 320
X_PITCH = 9
Y_PITCH = MOE_TILE + 4
VMEM_LIMIT = 56 * 1024 * 1024
EXPERTS_VMEM_LIMIT = 58 * 1024 * 1024
SC_CORES = 2
SC_SUBCORES = 16
SC_LANES = 16


def _split3(x):
    p1 = x.astype(BF16)
    r1 = x - p1.astype(F32)
    p2 = r1.astype(BF16)
    p3 = (r1 - p2.astype(F32)).astype(BF16)
    return p1, p2, p3


def _split2(x):
    p1 = x.astype(BF16)
    p2 = (x - p1.astype(F32)).astype(BF16)
    return p1, p2


def _dot(a, b):
    return jnp.dot(a, b, preferred_element_type=F32)


def _dot_nt(a, b):
    return lax.dot_general(a, b, (((1,), (1,)), ((), ())), preferred_element_type=F32)


def _rmsnorm(x, g):
    return x * lax.rsqrt(jnp.mean(x * x, axis=-1, keepdims=True) + EPS) * g


def _silu(x):
    return x * jax.nn.sigmoid(x)


def _token_rows_from_slabs(read_block, n_tok):
    nch = D_MODEL // LANES
    cols = [[] for _ in range(nch)]
    for g in range(n_tok // 8):
        blk = jnp.swapaxes(read_block(g), 0, 1)
        for cc in range(nch):
            cols[cc].append(blk[cc])
    return jnp.concatenate([jnp.concatenate(c, axis=0) for c in cols], axis=1)


def _params(*sem):
    return pltpu.CompilerParams(dimension_semantics=sem, vmem_limit_bytes=VMEM_LIMIT)


def _mod_kernel(c_ref, w_ref, b_ref, o_ref):
    a = _silu(c_ref[...])
    a_stack = jnp.concatenate(_split3(a), axis=0)
    w1, w2 = _split2(w_ref[...])
    r1 = _dot(a_stack, w1)
    r2 = _dot(a_stack[:32], w2)
    o_ref[...] = (r1[0:16] + r1[16:32] + r1[32:48] + r2[0:16] + r2[16:32]) + b_ref[...]


def _mod_rows(cvec, w_ada, b_ada):
    n = N_MOD * D_MODEL
    tn = 1536
    return pl.pallas_call(
        _mod_kernel,
        grid=(n // tn,),
        in_specs=[pl.BlockSpec((16, D_MODEL), lambda j: (0, 0)),
                  pl.BlockSpec((D_MODEL, tn), lambda j: (0, j)),
                  pl.BlockSpec((1, tn), lambda j: (0, j))],
        out_specs=pl.BlockSpec((16, tn), lambda j: (0, j)),
        out_shape=jax.ShapeDtypeStruct((16, n), F32),
        compiler_params=_params("arbitrary"),
        name="mod",
    )(cvec, w_ada, b_ada.reshape(1, n))


def _inproj_kernel(x_ref, mod_ref, g_ref, wm_ref, wt_ref, wg_ref, bgr_ref,
                   u_ref, k_ref, o_ref, qt_ref, vt_ref, gate_ref, gatet_ref):
    bs, tm, _ = x_ref.shape
    rows = bs * tm
    x = x_ref[...].reshape(rows, D_MODEL)
    mod = mod_ref[0]
    h = _rmsnorm(x, g_ref[...]) * (1.0 + mod[1:2]) + mod[0:1]
    h1, h2, h3 = _split3(h)
    z = _dot(h1, wm_ref[...])
    u_ref[...] = z[:, 0:512].astype(BF16).reshape(bs, tm, 512)
    k_ref[...] = (z[:, 512:1024] * K_SCALE).astype(BF16).reshape(bs, tm, 512)
    o_ref[...] = z[:, 1024:1536].astype(BF16).reshape(bs, tm, 512)
    zt = _dot_nt(wt_ref[...], h1).astype(BF16)
    r = _dot(jnp.concatenate([h1, h2, h3], axis=0), wg_ref[...])
    r12 = r[0:rows] + r[rows:2 * rows]
    gate = (r12 + r[2 * rows:]) + pltpu.roll(r12, LANES - GATE_COLS, axis=1) + bgr_ref[...]
    gate_ref[...] = gate.reshape(bs, tm, LANES)
    gate_t = gate.T
    for bb in range(bs):
        cols = slice(bb * tm, (bb + 1) * tm)
        qt_ref[bb] = zt[0:512, cols]
        vt_ref[bb] = zt[512:1024, cols]
        gatet_ref[bb] = gate_t[0:16, cols]


def _inproj(x, mod, mod_row, g, wm, wt, wg, bgr):
    b, t, _ = x.shape
    tm = min(t, 2 * TOKEN_TILE)
    bs = 2 * TOKEN_TILE // tm if mod_row(1) == mod_row(0) else 1
    const = lambda *shape: pl.BlockSpec(shape, lambda i, j: (0,) * len(shape))
    tok = lambda w: pl.BlockSpec((bs, tm, w), lambda i, j: (i, j, 0))
    tok_t = lambda r: pl.BlockSpec((bs, r, tm), lambda i, j: (i, 0, j))
    sd = jax.ShapeDtypeStruct
    return pl.pallas_call(
        _inproj_kernel,
        grid=(b // bs, t // tm),
        in_specs=[tok(D_MODEL),
                  pl.BlockSpec((1, N_MOD, D_MODEL), lambda i, j: (mod_row(i * bs), 0, 0)),
                  const(1, D_MODEL), const(D_MODEL, 1536), const(1024, D_MODEL),
                  const(D_MODEL, LANES), const(1, LANES)],
        out_specs=[tok(512), tok(512), tok(512), tok_t(512), tok_t(512), tok(LANES), tok_t(16)],
        out_shape=[sd((b, t, 512), BF16), sd((b, t, 512), BF16), sd((b, t, 512), BF16),
                   sd((b, 512, t), BF16), sd((b, 512, t), BF16), sd((b, t, LANES), F32),
                   sd((b, 16, t), F32)],
        compiler_params=_params("arbitrary", "arbitrary"),
        name="inproj",
    )(x, mod, g, wm, wt, wg, bgr)


def _log_sigmoid(x):
    return jnp.minimum(x, 0.0) - jnp.log1p(jnp.exp(-jnp.abs(x)))


def _scan_unit(st, k, qt, vt, u_col, u_row, b_row, btot, mask, s_prev, m_prev, use_state):
    dh = HEAD_DIM
    n = st.shape[0]
    ub = jnp.where(mask, jnp.broadcast_to(u_col, (n, n)), -jnp.inf)
    z = jnp.maximum(m_prev, jnp.max(ub, axis=0, keepdims=True))
    p = (jnp.exp(ub - z) * st).astype(BF16)
    ones = jnp.ones((N_ROWS, n), BF16)
    tot = _dot(jnp.concatenate([vt, ones], axis=0), p)
    if use_state:
        tot = tot + jnp.exp(m_prev - z) * _dot(s_prev.astype(BF16), qt)
    floor = jnp.exp(-(b_row + z))
    h_t = tot[:dh] / jnp.maximum(jnp.abs(tot[dh:dh + 1]), floor)
    g_row = btot + u_row
    m_new = jnp.maximum(btot + m_prev, jnp.max(g_row, axis=-1, keepdims=True))
    w_row = jnp.exp(g_row - m_new)
    vw = jnp.concatenate([(vt.astype(F32) * w_row).astype(BF16),
                          jnp.broadcast_to(w_row, (N_ROWS, n)).astype(BF16)], axis=0)
    s_new = jnp.exp(btot + m_prev - m_new) * s_prev + _dot(vw, k)
    return h_t.T, s_new, m_new


def _mlstm_kernel(*refs, nc, zero_init, emit_state):
    it = iter(refs)
    fwd_refs = tuple(next(it) for _ in range(5))
    bwd_refs = tuple(next(it) for _ in range(5)) if nc > 1 else fwd_refs
    if not zero_init:
        s0_ref, m0_ref = next(it), next(it)
    h_refs = (next(it), next(it))
    if emit_state:
        c_out, n_out, m_out = next(it), next(it), next(it)
    s_scr, m_scr = next(it), next(it)

    j = pl.program_id(1)
    n = CHUNK
    dh = HEAD_DIM

    @pl.when(j == 0)
    def _():
        if zero_init:
            s_scr[...] = jnp.zeros_like(s_scr)
            m_scr[...] = jnp.zeros_like(m_scr)
        else:
            s_scr[...] = s0_ref[0]
            m_scr[...] = m0_ref[0]

    rows = lax.broadcasted_iota(jnp.int32, (n, n), 0)
    cols = lax.broadcasted_iota(jnp.int32, (n, n), 1)
    le = rows <= cols
    ge = rows >= cols
    tri_le = le.astype(BF16)
    tri_ge = ge.astype(BF16)
    use_state = not (zero_init and nc == 1)

    def gate_terms(d):
        g_ref, gt_ref = (fwd_refs, bwd_refs)[d][3:5]
        gate = g_ref[0]
        gate_t = gt_ref[0]
        lf = _log_sigmoid(gate)
        lf_t = _log_sigmoid(gate_t)
        tri_c, tri_r = (tri_ge, tri_le) if d == 0 else (tri_le, tri_ge)
        bc = _dot(tri_c, jnp.concatenate(_split3(lf), axis=1))
        b_cols = bc[:, 0:128] + bc[:, 128:256] + bc[:, 256:384]
        br = _dot(jnp.concatenate(_split3(lf_t), axis=0), tri_r)
        b_rows = br[0:16] + br[16:32] + br[32:48]
        return gate, gate_t, b_cols, b_rows, jnp.sum(lf_t, axis=-1, keepdims=True)

    terms = [gate_terms(0), gate_terms(1)]
    hs = ([], [])
    for hd in range(HEADS):
        hsl = slice(hd * dh, (hd + 1) * dh)
        st = None
        for d in range(N_DIR):
            k_ref, qt_ref, vt_ref = (fwd_refs, bwd_refs)[d][0:3]
            gate, gate_t, b_cols, b_rows, tot_rows = terms[d]
            ci = d * 8 + hd
            cf = d * 8 + 4 + hd
            unit = d * HEADS + hd
            k = k_ref[0, :, hsl]
            qt = qt_ref[0, hsl, :]
            if st is None or nc > 1:
                st = _dot(k, qt)
            mask = le if d == 0 else ge
            h, s_new, m_new = _scan_unit(
                st, k, qt, vt_ref[0, hsl, :],
                gate[:, ci:ci + 1] - b_cols[:, cf:cf + 1],
                gate_t[ci:ci + 1, :] - b_rows[cf:cf + 1, :],
                b_rows[cf:cf + 1, :], tot_rows[cf:cf + 1, :],
                mask, s_scr[unit], m_scr[unit][:, 0:1], use_state)
            s_scr[unit] = s_new
            m_scr[unit] = jnp.broadcast_to(m_new, (1, LANES))
            hs[d].append(h)
    for d in range(N_DIR):
        h_refs[d][0] = jnp.concatenate(hs[d], axis=1).astype(BF16)

    if emit_state:
        @pl.when(j == nc - 1)
        def _():
            for unit in range(N_DIR * HEADS):
                s = s_scr[unit]
                c_out[0, unit] = s[:dh].T
                n_out[0, unit] = s[dh:dh + 1]
                m_out[0, unit] = m_scr[unit]


def _mlstm(k, qt, vt, gate, gate_t, s0, m0, emit_state):
    b, t, _ = k.shape
    nc = t // CHUNK
    zero_init = s0 is None
    nu = N_DIR * HEADS
    fwd = lambda w: pl.BlockSpec((1, CHUNK, w), lambda i, j: (i, j, 0))
    bwd = lambda w: pl.BlockSpec((1, CHUNK, w), lambda i, j: (i, nc - 1 - j, 0))
    fwd_t = lambda r: pl.BlockSpec((1, r, CHUNK), lambda i, j: (i, 0, j))
    bwd_t = lambda r: pl.BlockSpec((1, r, CHUNK), lambda i, j: (i, 0, nc - 1 - j))
    args = [k, qt, vt, gate, gate_t]
    in_specs = [fwd(512), fwd_t(512), fwd_t(512), fwd(LANES), fwd_t(16)]
    if nc > 1:
        args += [k, qt, vt, gate, gate_t]
        in_specs += [bwd(512), bwd_t(512), bwd_t(512), bwd(LANES), bwd_t(16)]
    if not zero_init:
        args += [s0, m0]
        in_specs += [pl.BlockSpec((1, nu, HEAD_DIM + N_ROWS, HEAD_DIM), lambda i, j: (i, 0, 0, 0)),
                     pl.BlockSpec((1, nu, 1, LANES), lambda i, j: (i, 0, 0, 0))]
    sd = jax.ShapeDtypeStruct
    out_shape = [sd((b, t, 512), BF16), sd((b, t, 512), BF16)]
    out_specs = [fwd(512), bwd(512)]
    if emit_state:
        out_shape += [sd((b, nu, HEAD_DIM, HEAD_DIM), F32), sd((b, nu, 1, HEAD_DIM), F32),
                      sd((b, nu, 1, LANES), F32)]
        out_specs += [pl.BlockSpec((1, nu, HEAD_DIM, HEAD_DIM), lambda i, j: (i, 0, 0, 0)),
                      pl.BlockSpec((1, nu, 1, HEAD_DIM), lambda i, j: (i, 0, 0, 0)),
                      pl.BlockSpec((1, nu, 1, LANES), lambda i, j: (i, 0, 0, 0))]
    return pl.pallas_call(
        functools.partial(_mlstm_kernel, nc=nc, zero_init=zero_init, emit_state=emit_state),
        grid=(b, nc),
        in_specs=in_specs,
        out_specs=out_specs,
        out_shape=out_shape,
        scratch_shapes=[pltpu.VMEM((nu, HEAD_DIM + N_ROWS, HEAD_DIM), F32),
                        pltpu.VMEM((nu, 1, LANES), F32)],
        compiler_params=_params("arbitrary", "arbitrary"),
        name="mlstm",
    )(*args)


def _outproj_kernel(x_ref, u_ref, hf_ref, hb_ref, o_ref, mod_ref, pm_ref, pinv_ref, wp_ref, ps_ref,
                    wo_ref, gpm_ref, gpf_ref, wr_ref, x1_ref, h2_ref, xp_ref, lg_ref):
    bs, tm, _ = x_ref.shape
    rows = bs * tm
    x = x_ref[...].reshape(rows, D_MODEL)
    mod = mod_ref[0]
    row0 = pl.multiple_of(pl.program_id(1) * tm, tm)
    diffs = []
    for g in range(POOL_GROUPS):
        sl = slice(g * POOL_GROUP_DIM, (g + 1) * POOL_GROUP_DIM)
        per_seq = []
        for bb in range(bs):
            box = _dot(pm_ref[g], u_ref[bb, :, sl])
            per_seq.append(box * pinv_ref[g] - u_ref[bb, pl.ds(row0, tm), sl].astype(F32))
        diffs.append(jnp.concatenate(per_seq, axis=0).astype(BF16))
    yps = [_dot(jnp.concatenate(diffs[2 * p:2 * p + 2], axis=1), wp_ref[p]) for p in range(POOL_GROUPS // 2)]
    y_pool = jnp.concatenate(yps, axis=1) * ps_ref[...]
    seq_rows = lambda ref: ref[...].reshape(rows, ref.shape[-1]).astype(F32)
    y_ml = jax.nn.sigmoid(seq_rows(o_ref)) * (seq_rows(hf_ref) + seq_rows(hb_ref))
    mix = _dot(jnp.concatenate([y_pool, y_ml], axis=1).astype(BF16), wo_ref[...])
    x1 = x + mod[2:3] * _rmsnorm(mix, gpm_ref[...])
    x1_ref[...] = x1.reshape(bs, tm, D_MODEL)
    h2 = _rmsnorm(x1, gpf_ref[...]) * (1.0 + mod[4:5]) + mod[3:4]
    p1, p2, p3 = _split3(h2)
    h2_ref[...] = p1.reshape(bs, tm, D_MODEL)
    nch = D_MODEL // LANES
    for g in range(rows // 8):
        cols = jnp.stack([h2[g * 8:(g + 1) * 8, cc * LANES:(cc + 1) * LANES] for cc in range(nch)], axis=0)
        bb, r0 = divmod(g * 8, tm)
        xp_ref[bb, r0:r0 + 8] = jnp.swapaxes(cols, 0, 1).astype(BF16)
    r = _dot(jnp.concatenate([p1, p2, p3], axis=0), wr_ref[...])
    r12 = r[0:rows] + r[rows:2 * rows]
    ne = N_EXPERTS
    lg = (r12 + r[2 * rows:])[:, 0:ne] + r12[:, ne:2 * ne] + r[0:rows, 2 * ne:3 * ne]
    lg_ref[...] = jnp.concatenate([lg, jnp.zeros_like(lg)], axis=1).T[0:ne]


def _outproj(x, u, hf, hb, o, mod, mod_row, pm, pinv, wp, ps, wo, gpm, gpf, wr):
    b, t, _ = x.shape
    tm = min(t, 2 * TOKEN_TILE)
    bs = 2 * TOKEN_TILE // tm if mod_row(1) == mod_row(0) else 1
    const = lambda *shape: pl.BlockSpec(shape, lambda i, j: (0,) * len(shape))
    tok = lambda w: pl.BlockSpec((bs, tm, w), lambda i, j: (i, j, 0))
    sd = jax.ShapeDtypeStruct
    return pl.pallas_call(
        _outproj_kernel,
        grid=(b // bs, t // tm),
        in_specs=[tok(D_MODEL),
                  pl.BlockSpec((bs, t, 512), lambda i, j: (i, 0, 0)),
                  tok(512), tok(512), tok(512),
                  pl.BlockSpec((1, N_MOD, D_MODEL), lambda i, j: (mod_row(i * bs), 0, 0)),
                  pl.BlockSpec((POOL_GROUPS, tm, t), lambda i, j: (0, j, 0)),
                  pl.BlockSpec((POOL_GROUPS, tm, 1), lambda i, j: (0, j, 0)),
                  const(POOL_GROUPS // 2, 2 * POOL_GROUP_DIM, 2 * POOL_GROUP_DIM), const(1, POOL_WIDTH),
                  const(D_MODEL, D_MODEL), const(1, D_MODEL), const(1, D_MODEL),
                  const(D_MODEL, 2 * LANES)],
        out_specs=[tok(D_MODEL), tok(D_MODEL),
                   pl.BlockSpec((bs, tm, D_MODEL // LANES, LANES), lambda i, j: (i, j, 0, 0)),
                   pl.BlockSpec((N_EXPERTS, bs * tm), lambda i, j: (0, i * (t // tm) + j))],
        out_shape=[sd((b, t, D_MODEL), F32), sd((b, t, D_MODEL), BF16),
                   sd((b, t, D_MODEL // LANES, LANES), BF16), sd((N_EXPERTS, b * t), F32)],
        compiler_params=_params("arbitrary", "arbitrary"),
        name="outproj",
    )(x, u, hf, hb, o, mod, pm, pinv, wp, ps, wo, gpm, gpf, wr)


def _router_kernel(lg_ref, br_ref, comb_ref, sel_ref, *, n_blocks):
    ng, gs = N_EXPERT_GROUPS, GROUP_SIZE
    neg = -jnp.inf
    lg = jnp.swapaxes(lg_ref[...], 0, 1)
    br = br_ref[...]
    s = [jax.nn.sigmoid(lg[j]) for j in range(gs)]
    biased = [s[j] + br[:, j, :] for j in range(gs)]
    fold = lambda op, xs: functools.reduce(op, xs)
    m1 = fold(jnp.maximum, biased)
    i1 = fold(jnp.minimum, [jnp.where(biased[j] == m1, j, gs) for j in range(gs)])
    m2 = fold(jnp.maximum, [jnp.where(i1 == j, neg, biased[j]) for j in range(gs)])
    cur = m1 + m2
    gi = lax.broadcasted_iota(jnp.int32, cur.shape, 0)
    gmask = jnp.zeros(cur.shape, F32)
    for _ in range(TOPK_GROUPS):
        mx = jnp.max(cur, axis=0, keepdims=True)
        ix = jnp.min(jnp.where(cur == mx, gi, ng), axis=0, keepdims=True)
        hit = gi == ix
        gmask = jnp.where(hit, 1.0, gmask)
        cur = jnp.where(hit, neg, cur)
    cand = [jnp.where(gmask > 0, biased[j], neg) for j in range(gs)]
    eidx = [gi * gs + j for j in range(gs)]
    selm = [jnp.zeros(cur.shape, F32) for _ in range(gs)]
    for _ in range(TOP_K):
        mx = jnp.max(fold(jnp.maximum, cand), axis=0, keepdims=True)
        ix = jnp.min(fold(jnp.minimum, [jnp.where(cand[j] == mx, eidx[j], N_EXPERTS) for j in range(gs)]),
                     axis=0, keepdims=True)
        for j in range(gs):
            hit = eidx[j] == ix
            selm[j] = jnp.where(hit, 1.0, selm[j])
            cand[j] = jnp.where(hit, neg, cand[j])
    sel = [selm[j] * s[j] for j in range(gs)]
    tot = jnp.sum(fold(jnp.add, sel), axis=0, keepdims=True)
    comb = [sel[j] / tot * ROUTED_SCALE for j in range(gs)]
    sel_ref[...] = jnp.swapaxes(jnp.stack(selm, axis=0), 0, 1)
    comb_e = jnp.swapaxes(jnp.stack(comb, axis=0), 0, 1).reshape(N_EXPERTS, -1)
    comb_t = jnp.concatenate([comb_e, jnp.zeros_like(comb_e)], axis=0).T
    comb_ref[...] = jnp.where(pl.program_id(0) < n_blocks, comb_t, 0.0)


def _router(logits_t, b_router):
    t = logits_t.shape[1]
    tl = 1024
    nb = t // tl
    shp = (N_EXPERT_GROUPS, GROUP_SIZE, t)
    blk = pl.BlockSpec((N_EXPERT_GROUPS, GROUP_SIZE, tl), lambda j: (0, 0, jnp.minimum(j, nb - 1)))
    comb, sel = pl.pallas_call(
        functools.partial(_router_kernel, n_blocks=nb),
        grid=(nb + 1,),
        in_specs=[blk, pl.BlockSpec((N_EXPERT_GROUPS, GROUP_SIZE, 1), lambda j: (0, 0, 0))],
        out_specs=[pl.BlockSpec((tl, LANES), lambda j: (j, 0)), blk],
        out_shape=[jax.ShapeDtypeStruct((t + tl, LANES), F32), jax.ShapeDtypeStruct(shp, F32)],
        compiler_params=_params("arbitrary"),
        name="router",
    )(logits_t.reshape(shp), b_router.reshape(N_EXPERT_GROUPS, GROUP_SIZE, 1))
    return comb, sel.reshape(N_EXPERTS, t)


def _plan_kernel(sel_ref, pos_ref, meta_ref, *, n_meta):
    t = sel_ref.shape[1]
    tm = float(MOE_TILE)
    sel = sel_ref[...]
    selb = sel.astype(BF16)
    blk = 256
    rr = lax.broadcasted_iota(jnp.int32, (blk, blk), 0)
    cc = lax.broadcasted_iota(jnp.int32, (blk, blk), 1)
    before = (rr < cc).astype(BF16)
    carry = jnp.zeros((N_EXPERTS, 1), F32)
    ranks = []
    for b in range(t // blk):
        sb = selb[:, b * blk:(b + 1) * blk]
        ranks.append(_dot(sb, before) + carry)
        carry = carry + jnp.sum(sel[:, b * blk:(b + 1) * blk], axis=1, keepdims=True)
    rank = jnp.concatenate(ranks, axis=1)
    cnt = carry
    ntile = jnp.floor((cnt + (tm - 0.5)) * (1.0 / tm))
    er = lax.broadcasted_iota(jnp.int32, (N_EXPERTS, N_EXPERTS), 0)
    ec = lax.broadcasted_iota(jnp.int32, (N_EXPERTS, N_EXPERTS), 1)
    below = (ec < er).astype(BF16)
    tstart = _dot(below, jnp.broadcast_to(ntile, (N_EXPERTS, LANES)).astype(BF16))[:, 0:1]
    pos = tstart * tm + rank
    erank = _dot(below, selb)
    rows = []
    for k in range(TOP_K):
        hit = (sel > 0.0) & (erank == float(k))
        rows.append(jnp.sum(jnp.where(hit, pos, 0.0), axis=0, keepdims=True))
    rows += [jnp.zeros((1, t), F32)] * (8 - TOP_K)
    pos_ref[...] = jnp.concatenate(rows, axis=0).astype(jnp.int32)

    tau = lax.broadcasted_iota(jnp.int32, (N_EXPERTS, n_meta), 1).astype(F32)
    eidx = lax.broadcasted_iota(jnp.int32, (N_EXPERTS, n_meta), 0).astype(F32)
    te = jnp.sum(((tstart + ntile) <= tau).astype(F32), axis=0, keepdims=True)
    te = jnp.minimum(te, float(N_EXPERTS - 1))
    onehot = eidx == te
    cnt_t = jnp.sum(jnp.where(onehot, cnt, 0.0), axis=0, keepdims=True)
    ts_t = jnp.sum(jnp.where(onehot, tstart, 0.0), axis=0, keepdims=True)
    tr = jnp.clip(cnt_t - (tau[0:1] - ts_t) * tm, 0.0, tm)
    tf = jnp.where((tau[0:1] == ts_t) & (tr > 0.0), 1.0, 0.0)
    meta_ref[...] = jnp.concatenate([te, tr, tf] + [jnp.zeros((1, n_meta), F32)] * 5, axis=0).astype(jnp.int32)


def _plan(sel, n_meta):
    t = sel.shape[1]
    sd = jax.ShapeDtypeStruct
    return pl.pallas_call(
        functools.partial(_plan_kernel, n_meta=n_meta),
        out_shape=[sd((8, t), jnp.int32), sd((8, n_meta), jnp.int32)],
        compiler_params=pltpu.CompilerParams(vmem_limit_bytes=VMEM_LIMIT),
        name="plan",
    )(sel)


def _slot_table(pos, n_slots, n_tokens):
    t = pos.shape[1]
    workers = SC_CORES * SC_SUBCORES
    per = -(-n_slots // (workers * SC_LANES)) * SC_LANES
    mesh = plsc.VectorSubcoreMesh(core_axis_name="core", subcore_axis_name="subcore")

    def body(pos_hbm, out_hbm, pos_v, tbl_v):
        lo = (lax.axis_index("core") * SC_SUBCORES + lax.axis_index("subcore")) * per
        pltpu.sync_copy(pos_hbm.at[pl.ds(0, TOP_K * t)], pos_v)
        dummy = jnp.full((SC_LANES,), n_tokens, jnp.int32)

        @plsc.parallel_loop(0, per, SC_LANES, unroll=8)
        def _(i):
            tbl_v[pl.ds(i, SC_LANES)] = dummy

        lane = lax.iota(jnp.int32, SC_LANES)
        for k in range(TOP_K):
            @plsc.parallel_loop(0, t, SC_LANES, unroll=8)
            def _(i):
                p = pos_v[pl.ds(k * t + i, SC_LANES)] - lo
                mine = (p >= 0) & (p < per)
                plsc.store_scatter(tbl_v, [jnp.where(mine, p, 0)], lane + i, mask=mine)
        pltpu.sync_copy(tbl_v, out_hbm.at[pl.ds(lo, per)])

    build = pl.kernel(body, out_type=jax.ShapeDtypeStruct((workers * per,), jnp.int32), mesh=mesh,
                      scratch_types=[pltpu.VMEM((TOP_K * t,), jnp.int32), pltpu.VMEM((per,), jnp.int32)],
                      compiler_params=pltpu.CompilerParams(needs_layout_passes=False), name="slot_table")
    return build(pos.reshape(-1))


def _experts_kernel(te_ref, tr_ref, xc_ref, xs_ref, slots_ref, comb_ref, wgu_ref, wd_ref, acc_out,
                    tbl_ref, xbuf, acc, stage0, stage1, cst0, cst1, act0, act1, ybuf0, ybuf1, sems):
    s = pl.program_id(0)
    n_tiles = te_ref.shape[0]
    tm = MOE_TILE
    nch = D_MODEL // LANES
    tile_at = lambda lag: jnp.clip(s - lag, 0, n_tiles - 1)
    t_g, t_1, t_2, t_3 = tile_at(0), tile_at(1), tile_at(2), tile_at(3)

    @pl.when(s == 0)
    def _():
        tc, ts = xc_ref.shape[0], xs_ref.shape[0]
        copies = (pltpu.make_async_copy(xc_ref, xbuf.at[pl.ds(0, tc)], sems.at[0]),
                  pltpu.make_async_copy(xs_ref, xbuf.at[pl.ds(tc, ts)], sems.at[1]))
        tbl_init = pltpu.make_async_copy(slots_ref, tbl_ref, sems.at[3])
        tbl_init.start()
        for cp in copies:
            cp.start()
        n_pad = xbuf.shape[0] - tc - ts
        xbuf[pl.ds(tc + ts, n_pad)] = jnp.zeros((n_pad,) + xbuf.shape[1:], BF16)
        for ref in (acc, stage0, stage1, cst0, cst1, act0, act1, ybuf0, ybuf1):
            ref[...] = jnp.zeros_like(ref)
        tbl_init.wait()
        for cp in copies:
            cp.wait()

    def gather(tile, stage, cst):
        base = tile * tm
        for j in range(tm):
            tok = tbl_ref[base + j]
            stage[pl.ds(j * X_PITCH, nch), :] = xbuf[tok].astype(F32)
            cst[pl.ds(j, 1), :] = comb_ref[pl.ds(tok, 1), :]

    def gate_up(tile, stage, cst, act):
        xb = jnp.concatenate(
            [jnp.concatenate([stage[pl.ds(g * 8 * X_PITCH + cc, 8, stride=X_PITCH), :] for g in range(tm // 8)], axis=0)
             for cc in range(nch)], axis=1).astype(BF16)
        gu = _dot(xb, wgu_ref[0].astype(BF16))
        lane = lax.broadcasted_iota(jnp.int32, (1, LANES), 1)
        w_col = jnp.sum(jnp.where(lane == te_ref[tile], cst[...], 0.0), axis=1, keepdims=True)
        act[...] = (_silu(gu[:, :EXPERT_DIM]) * gu[:, EXPERT_DIM:] * w_col).astype(BF16)

    def down(act, ybuf):
        y = _dot(act[...], wd_ref[0].astype(BF16))
        for cc in range(nch):
            ybuf[cc * Y_PITCH:cc * Y_PITCH + tm, :] = y[:, cc * LANES:(cc + 1) * LANES]

    def scatter(tile, ybuf):
        base = tile * tm
        sc_n = 16
        for i in range(tm // sc_n):
            toks = [tbl_ref[base + i * sc_n + u] for u in range(sc_n)]
            olds = [acc[toks[u]] for u in range(sc_n)]
            news = [olds[u] + ybuf[pl.ds(i * sc_n + u, nch, stride=Y_PITCH), :] for u in range(sc_n)]
            for u in range(sc_n):
                acc[toks[u]] = news[u]

    busy = (tr_ref[t_g] + tr_ref[t_1] + tr_ref[t_2] + tr_ref[t_3]) > 0
    bufs = ((stage0, cst0, act0, ybuf0), (stage1, cst1, act1, ybuf1))
    for par in range(2):
        stage_p, cst_p, act_p, ybuf_p = bufs[par]
        stage_q, cst_q, act_q, ybuf_q = bufs[1 - par]

        @pl.when(busy & (s % 2 == par))
        def _():
            gather(t_g, stage_p, cst_p)
            gate_up(t_1, stage_q, cst_q, act_q)
            down(act_p, ybuf_p)
            scatter(t_3, ybuf_q)

    @pl.when(s == pl.num_programs(0) - 1)
    def _():
        cp = pltpu.make_async_copy(acc, acc_out, sems.at[2])
        cp.start()
        cp.wait()


def _experts(te, tr, xc, xs, slots, comb, wgu, wd):
    n_tok = xc.shape[0] + xs.shape[0] + 8
    n_tiles = te.shape[0]
    tm = MOE_TILE
    nch = D_MODEL // LANES
    vm = pltpu.VMEM
    grid_spec = pltpu.PrefetchScalarGridSpec(
        num_scalar_prefetch=2,
        grid=(n_tiles + 3,),
        in_specs=[pl.BlockSpec(memory_space=pl.ANY), pl.BlockSpec(memory_space=pl.ANY),
                  pl.BlockSpec(memory_space=pl.ANY),
                  pl.BlockSpec((n_tok, LANES), lambda s, *_: (0, 0), pipeline_mode=pl.Buffered(1)),
                  pl.BlockSpec((1, D_MODEL, 2 * EXPERT_DIM), lambda s, te, *_: (te[jnp.clip(s - 1, 0, n_tiles - 1)], 0, 0)),
                  pl.BlockSpec((1, EXPERT_DIM, D_MODEL), lambda s, te, *_: (te[jnp.clip(s - 2, 0, n_tiles - 1)], 0, 0))],
        out_specs=pl.BlockSpec(memory_space=pl.ANY),
        scratch_shapes=[pltpu.SMEM(slots.shape, jnp.int32),
                        vm((n_tok, nch, LANES), BF16), vm((n_tok, nch, LANES), F32),
                        vm((tm * X_PITCH, LANES), F32), vm((tm * X_PITCH, LANES), F32),
                        vm((tm, LANES), F32), vm((tm, LANES), F32),
                        vm((tm, EXPERT_DIM), BF16), vm((tm, EXPERT_DIM), BF16),
                        vm((nch * Y_PITCH, LANES), F32), vm((nch * Y_PITCH, LANES), F32),
                        pltpu.SemaphoreType.DMA((4,))],
    )
    return pl.pallas_call(
        _experts_kernel,
        grid_spec=grid_spec,
        out_shape=jax.ShapeDtypeStruct((n_tok, nch, LANES), F32),
        compiler_params=pltpu.CompilerParams(dimension_semantics=("arbitrary",),
                                             vmem_limit_bytes=EXPERTS_VMEM_LIMIT),
        name="experts",
    )(te, tr, xc, xs, slots, comb, wgu, wd)


def _final_kernel(acc_ref, h_ref, x1_ref, mod_ref, wsg_ref, wsd_ref, gpo_ref, out_ref):
    routed = _token_rows_from_slabs(lambda g: acc_ref[g * 8:(g + 1) * 8], acc_ref.shape[0])
    gs = _dot(h_ref[...], wsg_ref[...])
    act = _silu(gs[:, :SHARED_DIM]) * gs[:, SHARED_DIM:]
    f = routed + _dot(act.astype(BF16), wsd_ref[...])
    out_ref[...] = x1_ref[...] + mod_ref[0][5:6] * _rmsnorm(f, gpo_ref[...])


def _final(acc, tile0, h2, x1, mod, mod_row, wsg, wsd, gpo):
    n = h2.shape[0]
    tm = FINAL_TILE
    const = lambda *shape: pl.BlockSpec(shape, lambda i: (0,) * len(shape))
    tok = lambda w: pl.BlockSpec((tm, w), lambda i: (i, 0))
    return pl.pallas_call(
        _final_kernel,
        grid=(n // tm,),
        in_specs=[pl.BlockSpec((tm, D_MODEL // LANES, LANES), lambda i: (i + tile0, 0, 0)),
                  tok(D_MODEL), tok(D_MODEL),
                  pl.BlockSpec((1, N_MOD, D_MODEL), lambda i: (mod_row(i), 0, 0)),
                  const(D_MODEL, 2 * SHARED_DIM), const(SHARED_DIM, D_MODEL), const(1, D_MODEL)],
        out_specs=tok(D_MODEL),
        out_shape=jax.ShapeDtypeStruct((n, D_MODEL), F32),
        compiler_params=_params("arbitrary"),
        name="final",
    )(acc, h2, x1, mod, wsg, wsd, gpo)


def _window_bounds(n, w):
    idx = np.arange(n)
    return np.clip(idx - w // 2, 0, n), np.clip(idx + w - w // 2, 0, n)


def _pool_operators(t, grid):
    mats, invs = [], []
    for w in POOL_WINDOWS:
        if grid:
            rlo, rhi = _window_bounds(t // GRID_W, w)
            clo, chi = _window_bounds(GRID_W, w)
            r = np.arange(t) // GRID_W
            c = np.arange(t) % GRID_W
            m = ((r[None, :] >= rlo[r][:, None]) & (r[None, :] < rhi[r][:, None])
                 & (c[None, :] >= clo[c][:, None]) & (c[None, :] < chi[c][:, None]))
            cnt = (rhi - rlo)[r] * (chi - clo)[c]
        else:
            lo, hi = _window_bounds(t, w)
            sidx = np.arange(t)
            m = (sidx[None, :] >= lo[:, None]) & (sidx[None, :] < hi[:, None])
            cnt = hi - lo
        mats.append(m.astype(np.float32))
        invs.append((1.0 / cnt.astype(np.float64)).astype(np.float32)[:, None])
    return jnp.asarray(np.stack(mats), BF16), jnp.asarray(np.stack(invs), F32)


def kernel(x_prompt, x_sample, state_C, state_n, state_m, c, c_ctx, w_ada, b_ada, g_pre_mix, w_in, b_gate,
           w_pool, pool_scale, w_out, g_post_mix, g_pre_ffn, w_router, b_router, w_expert_gu, w_expert_down,
           w_shared_gu, w_shared_down, g_post_ffn):
    b_ctx = x_prompt.shape[0]
    b_lat = x_sample.shape[0]
    nu = N_DIR * HEADS
    l = 0
    row = lambda a: a[l].reshape(1, -1).astype(F32)

    cvec = jnp.zeros((16, D_MODEL), F32).at[0].set(c_ctx.astype(F32)).at[1:1 + b_lat].set(c.astype(F32))
    mod = _mod_rows(cvec, w_ada[l], b_ada[l]).reshape(16, N_MOD, D_MODEL)

    w_in_l = w_in[l]
    p0 = POOL_WIDTH
    mw = MLSTM_WIDTH
    w_u, w_q, w_k, w_v, w_o = (w_in_l[:, lo:lo + 512] for lo in (0, p0, p0 + mw, p0 + 2 * mw, p0 + 3 * mw))
    wm = jnp.concatenate([w_u, w_k, w_o], axis=1).astype(BF16)
    wt = jnp.concatenate([w_q.T, w_v.T], axis=0).astype(BF16)
    wg_cols = w_in_l[:, p0 + 4 * mw:]
    wg = jnp.pad(jnp.concatenate(_split2(wg_cols), axis=1), ((0, 0), (0, LANES - 2 * GATE_COLS)))
    bg = b_gate[l].reshape(GATE_COLS).astype(F32)
    bgr = jnp.pad(bg, (0, LANES - GATE_COLS)).reshape(1, LANES)
    wpl = w_pool[l].astype(BF16)
    zg = jnp.zeros((POOL_GROUP_DIM, POOL_GROUP_DIM), BF16)
    wp = jnp.stack([jnp.block([[wpl[2 * p], zg], [zg, wpl[2 * p + 1]]]) for p in range(POOL_GROUPS // 2)])
    wo = w_out[l].astype(BF16)
    wr = jnp.pad(jnp.concatenate(_split3(w_router[l].astype(F32)), axis=1), ((0, 0), (0, 2 * LANES - 3 * N_EXPERTS)))
    wsg = w_shared_gu[l].astype(BF16)
    wsd = w_shared_down[l].astype(BF16)

    def mixer(x, mod_row, grid, s0, m0, emit_state):
        t = x.shape[1]
        u, k, o, qt, vt, gate, gate_t = _inproj(x.astype(F32), mod, mod_row, row(g_pre_mix), wm, wt, wg, bgr)
        outs = _mlstm(k, qt, vt, gate, gate_t, s0, m0, emit_state)
        hf, hb = outs[0], outs[1]
        pm, pinv = _pool_operators(t, grid)
        x1, h2, xp, lg = _outproj(x.astype(F32), u, hf, hb, o, mod, mod_row, pm, pinv, wp, row(pool_scale), wo,
                                  row(g_post_mix), row(g_pre_ffn), wr)
        return x1, h2, xp, lg, outs[2:]

    ctx_row = lambda i: 0
    lat_row = lambda i: i + 1
    x1c, h2c, xpc, lgc, (c_new, n_new, m_new) = mixer(x_prompt, ctx_row, False, None, None, True)
    s0 = jnp.concatenate(
        [jnp.swapaxes(state_C[:, l].reshape(b_lat, nu, HEAD_DIM, HEAD_DIM).astype(F32), -1, -2),
         jnp.broadcast_to(state_n[:, l].reshape(b_lat, nu, 1, HEAD_DIM).astype(F32),
                          (b_lat, nu, N_ROWS, HEAD_DIM))], axis=-2)
    m0 = jnp.broadcast_to(state_m[:, l].reshape(b_lat, nu, 1, 1).astype(F32), (b_lat, nu, 1, LANES))
    x1s, h2s, xps, lgs, _ = mixer(x_sample, lat_row, True, s0, m0, False)

    tc = b_ctx * x_prompt.shape[1]
    ts = b_lat * x_sample.shape[1]
    n_tok = tc + ts
    lg_all = jnp.concatenate([lgc, lgs], axis=1)
    comb_tok, sel = _router(lg_all, b_router[l].astype(F32))
    n_tiles = n_tok * TOP_K // MOE_TILE + N_EXPERTS
    n_meta = -(-n_tiles // LANES) * LANES
    pos, meta = _plan(sel, n_meta)

    slab = (D_MODEL // LANES, LANES)
    acc = _experts(meta[0, :n_tiles], meta[1, :n_tiles],
                   xpc.reshape((tc,) + slab), xps.reshape((ts,) + slab),
                   _slot_table(pos, n_tiles * MOE_TILE, n_tok), comb_tok,
                   w_expert_gu[l], w_expert_down[l])

    fin = functools.partial(_final, wsg=wsg, wsd=wsd, gpo=row(g_post_ffn))
    tiles_per_lat = x_sample.shape[1] // FINAL_TILE
    yc = fin(acc, 0, h2c.reshape(tc, D_MODEL), x1c.reshape(tc, D_MODEL), mod, ctx_row)
    ys = fin(acc, tc // FINAL_TILE, h2s.reshape(ts, D_MODEL), x1s.reshape(ts, D_MODEL), mod,
             lambda i: i // tiles_per_lat + 1)

    new_c = c_new.reshape(b_ctx, 1, N_DIR, HEADS, HEAD_DIM, HEAD_DIM)
    new_n = n_new.reshape(b_ctx, 1, N_DIR, HEADS, HEAD_DIM)
    new_m = m_new[..., 0].reshape(b_ctx, 1, N_DIR, HEADS)
    return (yc.reshape(x_prompt.shape), ys.reshape(x_sample.shape), new_c, new_n, new_m)
```

```python
import functools

import jax
import jax.numpy as jnp
import numpy as np
from jax import lax
from jax.experimental import pallas as pl
from jax.experimental.pallas import tpu as pltpu
from jax.experimental.pallas import tpu_sc as plsc

F32 = jnp.float32
BF16 = jnp.bfloat16

D_MODEL = 1024
GRID_W = 64
POOL_WIDTH = 512
POOL_GROUPS = 4
POOL_GROUP_DIM = 128
POOL_WINDOWS = (2, 4, 8, 16)
HEADS = 4
HEAD_DIM = 128
MLSTM_WIDTH = HEADS * HEAD_DIM
N_DIR = 2
GATE_COLS = N_DIR * 2 * HEADS
N_EXPERTS = 64
TOP_K = 6
N_EXPERT_GROUPS = 8
GROUP_SIZE = N_EXPERTS // N_EXPERT_GROUPS
TOPK_GROUPS = 4
EXPERT_DIM = 256
SHARED_DIM = 256
ROUTED_SCALE = 2.5
N_MOD = 6
EPS = 1e-6
K_SCALE = HEAD_DIM ** -0.5

LANES = 128
CHUNK = 256
N_ROWS = 16
TOKEN_TILE = 256
FINAL_TILE = 512
MOE_TILE = 320
WEIGHT_RING = 3
X_PITCH = 9
Y_PITCH = MOE_TILE + 4
VMEM_LIMIT = 56 * 1024 * 1024
EXPERTS_VMEM_LIMIT = 58 * 1024 * 1024
SC_CORES = 2
SC_SUBCORES = 16
SC_LANES = 16


def _split3(x):
    p1 = x.astype(BF16)
    r1 = x - p1.astype(F32)
    p2 = r1.astype(BF16)
    p3 = (r1 - p2.astype(F32)).astype(BF16)
    return p1, p2, p3


def _split2(x):
    p1 = x.astype(BF16)
    p2 = (x - p1.astype(F32)).astype(BF16)
    return p1, p2


def _dot(a, b):
    return jnp.dot(a, b, preferred_element_type=F32)


def _dot_nt(a, b):
    return lax.dot_general(a, b, (((1,), (1,)), ((), ())), preferred_element_type=F32)


def _rmsnorm(x, g):
    return x * lax.rsqrt(jnp.mean(x * x, axis=-1, keepdims=True) + EPS) * g


def _silu(x):
    return x * jax.nn.sigmoid(x)


def _token_rows_from_slabs(read_block, n_tok):
    nch = D_MODEL // LANES
    cols = [[] for _ in range(nch)]
    for g in range(n_tok // 8):
        blk = jnp.swapaxes(read_block(g), 0, 1)
        for cc in range(nch):
            cols[cc].append(blk[cc])
    return jnp.concatenate([jnp.concatenate(c, axis=0) for c in cols], axis=1)


def _params(*sem):
    return pltpu.CompilerParams(dimension_semantics=sem, vmem_limit_bytes=VMEM_LIMIT)


def _mod_kernel(c_ref, w_ref, b_ref, o_ref):
    a = _silu(c_ref[...])
    a_stack = jnp.concatenate(_split3(a), axis=0)
    w1, w2 = _split2(w_ref[...])
    r1 = _dot(a_stack, w1)
    r2 = _dot(a_stack[:32], w2)
    o_ref[...] = (r1[0:16] + r1[16:32] + r1[32:48] + r2[0:16] + r2[16:32]) + b_ref[...]


def _mod_rows(cvec, w_ada, b_ada):
    n = N_MOD * D_MODEL
    tn = 1536
    return pl.pallas_call(
        _mod_kernel,
        grid=(n // tn,),
        in_specs=[pl.BlockSpec((16, D_MODEL), lambda j: (0, 0)),
                  pl.BlockSpec((D_MODEL, tn), lambda j: (0, j)),
                  pl.BlockSpec((1, tn), lambda j: (0, j))],
        out_specs=pl.BlockSpec((16, tn), lambda j: (0, j)),
        out_shape=jax.ShapeDtypeStruct((16, n), F32),
        compiler_params=_params("arbitrary"),
        name="mod",
    )(cvec, w_ada, b_ada.reshape(1, n))


def _inproj_kernel(x_ref, mod_ref, g_ref, wm_ref, wt_ref, wg_ref, bgr_ref,
                   u_ref, k_ref, o_ref, qt_ref, vt_ref, gate_ref, gatet_ref):
    bs, tm, _ = x_ref.shape
    rows = bs * tm
    x = x_ref[...].reshape(rows, D_MODEL)
    mod = mod_ref[0]
    h = _rmsnorm(x, g_ref[...]) * (1.0 + mod[1:2]) + mod[0:1]
    h1, h2, h3 = _split3(h)
    z = _dot(h1, wm_ref[...])
    u_ref[...] = z[:, 0:512].astype(BF16).reshape(bs, tm, 512)
    k_ref[...] = (z[:, 512:1024] * K_SCALE).astype(BF16).reshape(bs, tm, 512)
    o_ref[...] = z[:, 1024:1536].astype(BF16).reshape(bs, tm, 512)
    zt = _dot_nt(wt_ref[...], h1).astype(BF16)
    r = _dot(jnp.concatenate([h1, h2, h3], axis=0), wg_ref[...])
    r12 = r[0:rows] + r[rows:2 * rows]
    gate = (r12 + r[2 * rows:]) + pltpu.roll(r12, LANES - GATE_COLS, axis=1) + bgr_ref[...]
    gate_ref[...] = gate.reshape(bs, tm, LANES)
    gate_t = gate.T
    for bb in range(bs):
        cols = slice(bb * tm, (bb + 1) * tm)
        qt_ref[bb] = zt[0:512, cols]
        vt_ref[bb] = zt[512:1024, cols]
        gatet_ref[bb] = gate_t[0:16, cols]


def _inproj(x, mod, mod_row, g, wm, wt, wg, bgr):
    b, t, _ = x.shape
    tm = min(t, 2 * TOKEN_TILE)
    bs = 2 * TOKEN_TILE // tm if mod_row(1) == mod_row(0) else 1
    const = lambda *shape: pl.BlockSpec(shape, lambda i, j: (0,) * len(shape))
    tok = lambda w: pl.BlockSpec((bs, tm, w), lambda i, j: (i, j, 0))
    tok_t = lambda r: pl.BlockSpec((bs, r, tm), lambda i, j: (i, 0, j))
    sd = jax.ShapeDtypeStruct
    return pl.pallas_call(
        _inproj_kernel,
        grid=(b // bs, t // tm),
        in_specs=[tok(D_MODEL),
                  pl.BlockSpec((1, N_MOD, D_MODEL), lambda i, j: (mod_row(i * bs), 0, 0)),
                  const(1, D_MODEL), const(D_MODEL, 1536), const(1024, D_MODEL),
                  const(D_MODEL, LANES), const(1, LANES)],
        out_specs=[tok(512), tok(512), tok(512), tok_t(512), tok_t(512), tok(LANES), tok_t(16)],
        out_shape=[sd((b, t, 512), BF16), sd((b, t, 512), BF16), sd((b, t, 512), BF16),
                   sd((b, 512, t), BF16), sd((b, 512, t), BF16), sd((b, t, LANES), F32),
                   sd((b, 16, t), F32)],
        compiler_params=_params("arbitrary", "arbitrary"),
        name="inproj",
    )(x, mod, g, wm, wt, wg, bgr)


def _log_sigmoid(x):
    return jnp.minimum(x, 0.0) - jnp.log1p(jnp.exp(-jnp.abs(x)))


def _scan_unit(st, k, qt, vt, u_col, u_row, b_row, btot, mask, s_prev, m_prev, use_state):
    dh = HEAD_DIM
    n = st.shape[0]
    ub = jnp.where(mask, jnp.broadcast_to(u_col, (n, n)), -jnp.inf)
    z = jnp.maximum(m_prev, jnp.max(ub, axis=0, keepdims=True))
    p = (jnp.exp(ub - z) * st).astype(BF16)
    ones = jnp.ones((N_ROWS, n), BF16)
    tot = _dot(jnp.concatenate([vt, ones], axis=0), p)
    if use_state:
        tot = tot + jnp.exp(m_prev - z) * _dot(s_prev.astype(BF16), qt)
    floor = jnp.exp(-(b_row + z))
    h_t = tot[:dh] / jnp.maximum(jnp.abs(tot[dh:dh + 1]), floor)
    g_row = btot + u_row
    m_new = jnp.maximum(btot + m_prev, jnp.max(g_row, axis=-1, keepdims=True))
    w_row = jnp.exp(g_row - m_new)
    vw = jnp.concatenate([(vt.astype(F32) * w_row).astype(BF16),
                          jnp.broadcast_to(w_row, (N_ROWS, n)).astype(BF16)], axis=0)
    s_new = jnp.exp(btot + m_prev - m_new) * s_prev + _dot(vw, k)
    return h_t.T, s_new, m_new


def _mlstm_kernel(*refs, nc, zero_init, emit_state):
    it = iter(refs)
    fwd_refs = tuple(next(it) for _ in range(5))
    bwd_refs = tuple(next(it) for _ in range(5)) if nc > 1 else fwd_refs
    if not zero_init:
        s0_ref, m0_ref = next(it), next(it)
    h_refs = (next(it), next(it))
    if emit_state:
        c_out, n_out, m_out = next(it), next(it), next(it)
    s_scr, m_scr = next(it), next(it)

    j = pl.program_id(1)
    n = CHUNK
    dh = HEAD_DIM

    @pl.when(j == 0)
    def _():
        if zero_init:
            s_scr[...] = jnp.zeros_like(s_scr)
            m_scr[...] = jnp.zeros_like(m_scr)
        else:
            s_scr[...] = s0_ref[0]
            m_scr[...] = m0_ref[0]

    rows = lax.broadcasted_iota(jnp.int32, (n, n), 0)
    cols = lax.broadcasted_iota(jnp.int32, (n, n), 1)
    le = rows <= cols
    ge = rows >= cols
    tri_le = le.astype(BF16)
    tri_ge = ge.astype(BF16)
    use_state = not (zero_init and nc == 1)

    def gate_terms(d):
        g_ref, gt_ref = (fwd_refs, bwd_refs)[d][3:5]
        gate = g_ref[0]
        gate_t = gt_ref[0]
        lf = _log_sigmoid(gate)
        lf_t = _log_sigmoid(gate_t)
        tri_c, tri_r = (tri_ge, tri_le) if d == 0 else (tri_le, tri_ge)
        bc = _dot(tri_c, jnp.concatenate(_split3(lf), axis=1))
        b_cols = bc[:, 0:128] + bc[:, 128:256] + bc[:, 256:384]
        br = _dot(jnp.concatenate(_split3(lf_t), axis=0), tri_r)
        b_rows = br[0:16] + br[16:32] + br[32:48]
        return gate, gate_t, b_cols, b_rows, jnp.sum(lf_t, axis=-1, keepdims=True)

    terms = [gate_terms(0), gate_terms(1)]
    hs = ([], [])
    for hd in range(HEADS):
        hsl = slice(hd * dh, (hd + 1) * dh)
        st = None
        for d in range(N_DIR):
            k_ref, qt_ref, vt_ref = (fwd_refs, bwd_refs)[d][0:3]
            gate, gate_t, b_cols, b_rows, tot_rows = terms[d]
            ci = d * 8 + hd
            cf = d * 8 + 4 + hd
            unit = d * HEADS + hd
            k = k_ref[0, :, hsl]
            qt = qt_ref[0, hsl, :]
            if st is None or nc > 1:
                st = _dot(k, qt)
            mask = le if d == 0 else ge
            h, s_new, m_new = _scan_unit(
                st, k, qt, vt_ref[0, hsl, :],
                gate[:, ci:ci + 1] - b_cols[:, cf:cf + 1],
                gate_t[ci:ci + 1, :] - b_rows[cf:cf + 1, :],
                b_rows[cf:cf + 1, :], tot_rows[cf:cf + 1, :],
                mask, s_scr[unit], m_scr[unit][:, 0:1], use_state)
            s_scr[unit] = s_new
            m_scr[unit] = jnp.broadcast_to(m_new, (1, LANES))
            hs[d].append(h)
    for d in range(N_DIR):
        h_refs[d][0] = jnp.concatenate(hs[d], axis=1).astype(BF16)

    if emit_state:
        @pl.when(j == nc - 1)
        def _():
            for unit in range(N_DIR * HEADS):
                s = s_scr[unit]
                c_out[0, unit] = s[:dh].T
                n_out[0, unit] = s[dh:dh + 1]
                m_out[0, unit] = m_scr[unit]


def _mlstm(k, qt, vt, gate, gate_t, s0, m0, emit_state):
    b, t, _ = k.shape
    nc = t // CHUNK
    zero_init = s0 is None
    nu = N_DIR * HEADS
    fwd = lambda w: pl.BlockSpec((1, CHUNK, w), lambda i, j: (i, j, 0))
    bwd = lambda w: pl.BlockSpec((1, CHUNK, w), lambda i, j: (i, nc - 1 - j, 0))
    fwd_t = lambda r: pl.BlockSpec((1, r, CHUNK), lambda i, j: (i, 0, j))
    bwd_t = lambda r: pl.BlockSpec((1, r, CHUNK), lambda i, j: (i, 0, nc - 1 - j))
    args = [k, qt, vt, gate, gate_t]
    in_specs = [fwd(512), fwd_t(512), fwd_t(512), fwd(LANES), fwd_t(16)]
    if nc > 1:
        args += [k, qt, vt, gate, gate_t]
        in_specs += [bwd(512), bwd_t(512), bwd_t(512), bwd(LANES), bwd_t(16)]
    if not zero_init:
        args += [s0, m0]
        in_specs += [pl.BlockSpec((1, nu, HEAD_DIM + N_ROWS, HEAD_DIM), lambda i, j: (i, 0, 0, 0)),
                     pl.BlockSpec((1, nu, 1, LANES), lambda i, j: (i, 0, 0, 0))]
    sd = jax.ShapeDtypeStruct
    out_shape = [sd((b, t, 512), BF16), sd((b, t, 512), BF16)]
    out_specs = [fwd(512), bwd(512)]
    if emit_state:
        out_shape += [sd((b, nu, HEAD_DIM, HEAD_DIM), F32), sd((b, nu, 1, HEAD_DIM), F32),
                      sd((b, nu, 1, LANES), F32)]
        out_specs += [pl.BlockSpec((1, nu, HEAD_DIM, HEAD_DIM), lambda i, j: (i, 0, 0, 0)),
                      pl.BlockSpec((1, nu, 1, HEAD_DIM), lambda i, j: (i, 0, 0, 0)),
                      pl.BlockSpec((1, nu, 1, LANES), lambda i, j: (i, 0, 0, 0))]
    return pl.pallas_call(
        functools.partial(_mlstm_kernel, nc=nc, zero_init=zero_init, emit_state=emit_state),
        grid=(b, nc),
        in_specs=in_specs,
        out_specs=out_specs,
        out_shape=out_shape,
        scratch_shapes=[pltpu.VMEM((nu, HEAD_DIM + N_ROWS, HEAD_DIM), F32),
                        pltpu.VMEM((nu, 1, LANES), F32)],
        compiler_params=_params("arbitrary", "arbitrary"),
        name="mlstm",
    )(*args)


def _outproj_kernel(x_ref, u_ref, hf_ref, hb_ref, o_ref, mod_ref, pm_ref, pinv_ref, wp_ref, ps_ref,
                    wo_ref, gpm_ref, gpf_ref, wr_ref, x1_ref, h2_ref, xp_ref, lg_ref):
    bs, tm, _ = x_ref.shape
    rows = bs * tm
    x = x_ref[...].reshape(rows, D_MODEL)
    mod = mod_ref[0]
    row0 = pl.multiple_of(pl.program_id(1) * tm, tm)
    diffs = []
    for g in range(POOL_GROUPS):
        sl = slice(g * POOL_GROUP_DIM, (g + 1) * POOL_GROUP_DIM)
        per_seq = []
        for bb in range(bs):
            box = _dot(pm_ref[g], u_ref[bb, :, sl])
            per_seq.append(box * pinv_ref[g] - u_ref[bb, pl.ds(row0, tm), sl].astype(F32))
        diffs.append(jnp.concatenate(per_seq, axis=0).astype(BF16))
    yps = [_dot(jnp.concatenate(diffs[2 * p:2 * p + 2], axis=1), wp_ref[p]) for p in range(POOL_GROUPS // 2)]
    y_pool = jnp.concatenate(yps, axis=1) * ps_ref[...]
    seq_rows = lambda ref: ref[...].reshape(rows, ref.shape[-1]).astype(F32)
    y_ml = jax.nn.sigmoid(seq_rows(o_ref)) * (seq_rows(hf_ref) + seq_rows(hb_ref))
    mix = _dot(jnp.concatenate([y_pool, y_ml], axis=1).astype(BF16), wo_ref[...])
    x1 = x + mod[2:3] * _rmsnorm(mix, gpm_ref[...])
    x1_ref[...] = x1.reshape(bs, tm, D_MODEL)
    h2 = _rmsnorm(x1, gpf_ref[...]) * (1.0 + mod[4:5]) + mod[3:4]
    p1, p2, p3 = _split3(h2)
    h2_ref[...] = p1.reshape(bs, tm, D_MODEL)
    nch = D_MODEL // LANES
    for g in range(rows // 8):
        cols = jnp.stack([h2[g * 8:(g + 1) * 8, cc * LANES:(cc + 1) * LANES] for cc in range(nch)], axis=0)
        bb, r0 = divmod(g * 8, tm)
        xp_ref[bb, r0:r0 + 8] = jnp.swapaxes(cols, 0, 1).astype(BF16)
    r = _dot(jnp.concatenate([p1, p2, p3], axis=0), wr_ref[...])
    r12 = r[0:rows] + r[rows:2 * rows]
    ne = N_EXPERTS
    lg = (r12 + r[2 * rows:])[:, 0:ne] + r12[:, ne:2 * ne] + r[0:rows, 2 * ne:3 * ne]
    lg_ref[...] = jnp.concatenate([lg, jnp.zeros_like(lg)], axis=1).T[0:ne]


def _outproj(x, u, hf, hb, o, mod, mod_row, pm, pinv, wp, ps, wo, gpm, gpf, wr):
    b, t, _ = x.shape
    tm = min(t, 2 * TOKEN_TILE)
    bs = 2 * TOKEN_TILE // tm if mod_row(1) == mod_row(0) else 1
    const = lambda *shape: pl.BlockSpec(shape, lambda i, j: (0,) * len(shape))
    tok = lambda w: pl.BlockSpec((bs, tm, w), lambda i, j: (i, j, 0))
    sd = jax.ShapeDtypeStruct
    return pl.pallas_call(
        _outproj_kernel,
        grid=(b // bs, t // tm),
        in_specs=[tok(D_MODEL),
                  pl.BlockSpec((bs, t, 512), lambda i, j: (i, 0, 0)),
                  tok(512), tok(512), tok(512),
                  pl.BlockSpec((1, N_MOD, D_MODEL), lambda i, j: (mod_row(i * bs), 0, 0)),
                  pl.BlockSpec((POOL_GROUPS, tm, t), lambda i, j: (0, j, 0)),
                  pl.BlockSpec((POOL_GROUPS, tm, 1), lambda i, j: (0, j, 0)),
                  const(POOL_GROUPS // 2, 2 * POOL_GROUP_DIM, 2 * POOL_GROUP_DIM), const(1, POOL_WIDTH),
                  const(D_MODEL, D_MODEL), const(1, D_MODEL), const(1, D_MODEL),
                  const(D_MODEL, 2 * LANES)],
        out_specs=[tok(D_MODEL), tok(D_MODEL),
                   pl.BlockSpec((bs, tm, D_MODEL // LANES, LANES), lambda i, j: (i, j, 0, 0)),
                   pl.BlockSpec((N_EXPERTS, bs * tm), lambda i, j: (0, i * (t // tm) + j))],
        out_shape=[sd((b, t, D_MODEL), F32), sd((b, t, D_MODEL), BF16),
                   sd((b, t, D_MODEL // LANES, LANES), BF16), sd((N_EXPERTS, b * t), F32)],
        compiler_params=_params("arbitrary", "arbitrary"),
        name="outproj",
    )(x, u, hf, hb, o, mod, pm, pinv, wp, ps, wo, gpm, gpf, wr)


def _router_kernel(lg_ref, br_ref, comb_ref, sel_ref, *, n_blocks):
    ng, gs = N_EXPERT_GROUPS, GROUP_SIZE
    neg = -jnp.inf
    lg = jnp.swapaxes(lg_ref[...], 0, 1)
    br = br_ref[...]
    s = [jax.nn.sigmoid(lg[j]) for j in range(gs)]
    biased = [s[j] + br[:, j, :] for j in range(gs)]
    fold = lambda op, xs: functools.reduce(op, xs)
    m1 = fold(jnp.maximum, biased)
    i1 = fold(jnp.minimum, [jnp.where(biased[j] == m1, j, gs) for j in range(gs)])
    m2 = fold(jnp.maximum, [jnp.where(i1 == j, neg, biased[j]) for j in range(gs)])
    cur = m1 + m2
    gi = lax.broadcasted_iota(jnp.int32, cur.shape, 0)
    gmask = jnp.zeros(cur.shape, F32)
    for _ in range(TOPK_GROUPS):
        mx = jnp.max(cur, axis=0, keepdims=True)
        ix = jnp.min(jnp.where(cur == mx, gi, ng), axis=0, keepdims=True)
        hit = gi == ix
        gmask = jnp.where(hit, 1.0, gmask)
        cur = jnp.where(hit, neg, cur)
    cand = [jnp.where(gmask > 0, biased[j], neg) for j in range(gs)]
    eidx = [gi * gs + j for j in range(gs)]
    selm = [jnp.zeros(cur.shape, F32) for _ in range(gs)]
    for _ in range(TOP_K):
        mx = jnp.max(fold(jnp.maximum, cand), axis=0, keepdims=True)
        ix = jnp.min(fold(jnp.minimum, [jnp.where(cand[j] == mx, eidx[j], N_EXPERTS) for j in range(gs)]),
                     axis=0, keepdims=True)
        for j in range(gs):
            hit = eidx[j] == ix
            selm[j] = jnp.where(hit, 1.0, selm[j])
            cand[j] = jnp.where(hit, neg, cand[j])
    sel = [selm[j] * s[j] for j in range(gs)]
    tot = jnp.sum(fold(jnp.add, sel), axis=0, keepdims=True)
    comb = [sel[j] / tot * ROUTED_SCALE for j in range(gs)]
    sel_ref[...] = jnp.swapaxes(jnp.stack(selm, axis=0), 0, 1)
    comb_e = jnp.swapaxes(jnp.stack(comb, axis=0), 0, 1).reshape(N_EXPERTS, -1)
    comb_t = jnp.concatenate([comb_e, jnp.zeros_like(comb_e)], axis=0).T
    comb_ref[...] = jnp.where(pl.program_id(0) < n_blocks, comb_t, 0.0)


def _router(logits_t, b_router):
    t = logits_t.shape[1]
    tl = 1024
    nb = t // tl
    shp = (N_EXPERT_GROUPS, GROUP_SIZE, t)
    blk = pl.BlockSpec((N_EXPERT_GROUPS, GROUP_SIZE, tl), lambda j: (0, 0, jnp.minimum(j, nb - 1)))
    comb, sel = pl.pallas_call(
        functools.partial(_router_kernel, n_blocks=nb),
        grid=(nb + 1,),
        in_specs=[blk, pl.BlockSpec((N_EXPERT_GROUPS, GROUP_SIZE, 1), lambda j: (0, 0, 0))],
        out_specs=[pl.BlockSpec((tl, LANES), lambda j: (j, 0)), blk],
        out_shape=[jax.ShapeDtypeStruct((t + tl, LANES), F32), jax.ShapeDtypeStruct(shp, F32)],
        compiler_params=_params("arbitrary"),
        name="router",
    )(logits_t.reshape(shp), b_router.reshape(N_EXPERT_GROUPS, GROUP_SIZE, 1))
    return comb, sel.reshape(N_EXPERTS, t)


def _plan_kernel(sel_ref, pos_ref, meta_ref, *, n_meta):
    t = sel_ref.shape[1]
    tm = float(MOE_TILE)
    sel = sel_ref[...]
    selb = sel.astype(BF16)
    blk = 256
    rr = lax.broadcasted_iota(jnp.int32, (blk, blk), 0)
    cc = lax.broadcasted_iota(jnp.int32, (blk, blk), 1)
    before = (rr < cc).astype(BF16)
    carry = jnp.zeros((N_EXPERTS, 1), F32)
    ranks = []
    for b in range(t // blk):
        sb = selb[:, b * blk:(b + 1) * blk]
        ranks.append(_dot(sb, before) + carry)
        carry = carry + jnp.sum(sel[:, b * blk:(b + 1) * blk], axis=1, keepdims=True)
    rank = jnp.concatenate(ranks, axis=1)
    cnt = carry
    ntile = jnp.floor((cnt + (tm - 0.5)) * (1.0 / tm))
    er = lax.broadcasted_iota(jnp.int32, (N_EXPERTS, N_EXPERTS), 0)
    ec = lax.broadcasted_iota(jnp.int32, (N_EXPERTS, N_EXPERTS), 1)
    below = (ec < er).astype(BF16)
    tstart = _dot(below, jnp.broadcast_to(ntile, (N_EXPERTS, LANES)).astype(BF16))[:, 0:1]
    pos = tstart * tm + rank
    erank = _dot(below, selb)
    rows = []
    for k in range(TOP_K):
        hit = (sel > 0.0) & (erank == float(k))
        rows.append(jnp.sum(jnp.where(hit, pos, 0.0), axis=0, keepdims=True))
    rows += [jnp.zeros((1, t), F32)] * (8 - TOP_K)
    pos_ref[...] = jnp.concatenate(rows, axis=0).astype(jnp.int32)

    tau = lax.broadcasted_iota(jnp.int32, (N_EXPERTS, n_meta), 1).astype(F32)
    eidx = lax.broadcasted_iota(jnp.int32, (N_EXPERTS, n_meta), 0).astype(F32)
    te = jnp.sum(((tstart + ntile) <= tau).astype(F32), axis=0, keepdims=True)
    te = jnp.minimum(te, float(N_EXPERTS - 1))
    onehot = eidx == te
    cnt_t = jnp.sum(jnp.where(onehot, cnt, 0.0), axis=0, keepdims=True)
    ts_t = jnp.sum(jnp.where(onehot, tstart, 0.0), axis=0, keepdims=True)
    tr = jnp.clip(cnt_t - (tau[0:1] - ts_t) * tm, 0.0, tm)
    tf = jnp.where((tau[0:1] == ts_t) & (tr > 0.0), 1.0, 0.0)
    owns = jnp.broadcast_to((cnt > 0.0).astype(F32), (N_EXPERTS, LANES)).astype(BF16)
    eord = _dot(below, owns)[:, 0:1]
    ord_t = jnp.sum(jnp.where(onehot, eord, 0.0), axis=0, keepdims=True)
    meta_ref[...] = jnp.concatenate([te, tr, tf, ord_t] + [jnp.zeros((1, n_meta), F32)] * 4, axis=0).astype(jnp.int32)


def _plan(sel, n_meta):
    t = sel.shape[1]
    sd = jax.ShapeDtypeStruct
    return pl.pallas_call(
        functools.partial(_plan_kernel, n_meta=n_meta),
        out_shape=[sd((8, t), jnp.int32), sd((8, n_meta), jnp.int32)],
        compiler_params=pltpu.CompilerParams(vmem_limit_bytes=VMEM_LIMIT),
        name="plan",
    )(sel)


def _slot_table(pos, n_slots, n_tokens):
    t = pos.shape[1]
    workers = SC_CORES * SC_SUBCORES
    per = -(-n_slots // (workers * SC_LANES)) * SC_LANES
    mesh = plsc.VectorSubcoreMesh(core_axis_name="core", subcore_axis_name="subcore")

    def body(pos_hbm, out_hbm, pos_v, tbl_v):
        lo = (lax.axis_index("core") * SC_SUBCORES + lax.axis_index("subcore")) * per
        pltpu.sync_copy(pos_hbm.at[pl.ds(0, TOP_K * t)], pos_v)
        dummy = jnp.full((SC_LANES,), n_tokens, jnp.int32)

        @plsc.parallel_loop(0, per, SC_LANES, unroll=8)
        def _(i):
            tbl_v[pl.ds(i, SC_LANES)] = dummy

        lane = lax.iota(jnp.int32, SC_LANES)
        for k in range(TOP_K):
            @plsc.parallel_loop(0, t, SC_LANES, unroll=8)
            def _(i):
                p = pos_v[pl.ds(k * t + i, SC_LANES)] - lo
                mine = (p >= 0) & (p < per)
                plsc.store_scatter(tbl_v, [jnp.where(mine, p, 0)], lane + i, mask=mine)
        pltpu.sync_copy(tbl_v, out_hbm.at[pl.ds(lo, per)])

    build = pl.kernel(body, out_type=jax.ShapeDtypeStruct((workers * per,), jnp.int32), mesh=mesh,
                      scratch_types=[pltpu.VMEM((TOP_K * t,), jnp.int32), pltpu.VMEM((per,), jnp.int32)],
                      compiler_params=pltpu.CompilerParams(needs_layout_passes=False), name="slot_table")
    return build(pos.reshape(-1))


def _experts_kernel(te_ref, tr_ref, tf_ref, ord_ref, xc_ref, xs_ref, slots_ref, comb_ref, wgu_hbm, wd_hbm, acc_out,
                    tbl_ref, xbuf, acc, stage0, stage1, cst0, cst1, act0, act1, ybuf0, ybuf1, wgu_buf, wd_buf,
                    sems, wsems):
    s = pl.program_id(0)
    n_tiles = te_ref.shape[0]
    tm = MOE_TILE
    nch = D_MODEL // LANES
    tile_at = lambda lag: jnp.clip(s - lag, 0, n_tiles - 1)
    t_g, t_1, t_2, t_3 = tile_at(0), tile_at(1), tile_at(2), tile_at(3)
    ring = wgu_buf.shape[0]

    def wgu_copy(tile):
        slot = ord_ref[tile] % ring
        return pltpu.make_async_copy(wgu_hbm.at[te_ref[tile]], wgu_buf.at[slot], wsems.at[0, slot])

    def wd_copy(tile):
        slot = ord_ref[tile] % ring
        return pltpu.make_async_copy(wd_hbm.at[te_ref[tile]], wd_buf.at[slot], wsems.at[1, slot])

    @pl.when(s == 0)
    def _():
        wgu_copy(0).start()
        wd_copy(0).start()
        tc, ts = xc_ref.shape[0], xs_ref.shape[0]
        copies = (pltpu.make_async_copy(xc_ref, xbuf.at[pl.ds(0, tc)], sems.at[0]),
                  pltpu.make_async_copy(xs_ref, xbuf.at[pl.ds(tc, ts)], sems.at[1]))
        tbl_init = pltpu.make_async_copy(slots_ref, tbl_ref, sems.at[3])
        tbl_init.start()
        for cp in copies:
            cp.start()
        n_pad = xbuf.shape[0] - tc - ts
        xbuf[pl.ds(tc + ts, n_pad)] = jnp.zeros((n_pad,) + xbuf.shape[1:], BF16)
        for ref in (acc, stage0, stage1, cst0, cst1, act0, act1, ybuf0, ybuf1):
            ref[...] = jnp.zeros_like(ref)
        tbl_init.wait()
        for cp in copies:
            cp.wait()
        wgu_copy(0).wait()
        wd_copy(0).wait()

    nxt = jnp.minimum(s + 1, n_tiles - 1)

    @pl.when((s + 1 < n_tiles) & (tf_ref[nxt] == 1))
    def _():
        wgu_copy(nxt).start()

    @pl.when((s >= 1) & (s < n_tiles) & (tf_ref[t_g] == 1))
    def _():
        wd_copy(t_g).start()

    @pl.when((s >= 2) & (s - 1 < n_tiles) & (tf_ref[t_1] == 1))
    def _():
        wgu_copy(t_1).wait()

    @pl.when((s >= 3) & (s - 2 < n_tiles) & (tf_ref[t_2] == 1))
    def _():
        wd_copy(t_2).wait()

    wgu_now = wgu_buf.at[ord_ref[t_1] % ring]
    wd_now = wd_buf.at[ord_ref[t_2] % ring]

    def gather(tile, stage, cst):
        base = tile * tm
        for j in range(tm):
            tok = tbl_ref[base + j]
            stage[pl.ds(j * X_PITCH, nch), :] = xbuf[tok].astype(F32)
            cst[pl.ds(j, 1), :] = comb_ref[pl.ds(tok, 1), :]

    def gate_up(tile, stage, cst, act):
        xb = jnp.concatenate(
            [jnp.concatenate([stage[pl.ds(g * 8 * X_PITCH + cc, 8, stride=X_PITCH), :] for g in range(tm // 8)], axis=0)
             for cc in range(nch)], axis=1).astype(BF16)
        gu = _dot(xb, wgu_now[...].astype(BF16))
        lane = lax.broadcasted_iota(jnp.int32, (1, LANES), 1)
        w_col = jnp.sum(jnp.where(lane == te_ref[tile], cst[...], 0.0), axis=1, keepdims=True)
        act[...] = (_silu(gu[:, :EXPERT_DIM]) * gu[:, EXPERT_DIM:] * w_col).astype(BF16)

    def down(act, ybuf):
        y = _dot(act[...], wd_now[...].astype(BF16))
        for cc in range(nch):
            ybuf[cc * Y_PITCH:cc * Y_PITCH + tm, :] = y[:, cc * LANES:(cc + 1) * LANES]

    def scatter(tile, ybuf):
        base = tile * tm
        sc_n = 16
        for i in range(tm // sc_n):
            toks = [tbl_ref[base + i * sc_n + u] for u in range(sc_n)]
            olds = [acc[toks[u]] for u in range(sc_n)]
            news = [olds[u] + ybuf[pl.ds(i * sc_n + u, nch, stride=Y_PITCH), :] for u in range(sc_n)]
            for u in range(sc_n):
                acc[toks[u]] = news[u]

    busy = (tr_ref[t_g] + tr_ref[t_1] + tr_ref[t_2] + tr_ref[t_3]) > 0
    bufs = ((stage0, cst0, act0, ybuf0), (stage1, cst1, act1, ybuf1))
    for par in range(2):
        stage_p, cst_p, act_p, ybuf_p = bufs[par]
        stage_q, cst_q, act_q, ybuf_q = bufs[1 - par]

        @pl.when(busy & (s % 2 == par))
        def _():
            gather(t_g, stage_p, cst_p)
            gate_up(t_1, stage_q, cst_q, act_q)
            down(act_p, ybuf_p)
            scatter(t_3, ybuf_q)

    @pl.when(s == pl.num_programs(0) - 1)
    def _():
        cp = pltpu.make_async_copy(acc, acc_out, sems.at[2])
        cp.start()
        cp.wait()


def _experts(te, tr, tf, eord, xc, xs, slots, comb, wgu, wd):
    n_tok = xc.shape[0] + xs.shape[0] + 8
    n_tiles = te.shape[0]
    tm = MOE_TILE
    nch = D_MODEL // LANES
    vm = pltpu.VMEM
    grid_spec = pltpu.PrefetchScalarGridSpec(
        num_scalar_prefetch=4,
        grid=(n_tiles + 3,),
        in_specs=[pl.BlockSpec(memory_space=pl.ANY), pl.BlockSpec(memory_space=pl.ANY),
                  pl.BlockSpec(memory_space=pl.ANY),
                  pl.BlockSpec((n_tok, LANES), lambda s, *_: (0, 0), pipeline_mode=pl.Buffered(1)),
                  pl.BlockSpec(memory_space=pl.ANY), pl.BlockSpec(memory_space=pl.ANY)],
        out_specs=pl.BlockSpec(memory_space=pl.ANY),
        scratch_shapes=[pltpu.SMEM(slots.shape, jnp.int32),
                        vm((n_tok, nch, LANES), BF16), vm((n_tok, nch, LANES), F32),
                        vm((tm * X_PITCH, LANES), F32), vm((tm * X_PITCH, LANES), F32),
                        vm((tm, LANES), F32), vm((tm, LANES), F32),
                        vm((tm, EXPERT_DIM), BF16), vm((tm, EXPERT_DIM), BF16),
                        vm((nch * Y_PITCH, LANES), F32), vm((nch * Y_PITCH, LANES), F32),
                        vm((WEIGHT_RING, D_MODEL, 2 * EXPERT_DIM), F32), vm((WEIGHT_RING, EXPERT_DIM, D_MODEL), F32),
                        pltpu.SemaphoreType.DMA((4,)), pltpu.SemaphoreType.DMA((2, WEIGHT_RING))],
    )
    return pl.pallas_call(
        _experts_kernel,
        grid_spec=grid_spec,
        out_shape=jax.ShapeDtypeStruct((n_tok, nch, LANES), F32),
        compiler_params=pltpu.CompilerParams(dimension_semantics=("arbitrary",),
                                             vmem_limit_bytes=EXPERTS_VMEM_LIMIT),
        name="experts",
    )(te, tr, tf, eord, xc, xs, slots, comb, wgu, wd)


def _final_kernel(acc_ref, h_ref, x1_ref, mod_ref, wsg_ref, wsd_ref, gpo_ref, out_ref):
    routed = _token_rows_from_slabs(lambda g: acc_ref[g * 8:(g + 1) * 8], acc_ref.shape[0])
    gs = _dot(h_ref[...], wsg_ref[...])
    act = _silu(gs[:, :SHARED_DIM]) * gs[:, SHARED_DIM:]
    f = routed + _dot(act.astype(BF16), wsd_ref[...])
    out_ref[...] = x1_ref[...] + mod_ref[0][5:6] * _rmsnorm(f, gpo_ref[...])


def _final(acc, tile0, h2, x1, mod, mod_row, wsg, wsd, gpo):
    n = h2.shape[0]
    tm = FINAL_TILE
    const = lambda *shape: pl.BlockSpec(shape, lambda i: (0,) * len(shape))
    tok = lambda w: pl.BlockSpec((tm, w), lambda i: (i, 0))
    return pl.pallas_call(
        _final_kernel,
        grid=(n // tm,),
        in_specs=[pl.BlockSpec((tm, D_MODEL // LANES, LANES), lambda i: (i + tile0, 0, 0)),
                  tok(D_MODEL), tok(D_MODEL),
                  pl.BlockSpec((1, N_MOD, D_MODEL), lambda i: (mod_row(i), 0, 0)),
                  const(D_MODEL, 2 * SHARED_DIM), const(SHARED_DIM, D_MODEL), const(1, D_MODEL)],
        out_specs=tok(D_MODEL),
        out_shape=jax.ShapeDtypeStruct((n, D_MODEL), F32),
        compiler_params=_params("arbitrary"),
        name="final",
    )(acc, h2, x1, mod, wsg, wsd, gpo)


def _window_bounds(n, w):
    idx = np.arange(n)
    return np.clip(idx - w // 2, 0, n), np.clip(idx + w - w // 2, 0, n)


def _pool_operators(t, grid):
    mats, invs = [], []
    for w in POOL_WINDOWS:
        if grid:
            rlo, rhi = _window_bounds(t // GRID_W, w)
            clo, chi = _window_bounds(GRID_W, w)
            r = np.arange(t) // GRID_W
            c = np.arange(t) % GRID_W
            m = ((r[None, :] >= rlo[r][:, None]) & (r[None, :] < rhi[r][:, None])
                 & (c[None, :] >= clo[c][:, None]) & (c[None, :] < chi[c][:, None]))
            cnt = (rhi - rlo)[r] * (chi - clo)[c]
        else:
            lo, hi = _window_bounds(t, w)
            sidx = np.arange(t)
            m = (sidx[None, :] >= lo[:, None]) & (sidx[None, :] < hi[:, None])
            cnt = hi - lo
        mats.append(m.astype(np.float32))
        invs.append((1.0 / cnt.astype(np.float64)).astype(np.float32)[:, None])
    return jnp.asarray(np.stack(mats), BF16), jnp.asarray(np.stack(invs), F32)


def kernel(x_prompt, x_sample, state_C, state_n, state_m, c, c_ctx, w_ada, b_ada, g_pre_mix, w_in, b_gate,
           w_pool, pool_scale, w_out, g_post_mix, g_pre_ffn, w_router, b_router, w_expert_gu, w_expert_down,
           w_shared_gu, w_shared_down, g_post_ffn):
    b_ctx = x_prompt.shape[0]
    b_lat = x_sample.shape[0]
    nu = N_DIR * HEADS
    l = 0
    row = lambda a: a[l].reshape(1, -1).astype(F32)

    cvec = jnp.zeros((16, D_MODEL), F32).at[0].set(c_ctx.astype(F32)).at[1:1 + b_lat].set(c.astype(F32))
    mod = _mod_rows(cvec, w_ada[l], b_ada[l]).reshape(16, N_MOD, D_MODEL)

    w_in_l = w_in[l]
    p0 = POOL_WIDTH
    mw = MLSTM_WIDTH
    w_u, w_q, w_k, w_v, w_o = (w_in_l[:, lo:lo + 512] for lo in (0, p0, p0 + mw, p0 + 2 * mw, p0 + 3 * mw))
    wm = jnp.concatenate([w_u, w_k, w_o], axis=1).astype(BF16)
    wt = jnp.concatenate([w_q.T, w_v.T], axis=0).astype(BF16)
    wg_cols = w_in_l[:, p0 + 4 * mw:]
    wg = jnp.pad(jnp.concatenate(_split2(wg_cols), axis=1), ((0, 0), (0, LANES - 2 * GATE_COLS)))
    bg = b_gate[l].reshape(GATE_COLS).astype(F32)
    bgr = jnp.pad(bg, (0, LANES - GATE_COLS)).reshape(1, LANES)
    wpl = w_pool[l].astype(BF16)
    zg = jnp.zeros((POOL_GROUP_DIM, POOL_GROUP_DIM), BF16)
    wp = jnp.stack([jnp.block([[wpl[2 * p], zg], [zg, wpl[2 * p + 1]]]) for p in range(POOL_GROUPS // 2)])
    wo = w_out[l].astype(BF16)
    wr = jnp.pad(jnp.concatenate(_split3(w_router[l].astype(F32)), axis=1), ((0, 0), (0, 2 * LANES - 3 * N_EXPERTS)))
    wsg = w_shared_gu[l].astype(BF16)
    wsd = w_shared_down[l].astype(BF16)

    def mixer(x, mod_row, grid, s0, m0, emit_state):
        t = x.shape[1]
        u, k, o, qt, vt, gate, gate_t = _inproj(x.astype(F32), mod, mod_row, row(g_pre_mix), wm, wt, wg, bgr)
        outs = _mlstm(k, qt, vt, gate, gate_t, s0, m0, emit_state)
        hf, hb = outs[0], outs[1]
        pm, pinv = _pool_operators(t, grid)
        x1, h2, xp, lg = _outproj(x.astype(F32), u, hf, hb, o, mod, mod_row, pm, pinv, wp, row(pool_scale), wo,
                                  row(g_post_mix), row(g_pre_ffn), wr)
        return x1, h2, xp, lg, outs[2:]

    ctx_row = lambda i: 0
    lat_row = lambda i: i + 1
    x1c, h2c, xpc, lgc, (c_new, n_new, m_new) = mixer(x_prompt, ctx_row, False, None, None, True)
    s0 = jnp.concatenate(
        [jnp.swapaxes(state_C[:, l].reshape(b_lat, nu, HEAD_DIM, HEAD_DIM).astype(F32), -1, -2),
         jnp.broadcast_to(state_n[:, l].reshape(b_lat, nu, 1, HEAD_DIM).astype(F32),
                          (b_lat, nu, N_ROWS, HEAD_DIM))], axis=-2)
    m0 = jnp.broadcast_to(state_m[:, l].reshape(b_lat, nu, 1, 1).astype(F32), (b_lat, nu, 1, LANES))
    x1s, h2s, xps, lgs, _ = mixer(x_sample, lat_row, True, s0, m0, False)

    tc = b_ctx * x_prompt.shape[1]
    ts = b_lat * x_sample.shape[1]
    n_tok = tc + ts
    lg_all = jnp.concatenate([lgc, lgs], axis=1)
    comb_tok, sel = _router(lg_all, b_router[l].astype(F32))
    n_tiles = n_tok * TOP_K // MOE_TILE + N_EXPERTS
    n_meta = -(-n_tiles // LANES) * LANES
    pos, meta = _plan(sel, n_meta)

    slab = (D_MODEL // LANES, LANES)
    acc = _experts(meta[0, :n_tiles], meta[1, :n_tiles], meta[2, :n_tiles], meta[3, :n_tiles],
                   xpc.reshape((tc,) + slab), xps.reshape((ts,) + slab),
                   _slot_table(pos, n_tiles * MOE_TILE, n_tok), comb_tok,
                   w_expert_gu[l], w_expert_down[l])

    fin = functools.partial(_final, wsg=wsg, wsd=wsd, gpo=row(g_post_ffn))
    tiles_per_lat = x_sample.shape[1] // FINAL_TILE
    yc = fin(acc, 0, h2c.reshape(tc, D_MODEL), x1c.reshape(tc, D_MODEL), mod, ctx_row)
    ys = fin(acc, tc // FINAL_TILE, h2s.reshape(ts, D_MODEL), x1s.reshape(ts, D_MODEL), mod,
             lambda i: i // tiles_per_lat + 1)

    new_c = c_new.reshape(b_ctx, 1, N_DIR, HEADS, HEAD_DIM, HEAD_DIM)
    new_n = n_new.reshape(b_ctx, 1, N_DIR, HEADS, HEAD_DIM)
    new_m = m_new[..., 0].reshape(b_ctx, 1, N_DIR, HEADS)
    return (yc.reshape(x_prompt.shape), ys.reshape(x_sample.shape), new_c, new_n, new_m)
```

```python
import functools

import jax
import jax.numpy as jnp
import numpy as np
from jax import lax
from jax.experimental import pallas as pl
from jax.experimental.pallas import tpu as pltpu
from jax.experimental.pallas import tpu_sc as plsc

F32 = jnp.float32
BF16 = jnp.bfloat16

D_MODEL = 1024
GRID_W = 64
POOL_WIDTH = 512
POOL_GROUPS = 4
POOL_GROUP_DIM = 128
POOL_WINDOWS = (2, 4, 8, 16)
HEADS = 4
HEAD_DIM = 128
MLSTM_WIDTH = HEADS * HEAD_DIM
N_DIR = 2
GATE_COLS = N_DIR * 2 * HEADS
N_EXPERTS = 64
TOP_K = 6
N_EXPERT_GROUPS = 8
GROUP_SIZE = N_EXPERTS // N_EXPERT_GROUPS
TOPK_GROUPS = 4
EXPERT_DIM = 256
SHARED_DIM = 256
ROUTED_SCALE = 2.5
N_MOD = 6
EPS = 1e-6
K_SCALE = HEAD_DIM ** -0.5

LANES = 128
CHUNK = 256
N_ROWS = 16
TOKEN_TILE = 256
FINAL_TILE = 512
MOE_TILE = 320
WEIGHT_RING = 3
X_PITCH = 9
Y_PITCH = MOE_TILE + 4
VMEM_LIMIT = 56 * 1024 * 1024
EXPERTS_VMEM_LIMIT = 58 * 1024 * 1024
SC_CORES = 2
SC_SUBCORES = 16
SC_LANES = 16


def _split3(x):
    p1 = x.astype(BF16)
    r1 = x - p1.astype(F32)
    p2 = r1.astype(BF16)
    p3 = (r1 - p2.astype(F32)).astype(BF16)
    return p1, p2, p3


def _split2(x):
    p1 = x.astype(BF16)
    p2 = (x - p1.astype(F32)).astype(BF16)
    return p1, p2


def _dot(a, b):
    return jnp.dot(a, b, preferred_element_type=F32)


def _dot_nt(a, b):
    return lax.dot_general(a, b, (((1,), (1,)), ((), ())), preferred_element_type=F32)


def _rmsnorm(x, g):
    return x * lax.rsqrt(jnp.mean(x * x, axis=-1, keepdims=True) + EPS) * g


def _silu(x):
    return x * jax.nn.sigmoid(x)


def _token_rows_from_slabs(read_block, n_tok):
    nch = D_MODEL // LANES
    cols = [[] for _ in range(nch)]
    for g in range(n_tok // 8):
        blk = jnp.swapaxes(read_block(g), 0, 1)
        for cc in range(nch):
            cols[cc].append(blk[cc])
    return jnp.concatenate([jnp.concatenate(c, axis=0) for c in cols], axis=1)


def _params(*sem):
    return pltpu.CompilerParams(dimension_semantics=sem, vmem_limit_bytes=VMEM_LIMIT)


def _mod_kernel(c_ref, w_ref, b_ref, o_ref):
    a = _silu(c_ref[...])
    a_stack = jnp.concatenate(_split3(a), axis=0)
    w1, w2 = _split2(w_ref[...])
    r1 = _dot(a_stack, w1)
    r2 = _dot(a_stack[:32], w2)
    o_ref[...] = (r1[0:16] + r1[16:32] + r1[32:48] + r2[0:16] + r2[16:32]) + b_ref[...]


def _mod_rows(cvec, w_ada, b_ada):
    n = N_MOD * D_MODEL
    tn = 1536
    return pl.pallas_call(
        _mod_kernel,
        grid=(n // tn,),
        in_specs=[pl.BlockSpec((16, D_MODEL), lambda j: (0, 0)),
                  pl.BlockSpec((D_MODEL, tn), lambda j: (0, j)),
                  pl.BlockSpec((1, tn), lambda j: (0, j))],
        out_specs=pl.BlockSpec((16, tn), lambda j: (0, j)),
        out_shape=jax.ShapeDtypeStruct((16, n), F32),
        compiler_params=_params("arbitrary"),
        name="mod",
    )(cvec, w_ada, b_ada.reshape(1, n))


def _inproj_kernel(x_ref, mod_ref, g_ref, wm_ref, wt_ref, wg_ref, bgr_ref,
                   u_ref, k_ref, o_ref, qt_ref, vt_ref, gate_ref, gatet_ref):
    bs, tm, _ = x_ref.shape
    rows = bs * tm
    x = x_ref[...].reshape(rows, D_MODEL)
    mod = mod_ref[0]
    h = _rmsnorm(x, g_ref[...]) * (1.0 + mod[1:2]) + mod[0:1]
    h1, h2, h3 = _split3(h)
    z = _dot(h1, wm_ref[...])
    u_ref[...] = z[:, 0:512].astype(BF16).reshape(bs, tm, 512)
    k_ref[...] = (z[:, 512:1024] * K_SCALE).astype(BF16).reshape(bs, tm, 512)
    o_ref[...] = z[:, 1024:1536].astype(BF16).reshape(bs, tm, 512)
    zt = _dot_nt(wt_ref[...], h1).astype(BF16)
    r = _dot(jnp.concatenate([h1, h2, h3], axis=0), wg_ref[...])
    r12 = r[0:rows] + r[rows:2 * rows]
    gate = (r12 + r[2 * rows:]) + pltpu.roll(r12, LANES - GATE_COLS, axis=1) + bgr_ref[...]
    gate_ref[...] = gate.reshape(bs, tm, LANES)
    gate_t = gate.T
    for bb in range(bs):
        cols = slice(bb * tm, (bb + 1) * tm)
        qt_ref[bb] = zt[0:512, cols]
        vt_ref[bb] = zt[512:1024, cols]
        gatet_ref[bb] = gate_t[0:16, cols]


def _inproj(x, mod, mod_row, g, wm, wt, wg, bgr):
    b, t, _ = x.shape
    tm = min(t, 2 * TOKEN_TILE)
    bs = 2 * TOKEN_TILE // tm if mod_row(1) == mod_row(0) else 1
    const = lambda *shape: pl.BlockSpec(shape, lambda i, j: (0,) * len(shape))
    tok = lambda w: pl.BlockSpec((bs, tm, w), lambda i, j: (i, j, 0))
    tok_t = lambda r: pl.BlockSpec((bs, r, tm), lambda i, j: (i, 0, j))
    sd = jax.ShapeDtypeStruct
    return pl.pallas_call(
        _inproj_kernel,
        grid=(b // bs, t // tm),
        in_specs=[tok(D_MODEL),
                  pl.BlockSpec((1, N_MOD, D_MODEL), lambda i, j: (mod_row(i * bs), 0, 0)),
                  const(1, D_MODEL), const(D_MODEL, 1536), const(1024, D_MODEL),
                  const(D_MODEL, LANES), const(1, LANES)],
        out_specs=[tok(512), tok(512), tok(512), tok_t(512), tok_t(512), tok(LANES), tok_t(16)],
        out_shape=[sd((b, t, 512), BF16), sd((b, t, 512), BF16), sd((b, t, 512), BF16),
                   sd((b, 512, t), BF16), sd((b, 512, t), BF16), sd((b, t, LANES), F32),
                   sd((b, 16, t), F32)],
        compiler_params=_params("arbitrary", "arbitrary"),
        name="inproj",
    )(x, mod, g, wm, wt, wg, bgr)


def _log_sigmoid(x):
    return jnp.minimum(x, 0.0) - jnp.log1p(jnp.exp(-jnp.abs(x)))


def _scan_unit(st, k, qt, vt, u_col, u_row, b_row, btot, mask, s_prev, m_prev, use_state):
    dh = HEAD_DIM
    n = st.shape[0]
    ub = jnp.where(mask, jnp.broadcast_to(u_col, (n, n)), -jnp.inf)
    z = jnp.maximum(m_prev, jnp.max(ub, axis=0, keepdims=True))
    p = (jnp.exp(ub - z) * st).astype(BF16)
    ones = jnp.ones((N_ROWS, n), BF16)
    tot = _dot(jnp.concatenate([vt, ones], axis=0), p)
    if use_state:
        tot = tot + jnp.exp(m_prev - z) * _dot(s_prev.astype(BF16), qt)
    floor = jnp.exp(-(b_row + z))
    h_t = tot[:dh] / jnp.maximum(jnp.abs(tot[dh:dh + 1]), floor)
    g_row = btot + u_row
    m_new = jnp.maximum(btot + m_prev, jnp.max(g_row, axis=-1, keepdims=True))
    w_row = jnp.exp(g_row - m_new)
    vw = jnp.concatenate([(vt.astype(F32) * w_row).astype(BF16),
                          jnp.broadcast_to(w_row, (N_ROWS, n)).astype(BF16)], axis=0)
    s_new = jnp.exp(btot + m_prev - m_new) * s_prev + _dot(vw, k)
    return h_t.T, s_new, m_new


def _mlstm_kernel(*refs, nc, zero_init, emit_state):
    it = iter(refs)
    fwd_refs = tuple(next(it) for _ in range(5))
    bwd_refs = tuple(next(it) for _ in range(5)) if nc > 1 else fwd_refs
    if not zero_init:
        s0_ref, m0_ref = next(it), next(it)
    h_refs = (next(it), next(it))
    if emit_state:
        c_out, n_out, m_out = next(it), next(it), next(it)
    s_scr, m_scr = next(it), next(it)

    j = pl.program_id(1)
    n = CHUNK
    dh = HEAD_DIM

    @pl.when(j == 0)
    def _():
        if zero_init:
            s_scr[...] = jnp.zeros_like(s_scr)
            m_scr[...] = jnp.zeros_like(m_scr)
        else:
            s_scr[...] = s0_ref[0]
            m_scr[...] = m0_ref[0]

    rows = lax.broadcasted_iota(jnp.int32, (n, n), 0)
    cols = lax.broadcasted_iota(jnp.int32, (n, n), 1)
    le = rows <= cols
    ge = rows >= cols
    tri_le = le.astype(BF16)
    tri_ge = ge.astype(BF16)
    use_state = not (zero_init and nc == 1)

    def gate_terms(d):
        g_ref, gt_ref = (fwd_refs, bwd_refs)[d][3:5]
        gate = g_ref[0]
        gate_t = gt_ref[0]
        lf = _log_sigmoid(gate)
        lf_t = _log_sigmoid(gate_t)
        tri_c, tri_r = (tri_ge, tri_le) if d == 0 else (tri_le, tri_ge)
        bc = _dot(tri_c, jnp.concatenate(_split3(lf), axis=1))
        b_cols = bc[:, 0:128] + bc[:, 128:256] + bc[:, 256:384]
        br = _dot(jnp.concatenate(_split3(lf_t), axis=0), tri_r)
        b_rows = br[0:16] + br[16:32] + br[32:48]
        return gate, gate_t, b_cols, b_rows, jnp.sum(lf_t, axis=-1, keepdims=True)

    terms = [gate_terms(0), gate_terms(1)]
    hs = ([], [])
    for hd in range(HEADS):
        hsl = slice(hd * dh, (hd + 1) * dh)
        st = None
        for d in range(N_DIR):
            k_ref, qt_ref, vt_ref = (fwd_refs, bwd_refs)[d][0:3]
            gate, gate_t, b_cols, b_rows, tot_rows = terms[d]
            ci = d * 8 + hd
            cf = d * 8 + 4 + hd
            unit = d * HEADS + hd
            k = k_ref[0, :, hsl]
            qt = qt_ref[0, hsl, :]
            if st is None or nc > 1:
                st = _dot(k, qt)
            mask = le if d == 0 else ge
            h, s_new, m_new = _scan_unit(
                st, k, qt, vt_ref[0, hsl, :],
                gate[:, ci:ci + 1] - b_cols[:, cf:cf + 1],
                gate_t[ci:ci + 1, :] - b_rows[cf:cf + 1, :],
                b_rows[cf:cf + 1, :], tot_rows[cf:cf + 1, :],
                mask, s_scr[unit], m_scr[unit][:, 0:1], use_state)
            s_scr[unit] = s_new
            m_scr[unit] = jnp.broadcast_to(m_new, (1, LANES))
            hs[d].append(h)
    for d in range(N_DIR):
        h_refs[d][0] = jnp.concatenate(hs[d], axis=1).astype(BF16)

    if emit_state:
        @pl.when(j == nc - 1)
        def _():
            for unit in range(N_DIR * HEADS):
                s = s_scr[unit]
                c_out[0, unit] = s[:dh].T
                n_out[0, unit] = s[dh:dh + 1]
                m_out[0, unit] = m_scr[unit]


def _mlstm(k, qt, vt, gate, gate_t, s0, m0, emit_state):
    b, t, _ = k.shape
    nc = t // CHUNK
    zero_init = s0 is None
    nu = N_DIR * HEADS
    fwd = lambda w: pl.BlockSpec((1, CHUNK, w), lambda i, j: (i, j, 0))
    bwd = lambda w: pl.BlockSpec((1, CHUNK, w), lambda i, j: (i, nc - 1 - j, 0))
    fwd_t = lambda r: pl.BlockSpec((1, r, CHUNK), lambda i, j: (i, 0, j))
    bwd_t = lambda r: pl.BlockSpec((1, r, CHUNK), lambda i, j: (i, 0, nc - 1 - j))
    args = [k, qt, vt, gate, gate_t]
    in_specs = [fwd(512), fwd_t(512), fwd_t(512), fwd(LANES), fwd_t(16)]
    if nc > 1:
        args += [k, qt, vt, gate, gate_t]
        in_specs += [bwd(512), bwd_t(512), bwd_t(512), bwd(LANES), bwd_t(16)]
    if not zero_init:
        args += [s0, m0]
        in_specs += [pl.BlockSpec((1, nu, HEAD_DIM + N_ROWS, HEAD_DIM), lambda i, j: (i, 0, 0, 0)),
                     pl.BlockSpec((1, nu, 1, LANES), lambda i, j: (i, 0, 0, 0))]
    sd = jax.ShapeDtypeStruct
    out_shape = [sd((b, t, 512), BF16), sd((b, t, 512), BF16)]
    out_specs = [fwd(512), bwd(512)]
    if emit_state:
        out_shape += [sd((b, nu, HEAD_DIM, HEAD_DIM), F32), sd((b, nu, 1, HEAD_DIM), F32),
                      sd((b, nu, 1, LANES), F32)]
        out_specs += [pl.BlockSpec((1, nu, HEAD_DIM, HEAD_DIM), lambda i, j: (i, 0, 0, 0)),
                      pl.BlockSpec((1, nu, 1, HEAD_DIM), lambda i, j: (i, 0, 0, 0)),
                      pl.BlockSpec((1, nu, 1, LANES), lambda i, j: (i, 0, 0, 0))]
    return pl.pallas_call(
        functools.partial(_mlstm_kernel, nc=nc, zero_init=zero_init, emit_state=emit_state),
        grid=(b, nc),
        in_specs=in_specs,
        out_specs=out_specs,
        out_shape=out_shape,
        scratch_shapes=[pltpu.VMEM((nu, HEAD_DIM + N_ROWS, HEAD_DIM), F32),
                        pltpu.VMEM((nu, 1, LANES), F32)],
        compiler_params=_params("arbitrary", "arbitrary"),
        name="mlstm",
    )(*args)


def _outproj_kernel(x_ref, u_ref, hf_ref, hb_ref, o_ref, mod_ref, pm_ref, pinv_ref, wp_ref, ps_ref,
                    wo_ref, gpm_ref, gpf_ref, wr_ref, x1_ref, h2_ref, xp_ref, lg_ref):
    bs, tm, _ = x_ref.shape
    rows = bs * tm
    x = x_ref[...].reshape(rows, D_MODEL)
    mod = mod_ref[0]
    row0 = pl.multiple_of(pl.program_id(1) * tm, tm)
    diffs = []
    for g in range(POOL_GROUPS):
        sl = slice(g * POOL_GROUP_DIM, (g + 1) * POOL_GROUP_DIM)
        per_seq = []
        for bb in range(bs):
            box = _dot(pm_ref[g], u_ref[bb, :, sl])
            per_seq.append(box * pinv_ref[g] - u_ref[bb, pl.ds(row0, tm), sl].astype(F32))
        diffs.append(jnp.concatenate(per_seq, axis=0).astype(BF16))
    yps = [_dot(jnp.concatenate(diffs[2 * p:2 * p + 2], axis=1), wp_ref[p]) for p in range(POOL_GROUPS // 2)]
    y_pool = jnp.concatenate(yps, axis=1) * ps_ref[...]
    seq_rows = lambda ref: ref[...].reshape(rows, ref.shape[-1]).astype(F32)
    y_ml = jax.nn.sigmoid(seq_rows(o_ref)) * (seq_rows(hf_ref) + seq_rows(hb_ref))
    mix = _dot(jnp.concatenate([y_pool, y_ml], axis=1).astype(BF16), wo_ref[...])
    x1 = x + mod[2:3] * _rmsnorm(mix, gpm_ref[...])
    x1_ref[...] = x1.reshape(bs, tm, D_MODEL)
    h2 = _rmsnorm(x1, gpf_ref[...]) * (1.0 + mod[4:5]) + mod[3:4]
    p1, p2, p3 = _split3(h2)
    h2_ref[...] = p1.reshape(bs, tm, D_MODEL)
    nch = D_MODEL // LANES
    for g in range(rows // 8):
        cols = jnp.stack([h2[g * 8:(g + 1) * 8, cc * LANES:(cc + 1) * LANES] for cc in range(nch)], axis=0)
        bb, r0 = divmod(g * 8, tm)
        xp_ref[bb, r0:r0 + 8] = jnp.swapaxes(cols, 0, 1).astype(BF16)
    r = _dot(jnp.concatenate([p1, p2, p3], axis=0), wr_ref[...])
    r12 = r[0:rows] + r[rows:2 * rows]
    ne = N_EXPERTS
    lg = (r12 + r[2 * rows:])[:, 0:ne] + r12[:, ne:2 * ne] + r[0:rows, 2 * ne:3 * ne]
    lg_ref[...] = jnp.concatenate([lg, jnp.zeros_like(lg)], axis=1).T[0:ne]


def _outproj(x, u, hf, hb, o, mod, mod_row, pm, pinv, wp, ps, wo, gpm, gpf, wr):
    b, t, _ = x.shape
    tm = min(t, 2 * TOKEN_TILE)
    bs = 2 * TOKEN_TILE // tm if mod_row(1) == mod_row(0) else 1
    const = lambda *shape: pl.BlockSpec(shape, lambda i, j: (0,) * len(shape))
    tok = lambda w: pl.BlockSpec((bs, tm, w), lambda i, j: (i, j, 0))
    sd = jax.ShapeDtypeStruct
    return pl.pallas_call(
        _outproj_kernel,
        grid=(b // bs, t // tm),
        in_specs=[tok(D_MODEL),
                  pl.BlockSpec((bs, t, 512), lambda i, j: (i, 0, 0)),
                  tok(512), tok(512), tok(512),
                  pl.BlockSpec((1, N_MOD, D_MODEL), lambda i, j: (mod_row(i * bs), 0, 0)),
                  pl.BlockSpec((POOL_GROUPS, tm, t), lambda i, j: (0, j, 0)),
                  pl.BlockSpec((POOL_GROUPS, tm, 1), lambda i, j: (0, j, 0)),
                  const(POOL_GROUPS // 2, 2 * POOL_GROUP_DIM, 2 * POOL_GROUP_DIM), const(1, POOL_WIDTH),
                  const(D_MODEL, D_MODEL), const(1, D_MODEL), const(1, D_MODEL),
                  const(D_MODEL, 2 * LANES)],
        out_specs=[tok(D_MODEL), tok(D_MODEL),
                   pl.BlockSpec((bs, tm, D_MODEL // LANES, LANES), lambda i, j: (i, j, 0, 0)),
                   pl.BlockSpec((N_EXPERTS, bs * tm), lambda i, j: (0, i * (t // tm) + j))],
        out_shape=[sd((b, t, D_MODEL), F32), sd((b, t, D_MODEL), BF16),
                   sd((b, t, D_MODEL // LANES, LANES), BF16), sd((N_EXPERTS, b * t), F32)],
        compiler_params=_params("arbitrary", "arbitrary"),
        name="outproj",
    )(x, u, hf, hb, o, mod, pm, pinv, wp, ps, wo, gpm, gpf, wr)


def _router_kernel(lg_ref, br_ref, comb_ref, sel_ref, *, n_blocks):
    ng, gs = N_EXPERT_GROUPS, GROUP_SIZE
    neg = -jnp.inf
    lg = jnp.swapaxes(lg_ref[...], 0, 1)
    br = br_ref[...]
    s = [jax.nn.sigmoid(lg[j]) for j in range(gs)]
    biased = [s[j] + br[:, j, :] for j in range(gs)]
    fold = lambda op, xs: functools.reduce(op, xs)
    m1 = fold(jnp.maximum, biased)
    i1 = fold(jnp.minimum, [jnp.where(biased[j] == m1, j, gs) for j in range(gs)])
    m2 = fold(jnp.maximum, [jnp.where(i1 == j, neg, biased[j]) for j in range(gs)])
    cur = m1 + m2
    gi = lax.broadcasted_iota(jnp.int32, cur.shape, 0)
    gmask = jnp.zeros(cur.shape, F32)
    for _ in range(TOPK_GROUPS):
        mx = jnp.max(cur, axis=0, keepdims=True)
        ix = jnp.min(jnp.where(cur == mx, gi, ng), axis=0, keepdims=True)
        hit = gi == ix
        gmask = jnp.where(hit, 1.0, gmask)
        cur = jnp.where(hit, neg, cur)
    cand = [jnp.where(gmask > 0, biased[j], neg) for j in range(gs)]
    eidx = [gi * gs + j for j in range(gs)]
    selm = [jnp.zeros(cur.shape, F32) for _ in range(gs)]
    for _ in range(TOP_K):
        mx = jnp.max(fold(jnp.maximum, cand), axis=0, keepdims=True)
        ix = jnp.min(fold(jnp.minimum, [jnp.where(cand[j] == mx, eidx[j], N_EXPERTS) for j in range(gs)]),
                     axis=0, keepdims=True)
        for j in range(gs):
            hit = eidx[j] == ix
            selm[j] = jnp.where(hit, 1.0, selm[j])
            cand[j] = jnp.where(hit, neg, cand[j])
    sel = [selm[j] * s[j] for j in range(gs)]
    tot = jnp.sum(fold(jnp.add, sel), axis=0, keepdims=True)
    comb = [sel[j] / tot * ROUTED_SCALE for j in range(gs)]
    sel_ref[...] = jnp.swapaxes(jnp.stack(selm, axis=0), 0, 1)
    comb_e = jnp.swapaxes(jnp.stack(comb, axis=0), 0, 1).reshape(N_EXPERTS, -1)
    comb_t = jnp.concatenate([comb_e, jnp.zeros_like(comb_e)], axis=0).T
    comb_ref[...] = jnp.where(pl.program_id(0) < n_blocks, comb_t, 0.0)


def _router(logits_t, b_router):
    t = logits_t.shape[1]
    tl = 1024
    nb = t // tl
    shp = (N_EXPERT_GROUPS, GROUP_SIZE, t)
    blk = pl.BlockSpec((N_EXPERT_GROUPS, GROUP_SIZE, tl), lambda j: (0, 0, jnp.minimum(j, nb - 1)))
    comb, sel = pl.pallas_call(
        functools.partial(_router_kernel, n_blocks=nb),
        grid=(nb + 1,),
        in_specs=[blk, pl.BlockSpec((N_EXPERT_GROUPS, GROUP_SIZE, 1), lambda j: (0, 0, 0))],
        out_specs=[pl.BlockSpec((tl, LANES), lambda j: (j, 0)), blk],
        out_shape=[jax.ShapeDtypeStruct((t + tl, LANES), F32), jax.ShapeDtypeStruct(shp, F32)],
        compiler_params=_params("arbitrary"),
        name="router",
    )(logits_t.reshape(shp), b_router.reshape(N_EXPERT_GROUPS, GROUP_SIZE, 1))
    return comb, sel.reshape(N_EXPERTS, t)


def _plan_kernel(sel_ref, pos_ref, meta_ref, *, n_meta):
    t = sel_ref.shape[1]
    tm = float(MOE_TILE)
    sel = sel_ref[...]
    selb = sel.astype(BF16)
    blk = 256
    rr = lax.broadcasted_iota(jnp.int32, (blk, blk), 0)
    cc = lax.broadcasted_iota(jnp.int32, (blk, blk), 1)
    before = (rr < cc).astype(BF16)
    carry = jnp.zeros((N_EXPERTS, 1), F32)
    ranks = []
    for b in range(t // blk):
        sb = selb[:, b * blk:(b + 1) * blk]
        ranks.append(_dot(sb, before) + carry)
        carry = carry + jnp.sum(sel[:, b * blk:(b + 1) * blk], axis=1, keepdims=True)
    rank = jnp.concatenate(ranks, axis=1)
    cnt = carry
    ntile = jnp.floor((cnt + (tm - 0.5)) * (1.0 / tm))
    er = lax.broadcasted_iota(jnp.int32, (N_EXPERTS, N_EXPERTS), 0)
    ec = lax.broadcasted_iota(jnp.int32, (N_EXPERTS, N_EXPERTS), 1)
    below = (ec < er).astype(BF16)
    tstart = _dot(below, jnp.broadcast_to(ntile, (N_EXPERTS, LANES)).astype(BF16))[:, 0:1]
    pos = tstart * tm + rank
    erank = _dot(below, selb)
    rows = []
    for k in range(TOP_K):
        hit = (sel > 0.0) & (erank == float(k))
        rows.append(jnp.sum(jnp.where(hit, pos, 0.0), axis=0, keepdims=True))
    rows += [jnp.zeros((1, t), F32)] * (8 - TOP_K)
    pos_ref[...] = jnp.concatenate(rows, axis=0).astype(jnp.int32)

    tau = lax.broadcasted_iota(jnp.int32, (N_EXPERTS, n_meta), 1).astype(F32)
    eidx = lax.broadcasted_iota(jnp.int32, (N_EXPERTS, n_meta), 0).astype(F32)
    te = jnp.sum(((tstart + ntile) <= tau).astype(F32), axis=0, keepdims=True)
    te = jnp.minimum(te, float(N_EXPERTS - 1))
    onehot = eidx == te
    cnt_t = jnp.sum(jnp.where(onehot, cnt, 0.0), axis=0, keepdims=True)
    ts_t = jnp.sum(jnp.where(onehot, tstart, 0.0), axis=0, keepdims=True)
    tr = jnp.clip(cnt_t - (tau[0:1] - ts_t) * tm, 0.0, tm)
    tf = jnp.where((tau[0:1] == ts_t) & (tr > 0.0), 1.0, 0.0)
    owns = jnp.broadcast_to((cnt > 0.0).astype(F32), (N_EXPERTS, LANES)).astype(BF16)
    eord = _dot(below, owns)[:, 0:1]
    ord_t = jnp.sum(jnp.where(onehot, eord, 0.0), axis=0, keepdims=True)
    meta_ref[...] = jnp.concatenate([te, tr, tf, ord_t] + [jnp.zeros((1, n_meta), F32)] * 4, axis=0).astype(jnp.int32)


def _plan(sel, n_meta):
    t = sel.shape[1]
    sd = jax.ShapeDtypeStruct
    return pl.pallas_call(
        functools.partial(_plan_kernel, n_meta=n_meta),
        out_shape=[sd((8, t), jnp.int32), sd((8, n_meta), jnp.int32)],
        compiler_params=pltpu.CompilerParams(vmem_limit_bytes=VMEM_LIMIT),
        name="plan",
    )(sel)


def _slot_table(pos, n_slots, n_tokens):
    t = pos.shape[1]
    workers = SC_CORES * SC_SUBCORES
    per = -(-n_slots // (workers * SC_LANES)) * SC_LANES
    mesh = plsc.VectorSubcoreMesh(core_axis_name="core", subcore_axis_name="subcore")

    def body(pos_hbm, out_hbm, pos_v, tbl_v):
        lo = (lax.axis_index("core") * SC_SUBCORES + lax.axis_index("subcore")) * per
        pltpu.sync_copy(pos_hbm.at[pl.ds(0, TOP_K * t)], pos_v)
        dummy = jnp.full((SC_LANES,), n_tokens, jnp.int32)

        @plsc.parallel_loop(0, per, SC_LANES, unroll=8)
        def _(i):
            tbl_v[pl.ds(i, SC_LANES)] = dummy

        lane = lax.iota(jnp.int32, SC_LANES)
        for k in range(TOP_K):
            @plsc.parallel_loop(0, t, SC_LANES, unroll=8)
            def _(i):
                p = pos_v[pl.ds(k * t + i, SC_LANES)] - lo
                mine = (p >= 0) & (p < per)
                plsc.store_scatter(tbl_v, [jnp.where(mine, p, 0)], lane + i, mask=mine)
        pltpu.sync_copy(tbl_v, out_hbm.at[pl.ds(lo, per)])

    build = pl.kernel(body, out_type=jax.ShapeDtypeStruct((workers * per,), jnp.int32), mesh=mesh,
                      scratch_types=[pltpu.VMEM((TOP_K * t,), jnp.int32), pltpu.VMEM((per,), jnp.int32)],
                      compiler_params=pltpu.CompilerParams(needs_layout_passes=False), name="slot_table")
    return build(pos.reshape(-1))


def _experts_kernel(te_ref, tr_ref, tf_ref, ord_ref, xc_ref, xs_ref, slots_ref, comb_ref, wgu_hbm, wd_hbm, acc_out,
                    tbl_ref, xbuf, acc, stage0, stage1, cst0, cst1, act0, act1, ybuf0, ybuf1, wgu_buf, wd_buf,
                    sems, wsems):
    s = pl.program_id(0)
    n_tiles = te_ref.shape[0]
    tm = MOE_TILE
    nch = D_MODEL // LANES
    tile_at = lambda lag: jnp.clip(s - lag, 0, n_tiles - 1)
    t_g, t_1, t_2, t_3 = tile_at(0), tile_at(1), tile_at(2), tile_at(3)
    ring_gu, ring_d = wgu_buf.shape[0], wd_buf.shape[0]

    def wgu_copy(tile):
        slot = ord_ref[tile] % ring_gu
        return pltpu.make_async_copy(wgu_hbm.at[te_ref[tile]], wgu_buf.at[slot], wsems.at[0, slot])

    def wd_copy(tile):
        slot = ord_ref[tile] % ring_d
        return pltpu.make_async_copy(wd_hbm.at[te_ref[tile]], wd_buf.at[slot], wsems.at[1, slot])

    @pl.when(s == 0)
    def _():
        wgu_copy(0).start()
        wd_copy(0).start()

        @pl.when(tf_ref[1] == 1)
        def _():
            wgu_copy(1).start()
        tc, ts = xc_ref.shape[0], xs_ref.shape[0]
        copies = (pltpu.make_async_copy(xc_ref, xbuf.at[pl.ds(0, tc)], sems.at[0]),
                  pltpu.make_async_copy(xs_ref, xbuf.at[pl.ds(tc, ts)], sems.at[1]))
        tbl_init = pltpu.make_async_copy(slots_ref, tbl_ref, sems.at[3])
        tbl_init.start()
        for cp in copies:
            cp.start()
        n_pad = xbuf.shape[0] - tc - ts
        xbuf[pl.ds(tc + ts, n_pad)] = jnp.zeros((n_pad,) + xbuf.shape[1:], BF16)
        for ref in (acc, stage0, stage1, cst0, cst1, act0, act1, ybuf0, ybuf1):
            ref[...] = jnp.zeros_like(ref)
        tbl_init.wait()
        for cp in copies:
            cp.wait()
        wgu_copy(0).wait()
        wd_copy(0).wait()

    nxt = jnp.minimum(s + 2, n_tiles - 1)

    @pl.when((s + 2 < n_tiles) & (tf_ref[nxt] == 1))
    def _():
        wgu_copy(nxt).start()

    @pl.when((s >= 1) & (s < n_tiles) & (tf_ref[t_g] == 1))
    def _():
        wd_copy(t_g).start()

    @pl.when((s >= 2) & (s - 1 < n_tiles) & (tf_ref[t_1] == 1))
    def _():
        wgu_copy(t_1).wait()

    @pl.when((s >= 3) & (s - 2 < n_tiles) & (tf_ref[t_2] == 1))
    def _():
        wd_copy(t_2).wait()

    wgu_now = wgu_buf.at[ord_ref[t_1] % ring_gu]
    wd_now = wd_buf.at[ord_ref[t_2] % ring_d]

    def gather(tile, stage, cst):
        base = tile * tm
        for j in range(tm):
            tok = tbl_ref[base + j]
            stage[pl.ds(j * X_PITCH, nch), :] = xbuf[tok].astype(F32)
            cst[pl.ds(j, 1), :] = comb_ref[pl.ds(tok, 1), :]

    def gate_up(tile, stage, cst, act):
        xb = jnp.concatenate(
            [jnp.concatenate([stage[pl.ds(g * 8 * X_PITCH + cc, 8, stride=X_PITCH), :] for g in range(tm // 8)], axis=0)
             for cc in range(nch)], axis=1).astype(BF16)
        gu = _dot(xb, wgu_now[...].astype(BF16))
        lane = lax.broadcasted_iota(jnp.int32, (1, LANES), 1)
        w_col = jnp.sum(jnp.where(lane == te_ref[tile], cst[...], 0.0), axis=1, keepdims=True)
        act[...] = (_silu(gu[:, :EXPERT_DIM]) * gu[:, EXPERT_DIM:] * w_col).astype(BF16)

    def down(act, ybuf):
        y = _dot(act[...], wd_now[...].astype(BF16))
        for cc in range(nch):
            ybuf[cc * Y_PITCH:cc * Y_PITCH + tm, :] = y[:, cc * LANES:(cc + 1) * LANES]

    def scatter(tile, ybuf):
        base = tile * tm
        sc_n = 16
        for i in range(tm // sc_n):
            toks = [tbl_ref[base + i * sc_n + u] for u in range(sc_n)]
            olds = [acc[toks[u]] for u in range(sc_n)]
            news = [olds[u] + ybuf[pl.ds(i * sc_n + u, nch, stride=Y_PITCH), :] for u in range(sc_n)]
            for u in range(sc_n):
                acc[toks[u]] = news[u]

    busy = (tr_ref[t_g] + tr_ref[t_1] + tr_ref[t_2] + tr_ref[t_3]) > 0
    bufs = ((stage0, cst0, act0, ybuf0), (stage1, cst1, act1, ybuf1))
    for par in range(2):
        stage_p, cst_p, act_p, ybuf_p = bufs[par]
        stage_q, cst_q, act_q, ybuf_q = bufs[1 - par]

        @pl.when(busy & (s % 2 == par))
        def _():
            gather(t_g, stage_p, cst_p)
            gate_up(t_1, stage_q, cst_q, act_q)
            down(act_p, ybuf_p)
            scatter(t_3, ybuf_q)

    @pl.when(s == pl.num_programs(0) - 1)
    def _():
        cp = pltpu.make_async_copy(acc, acc_out, sems.at[2])
        cp.start()
        cp.wait()


def _experts(te, tr, tf, eord, xc, xs, slots, comb, wgu, wd):
    n_tok = xc.shape[0] + xs.shape[0] + 8
    n_tiles = te.shape[0]
    tm = MOE_TILE
    nch = D_MODEL // LANES
    vm = pltpu.VMEM
    grid_spec = pltpu.PrefetchScalarGridSpec(
        num_scalar_prefetch=4,
        grid=(n_tiles + 3,),
        in_specs=[pl.BlockSpec(memory_space=pl.ANY), pl.BlockSpec(memory_space=pl.ANY),
                  pl.BlockSpec(memory_space=pl.ANY),
                  pl.BlockSpec((n_tok, LANES), lambda s, *_: (0, 0), pipeline_mode=pl.Buffered(1)),
                  pl.BlockSpec(memory_space=pl.ANY), pl.BlockSpec(memory_space=pl.ANY)],
        out_specs=pl.BlockSpec(memory_space=pl.ANY),
        scratch_shapes=[pltpu.SMEM(slots.shape, jnp.int32),
                        vm((n_tok, nch, LANES), BF16), vm((n_tok, nch, LANES), F32),
                        vm((tm * X_PITCH, LANES), F32), vm((tm * X_PITCH, LANES), F32),
                        vm((tm, LANES), F32), vm((tm, LANES), F32),
                        vm((tm, EXPERT_DIM), BF16), vm((tm, EXPERT_DIM), BF16),
                        vm((nch * Y_PITCH, LANES), F32), vm((nch * Y_PITCH, LANES), F32),
                        vm((WEIGHT_RING + 1, D_MODEL, 2 * EXPERT_DIM), F32), vm((WEIGHT_RING, EXPERT_DIM, D_MODEL), F32),
                        pltpu.SemaphoreType.DMA((4,)), pltpu.SemaphoreType.DMA((2, WEIGHT_RING + 1))],
    )
    return pl.pallas_call(
        _experts_kernel,
        grid_spec=grid_spec,
        out_shape=jax.ShapeDtypeStruct((n_tok, nch, LANES), F32),
        compiler_params=pltpu.CompilerParams(dimension_semantics=("arbitrary",),
                                             vmem_limit_bytes=EXPERTS_VMEM_LIMIT),
        name="experts",
    )(te, tr, tf, eord, xc, xs, slots, comb, wgu, wd)


def _final_kernel(acc_ref, h_ref, x1_ref, mod_ref, wsg_ref, wsd_ref, gpo_ref, out_ref):
    routed = _token_rows_from_slabs(lambda g: acc_ref[g * 8:(g + 1) * 8], acc_ref.shape[0])
    gs = _dot(h_ref[...], wsg_ref[...])
    act = _silu(gs[:, :SHARED_DIM]) * gs[:, SHARED_DIM:]
    f = routed + _dot(act.astype(BF16), wsd_ref[...])
    out_ref[...] = x1_ref[...] + mod_ref[0][5:6] * _rmsnorm(f, gpo_ref[...])


def _final(acc, tile0, h2, x1, mod, mod_row, wsg, wsd, gpo):
    n = h2.shape[0]
    tm = FINAL_TILE
    const = lambda *shape: pl.BlockSpec(shape, lambda i: (0,) * len(shape))
    tok = lambda w: pl.BlockSpec((tm, w), lambda i: (i, 0))
    return pl.pallas_call(
        _final_kernel,
        grid=(n // tm,),
        in_specs=[pl.BlockSpec((tm, D_MODEL // LANES, LANES), lambda i: (i + tile0, 0, 0)),
                  tok(D_MODEL), tok(D_MODEL),
                  pl.BlockSpec((1, N_MOD, D_MODEL), lambda i: (mod_row(i), 0, 0)),
                  const(D_MODEL, 2 * SHARED_DIM), const(SHARED_DIM, D_MODEL), const(1, D_MODEL)],
        out_specs=tok(D_MODEL),
        out_shape=jax.ShapeDtypeStruct((n, D_MODEL), F32),
        compiler_params=_params("arbitrary"),
        name="final",
    )(acc, h2, x1, mod, wsg, wsd, gpo)


def _window_bounds(n, w):
    idx = np.arange(n)
    return np.clip(idx - w // 2, 0, n), np.clip(idx + w - w // 2, 0, n)


def _pool_operators(t, grid):
    mats, invs = [], []
    for w in POOL_WINDOWS:
        if grid:
            rlo, rhi = _window_bounds(t // GRID_W, w)
            clo, chi = _window_bounds(GRID_W, w)
            r = np.arange(t) // GRID_W
            c = np.arange(t) % GRID_W
            m = ((r[None, :] >= rlo[r][:, None]) & (r[None, :] < rhi[r][:, None])
                 & (c[None, :] >= clo[c][:, None]) & (c[None, :] < chi[c][:, None]))
            cnt = (rhi - rlo)[r] * (chi - clo)[c]
        else:
            lo, hi = _window_bounds(t, w)
            sidx = np.arange(t)
            m = (sidx[None, :] >= lo[:, None]) & (sidx[None, :] < hi[:, None])
            cnt = hi - lo
        mats.append(m.astype(np.float32))
        invs.append((1.0 / cnt.astype(np.float64)).astype(np.float32)[:, None])
    return jnp.asarray(np.stack(mats), BF16), jnp.asarray(np.stack(invs), F32)


def kernel(x_prompt, x_sample, state_C, state_n, state_m, c, c_ctx, w_ada, b_ada, g_pre_mix, w_in, b_gate,
           w_pool, pool_scale, w_out, g_post_mix, g_pre_ffn, w_router, b_router, w_expert_gu, w_expert_down,
           w_shared_gu, w_shared_down, g_post_ffn):
    b_ctx = x_prompt.shape[0]
    b_lat = x_sample.shape[0]
    nu = N_DIR * HEADS
    l = 0
    row = lambda a: a[l].reshape(1, -1).astype(F32)

    cvec = jnp.zeros((16, D_MODEL), F32).at[0].set(c_ctx.astype(F32)).at[1:1 + b_lat].set(c.astype(F32))
    mod = _mod_rows(cvec, w_ada[l], b_ada[l]).reshape(16, N_MOD, D_MODEL)

    w_in_l = w_in[l]
    p0 = POOL_WIDTH
    mw = MLSTM_WIDTH
    w_u, w_q, w_k, w_v, w_o = (w_in_l[:, lo:lo + 512] for lo in (0, p0, p0 + mw, p0 + 2 * mw, p0 + 3 * mw))
    wm = jnp.concatenate([w_u, w_k, w_o], axis=1).astype(BF16)
    wt = jnp.concatenate([w_q.T, w_v.T], axis=0).astype(BF16)
    wg_cols = w_in_l[:, p0 + 4 * mw:]
    wg = jnp.pad(jnp.concatenate(_split2(wg_cols), axis=1), ((0, 0), (0, LANES - 2 * GATE_COLS)))
    bg = b_gate[l].reshape(GATE_COLS).astype(F32)
    bgr = jnp.pad(bg, (0, LANES - GATE_COLS)).reshape(1, LANES)
    wpl = w_pool[l].astype(BF16)
    zg = jnp.zeros((POOL_GROUP_DIM, POOL_GROUP_DIM), BF16)
    wp = jnp.stack([jnp.block([[wpl[2 * p], zg], [zg, wpl[2 * p + 1]]]) for p in range(POOL_GROUPS // 2)])
    wo = w_out[l].astype(BF16)
    wr = jnp.pad(jnp.concatenate(_split3(w_router[l].astype(F32)), axis=1), ((0, 0), (0, 2 * LANES - 3 * N_EXPERTS)))
    wsg = w_shared_gu[l].astype(BF16)
    wsd = w_shared_down[l].astype(BF16)

    def mixer(x, mod_row, grid, s0, m0, emit_state):
        t = x.shape[1]
        u, k, o, qt, vt, gate, gate_t = _inproj(x.astype(F32), mod, mod_row, row(g_pre_mix), wm, wt, wg, bgr)
        outs = _mlstm(k, qt, vt, gate, gate_t, s0, m0, emit_state)
        hf, hb = outs[0], outs[1]
        pm, pinv = _pool_operators(t, grid)
        x1, h2, xp, lg = _outproj(x.astype(F32), u, hf, hb, o, mod, mod_row, pm, pinv, wp, row(pool_scale), wo,
                                  row(g_post_mix), row(g_pre_ffn), wr)
        return x1, h2, xp, lg, outs[2:]

    ctx_row = lambda i: 0
    lat_row = lambda i: i + 1
    x1c, h2c, xpc, lgc, (c_new, n_new, m_new) = mixer(x_prompt, ctx_row, False, None, None, True)
    s0 = jnp.concatenate(
        [jnp.swapaxes(state_C[:, l].reshape(b_lat, nu, HEAD_DIM, HEAD_DIM).astype(F32), -1, -2),
         jnp.broadcast_to(state_n[:, l].reshape(b_lat, nu, 1, HEAD_DIM).astype(F32),
                          (b_lat, nu, N_ROWS, HEAD_DIM))], axis=-2)
    m0 = jnp.broadcast_to(state_m[:, l].reshape(b_lat, nu, 1, 1).astype(F32), (b_lat, nu, 1, LANES))
    x1s, h2s, xps, lgs, _ = mixer(x_sample, lat_row, True, s0, m0, False)

    tc = b_ctx * x_prompt.shape[1]
    ts = b_lat * x_sample.shape[1]
    n_tok = tc + ts
    lg_all = jnp.concatenate([lgc, lgs], axis=1)
    comb_tok, sel = _router(lg_all, b_router[l].astype(F32))
    n_tiles = n_tok * TOP_K // MOE_TILE + N_EXPERTS
    n_meta = -(-n_tiles // LANES) * LANES
    pos, meta = _plan(sel, n_meta)

    slab = (D_MODEL // LANES, LANES)
    acc = _experts(meta[0, :n_tiles], meta[1, :n_tiles], meta[2, :n_tiles], meta[3, :n_tiles],
                   xpc.reshape((tc,) + slab), xps.reshape((ts,) + slab),
                   _slot_table(pos, n_tiles * MOE_TILE, n_tok), comb_tok,
                   w_expert_gu[l], w_expert_down[l])

    fin = functools.partial(_final, wsg=wsg, wsd=wsd, gpo=row(g_post_ffn))
    tiles_per_lat = x_sample.shape[1] // FINAL_TILE
    yc = fin(acc, 0, h2c.reshape(tc, D_MODEL), x1c.reshape(tc, D_MODEL), mod, ctx_row)
    ys = fin(acc, tc // FINAL_TILE, h2s.reshape(ts, D_MODEL), x1s.reshape(ts, D_MODEL), mod,
             lambda i: i // tiles_per_lat + 1)

    new_c = c_new.reshape(b_ctx, 1, N_DIR, HEADS, HEAD_DIM, HEAD_DIM)
    new_n = n_new.reshape(b_ctx, 1, N_DIR, HEADS, HEAD_DIM)
    new_m = m_new[..., 0].reshape(b_ctx, 1, N_DIR, HEADS)
    return (yc.reshape(x_prompt.shape), ys.reshape(x_sample.shape), new_c, new_n, new_m)
```
